```python
import jax, jax.numpy as jnp
from jax import lax
import numpy as np

D_MODEL = 2048
BATCH = 8
SEQ = 4096
DEPTH = 2

HEAD_DIM = 128
N_SB_HEADS = D_MODEL // (2 * HEAD_DIM)
N_FOX_HEADS = D_MODEL // (2 * HEAD_DIM)
SB_WIDTH = N_SB_HEADS * HEAD_DIM
FOX_WIDTH = N_FOX_HEADS * HEAD_DIM
IN_WIDTH = 3 * SB_WIDTH + 3 * FOX_WIDTH + N_FOX_HEADS

N_MLA_HEADS = 16
Q_LORA_RANK = 512
KV_LORA_RANK = 512
QK_NOPE_DIM = 128
QK_ROPE_DIM = 64
QK_HEAD_DIM = QK_NOPE_DIM + QK_ROPE_DIM
V_HEAD_DIM = 128
DOWN_WIDTH = Q_LORA_RANK + KV_LORA_RANK + QK_ROPE_DIM

D_FF = 4 * D_MODEL
BLOCK_Q = 128
ROPE_THETA = 10000.0
EPS = 1e-6
FORGET_BIAS_CENTER = 3.0

N_EVEN = (DEPTH + 1) // 2
N_ODD = DEPTH // 2

kernel_name = "hybrid_stickbreak_fox_mla_sqrelu"


def rmsnorm(x, g):
    x32 = x.astype(jnp.float32)
    y = x32 * lax.rsqrt(jnp.mean(x32 * x32, axis=-1, keepdims=True) + EPS)
    return (y * g.astype(jnp.float32)).astype(x.dtype)


def to_heads(t, n_heads):
    b, s, _ = t.shape
    return t.reshape(b, s, n_heads, -1).transpose(0, 2, 1, 3)


def query_blocks(q):
    b, h, s, d = q.shape
    return q.reshape(b, h, s // BLOCK_Q, BLOCK_Q, d).transpose(2, 0, 1, 3, 4)


def merge_blocks(o):
    nb, b, h, bq, d = o.shape
    return o.transpose(1, 0, 3, 2, 4).reshape(b, nb * bq, h * d)


def stick_breaking_attention(q, k, v):
    s = q.shape[2]
    scale = q.shape[-1] ** -0.5
    key_pos = jnp.arange(s)

    def one_block(args):
        qb, q0 = args
        z = jnp.einsum('bhqd,bhkd->bhqk', qb, k,
                       preferred_element_type=jnp.float32) * scale
        q_pos = q0 + jnp.arange(BLOCK_Q)
        strict = key_pos[None, :] < q_pos[:, None]
        log_beta = jax.nn.log_sigmoid(z)
        log_one_minus = jnp.where(strict, jax.nn.log_sigmoid(-z), 0.0)
        suffix = lax.cumsum(log_one_minus, axis=3, reverse=True) - log_one_minus
        w = jnp.where(strict, jnp.exp(log_beta + suffix), 0.0)
        return jnp.einsum('bhqk,bhkd->bhqd', w.astype(v.dtype), v,
                          preferred_element_type=jnp.float32).astype(v.dtype)

    starts = jnp.arange(s // BLOCK_Q, dtype=jnp.int32) * BLOCK_Q
    return merge_blocks(lax.map(one_block, (query_blocks(q), starts)))


def causal_softmax_attention(q, k, v, log_decay_cum=None):
    s = q.shape[2]
    scale = q.shape[-1] ** -0.5
    key_pos = jnp.arange(s)

    def one_block(args):
        qb, q0 = args
        logits = jnp.einsum('bhqd,bhkd->bhqk', qb, k,
                            preferred_element_type=jnp.float32) * scale
        if log_decay_cum is not None:
            f_q = lax.dynamic_slice_in_dim(log_decay_cum, q0, BLOCK_Q, axis=2)
            logits = logits + f_q[..., :, None] - log_decay_cum[..., None, :]
        q_pos = q0 + jnp.arange(BLOCK_Q)
        causal = key_pos[None, :] <= q_pos[:, None]
        p = jax.nn.softmax(jnp.where(causal, logits, -jnp.inf), axis=-1)
        return jnp.einsum('bhqk,bhkd->bhqd', p.astype(v.dtype), v,
                          preferred_element_type=jnp.float32).astype(v.dtype)

    starts = jnp.arange(s // BLOCK_Q, dtype=jnp.int32) * BLOCK_Q
    return merge_blocks(lax.map(one_block, (query_blocks(q), starts)))


def apply_rope(t, positions):
    half = t.shape[-1] // 2
    inv_freq = ROPE_THETA ** (-jnp.arange(half, dtype=jnp.float32) / half)
    ang = positions.astype(jnp.float32)[:, None, :, None] * inv_freq
    cos, sin = jnp.cos(ang), jnp.sin(ang)
    t32 = t.astype(jnp.float32)
    t1, t2 = t32[..., :half], t32[..., half:]
    return jnp.concatenate([t1 * cos - t2 * sin, t2 * cos + t1 * sin], axis=-1).astype(t.dtype)


def sb_fox_mixer(h, w_in, b_f, fox_q_g, fox_k_g, w_o):
    proj = h @ w_in
    cuts = np.cumsum([SB_WIDTH] * 3 + [FOX_WIDTH] * 3)
    q_sb, k_sb, v_sb, q_fx, k_fx, v_fx, f_logit = jnp.split(proj, cuts, axis=-1)
    o_sb = stick_breaking_attention(to_heads(q_sb, N_SB_HEADS),
                                    to_heads(k_sb, N_SB_HEADS),
                                    to_heads(v_sb, N_SB_HEADS))
    log_f = jax.nn.log_sigmoid(f_logit.astype(jnp.float32) + b_f.astype(jnp.float32))
    log_f_cum = jnp.cumsum(log_f, axis=1).transpose(0, 2, 1)
    q_f = rmsnorm(to_heads(q_fx, N_FOX_HEADS), fox_q_g)
    k_f = rmsnorm(to_heads(k_fx, N_FOX_HEADS), fox_k_g)
    o_fx = causal_softmax_attention(q_f, k_f, to_heads(v_fx, N_FOX_HEADS), log_f_cum)
    return jnp.concatenate([o_sb, o_fx], axis=-1) @ w_o


def mla_mixer(h, positions, w_down, q_a_g, kv_a_g, w_uq, w_ukv, q_g, k_g, w_o):
    b, s, _ = h.shape
    down = h @ w_down
    c_q, c_kv, k_pe = jnp.split(down, [Q_LORA_RANK, Q_LORA_RANK + KV_LORA_RANK], axis=-1)
    c_q = rmsnorm(c_q, q_a_g)
    c_kv = rmsnorm(c_kv, kv_a_g)
    q = to_heads(c_q @ w_uq, N_MLA_HEADS)
    kv = to_heads(c_kv @ w_ukv, N_MLA_HEADS)
    k_nope, v = kv[..., :QK_NOPE_DIM], kv[..., QK_NOPE_DIM:]
    k_pe = jnp.broadcast_to(k_pe[:, None], (b, N_MLA_HEADS, s, QK_ROPE_DIM))
    k = jnp.concatenate([k_nope, k_pe], axis=-1)
    q = rmsnorm(q, q_g)
    k = rmsnorm(k, k_g)
    q = jnp.concatenate([q[..., :QK_NOPE_DIM], apply_rope(q[..., QK_NOPE_DIM:], positions)], axis=-1)
    k = jnp.concatenate([k[..., :QK_NOPE_DIM], apply_rope(k[..., QK_NOPE_DIM:], positions)], axis=-1)
    o = causal_softmax_attention(q, k, v)
    return o @ w_o


def squared_relu_mlp(h, w_up, w_down):
    a = jnp.square(jax.nn.relu(h @ w_up))
    return a @ w_down


def _fwd_setup_inputs(seed: int = 0) -> dict:
    key = jax.random.key(seed)
    ks = jax.random.split(key, 24)

    def dense(k, shape, fan_in):
        return jax.random.normal(k, shape, jnp.float32) * (fan_in ** -0.5)

    def gain(k, shape):
        return 1.0 + 0.02 * jax.random.normal(k, shape, jnp.float32)

    x = jax.random.normal(ks[0], (BATCH, SEQ, D_MODEL), jnp.float32)
    offsets = jax.random.randint(ks[1], (BATCH, 1), 0, 2048, dtype=jnp.int32)
    positions = offsets + jnp.arange(SEQ, dtype=jnp.int32)[None, :]
    return {
        "x": x,
        "positions": positions,
        "ln_mix_g": gain(ks[2], (DEPTH, D_MODEL)),
        "ln_mlp_g": gain(ks[3], (DEPTH, D_MODEL)),
        "sf_w_in": dense(ks[4], (N_EVEN, D_MODEL, IN_WIDTH), D_MODEL),
        "sf_b_f": FORGET_BIAS_CENTER + 0.1 * jax.random.normal(ks[5], (N_EVEN, N_FOX_HEADS), jnp.float32),
        "fox_q_g": gain(ks[6], (N_EVEN, HEAD_DIM)),
        "fox_k_g": gain(ks[7], (N_EVEN, HEAD_DIM)),
        "sf_w_o": dense(ks[8], (N_EVEN, SB_WIDTH + FOX_WIDTH, D_MODEL), SB_WIDTH + FOX_WIDTH),
        "mla_w_down": dense(ks[9], (N_ODD, D_MODEL, DOWN_WIDTH), D_MODEL),
        "mla_q_a_g": gain(ks[10], (N_ODD, Q_LORA_RANK)),
        "mla_kv_a_g": gain(ks[11], (N_ODD, KV_LORA_RANK)),
        "mla_w_uq": dense(ks[12], (N_ODD, Q_LORA_RANK, N_MLA_HEADS * QK_HEAD_DIM), Q_LORA_RANK),
        "mla_w_ukv": dense(ks[13], (N_ODD, KV_LORA_RANK, N_MLA_HEADS * (QK_NOPE_DIM + V_HEAD_DIM)), KV_LORA_RANK),
        "mla_q_g": gain(ks[14], (N_ODD, QK_HEAD_DIM)),
        "mla_k_g": gain(ks[15], (N_ODD, QK_HEAD_DIM)),
        "mla_w_o": dense(ks[16], (N_ODD, N_MLA_HEADS * V_HEAD_DIM, D_MODEL), N_MLA_HEADS * V_HEAD_DIM),
        "mlp_w_up": dense(ks[17], (DEPTH, D_MODEL, D_FF), D_MODEL),
        "mlp_w_down": dense(ks[18], (DEPTH, D_FF, D_MODEL), D_FF),
    }


def _fwd_reference(x, positions, ln_mix_g, ln_mlp_g, sf_w_in, sf_b_f, fox_q_g, fox_k_g, sf_w_o,
              mla_w_down, mla_q_a_g, mla_kv_a_g, mla_w_uq, mla_w_ukv, mla_q_g, mla_k_g,
              mla_w_o, mlp_w_up, mlp_w_down):
    for layer in range(DEPTH):
        i = layer // 2
        h = rmsnorm(x, ln_mix_g[layer])
        if layer % 2 == 0:
            x = x + sb_fox_mixer(h, sf_w_in[i], sf_b_f[i], fox_q_g[i], fox_k_g[i], sf_w_o[i])
        else:
            x = x + mla_mixer(h, positions, mla_w_down[i], mla_q_a_g[i], mla_kv_a_g[i],
                              mla_w_uq[i], mla_w_ukv[i], mla_q_g[i], mla_k_g[i], mla_w_o[i])
        h = rmsnorm(x, ln_mlp_g[layer])
        x = x + squared_relu_mlp(h, mlp_w_up[layer], mlp_w_down[layer])
    return x


import jax as _jax
import jax.numpy as _jnp

TWIN_FORMAT = 'train_step'
FWD_PARAMS = ['x', 'positions', 'ln_mix_g', 'ln_mlp_g', 'sf_w_in', 'sf_b_f', 'fox_q_g', 'fox_k_g', 'sf_w_o', 'mla_w_down', 'mla_q_a_g', 'mla_kv_a_g', 'mla_w_uq', 'mla_w_ukv', 'mla_q_g', 'mla_k_g', 'mla_w_o', 'mlp_w_up', 'mlp_w_down']
TWIN_WEIGHTS = ['ln_mix_g', 'ln_mlp_g', 'sf_w_in', 'sf_b_f', 'fox_q_g', 'fox_k_g', 'sf_w_o', 'mla_w_down', 'mla_q_a_g', 'mla_kv_a_g', 'mla_w_uq', 'mla_w_ukv', 'mla_q_g', 'mla_k_g', 'mla_w_o', 'mlp_w_up', 'mlp_w_down']
TWIN_DIFF_INPUT = 'x'
TWIN_INPUTS = ['x', 'positions', 'ln_mix_g', 'ln_mlp_g', 'sf_w_in', 'sf_b_f', 'fox_q_g', 'fox_k_g', 'sf_w_o', 'mla_w_down', 'mla_q_a_g', 'mla_kv_a_g', 'mla_w_uq', 'mla_w_ukv', 'mla_q_g', 'mla_k_g', 'mla_w_o', 'mlp_w_up', 'mlp_w_down', 'loss_target', 'm_ln_mix_g', 'm_ln_mlp_g', 'm_sf_w_in', 'm_sf_b_f', 'm_fox_q_g', 'm_fox_k_g', 'm_sf_w_o', 'm_mla_w_down', 'm_mla_q_a_g', 'm_mla_kv_a_g', 'm_mla_w_uq', 'm_mla_w_ukv', 'm_mla_q_g', 'm_mla_k_g', 'm_mla_w_o', 'm_mlp_w_up', 'm_mlp_w_down', 'v_ln_mix_g', 'v_ln_mlp_g', 'v_sf_w_in', 'v_sf_b_f', 'v_fox_q_g', 'v_fox_k_g', 'v_sf_w_o', 'v_mla_w_down', 'v_mla_q_a_g', 'v_mla_kv_a_g', 'v_mla_w_uq', 'v_mla_w_ukv', 'v_mla_q_g', 'v_mla_k_g', 'v_mla_w_o', 'v_mlp_w_up', 'v_mlp_w_down']
TWIN_OUTPUTS = ['loss', 'grad_x', 'grad_ln_mix_g', 'grad_ln_mlp_g', 'grad_sf_w_in', 'grad_sf_b_f', 'grad_fox_q_g', 'grad_fox_k_g', 'grad_sf_w_o', 'grad_mla_w_down', 'grad_mla_q_a_g', 'grad_mla_kv_a_g', 'grad_mla_w_uq', 'grad_mla_w_ukv', 'grad_mla_q_g', 'grad_mla_k_g', 'grad_mla_w_o', 'grad_mlp_w_up', 'grad_mlp_w_down', 'delta_ln_mix_g', 'delta_ln_mlp_g', 'delta_sf_w_in', 'delta_sf_b_f', 'delta_fox_q_g', 'delta_fox_k_g', 'delta_sf_w_o', 'delta_mla_w_down', 'delta_mla_q_a_g', 'delta_mla_kv_a_g', 'delta_mla_w_uq', 'delta_mla_w_ukv', 'delta_mla_q_g', 'delta_mla_k_g', 'delta_mla_w_o', 'delta_mlp_w_up', 'delta_mlp_w_down', 'new_m_ln_mix_g', 'new_m_ln_mlp_g', 'new_m_sf_w_in', 'new_m_sf_b_f', 'new_m_fox_q_g', 'new_m_fox_k_g', 'new_m_sf_w_o', 'new_m_mla_w_down', 'new_m_mla_q_a_g', 'new_m_mla_kv_a_g', 'new_m_mla_w_uq', 'new_m_mla_w_ukv', 'new_m_mla_q_g', 'new_m_mla_k_g', 'new_m_mla_w_o', 'new_m_mlp_w_up', 'new_m_mlp_w_down', 'new_v_ln_mix_g', 'new_v_ln_mlp_g', 'new_v_sf_w_in', 'new_v_sf_b_f', 'new_v_fox_q_g', 'new_v_fox_k_g', 'new_v_sf_w_o', 'new_v_mla_w_down', 'new_v_mla_q_a_g', 'new_v_mla_kv_a_g', 'new_v_mla_w_uq', 'new_v_mla_w_ukv', 'new_v_mla_q_g', 'new_v_mla_k_g', 'new_v_mla_w_o', 'new_v_mlp_w_up', 'new_v_mlp_w_down']
TWIN_LEAF_KINDS = {'loss': 'loss', 'grad_x': 'grad_x', 'grad_ln_mix_g': 'grad_w', 'grad_ln_mlp_g': 'grad_w', 'grad_sf_w_in': 'grad_w', 'grad_sf_b_f': 'grad_w', 'grad_fox_q_g': 'grad_w', 'grad_fox_k_g': 'grad_w', 'grad_sf_w_o': 'grad_w', 'grad_mla_w_down': 'grad_w', 'grad_mla_q_a_g': 'grad_w', 'grad_mla_kv_a_g': 'grad_w', 'grad_mla_w_uq': 'grad_w', 'grad_mla_w_ukv': 'grad_w', 'grad_mla_q_g': 'grad_w', 'grad_mla_k_g': 'grad_w', 'grad_mla_w_o': 'grad_w', 'grad_mlp_w_up': 'grad_w', 'grad_mlp_w_down': 'grad_w', 'delta_ln_mix_g': 'delta_w', 'delta_ln_mlp_g': 'delta_w', 'delta_sf_w_in': 'delta_w', 'delta_sf_b_f': 'delta_w', 'delta_fox_q_g': 'delta_w', 'delta_fox_k_g': 'delta_w', 'delta_sf_w_o': 'delta_w', 'delta_mla_w_down': 'delta_w', 'delta_mla_q_a_g': 'delta_w', 'delta_mla_kv_a_g': 'delta_w', 'delta_mla_w_uq': 'delta_w', 'delta_mla_w_ukv': 'delta_w', 'delta_mla_q_g': 'delta_w', 'delta_mla_k_g': 'delta_w', 'delta_mla_w_o': 'delta_w', 'delta_mlp_w_up': 'delta_w', 'delta_mlp_w_down': 'delta_w', 'new_m_ln_mix_g': 'new_m', 'new_m_ln_mlp_g': 'new_m', 'new_m_sf_w_in': 'new_m', 'new_m_sf_b_f': 'new_m', 'new_m_fox_q_g': 'new_m', 'new_m_fox_k_g': 'new_m', 'new_m_sf_w_o': 'new_m', 'new_m_mla_w_down': 'new_m', 'new_m_mla_q_a_g': 'new_m', 'new_m_mla_kv_a_g': 'new_m', 'new_m_mla_w_uq': 'new_m', 'new_m_mla_w_ukv': 'new_m', 'new_m_mla_q_g': 'new_m', 'new_m_mla_k_g': 'new_m', 'new_m_mla_w_o': 'new_m', 'new_m_mlp_w_up': 'new_m', 'new_m_mlp_w_down': 'new_m', 'new_v_ln_mix_g': 'new_v', 'new_v_ln_mlp_g': 'new_v', 'new_v_sf_w_in': 'new_v', 'new_v_sf_b_f': 'new_v', 'new_v_fox_q_g': 'new_v', 'new_v_fox_k_g': 'new_v', 'new_v_sf_w_o': 'new_v', 'new_v_mla_w_down': 'new_v', 'new_v_mla_q_a_g': 'new_v', 'new_v_mla_kv_a_g': 'new_v', 'new_v_mla_w_uq': 'new_v', 'new_v_mla_w_ukv': 'new_v', 'new_v_mla_q_g': 'new_v', 'new_v_mla_k_g': 'new_v', 'new_v_mla_w_o': 'new_v', 'new_v_mlp_w_up': 'new_v', 'new_v_mlp_w_down': 'new_v'}


def _forward(args):
    return _fwd_reference(*[args[k] for k in FWD_PARAMS])


def _output_shape():
    def fwd():
        inp = _fwd_setup_inputs(0)
        return _fwd_reference(*[inp[k] for k in FWD_PARAMS])
    out = _jax.eval_shape(fwd)
    return out.shape, out.dtype

N_MICROBATCH = 1
ADAM_LR = 0.001
ADAM_B1 = 0.9
ADAM_B2 = 0.999
ADAM_EPS = 1e-08
ADAM_WD = 0.01
ADAM_STEP = 10
PER_EXAMPLE_BATCH_AXIS = {'x': 0, 'positions': 0, 'loss_target': 0}
SHARED_INPUTS = []
_WEIGHT_DTYPES = {'ln_mix_g': _jnp.float32, 'ln_mlp_g': _jnp.float32, 'sf_w_in': _jnp.float32, 'sf_b_f': _jnp.float32, 'fox_q_g': _jnp.float32, 'fox_k_g': _jnp.float32, 'sf_w_o': _jnp.float32, 'mla_w_down': _jnp.float32, 'mla_q_a_g': _jnp.float32, 'mla_kv_a_g': _jnp.float32, 'mla_w_uq': _jnp.float32, 'mla_w_ukv': _jnp.float32, 'mla_q_g': _jnp.float32, 'mla_k_g': _jnp.float32, 'mla_w_o': _jnp.float32, 'mlp_w_up': _jnp.float32, 'mlp_w_down': _jnp.float32}
MOMENT_SCALE = {'ln_mix_g': 5.501101e+00, 'ln_mlp_g': 4.917334e+01, 'sf_w_in': 2.469573e-01, 'sf_b_f': 5.180908e+01, 'fox_q_g': 5.135144e+00, 'fox_k_g': 5.163190e+00, 'sf_w_o': 3.968550e-01, 'mla_w_down': 7.887584e+00, 'mla_q_a_g': 3.626655e-01, 'mla_kv_a_g': 1.495708e+01, 'mla_w_uq': 1.285177e-01, 'mla_w_ukv': 3.923790e+00, 'mla_q_g': 9.415082e-01, 'mla_k_g': 9.437677e-01, 'mla_w_o': 5.325336e+00, 'mlp_w_up': 2.381036e+00, 'mlp_w_down': 9.636577e+00}


def _to_microbatches(a, axis):
    t = _jnp.moveaxis(a, axis, 0)
    t = t.reshape((N_MICROBATCH, t.shape[0] // N_MICROBATCH) + t.shape[1:])
    return _jnp.moveaxis(t, 1, axis + 1)


def setup_inputs(seed: int = 0) -> dict:
    inp = _fwd_setup_inputs(seed)
    key = _jax.random.fold_in(_jax.random.key(seed), 7919)
    shape, _ = _output_shape()
    out = dict(inp)
    out["loss_target"] = _jax.random.normal(_jax.random.fold_in(key, 0), shape, _jnp.float32)
    for i, name in enumerate(TWIN_WEIGHTS):
        w = inp[name].astype(_jnp.float32)
        if MOMENT_SCALE is None:
            s = _jnp.sqrt(_jnp.mean(_jnp.square(w)) + 1e-30)
        else:
            s = MOMENT_SCALE[name]
        km, kv = _jax.random.split(_jax.random.fold_in(key, i + 1))
        out[name] = w
        out["m_" + name] = s * _jax.random.normal(km, w.shape, _jnp.float32)
        out["v_" + name] = (s * s) * _jax.random.uniform(kv, w.shape, _jnp.float32, 0.5, 1.5)
    if N_MICROBATCH > 1:
        for name, axis in PER_EXAMPLE_BATCH_AXIS.items():
            out[name] = _to_microbatches(out[name], axis)
    return {'x': out['x'], 'positions': out['positions'], 'ln_mix_g': out['ln_mix_g'], 'ln_mlp_g': out['ln_mlp_g'], 'sf_w_in': out['sf_w_in'], 'sf_b_f': out['sf_b_f'], 'fox_q_g': out['fox_q_g'], 'fox_k_g': out['fox_k_g'], 'sf_w_o': out['sf_w_o'], 'mla_w_down': out['mla_w_down'], 'mla_q_a_g': out['mla_q_a_g'], 'mla_kv_a_g': out['mla_kv_a_g'], 'mla_w_uq': out['mla_w_uq'], 'mla_w_ukv': out['mla_w_ukv'], 'mla_q_g': out['mla_q_g'], 'mla_k_g': out['mla_k_g'], 'mla_w_o': out['mla_w_o'], 'mlp_w_up': out['mlp_w_up'], 'mlp_w_down': out['mlp_w_down'], 'loss_target': out['loss_target'], 'm_ln_mix_g': out['m_ln_mix_g'], 'm_ln_mlp_g': out['m_ln_mlp_g'], 'm_sf_w_in': out['m_sf_w_in'], 'm_sf_b_f': out['m_sf_b_f'], 'm_fox_q_g': out['m_fox_q_g'], 'm_fox_k_g': out['m_fox_k_g'], 'm_sf_w_o': out['m_sf_w_o'], 'm_mla_w_down': out['m_mla_w_down'], 'm_mla_q_a_g': out['m_mla_q_a_g'], 'm_mla_kv_a_g': out['m_mla_kv_a_g'], 'm_mla_w_uq': out['m_mla_w_uq'], 'm_mla_w_ukv': out['m_mla_w_ukv'], 'm_mla_q_g': out['m_mla_q_g'], 'm_mla_k_g': out['m_mla_k_g'], 'm_mla_w_o': out['m_mla_w_o'], 'm_mlp_w_up': out['m_mlp_w_up'], 'm_mlp_w_down': out['m_mlp_w_down'], 'v_ln_mix_g': out['v_ln_mix_g'], 'v_ln_mlp_g': out['v_ln_mlp_g'], 'v_sf_w_in': out['v_sf_w_in'], 'v_sf_b_f': out['v_sf_b_f'], 'v_fox_q_g': out['v_fox_q_g'], 'v_fox_k_g': out['v_fox_k_g'], 'v_sf_w_o': out['v_sf_w_o'], 'v_mla_w_down': out['v_mla_w_down'], 'v_mla_q_a_g': out['v_mla_q_a_g'], 'v_mla_kv_a_g': out['v_mla_kv_a_g'], 'v_mla_w_uq': out['v_mla_w_uq'], 'v_mla_w_ukv': out['v_mla_w_ukv'], 'v_mla_q_g': out['v_mla_q_g'], 'v_mla_k_g': out['v_mla_k_g'], 'v_mla_w_o': out['v_mla_w_o'], 'v_mlp_w_up': out['v_mlp_w_up'], 'v_mlp_w_down': out['v_mlp_w_down']}


def _loss(weights, diff, rest, loss_target):
    with _jax.named_scope("forward"):
        args = {**rest, TWIN_DIFF_INPUT: diff, **{k: w.astype(_WEIGHT_DTYPES[k]) for k, w in weights.items()}}
        y = _forward(args)
    with _jax.named_scope("loss_head"):
        err = _jnp.square(y.astype(_jnp.float32) - loss_target)
        return 0.5 * _jnp.sum(_jnp.mean(err, axis=-1)) if err.ndim else 0.5 * err


def _adamw(w, g, m, v):
    m = ADAM_B1 * m + (1.0 - ADAM_B1) * g
    v = ADAM_B2 * v + (1.0 - ADAM_B2) * _jnp.square(g)
    m_hat = m / (1.0 - ADAM_B1 ** ADAM_STEP)
    v_hat = v / (1.0 - ADAM_B2 ** ADAM_STEP)
    delta = -ADAM_LR * (m_hat / (_jnp.sqrt(v_hat) + ADAM_EPS) + ADAM_WD * w)
    return delta, m, v


def reference(x, positions, ln_mix_g, ln_mlp_g, sf_w_in, sf_b_f, fox_q_g, fox_k_g, sf_w_o, mla_w_down, mla_q_a_g, mla_kv_a_g, mla_w_uq, mla_w_ukv, mla_q_g, mla_k_g, mla_w_o, mlp_w_up, mlp_w_down, loss_target, m_ln_mix_g, m_ln_mlp_g, m_sf_w_in, m_sf_b_f, m_fox_q_g, m_fox_k_g, m_sf_w_o, m_mla_w_down, m_mla_q_a_g, m_mla_kv_a_g, m_mla_w_uq, m_mla_w_ukv, m_mla_q_g, m_mla_k_g, m_mla_w_o, m_mlp_w_up, m_mlp_w_down, v_ln_mix_g, v_ln_mlp_g, v_sf_w_in, v_sf_b_f, v_fox_q_g, v_fox_k_g, v_sf_w_o, v_mla_w_down, v_mla_q_a_g, v_mla_kv_a_g, v_mla_w_uq, v_mla_w_ukv, v_mla_q_g, v_mla_k_g, v_mla_w_o, v_mlp_w_up, v_mlp_w_down):
    given = dict(x=x, positions=positions, ln_mix_g=ln_mix_g, ln_mlp_g=ln_mlp_g, sf_w_in=sf_w_in, sf_b_f=sf_b_f, fox_q_g=fox_q_g, fox_k_g=fox_k_g, sf_w_o=sf_w_o, mla_w_down=mla_w_down, mla_q_a_g=mla_q_a_g, mla_kv_a_g=mla_kv_a_g, mla_w_uq=mla_w_uq, mla_w_ukv=mla_w_ukv, mla_q_g=mla_q_g, mla_k_g=mla_k_g, mla_w_o=mla_w_o, mlp_w_up=mlp_w_up, mlp_w_down=mlp_w_down, loss_target=loss_target, m_ln_mix_g=m_ln_mix_g, m_ln_mlp_g=m_ln_mlp_g, m_sf_w_in=m_sf_w_in, m_sf_b_f=m_sf_b_f, m_fox_q_g=m_fox_q_g, m_fox_k_g=m_fox_k_g, m_sf_w_o=m_sf_w_o, m_mla_w_down=m_mla_w_down, m_mla_q_a_g=m_mla_q_a_g, m_mla_kv_a_g=m_mla_kv_a_g, m_mla_w_uq=m_mla_w_uq, m_mla_w_ukv=m_mla_w_ukv, m_mla_q_g=m_mla_q_g, m_mla_k_g=m_mla_k_g, m_mla_w_o=m_mla_w_o, m_mlp_w_up=m_mlp_w_up, m_mlp_w_down=m_mlp_w_down, v_ln_mix_g=v_ln_mix_g, v_ln_mlp_g=v_ln_mlp_g, v_sf_w_in=v_sf_w_in, v_sf_b_f=v_sf_b_f, v_fox_q_g=v_fox_q_g, v_fox_k_g=v_fox_k_g, v_sf_w_o=v_sf_w_o, v_mla_w_down=v_mla_w_down, v_mla_q_a_g=v_mla_q_a_g, v_mla_kv_a_g=v_mla_kv_a_g, v_mla_w_uq=v_mla_w_uq, v_mla_w_ukv=v_mla_w_ukv, v_mla_q_g=v_mla_q_g, v_mla_k_g=v_mla_k_g, v_mla_w_o=v_mla_w_o, v_mlp_w_up=v_mlp_w_up, v_mlp_w_down=v_mlp_w_down)
    weights = {n: given[n] for n in TWIN_WEIGHTS}
    shared = {n: given[n] for n in SHARED_INPUTS}
    per_example = {n: given[n] for n in ['x', 'positions']}
    grad_fn = _jax.value_and_grad(_loss, argnums=(0, 1))

    def one_microbatch(ex, loss_target):
        ex = dict(ex)
        diff = ex.pop(TWIN_DIFF_INPUT)
        return grad_fn(weights, diff, {**shared, **ex}, loss_target)

    if N_MICROBATCH == 1:
        loss, (grad_w, grad_x) = one_microbatch(per_example, given["loss_target"])
    else:
        def body(carry, xs):
            loss_sum, grad_sum = carry
            l_k, (gw_k, gx_k) = one_microbatch(xs[0], xs[1])
            with _jax.named_scope("update"):
                return (loss_sum + l_k, _jax.tree.map(_jnp.add, grad_sum, gw_k)), gx_k

        init = (_jnp.zeros((), _jnp.float32), _jax.tree.map(_jnp.zeros_like, weights))
        (loss, grad_w), grad_x = _jax.lax.scan(body, init, (per_example, given["loss_target"]))
    with _jax.named_scope("update"):
        delta_w, new_m, new_v = {}, {}, {}
        for n in TWIN_WEIGHTS:
            delta_w[n], new_m[n], new_v[n] = _adamw(weights[n], grad_w[n], given["m_" + n], given["v_" + n])
    return (loss, grad_x, *[grad_w[n] for n in TWIN_WEIGHTS], *[delta_w[n] for n in TWIN_WEIGHTS],
            *[new_m[n] for n in TWIN_WEIGHTS], *[new_v[n] for n in TWIN_WEIGHTS])
```

```python
import functools
import math

import jax
import jax.numpy as jnp
import numpy as np
from jax import lax
from jax.experimental import pallas as pl
from jax.experimental.pallas import tpu as pltpu

F32 = jnp.float32
BF16 = jnp.bfloat16

NORM_EPS = 1e-6
ROPE_THETA = 10000.0
FORGET_HEADS = 8
HEAD_DIM = 128
ROPE_DIM = 64
MLA_HEADS = 16
MLA_QK_DIM = 192
MLA_PAD_DIM = 256
ADAM_LR, ADAM_B1, ADAM_B2, ADAM_EPS, ADAM_WD, ADAM_STEP = 0.001, 0.9, 0.999, 1e-08, 0.01, 10

N_DEV = 8
LANES = 128
VMEM_LIMIT = 56 * 1024 * 1024
MASKED = -1e30
MESH = pl.DeviceIdType.MESH

NT_DIMS = (((1,), (1,)), ((), ()))
TN_DIMS = (((0,), (0,)), ((), ()))
NN_DIMS = (((1,), (0,)), ((), ()))


def _params(*sem):
    return pltpu.CompilerParams(dimension_semantics=sem, vmem_limit_bytes=VMEM_LIMIT)


def _pick(n, pref):
    best = None
    for t in range(LANES, min(n, pref) + 1, LANES):
        if n % t == 0:
            best = t
    return n if best is None or 2 * best < min(n, pref) else best


def _rows_call(fn, name, row_ins, full_ins, row_outs, acc_outs=(), tile=256):
    row_ins = [r if isinstance(r, tuple) else (r, r.shape[1], 0) for r in row_ins]
    t_rows = row_ins[0][0].shape[0]
    assert t_rows % tile == 0
    n_in = len(row_ins) + len(full_ins)
    n_row_out = len(row_outs)

    def body(*refs):
        res = fn(*[r[...] for r in refs[:n_in]])
        res = res if isinstance(res, tuple) else (res,)
        for ref, val in zip(refs[n_in:n_in + n_row_out], res[:n_row_out]):
            ref[...] = val.astype(ref.dtype)
        acc_refs = refs[n_in + n_row_out:]
        if acc_refs:
            @pl.when(pl.program_id(0) == 0)
            def _():
                for ref in acc_refs:
                    ref[...] = jnp.zeros_like(ref)
            for ref, val in zip(acc_refs, res[n_row_out:]):
                ref[...] += val.astype(ref.dtype)

    in_specs = [pl.BlockSpec((tile, w), functools.partial(lambda i, cb: (i, cb), cb=cb)) for _, w, cb in row_ins]
    in_specs += [pl.BlockSpec(a.shape, lambda i: (0, 0)) for a in full_ins]
    out_specs = [pl.BlockSpec((tile, c), lambda i: (i, 0)) for c, _ in row_outs]
    out_specs += [pl.BlockSpec(s, lambda i: (0, 0)) for s, _ in acc_outs]
    out_shape = [jax.ShapeDtypeStruct((t_rows, c), d) for c, d in row_outs]
    out_shape += [jax.ShapeDtypeStruct(s, d) for s, d in acc_outs]
    outs = pl.pallas_call(
        body, name=name, grid=(t_rows // tile,), in_specs=in_specs, out_specs=out_specs, out_shape=out_shape,
        compiler_params=_params("arbitrary"),
    )(*[r[0] for r in row_ins], *full_ins)
    return outs[0] if len(outs) == 1 else tuple(outs)


def _matmul(a, b, form, out_dtype, name, residual=None, tm=512, tn=1024, tk=512):
    if form == "nn":
        (m, k), n = a.shape, b.shape[1]
    elif form == "nt":
        (m, k), n = a.shape, b.shape[0]
    else:
        (k, m), n = a.shape, b.shape[1]
    tm, tn, tk = _pick(m, tm), _pick(n, tn), _pick(k, tk)
    nk = k // tk
    dims = {"nn": NN_DIMS, "nt": NT_DIMS, "tn": TN_DIMS}[form]

    def body(*refs):
        a_ref, b_ref = refs[0], refs[1]
        res_ref = refs[2] if residual is not None else None
        o_ref, acc_ref = refs[-2], refs[-1]
        kk = pl.program_id(2)

        @pl.when(kk == 0)
        def _():
            acc_ref[...] = jnp.zeros_like(acc_ref)

        acc_ref[...] += lax.dot_general(a_ref[...].astype(BF16), b_ref[...].astype(BF16), dims,
                                        preferred_element_type=F32)

        @pl.when(kk == nk - 1)
        def _():
            out = acc_ref[...]
            if res_ref is not None:
                out = out + res_ref[...]
            o_ref[...] = out.astype(o_ref.dtype)

    a_spec = pl.BlockSpec((tk, tm), lambda i, j, kk: (kk, i)) if form == "tn" else pl.BlockSpec((tm, tk), lambda i, j, kk: (i, kk))
    b_spec = pl.BlockSpec((tn, tk), lambda i, j, kk: (j, kk)) if form == "nt" else pl.BlockSpec((tk, tn), lambda i, j, kk: (kk, j))
    o_spec = pl.BlockSpec((tm, tn), lambda i, j, kk: (i, j))
    ins, in_specs = [a, b], [a_spec, b_spec]
    if residual is not None:
        ins.append(residual)
        in_specs.append(o_spec)
    return pl.pallas_call(
        body, name=name, grid=(m // tm, n // tn, nk), in_specs=in_specs, out_specs=o_spec,
        out_shape=jax.ShapeDtypeStruct((m, n), out_dtype), scratch_shapes=[pltpu.VMEM((tm, tn), F32)],
        compiler_params=_params("parallel", "parallel", "arbitrary"),
    )(*ins)


def _log_sigmoid_parts(z):
    return jnp.log1p(jnp.exp(-jnp.abs(z)))


def _rms_fwd(x, g, n=None):
    n = x.shape[-1] if n is None else n
    r = lax.rsqrt(jnp.sum(x * x, axis=-1, keepdims=True) / n + NORM_EPS)
    return x * r * g


def _rms_bwd(x, g, dout, n=None):
    n = x.shape[-1] if n is None else n
    r = lax.rsqrt(jnp.sum(x * x, axis=-1, keepdims=True) / n + NORM_EPS)
    y = x * r
    dg = jnp.sum(dout * y, axis=0, keepdims=True)
    dy = dout * g
    dx = r * (dy - y * (jnp.sum(dy * y, axis=-1, keepdims=True) / n))
    return dx, dg


def _swap_halves(r):
    lane = lax.broadcasted_iota(jnp.int32, r.shape, 1)
    return jnp.where(lane < ROPE_DIM // 2, pltpu.roll(r, LANES - ROPE_DIM // 2, 1), pltpu.roll(r, ROPE_DIM // 2, 1))


def _rope_fwd(r, cos_t, sin_s):
    return r * cos_t + _swap_halves(r) * sin_s


def _rope_bwd(dr, cos_t, sin_s):
    return dr * cos_t + _swap_halves(dr * sin_s)


def _split3(x):
    hi = x.astype(BF16)
    r1 = x - hi.astype(F32)
    mid = r1.astype(BF16)
    lo = (r1 - mid.astype(F32)).astype(BF16)
    return hi, mid, lo


def _causal_iotas(qi, tq, tk):
    row = qi * tq + lax.broadcasted_iota(jnp.int32, (tq, tk), 0)
    col = lax.broadcasted_iota(jnp.int32, (tq, tk), 1)
    return row, col


def _suffix_matrix(tk, inclusive):
    j = lax.broadcasted_iota(jnp.int32, (2 * tk, tk), 0) % tk
    s = lax.broadcasted_iota(jnp.int32, (2 * tk, tk), 1)
    return jnp.where((j >= s) if inclusive else (j > s), 1.0, 0.0).astype(BF16)


def _suffix_sum(x, mat):
    hi = x.astype(BF16)
    lo = (x - hi.astype(F32)).astype(BF16)
    return lax.dot_general(jnp.concatenate([hi, lo], axis=1), mat, NN_DIMS, preferred_element_type=F32)


def _attn_specs(t_rows, tq, heads, dk, dv, q_off, k_off, v_off):
    q_spec = pl.BlockSpec((tq, dk), lambda h, i: (i, q_off + h))
    k_spec = pl.BlockSpec((t_rows, dk), lambda h, i: (0, k_off + h))
    v_spec = pl.BlockSpec((t_rows, dv), lambda h, i: (0, v_off + h))
    return q_spec, k_spec, v_spec


def _weighted_values(weights, v):
    hi = weights.astype(BF16)
    lo = (weights - hi.astype(F32)).astype(BF16)
    return (lax.dot_general(hi, v, NN_DIMS, preferred_element_type=F32),
            lax.dot_general(lo, v, NN_DIMS, preferred_element_type=F32))


def _attn_fwd(kind, q_arr, k_arr, v_arr, heads, dk, dv, scale, name, q_off=0, k_off=0, v_off=0, fcol=None, frow=None,
              tq=256):
    t_rows = q_arr.shape[0]
    tk = tq
    stick = kind == "stick"
    decay = fcol is not None
    n_in = 5 if decay else 3

    def body(*refs):
        q_ref, k_ref, v_ref = refs[:3]
        fcol_ref, frow_ref = (refs[3], refs[4]) if decay else (None, None)
        o_ref, fine_ref = refs[n_in], refs[n_in + 1]
        lse_ref = None if stick else refs[n_in + 2]
        qi = pl.program_id(1)
        q = q_ref[...]
        row, col = _causal_iotas(qi, tq, tk)
        n_kb = qi + 1
        zeros_o = jnp.zeros((tq, dv), F32)

        if stick:
            mat = _suffix_matrix(tk, inclusive=False)

            def step(i, carry):
                c, acc, rem = carry
                ks = pl.multiple_of((n_kb - 1 - i) * tk, tk)
                k = k_ref[pl.ds(ks, tk), :]
                v = v_ref[pl.ds(ks, tk), :]
                z = lax.dot_general(q, k, NT_DIMS, preferred_element_type=F32) * scale
                strict = (col + ks) < row
                lg = _log_sigmoid_parts(z)
                lom = jnp.where(strict, jnp.minimum(-z, 0.0) - lg, 0.0)
                log_w = (jnp.minimum(z, 0.0) - lg) + (_suffix_sum(lom, mat) + c)
                w = jnp.where(strict, jnp.exp(log_w), 0.0)
                d_acc, d_rem = _weighted_values(w, v)
                return c + jnp.sum(lom, axis=1, keepdims=True), acc + d_acc, rem + d_rem

            _, acc, rem = lax.fori_loop(0, n_kb, step, (jnp.zeros((tq, 1), F32), zeros_o, zeros_o))
            o_ref[...] = acc.astype(o_ref.dtype)
            fine_ref[...] = acc + rem
        else:
            fc = fcol_ref[...] if decay else None

            def step(kb, carry):
                m, l, acc, rem = carry
                ks = pl.multiple_of(kb * tk, tk)
                k = k_ref[pl.ds(ks, tk), :]
                v = v_ref[pl.ds(ks, tk), :]
                s = lax.dot_general(q, k, NT_DIMS, preferred_element_type=F32) * scale
                if decay:
                    s = (s + fc) - frow_ref[:, pl.ds(ks, tk)]
                s = jnp.where((col + ks) <= row, s, MASKED)
                m_new = jnp.maximum(m, jnp.max(s, axis=1, keepdims=True))
                alpha = jnp.exp(m - m_new)
                p = jnp.exp(s - m_new)
                l = alpha * l + jnp.sum(p, axis=1, keepdims=True)
                d_acc, d_rem = _weighted_values(p, v)
                return m_new, l, alpha * acc + d_acc, alpha * rem + d_rem

            m, l, acc, rem = lax.fori_loop(0, n_kb, step, (jnp.full((tq, 1), MASKED, F32), jnp.zeros((tq, 1), F32),
                                                           zeros_o, zeros_o))
            inv_l = 1.0 / l
            o_ref[...] = (acc * inv_l).astype(o_ref.dtype)
            fine_ref[...] = (acc + rem) * inv_l
            lse_ref[...] = m + jnp.log(l)

    q_spec, k_spec, v_spec = _attn_specs(t_rows, tq, heads, dk, dv, q_off, k_off, v_off)
    stat_spec = pl.BlockSpec((None, tq, 1), lambda h, i: (h, i, 0))
    ins, in_specs = [q_arr, k_arr, v_arr], [q_spec, k_spec, v_spec]
    if decay:
        ins += [fcol, frow]
        in_specs += [stat_spec, pl.BlockSpec((None, 1, t_rows), lambda h, i: (h, 0, 0))]
    o_spec = pl.BlockSpec((tq, dv), lambda h, i: (i, h))
    out_specs = [o_spec, o_spec]
    out_shape = [jax.ShapeDtypeStruct((t_rows, heads * dv), BF16), jax.ShapeDtypeStruct((t_rows, heads * dv), F32)]
    if not stick:
        out_specs.append(stat_spec)
        out_shape.append(jax.ShapeDtypeStruct((heads, t_rows, 1), F32))
    return tuple(pl.pallas_call(
        body, name=name, grid=(heads, t_rows // tq), in_specs=in_specs, out_specs=out_specs, out_shape=out_shape,
        compiler_params=_params("parallel", "arbitrary"),
    )(*ins))


def _attn_bwd(kind, q_arr, k_arr, v_arr, o_arr, do_arr, heads, dk, dv, scale, name, q_off=0, k_off=0, v_off=0,
              do_off=0, lse=None, fcol=None, frow=None, tq=256):
    t_rows = q_arr.shape[0]
    tk = tq
    stick = kind == "stick"
    decay = fcol is not None
    n_in = 5 + (0 if stick else 1) + (2 if decay else 0)

    def body(*refs):
        q_ref, k_ref, v_ref, o_ref, do_ref = refs[:5]
        lse_ref = None if stick else refs[5]
        fcol_ref, frow_ref = (refs[6], refs[7]) if decay else (None, None)
        dq_ref, dk_ref, dv_ref = refs[n_in:n_in + 3]
        dfcol_ref, dfrow_ref = (refs[n_in + 3], refs[n_in + 4]) if decay else (None, None)
        qi = pl.program_id(1)

        @pl.when(qi == 0)
        def _():
            dk_ref[...] = jnp.zeros_like(dk_ref)
            dv_ref[...] = jnp.zeros_like(dv_ref)
            if decay:
                dfrow_ref[...] = jnp.zeros_like(dfrow_ref)

        q = q_ref[...]
        do = do_ref[...]
        delta = jnp.sum(do.astype(F32) * o_ref[...], axis=1, keepdims=True)
        row, col = _causal_iotas(qi, tq, tk)
        n_kb = qi + 1

        def accumulate(ks, d_logits, weights, k):
            d_logits = (d_logits * scale).astype(BF16)
            dk_ref[pl.ds(ks, tk), :] += lax.dot_general(d_logits, q, TN_DIMS, preferred_element_type=F32)
            dv_ref[pl.ds(ks, tk), :] += lax.dot_general(weights.astype(BF16), do, TN_DIMS, preferred_element_type=F32)
            return lax.dot_general(d_logits, k, NN_DIMS, preferred_element_type=F32)

        if stick:
            mat_ex = _suffix_matrix(tk, inclusive=False)
            mat_in = _suffix_matrix(tk, inclusive=True)

            def step(i, carry):
                c, gs, dq = carry
                ks = pl.multiple_of((n_kb - 1 - i) * tk, tk)
                k = k_ref[pl.ds(ks, tk), :]
                v = v_ref[pl.ds(ks, tk), :]
                z = lax.dot_general(q, k, NT_DIMS, preferred_element_type=F32) * scale
                strict = (col + ks) < row
                lg = _log_sigmoid_parts(z)
                log_beta = jnp.minimum(z, 0.0) - lg
                log_omb = jnp.minimum(-z, 0.0) - lg
                lom = jnp.where(strict, log_omb, 0.0)
                w = jnp.where(strict, jnp.exp(log_beta + (_suffix_sum(lom, mat_ex) + c)), 0.0)
                dw = lax.dot_general(do, v, NT_DIMS, preferred_element_type=F32)
                g = w * dw
                g_before = delta - (gs + _suffix_sum(g, mat_in))
                dz = jnp.where(strict, g * jnp.exp(log_omb) - g_before * jnp.exp(log_beta), 0.0)
                dq = dq + accumulate(ks, dz, w, k)
                return c + jnp.sum(lom, axis=1, keepdims=True), gs + jnp.sum(g, axis=1, keepdims=True), dq

            zero = jnp.zeros((tq, 1), F32)
            _, _, dq = lax.fori_loop(0, n_kb, step, (zero, zero, jnp.zeros((tq, dk), F32)))
        else:
            lse_v = lse_ref[...]
            fc = fcol_ref[...] if decay else None

            def step(kb, carry):
                dq, row_sum = carry
                ks = pl.multiple_of(kb * tk, tk)
                k = k_ref[pl.ds(ks, tk), :]
                v = v_ref[pl.ds(ks, tk), :]
                s = lax.dot_general(q, k, NT_DIMS, preferred_element_type=F32) * scale
                if decay:
                    s = (s + fc) - frow_ref[:, pl.ds(ks, tk)]
                p = jnp.where((col + ks) <= row, jnp.exp(s - lse_v), 0.0)
                dp = lax.dot_general(do, v, NT_DIMS, preferred_element_type=F32)
                ds = p * (dp - delta)
                if decay:
                    dfrow_ref[:, pl.ds(ks, tk)] += jnp.sum(ds, axis=0, keepdims=True)
                    row_sum = row_sum + jnp.sum(ds, axis=1, keepdims=True)
                return dq + accumulate(ks, ds, p, k), row_sum

            dq, row_sum = lax.fori_loop(0, n_kb, step, (jnp.zeros((tq, dk), F32), jnp.zeros((tq, 1), F32)))
            if decay:
                dfcol_ref[...] = row_sum
        dq_ref[...] = dq

    q_spec, k_spec, v_spec = _attn_specs(t_rows, tq, heads, dk, dv, q_off, k_off, v_off)
    stat_spec = pl.BlockSpec((None, tq, 1), lambda h, i: (h, i, 0))
    frow_spec = pl.BlockSpec((None, 1, t_rows), lambda h, i: (h, 0, 0))
    ins = [q_arr, k_arr, v_arr, o_arr, do_arr]
    in_specs = [q_spec, k_spec, v_spec, pl.BlockSpec((tq, dv), lambda h, i: (i, h)),
                pl.BlockSpec((tq, dv), lambda h, i: (i, do_off + h))]
    if not stick:
        ins.append(lse)
        in_specs.append(stat_spec)
    if decay:
        ins += [fcol, frow]
        in_specs += [stat_spec, frow_spec]
    out_specs = [pl.BlockSpec((tq, dk), lambda h, i: (i, h)), pl.BlockSpec((t_rows, dk), lambda h, i: (0, h)),
                 pl.BlockSpec((t_rows, dv), lambda h, i: (0, h))]
    out_shape = [jax.ShapeDtypeStruct((t_rows, heads * dk), F32), jax.ShapeDtypeStruct((t_rows, heads * dk), F32),
                 jax.ShapeDtypeStruct((t_rows, heads * dv), F32)]
    if decay:
        out_specs += [stat_spec, frow_spec]
        out_shape += [jax.ShapeDtypeStruct((heads, t_rows, 1), F32), jax.ShapeDtypeStruct((heads, 1, t_rows), F32)]
    return pl.pallas_call(
        body, name=name, grid=(heads, t_rows // tq), in_specs=in_specs, out_specs=out_specs, out_shape=out_shape,
        compiler_params=_params("parallel", "arbitrary"),
    )(*ins)


def _prefix_matrix(reverse):
    j = lax.broadcasted_iota(jnp.int32, (LANES, LANES), 0)
    s = lax.broadcasted_iota(jnp.int32, (LANES, LANES), 1)
    return jnp.where((j >= s) if reverse else (j <= s), 1.0, 0.0).astype(BF16)


def _chunk_cumsum(x, mat):
    return sum(lax.dot_general(part, mat, NN_DIMS, preferred_element_type=F32) for part in _split3(x))


def _gate_fwd(logit_t, bias_col):
    heads, t_rows = logit_t.shape

    def body(x_ref, b_ref, out_ref):
        mat = _prefix_matrix(reverse=False)

        def step(ci, carry):
            cs = pl.multiple_of(ci * LANES, LANES)
            pre = x_ref[:, pl.ds(cs, LANES)] + b_ref[...]
            log_f = jnp.minimum(pre, 0.0) - _log_sigmoid_parts(pre)
            out_ref[:, pl.ds(cs, LANES)] = _chunk_cumsum(log_f, mat) + carry
            return carry + jnp.sum(log_f, axis=1, keepdims=True)

        lax.fori_loop(0, t_rows // LANES, step, jnp.zeros((heads, 1), F32))

    return pl.pallas_call(body, name="gate_fwd", out_shape=jax.ShapeDtypeStruct((heads, t_rows), F32),
                          compiler_params=pltpu.CompilerParams(vmem_limit_bytes=VMEM_LIMIT))(logit_t, bias_col)


def _gate_bwd(dcum_t, logit_t, bias_col):
    heads, t_rows = logit_t.shape
    n_chunks = t_rows // LANES

    def body(d_ref, x_ref, b_ref, dx_ref, db_ref):
        mat = _prefix_matrix(reverse=True)

        def step(i, carry):
            tail, db = carry
            cs = pl.multiple_of((n_chunks - 1 - i) * LANES, LANES)
            d = d_ref[:, pl.ds(cs, LANES)]
            d_log_f = _chunk_cumsum(d, mat) + tail
            pre = x_ref[:, pl.ds(cs, LANES)] + b_ref[...]
            e = jnp.exp(-jnp.abs(pre))
            d_pre = d_log_f * (jnp.where(pre >= 0.0, e, 1.0) / (1.0 + e))
            dx_ref[:, pl.ds(cs, LANES)] = d_pre
            return tail + jnp.sum(d, axis=1, keepdims=True), db + jnp.sum(d_pre, axis=1, keepdims=True)

        zero = jnp.zeros((heads, 1), F32)
        _, db = lax.fori_loop(0, n_chunks, step, (zero, zero))
        db_ref[...] = db

    return pl.pallas_call(body, name="gate_bwd",
                          out_shape=(jax.ShapeDtypeStruct((heads, t_rows), F32), jax.ShapeDtypeStruct((heads, 1), F32)),
                          compiler_params=pltpu.CompilerParams(vmem_limit_bytes=VMEM_LIMIT))(dcum_t, logit_t, bias_col)


def _norm_fwd(x, g, name):
    return _rows_call(lambda xv, gv: _rms_fwd(xv, gv), name, [x], [g], [(x.shape[1], BF16)])


def _norm_bwd(x, g, dh, dres, name):
    def fn(xv, dhv, dresv, gv):
        dx, dg = _rms_bwd(xv, gv, dhv)
        return dresv + dx, dg
    return _rows_call(fn, name, [x, dh, dres], [g], [(x.shape[1], F32)], [((1, x.shape[1]), F32)])


def _sqrelu_fwd(u, name):
    return _rows_call(lambda uv: jnp.square(jnp.maximum(uv, 0.0)), name, [u], [], [(u.shape[1], BF16)], tile=128)


def _sqrelu_bwd(da, u, name):
    return _rows_call(lambda dav, uv: dav * (2.0 * jnp.maximum(uv, 0.0)), name, [da, u], [], [(u.shape[1], BF16)],
                      tile=128)


def _loss_fwd_bwd(y, target):
    d_model = y.shape[1]

    def fn(yv, tv):
        err = yv - tv
        return err * (1.0 / d_model), jnp.sum(jnp.sum(err * err, axis=1, keepdims=True), axis=0, keepdims=True)
    return _rows_call(fn, "loss", [y, target], [], [(d_model, F32)], [((1, 1), F32)])


def _heads_apply(fn, n_heads, width, *tiles):
    return [fn(*[t[:, h * width:(h + 1) * width] for t in tiles]) for h in range(n_heads)]


def _fox_norm_fwd(pb, gq, gk, heads):
    width = heads * HEAD_DIM

    def fn(qk, gqv, gkv):
        q = jnp.concatenate(_heads_apply(lambda t: _rms_fwd(t, gqv), heads, HEAD_DIM, qk[:, :width]), axis=1)
        k = jnp.concatenate(_heads_apply(lambda t: _rms_fwd(t, gkv), heads, HEAD_DIM, qk[:, width:]), axis=1)
        return q, k
    return _rows_call(fn, "fox_norm_fwd", [(pb, 2 * width, 0)], [gq, gk], [(width, BF16), (width, BF16)])


def _fox_norm_bwd(pb, gq, gk, dq, dk, heads):
    width = heads * HEAD_DIM

    def fn(qk, dqv, dkv, gqv, gkv):
        res_q = _heads_apply(lambda t, d: _rms_bwd(t, gqv, d), heads, HEAD_DIM, qk[:, :width], dqv)
        res_k = _heads_apply(lambda t, d: _rms_bwd(t, gkv, d), heads, HEAD_DIM, qk[:, width:], dkv)
        dqk = jnp.concatenate([r[0] for r in res_q] + [r[0] for r in res_k], axis=1)
        return dqk, sum(r[1] for r in res_q), sum(r[1] for r in res_k)
    return _rows_call(fn, "fox_norm_bwd", [(pb, 2 * width, 0), dq, dk], [gq, gk], [(2 * width, BF16)],
                      [((1, HEAD_DIM), F32), ((1, HEAD_DIM), F32)])


def _lora_norm_fwd(down, gq, gkv, rank):
    def fn(dv, gqv, gkvv):
        return _rms_fwd(dv[:, :rank], gqv), _rms_fwd(dv[:, rank:], gkvv)
    return _rows_call(fn, "lora_norm_fwd", [(down, 2 * rank, 0)], [gq, gkv], [(rank, BF16), (rank, BF16)])


def _lora_norm_bwd(down, gq, gkv, dcq, dckv, dkpe, rank):
    def fn(dv, dcqv, dckvv, dkpev, gqv, gkvv):
        dxq, dgq = _rms_bwd(dv[:, :rank], gqv, dcqv)
        dxkv, dgkv = _rms_bwd(dv[:, rank:], gkvv, dckvv)
        return jnp.concatenate([dxq, dxkv, dkpev], axis=1), dgq, dgkv
    return _rows_call(fn, "lora_norm_bwd", [(down, 2 * rank, 0), dcq, dckv, dkpe], [gq, gkv],
                      [(2 * rank + LANES, BF16)], [((1, rank), F32), ((1, rank), F32)])


def _rope_tables(pos_col, inv_freq, sin_sign):
    def fn(pos, invf, sign):
        ang = pos.astype(F32) * invf
        return jnp.cos(ang) * jnp.abs(sign), jnp.sin(ang) * sign
    return _rows_call(fn, "rope_tables", [pos_col], [inv_freq, sin_sign], [(LANES, F32), (LANES, F32)])


def _mla_prep_fwd(q_raw, kv, down, kpe_block, qg, kg, cos_t, sin_s):
    def fn(qv, kvv, kpe, cosv, sinv, qgv, kgv):
        qs, ks, vs = [], [], []
        for h in range(MLA_HEADS):
            qn = _rms_fwd(qv[:, h * MLA_PAD_DIM:(h + 1) * MLA_PAD_DIM], qgv, MLA_QK_DIM)
            qs += [qn[:, :HEAD_DIM], _rope_fwd(qn[:, HEAD_DIM:], cosv, sinv)]
            k_full = jnp.concatenate([kvv[:, h * MLA_PAD_DIM:h * MLA_PAD_DIM + HEAD_DIM], kpe], axis=1)
            kn = _rms_fwd(k_full, kgv, MLA_QK_DIM)
            ks += [kn[:, :HEAD_DIM], _rope_fwd(kn[:, HEAD_DIM:], cosv, sinv)]
            vs.append(kvv[:, h * MLA_PAD_DIM + HEAD_DIM:(h + 1) * MLA_PAD_DIM])
        return jnp.concatenate(qs, axis=1), jnp.concatenate(ks, axis=1), jnp.concatenate(vs, axis=1)
    wide = MLA_HEADS * MLA_PAD_DIM
    return _rows_call(fn, "mla_prep_fwd", [q_raw, kv, (down, LANES, kpe_block), cos_t, sin_s], [qg, kg],
                      [(wide, BF16), (wide, BF16), (MLA_HEADS * HEAD_DIM, BF16)], tile=128)


def _mla_prep_bwd(q_raw, kv, down, kpe_block, qg, kg, cos_t, sin_s, dq, dk, dv):
    def fn(qv, kvv, kpe, cosv, sinv, dqv, dkv, dvv, qgv, kgv):
        dqs, dkvs = [], []
        dkpe = jnp.zeros_like(kpe)
        dqg = jnp.zeros_like(qgv)
        dkg = jnp.zeros_like(kgv)
        for h in range(MLA_HEADS):
            lo, hi = h * MLA_PAD_DIM, (h + 1) * MLA_PAD_DIM
            dqn = jnp.concatenate([dqv[:, lo:lo + HEAD_DIM], _rope_bwd(dqv[:, lo + HEAD_DIM:hi], cosv, sinv)], axis=1)
            dqh, dg = _rms_bwd(qv[:, lo:hi], qgv, dqn, MLA_QK_DIM)
            dqs.append(dqh)
            dqg = dqg + dg
            k_full = jnp.concatenate([kvv[:, lo:lo + HEAD_DIM], kpe], axis=1)
            dkn = jnp.concatenate([dkv[:, lo:lo + HEAD_DIM], _rope_bwd(dkv[:, lo + HEAD_DIM:hi], cosv, sinv)], axis=1)
            dkh, dg = _rms_bwd(k_full, kgv, dkn, MLA_QK_DIM)
            dkg = dkg + dg
            dkpe = dkpe + dkh[:, HEAD_DIM:]
            dkvs += [dkh[:, :HEAD_DIM], dvv[:, h * HEAD_DIM:(h + 1) * HEAD_DIM]]
        return jnp.concatenate(dqs, axis=1), jnp.concatenate(dkvs, axis=1), dkpe, dqg, dkg
    wide = MLA_HEADS * MLA_PAD_DIM
    return _rows_call(fn, "mla_prep_bwd", [q_raw, kv, (down, LANES, kpe_block), cos_t, sin_s, dq, dk, dv], [qg, kg],
                      [(wide, BF16), (wide, BF16), (LANES, F32)], [((1, MLA_PAD_DIM), F32), ((1, MLA_PAD_DIM), F32)],
                      tile=128)


def _mlp_fwd(x, g, w_up, w_down, tag):
    h = _norm_fwd(x, g, f"mlp_norm_fwd{tag}")
    u = _matmul(h, w_up, "nn", F32, f"mlp_up{tag}")
    a = _sqrelu_fwd(u, f"sqrelu_fwd{tag}")
    return _matmul(a, w_down, "nn", F32, f"mlp_down{tag}", residual=x), (h, u, a)


def _mlp_bwd(x, g, w_up, w_down, saved, dy, tag):
    h, u, a = saved
    dw_down = _matmul(a, dy, "tn", BF16, f"mlp_dwdown{tag}")
    da = _matmul(dy, w_down, "nt", F32, f"mlp_da{tag}")
    du = _sqrelu_bwd(da, u, f"sqrelu_bwd{tag}")
    dw_up = _matmul(h, du, "tn", BF16, f"mlp_dwup{tag}")
    dh = _matmul(du, w_up, "nt", F32, f"mlp_dh{tag}")
    dx, dg = _norm_bwd(x, g, dh, dy, f"mlp_norm_bwd{tag}")
    return dx, dg, dw_up, dw_down


def _local_step(x, pos_col, target, w):
    t_rows, d_model = x.shape
    hs = w["w_a"].shape[1] // (4 * HEAD_DIM)
    sb_w = hs * HEAD_DIM
    grads = {}

    h0 = _norm_fwd(x, w["ln_mix0"], "mix0_norm_fwd")
    pa = _matmul(h0, w["w_a"], "nn", BF16, "in_proj_a")
    pb = _matmul(h0, w["w_b"], "nn", F32, "in_proj_b")
    o_sb, o_sb_fine = _attn_fwd("stick", pa, pa, pa, hs, HEAD_DIM, HEAD_DIM, HEAD_DIM ** -0.5, "stick_fwd",
                                q_off=0, k_off=hs, v_off=2 * hs)
    logit_t = pb[:, 2 * sb_w:2 * sb_w + hs].T
    bias_col = w["b_f"][0, :hs].reshape(hs, 1)
    f_cum = _gate_fwd(logit_t, bias_col)
    f_col, f_row = f_cum[:, :, None], f_cum[:, None, :]
    qf, kf = _fox_norm_fwd(pb, w["fox_q_g"], w["fox_k_g"], hs)
    o_fx, o_fx_fine, lse_fx = _attn_fwd("softmax", qf, kf, pa, hs, HEAD_DIM, HEAD_DIM, HEAD_DIM ** -0.5, "fox_fwd",
                                        v_off=3 * hs, fcol=f_col, frow=f_row)
    o0 = jnp.concatenate([o_sb, o_fx], axis=1)
    x1 = _matmul(o0, w["w_o0"], "nn", F32, "out_proj0", residual=x)
    x2, mlp0 = _mlp_fwd(x1, w["ln_mlp0"], w["w_up0"], w["w_dn0"], "0")

    rank = w["w_uq"].shape[0]
    h2 = _norm_fwd(x2, w["ln_mix1"], "mix1_norm_fwd")
    down = _matmul(h2, w["w_down"], "nn", F32, "mla_down")
    cqn, ckvn = _lora_norm_fwd(down, w["q_a_g"], w["kv_a_g"], rank)
    q_raw = _matmul(cqn, w["w_uq"], "nn", F32, "mla_uq")
    kv = _matmul(ckvn, w["w_ukv"], "nn", F32, "mla_ukv")
    cos_t, sin_s = _rope_tables(pos_col, w["inv_freq"], w["sin_sign"])
    kpe_block = 2 * rank // LANES
    qm, km, vm = _mla_prep_fwd(q_raw, kv, down, kpe_block, w["mla_q_g"], w["mla_k_g"], cos_t, sin_s)
    o_m, o_m_fine, lse_m = _attn_fwd("softmax", qm, km, vm, MLA_HEADS, MLA_PAD_DIM, HEAD_DIM, MLA_QK_DIM ** -0.5,
                                     "mla_fwd")
    x3 = _matmul(o_m, w["w_o1"], "nn", F32, "out_proj1", residual=x2)
    x4, mlp1 = _mlp_fwd(x3, w["ln_mlp1"], w["w_up1"], w["w_dn1"], "1")

    dy, sq_err = _loss_fwd_bwd(x4, target)

    dx3, grads["ln_mlp1"], grads["w_up1"], grads["w_dn1"] = _mlp_bwd(x3, w["ln_mlp1"], w["w_up1"], w["w_dn1"], mlp1, dy, "1")
    grads["w_o1"] = _matmul(o_m, dx3, "tn", BF16, "dw_o1")
    do_m = _matmul(dx3, w["w_o1"], "nt", BF16, "do_mla")
    dqm, dkm, dvm = _attn_bwd("softmax", qm, km, vm, o_m_fine, do_m, MLA_HEADS, MLA_PAD_DIM, HEAD_DIM, MLA_QK_DIM ** -0.5,
                              "mla_bwd", lse=lse_m)
    dq_raw, dkv, dkpe, grads["mla_q_g"], grads["mla_k_g"] = _mla_prep_bwd(
        q_raw, kv, down, kpe_block, w["mla_q_g"], w["mla_k_g"], cos_t, sin_s, dqm, dkm, dvm)
    grads["w_uq"] = _matmul(cqn, dq_raw, "tn", BF16, "dw_uq")
    grads["w_ukv"] = _matmul(ckvn, dkv, "tn", BF16, "dw_ukv")
    dcqn = _matmul(dq_raw, w["w_uq"], "nt", F32, "d_cq")
    dckvn = _matmul(dkv, w["w_ukv"], "nt", F32, "d_ckv")
    ddown, grads["q_a_g"], grads["kv_a_g"] = _lora_norm_bwd(down, w["q_a_g"], w["kv_a_g"], dcqn, dckvn, dkpe, rank)
    grads["w_down"] = _matmul(h2, ddown, "tn", BF16, "dw_down")
    dh2 = _matmul(ddown, w["w_down"], "nt", F32, "d_h2")
    dx2, grads["ln_mix1"] = _norm_bwd(x2, w["ln_mix1"], dh2, dx3, "mix1_norm_bwd")

    dx1, grads["ln_mlp0"], grads["w_up0"], grads["w_dn0"] = _mlp_bwd(x1, w["ln_mlp0"], w["w_up0"], w["w_dn0"], mlp0, dx2, "0")
    grads["w_o0"] = _matmul(o0, dx1, "tn", BF16, "dw_o0")
    do0 = _matmul(dx1, w["w_o0"], "nt", BF16, "do_mix0")
    dq_sb, dk_sb, dv_sb = _attn_bwd("stick", pa, pa, pa, o_sb_fine, do0, hs, HEAD_DIM, HEAD_DIM, HEAD_DIM ** -0.5,
                                    "stick_bwd", q_off=0, k_off=hs, v_off=2 * hs, do_off=0)
    dqf, dkf, dv_fx, ds_rows, ds_cols = _attn_bwd(
        "softmax", qf, kf, pa, o_fx_fine, do0, hs, HEAD_DIM, HEAD_DIM, HEAD_DIM ** -0.5, "fox_bwd", v_off=3 * hs,
        do_off=hs, lse=lse_fx, fcol=f_col, frow=f_row)
    dqk_fx, grads["fox_q_g"], grads["fox_k_g"] = _fox_norm_bwd(pb, w["fox_q_g"], w["fox_k_g"], dqf, dkf, hs)
    dlogit_t, db_f = _gate_bwd(ds_rows[:, :, 0] - ds_cols[:, 0, :], logit_t, bias_col)
    grads["b_f"] = db_f.reshape(1, hs)
    dpa = jnp.concatenate([dq_sb.astype(BF16), dk_sb.astype(BF16), dv_sb.astype(BF16), dv_fx.astype(BF16)], axis=1)
    dlogit_pad = jnp.pad(dlogit_t.T.astype(BF16), ((0, 0), (0, pb.shape[1] - 2 * sb_w - hs)))
    dpb = jnp.concatenate([dqk_fx, dlogit_pad], axis=1)
    grads["w_a"] = _matmul(h0, dpa, "tn", BF16, "dw_a")
    grads["w_b"] = _matmul(h0, dpb, "tn", BF16, "dw_b")
    dh0 = _matmul(dpb, w["w_b"], "nt", F32, "d_h0_b")
    dh0 = _matmul(dpa, w["w_a"], "nt", F32, "d_h0_a", residual=dh0)
    grad_x, grads["ln_mix0"] = _norm_bwd(x, w["ln_mix0"], dh0, dx1, "mix0_norm_bwd")
    return sq_err, grad_x, grads


def _mesh_position():
    x, y, c = lax.axis_index("x"), lax.axis_index("y"), lax.axis_index("c")
    return x, y, c, 4 * x + 2 * y + c


def _peer(x, y, c, k):
    bx, by, bc = (k >> 2) & 1, (k >> 1) & 1, k & 1
    px, py, pc = x ^ bx, y ^ by, c ^ bc
    return (px, py, pc), 4 * px + 2 * py + pc


def _all_gather(block, name):
    def body(x_ref, out_ref, send_sems, recv_sems, local_sem):
        x, y, c, me = _mesh_position()
        sibling = (x, y, 1 - c)
        chips = [(1 - x, y), (x, 1 - y), (1 - x, 1 - y)]

        def slot(px, py, pc):
            return out_ref.at[4 * px + 2 * py + pc]

        def copy(k, blk, to, src=None):
            return pltpu.make_async_remote_copy(
                src_ref=slot(*blk) if src is None else src, dst_ref=slot(*blk), send_sem=send_sems.at[k],
                recv_sem=recv_sems.at[k], device_id=to, device_id_type=MESH)

        mine = pltpu.make_async_copy(x_ref, out_ref.at[me], local_sem)
        mine.start()
        first = [copy(0, (x, y, c), sibling, src=x_ref)]
        first += [copy(1 + j, (x, y, c), (*chip, c), src=x_ref) for j, chip in enumerate(chips)]
        for cp in first:
            cp.start()
        passed = [copy(4 + j, (*chip, c), sibling) for j, chip in enumerate(chips)]
        for j, chip in enumerate(chips):
            copy(1 + j, (*chip, c), (x, y, c)).wait_recv()
            passed[j].start()
        copy(0, (x, y, 1 - c), (x, y, c)).wait_recv()
        for j, chip in enumerate(chips):
            copy(4 + j, (*chip, 1 - c), (x, y, c)).wait_recv()
        for cp in first + passed:
            cp.wait_send()
        mine.wait()

    return pl.pallas_call(
        body, name=name, out_shape=jax.ShapeDtypeStruct((N_DEV,) + block.shape, block.dtype),
        in_specs=[pl.BlockSpec(memory_space=pl.ANY)], out_specs=pl.BlockSpec(memory_space=pl.ANY),
        scratch_shapes=[pltpu.SemaphoreType.DMA((7,)), pltpu.SemaphoreType.DMA((7,)), pltpu.SemaphoreType.DMA],
    )(block)


def _all_to_all(slots, name):
    def body(g_ref, out_ref, send_sems, recv_sems, local_sem):
        x, y, c, me = _mesh_position()
        mine = pltpu.make_async_copy(g_ref.at[me], out_ref.at[me], local_sem)
        mine.start()
        copies = []
        for k in range(1, N_DEV):
            peer, peer_idx = _peer(x, y, c, k)
            cp = pltpu.make_async_remote_copy(
                src_ref=g_ref.at[peer_idx], dst_ref=out_ref.at[me], send_sem=send_sems.at[k - 1],
                recv_sem=recv_sems.at[k - 1], device_id=peer, device_id_type=MESH)
            cp.start()
            copies.append(cp)
        for k in range(1, N_DEV):
            peer, peer_idx = _peer(x, y, c, k)
            pltpu.make_async_remote_copy(
                src_ref=g_ref.at[me], dst_ref=out_ref.at[peer_idx], send_sem=send_sems.at[k - 1],
                recv_sem=recv_sems.at[k - 1], device_id=peer, device_id_type=MESH).wait_recv()
        for cp in copies:
            cp.wait_send()
        mine.wait()

    return pl.pallas_call(
        body, name=name, out_shape=jax.ShapeDtypeStruct(slots.shape, slots.dtype),
        in_specs=[pl.BlockSpec(memory_space=pl.ANY)], out_specs=pl.BlockSpec(memory_space=pl.ANY),
        scratch_shapes=[pltpu.SemaphoreType.DMA((7,)), pltpu.SemaphoreType.DMA((7,)), pltpu.SemaphoreType.DMA],
    )(slots)


PACK_TILE = 1024


def _as_rows(a, row_multiple=16):
    flat = a.reshape(-1)
    rows = -(-flat.shape[0] // LANES)
    rows = -(-rows // row_multiple) * row_multiple
    return jnp.pad(flat, (0, rows * LANES - flat.shape[0])).reshape(rows, LANES)


def _pack_rows(parts, axis, total_rows, dtype):
    used = sum(p.shape[axis] for p in parts)
    shape = list(parts[0].shape)
    shape[axis] = total_rows - used
    return jnp.concatenate([p.astype(dtype) for p in parts] + [jnp.ones(shape, dtype)], axis=axis)


def _cast_bf16(a):
    return _rows_call(lambda v: v, "cast_bf16", [a], [], [(a.shape[1], BF16)], tile=PACK_TILE)


def _adam_math(w, g, m, v):
    m = ADAM_B1 * m + (1.0 - ADAM_B1) * g
    v = ADAM_B2 * v + (1.0 - ADAM_B2) * jnp.square(g)
    m_hat = m / (1.0 - ADAM_B1 ** ADAM_STEP)
    v_hat = v / (1.0 - ADAM_B2 ** ADAM_STEP)
    delta = -ADAM_LR * (m_hat / (jnp.sqrt(v_hat) + ADAM_EPS) + ADAM_WD * w)
    return delta, m, v


def _adam_big(recv, w, m, v):
    rows = w.shape[0]

    def body(r_ref, w_ref, m_ref, v_ref, g_ref, d_ref, nm_ref, nv_ref):
        g = r_ref[0].astype(F32)
        for s in range(1, N_DEV):
            g = g + r_ref[s].astype(F32)
        delta, nm, nv = _adam_math(w_ref[...], g, m_ref[...], v_ref[...])
        g_ref[...] = g
        d_ref[...] = delta
        nm_ref[...] = nm
        nv_ref[...] = nv

    spec = pl.BlockSpec((PACK_TILE, LANES), lambda i: (i, 0))
    out = jax.ShapeDtypeStruct((rows, LANES), F32)
    return pl.pallas_call(
        body, name="adam_big", grid=(rows // PACK_TILE,),
        in_specs=[pl.BlockSpec((N_DEV, PACK_TILE, LANES), lambda i: (0, i, 0)), spec, spec, spec],
        out_specs=[spec] * 4, out_shape=[out] * 4, compiler_params=_params("parallel"),
    )(recv, w, m, v)


def _sum_slots(gathered):
    rows = gathered.shape[1]

    def body(r_ref, o_ref):
        acc = r_ref[0]
        for s in range(1, N_DEV):
            acc = acc + r_ref[s]
        o_ref[...] = acc

    return pl.pallas_call(body, name="sum_small", out_shape=jax.ShapeDtypeStruct((rows, LANES), F32))(gathered)


def _adam_small(w, g, m, v):
    def fn(wv, gv, mv, vv):
        return _adam_math(wv, gv, mv, vv)
    return _rows_call(fn, "adam_small", [w, g, m, v], [], [(LANES, F32)] * 3, tile=w.shape[0])


BIG = ["sf_w_in", "sf_w_o", "mla_w_down", "mla_w_uq", "mla_w_ukv", "mla_w_o", "mlp_w_up", "mlp_w_down"]
SMALL = ["ln_mix_g", "ln_mlp_g", "sf_b_f", "fox_q_g", "fox_k_g", "mla_q_a_g", "mla_kv_a_g", "mla_q_g", "mla_k_g"]
ALL_W = ["ln_mix_g", "ln_mlp_g", "sf_w_in", "sf_b_f", "fox_q_g", "fox_k_g", "sf_w_o", "mla_w_down", "mla_q_a_g",
         "mla_kv_a_g", "mla_w_uq", "mla_w_ukv", "mla_q_g", "mla_k_g", "mla_w_o", "mlp_w_up", "mlp_w_down"]
SHARD_AXIS = {"sf_w_in": 2, "sf_w_o": 1, "mla_w_down": 1, "mla_w_uq": 2, "mla_w_ukv": 2, "mla_w_o": 1, "mlp_w_up": 2,
              "mlp_w_down": 1}


def _unshard(stack, axis):
    moved = jnp.moveaxis(stack, 0, axis)
    shape = list(stack.shape[1:])
    shape[axis] *= N_DEV
    return moved.reshape(shape)


def _shard_stack(full, axis):
    shape = list(full.shape)
    shape[axis:axis + 1] = [N_DEV, shape[axis] // N_DEV]
    return jnp.moveaxis(full.reshape(shape), axis, 0)


def _model_weights(full, small):
    w_in = full["sf_w_in"][0]
    d_model = w_in.shape[0]
    n_fx = small["sf_b_f"].shape[1]
    sb_w = (w_in.shape[1] - n_fx) // 6
    cols = lambda i: w_in[:, i * sb_w:(i + 1) * sb_w]
    w_a = jnp.concatenate([cols(0), cols(1), cols(2), cols(5)], axis=1)
    w_b = jnp.concatenate([cols(3), cols(4), w_in[:, 6 * sb_w:], jnp.zeros((d_model, LANES - n_fx), w_in.dtype)], axis=1)
    rank = full["mla_w_uq"].shape[1]
    w_down = jnp.pad(full["mla_w_down"][0], ((0, 0), (0, LANES - ROPE_DIM)))
    w_uq = full["mla_w_uq"][0].reshape(rank, MLA_HEADS, MLA_QK_DIM)
    w_uq = jnp.pad(w_uq, ((0, 0), (0, 0), (0, MLA_PAD_DIM - MLA_QK_DIM))).reshape(rank, MLA_HEADS * MLA_PAD_DIM)
    half = ROPE_DIM // 2
    inv_freq = ROPE_THETA ** (-jnp.arange(half, dtype=F32) / half)
    zeros64 = jnp.zeros((ROPE_DIM,), F32)
    pad256 = lambda g: jnp.pad(g, ((0, 0), (0, MLA_PAD_DIM - MLA_QK_DIM)))
    pad_lanes = lambda g: jnp.pad(g, ((0, 0), (0, LANES - g.shape[1])))
    return dict(
        ln_mix0=small["ln_mix_g"][0:1], ln_mix1=small["ln_mix_g"][1:2],
        ln_mlp0=small["ln_mlp_g"][0:1], ln_mlp1=small["ln_mlp_g"][1:2],
        w_a=w_a, w_b=w_b, b_f=pad_lanes(small["sf_b_f"]), fox_q_g=small["fox_q_g"], fox_k_g=small["fox_k_g"],
        w_o0=full["sf_w_o"][0], w_up0=full["mlp_w_up"][0], w_dn0=full["mlp_w_down"][0],
        w_up1=full["mlp_w_up"][1], w_dn1=full["mlp_w_down"][1],
        w_down=w_down, q_a_g=small["mla_q_a_g"], kv_a_g=small["mla_kv_a_g"], w_uq=w_uq, w_ukv=full["mla_w_ukv"][0],
        mla_q_g=pad256(small["mla_q_g"]), mla_k_g=pad256(small["mla_k_g"]), w_o1=full["mla_w_o"][0],
        inv_freq=jnp.concatenate([inv_freq, inv_freq, zeros64]).reshape(1, LANES),
        sin_sign=jnp.concatenate([-jnp.ones((half,), F32), jnp.ones((half,), F32), zeros64]).reshape(1, LANES),
    )


def _grads_in_weight_layout(g, full):
    n_fx = g["b_f"].shape[1]
    ga, gb = g["w_a"], g["w_b"]
    sb_w = ga.shape[1] // 4
    ca = lambda i: ga[:, i * sb_w:(i + 1) * sb_w]
    g_in = jnp.concatenate([ca(0), ca(1), ca(2), gb[:, :sb_w], gb[:, sb_w:2 * sb_w], ca(3),
                            gb[:, 2 * sb_w:2 * sb_w + n_fx]], axis=1)
    rank = g["w_uq"].shape[0]
    g_uq = g["w_uq"].reshape(rank, MLA_HEADS, MLA_PAD_DIM)[:, :, :MLA_QK_DIM].reshape(rank, MLA_HEADS * MLA_QK_DIM)
    g_full = {
        "sf_w_in": g_in[None], "sf_w_o": g["w_o0"][None],
        "mla_w_down": g["w_down"][None, :, :full["mla_w_down"].shape[2]],
        "mla_w_uq": g_uq[None], "mla_w_ukv": g["w_ukv"][None], "mla_w_o": g["w_o1"][None],
        "mlp_w_up": jnp.stack([g["w_up0"], g["w_up1"]]), "mlp_w_down": jnp.stack([g["w_dn0"], g["w_dn1"]]),
    }
    small_g = {
        "ln_mix_g": jnp.concatenate([g["ln_mix0"], g["ln_mix1"]], axis=0),
        "ln_mlp_g": jnp.concatenate([g["ln_mlp0"], g["ln_mlp1"]], axis=0),
        "sf_b_f": g["b_f"], "fox_q_g": g["fox_q_g"], "fox_k_g": g["fox_k_g"],
        "mla_q_a_g": g["q_a_g"], "mla_kv_a_g": g["kv_a_g"],
        "mla_q_g": g["mla_q_g"][:, :MLA_QK_DIM], "mla_k_g": g["mla_k_g"][:, :MLA_QK_DIM],
    }
    return g_full, small_g


def kernel(x, positions, ln_mix_g, ln_mlp_g, sf_w_in, sf_b_f, fox_q_g, fox_k_g, sf_w_o, mla_w_down, mla_q_a_g, mla_kv_a_g, mla_w_uq, mla_w_ukv, mla_q_g, mla_k_g, mla_w_o, mlp_w_up, mlp_w_down, loss_target, m_ln_mix_g, m_ln_mlp_g, m_sf_w_in, m_sf_b_f, m_fox_q_g, m_fox_k_g, m_sf_w_o, m_mla_w_down, m_mla_q_a_g, m_mla_kv_a_g, m_mla_w_uq, m_mla_w_ukv, m_mla_q_g, m_mla_k_g, m_mla_w_o, m_mlp_w_up, m_mlp_w_down, v_ln_mix_g, v_ln_mlp_g, v_sf_w_in, v_sf_b_f, v_fox_q_g, v_fox_k_g, v_sf_w_o, v_mla_w_down, v_mla_q_a_g, v_mla_kv_a_g, v_mla_w_uq, v_mla_w_ukv, v_mla_q_g, v_mla_k_g, v_mla_w_o, v_mlp_w_up, v_mlp_w_down):
    given = dict(locals())
    wts = {n: given[n] for n in ALL_W}
    mom = {n: given["m_" + n] for n in ALL_W}
    var = {n: given["v_" + n] for n in ALL_W}
    me = 4 * lax.axis_index("x") + 2 * lax.axis_index("y") + lax.axis_index("c")
    t_rows, d_model = x.shape[1], x.shape[2]

    big_rows = [_as_rows(wts[n]).shape[0] for n in BIG]
    total_rows = -(-sum(big_rows) // PACK_TILE) * PACK_TILE
    w_packed = _pack_rows([_as_rows(wts[n]) for n in BIG], 0, total_rows, F32)
    gathered = _all_gather(_cast_bf16(w_packed), "gather_weights")
    gains_local = _as_rows(jnp.concatenate([mla_q_a_g, mla_kv_a_g], axis=1))
    gains = _all_gather(gains_local, "gather_gains")
    full, off = {}, 0
    for n, r in zip(BIG, big_rows):
        shard = wts[n].shape
        stack = gathered[:, off:off + r, :].reshape(N_DEV, -1)[:, :math.prod(shard)].reshape((N_DEV,) + shard)
        full[n] = _unshard(stack, SHARD_AXIS[n])
        off += r
    lora_n = mla_q_a_g.shape[1]
    gains_flat = gains.reshape(N_DEV, -1)[:, :2 * lora_n]
    q_a_full = gains_flat[:, :lora_n].reshape(1, -1)
    kv_a_full = gains_flat[:, lora_n:].reshape(1, -1)

    small = dict(ln_mix_g=ln_mix_g, ln_mlp_g=ln_mlp_g, sf_b_f=sf_b_f, fox_q_g=fox_q_g, fox_k_g=fox_k_g,
                 mla_q_a_g=q_a_full, mla_kv_a_g=kv_a_full, mla_q_g=mla_q_g, mla_k_g=mla_k_g)
    w = _model_weights(full, small)
    sq_err, grad_x, g = _local_step(x[0], positions.reshape(t_rows, 1), loss_target[0], w)
    g_full, small_g = _grads_in_weight_layout(g, full)

    g_slots = []
    for n in BIG:
        stack = _shard_stack(g_full[n], SHARD_AXIS[n]).reshape(N_DEV, -1)
        rows = _as_rows(wts[n]).shape[0]
        g_slots.append(jnp.pad(stack, ((0, 0), (0, rows * LANES - stack.shape[1]))).reshape(N_DEV, rows, LANES))
    recv = _all_to_all(_pack_rows(g_slots, 1, total_rows, BF16), "exchange_grads")
    m_packed = _pack_rows([_as_rows(mom[n]) for n in BIG], 0, total_rows, F32)
    v_packed = _pack_rows([_as_rows(var[n]) for n in BIG], 0, total_rows, F32)
    big_out = _adam_big(recv, w_packed, m_packed, v_packed)

    small_parts = [_as_rows(small_g[n], 8) for n in SMALL] + [_as_rows(sq_err, 8)]
    small_rows = sum(p.shape[0] for p in small_parts) + 8
    small_sum = _sum_slots(_all_gather(_pack_rows(small_parts, 0, small_rows, F32), "gather_small_grads"))
    red, off = {}, 0
    for n, p in zip(SMALL + ["loss"], small_parts):
        red[n] = small_sum[off:off + p.shape[0]].reshape(-1)
        off += p.shape[0]
    loss = 0.5 * red["loss"][0] / d_model
    small_grad = {}
    for n in SMALL:
        if n in ("mla_q_a_g", "mla_kv_a_g"):
            small_grad[n] = lax.dynamic_slice(red[n], (me * lora_n,), (lora_n,)).reshape(wts[n].shape)
        else:
            small_grad[n] = red[n][:wts[n].size].reshape(wts[n].shape)
    pack_small = lambda d: jnp.concatenate([_as_rows(d[n], 8) for n in SMALL], axis=0)
    small_out = _adam_small(pack_small(wts), pack_small(small_grad), pack_small(mom), pack_small(var))

    results = {"grad": dict(small_grad), "delta": {}, "new_m": {}, "new_v": {}}
    off = 0
    for n, r in zip(BIG, big_rows):
        for kind, packed in zip(["grad", "delta", "new_m", "new_v"], big_out):
            results[kind][n] = packed[off:off + r].reshape(-1)[:wts[n].size].reshape(wts[n].shape)
        off += r
    off = 0
    for n in SMALL:
        r = _as_rows(wts[n], 8).shape[0]
        for kind, packed in zip(["delta", "new_m", "new_v"], small_out):
            results[kind][n] = packed[off:off + r].reshape(-1)[:wts[n].size].reshape(wts[n].shape)
        off += r
    outs = [loss, grad_x[None]]
    for kind in ["grad", "delta", "new_m", "new_v"]:
        outs += [results[kind][n] for n in ALL_W]
    return tuple(outs)
```

```python
import functools
import math

import jax
import jax.numpy as jnp
import numpy as np
from jax import lax
from jax.experimental import pallas as pl
from jax.experimental.pallas import tpu as pltpu

F32 = jnp.float32
BF16 = jnp.bfloat16

NORM_EPS = 1e-6
ROPE_THETA = 10000.0
HEAD_DIM = 128
ROPE_DIM = 64
MLA_HEADS = 16
MLA_QK_DIM = 192
MLA_PAD_DIM = 256
ADAM_LR, ADAM_B1, ADAM_B2, ADAM_EPS, ADAM_WD, ADAM_STEP = 0.001, 0.9, 0.999, 1e-08, 0.01, 10

N_DEV = 8
LANES = 128
VMEM_LIMIT = 56 * 1024 * 1024
MATMUL_VMEM_BUDGET = 40 * 1024 * 1024
MASKED = -1e30
MESH = pl.DeviceIdType.MESH

NT_DIMS = (((1,), (1,)), ((), ()))
TN_DIMS = (((0,), (0,)), ((), ()))
NN_DIMS = (((1,), (0,)), ((), ()))


def _params(*sem):
    return pltpu.CompilerParams(dimension_semantics=sem, vmem_limit_bytes=VMEM_LIMIT)


def _pick(n, pref):
    best = None
    for t in range(LANES, min(n, pref) + 1, LANES):
        if n % t == 0:
            best = t
    return n if best is None or 2 * best < min(n, pref) else best


def _rows_call(fn, name, row_ins, full_ins, row_outs, acc_outs=(), tile=256):
    row_ins = [r if isinstance(r, tuple) else (r, r.shape[1], 0) for r in row_ins]
    t_rows = row_ins[0][0].shape[0]
    assert t_rows % tile == 0
    n_in = len(row_ins) + len(full_ins)
    n_row_out = len(row_outs)

    def body(*refs):
        res = fn(*[r[...] for r in refs[:n_in]])
        res = res if isinstance(res, tuple) else (res,)
        for ref, val in zip(refs[n_in:n_in + n_row_out], res[:n_row_out]):
            ref[...] = val.astype(ref.dtype)
        acc_refs = refs[n_in + n_row_out:]
        if acc_refs:
            @pl.when(pl.program_id(0) == 0)
            def _():
                for ref in acc_refs:
                    ref[...] = jnp.zeros_like(ref)
            for ref, val in zip(acc_refs, res[n_row_out:]):
                ref[...] += val.astype(ref.dtype)

    in_specs = [pl.BlockSpec((tile, w), functools.partial(lambda i, cb: (i, cb), cb=cb)) for _, w, cb in row_ins]
    in_specs += [pl.BlockSpec(a.shape, lambda i: (0, 0)) for a in full_ins]
    out_specs = [pl.BlockSpec((tile, c), lambda i: (i, 0)) for c, _ in row_outs]
    out_specs += [pl.BlockSpec(s, lambda i: (0, 0)) for s, _ in acc_outs]
    out_shape = [jax.ShapeDtypeStruct((t_rows, c), d) for c, d in row_outs]
    out_shape += [jax.ShapeDtypeStruct(s, d) for s, d in acc_outs]
    outs = pl.pallas_call(
        body, name=name, grid=(t_rows // tile,), in_specs=in_specs, out_specs=out_specs, out_shape=out_shape,
        compiler_params=_params("arbitrary"),
    )(*[r[0] for r in row_ins], *full_ins)
    return outs[0] if len(outs) == 1 else tuple(outs)


def _matmul_tiles(m, n, k, in_bytes, out_bytes):
    tn = n if n <= 1280 else _pick(n, 1024)
    tks = [k] + [k // d for d in (2, 4, 8, 16) if k % (d * LANES) == 0]
    for tk in [t for t in tks if t <= 4096] or [tks[-1]]:
        for tm in (1024, 512, 256):
            if m % tm:
                continue
            acc = 2 * tm * tn * 4 if tk < k else tm * tn * 4
            if 2 * (tm * tk + tk * tn) * in_bytes + 2 * tm * tn * out_bytes + acc <= MATMUL_VMEM_BUDGET:
                return tm, tn, tk
    raise ValueError(f"no matmul tiling for {m}x{n}x{k}")


def _matmul(a, b, form, name, out_dtypes=(F32,), epilogue=None, extras=()):
    if form == "nn":
        (m, k), n = a.shape, b.shape[1]
    elif form == "nt":
        (m, k), n = a.shape, b.shape[0]
    else:
        (k, m), n = a.shape, b.shape[1]
    in_bytes = max(a.dtype.itemsize, b.dtype.itemsize)
    out_bytes = sum(jnp.dtype(d).itemsize for d in out_dtypes) + sum(e.dtype.itemsize for e in extras)
    tm, tn, tk = _matmul_tiles(m, n, k, in_bytes, out_bytes)
    nk = k // tk
    dims = {"nn": NN_DIMS, "nt": NT_DIMS, "tn": TN_DIMS}[form]
    n_extra, n_out = len(extras), len(out_dtypes)

    def body(*refs):
        a_ref, b_ref = refs[0], refs[1]
        extra_refs = refs[2:2 + n_extra]
        out_refs = refs[2 + n_extra:2 + n_extra + n_out]

        def finish(acc):
            vals = (acc,) if epilogue is None else epilogue(acc, *[r[...] for r in extra_refs])
            for ref, val in zip(out_refs, vals):
                ref[...] = val.astype(ref.dtype)

        part = lax.dot_general(a_ref[...].astype(BF16), b_ref[...].astype(BF16), dims, preferred_element_type=F32)
        if nk == 1:
            finish(part)
        else:
            acc_ref = refs[-1]
            kk = pl.program_id(2)

            @pl.when(kk == 0)
            def _():
                acc_ref[...] = part

            @pl.when(kk > 0)
            def _():
                acc_ref[...] += part

            @pl.when(kk == nk - 1)
            def _():
                finish(acc_ref[...])

    a_spec = pl.BlockSpec((tk, tm), lambda i, j, kk: (kk, i)) if form == "tn" else pl.BlockSpec((tm, tk), lambda i, j, kk: (i, kk))
    b_spec = pl.BlockSpec((tn, tk), lambda i, j, kk: (j, kk)) if form == "nt" else pl.BlockSpec((tk, tn), lambda i, j, kk: (kk, j))
    o_spec = pl.BlockSpec((tm, tn), lambda i, j, kk: (i, j))
    outs = pl.pallas_call(
        body, name=name, grid=(m // tm, n // tn, nk), in_specs=[a_spec, b_spec] + [o_spec] * n_extra,
        out_specs=[o_spec] * n_out, out_shape=[jax.ShapeDtypeStruct((m, n), d) for d in out_dtypes],
        scratch_shapes=[pltpu.VMEM((tm, tn), F32)] if nk > 1 else [],
        compiler_params=_params("parallel", "parallel", "arbitrary"),
    )(a, b, *extras)
    return outs[0] if n_out == 1 else tuple(outs)


def _add_residual(acc, res):
    return (acc + res,)


def _log_sigmoid_parts(z):
    return jnp.log1p(jnp.exp(-jnp.abs(z)))


def _rms_fwd(x, g, n=None):
    n = x.shape[-1] if n is None else n
    r = lax.rsqrt(jnp.sum(x * x, axis=-1, keepdims=True) / n + NORM_EPS)
    return x * r * g


def _rms_bwd(x, g, dout, n=None):
    n = x.shape[-1] if n is None else n
    r = lax.rsqrt(jnp.sum(x * x, axis=-1, keepdims=True) / n + NORM_EPS)
    y = x * r
    dg = jnp.sum(dout * y, axis=0, keepdims=True)
    dy = dout * g
    dx = r * (dy - y * (jnp.sum(dy * y, axis=-1, keepdims=True) / n))
    return dx, dg


def _swap_halves(r):
    lane = lax.broadcasted_iota(jnp.int32, r.shape, 1)
    return jnp.where(lane < ROPE_DIM // 2, pltpu.roll(r, LANES - ROPE_DIM // 2, 1), pltpu.roll(r, ROPE_DIM // 2, 1))


def _rope_fwd(r, cos_t, sin_s):
    return r * cos_t + _swap_halves(r) * sin_s


def _rope_bwd(dr, cos_t, sin_s):
    return dr * cos_t + _swap_halves(dr * sin_s)


def _split3(x):
    hi = x.astype(BF16)
    r1 = x - hi.astype(F32)
    mid = r1.astype(BF16)
    lo = (r1 - mid.astype(F32)).astype(BF16)
    return hi, mid, lo


def _mesh_position():
    x, y, c = lax.axis_index("x"), lax.axis_index("y"), lax.axis_index("c")
    return x, y, c, 4 * x + 2 * y + c


def _peer(x, y, c, k):
    bx, by, bc = (k >> 2) & 1, (k >> 1) & 1, k & 1
    px, py, pc = x ^ bx, y ^ by, c ^ bc
    return (px, py, pc), 4 * px + 2 * py + pc


def _gather_steps(x_ref, out_ref, send_sems, recv_sems, local_sem):
    x, y, c, me = _mesh_position()
    sibling = (x, y, 1 - c)
    chips = [(1 - x, y), (x, 1 - y), (1 - x, 1 - y)]

    def slot(px, py, pc):
        return out_ref.at[4 * px + 2 * py + pc]

    def copy(k, blk, to, src=None):
        return pltpu.make_async_remote_copy(
            src_ref=slot(*blk) if src is None else src, dst_ref=slot(*blk), send_sem=send_sems.at[k],
            recv_sem=recv_sems.at[k], device_id=to, device_id_type=MESH)

    mine = pltpu.make_async_copy(x_ref, out_ref.at[me], local_sem)
    first = [copy(0, (x, y, c), sibling, src=x_ref)]
    first += [copy(1 + j, (x, y, c), (*chip, c), src=x_ref) for j, chip in enumerate(chips)]
    passed = [copy(4 + j, (*chip, c), sibling) for j, chip in enumerate(chips)]

    def start():
        mine.start()
        for cp in first:
            cp.start()

    def forward():
        for j, chip in enumerate(chips):
            copy(1 + j, (*chip, c), (x, y, c)).wait_recv()
            passed[j].start()

    def finish():
        copy(0, (x, y, 1 - c), (x, y, c)).wait_recv()
        for j, chip in enumerate(chips):
            copy(4 + j, (*chip, 1 - c), (x, y, c)).wait_recv()
        for cp in first + passed:
            cp.wait_send()
        mine.wait()

    return start, forward, finish


def _all_to_all_steps(g_ref, out_ref, send_sems, recv_sems, local_sem):
    x, y, c, me = _mesh_position()
    mine = pltpu.make_async_copy(g_ref.at[me], out_ref.at[me], local_sem)
    copies = []
    for k in range(1, N_DEV):
        peer, peer_idx = _peer(x, y, c, k)
        copies.append(pltpu.make_async_remote_copy(
            src_ref=g_ref.at[peer_idx], dst_ref=out_ref.at[me], send_sem=send_sems.at[k - 1],
            recv_sem=recv_sems.at[k - 1], device_id=peer, device_id_type=MESH))

    def start():
        mine.start()
        for cp in copies:
            cp.start()

    def finish():
        for k in range(1, N_DEV):
            peer, peer_idx = _peer(x, y, c, k)
            pltpu.make_async_remote_copy(
                src_ref=g_ref.at[me], dst_ref=out_ref.at[peer_idx], send_sem=send_sems.at[k - 1],
                recv_sem=recv_sems.at[k - 1], device_id=peer, device_id_type=MESH).wait_recv()
        for cp in copies:
            cp.wait_send()
        mine.wait()

    return start, None, finish


EXCHANGE_STEPS = {"gather": _gather_steps, "all_to_all": _all_to_all_steps}
EXCHANGE_SCRATCH = [pltpu.SemaphoreType.DMA((7,)), pltpu.SemaphoreType.DMA((7,)), pltpu.SemaphoreType.DMA]
ANY_SPEC = pl.BlockSpec(memory_space=pl.ANY)


def _exchange_out_shape(kind, arr):
    return jax.ShapeDtypeStruct(((N_DEV,) + arr.shape) if kind == "gather" else arr.shape, arr.dtype)


def _exchange(kind, arr, name):
    def body(src_ref, dst_ref, send_sems, recv_sems, local_sem):
        start, forward, finish = EXCHANGE_STEPS[kind](src_ref, dst_ref, send_sems, recv_sems, local_sem)
        start()
        if forward is not None:
            forward()
        finish()

    return pl.pallas_call(body, name=name, out_shape=_exchange_out_shape(kind, arr), in_specs=[ANY_SPEC],
                          out_specs=ANY_SPEC, scratch_shapes=EXCHANGE_SCRATCH)(arr)


def _run_hosted(hosted, src_refs, dst_refs, sem_refs, step, n_steps, when):
    for idx, (kind, _) in enumerate(hosted):
        start, forward, finish = EXCHANGE_STEPS[kind](src_refs[idx], dst_refs[idx], *sem_refs[3 * idx:3 * idx + 3])
        if when == "start":
            pl.when(step == 0)(start)
            if forward is not None:
                pl.when(step == (3 * n_steps) // 4)(forward)
        else:
            pl.when(step == n_steps - 1)(finish)


def _causal_iotas(qi, tq, tk):
    row = qi * tq + lax.broadcasted_iota(jnp.int32, (tq, tk), 0)
    col = lax.broadcasted_iota(jnp.int32, (tq, tk), 1)
    return row, col


def _suffix_matrix(tk, inclusive):
    j = lax.broadcasted_iota(jnp.int32, (2 * tk, tk), 0) % tk
    s = lax.broadcasted_iota(jnp.int32, (2 * tk, tk), 1)
    return jnp.where((j >= s) if inclusive else (j > s), 1.0, 0.0).astype(BF16)


def _suffix_sum(x, mat):
    hi = x.astype(BF16)
    lo = (x - hi.astype(F32)).astype(BF16)
    return lax.dot_general(jnp.concatenate([hi, lo], axis=1), mat, NN_DIMS, preferred_element_type=F32)


def _attn_specs(t_rows, tq, heads, dk, dv, q_off, k_off, v_off):
    q_spec = pl.BlockSpec((tq, dk), lambda h, i: (i, q_off + h))
    k_spec = pl.BlockSpec((t_rows, dk), lambda h, i: (0, k_off + h))
    v_spec = pl.BlockSpec((t_rows, dv), lambda h, i: (0, v_off + h))
    return q_spec, k_spec, v_spec


def _weighted_values(weights, v):
    hi = weights.astype(BF16)
    lo = (weights - hi.astype(F32)).astype(BF16)
    return (lax.dot_general(hi, v, NN_DIMS, preferred_element_type=F32),
            lax.dot_general(lo, v, NN_DIMS, preferred_element_type=F32))


def _attn_fwd(kind, q_arr, k_arr, v_arr, heads, dk, dv, scale, name, q_off=0, k_off=0, v_off=0, fcol=None, frow=None,
              tq=256, hosted=()):
    t_rows = q_arr.shape[0]
    tk = tq
    nq = t_rows // tq
    stick = kind == "stick"
    decay = fcol is not None
    n_in = 5 if decay else 3
    n_out = 2 if stick else 3
    n_host = len(hosted)

    def body(*refs):
        q_ref, k_ref, v_ref = refs[:3]
        fcol_ref, frow_ref = (refs[3], refs[4]) if decay else (None, None)
        base = n_in + n_host
        o_ref, fine_ref = refs[base], refs[base + 1]
        lse_ref = None if stick else refs[base + 2]
        host_args = (hosted, refs[n_in:base], refs[base + n_out:base + n_out + n_host], refs[base + n_out + n_host:],
                     pl.program_id(0) * nq + pl.program_id(1), heads * nq)
        _run_hosted(*host_args, "start")
        qi = pl.program_id(1)
        q = q_ref[...]
        row, col = _causal_iotas(qi, tq, tk)
        n_kb = qi + 1
        zeros_o = jnp.zeros((tq, dv), F32)

        if stick:
            mat = _suffix_matrix(tk, inclusive=False)

            def step(i, carry):
                c, acc, rem = carry
                ks = pl.multiple_of((n_kb - 1 - i) * tk, tk)
                k = k_ref[pl.ds(ks, tk), :]
                v = v_ref[pl.ds(ks, tk), :]
                z = lax.dot_general(q, k, NT_DIMS, preferred_element_type=F32) * scale
                strict = (col + ks) < row
                lg = _log_sigmoid_parts(z)
                lom = jnp.where(strict, jnp.minimum(-z, 0.0) - lg, 0.0)
                log_w = (jnp.minimum(z, 0.0) - lg) + (_suffix_sum(lom, mat) + c)
                w = jnp.where(strict, jnp.exp(log_w), 0.0)
                d_acc, d_rem = _weighted_values(w, v)
                return c + jnp.sum(lom, axis=1, keepdims=True), acc + d_acc, rem + d_rem

            _, acc, rem = lax.fori_loop(0, n_kb, step, (jnp.zeros((tq, 1), F32), zeros_o, zeros_o))
            o_ref[...] = acc.astype(o_ref.dtype)
            fine_ref[...] = acc + rem
        else:
            fc = fcol_ref[...] if decay else None

            def step(kb, carry):
                m, l, acc, rem = carry
                ks = pl.multiple_of(kb * tk, tk)
                k = k_ref[pl.ds(ks, tk), :]
                v = v_ref[pl.ds(ks, tk), :]
                s = lax.dot_general(q, k, NT_DIMS, preferred_element_type=F32) * scale
                if decay:
                    s = (s + fc) - frow_ref[:, pl.ds(ks, tk)]
                s = jnp.where((col + ks) <= row, s, MASKED)
                m_new = jnp.maximum(m, jnp.max(s, axis=1, keepdims=True))
                alpha = jnp.exp(m - m_new)
                p = jnp.exp(s - m_new)
                l = alpha * l + jnp.sum(p, axis=1, keepdims=True)
                d_acc, d_rem = _weighted_values(p, v)
                return m_new, l, alpha * acc + d_acc, alpha * rem + d_rem

            m, l, acc, rem = lax.fori_loop(0, n_kb, step, (jnp.full((tq, 1), MASKED, F32), jnp.zeros((tq, 1), F32),
                                                           zeros_o, zeros_o))
            inv_l = 1.0 / l
            o_ref[...] = (acc * inv_l).astype(o_ref.dtype)
            fine_ref[...] = (acc + rem) * inv_l
            lse_ref[...] = m + jnp.log(l)
        _run_hosted(*host_args, "finish")

    q_spec, k_spec, v_spec = _attn_specs(t_rows, tq, heads, dk, dv, q_off, k_off, v_off)
    stat_spec = pl.BlockSpec((None, tq, 1), lambda h, i: (h, i, 0))
    ins, in_specs = [q_arr, k_arr, v_arr], [q_spec, k_spec, v_spec]
    if decay:
        ins += [fcol, frow]
        in_specs += [stat_spec, pl.BlockSpec((None, 1, t_rows), lambda h, i: (h, 0, 0))]
    o_spec = pl.BlockSpec((tq, dv), lambda h, i: (i, h))
    out_specs = [o_spec, o_spec]
    out_shape = [jax.ShapeDtypeStruct((t_rows, heads * dv), BF16), jax.ShapeDtypeStruct((t_rows, heads * dv), F32)]
    if not stick:
        out_specs.append(stat_spec)
        out_shape.append(jax.ShapeDtypeStruct((heads, t_rows, 1), F32))
    return tuple(pl.pallas_call(
        body, name=name, grid=(heads, nq), in_specs=in_specs + [ANY_SPEC] * n_host,
        out_specs=out_specs + [ANY_SPEC] * n_host,
        out_shape=out_shape + [_exchange_out_shape(kd, arr) for kd, arr in hosted],
        scratch_shapes=EXCHANGE_SCRATCH * n_host,
        compiler_params=_params("arbitrary" if n_host else "parallel", "arbitrary"),
    )(*ins, *[arr for _, arr in hosted]))


def _attn_bwd(kind, q_arr, k_arr, v_arr, o_arr, do_arr, heads, dk, dv, scale, name, q_off=0, k_off=0, v_off=0,
              do_off=0, lse=None, fcol=None, frow=None, tq=256, hosted=()):
    t_rows = q_arr.shape[0]
    tk = tq
    nq = t_rows // tq
    stick = kind == "stick"
    decay = fcol is not None
    n_in = 5 + (0 if stick else 1) + (2 if decay else 0)
    n_out = 5 if decay else 3
    n_host = len(hosted)

    def body(*refs):
        q_ref, k_ref, v_ref, o_ref, do_ref = refs[:5]
        lse_ref = None if stick else refs[5]
        fcol_ref, frow_ref = (refs[6], refs[7]) if decay else (None, None)
        base = n_in + n_host
        dq_ref, dk_ref, dv_ref = refs[base:base + 3]
        dfcol_ref, dfrow_ref = (refs[base + 3], refs[base + 4]) if decay else (None, None)
        host_args = (hosted, refs[n_in:base], refs[base + n_out:base + n_out + n_host], refs[base + n_out + n_host:],
                     pl.program_id(0) * nq + pl.program_id(1), heads * nq)
        _run_hosted(*host_args, "start")
        qi = pl.program_id(1)

        @pl.when(qi == 0)
        def _():
            dk_ref[...] = jnp.zeros_like(dk_ref)
            dv_ref[...] = jnp.zeros_like(dv_ref)
            if decay:
                dfrow_ref[...] = jnp.zeros_like(dfrow_ref)

        q = q_ref[...]
        do = do_ref[...]
        delta = jnp.sum(do.astype(F32) * o_ref[...], axis=1, keepdims=True)
        row, col = _causal_iotas(qi, tq, tk)
        n_kb = qi + 1

        def accumulate(ks, d_logits, weights, k):
            d_logits = (d_logits * scale).astype(BF16)
            dk_ref[pl.ds(ks, tk), :] += lax.dot_general(d_logits, q, TN_DIMS, preferred_element_type=F32)
            dv_ref[pl.ds(ks, tk), :] += lax.dot_general(weights.astype(BF16), do, TN_DIMS, preferred_element_type=F32)
            return lax.dot_general(d_logits, k, NN_DIMS, preferred_element_type=F32)

        if stick:
            mat_ex = _suffix_matrix(tk, inclusive=False)
            mat_in = _suffix_matrix(tk, inclusive=True)

            def step(i, carry):
                c, gs, dq = carry
                ks = pl.multiple_of((n_kb - 1 - i) * tk, tk)
                k = k_ref[pl.ds(ks, tk), :]
                v = v_ref[pl.ds(ks, tk), :]
                z = lax.dot_general(q, k, NT_DIMS, preferred_element_type=F32) * scale
                strict = (col + ks) < row
                lg = _log_sigmoid_parts(z)
                log_beta = jnp.minimum(z, 0.0) - lg
                log_omb = jnp.minimum(-z, 0.0) - lg
                lom = jnp.where(strict, log_omb, 0.0)
                w = jnp.where(strict, jnp.exp(log_beta + (_suffix_sum(lom, mat_ex) + c)), 0.0)
                dw = lax.dot_general(do, v, NT_DIMS, preferred_element_type=F32)
                g = w * dw
                g_before = delta - (gs + _suffix_sum(g, mat_in))
                dz = jnp.where(strict, g * jnp.exp(log_omb) - g_before * jnp.exp(log_beta), 0.0)
                dq = dq + accumulate(ks, dz, w, k)
                return c + jnp.sum(lom, axis=1, keepdims=True), gs + jnp.sum(g, axis=1, keepdims=True), dq

            zero = jnp.zeros((tq, 1), F32)
            _, _, dq = lax.fori_loop(0, n_kb, step, (zero, zero, jnp.zeros((tq, dk), F32)))
        else:
            lse_v = lse_ref[...]
            fc = fcol_ref[...] if decay else None

            def step(kb, carry):
                dq, row_sum = carry
                ks = pl.multiple_of(kb * tk, tk)
                k = k_ref[pl.ds(ks, tk), :]
                v = v_ref[pl.ds(ks, tk), :]
                s = lax.dot_general(q, k, NT_DIMS, preferred_element_type=F32) * scale
                if decay:
                    s = (s + fc) - frow_ref[:, pl.ds(ks, tk)]
                p = jnp.where((col + ks) <= row, jnp.exp(s - lse_v), 0.0)
                dp = lax.dot_general(do, v, NT_DIMS, preferred_element_type=F32)
                ds = p * (dp - delta)
                if decay:
                    dfrow_ref[:, pl.ds(ks, tk)] += jnp.sum(ds, axis=0, keepdims=True)
                    row_sum = row_sum + jnp.sum(ds, axis=1, keepdims=True)
                return dq + accumulate(ks, ds, p, k), row_sum

            dq, row_sum = lax.fori_loop(0, n_kb, step, (jnp.zeros((tq, dk), F32), jnp.zeros((tq, 1), F32)))
            if decay:
                dfcol_ref[...] = row_sum
        dq_ref[...] = dq
        _run_hosted(*host_args, "finish")

    q_spec, k_spec, v_spec = _attn_specs(t_rows, tq, heads, dk, dv, q_off, k_off, v_off)
    stat_spec = pl.BlockSpec((None, tq, 1), lambda h, i: (h, i, 0))
    frow_spec = pl.BlockSpec((None, 1, t_rows), lambda h, i: (h, 0, 0))
    ins = [q_arr, k_arr, v_arr, o_arr, do_arr]
    in_specs = [q_spec, k_spec, v_spec, pl.BlockSpec((tq, dv), lambda h, i: (i, h)),
                pl.BlockSpec((tq, dv), lambda h, i: (i, do_off + h))]
    if not stick:
        ins.append(lse)
        in_specs.append(stat_spec)
    if decay:
        ins += [fcol, frow]
        in_specs += [stat_spec, frow_spec]
    out_specs = [pl.BlockSpec((tq, dk), lambda h, i: (i, h)), pl.BlockSpec((t_rows, dk), lambda h, i: (0, h)),
                 pl.BlockSpec((t_rows, dv), lambda h, i: (0, h))]
    out_shape = [jax.ShapeDtypeStruct((t_rows, heads * dk), F32), jax.ShapeDtypeStruct((t_rows, heads * dk), F32),
                 jax.ShapeDtypeStruct((t_rows, heads * dv), F32)]
    if decay:
        out_specs += [stat_spec, frow_spec]
        out_shape += [jax.ShapeDtypeStruct((heads, t_rows, 1), F32), jax.ShapeDtypeStruct((heads, 1, t_rows), F32)]
    return pl.pallas_call(
        body, name=name, grid=(heads, nq), in_specs=in_specs + [ANY_SPEC] * n_host,
        out_specs=out_specs + [ANY_SPEC] * n_host,
        out_shape=out_shape + [_exchange_out_shape(kd, arr) for kd, arr in hosted],
        scratch_shapes=EXCHANGE_SCRATCH * n_host,
        compiler_params=_params("arbitrary" if n_host else "parallel", "arbitrary"),
    )(*ins, *[arr for _, arr in hosted])


def _prefix_matrix(reverse):
    j = lax.broadcasted_iota(jnp.int32, (LANES, LANES), 0)
    s = lax.broadcasted_iota(jnp.int32, (LANES, LANES), 1)
    return jnp.where((j >= s) if reverse else (j <= s), 1.0, 0.0).astype(BF16)


def _chunk_cumsum(x, mat):
    return sum(lax.dot_general(part, mat, NN_DIMS, preferred_element_type=F32) for part in _split3(x))


def _gate_fwd(logit_t, bias_col):
    heads, t_rows = logit_t.shape

    def body(x_ref, b_ref, out_ref):
        mat = _prefix_matrix(reverse=False)

        def step(ci, carry):
            cs = pl.multiple_of(ci * LANES, LANES)
            pre = x_ref[:, pl.ds(cs, LANES)] + b_ref[...]
            log_f = jnp.minimum(pre, 0.0) - _log_sigmoid_parts(pre)
            out_ref[:, pl.ds(cs, LANES)] = _chunk_cumsum(log_f, mat) + carry
            return carry + jnp.sum(log_f, axis=1, keepdims=True)

        lax.fori_loop(0, t_rows // LANES, step, jnp.zeros((heads, 1), F32))

    return pl.pallas_call(body, name="gate_fwd", out_shape=jax.ShapeDtypeStruct((heads, t_rows), F32),
                          compiler_params=pltpu.CompilerParams(vmem_limit_bytes=VMEM_LIMIT))(logit_t, bias_col)


def _gate_bwd(dcum_t, logit_t, bias_col):
    heads, t_rows = logit_t.shape
    n_chunks = t_rows // LANES

    def body(d_ref, x_ref, b_ref, dx_ref, db_ref):
        mat = _prefix_matrix(reverse=True)

        def step(i, carry):
            tail, db = carry
            cs = pl.multiple_of((n_chunks - 1 - i) * LANES, LANES)
            d = d_ref[:, pl.ds(cs, LANES)]
            d_log_f = _chunk_cumsum(d, mat) + tail
            pre = x_ref[:, pl.ds(cs, LANES)] + b_ref[...]
            e = jnp.exp(-jnp.abs(pre))
            d_pre = d_log_f * (jnp.where(pre >= 0.0, e, 1.0) / (1.0 + e))
            dx_ref[:, pl.ds(cs, LANES)] = d_pre
            return tail + jnp.sum(d, axis=1, keepdims=True), db + jnp.sum(d_pre, axis=1, keepdims=True)

        zero = jnp.zeros((heads, 1), F32)
        _, db = lax.fori_loop(0, n_chunks, step, (zero, zero))
        db_ref[...] = db

    return pl.pallas_call(body, name="gate_bwd",
                          out_shape=(jax.ShapeDtypeStruct((heads, t_rows), F32), jax.ShapeDtypeStruct((heads, 1), F32)),
                          compiler_params=pltpu.CompilerParams(vmem_limit_bytes=VMEM_LIMIT))(dcum_t, logit_t, bias_col)


def _norm_fwd(x, g, name):
    return _rows_call(lambda xv, gv: _rms_fwd(xv, gv), name, [x], [g], [(x.shape[1], BF16)])


def _norm_bwd(x, g, dh, dres, name):
    def fn(xv, dhv, dresv, gv):
        dx, dg = _rms_bwd(xv, gv, dhv)
        dx = dresv + dx
        return dx, dx, dg
    return _rows_call(fn, name, [x, dh, dres], [g], [(x.shape[1], F32), (x.shape[1], BF16)], [((1, x.shape[1]), F32)])


def _loss_fwd_bwd(y, target):
    d_model = y.shape[1]

    def fn(yv, tv):
        err = yv - tv
        dy = err * (1.0 / d_model)
        return dy, dy, jnp.sum(jnp.sum(err * err, axis=1, keepdims=True), axis=0, keepdims=True)
    return _rows_call(fn, "loss", [y, target], [], [(d_model, F32), (d_model, BF16)], [((1, 1), F32)])


def _heads_apply(fn, n_heads, width, *tiles):
    return [fn(*[t[:, h * width:(h + 1) * width] for t in tiles]) for h in range(n_heads)]


def _fox_norm_fwd(pb, gq, gk, heads):
    width = heads * HEAD_DIM

    def fn(qk, gqv, gkv):
        q = jnp.concatenate(_heads_apply(lambda t: _rms_fwd(t, gqv), heads, HEAD_DIM, qk[:, :width]), axis=1)
        k = jnp.concatenate(_heads_apply(lambda t: _rms_fwd(t, gkv), heads, HEAD_DIM, qk[:, width:]), axis=1)
        return q, k
    return _rows_call(fn, "fox_norm_fwd", [(pb, 2 * width, 0)], [gq, gk], [(width, BF16), (width, BF16)])


def _fox_norm_bwd(pb, gq, gk, dq, dk, heads):
    width = heads * HEAD_DIM

    def fn(qk, dqv, dkv, gqv, gkv):
        res_q = _heads_apply(lambda t, d: _rms_bwd(t, gqv, d), heads, HEAD_DIM, qk[:, :width], dqv)
        res_k = _heads_apply(lambda t, d: _rms_bwd(t, gkv, d), heads, HEAD_DIM, qk[:, width:], dkv)
        dqk = jnp.concatenate([r[0] for r in res_q] + [r[0] for r in res_k], axis=1)
        return dqk, sum(r[1] for r in res_q), sum(r[1] for r in res_k)
    return _rows_call(fn, "fox_norm_bwd", [(pb, 2 * width, 0), dq, dk], [gq, gk], [(2 * width, BF16)],
                      [((1, HEAD_DIM), F32), ((1, HEAD_DIM), F32)])


def _lora_norm_fwd(down, gq, gkv, rank):
    def fn(dv, gqv, gkvv):
        return _rms_fwd(dv[:, :rank], gqv), _rms_fwd(dv[:, rank:], gkvv)
    return _rows_call(fn, "lora_norm_fwd", [(down, 2 * rank, 0)], [gq, gkv], [(rank, BF16), (rank, BF16)])


def _lora_norm_bwd(down, gq, gkv, dcq, dckv, dkpe, rank):
    def fn(dv, dcqv, dckvv, dkpev, gqv, gkvv):
        dxq, dgq = _rms_bwd(dv[:, :rank], gqv, dcqv)
        dxkv, dgkv = _rms_bwd(dv[:, rank:], gkvv, dckvv)
        return jnp.concatenate([dxq, dxkv, dkpev], axis=1), dgq, dgkv
    return _rows_call(fn, "lora_norm_bwd", [(down, 2 * rank, 0), dcq, dckv, dkpe], [gq, gkv],
                      [(2 * rank + LANES, BF16)], [((1, rank), F32), ((1, rank), F32)])


def _rope_tables(pos_col, inv_freq, sin_sign):
    def fn(pos, invf, sign):
        ang = pos.astype(F32) * invf
        return jnp.cos(ang) * jnp.abs(sign), jnp.sin(ang) * sign
    return _rows_call(fn, "rope_tables", [pos_col], [inv_freq, sin_sign], [(LANES, F32), (LANES, F32)])


def _mla_prep_fwd(q_raw, kv, down, kpe_block, qg, kg, cos_t, sin_s):
    def fn(qv, kvv, kpe, cosv, sinv, qgv, kgv):
        qs, ks, vs = [], [], []
        for h in range(MLA_HEADS):
            qn = _rms_fwd(qv[:, h * MLA_PAD_DIM:(h + 1) * MLA_PAD_DIM], qgv, MLA_QK_DIM)
            qs += [qn[:, :HEAD_DIM], _rope_fwd(qn[:, HEAD_DIM:], cosv, sinv)]
            k_full = jnp.concatenate([kvv[:, h * MLA_PAD_DIM:h * MLA_PAD_DIM + HEAD_DIM], kpe], axis=1)
            kn = _rms_fwd(k_full, kgv, MLA_QK_DIM)
            ks += [kn[:, :HEAD_DIM], _rope_fwd(kn[:, HEAD_DIM:], cosv, sinv)]
            vs.append(kvv[:, h * MLA_PAD_DIM + HEAD_DIM:(h + 1) * MLA_PAD_DIM])
        return jnp.concatenate(qs, axis=1), jnp.concatenate(ks, axis=1), jnp.concatenate(vs, axis=1)
    wide = MLA_HEADS * MLA_PAD_DIM
    return _rows_call(fn, "mla_prep_fwd", [q_raw, kv, (down, LANES, kpe_block), cos_t, sin_s], [qg, kg],
                      [(wide, BF16), (wide, BF16), (MLA_HEADS * HEAD_DIM, BF16)], tile=128)


def _mla_prep_bwd(q_raw, kv, down, kpe_block, qg, kg, cos_t, sin_s, dq, dk, dv):
    def fn(qv, kvv, kpe, cosv, sinv, dqv, dkv, dvv, qgv, kgv):
        dqs, dkvs = [], []
        dkpe = jnp.zeros_like(kpe)
        dqg = jnp.zeros_like(qgv)
        dkg = jnp.zeros_like(kgv)
        for h in range(MLA_HEADS):
            lo, hi = h * MLA_PAD_DIM, (h + 1) * MLA_PAD_DIM
            dqn = jnp.concatenate([dqv[:, lo:lo + HEAD_DIM], _rope_bwd(dqv[:, lo + HEAD_DIM:hi], cosv, sinv)], axis=1)
            dqh, dg = _rms_bwd(qv[:, lo:hi], qgv, dqn, MLA_QK_DIM)
            dqs.append(dqh)
            dqg = dqg + dg
            k_full = jnp.concatenate([kvv[:, lo:lo + HEAD_DIM], kpe], axis=1)
            dkn = jnp.concatenate([dkv[:, lo:lo + HEAD_DIM], _rope_bwd(dkv[:, lo + HEAD_DIM:hi], cosv, sinv)], axis=1)
            dkh, dg = _rms_bwd(k_full, kgv, dkn, MLA_QK_DIM)
            dkg = dkg + dg
            dkpe = dkpe + dkh[:, HEAD_DIM:]
            dkvs += [dkh[:, :HEAD_DIM], dvv[:, h * HEAD_DIM:(h + 1) * HEAD_DIM]]
        return jnp.concatenate(dqs, axis=1), jnp.concatenate(dkvs, axis=1), dkpe, dqg, dkg
    wide = MLA_HEADS * MLA_PAD_DIM
    return _rows_call(fn, "mla_prep_bwd", [q_raw, kv, (down, LANES, kpe_block), cos_t, sin_s, dq, dk, dv], [qg, kg],
                      [(wide, BF16), (wide, BF16), (LANES, F32)], [((1, MLA_PAD_DIM), F32), ((1, MLA_PAD_DIM), F32)],
                      tile=128)


def _sqrelu_up(acc):
    return acc, jnp.square(jnp.maximum(acc, 0.0))


def _sqrelu_grad(acc, u):
    return (acc * (2.0 * jnp.maximum(u, 0.0)),)


def _mlp_fwd(x, g, w_up, w_down, tag):
    h = _norm_fwd(x, g, f"mlp_norm_fwd{tag}")
    u, a = _matmul(h, w_up, "nn", f"mlp_up{tag}", (F32, BF16), _sqrelu_up)
    return _matmul(a, w_down, "nn", f"mlp_down{tag}", (F32,), _add_residual, (x,)), (h, u, a)


def _mlp_bwd(x, g, w_up, w_down, saved, dy, dy16, tag):
    h, u, a = saved
    dw_down = _matmul(a, dy16, "tn", f"mlp_dwdown{tag}", (BF16,))
    du = _matmul(dy16, w_down, "nt", f"mlp_du{tag}", (BF16,), _sqrelu_grad, (u,))
    dw_up = _matmul(h, du, "tn", f"mlp_dwup{tag}", (BF16,))
    dh = _matmul(du, w_up, "nt", f"mlp_dh{tag}")
    dx, dx16, dg = _norm_bwd(x, g, dh, dy, f"mlp_norm_bwd{tag}")
    return dx, dx16, dg, dw_up, dw_down


def _local_step(x, pos_col, target, w, dist=None):
    w = dict(w)
    hs = w["w_a"].shape[1] // (4 * HEAD_DIM)
    sb_w = hs * HEAD_DIM
    grads, received = {}, {}

    def gather_in(group):
        return [("gather", dist["blocks"][group])] if dist else []

    def exchange_in(group):
        return [("all_to_all", dist["slots_of"](group, grads))] if dist else []

    h0 = _norm_fwd(x, w["ln_mix0"], "mix0_norm_fwd")
    pa = _matmul(h0, w["w_a"], "nn", "in_proj_a", (BF16,))
    pb = _matmul(h0, w["w_b"], "nn", "in_proj_b")
    o_sb, o_sb_fine, *got = _attn_fwd("stick", pa, pa, pa, hs, HEAD_DIM, HEAD_DIM, HEAD_DIM ** -0.5, "stick_fwd",
                                      q_off=0, k_off=hs, v_off=2 * hs, hosted=gather_in("mlp0"))
    if dist:
        w.update(dist["weights_of"]("mlp0", got[0]))
    logit_t = pb[:, 2 * sb_w:2 * sb_w + hs].T
    bias_col = w["b_f"][0, :hs].reshape(hs, 1)
    f_cum = _gate_fwd(logit_t, bias_col)
    f_col, f_row = f_cum[:, :, None], f_cum[:, None, :]
    qf, kf = _fox_norm_fwd(pb, w["fox_q_g"], w["fox_k_g"], hs)
    o_fx, o_fx_fine, lse_fx, *got = _attn_fwd("softmax", qf, kf, pa, hs, HEAD_DIM, HEAD_DIM, HEAD_DIM ** -0.5,
                                              "fox_fwd", v_off=3 * hs, fcol=f_col, frow=f_row,
                                              hosted=gather_in("layer1"))
    if dist:
        w.update(dist["weights_of"]("layer1", got[0]))
    o0 = jnp.concatenate([o_sb, o_fx], axis=1)
    x1 = _matmul(o0, w["w_o0"], "nn", "out_proj0", (F32,), _add_residual, (x,))
    x2, mlp0 = _mlp_fwd(x1, w["ln_mlp0"], w["w_up0"], w["w_dn0"], "0")

    rank = w["w_uq"].shape[0]
    h2 = _norm_fwd(x2, w["ln_mix1"], "mix1_norm_fwd")
    down = _matmul(h2, w["w_down"], "nn", "mla_down")
    cqn, ckvn = _lora_norm_fwd(down, w["q_a_g"], w["kv_a_g"], rank)
    q_raw = _matmul(cqn, w["w_uq"], "nn", "mla_uq")
    kv = _matmul(ckvn, w["w_ukv"], "nn", "mla_ukv")
    cos_t, sin_s = _rope_tables(pos_col, w["inv_freq"], w["sin_sign"])
    kpe_block = 2 * rank // LANES
    qm, km, vm = _mla_prep_fwd(q_raw, kv, down, kpe_block, w["mla_q_g"], w["mla_k_g"], cos_t, sin_s)
    o_m, o_m_fine, lse_m = _attn_fwd("softmax", qm, km, vm, MLA_HEADS, MLA_PAD_DIM, HEAD_DIM, MLA_QK_DIM ** -0.5,
                                     "mla_fwd")
    x3 = _matmul(o_m, w["w_o1"], "nn", "out_proj1", (F32,), _add_residual, (x2,))
    x4, mlp1 = _mlp_fwd(x3, w["ln_mlp1"], w["w_up1"], w["w_dn1"], "1")

    dy, dy16, sq_err = _loss_fwd_bwd(x4, target)

    dx3, dx3_16, grads["ln_mlp1"], grads["w_up1"], grads["w_dn1"] = _mlp_bwd(
        x3, w["ln_mlp1"], w["w_up1"], w["w_dn1"], mlp1, dy, dy16, "1")
    grads["w_o1"] = _matmul(o_m, dx3_16, "tn", "dw_o1", (BF16,))
    do_m = _matmul(dx3_16, w["w_o1"], "nt", "do_mla", (BF16,))
    dqm, dkm, dvm, *got = _attn_bwd("softmax", qm, km, vm, o_m_fine, do_m, MLA_HEADS, MLA_PAD_DIM, HEAD_DIM,
                                    MLA_QK_DIM ** -0.5, "mla_bwd", lse=lse_m, hosted=exchange_in("mlp1"))
    received["mlp1"] = got[0] if dist else None
    dq_raw, dkv, dkpe, grads["mla_q_g"], grads["mla_k_g"] = _mla_prep_bwd(
        q_raw, kv, down, kpe_block, w["mla_q_g"], w["mla_k_g"], cos_t, sin_s, dqm, dkm, dvm)
    grads["w_uq"] = _matmul(cqn, dq_raw, "tn", "dw_uq", (BF16,))
    grads["w_ukv"] = _matmul(ckvn, dkv, "tn", "dw_ukv", (BF16,))
    dcqn = _matmul(dq_raw, w["w_uq"], "nt", "d_cq")
    dckvn = _matmul(dkv, w["w_ukv"], "nt", "d_ckv")
    ddown, grads["q_a_g"], grads["kv_a_g"] = _lora_norm_bwd(down, w["q_a_g"], w["kv_a_g"], dcqn, dckvn, dkpe, rank)
    grads["w_down"] = _matmul(h2, ddown, "tn", "dw_down", (BF16,))
    dh2 = _matmul(ddown, w["w_down"], "nt", "d_h2")
    dx2, dx2_16, grads["ln_mix1"] = _norm_bwd(x2, w["ln_mix1"], dh2, dx3, "mix1_norm_bwd")

    dx1, dx1_16, grads["ln_mlp0"], grads["w_up0"], grads["w_dn0"] = _mlp_bwd(
        x1, w["ln_mlp0"], w["w_up0"], w["w_dn0"], mlp0, dx2, dx2_16, "0")
    grads["w_o0"] = _matmul(o0, dx1_16, "tn", "dw_o0", (BF16,))
    do0 = _matmul(dx1_16, w["w_o0"], "nt", "do_mix0", (BF16,))
    dq_sb, dk_sb, dv_sb, *got = _attn_bwd("stick", pa, pa, pa, o_sb_fine, do0, hs, HEAD_DIM, HEAD_DIM, HEAD_DIM ** -0.5,
                                          "stick_bwd", q_off=0, k_off=hs, v_off=2 * hs, do_off=0,
                                          hosted=exchange_in("mla"))
    received["mla"] = got[0] if dist else None
    dqf, dkf, dv_fx, ds_rows, ds_cols, *got = _attn_bwd(
        "softmax", qf, kf, pa, o_fx_fine, do0, hs, HEAD_DIM, HEAD_DIM, HEAD_DIM ** -0.5, "fox_bwd", v_off=3 * hs,
        do_off=hs, lse=lse_fx, fcol=f_col, frow=f_row, hosted=exchange_in("mlp0"))
    received["mlp0"] = got[0] if dist else None
    dqk_fx, grads["fox_q_g"], grads["fox_k_g"] = _fox_norm_bwd(pb, w["fox_q_g"], w["fox_k_g"], dqf, dkf, hs)
    dlogit_t, db_f = _gate_bwd(ds_rows[:, :, 0] - ds_cols[:, 0, :], logit_t, bias_col)
    grads["b_f"] = db_f.reshape(1, hs)
    dpa = jnp.concatenate([dq_sb.astype(BF16), dk_sb.astype(BF16), dv_sb.astype(BF16), dv_fx.astype(BF16)], axis=1)
    dlogit_pad = jnp.pad(dlogit_t.T.astype(BF16), ((0, 0), (0, pb.shape[1] - 2 * sb_w - hs)))
    dpb = jnp.concatenate([dqk_fx, dlogit_pad], axis=1)
    grads["w_a"] = _matmul(h0, dpa, "tn", "dw_a", (BF16,))
    grads["w_b"] = _matmul(h0, dpb, "tn", "dw_b", (BF16,))
    dh0 = _matmul(dpb, w["w_b"], "nt", "d_h0_b")
    dh0 = _matmul(dpa, w["w_a"], "nt", "d_h0_a", (F32,), _add_residual, (dh0,))
    grad_x, _, grads["ln_mix0"] = _norm_bwd(x, w["ln_mix0"], dh0, dx1, "mix0_norm_bwd")
    return sq_err, grad_x, grads, received


PIECES = {
    "sf_w_in": ("sf_w_in", 0, 1), "sf_w_o": ("sf_w_o", 0, 0), "mla_w_down": ("mla_w_down", 0, 0),
    "mla_w_uq": ("mla_w_uq", 0, 1), "mla_w_ukv": ("mla_w_ukv", 0, 1), "mla_w_o": ("mla_w_o", 0, 0),
    "mlp_w_up0": ("mlp_w_up", 0, 1), "mlp_w_up1": ("mlp_w_up", 1, 1),
    "mlp_w_down0": ("mlp_w_down", 0, 0), "mlp_w_down1": ("mlp_w_down", 1, 0),
}
GROUPS = {
    "mix0": ["sf_w_in", "sf_w_o"], "mlp0": ["mlp_w_up0", "mlp_w_down0"],
    "mla": ["mla_w_down", "mla_w_uq", "mla_w_ukv", "mla_w_o"], "mlp1": ["mlp_w_up1", "mlp_w_down1"],
}
GROUPS["layer1"] = GROUPS["mla"] + GROUPS["mlp1"]
SMALL = ["ln_mix_g", "ln_mlp_g", "sf_b_f", "fox_q_g", "fox_k_g", "mla_q_a_g", "mla_kv_a_g", "mla_q_g", "mla_k_g"]
ALL_W = ["ln_mix_g", "ln_mlp_g", "sf_w_in", "sf_b_f", "fox_q_g", "fox_k_g", "sf_w_o", "mla_w_down", "mla_q_a_g",
         "mla_kv_a_g", "mla_w_uq", "mla_w_ukv", "mla_q_g", "mla_k_g", "mla_w_o", "mlp_w_up", "mlp_w_down"]


def _weights_mix0(full, small):
    w_in = full["sf_w_in"]
    d_model = w_in.shape[0]
    n_fx = small["sf_b_f"].shape[1]
    sb_w = (w_in.shape[1] - n_fx) // 6
    cols = lambda i: w_in[:, i * sb_w:(i + 1) * sb_w]
    w_a = jnp.concatenate([cols(0), cols(1), cols(2), cols(5)], axis=1)
    w_b = jnp.concatenate([cols(3), cols(4), w_in[:, 6 * sb_w:], jnp.zeros((d_model, LANES - n_fx), w_in.dtype)], axis=1)
    half = ROPE_DIM // 2
    inv_freq = ROPE_THETA ** (-jnp.arange(half, dtype=F32) / half)
    zeros64 = jnp.zeros((ROPE_DIM,), F32)
    pad256 = lambda g: jnp.pad(g, ((0, 0), (0, MLA_PAD_DIM - MLA_QK_DIM)))
    pad_lanes = lambda g: jnp.pad(g, ((0, 0), (0, LANES - g.shape[1])))
    return dict(
        ln_mix0=small["ln_mix_g"][0:1], ln_mix1=small["ln_mix_g"][1:2],
        ln_mlp0=small["ln_mlp_g"][0:1], ln_mlp1=small["ln_mlp_g"][1:2],
        w_a=w_a, w_b=w_b, b_f=pad_lanes(small["sf_b_f"]), fox_q_g=small["fox_q_g"], fox_k_g=small["fox_k_g"],
        w_o0=full["sf_w_o"], q_a_g=small["mla_q_a_g"], kv_a_g=small["mla_kv_a_g"],
        mla_q_g=pad256(small["mla_q_g"]), mla_k_g=pad256(small["mla_k_g"]),
        inv_freq=jnp.concatenate([inv_freq, inv_freq, zeros64]).reshape(1, LANES),
        sin_sign=jnp.concatenate([-jnp.ones((half,), F32), jnp.ones((half,), F32), zeros64]).reshape(1, LANES),
    )


def _weights_mlp0(full):
    return dict(w_up0=full["mlp_w_up0"], w_dn0=full["mlp_w_down0"])


def _weights_layer1(full):
    rank = full["mla_w_uq"].shape[0]
    w_uq = full["mla_w_uq"].reshape(rank, MLA_HEADS, MLA_QK_DIM)
    w_uq = jnp.pad(w_uq, ((0, 0), (0, 0), (0, MLA_PAD_DIM - MLA_QK_DIM))).reshape(rank, MLA_HEADS * MLA_PAD_DIM)
    return dict(w_down=jnp.pad(full["mla_w_down"], ((0, 0), (0, LANES - ROPE_DIM))), w_uq=w_uq,
                w_ukv=full["mla_w_ukv"], w_o1=full["mla_w_o"], w_up1=full["mlp_w_up1"], w_dn1=full["mlp_w_down1"])


WEIGHTS_OF = {"mlp0": _weights_mlp0, "layer1": _weights_layer1}


def _piece_grad(g, piece):
    if piece == "sf_w_in":
        n_fx = g["b_f"].shape[1]
        ga, gb = g["w_a"], g["w_b"]
        sb_w = ga.shape[1] // 4
        ca = lambda i: ga[:, i * sb_w:(i + 1) * sb_w]
        return jnp.concatenate([ca(0), ca(1), ca(2), gb[:, :sb_w], gb[:, sb_w:2 * sb_w], ca(3),
                                gb[:, 2 * sb_w:2 * sb_w + n_fx]], axis=1)
    if piece == "mla_w_uq":
        rank = g["w_uq"].shape[0]
        return g["w_uq"].reshape(rank, MLA_HEADS, MLA_PAD_DIM)[:, :, :MLA_QK_DIM].reshape(rank, MLA_HEADS * MLA_QK_DIM)
    if piece == "mla_w_down":
        return g["w_down"][:, :g["w_down"].shape[1] - (LANES - ROPE_DIM)]
    return g[{"sf_w_o": "w_o0", "mla_w_ukv": "w_ukv", "mla_w_o": "w_o1", "mlp_w_up0": "w_up0", "mlp_w_up1": "w_up1",
              "mlp_w_down0": "w_dn0", "mlp_w_down1": "w_dn1"}[piece]]


def _small_grads(g):
    return {
        "ln_mix_g": jnp.concatenate([g["ln_mix0"], g["ln_mix1"]], axis=0),
        "ln_mlp_g": jnp.concatenate([g["ln_mlp0"], g["ln_mlp1"]], axis=0),
        "sf_b_f": g["b_f"], "fox_q_g": g["fox_q_g"], "fox_k_g": g["fox_k_g"],
        "mla_q_a_g": g["q_a_g"], "mla_kv_a_g": g["kv_a_g"],
        "mla_q_g": g["mla_q_g"][:, :MLA_QK_DIM], "mla_k_g": g["mla_k_g"][:, :MLA_QK_DIM],
    }


PACK_TILE = 1024


def _as_rows(a, row_multiple=16):
    flat = a.reshape(-1)
    rows = -(-flat.shape[0] // LANES)
    rows = -(-rows // row_multiple) * row_multiple
    return jnp.pad(flat, (0, rows * LANES - flat.shape[0])).reshape(rows, LANES)


def _pack_rows(parts, axis, dtype, row_multiple=PACK_TILE, spare_rows=0):
    used = sum(p.shape[axis] for p in parts)
    shape = list(parts[0].shape)
    shape[axis] = -(-used // row_multiple) * row_multiple + spare_rows - used
    return jnp.concatenate([p.astype(dtype) for p in parts] + [jnp.ones(shape, dtype)], axis=axis)


def _unshard(stack, axis):
    moved = jnp.moveaxis(stack, 0, axis)
    shape = list(stack.shape[1:])
    shape[axis] *= N_DEV
    return moved.reshape(shape)


def _shard_stack(full, axis):
    shape = list(full.shape)
    shape[axis:axis + 1] = [N_DEV, shape[axis] // N_DEV]
    return jnp.moveaxis(full.reshape(shape), axis, 0)


def _cast_bf16(a, name):
    return _rows_call(lambda v: v, name, [a], [], [(a.shape[1], BF16)], tile=PACK_TILE)


def _adam_math(w, g, m, v):
    m = ADAM_B1 * m + (1.0 - ADAM_B1) * g
    v = ADAM_B2 * v + (1.0 - ADAM_B2) * jnp.square(g)
    m_hat = m / (1.0 - ADAM_B1 ** ADAM_STEP)
    v_hat = v / (1.0 - ADAM_B2 ** ADAM_STEP)
    delta = -ADAM_LR * (m_hat / (jnp.sqrt(v_hat) + ADAM_EPS) + ADAM_WD * w)
    return delta, m, v


def _adam_big(recv, w, m, v, name):
    rows = w.shape[0]

    def body(r_ref, w_ref, m_ref, v_ref, g_ref, d_ref, nm_ref, nv_ref):
        g = r_ref[0].astype(F32)
        for s in range(1, N_DEV):
            g = g + r_ref[s].astype(F32)
        delta, nm, nv = _adam_math(w_ref[...], g, m_ref[...], v_ref[...])
        g_ref[...] = g
        d_ref[...] = delta
        nm_ref[...] = nm
        nv_ref[...] = nv

    spec = pl.BlockSpec((PACK_TILE, LANES), lambda i: (i, 0))
    out = jax.ShapeDtypeStruct((rows, LANES), F32)
    return pl.pallas_call(
        body, name=name, grid=(rows // PACK_TILE,),
        in_specs=[pl.BlockSpec((N_DEV, PACK_TILE, LANES), lambda i: (0, i, 0)), spec, spec, spec],
        out_specs=[spec] * 4, out_shape=[out] * 4, compiler_params=_params("parallel"),
    )(recv, w, m, v)


def _sum_slots(gathered):
    rows = gathered.shape[1]

    def body(r_ref, o_ref):
        acc = r_ref[0]
        for s in range(1, N_DEV):
            acc = acc + r_ref[s]
        o_ref[...] = acc

    return pl.pallas_call(body, name="sum_small", out_shape=jax.ShapeDtypeStruct((rows, LANES), F32))(gathered)


def _adam_small(w, g, m, v):
    def fn(wv, gv, mv, vv):
        return _adam_math(wv, gv, mv, vv)
    return _rows_call(fn, "adam_small", [w, g, m, v], [], [(LANES, F32)] * 3, tile=w.shape[0])


def kernel(x, positions, ln_mix_g, ln_mlp_g, sf_w_in, sf_b_f, fox_q_g, fox_k_g, sf_w_o, mla_w_down, mla_q_a_g, mla_kv_a_g, mla_w_uq, mla_w_ukv, mla_q_g, mla_k_g, mla_w_o, mlp_w_up, mlp_w_down, loss_target, m_ln_mix_g, m_ln_mlp_g, m_sf_w_in, m_sf_b_f, m_fox_q_g, m_fox_k_g, m_sf_w_o, m_mla_w_down, m_mla_q_a_g, m_mla_kv_a_g, m_mla_w_uq, m_mla_w_ukv, m_mla_q_g, m_mla_k_g, m_mla_w_o, m_mlp_w_up, m_mlp_w_down, v_ln_mix_g, v_ln_mlp_g, v_sf_w_in, v_sf_b_f, v_fox_q_g, v_fox_k_g, v_sf_w_o, v_mla_w_down, v_mla_q_a_g, v_mla_kv_a_g, v_mla_w_uq, v_mla_w_ukv, v_mla_q_g, v_mla_k_g, v_mla_w_o, v_mlp_w_up, v_mlp_w_down):
    given = dict(locals())
    wts = {n: given[n] for n in ALL_W}
    mom = {n: given["m_" + n] for n in ALL_W}
    var = {n: given["v_" + n] for n in ALL_W}
    me = 4 * lax.axis_index("x") + 2 * lax.axis_index("y") + lax.axis_index("c")
    t_rows, d_model = x.shape[1], x.shape[2]
    piece_of = lambda d, p: d[PIECES[p][0]][PIECES[p][1]]

    def pack_group(d, group, dtype=F32):
        return _pack_rows([_as_rows(piece_of(d, p)) for p in GROUPS[group]], 0, dtype)

    def unpack_group(packed, group, lead=()):
        out, off = {}, 0
        for p in GROUPS[group]:
            shard = piece_of(wts, p).shape
            rows = _as_rows(piece_of(wts, p)).shape[0]
            out[p] = packed[..., off:off + rows, :].reshape(lead + (-1,))[..., :math.prod(shard)].reshape(lead + shard)
            off += rows
        return out

    def whole_pieces(gathered, group):
        return {p: _unshard(s, PIECES[p][2]) for p, s in unpack_group(gathered, group, (N_DEV,)).items()}

    def slots_of(group, grads):
        parts = []
        for p in GROUPS[group]:
            stack = _shard_stack(_piece_grad(grads, p), PIECES[p][2]).reshape(N_DEV, -1)
            rows = _as_rows(piece_of(wts, p)).shape[0]
            parts.append(jnp.pad(stack, ((0, 0), (0, rows * LANES - stack.shape[1]))).reshape(N_DEV, rows, LANES))
        return _pack_rows(parts, 1, BF16)

    w_packed = {grp: pack_group(wts, grp) for grp in ("mix0", "mlp0", "layer1", "mla", "mlp1")}
    blocks = {grp: _cast_bf16(w_packed[grp], f"cast_{grp}") for grp in ("mix0", "mlp0", "layer1")}
    mix0 = whole_pieces(_exchange("gather", blocks["mix0"], "gather_mix0"), "mix0")
    gains = _exchange("gather", _as_rows(jnp.concatenate([mla_q_a_g, mla_kv_a_g], axis=1)), "gather_gains")
    lora_n = mla_q_a_g.shape[1]
    gains_flat = gains.reshape(N_DEV, -1)[:, :2 * lora_n]
    small = dict(ln_mix_g=ln_mix_g, ln_mlp_g=ln_mlp_g, sf_b_f=sf_b_f, fox_q_g=fox_q_g, fox_k_g=fox_k_g,
                 mla_q_a_g=gains_flat[:, :lora_n].reshape(1, -1), mla_kv_a_g=gains_flat[:, lora_n:].reshape(1, -1),
                 mla_q_g=mla_q_g, mla_k_g=mla_k_g)
    dist = dict(blocks=blocks, slots_of=slots_of,
                weights_of=lambda grp, gathered: WEIGHTS_OF[grp](whole_pieces(gathered, grp)))
    sq_err, grad_x, g, received = _local_step(x[0], positions.reshape(t_rows, 1), loss_target[0],
                                              _weights_mix0(mix0, small), dist)
    received["mix0"] = _exchange("all_to_all", slots_of("mix0", g), "exchange_mix0")

    results = {kind: {} for kind in ("grad", "delta", "new_m", "new_v")}
    for grp in ("mlp1", "mla", "mlp0", "mix0"):
        outs = _adam_big(received[grp], w_packed[grp], pack_group(mom, grp), pack_group(var, grp), f"adam_{grp}")
        for kind, packed in zip(("grad", "delta", "new_m", "new_v"), outs):
            results[kind].update(unpack_group(packed, grp))

    small_g = _small_grads(g)
    small_parts = [_as_rows(small_g[n], 8) for n in SMALL] + [_as_rows(sq_err, 8)]
    small_sum = _sum_slots(_exchange("gather", _pack_rows(small_parts, 0, F32, 8, 8), "gather_small_grads"))
    red, off = {}, 0
    for n, p in zip(SMALL + ["loss"], small_parts):
        red[n] = small_sum[off:off + p.shape[0]].reshape(-1)
        off += p.shape[0]
    loss = 0.5 * red["loss"][0] / d_model
    for n in SMALL:
        if n in ("mla_q_a_g", "mla_kv_a_g"):
            results["grad"][n] = lax.dynamic_slice(red[n], (me * lora_n,), (lora_n,)).reshape(wts[n].shape)
        else:
            results["grad"][n] = red[n][:wts[n].size].reshape(wts[n].shape)
    pack_small = lambda d: jnp.concatenate([_as_rows(d[n], 8) for n in SMALL], axis=0)
    small_out = _adam_small(pack_small(wts), pack_small(results["grad"]), pack_small(mom), pack_small(var))
    off = 0
    for n in SMALL:
        r = _as_rows(wts[n], 8).shape[0]
        for kind, packed in zip(["delta", "new_m", "new_v"], small_out):
            results[kind][n] = packed[off:off + r].reshape(-1)[:wts[n].size].reshape(wts[n].shape)
        off += r

    def whole(kind, n):
        layers = sorted((layer, p) for p, (name, layer, _) in PIECES.items() if name == n)
        return jnp.stack([results[kind][p] for _, p in layers]) if layers else results[kind][n]

    outs = [loss, grad_x[None]]
    for kind in ["grad", "delta", "new_m", "new_v"]:
        outs += [whole(kind, n) for n in ALL_W]
    return tuple(outs)
```

```python
import functools
import math

import jax
import jax.numpy as jnp
import numpy as np
from jax import lax
from jax.experimental import pallas as pl
from jax.experimental.pallas import tpu as pltpu

F32 = jnp.float32
BF16 = jnp.bfloat16

NORM_EPS = 1e-6
ROPE_THETA = 10000.0
HEAD_DIM = 128
ROPE_DIM = 64
MLA_HEADS = 16
MLA_QK_DIM = 192
MLA_PAD_DIM = 256
ADAM_LR, ADAM_B1, ADAM_B2, ADAM_EPS, ADAM_WD, ADAM_STEP = 0.001, 0.9, 0.999, 1e-08, 0.01, 10

N_DEV = 8
LANES = 128
VMEM_LIMIT = 56 * 1024 * 1024
MATMUL_VMEM_BUDGET = 40 * 1024 * 1024
MASKED = -1e30
MESH = pl.DeviceIdType.MESH

NT_DIMS = (((1,), (1,)), ((), ()))
TN_DIMS = (((0,), (0,)), ((), ()))
NN_DIMS = (((1,), (0,)), ((), ()))


def _params(*sem):
    return pltpu.CompilerParams(dimension_semantics=sem, vmem_limit_bytes=VMEM_LIMIT)


def _pick(n, pref):
    best = None
    for t in range(LANES, min(n, pref) + 1, LANES):
        if n % t == 0:
            best = t
    return n if best is None or 2 * best < min(n, pref) else best


def _rows_call(fn, name, row_ins, full_ins, row_outs, acc_outs=(), tile=256):
    row_ins = [r if isinstance(r, tuple) else (r, r.shape[1], 0) for r in row_ins]
    t_rows = row_ins[0][0].shape[0]
    assert t_rows % tile == 0
    n_in = len(row_ins) + len(full_ins)
    n_row_out = len(row_outs)

    def body(*refs):
        res = fn(*[r[...] for r in refs[:n_in]])
        res = res if isinstance(res, tuple) else (res,)
        for ref, val in zip(refs[n_in:n_in + n_row_out], res[:n_row_out]):
            ref[...] = val.astype(ref.dtype)
        acc_refs = refs[n_in + n_row_out:]
        if acc_refs:
            @pl.when(pl.program_id(0) == 0)
            def _():
                for ref in acc_refs:
                    ref[...] = jnp.zeros_like(ref)
            for ref, val in zip(acc_refs, res[n_row_out:]):
                ref[...] += val.astype(ref.dtype)

    in_specs = [pl.BlockSpec((tile, w), functools.partial(lambda i, cb: (i, cb), cb=cb)) for _, w, cb in row_ins]
    in_specs += [pl.BlockSpec(a.shape, lambda i: (0, 0)) for a in full_ins]
    out_specs = [pl.BlockSpec((tile, c), lambda i: (i, 0)) for c, _ in row_outs]
    out_specs += [pl.BlockSpec(s, lambda i: (0, 0)) for s, _ in acc_outs]
    out_shape = [jax.ShapeDtypeStruct((t_rows, c), d) for c, d in row_outs]
    out_shape += [jax.ShapeDtypeStruct(s, d) for s, d in acc_outs]
    outs = pl.pallas_call(
        body, name=name, grid=(t_rows // tile,), in_specs=in_specs, out_specs=out_specs, out_shape=out_shape,
        compiler_params=_params("arbitrary"),
    )(*[r[0] for r in row_ins], *full_ins)
    return outs[0] if len(outs) == 1 else tuple(outs)


def _matmul_tiles(m, n, k, in_bytes, out_bytes):
    tn = n if n <= 1280 else _pick(n, 1024)
    tks = [k] + [k // d for d in (2, 4, 8, 16) if k % (d * LANES) == 0]
    for tk in [t for t in tks if t <= 4096] or [tks[-1]]:
        for tm in (1024, 512, 256):
            if m % tm:
                continue
            acc = 2 * tm * tn * 4 if tk < k else tm * tn * 4
            if 2 * (tm * tk + tk * tn) * in_bytes + 2 * tm * tn * out_bytes + acc <= MATMUL_VMEM_BUDGET:
                return tm, tn, tk
    raise ValueError(f"no matmul tiling for {m}x{n}x{k}")


def _matmul(a, b, form, name, out_dtypes=(F32,), epilogue=None, extras=()):
    if form == "nn":
        (m, k), n = a.shape, b.shape[1]
    elif form == "nt":
        (m, k), n = a.shape, b.shape[0]
    else:
        (k, m), n = a.shape, b.shape[1]
    in_bytes = max(a.dtype.itemsize, b.dtype.itemsize)
    out_bytes = sum(jnp.dtype(d).itemsize for d in out_dtypes) + sum(e.dtype.itemsize for e in extras)
    tm, tn, tk = _matmul_tiles(m, n, k, in_bytes, out_bytes)
    nk = k // tk
    dims = {"nn": NN_DIMS, "nt": NT_DIMS, "tn": TN_DIMS}[form]
    n_extra, n_out = len(extras), len(out_dtypes)

    def body(*refs):
        a_ref, b_ref = refs[0], refs[1]
        extra_refs = refs[2:2 + n_extra]
        out_refs = refs[2 + n_extra:2 + n_extra + n_out]

        def finish(acc):
            vals = (acc,) if epilogue is None else epilogue(acc, *[r[...] for r in extra_refs])
            for ref, val in zip(out_refs, vals):
                ref[...] = val.astype(ref.dtype)

        part = lax.dot_general(a_ref[...].astype(BF16), b_ref[...].astype(BF16), dims, preferred_element_type=F32)
        if nk == 1:
            finish(part)
        else:
            acc_ref = refs[-1]
            kk = pl.program_id(2)

            @pl.when(kk == 0)
            def _():
                acc_ref[...] = part

            @pl.when(kk > 0)
            def _():
                acc_ref[...] += part

            @pl.when(kk == nk - 1)
            def _():
                finish(acc_ref[...])

    a_spec = pl.BlockSpec((tk, tm), lambda i, j, kk: (kk, i)) if form == "tn" else pl.BlockSpec((tm, tk), lambda i, j, kk: (i, kk))
    b_spec = pl.BlockSpec((tn, tk), lambda i, j, kk: (j, kk)) if form == "nt" else pl.BlockSpec((tk, tn), lambda i, j, kk: (kk, j))
    o_spec = pl.BlockSpec((tm, tn), lambda i, j, kk: (i, j))
    outs = pl.pallas_call(
        body, name=name, grid=(m // tm, n // tn, nk), in_specs=[a_spec, b_spec] + [o_spec] * n_extra,
        out_specs=[o_spec] * n_out, out_shape=[jax.ShapeDtypeStruct((m, n), d) for d in out_dtypes],
        scratch_shapes=[pltpu.VMEM((tm, tn), F32)] if nk > 1 else [],
        compiler_params=_params("parallel", "parallel", "arbitrary"),
    )(a, b, *extras)
    return outs[0] if n_out == 1 else tuple(outs)


def _add_residual(acc, res):
    return (acc + res,)


def _log_sigmoid_parts(z):
    return jnp.log1p(jnp.exp(-jnp.abs(z)))


def _rms_fwd(x, g, n=None):
    n = x.shape[-1] if n is None else n
    r = lax.rsqrt(jnp.sum(x * x, axis=-1, keepdims=True) / n + NORM_EPS)
    return x * r * g


def _rms_bwd(x, g, dout, n=None):
    n = x.shape[-1] if n is None else n
    r = lax.rsqrt(jnp.sum(x * x, axis=-1, keepdims=True) / n + NORM_EPS)
    y = x * r
    dg = jnp.sum(dout * y, axis=0, keepdims=True)
    dy = dout * g
    dx = r * (dy - y * (jnp.sum(dy * y, axis=-1, keepdims=True) / n))
    return dx, dg


def _swap_halves(r):
    lane = lax.broadcasted_iota(jnp.int32, r.shape, 1)
    return jnp.where(lane < ROPE_DIM // 2, pltpu.roll(r, LANES - ROPE_DIM // 2, 1), pltpu.roll(r, ROPE_DIM // 2, 1))


def _rope_fwd(r, cos_t, sin_s):
    return r * cos_t + _swap_halves(r) * sin_s


def _rope_bwd(dr, cos_t, sin_s):
    return dr * cos_t + _swap_halves(dr * sin_s)


def _split3(x):
    hi = x.astype(BF16)
    r1 = x - hi.astype(F32)
    mid = r1.astype(BF16)
    lo = (r1 - mid.astype(F32)).astype(BF16)
    return hi, mid, lo


def _mesh_position():
    x, y, c = lax.axis_index("x"), lax.axis_index("y"), lax.axis_index("c")
    return x, y, c, 4 * x + 2 * y + c


def _peer(x, y, c, k):
    bx, by, bc = (k >> 2) & 1, (k >> 1) & 1, k & 1
    px, py, pc = x ^ bx, y ^ by, c ^ bc
    return (px, py, pc), 4 * px + 2 * py + pc


def _gather_steps(x_ref, out_ref, send_sems, recv_sems, local_sem):
    x, y, c, me = _mesh_position()
    sibling = (x, y, 1 - c)
    chips = [(1 - x, y), (x, 1 - y), (1 - x, 1 - y)]

    def slot(px, py, pc):
        return out_ref.at[4 * px + 2 * py + pc]

    def copy(k, blk, to, src=None):
        return pltpu.make_async_remote_copy(
            src_ref=slot(*blk) if src is None else src, dst_ref=slot(*blk), send_sem=send_sems.at[k],
            recv_sem=recv_sems.at[k], device_id=to, device_id_type=MESH)

    mine = pltpu.make_async_copy(x_ref, out_ref.at[me], local_sem)
    first = [copy(0, (x, y, c), sibling, src=x_ref)]
    first += [copy(1 + j, (x, y, c), (*chip, c), src=x_ref) for j, chip in enumerate(chips)]
    passed = [copy(4 + j, (*chip, c), sibling) for j, chip in enumerate(chips)]

    def start():
        mine.start()
        for cp in first:
            cp.start()

    def forward():
        for j, chip in enumerate(chips):
            copy(1 + j, (*chip, c), (x, y, c)).wait_recv()
            passed[j].start()

    def finish():
        copy(0, (x, y, 1 - c), (x, y, c)).wait_recv()
        for j, chip in enumerate(chips):
            copy(4 + j, (*chip, 1 - c), (x, y, c)).wait_recv()
        for cp in first + passed:
            cp.wait_send()
        mine.wait()

    return start, forward, finish


def _all_to_all_steps(g_ref, out_ref, send_sems, recv_sems, local_sem):
    x, y, c, me = _mesh_position()
    mine = pltpu.make_async_copy(g_ref.at[me], out_ref.at[me], local_sem)
    copies = []
    for k in range(1, N_DEV):
        peer, peer_idx = _peer(x, y, c, k)
        copies.append(pltpu.make_async_remote_copy(
            src_ref=g_ref.at[peer_idx], dst_ref=out_ref.at[me], send_sem=send_sems.at[k - 1],
            recv_sem=recv_sems.at[k - 1], device_id=peer, device_id_type=MESH))

    def start():
        mine.start()
        for cp in copies:
            cp.start()

    def finish():
        for k in range(1, N_DEV):
            peer, peer_idx = _peer(x, y, c, k)
            pltpu.make_async_remote_copy(
                src_ref=g_ref.at[me], dst_ref=out_ref.at[peer_idx], send_sem=send_sems.at[k - 1],
                recv_sem=recv_sems.at[k - 1], device_id=peer, device_id_type=MESH).wait_recv()
        for cp in copies:
            cp.wait_send()
        mine.wait()

    return start, None, finish


EXCHANGE_STEPS = {"gather": _gather_steps, "all_to_all": _all_to_all_steps}
EXCHANGE_SCRATCH = [pltpu.SemaphoreType.DMA((7,)), pltpu.SemaphoreType.DMA((7,)), pltpu.SemaphoreType.DMA]
ANY_SPEC = pl.BlockSpec(memory_space=pl.ANY)


def _exchange_out_shape(kind, arr):
    return jax.ShapeDtypeStruct(((N_DEV,) + arr.shape) if kind == "gather" else arr.shape, arr.dtype)


def _exchange(kind, arr, name):
    def body(src_ref, dst_ref, send_sems, recv_sems, local_sem):
        start, forward, finish = EXCHANGE_STEPS[kind](src_ref, dst_ref, send_sems, recv_sems, local_sem)
        start()
        if forward is not None:
            forward()
        finish()

    return pl.pallas_call(body, name=name, out_shape=_exchange_out_shape(kind, arr), in_specs=[ANY_SPEC],
                          out_specs=ANY_SPEC, scratch_shapes=EXCHANGE_SCRATCH)(arr)


def _run_hosted(hosted, src_refs, dst_refs, sem_refs, step, n_steps, when):
    for idx, (kind, _) in enumerate(hosted):
        start, forward, finish = EXCHANGE_STEPS[kind](src_refs[idx], dst_refs[idx], *sem_refs[3 * idx:3 * idx + 3])
        if when == "start":
            pl.when(step == 0)(start)
            if forward is not None:
                pl.when(step == (3 * n_steps) // 4)(forward)
        else:
            pl.when(step == n_steps - 1)(finish)


def _causal_iotas(qi, tq, tk):
    row = qi * tq + lax.broadcasted_iota(jnp.int32, (tq, tk), 0)
    col = lax.broadcasted_iota(jnp.int32, (tq, tk), 1)
    return row, col


def _suffix_matrix(tk, inclusive):
    j = lax.broadcasted_iota(jnp.int32, (2 * tk, tk), 0) % tk
    s = lax.broadcasted_iota(jnp.int32, (2 * tk, tk), 1)
    return jnp.where((j >= s) if inclusive else (j > s), 1.0, 0.0).astype(BF16)


def _suffix_sum(x, mat):
    hi = x.astype(BF16)
    lo = (x - hi.astype(F32)).astype(BF16)
    return lax.dot_general(jnp.concatenate([hi, lo], axis=1), mat, NN_DIMS, preferred_element_type=F32)


def _attn_specs(t_rows, tq, heads, dk, dv, q_off, k_off, v_off):
    q_spec = pl.BlockSpec((tq, dk), lambda h, i: (i, q_off + h))
    kt_spec = pl.BlockSpec((dk, t_rows), lambda h, i: (k_off + h, 0))
    v_spec = pl.BlockSpec((t_rows, dv), lambda h, i: (0, v_off + h))
    return q_spec, kt_spec, v_spec


def _split_weights(weights):
    hi = weights.astype(BF16)
    return hi, (weights - hi.astype(F32)).astype(BF16)


def _weighted_values(split, v):
    return (lax.dot_general(split[0], v, NN_DIMS, preferred_element_type=F32),
            lax.dot_general(split[1], v, NN_DIMS, preferred_element_type=F32))


def _attn_fwd(kind, q_arr, kt_arr, v_arr, heads, dk, dv, scale, name, q_off=0, k_off=0, v_off=0, fcol=None, frow=None,
              tq=256, hosted=()):
    t_rows = q_arr.shape[0]
    tk = tq
    nq = t_rows // tq
    stick = kind == "stick"
    decay = fcol is not None
    n_in = 5 if decay else 3
    n_out = 2 if stick else 3
    n_host = len(hosted)

    def body(*refs):
        q_ref, kt_ref, v_ref = refs[:3]
        fcol_ref, frow_ref = (refs[3], refs[4]) if decay else (None, None)
        base = n_in + n_host
        o_ref, fine_ref = refs[base], refs[base + 1]
        lse_ref = None if stick else refs[base + 2]
        host_args = (hosted, refs[n_in:base], refs[base + n_out:base + n_out + n_host], refs[base + n_out + n_host:],
                     pl.program_id(0) * nq + pl.program_id(1), heads * nq)
        _run_hosted(*host_args, "start")
        qi = pl.program_id(1)
        q = q_ref[...]
        row, col = _causal_iotas(qi, tq, tk)
        n_kb = qi + 1
        zeros_o = jnp.zeros((tq, dv), F32)

        no_weights = (jnp.zeros((tq, tk), BF16), jnp.zeros((tq, tk), BF16))

        def raw_logits(kb):
            return lax.dot_general(q, kt_ref[:, pl.ds(pl.multiple_of(kb * tk, tk), tk)], NN_DIMS,
                                   preferred_element_type=F32)

        def values(kb):
            return v_ref[pl.ds(pl.multiple_of(kb * tk, tk), tk), :]

        if stick:
            mat = _suffix_matrix(tk, inclusive=False)

            def step(i, carry):
                c, acc, rem, raw, prev = carry
                raw_next = raw_logits(jnp.maximum(n_kb - 2 - i, 0))
                d_acc, d_rem = _weighted_values(prev, values(jnp.minimum(n_kb - i, n_kb - 1)))
                ks = pl.multiple_of((n_kb - 1 - i) * tk, tk)
                z = raw * scale
                strict = (col + ks) < row
                lg = _log_sigmoid_parts(z)
                lom = jnp.where(strict, jnp.minimum(-z, 0.0) - lg, 0.0)
                log_w = (jnp.minimum(z, 0.0) - lg) + (_suffix_sum(lom, mat) + c)
                w = jnp.where(strict, jnp.exp(log_w), 0.0)
                return (c + jnp.sum(lom, axis=1, keepdims=True), acc + d_acc, rem + d_rem, raw_next,
                        _split_weights(w))

            _, acc, rem, _, last = lax.fori_loop(0, n_kb, step, (jnp.zeros((tq, 1), F32), zeros_o, zeros_o,
                                                                 raw_logits(n_kb - 1), no_weights))
            d_acc, d_rem = _weighted_values(last, values(0))
            acc, rem = acc + d_acc, rem + d_rem
            o_ref[...] = acc.astype(o_ref.dtype)
            fine_ref[...] = acc + rem
        else:
            fc = fcol_ref[...] if decay else None

            def step(kb, carry):
                m, l, acc, rem, raw, prev = carry
                raw_next = raw_logits(jnp.minimum(kb + 1, n_kb - 1))
                d_acc, d_rem = _weighted_values(prev, values(jnp.maximum(kb - 1, 0)))
                ks = pl.multiple_of(kb * tk, tk)
                s = raw * scale
                if decay:
                    s = (s + fc) - frow_ref[:, pl.ds(ks, tk)]
                s = jnp.where((col + ks) <= row, s, MASKED)
                m_new = jnp.maximum(m, jnp.max(s, axis=1, keepdims=True))
                alpha = jnp.exp(m - m_new)
                p = jnp.exp(s - m_new)
                l = alpha * l + jnp.sum(p, axis=1, keepdims=True)
                return m_new, l, alpha * (acc + d_acc), alpha * (rem + d_rem), raw_next, _split_weights(p)

            m, l, acc, rem, _, last = lax.fori_loop(
                0, n_kb, step, (jnp.full((tq, 1), MASKED, F32), jnp.zeros((tq, 1), F32), zeros_o, zeros_o,
                                raw_logits(0), no_weights))
            d_acc, d_rem = _weighted_values(last, values(n_kb - 1))
            acc, rem = acc + d_acc, rem + d_rem
            inv_l = 1.0 / l
            o_ref[...] = (acc * inv_l).astype(o_ref.dtype)
            fine_ref[...] = (acc + rem) * inv_l
            lse_ref[...] = m + jnp.log(l)
        _run_hosted(*host_args, "finish")

    q_spec, k_spec, v_spec = _attn_specs(t_rows, tq, heads, dk, dv, q_off, k_off, v_off)
    stat_spec = pl.BlockSpec((None, tq, 1), lambda h, i: (h, i, 0))
    ins, in_specs = [q_arr, kt_arr, v_arr], [q_spec, k_spec, v_spec]
    if decay:
        ins += [fcol, frow]
        in_specs += [stat_spec, pl.BlockSpec((None, 1, t_rows), lambda h, i: (h, 0, 0))]
    o_spec = pl.BlockSpec((tq, dv), lambda h, i: (i, h))
    out_specs = [o_spec, o_spec]
    out_shape = [jax.ShapeDtypeStruct((t_rows, heads * dv), BF16), jax.ShapeDtypeStruct((t_rows, heads * dv), F32)]
    if not stick:
        out_specs.append(stat_spec)
        out_shape.append(jax.ShapeDtypeStruct((heads, t_rows, 1), F32))
    return tuple(pl.pallas_call(
        body, name=name, grid=(heads, nq), in_specs=in_specs + [ANY_SPEC] * n_host,
        out_specs=out_specs + [ANY_SPEC] * n_host,
        out_shape=out_shape + [_exchange_out_shape(kd, arr) for kd, arr in hosted],
        scratch_shapes=EXCHANGE_SCRATCH * n_host,
        compiler_params=_params("arbitrary" if n_host else "parallel", "arbitrary"),
    )(*ins, *[arr for _, arr in hosted]))


def _attn_bwd(kind, q_arr, k_arr, kt_arr, vt_arr, o_arr, do_arr, heads, dk, dv, scale, name, q_off=0, k_off=0, kt_off=0,
              vt_off=0, do_off=0, lse=None, fcol=None, frow=None, tq=256, hosted=()):
    t_rows = q_arr.shape[0]
    tk = tq
    nq = t_rows // tq
    stick = kind == "stick"
    decay = fcol is not None
    n_in = 6 + (0 if stick else 1) + (2 if decay else 0)
    n_out = 5 if decay else 3
    n_host = len(hosted)

    def body(*refs):
        q_ref, k_ref, kt_ref, vt_ref, o_ref, do_ref = refs[:6]
        lse_ref = None if stick else refs[6]
        fcol_ref, frow_ref = (refs[7], refs[8]) if decay else (None, None)
        base = n_in + n_host
        dq_ref, dk_ref, dv_ref = refs[base:base + 3]
        dfcol_ref, dfrow_ref = (refs[base + 3], refs[base + 4]) if decay else (None, None)
        host_args = (hosted, refs[n_in:base], refs[base + n_out:base + n_out + n_host], refs[base + n_out + n_host:],
                     pl.program_id(0) * nq + pl.program_id(1), heads * nq)
        _run_hosted(*host_args, "start")
        qi = pl.program_id(1)

        @pl.when(qi == 0)
        def _():
            dk_ref[...] = jnp.zeros_like(dk_ref)
            dv_ref[...] = jnp.zeros_like(dv_ref)
            if decay:
                dfrow_ref[...] = jnp.zeros_like(dfrow_ref)

        q = q_ref[...]
        do = do_ref[...]
        delta = jnp.sum(do.astype(F32) * o_ref[...], axis=1, keepdims=True)
        row, col = _causal_iotas(qi, tq, tk)
        n_kb = qi + 1

        no_pair = (jnp.zeros((tq, tk), BF16), jnp.zeros((tq, tk), BF16))

        def accumulate(kb, pair):
            at = pl.ds(pl.multiple_of(kb * tk, tk), tk)
            dk_ref[at, :] += lax.dot_general(pair[0], q, TN_DIMS, preferred_element_type=F32)
            dv_ref[at, :] += lax.dot_general(pair[1], do, TN_DIMS, preferred_element_type=F32)
            return lax.dot_general(pair[0], k_ref[at, :], NN_DIMS, preferred_element_type=F32)

        def raw_products(kb):
            at = pl.ds(pl.multiple_of(kb * tk, tk), tk)
            return (lax.dot_general(q, kt_ref[:, at], NN_DIMS, preferred_element_type=F32),
                    lax.dot_general(do, vt_ref[:, at], NN_DIMS, preferred_element_type=F32))

        if stick:
            mat_ex = _suffix_matrix(tk, inclusive=False)
            mat_in = _suffix_matrix(tk, inclusive=True)

            def step(i, carry):
                c, gs, dq, (raw, dw), prev = carry
                raw_next = raw_products(jnp.maximum(n_kb - 2 - i, 0))
                dq = dq + accumulate(jnp.minimum(n_kb - i, n_kb - 1), prev)
                ks = pl.multiple_of((n_kb - 1 - i) * tk, tk)
                z = raw * scale
                strict = (col + ks) < row
                lg = _log_sigmoid_parts(z)
                log_beta = jnp.minimum(z, 0.0) - lg
                log_omb = jnp.minimum(-z, 0.0) - lg
                lom = jnp.where(strict, log_omb, 0.0)
                w = jnp.where(strict, jnp.exp(log_beta + (_suffix_sum(lom, mat_ex) + c)), 0.0)
                g = w * dw
                g_before = delta - (gs + _suffix_sum(g, mat_in))
                dz = jnp.where(strict, g * jnp.exp(log_omb) - g_before * jnp.exp(log_beta), 0.0)
                return (c + jnp.sum(lom, axis=1, keepdims=True), gs + jnp.sum(g, axis=1, keepdims=True), dq,
                        raw_next, ((dz * scale).astype(BF16), w.astype(BF16)))

            zero = jnp.zeros((tq, 1), F32)
            _, _, dq, _, last = lax.fori_loop(0, n_kb, step, (zero, zero, jnp.zeros((tq, dk), F32),
                                                              raw_products(n_kb - 1), no_pair))
            dq = dq + accumulate(0, last)
        else:
            lse_v = lse_ref[...]
            fc = fcol_ref[...] if decay else None

            def step(kb, carry):
                dq, row_sum, (raw, dp), prev = carry
                raw_next = raw_products(jnp.minimum(kb + 1, n_kb - 1))
                dq = dq + accumulate(jnp.maximum(kb - 1, 0), prev)
                ks = pl.multiple_of(kb * tk, tk)
                s = raw * scale
                if decay:
                    s = (s + fc) - frow_ref[:, pl.ds(ks, tk)]
                p = jnp.where((col + ks) <= row, jnp.exp(s - lse_v), 0.0)
                ds = p * (dp - delta)
                if decay:
                    dfrow_ref[:, pl.ds(ks, tk)] += jnp.sum(ds, axis=0, keepdims=True)
                    row_sum = row_sum + jnp.sum(ds, axis=1, keepdims=True)
                return dq, row_sum, raw_next, ((ds * scale).astype(BF16), p.astype(BF16))

            dq, row_sum, _, last = lax.fori_loop(0, n_kb, step, (jnp.zeros((tq, dk), F32), jnp.zeros((tq, 1), F32),
                                                                raw_products(0), no_pair))
            dq = dq + accumulate(n_kb - 1, last)
            if decay:
                dfcol_ref[...] = row_sum
        dq_ref[...] = dq
        _run_hosted(*host_args, "finish")

    q_spec, kt_spec, _ = _attn_specs(t_rows, tq, heads, dk, dv, q_off, kt_off, 0)
    stat_spec = pl.BlockSpec((None, tq, 1), lambda h, i: (h, i, 0))
    frow_spec = pl.BlockSpec((None, 1, t_rows), lambda h, i: (h, 0, 0))
    ins = [q_arr, k_arr, kt_arr, vt_arr, o_arr, do_arr]
    in_specs = [q_spec, pl.BlockSpec((t_rows, dk), lambda h, i: (0, k_off + h)), kt_spec,
                pl.BlockSpec((dv, t_rows), lambda h, i: (vt_off + h, 0)), pl.BlockSpec((tq, dv), lambda h, i: (i, h)),
                pl.BlockSpec((tq, dv), lambda h, i: (i, do_off + h))]
    if not stick:
        ins.append(lse)
        in_specs.append(stat_spec)
    if decay:
        ins += [fcol, frow]
        in_specs += [stat_spec, frow_spec]
    out_specs = [pl.BlockSpec((tq, dk), lambda h, i: (i, h)), pl.BlockSpec((t_rows, dk), lambda h, i: (0, h)),
                 pl.BlockSpec((t_rows, dv), lambda h, i: (0, h))]
    out_shape = [jax.ShapeDtypeStruct((t_rows, heads * dk), F32), jax.ShapeDtypeStruct((t_rows, heads * dk), F32),
                 jax.ShapeDtypeStruct((t_rows, heads * dv), F32)]
    if decay:
        out_specs += [stat_spec, frow_spec]
        out_shape += [jax.ShapeDtypeStruct((heads, t_rows, 1), F32), jax.ShapeDtypeStruct((heads, 1, t_rows), F32)]
    return pl.pallas_call(
        body, name=name, grid=(heads, nq), in_specs=in_specs + [ANY_SPEC] * n_host,
        out_specs=out_specs + [ANY_SPEC] * n_host,
        out_shape=out_shape + [_exchange_out_shape(kd, arr) for kd, arr in hosted],
        scratch_shapes=EXCHANGE_SCRATCH * n_host,
        compiler_params=_params("arbitrary" if n_host else "parallel", "arbitrary"),
    )(*ins, *[arr for _, arr in hosted])


def _prefix_matrix(reverse):
    j = lax.broadcasted_iota(jnp.int32, (LANES, LANES), 0)
    s = lax.broadcasted_iota(jnp.int32, (LANES, LANES), 1)
    return jnp.where((j >= s) if reverse else (j <= s), 1.0, 0.0).astype(BF16)


def _chunk_cumsum(x, mat):
    return sum(lax.dot_general(part, mat, NN_DIMS, preferred_element_type=F32) for part in _split3(x))


def _gate_fwd(logit_t, bias_col):
    heads, t_rows = logit_t.shape

    def body(x_ref, b_ref, out_ref):
        mat = _prefix_matrix(reverse=False)

        def step(ci, carry):
            cs = pl.multiple_of(ci * LANES, LANES)
            pre = x_ref[:, pl.ds(cs, LANES)] + b_ref[...]
            log_f = jnp.minimum(pre, 0.0) - _log_sigmoid_parts(pre)
            out_ref[:, pl.ds(cs, LANES)] = _chunk_cumsum(log_f, mat) + carry
            return carry + jnp.sum(log_f, axis=1, keepdims=True)

        lax.fori_loop(0, t_rows // LANES, step, jnp.zeros((heads, 1), F32))

    return pl.pallas_call(body, name="gate_fwd", out_shape=jax.ShapeDtypeStruct((heads, t_rows), F32),
                          compiler_params=pltpu.CompilerParams(vmem_limit_bytes=VMEM_LIMIT))(logit_t, bias_col)


def _gate_bwd(dcum_t, logit_t, bias_col):
    heads, t_rows = logit_t.shape
    n_chunks = t_rows // LANES

    def body(d_ref, x_ref, b_ref, dx_ref, db_ref):
        mat = _prefix_matrix(reverse=True)

        def step(i, carry):
            tail, db = carry
            cs = pl.multiple_of((n_chunks - 1 - i) * LANES, LANES)
            d = d_ref[:, pl.ds(cs, LANES)]
            d_log_f = _chunk_cumsum(d, mat) + tail
            pre = x_ref[:, pl.ds(cs, LANES)] + b_ref[...]
            e = jnp.exp(-jnp.abs(pre))
            d_pre = d_log_f * (jnp.where(pre >= 0.0, e, 1.0) / (1.0 + e))
            dx_ref[:, pl.ds(cs, LANES)] = d_pre
            return tail + jnp.sum(d, axis=1, keepdims=True), db + jnp.sum(d_pre, axis=1, keepdims=True)

        zero = jnp.zeros((heads, 1), F32)
        _, db = lax.fori_loop(0, n_chunks, step, (zero, zero))
        db_ref[...] = db

    return pl.pallas_call(body, name="gate_bwd",
                          out_shape=(jax.ShapeDtypeStruct((heads, t_rows), F32), jax.ShapeDtypeStruct((heads, 1), F32)),
                          compiler_params=pltpu.CompilerParams(vmem_limit_bytes=VMEM_LIMIT))(dcum_t, logit_t, bias_col)


def _norm_fwd(x, g, name):
    return _rows_call(lambda xv, gv: _rms_fwd(xv, gv), name, [x], [g], [(x.shape[1], BF16)])


def _norm_bwd(x, g, dh, dres, name):
    def fn(xv, dhv, dresv, gv):
        dx, dg = _rms_bwd(xv, gv, dhv)
        dx = dresv + dx
        return dx, dx, dg
    return _rows_call(fn, name, [x, dh, dres], [g], [(x.shape[1], F32), (x.shape[1], BF16)], [((1, x.shape[1]), F32)])


def _loss_fwd_bwd(y, target):
    d_model = y.shape[1]

    def fn(yv, tv):
        err = yv - tv
        dy = err * (1.0 / d_model)
        return dy, dy, jnp.sum(jnp.sum(err * err, axis=1, keepdims=True), axis=0, keepdims=True)
    return _rows_call(fn, "loss", [y, target], [], [(d_model, F32), (d_model, BF16)], [((1, 1), F32)])


def _heads_apply(fn, n_heads, width, *tiles):
    return [fn(*[t[:, h * width:(h + 1) * width] for t in tiles]) for h in range(n_heads)]


def _fox_norm_fwd(pb, gq, gk, heads):
    width = heads * HEAD_DIM

    def fn(qk, gqv, gkv):
        q = jnp.concatenate(_heads_apply(lambda t: _rms_fwd(t, gqv), heads, HEAD_DIM, qk[:, :width]), axis=1)
        k = jnp.concatenate(_heads_apply(lambda t: _rms_fwd(t, gkv), heads, HEAD_DIM, qk[:, width:]), axis=1)
        return q, k
    return _rows_call(fn, "fox_norm_fwd", [(pb, 2 * width, 0)], [gq, gk], [(width, BF16), (width, BF16)])


def _fox_norm_bwd(pb, gq, gk, dq, dk, heads):
    width = heads * HEAD_DIM

    def fn(qk, dqv, dkv, gqv, gkv):
        res_q = _heads_apply(lambda t, d: _rms_bwd(t, gqv, d), heads, HEAD_DIM, qk[:, :width], dqv)
        res_k = _heads_apply(lambda t, d: _rms_bwd(t, gkv, d), heads, HEAD_DIM, qk[:, width:], dkv)
        dqk = jnp.concatenate([r[0] for r in res_q] + [r[0] for r in res_k], axis=1)
        return dqk, sum(r[1] for r in res_q), sum(r[1] for r in res_k)
    return _rows_call(fn, "fox_norm_bwd", [(pb, 2 * width, 0), dq, dk], [gq, gk], [(2 * width, BF16)],
                      [((1, HEAD_DIM), F32), ((1, HEAD_DIM), F32)])


def _lora_norm_fwd(down, gq, gkv, rank):
    def fn(dv, gqv, gkvv):
        return _rms_fwd(dv[:, :rank], gqv), _rms_fwd(dv[:, rank:], gkvv)
    return _rows_call(fn, "lora_norm_fwd", [(down, 2 * rank, 0)], [gq, gkv], [(rank, BF16), (rank, BF16)])


def _lora_norm_bwd(down, gq, gkv, dcq, dckv, dkpe, rank):
    def fn(dv, dcqv, dckvv, dkpev, gqv, gkvv):
        dxq, dgq = _rms_bwd(dv[:, :rank], gqv, dcqv)
        dxkv, dgkv = _rms_bwd(dv[:, rank:], gkvv, dckvv)
        return jnp.concatenate([dxq, dxkv, dkpev], axis=1), dgq, dgkv
    return _rows_call(fn, "lora_norm_bwd", [(down, 2 * rank, 0), dcq, dckv, dkpe], [gq, gkv],
                      [(2 * rank + LANES, BF16)], [((1, rank), F32), ((1, rank), F32)])


def _rope_tables(pos_col, inv_freq, sin_sign):
    def fn(pos, invf, sign):
        ang = pos.astype(F32) * invf
        return jnp.cos(ang) * jnp.abs(sign), jnp.sin(ang) * sign
    return _rows_call(fn, "rope_tables", [pos_col], [inv_freq, sin_sign], [(LANES, F32), (LANES, F32)])


def _mla_prep_fwd(q_raw, kv, down, kpe_block, qg, kg, cos_t, sin_s):
    def fn(qv, kvv, kpe, cosv, sinv, qgv, kgv):
        qs, ks, vs = [], [], []
        for h in range(MLA_HEADS):
            qn = _rms_fwd(qv[:, h * MLA_PAD_DIM:(h + 1) * MLA_PAD_DIM], qgv, MLA_QK_DIM)
            qs += [qn[:, :HEAD_DIM], _rope_fwd(qn[:, HEAD_DIM:], cosv, sinv)]
            k_full = jnp.concatenate([kvv[:, h * MLA_PAD_DIM:h * MLA_PAD_DIM + HEAD_DIM], kpe], axis=1)
            kn = _rms_fwd(k_full, kgv, MLA_QK_DIM)
            ks += [kn[:, :HEAD_DIM], _rope_fwd(kn[:, HEAD_DIM:], cosv, sinv)]
            vs.append(kvv[:, h * MLA_PAD_DIM + HEAD_DIM:(h + 1) * MLA_PAD_DIM])
        return jnp.concatenate(qs, axis=1), jnp.concatenate(ks, axis=1), jnp.concatenate(vs, axis=1)
    wide = MLA_HEADS * MLA_PAD_DIM
    return _rows_call(fn, "mla_prep_fwd", [q_raw, kv, (down, LANES, kpe_block), cos_t, sin_s], [qg, kg],
                      [(wide, BF16), (wide, BF16), (MLA_HEADS * HEAD_DIM, BF16)], tile=128)


def _mla_prep_bwd(q_raw, kv, down, kpe_block, qg, kg, cos_t, sin_s, dq, dk, dv):
    def fn(qv, kvv, kpe, cosv, sinv, dqv, dkv, dvv, qgv, kgv):
        dqs, dkvs = [], []
        dkpe = jnp.zeros_like(kpe)
        dqg = jnp.zeros_like(qgv)
        dkg = jnp.zeros_like(kgv)
        for h in range(MLA_HEADS):
            lo, hi = h * MLA_PAD_DIM, (h + 1) * MLA_PAD_DIM
            dqn = jnp.concatenate([dqv[:, lo:lo + HEAD_DIM], _rope_bwd(dqv[:, lo + HEAD_DIM:hi], cosv, sinv)], axis=1)
            dqh, dg = _rms_bwd(qv[:, lo:hi], qgv, dqn, MLA_QK_DIM)
            dqs.append(dqh)
            dqg = dqg + dg
            k_full = jnp.concatenate([kvv[:, lo:lo + HEAD_DIM], kpe], axis=1)
            dkn = jnp.concatenate([dkv[:, lo:lo + HEAD_DIM], _rope_bwd(dkv[:, lo + HEAD_DIM:hi], cosv, sinv)], axis=1)
            dkh, dg = _rms_bwd(k_full, kgv, dkn, MLA_QK_DIM)
            dkg = dkg + dg
            dkpe = dkpe + dkh[:, HEAD_DIM:]
            dkvs += [dkh[:, :HEAD_DIM], dvv[:, h * HEAD_DIM:(h + 1) * HEAD_DIM]]
        return jnp.concatenate(dqs, axis=1), jnp.concatenate(dkvs, axis=1), dkpe, dqg, dkg
    wide = MLA_HEADS * MLA_PAD_DIM
    return _rows_call(fn, "mla_prep_bwd", [q_raw, kv, (down, LANES, kpe_block), cos_t, sin_s, dq, dk, dv], [qg, kg],
                      [(wide, BF16), (wide, BF16), (LANES, F32)], [((1, MLA_PAD_DIM), F32), ((1, MLA_PAD_DIM), F32)],
                      tile=128)


def _sqrelu_up(acc):
    return acc, jnp.square(jnp.maximum(acc, 0.0))


def _sqrelu_grad(acc, u):
    return (acc * (2.0 * jnp.maximum(u, 0.0)),)


def _mlp_fwd(x, g, w_up, w_down, tag):
    h = _norm_fwd(x, g, f"mlp_norm_fwd{tag}")
    u, a = _matmul(h, w_up, "nn", f"mlp_up{tag}", (F32, BF16), _sqrelu_up)
    return _matmul(a, w_down, "nn", f"mlp_down{tag}", (F32,), _add_residual, (x,)), (h, u, a)


def _mlp_bwd(x, g, w_up, w_down, saved, dy, dy16, tag):
    h, u, a = saved
    dw_down = _matmul(a, dy16, "tn", f"mlp_dwdown{tag}", (BF16,))
    du = _matmul(dy16, w_down, "nt", f"mlp_du{tag}", (BF16,), _sqrelu_grad, (u,))
    dw_up = _matmul(h, du, "tn", f"mlp_dwup{tag}", (BF16,))
    dh = _matmul(du, w_up, "nt", f"mlp_dh{tag}")
    dx, dx16, dg = _norm_bwd(x, g, dh, dy, f"mlp_norm_bwd{tag}")
    return dx, dx16, dg, dw_up, dw_down


def _local_step(x, pos_col, target, w, dist=None):
    w = dict(w)
    hs = w["w_a"].shape[1] // (4 * HEAD_DIM)
    sb_w = hs * HEAD_DIM
    grads, received = {}, {}

    def gather_in(group):
        return [("gather", dist["blocks"][group])] if dist else []

    def exchange_in(group):
        return [("all_to_all", dist["slots_of"](group, grads))] if dist else []

    h0 = _norm_fwd(x, w["ln_mix0"], "mix0_norm_fwd")
    pa = _matmul(h0, w["w_a"], "nn", "in_proj_a", (BF16,))
    pb = _matmul(h0, w["w_b"], "nn", "in_proj_b")
    pat = pa[:, sb_w:].T
    o_sb, o_sb_fine, *got = _attn_fwd("stick", pa, pat, pa, hs, HEAD_DIM, HEAD_DIM, HEAD_DIM ** -0.5, "stick_fwd",
                                      q_off=0, k_off=0, v_off=2 * hs, hosted=gather_in("mlp0"))
    if dist:
        w.update(dist["weights_of"]("mlp0", got[0]))
    logit_t = pb[:, 2 * sb_w:2 * sb_w + hs].T
    bias_col = w["b_f"][0, :hs].reshape(hs, 1)
    f_cum = _gate_fwd(logit_t, bias_col)
    f_col, f_row = f_cum[:, :, None], f_cum[:, None, :]
    qf, kf = _fox_norm_fwd(pb, w["fox_q_g"], w["fox_k_g"], hs)
    kft = kf.T
    o_fx, o_fx_fine, lse_fx, *got = _attn_fwd("softmax", qf, kft, pa, hs, HEAD_DIM, HEAD_DIM, HEAD_DIM ** -0.5,
                                              "fox_fwd", v_off=3 * hs, fcol=f_col, frow=f_row,
                                              hosted=gather_in("layer1"))
    if dist:
        w.update(dist["weights_of"]("layer1", got[0]))
    o0 = jnp.concatenate([o_sb, o_fx], axis=1)
    x1 = _matmul(o0, w["w_o0"], "nn", "out_proj0", (F32,), _add_residual, (x,))
    x2, mlp0 = _mlp_fwd(x1, w["ln_mlp0"], w["w_up0"], w["w_dn0"], "0")

    rank = w["w_uq"].shape[0]
    h2 = _norm_fwd(x2, w["ln_mix1"], "mix1_norm_fwd")
    down = _matmul(h2, w["w_down"], "nn", "mla_down")
    cqn, ckvn = _lora_norm_fwd(down, w["q_a_g"], w["kv_a_g"], rank)
    q_raw = _matmul(cqn, w["w_uq"], "nn", "mla_uq")
    kv = _matmul(ckvn, w["w_ukv"], "nn", "mla_ukv")
    cos_t, sin_s = _rope_tables(pos_col, w["inv_freq"], w["sin_sign"])
    kpe_block = 2 * rank // LANES
    qm, km, vm = _mla_prep_fwd(q_raw, kv, down, kpe_block, w["mla_q_g"], w["mla_k_g"], cos_t, sin_s)
    kmt, vmt = km.T, vm.T
    o_m, o_m_fine, lse_m = _attn_fwd("softmax", qm, kmt, vm, MLA_HEADS, MLA_PAD_DIM, HEAD_DIM, MLA_QK_DIM ** -0.5,
                                     "mla_fwd")
    x3 = _matmul(o_m, w["w_o1"], "nn", "out_proj1", (F32,), _add_residual, (x2,))
    x4, mlp1 = _mlp_fwd(x3, w["ln_mlp1"], w["w_up1"], w["w_dn1"], "1")

    dy, dy16, sq_err = _loss_fwd_bwd(x4, target)

    dx3, dx3_16, grads["ln_mlp1"], grads["w_up1"], grads["w_dn1"] = _mlp_bwd(
        x3, w["ln_mlp1"], w["w_up1"], w["w_dn1"], mlp1, dy, dy16, "1")
    grads["w_o1"] = _matmul(o_m, dx3_16, "tn", "dw_o1", (BF16,))
    do_m = _matmul(dx3_16, w["w_o1"], "nt", "do_mla", (BF16,))
    dqm, dkm, dvm, *got = _attn_bwd("softmax", qm, km, kmt, vmt, o_m_fine, do_m, MLA_HEADS, MLA_PAD_DIM, HEAD_DIM,
                                    MLA_QK_DIM ** -0.5, "mla_bwd", lse=lse_m, hosted=exchange_in("mlp1"))
    received["mlp1"] = got[0] if dist else None
    dq_raw, dkv, dkpe, grads["mla_q_g"], grads["mla_k_g"] = _mla_prep_bwd(
        q_raw, kv, down, kpe_block, w["mla_q_g"], w["mla_k_g"], cos_t, sin_s, dqm, dkm, dvm)
    grads["w_uq"] = _matmul(cqn, dq_raw, "tn", "dw_uq", (BF16,))
    grads["w_ukv"] = _matmul(ckvn, dkv, "tn", "dw_ukv", (BF16,))
    dcqn = _matmul(dq_raw, w["w_uq"], "nt", "d_cq")
    dckvn = _matmul(dkv, w["w_ukv"], "nt", "d_ckv")
    ddown, grads["q_a_g"], grads["kv_a_g"] = _lora_norm_bwd(down, w["q_a_g"], w["kv_a_g"], dcqn, dckvn, dkpe, rank)
    grads["w_down"] = _matmul(h2, ddown, "tn", "dw_down", (BF16,))
    dh2 = _matmul(ddown, w["w_down"], "nt", "d_h2")
    dx2, dx2_16, grads["ln_mix1"] = _norm_bwd(x2, w["ln_mix1"], dh2, dx3, "mix1_norm_bwd")

    dx1, dx1_16, grads["ln_mlp0"], grads["w_up0"], grads["w_dn0"] = _mlp_bwd(
        x1, w["ln_mlp0"], w["w_up0"], w["w_dn0"], mlp0, dx2, dx2_16, "0")
    grads["w_o0"] = _matmul(o0, dx1_16, "tn", "dw_o0", (BF16,))
    do0 = _matmul(dx1_16, w["w_o0"], "nt", "do_mix0", (BF16,))
    dq_sb, dk_sb, dv_sb, *got = _attn_bwd("stick", pa, pa, pat, pat, o_sb_fine, do0, hs, HEAD_DIM, HEAD_DIM,
                                          HEAD_DIM ** -0.5, "stick_bwd", q_off=0, k_off=hs, kt_off=0, vt_off=hs, do_off=0,
                                          hosted=exchange_in("mla"))
    received["mla"] = got[0] if dist else None
    dqf, dkf, dv_fx, ds_rows, ds_cols, *got = _attn_bwd(
        "softmax", qf, kf, kft, pat, o_fx_fine, do0, hs, HEAD_DIM, HEAD_DIM, HEAD_DIM ** -0.5, "fox_bwd", vt_off=2 * hs,
        do_off=hs, lse=lse_fx, fcol=f_col, frow=f_row, hosted=exchange_in("mlp0"))
    received["mlp0"] = got[0] if dist else None
    dqk_fx, grads["fox_q_g"], grads["fox_k_g"] = _fox_norm_bwd(pb, w["fox_q_g"], w["fox_k_g"], dqf, dkf, hs)
    dlogit_t, db_f = _gate_bwd(ds_rows[:, :, 0] - ds_cols[:, 0, :], logit_t, bias_col)
    grads["b_f"] = db_f.reshape(1, hs)
    dpa = jnp.concatenate([dq_sb.astype(BF16), dk_sb.astype(BF16), dv_sb.astype(BF16), dv_fx.astype(BF16)], axis=1)
    dlogit_pad = jnp.pad(dlogit_t.T.astype(BF16), ((0, 0), (0, pb.shape[1] - 2 * sb_w - hs)))
    dpb = jnp.concatenate([dqk_fx, dlogit_pad], axis=1)
    grads["w_a"] = _matmul(h0, dpa, "tn", "dw_a", (BF16,))
    grads["w_b"] = _matmul(h0, dpb, "tn", "dw_b", (BF16,))
    dh0 = _matmul(dpb, w["w_b"], "nt", "d_h0_b")
    dh0 = _matmul(dpa, w["w_a"], "nt", "d_h0_a", (F32,), _add_residual, (dh0,))
    grad_x, _, grads["ln_mix0"] = _norm_bwd(x, w["ln_mix0"], dh0, dx1, "mix0_norm_bwd")
    return sq_err, grad_x, grads, received


PIECES = {
    "sf_w_in": ("sf_w_in", 0, 1), "sf_w_o": ("sf_w_o", 0, 0), "mla_w_down": ("mla_w_down", 0, 0),
    "mla_w_uq": ("mla_w_uq", 0, 1), "mla_w_ukv": ("mla_w_ukv", 0, 1), "mla_w_o": ("mla_w_o", 0, 0),
    "mlp_w_up0": ("mlp_w_up", 0, 1), "mlp_w_up1": ("mlp_w_up", 1, 1),
    "mlp_w_down0": ("mlp_w_down", 0, 0), "mlp_w_down1": ("mlp_w_down", 1, 0),
}
GROUPS = {
    "mix0": ["sf_w_in", "sf_w_o"], "mlp0": ["mlp_w_up0", "mlp_w_down0"],
    "mla": ["mla_w_down", "mla_w_uq", "mla_w_ukv", "mla_w_o"], "mlp1": ["mlp_w_up1", "mlp_w_down1"],
}
GROUPS["layer1"] = GROUPS["mla"] + GROUPS["mlp1"]
SMALL = ["ln_mix_g", "ln_mlp_g", "sf_b_f", "fox_q_g", "fox_k_g", "mla_q_a_g", "mla_kv_a_g", "mla_q_g", "mla_k_g"]
ALL_W = ["ln_mix_g", "ln_mlp_g", "sf_w_in", "sf_b_f", "fox_q_g", "fox_k_g", "sf_w_o", "mla_w_down", "mla_q_a_g",
         "mla_kv_a_g", "mla_w_uq", "mla_w_ukv", "mla_q_g", "mla_k_g", "mla_w_o", "mlp_w_up", "mlp_w_down"]


def _weights_mix0(full, small):
    w_in = full["sf_w_in"]
    d_model = w_in.shape[0]
    n_fx = small["sf_b_f"].shape[1]
    sb_w = (w_in.shape[1] - n_fx) // 6
    cols = lambda i: w_in[:, i * sb_w:(i + 1) * sb_w]
    w_a = jnp.concatenate([cols(0), cols(1), cols(2), cols(5)], axis=1)
    w_b = jnp.concatenate([cols(3), cols(4), w_in[:, 6 * sb_w:], jnp.zeros((d_model, LANES - n_fx), w_in.dtype)], axis=1)
    half = ROPE_DIM // 2
    inv_freq = ROPE_THETA ** (-jnp.arange(half, dtype=F32) / half)
    zeros64 = jnp.zeros((ROPE_DIM,), F32)
    pad256 = lambda g: jnp.pad(g, ((0, 0), (0, MLA_PAD_DIM - MLA_QK_DIM)))
    pad_lanes = lambda g: jnp.pad(g, ((0, 0), (0, LANES - g.shape[1])))
    return dict(
        ln_mix0=small["ln_mix_g"][0:1], ln_mix1=small["ln_mix_g"][1:2],
        ln_mlp0=small["ln_mlp_g"][0:1], ln_mlp1=small["ln_mlp_g"][1:2],
        w_a=w_a, w_b=w_b, b_f=pad_lanes(small["sf_b_f"]), fox_q_g=small["fox_q_g"], fox_k_g=small["fox_k_g"],
        w_o0=full["sf_w_o"], q_a_g=small["mla_q_a_g"], kv_a_g=small["mla_kv_a_g"],
        mla_q_g=pad256(small["mla_q_g"]), mla_k_g=pad256(small["mla_k_g"]),
        inv_freq=jnp.concatenate([inv_freq, inv_freq, zeros64]).reshape(1, LANES),
        sin_sign=jnp.concatenate([-jnp.ones((half,), F32), jnp.ones((half,), F32), zeros64]).reshape(1, LANES),
    )


def _weights_mlp0(full):
    return dict(w_up0=full["mlp_w_up0"], w_dn0=full["mlp_w_down0"])


def _weights_layer1(full):
    rank = full["mla_w_uq"].shape[0]
    w_uq = full["mla_w_uq"].reshape(rank, MLA_HEADS, MLA_QK_DIM)
    w_uq = jnp.pad(w_uq, ((0, 0), (0, 0), (0, MLA_PAD_DIM - MLA_QK_DIM))).reshape(rank, MLA_HEADS * MLA_PAD_DIM)
    return dict(w_down=jnp.pad(full["mla_w_down"], ((0, 0), (0, LANES - ROPE_DIM))), w_uq=w_uq,
                w_ukv=full["mla_w_ukv"], w_o1=full["mla_w_o"], w_up1=full["mlp_w_up1"], w_dn1=full["mlp_w_down1"])


WEIGHTS_OF = {"mlp0": _weights_mlp0, "layer1": _weights_layer1}


def _piece_grad(g, piece):
    if piece == "sf_w_in":
        n_fx = g["b_f"].shape[1]
        ga, gb = g["w_a"], g["w_b"]
        sb_w = ga.shape[1] // 4
        ca = lambda i: ga[:, i * sb_w:(i + 1) * sb_w]
        return jnp.concatenate([ca(0), ca(1), ca(2), gb[:, :sb_w], gb[:, sb_w:2 * sb_w], ca(3),
                                gb[:, 2 * sb_w:2 * sb_w + n_fx]], axis=1)
    if piece == "mla_w_uq":
        rank = g["w_uq"].shape[0]
        return g["w_uq"].reshape(rank, MLA_HEADS, MLA_PAD_DIM)[:, :, :MLA_QK_DIM].reshape(rank, MLA_HEADS * MLA_QK_DIM)
    if piece == "mla_w_down":
        return g["w_down"][:, :g["w_down"].shape[1] - (LANES - ROPE_DIM)]
    return g[{"sf_w_o": "w_o0", "mla_w_ukv": "w_ukv", "mla_w_o": "w_o1", "mlp_w_up0": "w_up0", "mlp_w_up1": "w_up1",
              "mlp_w_down0": "w_dn0", "mlp_w_down1": "w_dn1"}[piece]]


def _small_grads(g):
    return {
        "ln_mix_g": jnp.concatenate([g["ln_mix0"], g["ln_mix1"]], axis=0),
        "ln_mlp_g": jnp.concatenate([g["ln_mlp0"], g["ln_mlp1"]], axis=0),
        "sf_b_f": g["b_f"], "fox_q_g": g["fox_q_g"], "fox_k_g": g["fox_k_g"],
        "mla_q_a_g": g["q_a_g"], "mla_kv_a_g": g["kv_a_g"],
        "mla_q_g": g["mla_q_g"][:, :MLA_QK_DIM], "mla_k_g": g["mla_k_g"][:, :MLA_QK_DIM],
    }


PACK_TILE = 1024


def _as_rows(a, row_multiple=16):
    flat = a.reshape(-1)
    rows = -(-flat.shape[0] // LANES)
    rows = -(-rows // row_multiple) * row_multiple
    return jnp.pad(flat, (0, rows * LANES - flat.shape[0])).reshape(rows, LANES)


def _pack_rows(parts, axis, dtype, row_multiple=PACK_TILE, spare_rows=0):
    used = sum(p.shape[axis] for p in parts)
    shape = list(parts[0].shape)
    shape[axis] = -(-used // row_multiple) * row_multiple + spare_rows - used
    return jnp.concatenate([p.astype(dtype) for p in parts] + [jnp.ones(shape, dtype)], axis=axis)


def _unshard(stack, axis):
    moved = jnp.moveaxis(stack, 0, axis)
    shape = list(stack.shape[1:])
    shape[axis] *= N_DEV
    return moved.reshape(shape)


def _shard_stack(full, axis):
    shape = list(full.shape)
    shape[axis:axis + 1] = [N_DEV, shape[axis] // N_DEV]
    return jnp.moveaxis(full.reshape(shape), axis, 0)


def _cast_bf16(a, name):
    return _rows_call(lambda v: v, name, [a], [], [(a.shape[1], BF16)], tile=PACK_TILE)


def _adam_math(w, g, m, v):
    m = ADAM_B1 * m + (1.0 - ADAM_B1) * g
    v = ADAM_B2 * v + (1.0 - ADAM_B2) * jnp.square(g)
    m_hat = m / (1.0 - ADAM_B1 ** ADAM_STEP)
    v_hat = v / (1.0 - ADAM_B2 ** ADAM_STEP)
    delta = -ADAM_LR * (m_hat / (jnp.sqrt(v_hat) + ADAM_EPS) + ADAM_WD * w)
    return delta, m, v


def _adam_big(recv, w, m, v, name):
    rows = w.shape[0]

    def body(r_ref, w_ref, m_ref, v_ref, g_ref, d_ref, nm_ref, nv_ref):
        g = r_ref[0].astype(F32)
        for s in range(1, N_DEV):
            g = g + r_ref[s].astype(F32)
        delta, nm, nv = _adam_math(w_ref[...], g, m_ref[...], v_ref[...])
        g_ref[...] = g
        d_ref[...] = delta
        nm_ref[...] = nm
        nv_ref[...] = nv

    spec = pl.BlockSpec((PACK_TILE, LANES), lambda i: (i, 0))
    out = jax.ShapeDtypeStruct((rows, LANES), F32)
    return pl.pallas_call(
        body, name=name, grid=(rows // PACK_TILE,),
        in_specs=[pl.BlockSpec((N_DEV, PACK_TILE, LANES), lambda i: (0, i, 0)), spec, spec, spec],
        out_specs=[spec] * 4, out_shape=[out] * 4, compiler_params=_params("parallel"),
    )(recv, w, m, v)


def _sum_slots(gathered):
    rows = gathered.shape[1]

    def body(r_ref, o_ref):
        acc = r_ref[0]
        for s in range(1, N_DEV):
            acc = acc + r_ref[s]
        o_ref[...] = acc

    return pl.pallas_call(body, name="sum_small", out_shape=jax.ShapeDtypeStruct((rows, LANES), F32))(gathered)


def _adam_small(w, g, m, v):
    def fn(wv, gv, mv, vv):
        return _adam_math(wv, gv, mv, vv)
    return _rows_call(fn, "adam_small", [w, g, m, v], [], [(LANES, F32)] * 3, tile=w.shape[0])


def kernel(x, positions, ln_mix_g, ln_mlp_g, sf_w_in, sf_b_f, fox_q_g, fox_k_g, sf_w_o, mla_w_down, mla_q_a_g, mla_kv_a_g, mla_w_uq, mla_w_ukv, mla_q_g, mla_k_g, mla_w_o, mlp_w_up, mlp_w_down, loss_target, m_ln_mix_g, m_ln_mlp_g, m_sf_w_in, m_sf_b_f, m_fox_q_g, m_fox_k_g, m_sf_w_o, m_mla_w_down, m_mla_q_a_g, m_mla_kv_a_g, m_mla_w_uq, m_mla_w_ukv, m_mla_q_g, m_mla_k_g, m_mla_w_o, m_mlp_w_up, m_mlp_w_down, v_ln_mix_g, v_ln_mlp_g, v_sf_w_in, v_sf_b_f, v_fox_q_g, v_fox_k_g, v_sf_w_o, v_mla_w_down, v_mla_q_a_g, v_mla_kv_a_g, v_mla_w_uq, v_mla_w_ukv, v_mla_q_g, v_mla_k_g, v_mla_w_o, v_mlp_w_up, v_mlp_w_down):
    given = dict(locals())
    wts = {n: given[n] for n in ALL_W}
    mom = {n: given["m_" + n] for n in ALL_W}
    var = {n: given["v_" + n] for n in ALL_W}
    me = 4 * lax.axis_index("x") + 2 * lax.axis_index("y") + lax.axis_index("c")
    t_rows, d_model = x.shape[1], x.shape[2]
    piece_of = lambda d, p: d[PIECES[p][0]][PIECES[p][1]]

    def pack_group(d, group, dtype=F32):
        return _pack_rows([_as_rows(piece_of(d, p)) for p in GROUPS[group]], 0, dtype)

    def unpack_group(packed, group, lead=()):
        out, off = {}, 0
        for p in GROUPS[group]:
            shard = piece_of(wts, p).shape
            rows = _as_rows(piece_of(wts, p)).shape[0]
            out[p] = packed[..., off:off + rows, :].reshape(lead + (-1,))[..., :math.prod(shard)].reshape(lead + shard)
            off += rows
        return out

    def whole_pieces(gathered, group):
        return {p: _unshard(s, PIECES[p][2]) for p, s in unpack_group(gathered, group, (N_DEV,)).items()}

    def slots_of(group, grads):
        parts = []
        for p in GROUPS[group]:
            stack = _shard_stack(_piece_grad(grads, p), PIECES[p][2]).reshape(N_DEV, -1)
            rows = _as_rows(piece_of(wts, p)).shape[0]
            parts.append(jnp.pad(stack, ((0, 0), (0, rows * LANES - stack.shape[1]))).reshape(N_DEV, rows, LANES))
        return _pack_rows(parts, 1, BF16)

    w_packed = {grp: pack_group(wts, grp) for grp in ("mix0", "mlp0", "layer1", "mla", "mlp1")}
    blocks = {grp: _cast_bf16(w_packed[grp], f"cast_{grp}") for grp in ("mix0", "mlp0", "layer1")}
    mix0 = whole_pieces(_exchange("gather", blocks["mix0"], "gather_mix0"), "mix0")
    gains = _exchange("gather", _as_rows(jnp.concatenate([mla_q_a_g, mla_kv_a_g], axis=1)), "gather_gains")
    lora_n = mla_q_a_g.shape[1]
    gains_flat = gains.reshape(N_DEV, -1)[:, :2 * lora_n]
    small = dict(ln_mix_g=ln_mix_g, ln_mlp_g=ln_mlp_g, sf_b_f=sf_b_f, fox_q_g=fox_q_g, fox_k_g=fox_k_g,
                 mla_q_a_g=gains_flat[:, :lora_n].reshape(1, -1), mla_kv_a_g=gains_flat[:, lora_n:].reshape(1, -1),
                 mla_q_g=mla_q_g, mla_k_g=mla_k_g)
    dist = dict(blocks=blocks, slots_of=slots_of,
                weights_of=lambda grp, gathered: WEIGHTS_OF[grp](whole_pieces(gathered, grp)))
    sq_err, grad_x, g, received = _local_step(x[0], positions.reshape(t_rows, 1), loss_target[0],
                                              _weights_mix0(mix0, small), dist)
    received["mix0"] = _exchange("all_to_all", slots_of("mix0", g), "exchange_mix0")

    results = {kind: {} for kind in ("grad", "delta", "new_m", "new_v")}
    for grp in ("mlp1", "mla", "mlp0", "mix0"):
        outs = _adam_big(received[grp], w_packed[grp], pack_group(mom, grp), pack_group(var, grp), f"adam_{grp}")
        for kind, packed in zip(("grad", "delta", "new_m", "new_v"), outs):
            results[kind].update(unpack_group(packed, grp))

    small_g = _small_grads(g)
    small_parts = [_as_rows(small_g[n], 8) for n in SMALL] + [_as_rows(sq_err, 8)]
    small_sum = _sum_slots(_exchange("gather", _pack_rows(small_parts, 0, F32, 8, 8), "gather_small_grads"))
    red, off = {}, 0
    for n, p in zip(SMALL + ["loss"], small_parts):
        red[n] = small_sum[off:off + p.shape[0]].reshape(-1)
        off += p.shape[0]
    loss = 0.5 * red["loss"][0] / d_model
    for n in SMALL:
        if n in ("mla_q_a_g", "mla_kv_a_g"):
            results["grad"][n] = lax.dynamic_slice(red[n], (me * lora_n,), (lora_n,)).reshape(wts[n].shape)
        else:
            results["grad"][n] = red[n][:wts[n].size].reshape(wts[n].shape)
    pack_small = lambda d: jnp.concatenate([_as_rows(d[n], 8) for n in SMALL], axis=0)
    small_out = _adam_small(pack_small(wts), pack_small(results["grad"]), pack_small(mom), pack_small(var))
    off = 0
    for n in SMALL:
        r = _as_rows(wts[n], 8).shape[0]
        for kind, packed in zip(["delta", "new_m", "new_v"], small_out):
            results[kind][n] = packed[off:off + r].reshape(-1)[:wts[n].size].reshape(wts[n].shape)
        off += r

    def whole(kind, n):
        layers = sorted((layer, p) for p, (name, layer, _) in PIECES.items() if name == n)
        return jnp.stack([results[kind][p] for _, p in layers]) if layers else results[kind][n]

    outs = [loss, grad_x[None]]
    for kind in ["grad", "delta", "new_m", "new_v"]:
        outs += [whole(kind, n) for n in ALL_W]
    return tuple(outs)
```

```python
import functools
import math

import jax
import jax.numpy as jnp
import numpy as np
from jax import lax
from jax.experimental import pallas as pl
from jax.experimental.pallas import tpu as pltpu

F32 = jnp.float32
BF16 = jnp.bfloat16

NORM_EPS = 1e-6
ROPE_THETA = 10000.0
HEAD_DIM = 128
ROPE_DIM = 64
MLA_HEADS = 16
MLA_QK_DIM = 192
MLA_PAD_DIM = 256
ADAM_LR, ADAM_B1, ADAM_B2, ADAM_EPS, ADAM_WD, ADAM_STEP = 0.001, 0.9, 0.999, 1e-08, 0.01, 10

N_DEV = 8
LANES = 128
VMEM_LIMIT = 56 * 1024 * 1024
MATMUL_VMEM_BUDGET = 40 * 1024 * 1024
MASKED = -1e30
MESH = pl.DeviceIdType.MESH

NT_DIMS = (((1,), (1,)), ((), ()))
TN_DIMS = (((0,), (0,)), ((), ()))
NN_DIMS = (((1,), (0,)), ((), ()))


def _params(*sem):
    return pltpu.CompilerParams(dimension_semantics=sem, vmem_limit_bytes=VMEM_LIMIT)


def _pick(n, pref):
    best = None
    for t in range(LANES, min(n, pref) + 1, LANES):
        if n % t == 0:
            best = t
    return n if best is None or 2 * best < min(n, pref) else best


def _rows_call(fn, name, row_ins, full_ins, row_outs, acc_outs=(), tile=256):
    row_ins = [r if isinstance(r, tuple) else (r, r.shape[1], 0) for r in row_ins]
    t_rows = row_ins[0][0].shape[0]
    assert t_rows % tile == 0
    n_in = len(row_ins) + len(full_ins)
    n_row_out = len(row_outs)

    def body(*refs):
        res = fn(*[r[...] for r in refs[:n_in]])
        res = res if isinstance(res, tuple) else (res,)
        for ref, val in zip(refs[n_in:n_in + n_row_out], res[:n_row_out]):
            ref[...] = val.astype(ref.dtype)
        acc_refs = refs[n_in + n_row_out:]
        if acc_refs:
            @pl.when(pl.program_id(0) == 0)
            def _():
                for ref in acc_refs:
                    ref[...] = jnp.zeros_like(ref)
            for ref, val in zip(acc_refs, res[n_row_out:]):
                ref[...] += val.astype(ref.dtype)

    in_specs = [pl.BlockSpec((tile, w), functools.partial(lambda i, cb: (i, cb), cb=cb)) for _, w, cb in row_ins]
    in_specs += [pl.BlockSpec(a.shape, lambda i: (0, 0)) for a in full_ins]
    out_specs = [pl.BlockSpec((tile, c), lambda i: (i, 0)) for c, _ in row_outs]
    out_specs += [pl.BlockSpec(s, lambda i: (0, 0)) for s, _ in acc_outs]
    out_shape = [jax.ShapeDtypeStruct((t_rows, c), d) for c, d in row_outs]
    out_shape += [jax.ShapeDtypeStruct(s, d) for s, d in acc_outs]
    outs = pl.pallas_call(
        body, name=name, grid=(t_rows // tile,), in_specs=in_specs, out_specs=out_specs, out_shape=out_shape,
        compiler_params=_params("arbitrary"),
    )(*[r[0] for r in row_ins], *full_ins)
    return outs[0] if len(outs) == 1 else tuple(outs)


def _matmul_tiles(m, n, k, in_bytes, out_bytes):
    tn = n if n <= 1280 else _pick(n, 1024)
    tks = [k] + [k // d for d in (2, 4, 8, 16) if k % (d * LANES) == 0]
    for tk in [t for t in tks if t <= 4096] or [tks[-1]]:
        for tm in (1024, 512, 256):
            if m % tm:
                continue
            acc = 2 * tm * tn * 4 if tk < k else tm * tn * 4
            if 2 * (tm * tk + tk * tn) * in_bytes + 2 * tm * tn * out_bytes + acc <= MATMUL_VMEM_BUDGET:
                return tm, tn, tk
    raise ValueError(f"no matmul tiling for {m}x{n}x{k}")


def _matmul(a, b, form, name, out_dtypes=(F32,), epilogue=None, extras=()):
    if form == "nn":
        (m, k), n = a.shape, b.shape[1]
    elif form == "nt":
        (m, k), n = a.shape, b.shape[0]
    else:
        (k, m), n = a.shape, b.shape[1]
    in_bytes = max(a.dtype.itemsize, b.dtype.itemsize)
    out_bytes = sum(jnp.dtype(d).itemsize for d in out_dtypes) + sum(e.dtype.itemsize for e in extras)
    tm, tn, tk = _matmul_tiles(m, n, k, in_bytes, out_bytes)
    nk = k // tk
    dims = {"nn": NN_DIMS, "nt": NT_DIMS, "tn": TN_DIMS}[form]
    n_extra, n_out = len(extras), len(out_dtypes)

    def body(*refs):
        a_ref, b_ref = refs[0], refs[1]
        extra_refs = refs[2:2 + n_extra]
        out_refs = refs[2 + n_extra:2 + n_extra + n_out]

        def finish(acc):
            vals = (acc,) if epilogue is None else epilogue(acc, *[r[...] for r in extra_refs])
            for ref, val in zip(out_refs, vals):
                ref[...] = val.astype(ref.dtype)

        part = lax.dot_general(a_ref[...].astype(BF16), b_ref[...].astype(BF16), dims, preferred_element_type=F32)
        if nk == 1:
            finish(part)
        else:
            acc_ref = refs[-1]
            kk = pl.program_id(2)

            @pl.when(kk == 0)
            def _():
                acc_ref[...] = part

            @pl.when(kk > 0)
            def _():
                acc_ref[...] += part

            @pl.when(kk == nk - 1)
            def _():
                finish(acc_ref[...])

    a_spec = pl.BlockSpec((tk, tm), lambda i, j, kk: (kk, i)) if form == "tn" else pl.BlockSpec((tm, tk), lambda i, j, kk: (i, kk))
    b_spec = pl.BlockSpec((tn, tk), lambda i, j, kk: (j, kk)) if form == "nt" else pl.BlockSpec((tk, tn), lambda i, j, kk: (kk, j))
    o_spec = pl.BlockSpec((tm, tn), lambda i, j, kk: (i, j))
    outs = pl.pallas_call(
        body, name=name, grid=(m // tm, n // tn, nk), in_specs=[a_spec, b_spec] + [o_spec] * n_extra,
        out_specs=[o_spec] * n_out, out_shape=[jax.ShapeDtypeStruct((m, n), d) for d in out_dtypes],
        scratch_shapes=[pltpu.VMEM((tm, tn), F32)] if nk > 1 else [],
        compiler_params=_params("parallel", "parallel", "arbitrary"),
    )(a, b, *extras)
    return outs[0] if n_out == 1 else tuple(outs)


def _add_residual(acc, res):
    return (acc + res,)


def _log_sigmoid_parts(z):
    return jnp.log1p(jnp.exp(-jnp.abs(z)))


def _rms_fwd(x, g, n=None):
    n = x.shape[-1] if n is None else n
    r = lax.rsqrt(jnp.sum(x * x, axis=-1, keepdims=True) / n + NORM_EPS)
    return x * r * g


def _rms_bwd(x, g, dout, n=None):
    n = x.shape[-1] if n is None else n
    r = lax.rsqrt(jnp.sum(x * x, axis=-1, keepdims=True) / n + NORM_EPS)
    y = x * r
    dg = jnp.sum(dout * y, axis=0, keepdims=True)
    dy = dout * g
    dx = r * (dy - y * (jnp.sum(dy * y, axis=-1, keepdims=True) / n))
    return dx, dg


def _swap_halves(r):
    lane = lax.broadcasted_iota(jnp.int32, r.shape, 1)
    return jnp.where(lane < ROPE_DIM // 2, pltpu.roll(r, LANES - ROPE_DIM // 2, 1), pltpu.roll(r, ROPE_DIM // 2, 1))


def _rope_fwd(r, cos_t, sin_s):
    return r * cos_t + _swap_halves(r) * sin_s


def _rope_bwd(dr, cos_t, sin_s):
    return dr * cos_t + _swap_halves(dr * sin_s)


def _split3(x):
    hi = x.astype(BF16)
    r1 = x - hi.astype(F32)
    mid = r1.astype(BF16)
    lo = (r1 - mid.astype(F32)).astype(BF16)
    return hi, mid, lo


def _mesh_position():
    x, y, c = lax.axis_index("x"), lax.axis_index("y"), lax.axis_index("c")
    return x, y, c, 4 * x + 2 * y + c


def _peer(x, y, c, k):
    bx, by, bc = (k >> 2) & 1, (k >> 1) & 1, k & 1
    px, py, pc = x ^ bx, y ^ by, c ^ bc
    return (px, py, pc), 4 * px + 2 * py + pc


def _gather_steps(x_ref, out_ref, send_sems, recv_sems, local_sem):
    x, y, c, me = _mesh_position()
    sibling = (x, y, 1 - c)
    chips = [(1 - x, y), (x, 1 - y), (1 - x, 1 - y)]

    def slot(px, py, pc):
        return out_ref.at[4 * px + 2 * py + pc]

    def copy(k, blk, to, src=None):
        return pltpu.make_async_remote_copy(
            src_ref=slot(*blk) if src is None else src, dst_ref=slot(*blk), send_sem=send_sems.at[k],
            recv_sem=recv_sems.at[k], device_id=to, device_id_type=MESH)

    mine = pltpu.make_async_copy(x_ref, out_ref.at[me], local_sem)
    first = [copy(0, (x, y, c), sibling, src=x_ref)]
    first += [copy(1 + j, (x, y, c), (*chip, c), src=x_ref) for j, chip in enumerate(chips)]
    passed = [copy(4 + j, (*chip, c), sibling) for j, chip in enumerate(chips)]

    def start():
        mine.start()
        for cp in first:
            cp.start()

    def forward():
        for j, chip in enumerate(chips):
            copy(1 + j, (*chip, c), (x, y, c)).wait_recv()
            passed[j].start()

    def finish():
        copy(0, (x, y, 1 - c), (x, y, c)).wait_recv()
        for j, chip in enumerate(chips):
            copy(4 + j, (*chip, 1 - c), (x, y, c)).wait_recv()
        for cp in first + passed:
            cp.wait_send()
        mine.wait()

    return start, forward, finish


def _all_to_all_steps(g_ref, out_ref, send_sems, recv_sems, local_sem):
    x, y, c, me = _mesh_position()
    mine = pltpu.make_async_copy(g_ref.at[me], out_ref.at[me], local_sem)
    copies = []
    for k in range(1, N_DEV):
        peer, peer_idx = _peer(x, y, c, k)
        copies.append(pltpu.make_async_remote_copy(
            src_ref=g_ref.at[peer_idx], dst_ref=out_ref.at[me], send_sem=send_sems.at[k - 1],
            recv_sem=recv_sems.at[k - 1], device_id=peer, device_id_type=MESH))

    def start():
        mine.start()
        for cp in copies:
            cp.start()

    def finish():
        for k in range(1, N_DEV):
            peer, peer_idx = _peer(x, y, c, k)
            pltpu.make_async_remote_copy(
                src_ref=g_ref.at[me], dst_ref=out_ref.at[peer_idx], send_sem=send_sems.at[k - 1],
                recv_sem=recv_sems.at[k - 1], device_id=peer, device_id_type=MESH).wait_recv()
        for cp in copies:
            cp.wait_send()
        mine.wait()

    return start, None, finish


EXCHANGE_STEPS = {"gather": _gather_steps, "all_to_all": _all_to_all_steps}
EXCHANGE_SCRATCH = [pltpu.SemaphoreType.DMA((7,)), pltpu.SemaphoreType.DMA((7,)), pltpu.SemaphoreType.DMA]
ANY_SPEC = pl.BlockSpec(memory_space=pl.ANY)


def _exchange_out_shape(kind, arr):
    return jax.ShapeDtypeStruct(((N_DEV,) + arr.shape) if kind == "gather" else arr.shape, arr.dtype)


def _exchange(kind, arrs, name):
    n = len(arrs)

    def body(*refs):
        steps = [EXCHANGE_STEPS[kind](refs[i], refs[n + i], *refs[2 * n + 3 * i:2 * n + 3 * i + 3]) for i in range(n)]
        for start, _, _ in steps:
            start()
        for _, forward, _ in steps:
            if forward is not None:
                forward()
        for _, _, finish in steps:
            finish()

    return pl.pallas_call(body, name=name, out_shape=[_exchange_out_shape(kind, a) for a in arrs],
                          in_specs=[ANY_SPEC] * n, out_specs=[ANY_SPEC] * n, scratch_shapes=EXCHANGE_SCRATCH * n)(*arrs)


def _run_hosted(hosted, src_refs, dst_refs, sem_refs, step, n_steps, when):
    for idx, (kind, _) in enumerate(hosted):
        start, forward, finish = EXCHANGE_STEPS[kind](src_refs[idx], dst_refs[idx], *sem_refs[3 * idx:3 * idx + 3])
        if when == "start":
            pl.when(step == 0)(start)
            if forward is not None:
                pl.when(step == (3 * n_steps) // 4)(forward)
        else:
            pl.when(step == n_steps - 1)(finish)


def _causal_iotas(qi, tq, tk):
    row = qi * tq + lax.broadcasted_iota(jnp.int32, (tq, tk), 0)
    col = lax.broadcasted_iota(jnp.int32, (tq, tk), 1)
    return row, col


def _suffix_matrix(tk, inclusive):
    j = lax.broadcasted_iota(jnp.int32, (2 * tk, tk), 0) % tk
    s = lax.broadcasted_iota(jnp.int32, (2 * tk, tk), 1)
    return jnp.where((j >= s) if inclusive else (j > s), 1.0, 0.0).astype(BF16)


def _suffix_sum(x, mat):
    hi = x.astype(BF16)
    lo = (x - hi.astype(F32)).astype(BF16)
    return lax.dot_general(jnp.concatenate([hi, lo], axis=1), mat, NN_DIMS, preferred_element_type=F32)


def _attn_specs(t_rows, tq, heads, dk, dv, q_off, k_off, v_off):
    q_spec = pl.BlockSpec((tq, dk), lambda h, i: (i, q_off + h))
    kt_spec = pl.BlockSpec((dk, t_rows), lambda h, i: (k_off + h, 0))
    v_spec = pl.BlockSpec((t_rows, dv), lambda h, i: (0, v_off + h))
    return q_spec, kt_spec, v_spec


def _split_weights(weights):
    hi = weights.astype(BF16)
    return hi, (weights - hi.astype(F32)).astype(BF16)


def _weighted_values(split, v):
    return (lax.dot_general(split[0], v, NN_DIMS, preferred_element_type=F32),
            lax.dot_general(split[1], v, NN_DIMS, preferred_element_type=F32))


def _attn_fwd(kind, q_arr, kt_arr, v_arr, heads, dk, dv, scale, name, q_off=0, k_off=0, v_off=0, fcol=None, frow=None,
              tq=256, hosted=()):
    t_rows = q_arr.shape[0]
    tk = tq
    nq = t_rows // tq
    stick = kind == "stick"
    decay = fcol is not None
    n_in = 5 if decay else 3
    n_out = 2 if stick else 3
    n_host = len(hosted)

    def body(*refs):
        q_ref, kt_ref, v_ref = refs[:3]
        fcol_ref, frow_ref = (refs[3], refs[4]) if decay else (None, None)
        base = n_in + n_host
        o_ref, fine_ref = refs[base], refs[base + 1]
        lse_ref = None if stick else refs[base + 2]
        host_args = (hosted, refs[n_in:base], refs[base + n_out:base + n_out + n_host], refs[base + n_out + n_host:],
                     pl.program_id(0) * nq + pl.program_id(1), heads * nq)
        _run_hosted(*host_args, "start")
        qi = pl.program_id(1)
        q = q_ref[...]
        row, col = _causal_iotas(qi, tq, tk)
        n_kb = qi + 1
        zeros_o = jnp.zeros((tq, dv), F32)

        no_weights = (jnp.zeros((tq, tk), BF16), jnp.zeros((tq, tk), BF16))

        def raw_logits(kb):
            return lax.dot_general(q, kt_ref[:, pl.ds(pl.multiple_of(kb * tk, tk), tk)], NN_DIMS,
                                   preferred_element_type=F32)

        def values(kb):
            return v_ref[pl.ds(pl.multiple_of(kb * tk, tk), tk), :]

        if stick:
            mat = _suffix_matrix(tk, inclusive=False)

            def step(i, carry):
                c, acc, rem, raw, prev = carry
                raw_next = raw_logits(jnp.maximum(n_kb - 2 - i, 0))
                d_acc, d_rem = _weighted_values(prev, values(jnp.minimum(n_kb - i, n_kb - 1)))
                ks = pl.multiple_of((n_kb - 1 - i) * tk, tk)
                z = raw * scale
                strict = (col + ks) < row
                lg = _log_sigmoid_parts(z)
                lom = jnp.where(strict, jnp.minimum(-z, 0.0) - lg, 0.0)
                log_w = (jnp.minimum(z, 0.0) - lg) + (_suffix_sum(lom, mat) + c)
                w = jnp.where(strict, jnp.exp(log_w), 0.0)
                return (c + jnp.sum(lom, axis=1, keepdims=True), acc + d_acc, rem + d_rem, raw_next,
                        _split_weights(w))

            _, acc, rem, _, last = lax.fori_loop(0, n_kb, step, (jnp.zeros((tq, 1), F32), zeros_o, zeros_o,
                                                                 raw_logits(n_kb - 1), no_weights))
            d_acc, d_rem = _weighted_values(last, values(0))
            acc, rem = acc + d_acc, rem + d_rem
            o_ref[...] = acc.astype(o_ref.dtype)
            fine_ref[...] = acc + rem
        else:
            fc = fcol_ref[...] if decay else None

            def step(kb, carry):
                m, l, acc, rem, raw, prev = carry
                raw_next = raw_logits(jnp.minimum(kb + 1, n_kb - 1))
                d_acc, d_rem = _weighted_values(prev, values(jnp.maximum(kb - 1, 0)))
                ks = pl.multiple_of(kb * tk, tk)
                s = raw * scale
                if decay:
                    s = (s + fc) - frow_ref[:, pl.ds(ks, tk)]
                s = jnp.where((col + ks) <= row, s, MASKED)
                m_new = jnp.maximum(m, jnp.max(s, axis=1, keepdims=True))
                alpha = jnp.exp(m - m_new)
                p = jnp.exp(s - m_new)
                l = alpha * l + jnp.sum(p, axis=1, keepdims=True)
                return m_new, l, alpha * (acc + d_acc), alpha * (rem + d_rem), raw_next, _split_weights(p)

            m, l, acc, rem, _, last = lax.fori_loop(
                0, n_kb, step, (jnp.full((tq, 1), MASKED, F32), jnp.zeros((tq, 1), F32), zeros_o, zeros_o,
                                raw_logits(0), no_weights))
            d_acc, d_rem = _weighted_values(last, values(n_kb - 1))
            acc, rem = acc + d_acc, rem + d_rem
            inv_l = 1.0 / l
            o_ref[...] = (acc * inv_l).astype(o_ref.dtype)
            fine_ref[...] = (acc + rem) * inv_l
            lse_ref[...] = m + jnp.log(l)
        _run_hosted(*host_args, "finish")

    q_spec, k_spec, v_spec = _attn_specs(t_rows, tq, heads, dk, dv, q_off, k_off, v_off)
    stat_spec = pl.BlockSpec((None, tq, 1), lambda h, i: (h, i, 0))
    ins, in_specs = [q_arr, kt_arr, v_arr], [q_spec, k_spec, v_spec]
    if decay:
        ins += [fcol, frow]
        in_specs += [stat_spec, pl.BlockSpec((None, 1, t_rows), lambda h, i: (h, 0, 0))]
    o_spec = pl.BlockSpec((tq, dv), lambda h, i: (i, h))
    out_specs = [o_spec, o_spec]
    out_shape = [jax.ShapeDtypeStruct((t_rows, heads * dv), BF16), jax.ShapeDtypeStruct((t_rows, heads * dv), F32)]
    if not stick:
        out_specs.append(stat_spec)
        out_shape.append(jax.ShapeDtypeStruct((heads, t_rows, 1), F32))
    return tuple(pl.pallas_call(
        body, name=name, grid=(heads, nq), in_specs=in_specs + [ANY_SPEC] * n_host,
        out_specs=out_specs + [ANY_SPEC] * n_host,
        out_shape=out_shape + [_exchange_out_shape(kd, arr) for kd, arr in hosted],
        scratch_shapes=EXCHANGE_SCRATCH * n_host,
        compiler_params=_params("arbitrary" if n_host else "parallel", "arbitrary"),
    )(*ins, *[arr for _, arr in hosted]))


def _attn_bwd(kind, q_arr, k_arr, kt_arr, vt_arr, o_arr, do_arr, heads, dk, dv, scale, name, q_off=0, k_off=0, kt_off=0,
              vt_off=0, do_off=0, lse=None, fcol=None, frow=None, tq=256, hosted=()):
    t_rows = q_arr.shape[0]
    tk = tq
    nq = t_rows // tq
    stick = kind == "stick"
    decay = fcol is not None
    n_in = 6 + (0 if stick else 1) + (2 if decay else 0)
    n_out = 5 if decay else 3
    n_host = len(hosted)

    def body(*refs):
        q_ref, k_ref, kt_ref, vt_ref, o_ref, do_ref = refs[:6]
        lse_ref = None if stick else refs[6]
        fcol_ref, frow_ref = (refs[7], refs[8]) if decay else (None, None)
        base = n_in + n_host
        dq_ref, dk_ref, dv_ref = refs[base:base + 3]
        dfcol_ref, dfrow_ref = (refs[base + 3], refs[base + 4]) if decay else (None, None)
        host_args = (hosted, refs[n_in:base], refs[base + n_out:base + n_out + n_host], refs[base + n_out + n_host:],
                     pl.program_id(0) * nq + pl.program_id(1), heads * nq)
        _run_hosted(*host_args, "start")
        qi = pl.program_id(1)

        @pl.when(qi == 0)
        def _():
            dk_ref[...] = jnp.zeros_like(dk_ref)
            dv_ref[...] = jnp.zeros_like(dv_ref)
            if decay:
                dfrow_ref[...] = jnp.zeros_like(dfrow_ref)

        q = q_ref[...]
        do = do_ref[...]
        delta = jnp.sum(do.astype(F32) * o_ref[...], axis=1, keepdims=True)
        row, col = _causal_iotas(qi, tq, tk)
        n_kb = qi + 1

        no_pair = (jnp.zeros((tq, tk), BF16), jnp.zeros((tq, tk), BF16))

        def accumulate(kb, pair):
            at = pl.ds(pl.multiple_of(kb * tk, tk), tk)
            dk_ref[at, :] += lax.dot_general(pair[0], q, TN_DIMS, preferred_element_type=F32)
            dv_ref[at, :] += lax.dot_general(pair[1], do, TN_DIMS, preferred_element_type=F32)
            return lax.dot_general(pair[0], k_ref[at, :], NN_DIMS, preferred_element_type=F32)

        def raw_products(kb):
            at = pl.ds(pl.multiple_of(kb * tk, tk), tk)
            return (lax.dot_general(q, kt_ref[:, at], NN_DIMS, preferred_element_type=F32),
                    lax.dot_general(do, vt_ref[:, at], NN_DIMS, preferred_element_type=F32))

        if stick:
            mat_ex = _suffix_matrix(tk, inclusive=False)
            mat_in = _suffix_matrix(tk, inclusive=True)

            def step(i, carry):
                c, gs, dq, (raw, dw), prev = carry
                raw_next = raw_products(jnp.maximum(n_kb - 2 - i, 0))
                dq = dq + accumulate(jnp.minimum(n_kb - i, n_kb - 1), prev)
                ks = pl.multiple_of((n_kb - 1 - i) * tk, tk)
                z = raw * scale
                strict = (col + ks) < row
                lg = _log_sigmoid_parts(z)
                log_beta = jnp.minimum(z, 0.0) - lg
                log_omb = jnp.minimum(-z, 0.0) - lg
                lom = jnp.where(strict, log_omb, 0.0)
                w = jnp.where(strict, jnp.exp(log_beta + (_suffix_sum(lom, mat_ex) + c)), 0.0)
                g = w * dw
                g_before = delta - (gs + _suffix_sum(g, mat_in))
                dz = jnp.where(strict, g * jnp.exp(log_omb) - g_before * jnp.exp(log_beta), 0.0)
                return (c + jnp.sum(lom, axis=1, keepdims=True), gs + jnp.sum(g, axis=1, keepdims=True), dq,
                        raw_next, ((dz * scale).astype(BF16), w.astype(BF16)))

            zero = jnp.zeros((tq, 1), F32)
            _, _, dq, _, last = lax.fori_loop(0, n_kb, step, (zero, zero, jnp.zeros((tq, dk), F32),
                                                              raw_products(n_kb - 1), no_pair))
            dq = dq + accumulate(0, last)
        else:
            lse_v = lse_ref[...]
            fc = fcol_ref[...] if decay else None

            def step(kb, carry):
                dq, row_sum, (raw, dp), prev = carry
                raw_next = raw_products(jnp.minimum(kb + 1, n_kb - 1))
                dq = dq + accumulate(jnp.maximum(kb - 1, 0), prev)
                ks = pl.multiple_of(kb * tk, tk)
                s = raw * scale
                if decay:
                    s = (s + fc) - frow_ref[:, pl.ds(ks, tk)]
                p = jnp.where((col + ks) <= row, jnp.exp(s - lse_v), 0.0)
                ds = p * (dp - delta)
                if decay:
                    dfrow_ref[:, pl.ds(ks, tk)] += jnp.sum(ds, axis=0, keepdims=True)
                    row_sum = row_sum + jnp.sum(ds, axis=1, keepdims=True)
                return dq, row_sum, raw_next, ((ds * scale).astype(BF16), p.astype(BF16))

            dq, row_sum, _, last = lax.fori_loop(0, n_kb, step, (jnp.zeros((tq, dk), F32), jnp.zeros((tq, 1), F32),
                                                                raw_products(0), no_pair))
            dq = dq + accumulate(n_kb - 1, last)
            if decay:
                dfcol_ref[...] = row_sum
        dq_ref[...] = dq
        _run_hosted(*host_args, "finish")

    q_spec, kt_spec, _ = _attn_specs(t_rows, tq, heads, dk, dv, q_off, kt_off, 0)
    stat_spec = pl.BlockSpec((None, tq, 1), lambda h, i: (h, i, 0))
    frow_spec = pl.BlockSpec((None, 1, t_rows), lambda h, i: (h, 0, 0))
    ins = [q_arr, k_arr, kt_arr, vt_arr, o_arr, do_arr]
    in_specs = [q_spec, pl.BlockSpec((t_rows, dk), lambda h, i: (0, k_off + h)), kt_spec,
                pl.BlockSpec((dv, t_rows), lambda h, i: (vt_off + h, 0)), pl.BlockSpec((tq, dv), lambda h, i: (i, h)),
                pl.BlockSpec((tq, dv), lambda h, i: (i, do_off + h))]
    if not stick:
        ins.append(lse)
        in_specs.append(stat_spec)
    if decay:
        ins += [fcol, frow]
        in_specs += [stat_spec, frow_spec]
    out_specs = [pl.BlockSpec((tq, dk), lambda h, i: (i, h)), pl.BlockSpec((t_rows, dk), lambda h, i: (0, h)),
                 pl.BlockSpec((t_rows, dv), lambda h, i: (0, h))]
    out_shape = [jax.ShapeDtypeStruct((t_rows, heads * dk), F32), jax.ShapeDtypeStruct((t_rows, heads * dk), F32),
                 jax.ShapeDtypeStruct((t_rows, heads * dv), F32)]
    if decay:
        out_specs += [stat_spec, frow_spec]
        out_shape += [jax.ShapeDtypeStruct((heads, t_rows, 1), F32), jax.ShapeDtypeStruct((heads, 1, t_rows), F32)]
    return pl.pallas_call(
        body, name=name, grid=(heads, nq), in_specs=in_specs + [ANY_SPEC] * n_host,
        out_specs=out_specs + [ANY_SPEC] * n_host,
        out_shape=out_shape + [_exchange_out_shape(kd, arr) for kd, arr in hosted],
        scratch_shapes=EXCHANGE_SCRATCH * n_host,
        compiler_params=_params("arbitrary" if n_host else "parallel", "arbitrary"),
    )(*ins, *[arr for _, arr in hosted])


def _prefix_matrix(reverse):
    j = lax.broadcasted_iota(jnp.int32, (LANES, LANES), 0)
    s = lax.broadcasted_iota(jnp.int32, (LANES, LANES), 1)
    return jnp.where((j >= s) if reverse else (j <= s), 1.0, 0.0).astype(BF16)


def _chunk_cumsum(x, mat):
    return sum(lax.dot_general(part, mat, NN_DIMS, preferred_element_type=F32) for part in _split3(x))


def _gate_fwd(logit_t, bias_col):
    heads, t_rows = logit_t.shape

    def body(x_ref, b_ref, out_ref):
        mat = _prefix_matrix(reverse=False)

        def step(ci, carry):
            cs = pl.multiple_of(ci * LANES, LANES)
            pre = x_ref[:, pl.ds(cs, LANES)] + b_ref[...]
            log_f = jnp.minimum(pre, 0.0) - _log_sigmoid_parts(pre)
            out_ref[:, pl.ds(cs, LANES)] = _chunk_cumsum(log_f, mat) + carry
            return carry + jnp.sum(log_f, axis=1, keepdims=True)

        lax.fori_loop(0, t_rows // LANES, step, jnp.zeros((heads, 1), F32))

    return pl.pallas_call(body, name="gate_fwd", out_shape=jax.ShapeDtypeStruct((heads, t_rows), F32),
                          compiler_params=pltpu.CompilerParams(vmem_limit_bytes=VMEM_LIMIT))(logit_t, bias_col)


def _gate_bwd(dcum_t, logit_t, bias_col):
    heads, t_rows = logit_t.shape
    n_chunks = t_rows // LANES

    def body(d_ref, x_ref, b_ref, dx_ref, db_ref):
        mat = _prefix_matrix(reverse=True)

        def step(i, carry):
            tail, db = carry
            cs = pl.multiple_of((n_chunks - 1 - i) * LANES, LANES)
            d = d_ref[:, pl.ds(cs, LANES)]
            d_log_f = _chunk_cumsum(d, mat) + tail
            pre = x_ref[:, pl.ds(cs, LANES)] + b_ref[...]
            e = jnp.exp(-jnp.abs(pre))
            d_pre = d_log_f * (jnp.where(pre >= 0.0, e, 1.0) / (1.0 + e))
            dx_ref[:, pl.ds(cs, LANES)] = d_pre
            return tail + jnp.sum(d, axis=1, keepdims=True), db + jnp.sum(d_pre, axis=1, keepdims=True)

        zero = jnp.zeros((heads, 1), F32)
        _, db = lax.fori_loop(0, n_chunks, step, (zero, zero))
        db_ref[...] = db

    return pl.pallas_call(body, name="gate_bwd",
                          out_shape=(jax.ShapeDtypeStruct((heads, t_rows), F32), jax.ShapeDtypeStruct((heads, 1), F32)),
                          compiler_params=pltpu.CompilerParams(vmem_limit_bytes=VMEM_LIMIT))(dcum_t, logit_t, bias_col)


def _norm_fwd(x, g, name):
    return _rows_call(lambda xv, gv: _rms_fwd(xv, gv), name, [x], [g], [(x.shape[1], BF16)])


def _norm_bwd(x, g, dh, dres, name):
    def fn(xv, dhv, dresv, gv):
        dx, dg = _rms_bwd(xv, gv, dhv)
        dx = dresv + dx
        return dx, dx, dg
    return _rows_call(fn, name, [x, dh, dres], [g], [(x.shape[1], F32), (x.shape[1], BF16)], [((1, x.shape[1]), F32)])


def _loss_fwd_bwd(y, target):
    d_model = y.shape[1]

    def fn(yv, tv):
        err = yv - tv
        dy = err * (1.0 / d_model)
        return dy, dy, jnp.sum(jnp.sum(err * err, axis=1, keepdims=True), axis=0, keepdims=True)
    return _rows_call(fn, "loss", [y, target], [], [(d_model, F32), (d_model, BF16)], [((1, 1), F32)])


def _heads_apply(fn, n_heads, width, *tiles):
    return [fn(*[t[:, h * width:(h + 1) * width] for t in tiles]) for h in range(n_heads)]


def _fox_norm_fwd(pb, gq, gk, heads):
    width = heads * HEAD_DIM

    def fn(qk, gqv, gkv):
        q = jnp.concatenate(_heads_apply(lambda t: _rms_fwd(t, gqv), heads, HEAD_DIM, qk[:, :width]), axis=1)
        k = jnp.concatenate(_heads_apply(lambda t: _rms_fwd(t, gkv), heads, HEAD_DIM, qk[:, width:]), axis=1)
        return q, k
    return _rows_call(fn, "fox_norm_fwd", [(pb, 2 * width, 0)], [gq, gk], [(width, BF16), (width, BF16)])


def _fox_norm_bwd(pb, gq, gk, dq, dk, heads):
    width = heads * HEAD_DIM

    def fn(qk, dqv, dkv, gqv, gkv):
        res_q = _heads_apply(lambda t, d: _rms_bwd(t, gqv, d), heads, HEAD_DIM, qk[:, :width], dqv)
        res_k = _heads_apply(lambda t, d: _rms_bwd(t, gkv, d), heads, HEAD_DIM, qk[:, width:], dkv)
        dqk = jnp.concatenate([r[0] for r in res_q] + [r[0] for r in res_k], axis=1)
        return dqk, sum(r[1] for r in res_q), sum(r[1] for r in res_k)
    return _rows_call(fn, "fox_norm_bwd", [(pb, 2 * width, 0), dq, dk], [gq, gk], [(2 * width, BF16)],
                      [((1, HEAD_DIM), F32), ((1, HEAD_DIM), F32)])


def _lora_norm_fwd(down, gq, gkv, rank):
    def fn(dv, gqv, gkvv):
        return _rms_fwd(dv[:, :rank], gqv), _rms_fwd(dv[:, rank:], gkvv)
    return _rows_call(fn, "lora_norm_fwd", [(down, 2 * rank, 0)], [gq, gkv], [(rank, BF16), (rank, BF16)])


def _lora_norm_bwd(down, gq, gkv, dcq, dckv, dkpe, rank):
    def fn(dv, dcqv, dckvv, dkpev, gqv, gkvv):
        dxq, dgq = _rms_bwd(dv[:, :rank], gqv, dcqv)
        dxkv, dgkv = _rms_bwd(dv[:, rank:], gkvv, dckvv)
        return jnp.concatenate([dxq, dxkv, dkpev], axis=1), dgq, dgkv
    return _rows_call(fn, "lora_norm_bwd", [(down, 2 * rank, 0), dcq, dckv, dkpe], [gq, gkv],
                      [(2 * rank + LANES, BF16)], [((1, rank), F32), ((1, rank), F32)])


def _rope_tables(pos_col, inv_freq, sin_sign):
    def fn(pos, invf, sign):
        ang = pos.astype(F32) * invf
        return jnp.cos(ang) * jnp.abs(sign), jnp.sin(ang) * sign
    return _rows_call(fn, "rope_tables", [pos_col], [inv_freq, sin_sign], [(LANES, F32), (LANES, F32)])


def _mla_prep_fwd(q_raw, kv, down, kpe_block, qg, kg, cos_t, sin_s):
    def fn(qv, kvv, kpe, cosv, sinv, qgv, kgv):
        qs, ks, vs = [], [], []
        for h in range(MLA_HEADS):
            qn = _rms_fwd(qv[:, h * MLA_PAD_DIM:(h + 1) * MLA_PAD_DIM], qgv, MLA_QK_DIM)
            qs += [qn[:, :HEAD_DIM], _rope_fwd(qn[:, HEAD_DIM:], cosv, sinv)]
            k_full = jnp.concatenate([kvv[:, h * MLA_PAD_DIM:h * MLA_PAD_DIM + HEAD_DIM], kpe], axis=1)
            kn = _rms_fwd(k_full, kgv, MLA_QK_DIM)
            ks += [kn[:, :HEAD_DIM], _rope_fwd(kn[:, HEAD_DIM:], cosv, sinv)]
            vs.append(kvv[:, h * MLA_PAD_DIM + HEAD_DIM:(h + 1) * MLA_PAD_DIM])
        return jnp.concatenate(qs, axis=1), jnp.concatenate(ks, axis=1), jnp.concatenate(vs, axis=1)
    wide = MLA_HEADS * MLA_PAD_DIM
    return _rows_call(fn, "mla_prep_fwd", [q_raw, kv, (down, LANES, kpe_block), cos_t, sin_s], [qg, kg],
                      [(wide, BF16), (wide, BF16), (MLA_HEADS * HEAD_DIM, BF16)], tile=128)


def _mla_prep_bwd(q_raw, kv, down, kpe_block, qg, kg, cos_t, sin_s, dq, dk, dv):
    def fn(qv, kvv, kpe, cosv, sinv, dqv, dkv, dvv, qgv, kgv):
        dqs, dkvs = [], []
        dkpe = jnp.zeros_like(kpe)
        dqg = jnp.zeros_like(qgv)
        dkg = jnp.zeros_like(kgv)
        for h in range(MLA_HEADS):
            lo, hi = h * MLA_PAD_DIM, (h + 1) * MLA_PAD_DIM
            dqn = jnp.concatenate([dqv[:, lo:lo + HEAD_DIM], _rope_bwd(dqv[:, lo + HEAD_DIM:hi], cosv, sinv)], axis=1)
            dqh, dg = _rms_bwd(qv[:, lo:hi], qgv, dqn, MLA_QK_DIM)
            dqs.append(dqh)
            dqg = dqg + dg
            k_full = jnp.concatenate([kvv[:, lo:lo + HEAD_DIM], kpe], axis=1)
            dkn = jnp.concatenate([dkv[:, lo:lo + HEAD_DIM], _rope_bwd(dkv[:, lo + HEAD_DIM:hi], cosv, sinv)], axis=1)
            dkh, dg = _rms_bwd(k_full, kgv, dkn, MLA_QK_DIM)
            dkg = dkg + dg
            dkpe = dkpe + dkh[:, HEAD_DIM:]
            dkvs += [dkh[:, :HEAD_DIM], dvv[:, h * HEAD_DIM:(h + 1) * HEAD_DIM]]
        return jnp.concatenate(dqs, axis=1), jnp.concatenate(dkvs, axis=1), dkpe, dqg, dkg
    wide = MLA_HEADS * MLA_PAD_DIM
    return _rows_call(fn, "mla_prep_bwd", [q_raw, kv, (down, LANES, kpe_block), cos_t, sin_s, dq, dk, dv], [qg, kg],
                      [(wide, BF16), (wide, BF16), (LANES, F32)], [((1, MLA_PAD_DIM), F32), ((1, MLA_PAD_DIM), F32)],
                      tile=128)


def _sqrelu_up(acc):
    return acc, jnp.square(jnp.maximum(acc, 0.0))


def _sqrelu_grad(acc, u):
    return (acc * (2.0 * jnp.maximum(u, 0.0)),)


def _mlp_fwd(x, g, w_up, w_down, tag):
    h = _norm_fwd(x, g, f"mlp_norm_fwd{tag}")
    u, a = _matmul(h, w_up, "nn", f"mlp_up{tag}", (F32, BF16), _sqrelu_up)
    return _matmul(a, w_down, "nn", f"mlp_down{tag}", (F32,), _add_residual, (x,)), (h, u, a)


def _mlp_bwd(x, g, w_up, w_down, saved, dy, dy16, tag):
    h, u, a = saved
    dw_down = _matmul(a, dy16, "tn", f"mlp_dwdown{tag}", (BF16,))
    du = _matmul(dy16, w_down, "nt", f"mlp_du{tag}", (BF16,), _sqrelu_grad, (u,))
    dw_up = _matmul(h, du, "tn", f"mlp_dwup{tag}", (BF16,))
    dh = _matmul(du, w_up, "nt", f"mlp_dh{tag}")
    dx, dx16, dg = _norm_bwd(x, g, dh, dy, f"mlp_norm_bwd{tag}")
    return dx, dx16, dg, dw_up, dw_down


def _local_step(x, pos_col, target, w, dist=None):
    w = dict(w)
    hs = w["w_a"].shape[1] // (4 * HEAD_DIM)
    sb_w = hs * HEAD_DIM
    grads, received = {}, {}

    def gather_in(group):
        return [("gather", blk) for blk in dist["blocks"][group]] if dist else []

    def exchange_in(group):
        return [("all_to_all", slots) for slots in dist["slots_of"](group, grads)] if dist else []

    h0 = _norm_fwd(x, w["ln_mix0"], "mix0_norm_fwd")
    pa = _matmul(h0, w["w_a"], "nn", "in_proj_a", (BF16,))
    pb = _matmul(h0, w["w_b"], "nn", "in_proj_b")
    pat = pa[:, sb_w:].T
    o_sb, o_sb_fine, *got = _attn_fwd("stick", pa, pat, pa, hs, HEAD_DIM, HEAD_DIM, HEAD_DIM ** -0.5, "stick_fwd",
                                      q_off=0, k_off=0, v_off=2 * hs, hosted=gather_in("mlp0"))
    if dist:
        w.update(dist["weights_of"]("mlp0", got))
    logit_t = pb[:, 2 * sb_w:2 * sb_w + hs].T
    bias_col = w["b_f"][0, :hs].reshape(hs, 1)
    f_cum = _gate_fwd(logit_t, bias_col)
    f_col, f_row = f_cum[:, :, None], f_cum[:, None, :]
    qf, kf = _fox_norm_fwd(pb, w["fox_q_g"], w["fox_k_g"], hs)
    kft = kf.T
    o_fx, o_fx_fine, lse_fx, *got = _attn_fwd("softmax", qf, kft, pa, hs, HEAD_DIM, HEAD_DIM, HEAD_DIM ** -0.5,
                                              "fox_fwd", v_off=3 * hs, fcol=f_col, frow=f_row,
                                              hosted=gather_in("layer1"))
    if dist:
        w.update(dist["weights_of"]("layer1", got))
    o0 = jnp.concatenate([o_sb, o_fx], axis=1)
    x1 = _matmul(o0, w["w_o0"], "nn", "out_proj0", (F32,), _add_residual, (x,))
    x2, mlp0 = _mlp_fwd(x1, w["ln_mlp0"], w["w_up0"], w["w_dn0"], "0")

    rank = w["w_uq"].shape[0]
    h2 = _norm_fwd(x2, w["ln_mix1"], "mix1_norm_fwd")
    down = _matmul(h2, w["w_down"], "nn", "mla_down")
    cqn, ckvn = _lora_norm_fwd(down, w["q_a_g"], w["kv_a_g"], rank)
    q_raw = _matmul(cqn, w["w_uq"], "nn", "mla_uq")
    kv = _matmul(ckvn, w["w_ukv"], "nn", "mla_ukv")
    cos_t, sin_s = _rope_tables(pos_col, w["inv_freq"], w["sin_sign"])
    kpe_block = 2 * rank // LANES
    qm, km, vm = _mla_prep_fwd(q_raw, kv, down, kpe_block, w["mla_q_g"], w["mla_k_g"], cos_t, sin_s)
    kmt, vmt = km.T, vm.T
    o_m, o_m_fine, lse_m = _attn_fwd("softmax", qm, kmt, vm, MLA_HEADS, MLA_PAD_DIM, HEAD_DIM, MLA_QK_DIM ** -0.5,
                                     "mla_fwd")
    x3 = _matmul(o_m, w["w_o1"], "nn", "out_proj1", (F32,), _add_residual, (x2,))
    x4, mlp1 = _mlp_fwd(x3, w["ln_mlp1"], w["w_up1"], w["w_dn1"], "1")

    dy, dy16, sq_err = _loss_fwd_bwd(x4, target)

    dx3, dx3_16, grads["ln_mlp1"], grads["w_up1"], grads["w_dn1"] = _mlp_bwd(
        x3, w["ln_mlp1"], w["w_up1"], w["w_dn1"], mlp1, dy, dy16, "1")
    grads["w_o1"] = _matmul(o_m, dx3_16, "tn", "dw_o1", (BF16,))
    do_m = _matmul(dx3_16, w["w_o1"], "nt", "do_mla", (BF16,))
    dqm, dkm, dvm, *got = _attn_bwd("softmax", qm, km, kmt, vmt, o_m_fine, do_m, MLA_HEADS, MLA_PAD_DIM, HEAD_DIM,
                                    MLA_QK_DIM ** -0.5, "mla_bwd", lse=lse_m, hosted=exchange_in("mlp1"))
    received["mlp1"] = got
    dq_raw, dkv, dkpe, grads["mla_q_g"], grads["mla_k_g"] = _mla_prep_bwd(
        q_raw, kv, down, kpe_block, w["mla_q_g"], w["mla_k_g"], cos_t, sin_s, dqm, dkm, dvm)
    grads["w_uq"] = _matmul(cqn, dq_raw, "tn", "dw_uq", (BF16,))
    grads["w_ukv"] = _matmul(ckvn, dkv, "tn", "dw_ukv", (BF16,))
    dcqn = _matmul(dq_raw, w["w_uq"], "nt", "d_cq")
    dckvn = _matmul(dkv, w["w_ukv"], "nt", "d_ckv")
    ddown, grads["q_a_g"], grads["kv_a_g"] = _lora_norm_bwd(down, w["q_a_g"], w["kv_a_g"], dcqn, dckvn, dkpe, rank)
    grads["w_down"] = _matmul(h2, ddown, "tn", "dw_down", (BF16,))
    dh2 = _matmul(ddown, w["w_down"], "nt", "d_h2")
    dx2, dx2_16, grads["ln_mix1"] = _norm_bwd(x2, w["ln_mix1"], dh2, dx3, "mix1_norm_bwd")

    dx1, dx1_16, grads["ln_mlp0"], grads["w_up0"], grads["w_dn0"] = _mlp_bwd(
        x1, w["ln_mlp0"], w["w_up0"], w["w_dn0"], mlp0, dx2, dx2_16, "0")
    grads["w_o0"] = _matmul(o0, dx1_16, "tn", "dw_o0", (BF16,))
    do0 = _matmul(dx1_16, w["w_o0"], "nt", "do_mix0", (BF16,))
    dq_sb, dk_sb, dv_sb, *got = _attn_bwd("stick", pa, pa, pat, pat, o_sb_fine, do0, hs, HEAD_DIM, HEAD_DIM,
                                          HEAD_DIM ** -0.5, "stick_bwd", q_off=0, k_off=hs, kt_off=0, vt_off=hs, do_off=0,
                                          hosted=exchange_in("mla"))
    received["mla"] = got
    dqf, dkf, dv_fx, ds_rows, ds_cols, *got = _attn_bwd(
        "softmax", qf, kf, kft, pat, o_fx_fine, do0, hs, HEAD_DIM, HEAD_DIM, HEAD_DIM ** -0.5, "fox_bwd", vt_off=2 * hs,
        do_off=hs, lse=lse_fx, fcol=f_col, frow=f_row, hosted=exchange_in("mlp0"))
    received["mlp0"] = got
    dqk_fx, grads["fox_q_g"], grads["fox_k_g"] = _fox_norm_bwd(pb, w["fox_q_g"], w["fox_k_g"], dqf, dkf, hs)
    dlogit_t, db_f = _gate_bwd(ds_rows[:, :, 0] - ds_cols[:, 0, :], logit_t, bias_col)
    grads["b_f"] = db_f.reshape(1, hs)
    dpa = jnp.concatenate([dq_sb.astype(BF16), dk_sb.astype(BF16), dv_sb.astype(BF16), dv_fx.astype(BF16)], axis=1)
    dlogit_pad = jnp.pad(dlogit_t.T.astype(BF16), ((0, 0), (0, pb.shape[1] - 2 * sb_w - hs)))
    dpb = jnp.concatenate([dqk_fx, dlogit_pad], axis=1)
    grads["w_a"] = _matmul(h0, dpa, "tn", "dw_a", (BF16,))
    grads["w_b"] = _matmul(h0, dpb, "tn", "dw_b", (BF16,))
    dh0 = _matmul(dpb, w["w_b"], "nt", "d_h0_b")
    dh0 = _matmul(dpa, w["w_a"], "nt", "d_h0_a", (F32,), _add_residual, (dh0,))
    grad_x, _, grads["ln_mix0"] = _norm_bwd(x, w["ln_mix0"], dh0, dx1, "mix0_norm_bwd")
    return sq_err, grad_x, grads, received


PIECES = {
    "sf_w_in": ("sf_w_in", 0, 1), "sf_w_o": ("sf_w_o", 0, 0), "mla_w_down": ("mla_w_down", 0, 0),
    "mla_w_uq": ("mla_w_uq", 0, 1), "mla_w_ukv": ("mla_w_ukv", 0, 1), "mla_w_o": ("mla_w_o", 0, 0),
    "mlp_w_up0": ("mlp_w_up", 0, 1), "mlp_w_up1": ("mlp_w_up", 1, 1),
    "mlp_w_down0": ("mlp_w_down", 0, 0), "mlp_w_down1": ("mlp_w_down", 1, 0),
}
GROUPS = {
    "mix0": ["sf_w_in", "sf_w_o"], "mlp0": ["mlp_w_up0", "mlp_w_down0"],
    "mla": ["mla_w_down", "mla_w_uq", "mla_w_ukv", "mla_w_o"], "mlp1": ["mlp_w_up1", "mlp_w_down1"],
}
GROUPS["layer1"] = GROUPS["mla"] + GROUPS["mlp1"]
SMALL = ["ln_mix_g", "ln_mlp_g", "sf_b_f", "fox_q_g", "fox_k_g", "mla_q_a_g", "mla_kv_a_g", "mla_q_g", "mla_k_g"]
ALL_W = ["ln_mix_g", "ln_mlp_g", "sf_w_in", "sf_b_f", "fox_q_g", "fox_k_g", "sf_w_o", "mla_w_down", "mla_q_a_g",
         "mla_kv_a_g", "mla_w_uq", "mla_w_ukv", "mla_q_g", "mla_k_g", "mla_w_o", "mlp_w_up", "mlp_w_down"]


def _weights_mix0(full, small):
    w_in = full["sf_w_in"]
    d_model = w_in.shape[0]
    n_fx = small["sf_b_f"].shape[1]
    sb_w = (w_in.shape[1] - n_fx) // 6
    cols = lambda i: w_in[:, i * sb_w:(i + 1) * sb_w]
    w_a = jnp.concatenate([cols(0), cols(1), cols(2), cols(5)], axis=1)
    w_b = jnp.concatenate([cols(3), cols(4), w_in[:, 6 * sb_w:], jnp.zeros((d_model, LANES - n_fx), w_in.dtype)], axis=1)
    half = ROPE_DIM // 2
    inv_freq = ROPE_THETA ** (-jnp.arange(half, dtype=F32) / half)
    zeros64 = jnp.zeros((ROPE_DIM,), F32)
    pad256 = lambda g: jnp.pad(g, ((0, 0), (0, MLA_PAD_DIM - MLA_QK_DIM)))
    pad_lanes = lambda g: jnp.pad(g, ((0, 0), (0, LANES - g.shape[1])))
    return dict(
        ln_mix0=small["ln_mix_g"][0:1], ln_mix1=small["ln_mix_g"][1:2],
        ln_mlp0=small["ln_mlp_g"][0:1], ln_mlp1=small["ln_mlp_g"][1:2],
        w_a=w_a, w_b=w_b, b_f=pad_lanes(small["sf_b_f"]), fox_q_g=small["fox_q_g"], fox_k_g=small["fox_k_g"],
        w_o0=full["sf_w_o"], q_a_g=small["mla_q_a_g"], kv_a_g=small["mla_kv_a_g"],
        mla_q_g=pad256(small["mla_q_g"]), mla_k_g=pad256(small["mla_k_g"]),
        inv_freq=jnp.concatenate([inv_freq, inv_freq, zeros64]).reshape(1, LANES),
        sin_sign=jnp.concatenate([-jnp.ones((half,), F32), jnp.ones((half,), F32), zeros64]).reshape(1, LANES),
    )


def _weights_mlp0(full):
    return dict(w_up0=full["mlp_w_up0"], w_dn0=full["mlp_w_down0"])


def _weights_layer1(full):
    rank = full["mla_w_uq"].shape[0]
    w_uq = full["mla_w_uq"].reshape(rank, MLA_HEADS, MLA_QK_DIM)
    w_uq = jnp.pad(w_uq, ((0, 0), (0, 0), (0, MLA_PAD_DIM - MLA_QK_DIM))).reshape(rank, MLA_HEADS * MLA_PAD_DIM)
    return dict(w_down=jnp.pad(full["mla_w_down"], ((0, 0), (0, LANES - ROPE_DIM))), w_uq=w_uq,
                w_ukv=full["mla_w_ukv"], w_o1=full["mla_w_o"], w_up1=full["mlp_w_up1"], w_dn1=full["mlp_w_down1"])


WEIGHTS_OF = {"mlp0": _weights_mlp0, "layer1": _weights_layer1}


def _piece_grad(g, piece):
    if piece == "sf_w_in":
        n_fx = g["b_f"].shape[1]
        ga, gb = g["w_a"], g["w_b"]
        sb_w = ga.shape[1] // 4
        ca = lambda i: ga[:, i * sb_w:(i + 1) * sb_w]
        return jnp.concatenate([ca(0), ca(1), ca(2), gb[:, :sb_w], gb[:, sb_w:2 * sb_w], ca(3),
                                gb[:, 2 * sb_w:2 * sb_w + n_fx]], axis=1)
    if piece == "mla_w_uq":
        rank = g["w_uq"].shape[0]
        return g["w_uq"].reshape(rank, MLA_HEADS, MLA_PAD_DIM)[:, :, :MLA_QK_DIM].reshape(rank, MLA_HEADS * MLA_QK_DIM)
    if piece == "mla_w_down":
        return g["w_down"][:, :g["w_down"].shape[1] - (LANES - ROPE_DIM)]
    return g[{"sf_w_o": "w_o0", "mla_w_ukv": "w_ukv", "mla_w_o": "w_o1", "mlp_w_up0": "w_up0", "mlp_w_up1": "w_up1",
              "mlp_w_down0": "w_dn0", "mlp_w_down1": "w_dn1"}[piece]]


def _small_grads(g):
    return {
        "ln_mix_g": jnp.concatenate([g["ln_mix0"], g["ln_mix1"]], axis=0),
        "ln_mlp_g": jnp.concatenate([g["ln_mlp0"], g["ln_mlp1"]], axis=0),
        "sf_b_f": g["b_f"], "fox_q_g": g["fox_q_g"], "fox_k_g": g["fox_k_g"],
        "mla_q_a_g": g["q_a_g"], "mla_kv_a_g": g["kv_a_g"],
        "mla_q_g": g["mla_q_g"][:, :MLA_QK_DIM], "mla_k_g": g["mla_k_g"][:, :MLA_QK_DIM],
    }


PACK_TILE = 1024


def _as_rows(a, row_multiple=16):
    flat = a.reshape(-1)
    rows = -(-flat.shape[0] // LANES)
    rows = -(-rows // row_multiple) * row_multiple
    return jnp.pad(flat, (0, rows * LANES - flat.shape[0])).reshape(rows, LANES)


def _pack_rows(parts, axis, dtype, row_multiple=PACK_TILE, spare_rows=0):
    used = sum(p.shape[axis] for p in parts)
    shape = list(parts[0].shape)
    shape[axis] = -(-used // row_multiple) * row_multiple + spare_rows - used
    return jnp.concatenate([p.astype(dtype) for p in parts] + [jnp.ones(shape, dtype)], axis=axis)


def _unshard(stack, axis):
    moved = jnp.moveaxis(stack, 0, axis)
    shape = list(stack.shape[1:])
    shape[axis] *= N_DEV
    return moved.reshape(shape)


def _shard_stack(full, axis):
    shape = list(full.shape)
    shape[axis:axis + 1] = [N_DEV, shape[axis] // N_DEV]
    return jnp.moveaxis(full.reshape(shape), axis, 0)


OPT_TILE_ELEMS = 128 * 1024


def _row_tile(rows, cols):
    best = 16
    for t in range(16, rows + 1, 16):
        if rows % t == 0 and t * cols <= OPT_TILE_ELEMS:
            best = t
    assert rows % best == 0
    return best


def _cast_bf16(a, name):
    return _rows_call(lambda v: v, name, [a], [], [(a.shape[1], BF16)], tile=_row_tile(*a.shape))


def _adam_math(w, g, m, v):
    m = ADAM_B1 * m + (1.0 - ADAM_B1) * g
    v = ADAM_B2 * v + (1.0 - ADAM_B2) * jnp.square(g)
    m_hat = m / (1.0 - ADAM_B1 ** ADAM_STEP)
    v_hat = v / (1.0 - ADAM_B2 ** ADAM_STEP)
    delta = -ADAM_LR * (m_hat / (jnp.sqrt(v_hat) + ADAM_EPS) + ADAM_WD * w)
    return delta, m, v


def _adam_big(recvs, w, m, v, name):
    layers, rows, cols = w.shape
    tile = _row_tile(rows, cols)
    n_tiles = rows // tile

    def body(*refs):
        recv_refs = refs[:layers]
        w_ref, m_ref, v_ref, g_ref, d_ref, nm_ref, nv_ref = refs[layers:]
        layer = pl.program_id(0)

        def total(r_ref):
            acc = r_ref[0].astype(F32)
            for s in range(1, N_DEV):
                acc = acc + r_ref[s].astype(F32)
            return acc

        g = total(recv_refs[0])
        for j in range(1, layers):
            g = jnp.where(layer == j, total(recv_refs[j]), g)
        delta, nm, nv = _adam_math(w_ref[...], g, m_ref[...], v_ref[...])
        g_ref[...] = g
        d_ref[...] = delta
        nm_ref[...] = nm
        nv_ref[...] = nv

    def recv_spec(j):
        return pl.BlockSpec((N_DEV, tile, cols),
                            lambda l, i: (0, jnp.where(l == j, i, jnp.where(l < j, 0, n_tiles - 1)), 0))

    spec = pl.BlockSpec((None, tile, cols), lambda l, i: (l, i, 0))
    out = jax.ShapeDtypeStruct(w.shape, F32)
    return pl.pallas_call(
        body, name=name, grid=(layers, n_tiles), in_specs=[recv_spec(j) for j in range(layers)] + [spec] * 3,
        out_specs=[spec] * 4, out_shape=[out] * 4, compiler_params=_params("arbitrary", "arbitrary"),
    )(*recvs, w, m, v)


def _sum_slots(gathered):
    rows = gathered.shape[1]

    def body(r_ref, o_ref):
        acc = r_ref[0]
        for s in range(1, N_DEV):
            acc = acc + r_ref[s]
        o_ref[...] = acc

    return pl.pallas_call(body, name="sum_small", out_shape=jax.ShapeDtypeStruct((rows, LANES), F32))(gathered)


def _adam_small(w, g, m, v):
    def fn(wv, gv, mv, vv):
        return _adam_math(wv, gv, mv, vv)
    return _rows_call(fn, "adam_small", [w, g, m, v], [], [(LANES, F32)] * 3, tile=w.shape[0])


def kernel(x, positions, ln_mix_g, ln_mlp_g, sf_w_in, sf_b_f, fox_q_g, fox_k_g, sf_w_o, mla_w_down, mla_q_a_g, mla_kv_a_g, mla_w_uq, mla_w_ukv, mla_q_g, mla_k_g, mla_w_o, mlp_w_up, mlp_w_down, loss_target, m_ln_mix_g, m_ln_mlp_g, m_sf_w_in, m_sf_b_f, m_fox_q_g, m_fox_k_g, m_sf_w_o, m_mla_w_down, m_mla_q_a_g, m_mla_kv_a_g, m_mla_w_uq, m_mla_w_ukv, m_mla_q_g, m_mla_k_g, m_mla_w_o, m_mlp_w_up, m_mlp_w_down, v_ln_mix_g, v_ln_mlp_g, v_sf_w_in, v_sf_b_f, v_fox_q_g, v_fox_k_g, v_sf_w_o, v_mla_w_down, v_mla_q_a_g, v_mla_kv_a_g, v_mla_w_uq, v_mla_w_ukv, v_mla_q_g, v_mla_k_g, v_mla_w_o, v_mlp_w_up, v_mlp_w_down):
    given = dict(locals())
    wts = {n: given[n] for n in ALL_W}
    mom = {n: given["m_" + n] for n in ALL_W}
    var = {n: given["v_" + n] for n in ALL_W}
    me = 4 * lax.axis_index("x") + 2 * lax.axis_index("y") + lax.axis_index("c")
    t_rows, d_model = x.shape[1], x.shape[2]
    big = sorted({name for name, _, _ in PIECES.values()})

    def whole_pieces(gathered, group):
        return {p: _unshard(s, PIECES[p][2]) for p, s in zip(GROUPS[group], gathered)}

    def slots_of(group, grads):
        return [_shard_stack(_piece_grad(grads, p), PIECES[p][2]) for p in GROUPS[group]]

    cast = {n: _cast_bf16(wts[n].reshape(-1, wts[n].shape[2]), f"cast_{n}").reshape(wts[n].shape) for n in big}
    blocks = {grp: [cast[PIECES[p][0]][PIECES[p][1]] for p in GROUPS[grp]] for grp in ("mix0", "mlp0", "layer1")}
    mix0 = whole_pieces(_exchange("gather", blocks["mix0"], "gather_mix0"), "mix0")
    gains, = _exchange("gather", [_as_rows(jnp.concatenate([mla_q_a_g, mla_kv_a_g], axis=1))], "gather_gains")
    lora_n = mla_q_a_g.shape[1]
    gains_flat = gains.reshape(N_DEV, -1)[:, :2 * lora_n]
    small = dict(ln_mix_g=ln_mix_g, ln_mlp_g=ln_mlp_g, sf_b_f=sf_b_f, fox_q_g=fox_q_g, fox_k_g=fox_k_g,
                 mla_q_a_g=gains_flat[:, :lora_n].reshape(1, -1), mla_kv_a_g=gains_flat[:, lora_n:].reshape(1, -1),
                 mla_q_g=mla_q_g, mla_k_g=mla_k_g)
    dist = dict(blocks=blocks, slots_of=slots_of,
                weights_of=lambda grp, gathered: WEIGHTS_OF[grp](whole_pieces(gathered, grp)))
    sq_err, grad_x, g, received = _local_step(x[0], positions.reshape(t_rows, 1), loss_target[0],
                                              _weights_mix0(mix0, small), dist)
    received["mix0"] = _exchange("all_to_all", slots_of("mix0", g), "exchange_mix0")

    recv_of = {p: r for grp in ("mlp1", "mla", "mlp0", "mix0") for p, r in zip(GROUPS[grp], received[grp])}
    results = {kind: {} for kind in ("grad", "delta", "new_m", "new_v")}
    for n in big:
        layers = [p for _, p in sorted((layer, p) for p, (name, layer, _) in PIECES.items() if name == n)]
        outs = _adam_big([recv_of[p] for p in layers], wts[n], mom[n], var[n], f"adam_{n}")
        for kind, out in zip(("grad", "delta", "new_m", "new_v"), outs):
            results[kind][n] = out

    small_g = _small_grads(g)
    small_parts = [_as_rows(small_g[n], 8) for n in SMALL] + [_as_rows(sq_err, 8)]
    small_sum = _sum_slots(_exchange("gather", [_pack_rows(small_parts, 0, F32, 8, 8)], "gather_small_grads")[0])
    red, off = {}, 0
    for n, p in zip(SMALL + ["loss"], small_parts):
        red[n] = small_sum[off:off + p.shape[0]].reshape(-1)
        off += p.shape[0]
    loss = 0.5 * red["loss"][0] / d_model
    for n in SMALL:
        if n in ("mla_q_a_g", "mla_kv_a_g"):
            results["grad"][n] = lax.dynamic_slice(red[n], (me * lora_n,), (lora_n,)).reshape(wts[n].shape)
        else:
            results["grad"][n] = red[n][:wts[n].size].reshape(wts[n].shape)
    pack_small = lambda d: jnp.concatenate([_as_rows(d[n], 8) for n in SMALL], axis=0)
    small_out = _adam_small(pack_small(wts), pack_small(results["grad"]), pack_small(mom), pack_small(var))
    off = 0
    for n in SMALL:
        r = _as_rows(wts[n], 8).shape[0]
        for kind, packed in zip(["delta", "new_m", "new_v"], small_out):
            results[kind][n] = packed[off:off + r].reshape(-1)[:wts[n].size].reshape(wts[n].shape)
        off += r

    outs = [loss, grad_x[None]]
    for kind in ["grad", "delta", "new_m", "new_v"]:
        outs += [results[kind][n] for n in ALL_W]
    return tuple(outs)
```

```python
import functools
import math

import jax
import jax.numpy as jnp
import numpy as np
from jax import lax
from jax.experimental import pallas as pl
from jax.experimental.pallas import tpu as pltpu

F32 = jnp.float32
BF16 = jnp.bfloat16

NORM_EPS = 1e-6
ROPE_THETA = 10000.0
HEAD_DIM = 128
ROPE_DIM = 64
MLA_HEADS = 16
MLA_QK_DIM = 192
MLA_PAD_DIM = 256
ADAM_LR, ADAM_B1, ADAM_B2, ADAM_EPS, ADAM_WD, ADAM_STEP = 0.001, 0.9, 0.999, 1e-08, 0.01, 10

N_DEV = 8
LANES = 128
VMEM_LIMIT = 56 * 1024 * 1024
MATMUL_VMEM_BUDGET = 40 * 1024 * 1024
MASKED = -1e30
ATTN_TQ, ATTN_TK = 256, 256
MESH = pl.DeviceIdType.MESH

NT_DIMS = (((1,), (1,)), ((), ()))
TN_DIMS = (((0,), (0,)), ((), ()))
NN_DIMS = (((1,), (0,)), ((), ()))


def _params(*sem):
    return pltpu.CompilerParams(dimension_semantics=sem, vmem_limit_bytes=VMEM_LIMIT)


def _pick(n, pref):
    best = None
    for t in range(LANES, min(n, pref) + 1, LANES):
        if n % t == 0:
            best = t
    return n if best is None or 2 * best < min(n, pref) else best


def _rows_call(fn, name, row_ins, full_ins, row_outs, acc_outs=(), tile=256):
    row_ins = [r if isinstance(r, tuple) else (r, r.shape[1], 0) for r in row_ins]
    t_rows = row_ins[0][0].shape[0]
    assert t_rows % tile == 0
    n_in = len(row_ins) + len(full_ins)
    n_row_out = len(row_outs)

    def body(*refs):
        res = fn(*[r[...] for r in refs[:n_in]])
        res = res if isinstance(res, tuple) else (res,)
        for ref, val in zip(refs[n_in:n_in + n_row_out], res[:n_row_out]):
            ref[...] = val.astype(ref.dtype)
        acc_refs = refs[n_in + n_row_out:]
        if acc_refs:
            @pl.when(pl.program_id(0) == 0)
            def _():
                for ref in acc_refs:
                    ref[...] = jnp.zeros_like(ref)
            for ref, val in zip(acc_refs, res[n_row_out:]):
                ref[...] += val.astype(ref.dtype)

    in_specs = [pl.BlockSpec((tile, w), functools.partial(lambda i, cb: (i, cb), cb=cb)) for _, w, cb in row_ins]
    in_specs += [pl.BlockSpec(a.shape, lambda i: (0, 0)) for a in full_ins]
    out_specs = [pl.BlockSpec((tile, c), lambda i: (i, 0)) for c, _ in row_outs]
    out_specs += [pl.BlockSpec(s, lambda i: (0, 0)) for s, _ in acc_outs]
    out_shape = [jax.ShapeDtypeStruct((t_rows, c), d) for c, d in row_outs]
    out_shape += [jax.ShapeDtypeStruct(s, d) for s, d in acc_outs]
    outs = pl.pallas_call(
        body, name=name, grid=(t_rows // tile,), in_specs=in_specs, out_specs=out_specs, out_shape=out_shape,
        compiler_params=_params("arbitrary"),
    )(*[r[0] for r in row_ins], *full_ins)
    return outs[0] if len(outs) == 1 else tuple(outs)


def _matmul_tiles(m, n, k, in_bytes, out_bytes):
    tn = n if n <= 1280 else _pick(n, 1024)
    tks = [k] + [k // d for d in (2, 4, 8, 16) if k % (d * LANES) == 0]
    for tk in [t for t in tks if t <= 4096] or [tks[-1]]:
        for tm in (1024, 512, 256):
            if m % tm:
                continue
            acc = 2 * tm * tn * 4 if tk < k else tm * tn * 4
            if 2 * (tm * tk + tk * tn) * in_bytes + 2 * tm * tn * out_bytes + acc <= MATMUL_VMEM_BUDGET:
                return tm, tn, tk
    raise ValueError(f"no matmul tiling for {m}x{n}x{k}")


def _matmul(a, b, form, name, out_dtypes=(F32,), epilogue=None, extras=(), hosted=()):
    if form == "nn":
        (m, k), n = a.shape, b.shape[1]
    elif form == "nt":
        (m, k), n = a.shape, b.shape[0]
    else:
        (k, m), n = a.shape, b.shape[1]
    in_bytes = max(a.dtype.itemsize, b.dtype.itemsize)
    out_bytes = sum(jnp.dtype(d).itemsize for d in out_dtypes) + sum(e.dtype.itemsize for e in extras)
    tm, tn, tk = _matmul_tiles(m, n, k, in_bytes, out_bytes)
    nk = k // tk
    dims = {"nn": NN_DIMS, "nt": NT_DIMS, "tn": TN_DIMS}[form]
    n_extra, n_out, n_host = len(extras), len(out_dtypes), len(hosted)
    grid = (m // tm, n // tn, nk)

    def body(*refs):
        a_ref, b_ref = refs[0], refs[1]
        extra_refs = refs[2:2 + n_extra]
        base = 2 + n_extra + n_host
        out_refs = refs[base:base + n_out]
        sems_at = base + n_out + n_host
        step = (pl.program_id(0) * grid[1] + pl.program_id(1)) * nk + pl.program_id(2)
        host_args = (hosted, refs[2 + n_extra:base], refs[base + n_out:sems_at], refs[sems_at:sems_at + 3 * n_host],
                     step, grid[0] * grid[1] * nk)
        _run_hosted(*host_args, "start")

        def finish(acc):
            vals = (acc,) if epilogue is None else epilogue(acc, *[r[...] for r in extra_refs])
            for ref, val in zip(out_refs, vals):
                ref[...] = val.astype(ref.dtype)

        part = lax.dot_general(a_ref[...].astype(BF16), b_ref[...].astype(BF16), dims, preferred_element_type=F32)
        if nk == 1:
            finish(part)
        else:
            acc_ref = refs[-1]
            kk = pl.program_id(2)

            @pl.when(kk == 0)
            def _():
                acc_ref[...] = part

            @pl.when(kk > 0)
            def _():
                acc_ref[...] += part

            @pl.when(kk == nk - 1)
            def _():
                finish(acc_ref[...])
        _run_hosted(*host_args, "finish")

    a_spec = pl.BlockSpec((tk, tm), lambda i, j, kk: (kk, i)) if form == "tn" else pl.BlockSpec((tm, tk), lambda i, j, kk: (i, kk))
    b_spec = pl.BlockSpec((tn, tk), lambda i, j, kk: (j, kk)) if form == "nt" else pl.BlockSpec((tk, tn), lambda i, j, kk: (kk, j))
    o_spec = pl.BlockSpec((tm, tn), lambda i, j, kk: (i, j))
    outs = pl.pallas_call(
        body, name=name, grid=grid, in_specs=[a_spec, b_spec] + [o_spec] * n_extra + [ANY_SPEC] * n_host,
        out_specs=[o_spec] * n_out + [ANY_SPEC] * n_host,
        out_shape=[jax.ShapeDtypeStruct((m, n), d) for d in out_dtypes]
        + [_exchange_out_shape(kd, arr) for kd, arr in hosted],
        scratch_shapes=EXCHANGE_SCRATCH * n_host + ([pltpu.VMEM((tm, tn), F32)] if nk > 1 else []),
        compiler_params=_params(*(("arbitrary",) * 3 if n_host else ("parallel", "parallel", "arbitrary"))),
    )(a, b, *extras, *[arr for _, arr in hosted])
    return outs[0] if n_out + n_host == 1 else tuple(outs)


def _add_residual(acc, res):
    return (acc + res,)


def _log_sigmoid_parts(z):
    return jnp.log1p(jnp.exp(-jnp.abs(z)))


def _rms_fwd(x, g, n=None):
    n = x.shape[-1] if n is None else n
    r = lax.rsqrt(jnp.sum(x * x, axis=-1, keepdims=True) / n + NORM_EPS)
    return x * r * g


def _rms_bwd(x, g, dout, n=None):
    n = x.shape[-1] if n is None else n
    r = lax.rsqrt(jnp.sum(x * x, axis=-1, keepdims=True) / n + NORM_EPS)
    y = x * r
    dg = jnp.sum(dout * y, axis=0, keepdims=True)
    dy = dout * g
    dx = r * (dy - y * (jnp.sum(dy * y, axis=-1, keepdims=True) / n))
    return dx, dg


def _swap_halves(r):
    lane = lax.broadcasted_iota(jnp.int32, r.shape, 1)
    return jnp.where(lane < ROPE_DIM // 2, pltpu.roll(r, LANES - ROPE_DIM // 2, 1), pltpu.roll(r, ROPE_DIM // 2, 1))


def _rope_fwd(r, cos_t, sin_s):
    return r * cos_t + _swap_halves(r) * sin_s


def _rope_bwd(dr, cos_t, sin_s):
    return dr * cos_t + _swap_halves(dr * sin_s)


def _split3(x):
    hi = x.astype(BF16)
    r1 = x - hi.astype(F32)
    mid = r1.astype(BF16)
    lo = (r1 - mid.astype(F32)).astype(BF16)
    return hi, mid, lo


def _mesh_position():
    x, y, c = lax.axis_index("x"), lax.axis_index("y"), lax.axis_index("c")
    return x, y, c, 4 * x + 2 * y + c


def _peer(x, y, c, k):
    bx, by, bc = (k >> 2) & 1, (k >> 1) & 1, k & 1
    px, py, pc = x ^ bx, y ^ by, c ^ bc
    return (px, py, pc), 4 * px + 2 * py + pc


def _gather_steps(x_ref, out_ref, send_sems, recv_sems, local_sem):
    x, y, c, me = _mesh_position()
    sibling = (x, y, 1 - c)
    chips = [(1 - x, y), (x, 1 - y), (1 - x, 1 - y)]

    def slot(px, py, pc):
        return out_ref.at[4 * px + 2 * py + pc]

    def copy(k, blk, to, src=None):
        return pltpu.make_async_remote_copy(
            src_ref=slot(*blk) if src is None else src, dst_ref=slot(*blk), send_sem=send_sems.at[k],
            recv_sem=recv_sems.at[k], device_id=to, device_id_type=MESH)

    mine = pltpu.make_async_copy(x_ref, out_ref.at[me], local_sem)
    first = [copy(0, (x, y, c), sibling, src=x_ref)]
    first += [copy(1 + j, (x, y, c), (*chip, c), src=x_ref) for j, chip in enumerate(chips)]
    passed = [copy(4 + j, (*chip, c), sibling) for j, chip in enumerate(chips)]

    def start():
        mine.start()
        for cp in first:
            cp.start()

    def forward():
        for j, chip in enumerate(chips):
            copy(1 + j, (*chip, c), (x, y, c)).wait_recv()
            passed[j].start()

    def finish():
        copy(0, (x, y, 1 - c), (x, y, c)).wait_recv()
        for j, chip in enumerate(chips):
            copy(4 + j, (*chip, 1 - c), (x, y, c)).wait_recv()
        for cp in first + passed:
            cp.wait_send()
        mine.wait()

    return start, forward, finish


def _all_to_all_steps(g_ref, out_ref, send_sems, recv_sems, local_sem):
    x, y, c, me = _mesh_position()
    mine = pltpu.make_async_copy(g_ref.at[me], out_ref.at[me], local_sem)
    copies = []
    for k in range(1, N_DEV):
        peer, peer_idx = _peer(x, y, c, k)
        copies.append(pltpu.make_async_remote_copy(
            src_ref=g_ref.at[peer_idx], dst_ref=out_ref.at[me], send_sem=send_sems.at[k - 1],
            recv_sem=recv_sems.at[k - 1], device_id=peer, device_id_type=MESH))

    def start():
        mine.start()
        for cp in copies:
            cp.start()

    def finish():
        for k in range(1, N_DEV):
            peer, peer_idx = _peer(x, y, c, k)
            pltpu.make_async_remote_copy(
                src_ref=g_ref.at[me], dst_ref=out_ref.at[peer_idx], send_sem=send_sems.at[k - 1],
                recv_sem=recv_sems.at[k - 1], device_id=peer, device_id_type=MESH).wait_recv()
        for cp in copies:
            cp.wait_send()
        mine.wait()

    return start, None, finish


EXCHANGE_STEPS = {"gather": _gather_steps, "all_to_all": _all_to_all_steps}
EXCHANGE_SCRATCH = [pltpu.SemaphoreType.DMA((7,)), pltpu.SemaphoreType.DMA((7,)), pltpu.SemaphoreType.DMA]
ANY_SPEC = pl.BlockSpec(memory_space=pl.ANY)


def _exchange_out_shape(kind, arr):
    return jax.ShapeDtypeStruct(((N_DEV,) + arr.shape) if kind == "gather" else arr.shape, arr.dtype)


def _exchange(kind, arrs, name):
    n = len(arrs)

    def body(*refs):
        steps = [EXCHANGE_STEPS[kind](refs[i], refs[n + i], *refs[2 * n + 3 * i:2 * n + 3 * i + 3]) for i in range(n)]
        for start, _, _ in steps:
            start()
        for _, forward, _ in steps:
            if forward is not None:
                forward()
        for _, _, finish in steps:
            finish()

    return pl.pallas_call(body, name=name, out_shape=[_exchange_out_shape(kind, a) for a in arrs],
                          in_specs=[ANY_SPEC] * n, out_specs=[ANY_SPEC] * n, scratch_shapes=EXCHANGE_SCRATCH * n)(*arrs)


def _run_hosted(hosted, src_refs, dst_refs, sem_refs, step, n_steps, when):
    for idx, (kind, _) in enumerate(hosted):
        start, forward, finish = EXCHANGE_STEPS[kind](src_refs[idx], dst_refs[idx], *sem_refs[3 * idx:3 * idx + 3])
        if when == "start":
            pl.when(step == 0)(start)
            if forward is not None:
                pl.when(step == (3 * n_steps) // 4)(forward)
        else:
            pl.when(step == n_steps - 1)(finish)


def _causal_iotas(qi, tq, tk):
    row = qi * tq + lax.broadcasted_iota(jnp.int32, (tq, tk), 0)
    col = lax.broadcasted_iota(jnp.int32, (tq, tk), 1)
    return row, col


def _suffix_matrix(tk, inclusive):
    j = lax.broadcasted_iota(jnp.int32, (2 * tk, tk), 0) % tk
    s = lax.broadcasted_iota(jnp.int32, (2 * tk, tk), 1)
    return jnp.where((j >= s) if inclusive else (j > s), 1.0, 0.0).astype(BF16)


def _suffix_sum(x, mat):
    hi = x.astype(BF16)
    lo = (x - hi.astype(F32)).astype(BF16)
    return lax.dot_general(jnp.concatenate([hi, lo], axis=1), mat, NN_DIMS, preferred_element_type=F32)


def _attn_specs(t_rows, tq, heads, dk, dv, q_off, k_off, v_off):
    q_spec = pl.BlockSpec((tq, dk), lambda h, i: (i, q_off + h))
    kt_spec = pl.BlockSpec((dk, t_rows), lambda h, i: (k_off + h, 0))
    v_spec = pl.BlockSpec((t_rows, dv), lambda h, i: (0, v_off + h))
    return q_spec, kt_spec, v_spec


def _split_weights(weights):
    hi = weights.astype(BF16)
    return hi, (weights - hi.astype(F32)).astype(BF16)


def _weighted_values(split, v):
    return (lax.dot_general(split[0], v, NN_DIMS, preferred_element_type=F32),
            lax.dot_general(split[1], v, NN_DIMS, preferred_element_type=F32))


def _attn_fwd(kind, q_arr, kt_arr, v_arr, heads, dk, dv, scale, name, q_off=0, k_off=0, v_off=0, fcol=None, frow=None,
              tq=ATTN_TQ, tk=ATTN_TK, hosted=()):
    t_rows = q_arr.shape[0]
    nq = t_rows // tq
    stick = kind == "stick"
    decay = fcol is not None
    n_in = 5 if decay else 3
    n_out = 2 if stick else 3
    n_host = len(hosted)

    def body(*refs):
        q_ref, kt_ref, v_ref = refs[:3]
        fcol_ref, frow_ref = (refs[3], refs[4]) if decay else (None, None)
        base = n_in + n_host
        o_ref, fine_ref = refs[base], refs[base + 1]
        lse_ref = None if stick else refs[base + 2]
        host_args = (hosted, refs[n_in:base], refs[base + n_out:base + n_out + n_host], refs[base + n_out + n_host:],
                     pl.program_id(0) * nq + pl.program_id(1), heads * nq)
        _run_hosted(*host_args, "start")
        qi = pl.program_id(1)
        q = q_ref[...]
        row, col = _causal_iotas(qi, tq, tk)
        n_kb = ((qi + 1) * tq + tk - 1) // tk
        zeros_o = jnp.zeros((tq, dv), F32)

        no_weights = (jnp.zeros((tq, tk), BF16), jnp.zeros((tq, tk), BF16))

        def raw_logits(kb):
            return lax.dot_general(q, kt_ref[:, pl.ds(pl.multiple_of(kb * tk, tk), tk)], NN_DIMS,
                                   preferred_element_type=F32)

        def values(kb):
            return v_ref[pl.ds(pl.multiple_of(kb * tk, tk), tk), :]

        if stick:
            mat = _suffix_matrix(tk, inclusive=False)

            def step(i, carry):
                c, acc, rem, raw, prev = carry
                raw_next = raw_logits(jnp.maximum(n_kb - 2 - i, 0))
                d_acc, d_rem = _weighted_values(prev, values(jnp.minimum(n_kb - i, n_kb - 1)))
                ks = pl.multiple_of((n_kb - 1 - i) * tk, tk)
                z = raw * scale
                strict = (col + ks) < row
                lg = _log_sigmoid_parts(z)
                lom = jnp.where(strict, jnp.minimum(-z, 0.0) - lg, 0.0)
                log_w = (jnp.minimum(z, 0.0) - lg) + (_suffix_sum(lom, mat) + c)
                w = jnp.where(strict, jnp.exp(log_w), 0.0)
                return (c + jnp.sum(lom, axis=1, keepdims=True), acc + d_acc, rem + d_rem, raw_next,
                        _split_weights(w))

            _, acc, rem, _, last = lax.fori_loop(0, n_kb, step, (jnp.zeros((tq, 1), F32), zeros_o, zeros_o,
                                                                 raw_logits(n_kb - 1), no_weights))
            d_acc, d_rem = _weighted_values(last, values(0))
            acc, rem = acc + d_acc, rem + d_rem
            o_ref[...] = acc.astype(o_ref.dtype)
            fine_ref[...] = acc + rem
        else:
            fc = fcol_ref[...] if decay else None

            def step(kb, carry):
                m, l, acc, rem, raw, prev = carry
                raw_next = raw_logits(jnp.minimum(kb + 1, n_kb - 1))
                d_acc, d_rem = _weighted_values(prev, values(jnp.maximum(kb - 1, 0)))
                ks = pl.multiple_of(kb * tk, tk)
                s = raw * scale
                if decay:
                    s = (s + fc) - frow_ref[:, pl.ds(ks, tk)]
                s = jnp.where((col + ks) <= row, s, MASKED)
                m_new = jnp.maximum(m, jnp.max(s, axis=1, keepdims=True))
                alpha = jnp.exp(m - m_new)
                p = jnp.exp(s - m_new)
                l = alpha * l + jnp.sum(p, axis=1, keepdims=True)
                return m_new, l, alpha * (acc + d_acc), alpha * (rem + d_rem), raw_next, _split_weights(p)

            m, l, acc, rem, _, last = lax.fori_loop(
                0, n_kb, step, (jnp.full((tq, 1), MASKED, F32), jnp.zeros((tq, 1), F32), zeros_o, zeros_o,
                                raw_logits(0), no_weights))
            d_acc, d_rem = _weighted_values(last, values(n_kb - 1))
            acc, rem = acc + d_acc, rem + d_rem
            inv_l = 1.0 / l
            o_ref[...] = (acc * inv_l).astype(o_ref.dtype)
            fine_ref[...] = (acc + rem) * inv_l
            lse_ref[...] = m + jnp.log(l)
        _run_hosted(*host_args, "finish")

    q_spec, k_spec, v_spec = _attn_specs(t_rows, tq, heads, dk, dv, q_off, k_off, v_off)
    stat_spec = pl.BlockSpec((None, tq, 1), lambda h, i: (h, i, 0))
    ins, in_specs = [q_arr, kt_arr, v_arr], [q_spec, k_spec, v_spec]
    if decay:
        ins += [fcol, frow]
        in_specs += [stat_spec, pl.BlockSpec((None, 1, t_rows), lambda h, i: (h, 0, 0))]
    o_spec = pl.BlockSpec((tq, dv), lambda h, i: (i, h))
    out_specs = [o_spec, o_spec]
    out_shape = [jax.ShapeDtypeStruct((t_rows, heads * dv), BF16), jax.ShapeDtypeStruct((t_rows, heads * dv), F32)]
    if not stick:
        out_specs.append(stat_spec)
        out_shape.append(jax.ShapeDtypeStruct((heads, t_rows, 1), F32))
    return tuple(pl.pallas_call(
        body, name=name, grid=(heads, nq), in_specs=in_specs + [ANY_SPEC] * n_host,
        out_specs=out_specs + [ANY_SPEC] * n_host,
        out_shape=out_shape + [_exchange_out_shape(kd, arr) for kd, arr in hosted],
        scratch_shapes=EXCHANGE_SCRATCH * n_host,
        compiler_params=_params("arbitrary" if n_host else "parallel", "arbitrary"),
    )(*ins, *[arr for _, arr in hosted]))


def _attn_bwd(kind, q_arr, k_arr, kt_arr, vt_arr, o_arr, do_arr, heads, dk, dv, scale, name, q_off=0, k_off=0, kt_off=0,
              vt_off=0, do_off=0, lse=None, fcol=None, frow=None, tq=ATTN_TQ, tk=ATTN_TK, hosted=()):
    t_rows = q_arr.shape[0]
    nq = t_rows // tq
    stick = kind == "stick"
    decay = fcol is not None
    n_in = 6 + (0 if stick else 1) + (2 if decay else 0)
    n_out = 5 if decay else 3
    n_host = len(hosted)

    def body(*refs):
        q_ref, k_ref, kt_ref, vt_ref, o_ref, do_ref = refs[:6]
        lse_ref = None if stick else refs[6]
        fcol_ref, frow_ref = (refs[7], refs[8]) if decay else (None, None)
        base = n_in + n_host
        dq_ref, dk_ref, dv_ref = refs[base:base + 3]
        dfcol_ref, dfrow_ref = (refs[base + 3], refs[base + 4]) if decay else (None, None)
        host_args = (hosted, refs[n_in:base], refs[base + n_out:base + n_out + n_host], refs[base + n_out + n_host:],
                     pl.program_id(0) * nq + pl.program_id(1), heads * nq)
        _run_hosted(*host_args, "start")
        qi = pl.program_id(1)

        @pl.when(qi == 0)
        def _():
            dk_ref[...] = jnp.zeros_like(dk_ref)
            dv_ref[...] = jnp.zeros_like(dv_ref)
            if decay:
                dfrow_ref[...] = jnp.zeros_like(dfrow_ref)

        q = q_ref[...]
        do = do_ref[...]
        delta = jnp.sum(do.astype(F32) * o_ref[...], axis=1, keepdims=True)
        row, col = _causal_iotas(qi, tq, tk)
        n_kb = ((qi + 1) * tq + tk - 1) // tk

        no_pair = (jnp.zeros((tq, tk), BF16), jnp.zeros((tq, tk), BF16))

        def accumulate(kb, pair):
            at = pl.ds(pl.multiple_of(kb * tk, tk), tk)
            dk_ref[at, :] += lax.dot_general(pair[0], q, TN_DIMS, preferred_element_type=F32)
            dv_ref[at, :] += lax.dot_general(pair[1], do, TN_DIMS, preferred_element_type=F32)
            return lax.dot_general(pair[0], k_ref[at, :], NN_DIMS, preferred_element_type=F32)

        def raw_products(kb):
            at = pl.ds(pl.multiple_of(kb * tk, tk), tk)
            return (lax.dot_general(q, kt_ref[:, at], NN_DIMS, preferred_element_type=F32),
                    lax.dot_general(do, vt_ref[:, at], NN_DIMS, preferred_element_type=F32))

        if stick:
            mat_ex = _suffix_matrix(tk, inclusive=False)
            mat_in = _suffix_matrix(tk, inclusive=True)

            def step(i, carry):
                c, gs, dq, (raw, dw), prev = carry
                raw_next = raw_products(jnp.maximum(n_kb - 2 - i, 0))
                dq = dq + accumulate(jnp.minimum(n_kb - i, n_kb - 1), prev)
                ks = pl.multiple_of((n_kb - 1 - i) * tk, tk)
                z = raw * scale
                strict = (col + ks) < row
                lg = _log_sigmoid_parts(z)
                log_beta = jnp.minimum(z, 0.0) - lg
                log_omb = jnp.minimum(-z, 0.0) - lg
                lom = jnp.where(strict, log_omb, 0.0)
                w = jnp.where(strict, jnp.exp(log_beta + (_suffix_sum(lom, mat_ex) + c)), 0.0)
                g = w * dw
                g_before = delta - (gs + _suffix_sum(g, mat_in))
                dz = jnp.where(strict, g * jnp.exp(log_omb) - g_before * jnp.exp(log_beta), 0.0)
                return (c + jnp.sum(lom, axis=1, keepdims=True), gs + jnp.sum(g, axis=1, keepdims=True), dq,
                        raw_next, ((dz * scale).astype(BF16), w.astype(BF16)))

            zero = jnp.zeros((tq, 1), F32)
            _, _, dq, _, last = lax.fori_loop(0, n_kb, step, (zero, zero, jnp.zeros((tq, dk), F32),
                                                              raw_products(n_kb - 1), no_pair))
            dq = dq + accumulate(0, last)
        else:
            lse_v = lse_ref[...]
            fc = fcol_ref[...] if decay else None

            def step(kb, carry):
                dq, row_sum, (raw, dp), prev = carry
                raw_next = raw_products(jnp.minimum(kb + 1, n_kb - 1))
                dq = dq + accumulate(jnp.maximum(kb - 1, 0), prev)
                ks = pl.multiple_of(kb * tk, tk)
                s = raw * scale
                if decay:
                    s = (s + fc) - frow_ref[:, pl.ds(ks, tk)]
                p = jnp.where((col + ks) <= row, jnp.exp(s - lse_v), 0.0)
                ds = p * (dp - delta)
                if decay:
                    dfrow_ref[:, pl.ds(ks, tk)] += jnp.sum(ds, axis=0, keepdims=True)
                    row_sum = row_sum + jnp.sum(ds, axis=1, keepdims=True)
                return dq, row_sum, raw_next, ((ds * scale).astype(BF16), p.astype(BF16))

            dq, row_sum, _, last = lax.fori_loop(0, n_kb, step, (jnp.zeros((tq, dk), F32), jnp.zeros((tq, 1), F32),
                                                                raw_products(0), no_pair))
            dq = dq + accumulate(n_kb - 1, last)
            if decay:
                dfcol_ref[...] = row_sum
        dq_ref[...] = dq
        _run_hosted(*host_args, "finish")

    q_spec, kt_spec, _ = _attn_specs(t_rows, tq, heads, dk, dv, q_off, kt_off, 0)
    stat_spec = pl.BlockSpec((None, tq, 1), lambda h, i: (h, i, 0))
    frow_spec = pl.BlockSpec((None, 1, t_rows), lambda h, i: (h, 0, 0))
    ins = [q_arr, k_arr, kt_arr, vt_arr, o_arr, do_arr]
    in_specs = [q_spec, pl.BlockSpec((t_rows, dk), lambda h, i: (0, k_off + h)), kt_spec,
                pl.BlockSpec((dv, t_rows), lambda h, i: (vt_off + h, 0)), pl.BlockSpec((tq, dv), lambda h, i: (i, h)),
                pl.BlockSpec((tq, dv), lambda h, i: (i, do_off + h))]
    if not stick:
        ins.append(lse)
        in_specs.append(stat_spec)
    if decay:
        ins += [fcol, frow]
        in_specs += [stat_spec, frow_spec]
    out_specs = [pl.BlockSpec((tq, dk), lambda h, i: (i, h)), pl.BlockSpec((t_rows, dk), lambda h, i: (0, h)),
                 pl.BlockSpec((t_rows, dv), lambda h, i: (0, h))]
    out_shape = [jax.ShapeDtypeStruct((t_rows, heads * dk), F32), jax.ShapeDtypeStruct((t_rows, heads * dk), F32),
                 jax.ShapeDtypeStruct((t_rows, heads * dv), F32)]
    if decay:
        out_specs += [stat_spec, frow_spec]
        out_shape += [jax.ShapeDtypeStruct((heads, t_rows, 1), F32), jax.ShapeDtypeStruct((heads, 1, t_rows), F32)]
    return pl.pallas_call(
        body, name=name, grid=(heads, nq), in_specs=in_specs + [ANY_SPEC] * n_host,
        out_specs=out_specs + [ANY_SPEC] * n_host,
        out_shape=out_shape + [_exchange_out_shape(kd, arr) for kd, arr in hosted],
        scratch_shapes=EXCHANGE_SCRATCH * n_host,
        compiler_params=_params("arbitrary" if n_host else "parallel", "arbitrary"),
    )(*ins, *[arr for _, arr in hosted])


def _prefix_matrix(reverse):
    j = lax.broadcasted_iota(jnp.int32, (LANES, LANES), 0)
    s = lax.broadcasted_iota(jnp.int32, (LANES, LANES), 1)
    return jnp.where((j >= s) if reverse else (j <= s), 1.0, 0.0).astype(BF16)


def _chunk_cumsum(x, mat):
    return sum(lax.dot_general(part, mat, NN_DIMS, preferred_element_type=F32) for part in _split3(x))


def _gate_fwd(logit_t, bias_col):
    heads, t_rows = logit_t.shape

    def body(x_ref, b_ref, out_ref):
        mat = _prefix_matrix(reverse=False)

        def step(ci, carry):
            cs = pl.multiple_of(ci * LANES, LANES)
            pre = x_ref[:, pl.ds(cs, LANES)] + b_ref[...]
            log_f = jnp.minimum(pre, 0.0) - _log_sigmoid_parts(pre)
            out_ref[:, pl.ds(cs, LANES)] = _chunk_cumsum(log_f, mat) + carry
            return carry + jnp.sum(log_f, axis=1, keepdims=True)

        lax.fori_loop(0, t_rows // LANES, step, jnp.zeros((heads, 1), F32))

    return pl.pallas_call(body, name="gate_fwd", out_shape=jax.ShapeDtypeStruct((heads, t_rows), F32),
                          compiler_params=pltpu.CompilerParams(vmem_limit_bytes=VMEM_LIMIT))(logit_t, bias_col)


def _gate_bwd(dcum_t, logit_t, bias_col):
    heads, t_rows = logit_t.shape
    n_chunks = t_rows // LANES

    def body(d_ref, x_ref, b_ref, dx_ref, db_ref):
        mat = _prefix_matrix(reverse=True)

        def step(i, carry):
            tail, db = carry
            cs = pl.multiple_of((n_chunks - 1 - i) * LANES, LANES)
            d = d_ref[:, pl.ds(cs, LANES)]
            d_log_f = _chunk_cumsum(d, mat) + tail
            pre = x_ref[:, pl.ds(cs, LANES)] + b_ref[...]
            e = jnp.exp(-jnp.abs(pre))
            d_pre = d_log_f * (jnp.where(pre >= 0.0, e, 1.0) / (1.0 + e))
            dx_ref[:, pl.ds(cs, LANES)] = d_pre
            return tail + jnp.sum(d, axis=1, keepdims=True), db + jnp.sum(d_pre, axis=1, keepdims=True)

        zero = jnp.zeros((heads, 1), F32)
        _, db = lax.fori_loop(0, n_chunks, step, (zero, zero))
        db_ref[...] = db

    return pl.pallas_call(body, name="gate_bwd",
                          out_shape=(jax.ShapeDtypeStruct((heads, t_rows), F32), jax.ShapeDtypeStruct((heads, 1), F32)),
                          compiler_params=pltpu.CompilerParams(vmem_limit_bytes=VMEM_LIMIT))(dcum_t, logit_t, bias_col)


def _norm_fwd(x, g, name):
    return _rows_call(lambda xv, gv: _rms_fwd(xv, gv), name, [x], [g], [(x.shape[1], BF16)])


def _norm_bwd(x, g, dh, dres, name):
    def fn(xv, dhv, dresv, gv):
        dx, dg = _rms_bwd(xv, gv, dhv)
        dx = dresv + dx
        return dx, dx, dg
    return _rows_call(fn, name, [x, dh, dres], [g], [(x.shape[1], F32), (x.shape[1], BF16)], [((1, x.shape[1]), F32)])


def _loss_fwd_bwd(y, target):
    d_model = y.shape[1]

    def fn(yv, tv):
        err = yv - tv
        dy = err * (1.0 / d_model)
        return dy, dy, jnp.sum(jnp.sum(err * err, axis=1, keepdims=True), axis=0, keepdims=True)
    return _rows_call(fn, "loss", [y, target], [], [(d_model, F32), (d_model, BF16)], [((1, 1), F32)])


def _heads_apply(fn, n_heads, width, *tiles):
    return [fn(*[t[:, h * width:(h + 1) * width] for t in tiles]) for h in range(n_heads)]


def _fox_norm_fwd(pb, gq, gk, heads):
    width = heads * HEAD_DIM

    def fn(qk, gqv, gkv):
        q = jnp.concatenate(_heads_apply(lambda t: _rms_fwd(t, gqv), heads, HEAD_DIM, qk[:, :width]), axis=1)
        k = jnp.concatenate(_heads_apply(lambda t: _rms_fwd(t, gkv), heads, HEAD_DIM, qk[:, width:]), axis=1)
        return q, k
    return _rows_call(fn, "fox_norm_fwd", [(pb, 2 * width, 0)], [gq, gk], [(width, BF16), (width, BF16)])


def _fox_norm_bwd(pb, gq, gk, dq, dk, heads):
    width = heads * HEAD_DIM

    def fn(qk, dqv, dkv, gqv, gkv):
        res_q = _heads_apply(lambda t, d: _rms_bwd(t, gqv, d), heads, HEAD_DIM, qk[:, :width], dqv)
        res_k = _heads_apply(lambda t, d: _rms_bwd(t, gkv, d), heads, HEAD_DIM, qk[:, width:], dkv)
        dqk = jnp.concatenate([r[0] for r in res_q] + [r[0] for r in res_k], axis=1)
        return dqk, sum(r[1] for r in res_q), sum(r[1] for r in res_k)
    return _rows_call(fn, "fox_norm_bwd", [(pb, 2 * width, 0), dq, dk], [gq, gk], [(2 * width, BF16)],
                      [((1, HEAD_DIM), F32), ((1, HEAD_DIM), F32)])


def _lora_norm_fwd(down, gq, gkv, rank):
    def fn(dv, gqv, gkvv):
        return _rms_fwd(dv[:, :rank], gqv), _rms_fwd(dv[:, rank:], gkvv)
    return _rows_call(fn, "lora_norm_fwd", [(down, 2 * rank, 0)], [gq, gkv], [(rank, BF16), (rank, BF16)])


def _lora_norm_bwd(down, gq, gkv, dcq, dckv, dkpe, rank):
    def fn(dv, dcqv, dckvv, dkpev, gqv, gkvv):
        dxq, dgq = _rms_bwd(dv[:, :rank], gqv, dcqv)
        dxkv, dgkv = _rms_bwd(dv[:, rank:], gkvv, dckvv)
        return jnp.concatenate([dxq, dxkv, dkpev], axis=1), dgq, dgkv
    return _rows_call(fn, "lora_norm_bwd", [(down, 2 * rank, 0), dcq, dckv, dkpe], [gq, gkv],
                      [(2 * rank + LANES, BF16)], [((1, rank), F32), ((1, rank), F32)])


def _rope_tables(pos_col, inv_freq, sin_sign):
    def fn(pos, invf, sign):
        ang = pos.astype(F32) * invf
        return jnp.cos(ang) * jnp.abs(sign), jnp.sin(ang) * sign
    return _rows_call(fn, "rope_tables", [pos_col], [inv_freq, sin_sign], [(LANES, F32), (LANES, F32)])


def _mla_prep_fwd(q_raw, kv, down, kpe_block, qg, kg, cos_t, sin_s):
    def fn(qv, kvv, kpe, cosv, sinv, qgv, kgv):
        qs, ks, vs = [], [], []
        for h in range(MLA_HEADS):
            qn = _rms_fwd(qv[:, h * MLA_PAD_DIM:(h + 1) * MLA_PAD_DIM], qgv, MLA_QK_DIM)
            qs += [qn[:, :HEAD_DIM], _rope_fwd(qn[:, HEAD_DIM:], cosv, sinv)]
            k_full = jnp.concatenate([kvv[:, h * MLA_PAD_DIM:h * MLA_PAD_DIM + HEAD_DIM], kpe], axis=1)
            kn = _rms_fwd(k_full, kgv, MLA_QK_DIM)
            ks += [kn[:, :HEAD_DIM], _rope_fwd(kn[:, HEAD_DIM:], cosv, sinv)]
            vs.append(kvv[:, h * MLA_PAD_DIM + HEAD_DIM:(h + 1) * MLA_PAD_DIM])
        return jnp.concatenate(qs, axis=1), jnp.concatenate(ks, axis=1), jnp.concatenate(vs, axis=1)
    wide = MLA_HEADS * MLA_PAD_DIM
    return _rows_call(fn, "mla_prep_fwd", [q_raw, kv, (down, LANES, kpe_block), cos_t, sin_s], [qg, kg],
                      [(wide, BF16), (wide, BF16), (MLA_HEADS * HEAD_DIM, BF16)], tile=128)


def _mla_prep_bwd(q_raw, kv, down, kpe_block, qg, kg, cos_t, sin_s, dq, dk, dv):
    def fn(qv, kvv, kpe, cosv, sinv, dqv, dkv, dvv, qgv, kgv):
        dqs, dkvs = [], []
        dkpe = jnp.zeros_like(kpe)
        dqg = jnp.zeros_like(qgv)
        dkg = jnp.zeros_like(kgv)
        for h in range(MLA_HEADS):
            lo, hi = h * MLA_PAD_DIM, (h + 1) * MLA_PAD_DIM
            dqn = jnp.concatenate([dqv[:, lo:lo + HEAD_DIM], _rope_bwd(dqv[:, lo + HEAD_DIM:hi], cosv, sinv)], axis=1)
            dqh, dg = _rms_bwd(qv[:, lo:hi], qgv, dqn, MLA_QK_DIM)
            dqs.append(dqh)
            dqg = dqg + dg
            k_full = jnp.concatenate([kvv[:, lo:lo + HEAD_DIM], kpe], axis=1)
            dkn = jnp.concatenate([dkv[:, lo:lo + HEAD_DIM], _rope_bwd(dkv[:, lo + HEAD_DIM:hi], cosv, sinv)], axis=1)
            dkh, dg = _rms_bwd(k_full, kgv, dkn, MLA_QK_DIM)
            dkg = dkg + dg
            dkpe = dkpe + dkh[:, HEAD_DIM:]
            dkvs += [dkh[:, :HEAD_DIM], dvv[:, h * HEAD_DIM:(h + 1) * HEAD_DIM]]
        return jnp.concatenate(dqs, axis=1), jnp.concatenate(dkvs, axis=1), dkpe, dqg, dkg
    wide = MLA_HEADS * MLA_PAD_DIM
    return _rows_call(fn, "mla_prep_bwd", [q_raw, kv, (down, LANES, kpe_block), cos_t, sin_s, dq, dk, dv], [qg, kg],
                      [(wide, BF16), (wide, BF16), (LANES, F32)], [((1, MLA_PAD_DIM), F32), ((1, MLA_PAD_DIM), F32)],
                      tile=128)


def _sqrelu_up(acc):
    return acc, jnp.square(jnp.maximum(acc, 0.0))


def _sqrelu_grad(acc, u):
    return (acc * (2.0 * jnp.maximum(u, 0.0)),)


def _mlp_fwd(x, g, w_up, w_down, tag):
    h = _norm_fwd(x, g, f"mlp_norm_fwd{tag}")
    u, a = _matmul(h, w_up, "nn", f"mlp_up{tag}", (F32, BF16), _sqrelu_up)
    return _matmul(a, w_down, "nn", f"mlp_down{tag}", (F32,), _add_residual, (x,)), (h, u, a)


def _mlp_bwd(x, g, w_up, w_down, saved, dy, dy16, tag):
    h, u, a = saved
    dw_down = _matmul(a, dy16, "tn", f"mlp_dwdown{tag}", (BF16,))
    du = _matmul(dy16, w_down, "nt", f"mlp_du{tag}", (BF16,), _sqrelu_grad, (u,))
    dw_up = _matmul(h, du, "tn", f"mlp_dwup{tag}", (BF16,))
    dh = _matmul(du, w_up, "nt", f"mlp_dh{tag}")
    dx, dx16, dg = _norm_bwd(x, g, dh, dy, f"mlp_norm_bwd{tag}")
    return dx, dx16, dg, dw_up, dw_down


def _local_step(x, pos_col, target, w, dist=None):
    w = dict(w)
    hs = w["w_a"].shape[1] // (4 * HEAD_DIM)
    sb_w = hs * HEAD_DIM
    grads, received = {}, {}

    def gather_in(group):
        return [("gather", blk) for blk in dist["blocks"][group]] if dist else []

    def exchange_in(group):
        return [("all_to_all", slots) for slots in dist["slots_of"](group, grads)] if dist else []

    h0 = _norm_fwd(x, w["ln_mix0"], "mix0_norm_fwd")
    pa = _matmul(h0, w["w_a"], "nn", "in_proj_a", (BF16,))
    pb = _matmul(h0, w["w_b"], "nn", "in_proj_b")
    pat = pa[:, sb_w:].T
    o_sb, o_sb_fine, *got = _attn_fwd("stick", pa, pat, pa, hs, HEAD_DIM, HEAD_DIM, HEAD_DIM ** -0.5, "stick_fwd",
                                      q_off=0, k_off=0, v_off=2 * hs, hosted=gather_in("mlp0"))
    if dist:
        w.update(dist["weights_of"]("mlp0", got))
    logit_t = pb[:, 2 * sb_w:2 * sb_w + hs].T
    bias_col = w["b_f"][0, :hs].reshape(hs, 1)
    f_cum = _gate_fwd(logit_t, bias_col)
    f_col, f_row = f_cum[:, :, None], f_cum[:, None, :]
    qf, kf = _fox_norm_fwd(pb, w["fox_q_g"], w["fox_k_g"], hs)
    kft = kf.T
    o_fx, o_fx_fine, lse_fx, *got = _attn_fwd("softmax", qf, kft, pa, hs, HEAD_DIM, HEAD_DIM, HEAD_DIM ** -0.5,
                                              "fox_fwd", v_off=3 * hs, fcol=f_col, frow=f_row,
                                              hosted=gather_in("layer1"))
    if dist:
        w.update(dist["weights_of"]("layer1", got))
    o0 = jnp.concatenate([o_sb, o_fx], axis=1)
    x1 = _matmul(o0, w["w_o0"], "nn", "out_proj0", (F32,), _add_residual, (x,))
    x2, mlp0 = _mlp_fwd(x1, w["ln_mlp0"], w["w_up0"], w["w_dn0"], "0")

    rank = w["w_uq"].shape[0]
    h2 = _norm_fwd(x2, w["ln_mix1"], "mix1_norm_fwd")
    down = _matmul(h2, w["w_down"], "nn", "mla_down")
    cqn, ckvn = _lora_norm_fwd(down, w["q_a_g"], w["kv_a_g"], rank)
    q_raw = _matmul(cqn, w["w_uq"], "nn", "mla_uq")
    kv = _matmul(ckvn, w["w_ukv"], "nn", "mla_ukv")
    cos_t, sin_s = _rope_tables(pos_col, w["inv_freq"], w["sin_sign"])
    kpe_block = 2 * rank // LANES
    qm, km, vm = _mla_prep_fwd(q_raw, kv, down, kpe_block, w["mla_q_g"], w["mla_k_g"], cos_t, sin_s)
    kmt, vmt = km.T, vm.T
    o_m, o_m_fine, lse_m = _attn_fwd("softmax", qm, kmt, vm, MLA_HEADS, MLA_PAD_DIM, HEAD_DIM, MLA_QK_DIM ** -0.5,
                                     "mla_fwd")
    x3 = _matmul(o_m, w["w_o1"], "nn", "out_proj1", (F32,), _add_residual, (x2,))
    x4, mlp1 = _mlp_fwd(x3, w["ln_mlp1"], w["w_up1"], w["w_dn1"], "1")

    dy, dy16, sq_err = _loss_fwd_bwd(x4, target)

    dx3, dx3_16, grads["ln_mlp1"], grads["w_up1"], grads["w_dn1"] = _mlp_bwd(
        x3, w["ln_mlp1"], w["w_up1"], w["w_dn1"], mlp1, dy, dy16, "1")
    grads["w_o1"] = _matmul(o_m, dx3_16, "tn", "dw_o1", (BF16,))
    do_m = _matmul(dx3_16, w["w_o1"], "nt", "do_mla", (BF16,))
    dqm, dkm, dvm, *got = _attn_bwd("softmax", qm, km, kmt, vmt, o_m_fine, do_m, MLA_HEADS, MLA_PAD_DIM, HEAD_DIM,
                                    MLA_QK_DIM ** -0.5, "mla_bwd", lse=lse_m, hosted=exchange_in("mlp1"))
    received["mlp1"] = got
    dq_raw, dkv, dkpe, grads["mla_q_g"], grads["mla_k_g"] = _mla_prep_bwd(
        q_raw, kv, down, kpe_block, w["mla_q_g"], w["mla_k_g"], cos_t, sin_s, dqm, dkm, dvm)
    grads["w_uq"] = _matmul(cqn, dq_raw, "tn", "dw_uq", (BF16,))
    grads["w_ukv"] = _matmul(ckvn, dkv, "tn", "dw_ukv", (BF16,))
    dcqn = _matmul(dq_raw, w["w_uq"], "nt", "d_cq")
    dckvn = _matmul(dkv, w["w_ukv"], "nt", "d_ckv")
    ddown, grads["q_a_g"], grads["kv_a_g"] = _lora_norm_bwd(down, w["q_a_g"], w["kv_a_g"], dcqn, dckvn, dkpe, rank)
    grads["w_down"] = _matmul(h2, ddown, "tn", "dw_down", (BF16,))
    dh2 = _matmul(ddown, w["w_down"], "nt", "d_h2")
    dx2, dx2_16, grads["ln_mix1"] = _norm_bwd(x2, w["ln_mix1"], dh2, dx3, "mix1_norm_bwd")

    dx1, dx1_16, grads["ln_mlp0"], grads["w_up0"], grads["w_dn0"] = _mlp_bwd(
        x1, w["ln_mlp0"], w["w_up0"], w["w_dn0"], mlp0, dx2, dx2_16, "0")
    grads["w_o0"] = _matmul(o0, dx1_16, "tn", "dw_o0", (BF16,))
    do0 = _matmul(dx1_16, w["w_o0"], "nt", "do_mix0", (BF16,))
    dq_sb, dk_sb, dv_sb, *got = _attn_bwd("stick", pa, pa, pat, pat, o_sb_fine, do0, hs, HEAD_DIM, HEAD_DIM,
                                          HEAD_DIM ** -0.5, "stick_bwd", q_off=0, k_off=hs, kt_off=0, vt_off=hs, do_off=0,
                                          hosted=exchange_in("mla"))
    received["mla"] = got
    dqf, dkf, dv_fx, ds_rows, ds_cols, *got = _attn_bwd(
        "softmax", qf, kf, kft, pat, o_fx_fine, do0, hs, HEAD_DIM, HEAD_DIM, HEAD_DIM ** -0.5, "fox_bwd", vt_off=2 * hs,
        do_off=hs, lse=lse_fx, fcol=f_col, frow=f_row, hosted=exchange_in("mlp0"))
    received["mlp0"] = got
    dqk_fx, grads["fox_q_g"], grads["fox_k_g"] = _fox_norm_bwd(pb, w["fox_q_g"], w["fox_k_g"], dqf, dkf, hs)
    dlogit_t, db_f = _gate_bwd(ds_rows[:, :, 0] - ds_cols[:, 0, :], logit_t, bias_col)
    grads["b_f"] = db_f.reshape(1, hs)
    dpa = jnp.concatenate([dq_sb.astype(BF16), dk_sb.astype(BF16), dv_sb.astype(BF16), dv_fx.astype(BF16)], axis=1)
    dlogit_pad = jnp.pad(dlogit_t.T.astype(BF16), ((0, 0), (0, pb.shape[1] - 2 * sb_w - hs)))
    dpb = jnp.concatenate([dqk_fx, dlogit_pad], axis=1)
    grads["w_a"] = _matmul(h0, dpa, "tn", "dw_a", (BF16,))
    grads["w_b"] = _matmul(h0, dpb, "tn", "dw_b", (BF16,))
    dh0 = _matmul(dpb, w["w_b"], "nt", "d_h0_b")
    res = _matmul(dpa, w["w_a"], "nt", "d_h0_a", (F32,), _add_residual, (dh0,), hosted=exchange_in("mix0"))
    dh0, received["mix0"] = (res[0], list(res[1:])) if dist else (res, [])
    grad_x, _, grads["ln_mix0"] = _norm_bwd(x, w["ln_mix0"], dh0, dx1, "mix0_norm_bwd")
    return sq_err, grad_x, grads, received


PIECES = {
    "sf_w_in": ("sf_w_in", 0, 1), "sf_w_o": ("sf_w_o", 0, 0), "mla_w_down": ("mla_w_down", 0, 0),
    "mla_w_uq": ("mla_w_uq", 0, 1), "mla_w_ukv": ("mla_w_ukv", 0, 1), "mla_w_o": ("mla_w_o", 0, 0),
    "mlp_w_up0": ("mlp_w_up", 0, 1), "mlp_w_up1": ("mlp_w_up", 1, 1),
    "mlp_w_down0": ("mlp_w_down", 0, 0), "mlp_w_down1": ("mlp_w_down", 1, 0),
}
GROUPS = {
    "mix0": ["sf_w_in"], "mlp0": ["sf_w_o", "mlp_w_up0", "mlp_w_down0"],
    "mla": ["mla_w_down", "mla_w_uq", "mla_w_ukv", "mla_w_o"], "mlp1": ["mlp_w_up1", "mlp_w_down1"],
}
GROUPS["layer1"] = GROUPS["mla"] + GROUPS["mlp1"]
SMALL = ["ln_mix_g", "ln_mlp_g", "sf_b_f", "fox_q_g", "fox_k_g", "mla_q_a_g", "mla_kv_a_g", "mla_q_g", "mla_k_g"]
ALL_W = ["ln_mix_g", "ln_mlp_g", "sf_w_in", "sf_b_f", "fox_q_g", "fox_k_g", "sf_w_o", "mla_w_down", "mla_q_a_g",
         "mla_kv_a_g", "mla_w_uq", "mla_w_ukv", "mla_q_g", "mla_k_g", "mla_w_o", "mlp_w_up", "mlp_w_down"]


def _weights_mix0(full, small):
    w_in = full["sf_w_in"]
    d_model = w_in.shape[0]
    n_fx = small["sf_b_f"].shape[1]
    sb_w = (w_in.shape[1] - n_fx) // 6
    cols = lambda i: w_in[:, i * sb_w:(i + 1) * sb_w]
    w_a = jnp.concatenate([cols(0), cols(1), cols(2), cols(5)], axis=1)
    w_b = jnp.concatenate([cols(3), cols(4), w_in[:, 6 * sb_w:], jnp.zeros((d_model, LANES - n_fx), w_in.dtype)], axis=1)
    half = ROPE_DIM // 2
    inv_freq = ROPE_THETA ** (-jnp.arange(half, dtype=F32) / half)
    zeros64 = jnp.zeros((ROPE_DIM,), F32)
    pad256 = lambda g: jnp.pad(g, ((0, 0), (0, MLA_PAD_DIM - MLA_QK_DIM)))
    pad_lanes = lambda g: jnp.pad(g, ((0, 0), (0, LANES - g.shape[1])))
    return dict(
        ln_mix0=small["ln_mix_g"][0:1], ln_mix1=small["ln_mix_g"][1:2],
        ln_mlp0=small["ln_mlp_g"][0:1], ln_mlp1=small["ln_mlp_g"][1:2],
        w_a=w_a, w_b=w_b, b_f=pad_lanes(small["sf_b_f"]), fox_q_g=small["fox_q_g"], fox_k_g=small["fox_k_g"],
        q_a_g=small["mla_q_a_g"], kv_a_g=small["mla_kv_a_g"],
        mla_q_g=pad256(small["mla_q_g"]), mla_k_g=pad256(small["mla_k_g"]),
        inv_freq=jnp.concatenate([inv_freq, inv_freq, zeros64]).reshape(1, LANES),
        sin_sign=jnp.concatenate([-jnp.ones((half,), F32), jnp.ones((half,), F32), zeros64]).reshape(1, LANES),
    )


def _weights_mlp0(full):
    return dict(w_o0=full["sf_w_o"], w_up0=full["mlp_w_up0"], w_dn0=full["mlp_w_down0"])


def _weights_layer1(full):
    rank = full["mla_w_uq"].shape[0]
    w_uq = full["mla_w_uq"].reshape(rank, MLA_HEADS, MLA_QK_DIM)
    w_uq = jnp.pad(w_uq, ((0, 0), (0, 0), (0, MLA_PAD_DIM - MLA_QK_DIM))).reshape(rank, MLA_HEADS * MLA_PAD_DIM)
    return dict(w_down=jnp.pad(full["mla_w_down"], ((0, 0), (0, LANES - ROPE_DIM))), w_uq=w_uq,
                w_ukv=full["mla_w_ukv"], w_o1=full["mla_w_o"], w_up1=full["mlp_w_up1"], w_dn1=full["mlp_w_down1"])


WEIGHTS_OF = {"mlp0": _weights_mlp0, "layer1": _weights_layer1}


def _piece_grad(g, piece):
    if piece == "sf_w_in":
        n_fx = g["b_f"].shape[1]
        ga, gb = g["w_a"], g["w_b"]
        sb_w = ga.shape[1] // 4
        ca = lambda i: ga[:, i * sb_w:(i + 1) * sb_w]
        return jnp.concatenate([ca(0), ca(1), ca(2), gb[:, :sb_w], gb[:, sb_w:2 * sb_w], ca(3),
                                gb[:, 2 * sb_w:2 * sb_w + n_fx]], axis=1)
    if piece == "mla_w_uq":
        rank = g["w_uq"].shape[0]
        return g["w_uq"].reshape(rank, MLA_HEADS, MLA_PAD_DIM)[:, :, :MLA_QK_DIM].reshape(rank, MLA_HEADS * MLA_QK_DIM)
    if piece == "mla_w_down":
        return g["w_down"][:, :g["w_down"].shape[1] - (LANES - ROPE_DIM)]
    return g[{"sf_w_o": "w_o0", "mla_w_ukv": "w_ukv", "mla_w_o": "w_o1", "mlp_w_up0": "w_up0", "mlp_w_up1": "w_up1",
              "mlp_w_down0": "w_dn0", "mlp_w_down1": "w_dn1"}[piece]]


def _small_grads(g):
    return {
        "ln_mix_g": jnp.concatenate([g["ln_mix0"], g["ln_mix1"]], axis=0),
        "ln_mlp_g": jnp.concatenate([g["ln_mlp0"], g["ln_mlp1"]], axis=0),
        "sf_b_f": g["b_f"], "fox_q_g": g["fox_q_g"], "fox_k_g": g["fox_k_g"],
        "mla_q_a_g": g["q_a_g"], "mla_kv_a_g": g["kv_a_g"],
        "mla_q_g": g["mla_q_g"][:, :MLA_QK_DIM], "mla_k_g": g["mla_k_g"][:, :MLA_QK_DIM],
    }


PACK_TILE = 1024


def _as_rows(a, row_multiple=16):
    flat = a.reshape(-1)
    rows = -(-flat.shape[0] // LANES)
    rows = -(-rows // row_multiple) * row_multiple
    return jnp.pad(flat, (0, rows * LANES - flat.shape[0])).reshape(rows, LANES)


def _pack_rows(parts, axis, dtype, row_multiple=PACK_TILE, spare_rows=0):
    used = sum(p.shape[axis] for p in parts)
    shape = list(parts[0].shape)
    shape[axis] = -(-used // row_multiple) * row_multiple + spare_rows - used
    return jnp.concatenate([p.astype(dtype) for p in parts] + [jnp.ones(shape, dtype)], axis=axis)


def _unshard(stack, axis):
    moved = jnp.moveaxis(stack, 0, axis)
    shape = list(stack.shape[1:])
    shape[axis] *= N_DEV
    return moved.reshape(shape)


def _shard_stack(full, axis):
    shape = list(full.shape)
    shape[axis:axis + 1] = [N_DEV, shape[axis] // N_DEV]
    return jnp.moveaxis(full.reshape(shape), axis, 0)


OPT_TILE_ELEMS = 128 * 1024


def _row_tile(rows, cols):
    best = 16
    for t in range(16, rows + 1, 16):
        if rows % t == 0 and t * cols <= OPT_TILE_ELEMS:
            best = t
    assert rows % best == 0
    return best


def _cast_bf16(a, name):
    return _rows_call(lambda v: v, name, [a], [], [(a.shape[1], BF16)], tile=_row_tile(*a.shape))


def _adam_math(w, g, m, v):
    m = ADAM_B1 * m + (1.0 - ADAM_B1) * g
    v = ADAM_B2 * v + (1.0 - ADAM_B2) * jnp.square(g)
    m_hat = m / (1.0 - ADAM_B1 ** ADAM_STEP)
    v_hat = v / (1.0 - ADAM_B2 ** ADAM_STEP)
    delta = -ADAM_LR * (m_hat / (jnp.sqrt(v_hat) + ADAM_EPS) + ADAM_WD * w)
    return delta, m, v


def _adam_big(recvs, w, m, v, name):
    layers, rows, cols = w.shape
    tile = _row_tile(rows, cols)
    n_tiles = rows // tile

    def body(*refs):
        recv_refs = refs[:layers]
        w_ref, m_ref, v_ref, g_ref, d_ref, nm_ref, nv_ref = refs[layers:]
        layer = pl.program_id(0)

        def total(r_ref):
            acc = r_ref[0].astype(F32)
            for s in range(1, N_DEV):
                acc = acc + r_ref[s].astype(F32)
            return acc

        g = total(recv_refs[0])
        for j in range(1, layers):
            g = jnp.where(layer == j, total(recv_refs[j]), g)
        delta, nm, nv = _adam_math(w_ref[...], g, m_ref[...], v_ref[...])
        g_ref[...] = g
        d_ref[...] = delta
        nm_ref[...] = nm
        nv_ref[...] = nv

    def recv_spec(j):
        return pl.BlockSpec((N_DEV, tile, cols),
                            lambda l, i: (0, jnp.where(l == j, i, jnp.where(l < j, 0, n_tiles - 1)), 0))

    spec = pl.BlockSpec((None, tile, cols), lambda l, i: (l, i, 0))
    out = jax.ShapeDtypeStruct(w.shape, F32)
    return pl.pallas_call(
        body, name=name, grid=(layers, n_tiles), in_specs=[recv_spec(j) for j in range(layers)] + [spec] * 3,
        out_specs=[spec] * 4, out_shape=[out] * 4, compiler_params=_params("arbitrary", "arbitrary"),
    )(*recvs, w, m, v)


def _sum_slots(gathered):
    rows = gathered.shape[1]

    def body(r_ref, o_ref):
        acc = r_ref[0]
        for s in range(1, N_DEV):
            acc = acc + r_ref[s]
        o_ref[...] = acc

    return pl.pallas_call(body, name="sum_small", out_shape=jax.ShapeDtypeStruct((rows, LANES), F32))(gathered)


def _adam_small(w, g, m, v):
    def fn(wv, gv, mv, vv):
        return _adam_math(wv, gv, mv, vv)
    return _rows_call(fn, "adam_small", [w, g, m, v], [], [(LANES, F32)] * 3, tile=w.shape[0])


def kernel(x, positions, ln_mix_g, ln_mlp_g, sf_w_in, sf_b_f, fox_q_g, fox_k_g, sf_w_o, mla_w_down, mla_q_a_g, mla_kv_a_g, mla_w_uq, mla_w_ukv, mla_q_g, mla_k_g, mla_w_o, mlp_w_up, mlp_w_down, loss_target, m_ln_mix_g, m_ln_mlp_g, m_sf_w_in, m_sf_b_f, m_fox_q_g, m_fox_k_g, m_sf_w_o, m_mla_w_down, m_mla_q_a_g, m_mla_kv_a_g, m_mla_w_uq, m_mla_w_ukv, m_mla_q_g, m_mla_k_g, m_mla_w_o, m_mlp_w_up, m_mlp_w_down, v_ln_mix_g, v_ln_mlp_g, v_sf_w_in, v_sf_b_f, v_fox_q_g, v_fox_k_g, v_sf_w_o, v_mla_w_down, v_mla_q_a_g, v_mla_kv_a_g, v_mla_w_uq, v_mla_w_ukv, v_mla_q_g, v_mla_k_g, v_mla_w_o, v_mlp_w_up, v_mlp_w_down):
    given = dict(locals())
    wts = {n: given[n] for n in ALL_W}
    mom = {n: given["m_" + n] for n in ALL_W}
    var = {n: given["v_" + n] for n in ALL_W}
    me = 4 * lax.axis_index("x") + 2 * lax.axis_index("y") + lax.axis_index("c")
    t_rows, d_model = x.shape[1], x.shape[2]
    big = sorted({name for name, _, _ in PIECES.values()})

    def whole_pieces(gathered, group):
        return {p: _unshard(s, PIECES[p][2]) for p, s in zip(GROUPS[group], gathered)}

    def slots_of(group, grads):
        return [_shard_stack(_piece_grad(grads, p), PIECES[p][2]) for p in GROUPS[group]]

    cast = {n: _cast_bf16(wts[n].reshape(-1, wts[n].shape[2]), f"cast_{n}").reshape(wts[n].shape) for n in big}
    blocks = {grp: [cast[PIECES[p][0]][PIECES[p][1]] for p in GROUPS[grp]] for grp in ("mix0", "mlp0", "layer1")}
    mix0 = whole_pieces(_exchange("gather", blocks["mix0"], "gather_mix0"), "mix0")
    gains, = _exchange("gather", [_as_rows(jnp.concatenate([mla_q_a_g, mla_kv_a_g], axis=1))], "gather_gains")
    lora_n = mla_q_a_g.shape[1]
    gains_flat = gains.reshape(N_DEV, -1)[:, :2 * lora_n]
    small = dict(ln_mix_g=ln_mix_g, ln_mlp_g=ln_mlp_g, sf_b_f=sf_b_f, fox_q_g=fox_q_g, fox_k_g=fox_k_g,
                 mla_q_a_g=gains_flat[:, :lora_n].reshape(1, -1), mla_kv_a_g=gains_flat[:, lora_n:].reshape(1, -1),
                 mla_q_g=mla_q_g, mla_k_g=mla_k_g)
    dist = dict(blocks=blocks, slots_of=slots_of,
                weights_of=lambda grp, gathered: WEIGHTS_OF[grp](whole_pieces(gathered, grp)))
    sq_err, grad_x, g, received = _local_step(x[0], positions.reshape(t_rows, 1), loss_target[0],
                                              _weights_mix0(mix0, small), dist)

    recv_of = {p: r for grp in ("mlp1", "mla", "mlp0", "mix0") for p, r in zip(GROUPS[grp], received[grp])}
    results = {kind: {} for kind in ("grad", "delta", "new_m", "new_v")}
    for n in big:
        layers = [p for _, p in sorted((layer, p) for p, (name, layer, _) in PIECES.items() if name == n)]
        outs = _adam_big([recv_of[p] for p in layers], wts[n], mom[n], var[n], f"adam_{n}")
        for kind, out in zip(("grad", "delta", "new_m", "new_v"), outs):
            results[kind][n] = out

    small_g = _small_grads(g)
    small_parts = [_as_rows(small_g[n], 8) for n in SMALL] + [_as_rows(sq_err, 8)]
    small_sum = _sum_slots(_exchange("gather", [_pack_rows(small_parts, 0, F32, 8, 8)], "gather_small_grads")[0])
    red, off = {}, 0
    for n, p in zip(SMALL + ["loss"], small_parts):
        red[n] = small_sum[off:off + p.shape[0]].reshape(-1)
        off += p.shape[0]
    loss = 0.5 * red["loss"][0] / d_model
    for n in SMALL:
        if n in ("mla_q_a_g", "mla_kv_a_g"):
            results["grad"][n] = lax.dynamic_slice(red[n], (me * lora_n,), (lora_n,)).reshape(wts[n].shape)
        else:
            results["grad"][n] = red[n][:wts[n].size].reshape(wts[n].shape)
    pack_small = lambda d: jnp.concatenate([_as_rows(d[n], 8) for n in SMALL], axis=0)
    small_out = _adam_small(pack_small(wts), pack_small(results["grad"]), pack_small(mom), pack_small(var))
    off = 0
    for n in SMALL:
        r = _as_rows(wts[n], 8).shape[0]
        for kind, packed in zip(["delta", "new_m", "new_v"], small_out):
            results[kind][n] = packed[off:off + r].reshape(-1)[:wts[n].size].reshape(wts[n].shape)
        off += r

    outs = [loss, grad_x[None]]
    for kind in ["grad", "delta", "new_m", "new_v"]:
        outs += [results[kind][n] for n in ALL_W]
    return tuple(outs)
```

```python
import functools
import math

import jax
import jax.numpy as jnp
import numpy as np
from jax import lax
from jax.experimental import pallas as pl
from jax.experimental.pallas import tpu as pltpu

F32 = jnp.float32
BF16 = jnp.bfloat16

NORM_EPS = 1e-6
ROPE_THETA = 10000.0
HEAD_DIM = 128
ROPE_DIM = 64
MLA_HEADS = 16
MLA_QK_DIM = 192
MLA_PAD_DIM = 256
ADAM_LR, ADAM_B1, ADAM_B2, ADAM_EPS, ADAM_WD, ADAM_STEP = 0.001, 0.9, 0.999, 1e-08, 0.01, 10

N_DEV = 8
LANES = 128
VMEM_LIMIT = 56 * 1024 * 1024
MATMUL_VMEM_BUDGET = 40 * 1024 * 1024
MASKED = -1e30
ATTN_TQ, ATTN_TK = 256, 256
ATTN_TQ_WIDE = 512
MESH = pl.DeviceIdType.MESH

NT_DIMS = (((1,), (1,)), ((), ()))
TN_DIMS = (((0,), (0,)), ((), ()))
NN_DIMS = (((1,), (0,)), ((), ()))


def _params(*sem):
    return pltpu.CompilerParams(dimension_semantics=sem, vmem_limit_bytes=VMEM_LIMIT)


def _pick(n, pref):
    best = None
    for t in range(LANES, min(n, pref) + 1, LANES):
        if n % t == 0:
            best = t
    return n if best is None or 2 * best < min(n, pref) else best


def _rows_call(fn, name, row_ins, full_ins, row_outs, acc_outs=(), tile=256):
    row_ins = [r if isinstance(r, tuple) else (r, r.shape[1], 0) for r in row_ins]
    t_rows = row_ins[0][0].shape[0]
    assert t_rows % tile == 0
    n_in = len(row_ins) + len(full_ins)
    n_row_out = len(row_outs)

    def body(*refs):
        res = fn(*[r[...] for r in refs[:n_in]])
        res = res if isinstance(res, tuple) else (res,)
        for ref, val in zip(refs[n_in:n_in + n_row_out], res[:n_row_out]):
            ref[...] = val.astype(ref.dtype)
        acc_refs = refs[n_in + n_row_out:]
        if acc_refs:
            @pl.when(pl.program_id(0) == 0)
            def _():
                for ref in acc_refs:
                    ref[...] = jnp.zeros_like(ref)
            for ref, val in zip(acc_refs, res[n_row_out:]):
                ref[...] += val.astype(ref.dtype)

    in_specs = [pl.BlockSpec((tile, w), functools.partial(lambda i, cb: (i, cb), cb=cb)) for _, w, cb in row_ins]
    in_specs += [pl.BlockSpec(a.shape, lambda i: (0, 0)) for a in full_ins]
    out_specs = [pl.BlockSpec((tile, c), lambda i: (i, 0)) for c, _ in row_outs]
    out_specs += [pl.BlockSpec(s, lambda i: (0, 0)) for s, _ in acc_outs]
    out_shape = [jax.ShapeDtypeStruct((t_rows, c), d) for c, d in row_outs]
    out_shape += [jax.ShapeDtypeStruct(s, d) for s, d in acc_outs]
    outs = pl.pallas_call(
        body, name=name, grid=(t_rows // tile,), in_specs=in_specs, out_specs=out_specs, out_shape=out_shape,
        compiler_params=_params("arbitrary"),
    )(*[r[0] for r in row_ins], *full_ins)
    return outs[0] if len(outs) == 1 else tuple(outs)


def _matmul_tiles(m, n, k, in_bytes, out_bytes):
    tn = n if n <= 1280 else _pick(n, 1024)
    tks = [k] + [k // d for d in (2, 4, 8, 16) if k % (d * LANES) == 0]
    for tk in [t for t in tks if t <= 4096] or [tks[-1]]:
        for tm in (1024, 512, 256):
            if m % tm:
                continue
            acc = 2 * tm * tn * 4 if tk < k else tm * tn * 4
            if 2 * (tm * tk + tk * tn) * in_bytes + 2 * tm * tn * out_bytes + acc <= MATMUL_VMEM_BUDGET:
                return tm, tn, tk
    raise ValueError(f"no matmul tiling for {m}x{n}x{k}")


def _matmul(a, b, form, name, out_dtypes=(F32,), epilogue=None, extras=(), hosted=()):
    if form == "nn":
        (m, k), n = a.shape, b.shape[1]
    elif form == "nt":
        (m, k), n = a.shape, b.shape[0]
    else:
        (k, m), n = a.shape, b.shape[1]
    in_bytes = max(a.dtype.itemsize, b.dtype.itemsize)
    out_bytes = sum(jnp.dtype(d).itemsize for d in out_dtypes) + sum(e.dtype.itemsize for e in extras)
    tm, tn, tk = _matmul_tiles(m, n, k, in_bytes, out_bytes)
    nk = k // tk
    dims = {"nn": NN_DIMS, "nt": NT_DIMS, "tn": TN_DIMS}[form]
    n_extra, n_out, n_host = len(extras), len(out_dtypes), len(hosted)
    grid = (m // tm, n // tn, nk)

    def body(*refs):
        a_ref, b_ref = refs[0], refs[1]
        extra_refs = refs[2:2 + n_extra]
        base = 2 + n_extra + n_host
        out_refs = refs[base:base + n_out]
        sems_at = base + n_out + n_host
        step = (pl.program_id(0) * grid[1] + pl.program_id(1)) * nk + pl.program_id(2)
        host_args = (hosted, refs[2 + n_extra:base], refs[base + n_out:sems_at], refs[sems_at:sems_at + 3 * n_host],
                     step, grid[0] * grid[1] * nk)
        _run_hosted(*host_args, "start")

        def finish(acc):
            vals = (acc,) if epilogue is None else epilogue(acc, *[r[...] for r in extra_refs])
            for ref, val in zip(out_refs, vals):
                ref[...] = val.astype(ref.dtype)

        part = lax.dot_general(a_ref[...].astype(BF16), b_ref[...].astype(BF16), dims, preferred_element_type=F32)
        if nk == 1:
            finish(part)
        else:
            acc_ref = refs[-1]
            kk = pl.program_id(2)

            @pl.when(kk == 0)
            def _():
                acc_ref[...] = part

            @pl.when(kk > 0)
            def _():
                acc_ref[...] += part

            @pl.when(kk == nk - 1)
            def _():
                finish(acc_ref[...])
        _run_hosted(*host_args, "finish")

    a_spec = pl.BlockSpec((tk, tm), lambda i, j, kk: (kk, i)) if form == "tn" else pl.BlockSpec((tm, tk), lambda i, j, kk: (i, kk))
    b_spec = pl.BlockSpec((tn, tk), lambda i, j, kk: (j, kk)) if form == "nt" else pl.BlockSpec((tk, tn), lambda i, j, kk: (kk, j))
    o_spec = pl.BlockSpec((tm, tn), lambda i, j, kk: (i, j))
    outs = pl.pallas_call(
        body, name=name, grid=grid, in_specs=[a_spec, b_spec] + [o_spec] * n_extra + [ANY_SPEC] * n_host,
        out_specs=[o_spec] * n_out + [ANY_SPEC] * n_host,
        out_shape=[jax.ShapeDtypeStruct((m, n), d) for d in out_dtypes]
        + [_exchange_out_shape(kd, arr) for kd, arr in hosted],
        scratch_shapes=EXCHANGE_SCRATCH * n_host + ([pltpu.VMEM((tm, tn), F32)] if nk > 1 else []),
        compiler_params=_params(*(("arbitrary",) * 3 if n_host else ("parallel", "parallel", "arbitrary"))),
    )(a, b, *extras, *[arr for _, arr in hosted])
    return outs[0] if n_out + n_host == 1 else tuple(outs)


def _add_residual(acc, res):
    return (acc + res,)


def _log_sigmoid_parts(z):
    return jnp.log1p(jnp.exp(-jnp.abs(z)))


def _rms_fwd(x, g, n=None):
    n = x.shape[-1] if n is None else n
    r = lax.rsqrt(jnp.sum(x * x, axis=-1, keepdims=True) / n + NORM_EPS)
    return x * r * g


def _rms_bwd(x, g, dout, n=None):
    n = x.shape[-1] if n is None else n
    r = lax.rsqrt(jnp.sum(x * x, axis=-1, keepdims=True) / n + NORM_EPS)
    y = x * r
    dg = jnp.sum(dout * y, axis=0, keepdims=True)
    dy = dout * g
    dx = r * (dy - y * (jnp.sum(dy * y, axis=-1, keepdims=True) / n))
    return dx, dg


def _swap_halves(r):
    lane = lax.broadcasted_iota(jnp.int32, r.shape, 1)
    return jnp.where(lane < ROPE_DIM // 2, pltpu.roll(r, LANES - ROPE_DIM // 2, 1), pltpu.roll(r, ROPE_DIM // 2, 1))


def _rope_fwd(r, cos_t, sin_s):
    return r * cos_t + _swap_halves(r) * sin_s


def _rope_bwd(dr, cos_t, sin_s):
    return dr * cos_t + _swap_halves(dr * sin_s)


def _split3(x):
    hi = x.astype(BF16)
    r1 = x - hi.astype(F32)
    mid = r1.astype(BF16)
    lo = (r1 - mid.astype(F32)).astype(BF16)
    return hi, mid, lo


def _mesh_position():
    x, y, c = lax.axis_index("x"), lax.axis_index("y"), lax.axis_index("c")
    return x, y, c, 4 * x + 2 * y + c


def _peer(x, y, c, k):
    bx, by, bc = (k >> 2) & 1, (k >> 1) & 1, k & 1
    px, py, pc = x ^ bx, y ^ by, c ^ bc
    return (px, py, pc), 4 * px + 2 * py + pc


def _gather_steps(x_ref, out_ref, send_sems, recv_sems, local_sem):
    x, y, c, me = _mesh_position()
    sibling = (x, y, 1 - c)
    chips = [(1 - x, y), (x, 1 - y), (1 - x, 1 - y)]

    def slot(px, py, pc):
        return out_ref.at[4 * px + 2 * py + pc]

    def copy(k, blk, to, src=None):
        return pltpu.make_async_remote_copy(
            src_ref=slot(*blk) if src is None else src, dst_ref=slot(*blk), send_sem=send_sems.at[k],
            recv_sem=recv_sems.at[k], device_id=to, device_id_type=MESH)

    mine = pltpu.make_async_copy(x_ref, out_ref.at[me], local_sem)
    first = [copy(0, (x, y, c), sibling, src=x_ref)]
    first += [copy(1 + j, (x, y, c), (*chip, c), src=x_ref) for j, chip in enumerate(chips)]
    passed = [copy(4 + j, (*chip, c), sibling) for j, chip in enumerate(chips)]

    def start():
        mine.start()
        for cp in first:
            cp.start()

    def forward():
        for j, chip in enumerate(chips):
            copy(1 + j, (*chip, c), (x, y, c)).wait_recv()
            passed[j].start()

    def finish():
        copy(0, (x, y, 1 - c), (x, y, c)).wait_recv()
        for j, chip in enumerate(chips):
            copy(4 + j, (*chip, 1 - c), (x, y, c)).wait_recv()
        for cp in first + passed:
            cp.wait_send()
        mine.wait()

    return start, forward, finish


def _all_to_all_steps(g_ref, out_ref, send_sems, recv_sems, local_sem):
    x, y, c, me = _mesh_position()
    mine = pltpu.make_async_copy(g_ref.at[me], out_ref.at[me], local_sem)
    copies = []
    for k in range(1, N_DEV):
        peer, peer_idx = _peer(x, y, c, k)
        copies.append(pltpu.make_async_remote_copy(
            src_ref=g_ref.at[peer_idx], dst_ref=out_ref.at[me], send_sem=send_sems.at[k - 1],
            recv_sem=recv_sems.at[k - 1], device_id=peer, device_id_type=MESH))

    def start():
        mine.start()
        for cp in copies:
            cp.start()

    def finish():
        for k in range(1, N_DEV):
            peer, peer_idx = _peer(x, y, c, k)
            pltpu.make_async_remote_copy(
                src_ref=g_ref.at[me], dst_ref=out_ref.at[peer_idx], send_sem=send_sems.at[k - 1],
                recv_sem=recv_sems.at[k - 1], device_id=peer, device_id_type=MESH).wait_recv()
        for cp in copies:
            cp.wait_send()
        mine.wait()

    return start, None, finish


EXCHANGE_STEPS = {"gather": _gather_steps, "all_to_all": _all_to_all_steps}
EXCHANGE_SCRATCH = [pltpu.SemaphoreType.DMA((7,)), pltpu.SemaphoreType.DMA((7,)), pltpu.SemaphoreType.DMA]
ANY_SPEC = pl.BlockSpec(memory_space=pl.ANY)


def _exchange_out_shape(kind, arr):
    return jax.ShapeDtypeStruct(((N_DEV,) + arr.shape) if kind == "gather" else arr.shape, arr.dtype)


def _exchange(kind, arrs, name):
    n = len(arrs)

    def body(*refs):
        steps = [EXCHANGE_STEPS[kind](refs[i], refs[n + i], *refs[2 * n + 3 * i:2 * n + 3 * i + 3]) for i in range(n)]
        for start, _, _ in steps:
            start()
        for _, forward, _ in steps:
            if forward is not None:
                forward()
        for _, _, finish in steps:
            finish()

    return pl.pallas_call(body, name=name, out_shape=[_exchange_out_shape(kind, a) for a in arrs],
                          in_specs=[ANY_SPEC] * n, out_specs=[ANY_SPEC] * n, scratch_shapes=EXCHANGE_SCRATCH * n)(*arrs)


def _run_hosted(hosted, src_refs, dst_refs, sem_refs, step, n_steps, when):
    for idx, (kind, _) in enumerate(hosted):
        start, forward, finish = EXCHANGE_STEPS[kind](src_refs[idx], dst_refs[idx], *sem_refs[3 * idx:3 * idx + 3])
        if when == "start":
            pl.when(step == 0)(start)
            if forward is not None:
                pl.when(step == (3 * n_steps) // 4)(forward)
        else:
            pl.when(step == n_steps - 1)(finish)


def _causal_iotas(qi, tq, tk):
    row = qi * tq + lax.broadcasted_iota(jnp.int32, (tq, tk), 0)
    col = lax.broadcasted_iota(jnp.int32, (tq, tk), 1)
    return row, col


def _suffix_matrix(tk, inclusive):
    j = lax.broadcasted_iota(jnp.int32, (2 * tk, tk), 0) % tk
    s = lax.broadcasted_iota(jnp.int32, (2 * tk, tk), 1)
    return jnp.where((j >= s) if inclusive else (j > s), 1.0, 0.0).astype(BF16)


def _suffix_sum(x, mat):
    hi = x.astype(BF16)
    lo = (x - hi.astype(F32)).astype(BF16)
    return lax.dot_general(jnp.concatenate([hi, lo], axis=1), mat, NN_DIMS, preferred_element_type=F32)


def _attn_specs(t_rows, tq, heads, dk, dv, q_off, k_off, v_off):
    q_spec = pl.BlockSpec((tq, dk), lambda h, i: (i, q_off + h))
    kt_spec = pl.BlockSpec((dk, t_rows), lambda h, i: (k_off + h, 0))
    v_spec = pl.BlockSpec((t_rows, dv), lambda h, i: (0, v_off + h))
    return q_spec, kt_spec, v_spec


def _split_weights(weights):
    hi = weights.astype(BF16)
    return hi, (weights - hi.astype(F32)).astype(BF16)


def _weighted_values(split, v):
    return (lax.dot_general(split[0], v, NN_DIMS, preferred_element_type=F32),
            lax.dot_general(split[1], v, NN_DIMS, preferred_element_type=F32))


def _attn_fwd(kind, q_arr, kt_arr, v_arr, heads, dk, dv, scale, name, q_off=0, k_off=0, v_off=0, fcol=None, frow=None,
              tq=ATTN_TQ, tk=ATTN_TK, hosted=()):
    t_rows = q_arr.shape[0]
    tq, tk = min(tq, t_rows), min(tk, t_rows)
    nq = t_rows // tq
    stick = kind == "stick"
    decay = fcol is not None
    n_in = 5 if decay else 3
    n_out = 2 if stick else 3
    n_host = len(hosted)

    def body(*refs):
        q_ref, kt_ref, v_ref = refs[:3]
        fcol_ref, frow_ref = (refs[3], refs[4]) if decay else (None, None)
        base = n_in + n_host
        o_ref, fine_ref = refs[base], refs[base + 1]
        lse_ref = None if stick else refs[base + 2]
        host_args = (hosted, refs[n_in:base], refs[base + n_out:base + n_out + n_host], refs[base + n_out + n_host:],
                     pl.program_id(0) * nq + pl.program_id(1), heads * nq)
        _run_hosted(*host_args, "start")
        qi = pl.program_id(1)
        q = q_ref[...]
        row, col = _causal_iotas(qi, tq, tk)
        n_kb = ((qi + 1) * tq + tk - 1) // tk
        zeros_o = jnp.zeros((tq, dv), F32)

        no_weights = (jnp.zeros((tq, tk), BF16), jnp.zeros((tq, tk), BF16))

        def raw_logits(kb):
            return lax.dot_general(q, kt_ref[:, pl.ds(pl.multiple_of(kb * tk, tk), tk)], NN_DIMS,
                                   preferred_element_type=F32)

        def values(kb):
            return v_ref[pl.ds(pl.multiple_of(kb * tk, tk), tk), :]

        if stick:
            mat = _suffix_matrix(tk, inclusive=False)

            def step(i, carry):
                c, acc, rem, raw, prev = carry
                raw_next = raw_logits(jnp.maximum(n_kb - 2 - i, 0))
                d_acc, d_rem = _weighted_values(prev, values(jnp.minimum(n_kb - i, n_kb - 1)))
                ks = pl.multiple_of((n_kb - 1 - i) * tk, tk)
                z = raw * scale
                strict = (col + ks) < row
                lg = _log_sigmoid_parts(z)
                lom = jnp.where(strict, jnp.minimum(-z, 0.0) - lg, 0.0)
                log_w = (jnp.minimum(z, 0.0) - lg) + (_suffix_sum(lom, mat) + c)
                w = jnp.where(strict, jnp.exp(log_w), 0.0)
                return (c + jnp.sum(lom, axis=1, keepdims=True), acc + d_acc, rem + d_rem, raw_next,
                        _split_weights(w))

            _, acc, rem, _, last = lax.fori_loop(0, n_kb, step, (jnp.zeros((tq, 1), F32), zeros_o, zeros_o,
                                                                 raw_logits(n_kb - 1), no_weights))
            d_acc, d_rem = _weighted_values(last, values(0))
            acc, rem = acc + d_acc, rem + d_rem
            o_ref[...] = acc.astype(o_ref.dtype)
            fine_ref[...] = acc + rem
        else:
            fc = fcol_ref[...] if decay else None

            def step(kb, carry):
                m, l, acc, rem, raw, prev = carry
                raw_next = raw_logits(jnp.minimum(kb + 1, n_kb - 1))
                d_acc, d_rem = _weighted_values(prev, values(jnp.maximum(kb - 1, 0)))
                ks = pl.multiple_of(kb * tk, tk)
                s = raw * scale
                if decay:
                    s = (s + fc) - frow_ref[:, pl.ds(ks, tk)]
                s = jnp.where((col + ks) <= row, s, MASKED)
                m_new = jnp.maximum(m, jnp.max(s, axis=1, keepdims=True))
                alpha = jnp.exp(m - m_new)
                p = jnp.exp(s - m_new)
                l = alpha * l + jnp.sum(p, axis=1, keepdims=True)
                return m_new, l, alpha * (acc + d_acc), alpha * (rem + d_rem), raw_next, _split_weights(p)

            m, l, acc, rem, _, last = lax.fori_loop(
                0, n_kb, step, (jnp.full((tq, 1), MASKED, F32), jnp.zeros((tq, 1), F32), zeros_o, zeros_o,
                                raw_logits(0), no_weights))
            d_acc, d_rem = _weighted_values(last, values(n_kb - 1))
            acc, rem = acc + d_acc, rem + d_rem
            inv_l = 1.0 / l
            o_ref[...] = (acc * inv_l).astype(o_ref.dtype)
            fine_ref[...] = (acc + rem) * inv_l
            lse_ref[...] = m + jnp.log(l)
        _run_hosted(*host_args, "finish")

    q_spec, k_spec, v_spec = _attn_specs(t_rows, tq, heads, dk, dv, q_off, k_off, v_off)
    stat_spec = pl.BlockSpec((None, tq, 1), lambda h, i: (h, i, 0))
    ins, in_specs = [q_arr, kt_arr, v_arr], [q_spec, k_spec, v_spec]
    if decay:
        ins += [fcol, frow]
        in_specs += [stat_spec, pl.BlockSpec((None, 1, t_rows), lambda h, i: (h, 0, 0))]
    o_spec = pl.BlockSpec((tq, dv), lambda h, i: (i, h))
    out_specs = [o_spec, o_spec]
    out_shape = [jax.ShapeDtypeStruct((t_rows, heads * dv), BF16), jax.ShapeDtypeStruct((t_rows, heads * dv), F32)]
    if not stick:
        out_specs.append(stat_spec)
        out_shape.append(jax.ShapeDtypeStruct((heads, t_rows, 1), F32))
    return tuple(pl.pallas_call(
        body, name=name, grid=(heads, nq), in_specs=in_specs + [ANY_SPEC] * n_host,
        out_specs=out_specs + [ANY_SPEC] * n_host,
        out_shape=out_shape + [_exchange_out_shape(kd, arr) for kd, arr in hosted],
        scratch_shapes=EXCHANGE_SCRATCH * n_host,
        compiler_params=_params("arbitrary" if n_host else "parallel", "arbitrary"),
    )(*ins, *[arr for _, arr in hosted]))


def _attn_bwd(kind, q_arr, k_arr, kt_arr, vt_arr, o_arr, do_arr, heads, dk, dv, scale, name, q_off=0, k_off=0, kt_off=0,
              vt_off=0, do_off=0, lse=None, fcol=None, frow=None, tq=ATTN_TQ, tk=ATTN_TK, hosted=()):
    t_rows = q_arr.shape[0]
    tq, tk = min(tq, t_rows), min(tk, t_rows)
    nq = t_rows // tq
    stick = kind == "stick"
    decay = fcol is not None
    n_in = 6 + (0 if stick else 1) + (2 if decay else 0)
    n_out = 5 if decay else 3
    n_host = len(hosted)

    def body(*refs):
        q_ref, k_ref, kt_ref, vt_ref, o_ref, do_ref = refs[:6]
        lse_ref = None if stick else refs[6]
        fcol_ref, frow_ref = (refs[7], refs[8]) if decay else (None, None)
        base = n_in + n_host
        dq_ref, dk_ref, dv_ref = refs[base:base + 3]
        dfcol_ref, dfrow_ref = (refs[base + 3], refs[base + 4]) if decay else (None, None)
        host_args = (hosted, refs[n_in:base], refs[base + n_out:base + n_out + n_host], refs[base + n_out + n_host:],
                     pl.program_id(0) * nq + pl.program_id(1), heads * nq)
        _run_hosted(*host_args, "start")
        qi = pl.program_id(1)

        @pl.when(qi == 0)
        def _():
            dk_ref[...] = jnp.zeros_like(dk_ref)
            dv_ref[...] = jnp.zeros_like(dv_ref)
            if decay:
                dfrow_ref[...] = jnp.zeros_like(dfrow_ref)

        q = q_ref[...]
        do = do_ref[...]
        delta = jnp.sum(do.astype(F32) * o_ref[...], axis=1, keepdims=True)
        row, col = _causal_iotas(qi, tq, tk)
        n_kb = ((qi + 1) * tq + tk - 1) // tk

        no_pair = (jnp.zeros((tq, tk), BF16), jnp.zeros((tq, tk), BF16))

        def accumulate(kb, pair):
            at = pl.ds(pl.multiple_of(kb * tk, tk), tk)
            dk_ref[at, :] += lax.dot_general(pair[0], q, TN_DIMS, preferred_element_type=F32)
            dv_ref[at, :] += lax.dot_general(pair[1], do, TN_DIMS, preferred_element_type=F32)
            return lax.dot_general(pair[0], k_ref[at, :], NN_DIMS, preferred_element_type=F32)

        def raw_products(kb):
            at = pl.ds(pl.multiple_of(kb * tk, tk), tk)
            return (lax.dot_general(q, kt_ref[:, at], NN_DIMS, preferred_element_type=F32),
                    lax.dot_general(do, vt_ref[:, at], NN_DIMS, preferred_element_type=F32))

        if stick:
            mat_ex = _suffix_matrix(tk, inclusive=False)
            mat_in = _suffix_matrix(tk, inclusive=True)

            def step(i, carry):
                c, gs, dq, (raw, dw), prev = carry
                raw_next = raw_products(jnp.maximum(n_kb - 2 - i, 0))
                dq = dq + accumulate(jnp.minimum(n_kb - i, n_kb - 1), prev)
                ks = pl.multiple_of((n_kb - 1 - i) * tk, tk)
                z = raw * scale
                strict = (col + ks) < row
                lg = _log_sigmoid_parts(z)
                log_beta = jnp.minimum(z, 0.0) - lg
                log_omb = jnp.minimum(-z, 0.0) - lg
                lom = jnp.where(strict, log_omb, 0.0)
                w = jnp.where(strict, jnp.exp(log_beta + (_suffix_sum(lom, mat_ex) + c)), 0.0)
                g = w * dw
                g_before = delta - (gs + _suffix_sum(g, mat_in))
                dz = jnp.where(strict, g * jnp.exp(log_omb) - g_before * jnp.exp(log_beta), 0.0)
                return (c + jnp.sum(lom, axis=1, keepdims=True), gs + jnp.sum(g, axis=1, keepdims=True), dq,
                        raw_next, ((dz * scale).astype(BF16), w.astype(BF16)))

            zero = jnp.zeros((tq, 1), F32)
            _, _, dq, _, last = lax.fori_loop(0, n_kb, step, (zero, zero, jnp.zeros((tq, dk), F32),
                                                              raw_products(n_kb - 1), no_pair))
            dq = dq + accumulate(0, last)
        else:
            lse_v = lse_ref[...]
            fc = fcol_ref[...] if decay else None

            def step(kb, carry):
                dq, row_sum, (raw, dp), prev = carry
                raw_next = raw_products(jnp.minimum(kb + 1, n_kb - 1))
                dq = dq + accumulate(jnp.maximum(kb - 1, 0), prev)
                ks = pl.multiple_of(kb * tk, tk)
                s = raw * scale
                if decay:
                    s = (s + fc) - frow_ref[:, pl.ds(ks, tk)]
                p = jnp.where((col + ks) <= row, jnp.exp(s - lse_v), 0.0)
                ds = p * (dp - delta)
                if decay:
                    dfrow_ref[:, pl.ds(ks, tk)] += jnp.sum(ds, axis=0, keepdims=True)
                    row_sum = row_sum + jnp.sum(ds, axis=1, keepdims=True)
                return dq, row_sum, raw_next, ((ds * scale).astype(BF16), p.astype(BF16))

            dq, row_sum, _, last = lax.fori_loop(0, n_kb, step, (jnp.zeros((tq, dk), F32), jnp.zeros((tq, 1), F32),
                                                                raw_products(0), no_pair))
            dq = dq + accumulate(n_kb - 1, last)
            if decay:
                dfcol_ref[...] = row_sum
        dq_ref[...] = dq
        _run_hosted(*host_args, "finish")

    q_spec, kt_spec, _ = _attn_specs(t_rows, tq, heads, dk, dv, q_off, kt_off, 0)
    stat_spec = pl.BlockSpec((None, tq, 1), lambda h, i: (h, i, 0))
    frow_spec = pl.BlockSpec((None, 1, t_rows), lambda h, i: (h, 0, 0))
    ins = [q_arr, k_arr, kt_arr, vt_arr, o_arr, do_arr]
    in_specs = [q_spec, pl.BlockSpec((t_rows, dk), lambda h, i: (0, k_off + h)), kt_spec,
                pl.BlockSpec((dv, t_rows), lambda h, i: (vt_off + h, 0)), pl.BlockSpec((tq, dv), lambda h, i: (i, h)),
                pl.BlockSpec((tq, dv), lambda h, i: (i, do_off + h))]
    if not stick:
        ins.append(lse)
        in_specs.append(stat_spec)
    if decay:
        ins += [fcol, frow]
        in_specs += [stat_spec, frow_spec]
    out_specs = [pl.BlockSpec((tq, dk), lambda h, i: (i, h)), pl.BlockSpec((t_rows, dk), lambda h, i: (0, h)),
                 pl.BlockSpec((t_rows, dv), lambda h, i: (0, h))]
    out_shape = [jax.ShapeDtypeStruct((t_rows, heads * dk), F32), jax.ShapeDtypeStruct((t_rows, heads * dk), F32),
                 jax.ShapeDtypeStruct((t_rows, heads * dv), F32)]
    if decay:
        out_specs += [stat_spec, frow_spec]
        out_shape += [jax.ShapeDtypeStruct((heads, t_rows, 1), F32), jax.ShapeDtypeStruct((heads, 1, t_rows), F32)]
    return pl.pallas_call(
        body, name=name, grid=(heads, nq), in_specs=in_specs + [ANY_SPEC] * n_host,
        out_specs=out_specs + [ANY_SPEC] * n_host,
        out_shape=out_shape + [_exchange_out_shape(kd, arr) for kd, arr in hosted],
        scratch_shapes=EXCHANGE_SCRATCH * n_host,
        compiler_params=_params("arbitrary" if n_host else "parallel", "arbitrary"),
    )(*ins, *[arr for _, arr in hosted])


def _prefix_matrix(reverse):
    j = lax.broadcasted_iota(jnp.int32, (LANES, LANES), 0)
    s = lax.broadcasted_iota(jnp.int32, (LANES, LANES), 1)
    return jnp.where((j >= s) if reverse else (j <= s), 1.0, 0.0).astype(BF16)


def _chunk_cumsum(x, mat):
    return sum(lax.dot_general(part, mat, NN_DIMS, preferred_element_type=F32) for part in _split3(x))


def _gate_fwd(logit_t, bias_col):
    heads, t_rows = logit_t.shape

    def body(x_ref, b_ref, out_ref):
        mat = _prefix_matrix(reverse=False)

        def step(ci, carry):
            cs = pl.multiple_of(ci * LANES, LANES)
            pre = x_ref[:, pl.ds(cs, LANES)] + b_ref[...]
            log_f = jnp.minimum(pre, 0.0) - _log_sigmoid_parts(pre)
            out_ref[:, pl.ds(cs, LANES)] = _chunk_cumsum(log_f, mat) + carry
            return carry + jnp.sum(log_f, axis=1, keepdims=True)

        lax.fori_loop(0, t_rows // LANES, step, jnp.zeros((heads, 1), F32))

    return pl.pallas_call(body, name="gate_fwd", out_shape=jax.ShapeDtypeStruct((heads, t_rows), F32),
                          compiler_params=pltpu.CompilerParams(vmem_limit_bytes=VMEM_LIMIT))(logit_t, bias_col)


def _gate_bwd(dcum_t, logit_t, bias_col):
    heads, t_rows = logit_t.shape
    n_chunks = t_rows // LANES

    def body(d_ref, x_ref, b_ref, dx_ref, db_ref):
        mat = _prefix_matrix(reverse=True)

        def step(i, carry):
            tail, db = carry
            cs = pl.multiple_of((n_chunks - 1 - i) * LANES, LANES)
            d = d_ref[:, pl.ds(cs, LANES)]
            d_log_f = _chunk_cumsum(d, mat) + tail
            pre = x_ref[:, pl.ds(cs, LANES)] + b_ref[...]
            e = jnp.exp(-jnp.abs(pre))
            d_pre = d_log_f * (jnp.where(pre >= 0.0, e, 1.0) / (1.0 + e))
            dx_ref[:, pl.ds(cs, LANES)] = d_pre
            return tail + jnp.sum(d, axis=1, keepdims=True), db + jnp.sum(d_pre, axis=1, keepdims=True)

        zero = jnp.zeros((heads, 1), F32)
        _, db = lax.fori_loop(0, n_chunks, step, (zero, zero))
        db_ref[...] = db

    return pl.pallas_call(body, name="gate_bwd",
                          out_shape=(jax.ShapeDtypeStruct((heads, t_rows), F32), jax.ShapeDtypeStruct((heads, 1), F32)),
                          compiler_params=pltpu.CompilerParams(vmem_limit_bytes=VMEM_LIMIT))(dcum_t, logit_t, bias_col)


def _norm_fwd(x, g, name):
    return _rows_call(lambda xv, gv: _rms_fwd(xv, gv), name, [x], [g], [(x.shape[1], BF16)])


def _norm_bwd(x, g, dh, dres, name):
    def fn(xv, dhv, dresv, gv):
        dx, dg = _rms_bwd(xv, gv, dhv)
        dx = dresv + dx
        return dx, dx, dg
    return _rows_call(fn, name, [x, dh, dres], [g], [(x.shape[1], F32), (x.shape[1], BF16)], [((1, x.shape[1]), F32)])


def _loss_fwd_bwd(y, target):
    d_model = y.shape[1]

    def fn(yv, tv):
        err = yv - tv
        dy = err * (1.0 / d_model)
        return dy, dy, jnp.sum(jnp.sum(err * err, axis=1, keepdims=True), axis=0, keepdims=True)
    return _rows_call(fn, "loss", [y, target], [], [(d_model, F32), (d_model, BF16)], [((1, 1), F32)])


def _heads_apply(fn, n_heads, width, *tiles):
    return [fn(*[t[:, h * width:(h + 1) * width] for t in tiles]) for h in range(n_heads)]


def _fox_norm_fwd(pb, gq, gk, heads):
    width = heads * HEAD_DIM

    def fn(qk, gqv, gkv):
        q = jnp.concatenate(_heads_apply(lambda t: _rms_fwd(t, gqv), heads, HEAD_DIM, qk[:, :width]), axis=1)
        k = jnp.concatenate(_heads_apply(lambda t: _rms_fwd(t, gkv), heads, HEAD_DIM, qk[:, width:]), axis=1)
        return q, k
    return _rows_call(fn, "fox_norm_fwd", [(pb, 2 * width, 0)], [gq, gk], [(width, BF16), (width, BF16)])


def _fox_norm_bwd(pb, gq, gk, dq, dk, heads):
    width = heads * HEAD_DIM

    def fn(qk, dqv, dkv, gqv, gkv):
        res_q = _heads_apply(lambda t, d: _rms_bwd(t, gqv, d), heads, HEAD_DIM, qk[:, :width], dqv)
        res_k = _heads_apply(lambda t, d: _rms_bwd(t, gkv, d), heads, HEAD_DIM, qk[:, width:], dkv)
        dqk = jnp.concatenate([r[0] for r in res_q] + [r[0] for r in res_k], axis=1)
        return dqk, sum(r[1] for r in res_q), sum(r[1] for r in res_k)
    return _rows_call(fn, "fox_norm_bwd", [(pb, 2 * width, 0), dq, dk], [gq, gk], [(2 * width, BF16)],
                      [((1, HEAD_DIM), F32), ((1, HEAD_DIM), F32)])


def _lora_norm_fwd(down, gq, gkv, rank):
    def fn(dv, gqv, gkvv):
        return _rms_fwd(dv[:, :rank], gqv), _rms_fwd(dv[:, rank:], gkvv)
    return _rows_call(fn, "lora_norm_fwd", [(down, 2 * rank, 0)], [gq, gkv], [(rank, BF16), (rank, BF16)])


def _lora_norm_bwd(down, gq, gkv, dcq, dckv, dkpe, rank):
    def fn(dv, dcqv, dckvv, dkpev, gqv, gkvv):
        dxq, dgq = _rms_bwd(dv[:, :rank], gqv, dcqv)
        dxkv, dgkv = _rms_bwd(dv[:, rank:], gkvv, dckvv)
        return jnp.concatenate([dxq, dxkv, dkpev], axis=1), dgq, dgkv
    return _rows_call(fn, "lora_norm_bwd", [(down, 2 * rank, 0), dcq, dckv, dkpe], [gq, gkv],
                      [(2 * rank + LANES, BF16)], [((1, rank), F32), ((1, rank), F32)])


def _rope_tables(pos_col, inv_freq, sin_sign):
    def fn(pos, invf, sign):
        ang = pos.astype(F32) * invf
        return jnp.cos(ang) * jnp.abs(sign), jnp.sin(ang) * sign
    return _rows_call(fn, "rope_tables", [pos_col], [inv_freq, sin_sign], [(LANES, F32), (LANES, F32)])


def _mla_prep_fwd(q_raw, kv, down, kpe_block, qg, kg, cos_t, sin_s):
    def fn(qv, kvv, kpe, cosv, sinv, qgv, kgv):
        qs, ks, vs = [], [], []
        for h in range(MLA_HEADS):
            qn = _rms_fwd(qv[:, h * MLA_PAD_DIM:(h + 1) * MLA_PAD_DIM], qgv, MLA_QK_DIM)
            qs += [qn[:, :HEAD_DIM], _rope_fwd(qn[:, HEAD_DIM:], cosv, sinv)]
            k_full = jnp.concatenate([kvv[:, h * MLA_PAD_DIM:h * MLA_PAD_DIM + HEAD_DIM], kpe], axis=1)
            kn = _rms_fwd(k_full, kgv, MLA_QK_DIM)
            ks += [kn[:, :HEAD_DIM], _rope_fwd(kn[:, HEAD_DIM:], cosv, sinv)]
            vs.append(kvv[:, h * MLA_PAD_DIM + HEAD_DIM:(h + 1) * MLA_PAD_DIM])
        return jnp.concatenate(qs, axis=1), jnp.concatenate(ks, axis=1), jnp.concatenate(vs, axis=1)
    wide = MLA_HEADS * MLA_PAD_DIM
    return _rows_call(fn, "mla_prep_fwd", [q_raw, kv, (down, LANES, kpe_block), cos_t, sin_s], [qg, kg],
                      [(wide, BF16), (wide, BF16), (MLA_HEADS * HEAD_DIM, BF16)], tile=128)


def _mla_prep_bwd(q_raw, kv, down, kpe_block, qg, kg, cos_t, sin_s, dq, dk, dv):
    def fn(qv, kvv, kpe, cosv, sinv, dqv, dkv, dvv, qgv, kgv):
        dqs, dkvs = [], []
        dkpe = jnp.zeros_like(kpe)
        dqg = jnp.zeros_like(qgv)
        dkg = jnp.zeros_like(kgv)
        for h in range(MLA_HEADS):
            lo, hi = h * MLA_PAD_DIM, (h + 1) * MLA_PAD_DIM
            dqn = jnp.concatenate([dqv[:, lo:lo + HEAD_DIM], _rope_bwd(dqv[:, lo + HEAD_DIM:hi], cosv, sinv)], axis=1)
            dqh, dg = _rms_bwd(qv[:, lo:hi], qgv, dqn, MLA_QK_DIM)
            dqs.append(dqh)
            dqg = dqg + dg
            k_full = jnp.concatenate([kvv[:, lo:lo + HEAD_DIM], kpe], axis=1)
            dkn = jnp.concatenate([dkv[:, lo:lo + HEAD_DIM], _rope_bwd(dkv[:, lo + HEAD_DIM:hi], cosv, sinv)], axis=1)
            dkh, dg = _rms_bwd(k_full, kgv, dkn, MLA_QK_DIM)
            dkg = dkg + dg
            dkpe = dkpe + dkh[:, HEAD_DIM:]
            dkvs += [dkh[:, :HEAD_DIM], dvv[:, h * HEAD_DIM:(h + 1) * HEAD_DIM]]
        return jnp.concatenate(dqs, axis=1), jnp.concatenate(dkvs, axis=1), dkpe, dqg, dkg
    wide = MLA_HEADS * MLA_PAD_DIM
    return _rows_call(fn, "mla_prep_bwd", [q_raw, kv, (down, LANES, kpe_block), cos_t, sin_s, dq, dk, dv], [qg, kg],
                      [(wide, BF16), (wide, BF16), (LANES, F32)], [((1, MLA_PAD_DIM), F32), ((1, MLA_PAD_DIM), F32)],
                      tile=128)


def _sqrelu_up(acc):
    return acc, jnp.square(jnp.maximum(acc, 0.0))


def _sqrelu_grad(acc, u):
    return (acc * (2.0 * jnp.maximum(u, 0.0)),)


def _mlp_fwd(x, g, w_up, w_down, tag):
    h = _norm_fwd(x, g, f"mlp_norm_fwd{tag}")
    u, a = _matmul(h, w_up, "nn", f"mlp_up{tag}", (F32, BF16), _sqrelu_up)
    return _matmul(a, w_down, "nn", f"mlp_down{tag}", (F32,), _add_residual, (x,)), (h, u, a)


def _mlp_bwd(x, g, w_up, w_down, saved, dy, dy16, tag):
    h, u, a = saved
    dw_down = _matmul(a, dy16, "tn", f"mlp_dwdown{tag}", (BF16,))
    du = _matmul(dy16, w_down, "nt", f"mlp_du{tag}", (BF16,), _sqrelu_grad, (u,))
    dw_up = _matmul(h, du, "tn", f"mlp_dwup{tag}", (BF16,))
    dh = _matmul(du, w_up, "nt", f"mlp_dh{tag}")
    dx, dx16, dg = _norm_bwd(x, g, dh, dy, f"mlp_norm_bwd{tag}")
    return dx, dx16, dg, dw_up, dw_down


def _local_step(x, pos_col, target, w, dist=None):
    w = dict(w)
    hs = w["w_a"].shape[1] // (4 * HEAD_DIM)
    sb_w = hs * HEAD_DIM
    grads, received = {}, {}

    def gather_in(group):
        return [("gather", blk) for blk in dist["blocks"][group]] if dist else []

    def exchange_in(group):
        return [("all_to_all", slots) for slots in dist["slots_of"](group, grads)] if dist else []

    h0 = _norm_fwd(x, w["ln_mix0"], "mix0_norm_fwd")
    pa = _matmul(h0, w["w_a"], "nn", "in_proj_a", (BF16,))
    pb = _matmul(h0, w["w_b"], "nn", "in_proj_b")
    pat = pa[:, sb_w:].T
    o_sb, o_sb_fine, *got = _attn_fwd("stick", pa, pat, pa, hs, HEAD_DIM, HEAD_DIM, HEAD_DIM ** -0.5, "stick_fwd",
                                      q_off=0, k_off=0, v_off=2 * hs, tq=ATTN_TQ_WIDE, hosted=gather_in("mlp0"))
    if dist:
        w.update(dist["weights_of"]("mlp0", got))
    logit_t = pb[:, 2 * sb_w:2 * sb_w + hs].T
    bias_col = w["b_f"][0, :hs].reshape(hs, 1)
    f_cum = _gate_fwd(logit_t, bias_col)
    f_col, f_row = f_cum[:, :, None], f_cum[:, None, :]
    qf, kf = _fox_norm_fwd(pb, w["fox_q_g"], w["fox_k_g"], hs)
    kft = kf.T
    o_fx, o_fx_fine, lse_fx, *got = _attn_fwd("softmax", qf, kft, pa, hs, HEAD_DIM, HEAD_DIM, HEAD_DIM ** -0.5,
                                              "fox_fwd", v_off=3 * hs, fcol=f_col, frow=f_row, tq=ATTN_TQ_WIDE,
                                              hosted=gather_in("layer1"))
    if dist:
        w.update(dist["weights_of"]("layer1", got))
    o0 = jnp.concatenate([o_sb, o_fx], axis=1)
    x1 = _matmul(o0, w["w_o0"], "nn", "out_proj0", (F32,), _add_residual, (x,))
    x2, mlp0 = _mlp_fwd(x1, w["ln_mlp0"], w["w_up0"], w["w_dn0"], "0")

    rank = w["w_uq"].shape[0]
    h2 = _norm_fwd(x2, w["ln_mix1"], "mix1_norm_fwd")
    down = _matmul(h2, w["w_down"], "nn", "mla_down")
    cqn, ckvn = _lora_norm_fwd(down, w["q_a_g"], w["kv_a_g"], rank)
    q_raw = _matmul(cqn, w["w_uq"], "nn", "mla_uq")
    kv = _matmul(ckvn, w["w_ukv"], "nn", "mla_ukv")
    cos_t, sin_s = _rope_tables(pos_col, w["inv_freq"], w["sin_sign"])
    kpe_block = 2 * rank // LANES
    qm, km, vm = _mla_prep_fwd(q_raw, kv, down, kpe_block, w["mla_q_g"], w["mla_k_g"], cos_t, sin_s)
    kmt, vmt = km.T, vm.T
    o_m, o_m_fine, lse_m = _attn_fwd("softmax", qm, kmt, vm, MLA_HEADS, MLA_PAD_DIM, HEAD_DIM, MLA_QK_DIM ** -0.5,
                                     "mla_fwd", tq=ATTN_TQ_WIDE)
    x3 = _matmul(o_m, w["w_o1"], "nn", "out_proj1", (F32,), _add_residual, (x2,))
    x4, mlp1 = _mlp_fwd(x3, w["ln_mlp1"], w["w_up1"], w["w_dn1"], "1")

    dy, dy16, sq_err = _loss_fwd_bwd(x4, target)

    dx3, dx3_16, grads["ln_mlp1"], grads["w_up1"], grads["w_dn1"] = _mlp_bwd(
        x3, w["ln_mlp1"], w["w_up1"], w["w_dn1"], mlp1, dy, dy16, "1")
    grads["w_o1"] = _matmul(o_m, dx3_16, "tn", "dw_o1", (BF16,))
    do_m = _matmul(dx3_16, w["w_o1"], "nt", "do_mla", (BF16,))
    dqm, dkm, dvm, *got = _attn_bwd("softmax", qm, km, kmt, vmt, o_m_fine, do_m, MLA_HEADS, MLA_PAD_DIM, HEAD_DIM,
                                    MLA_QK_DIM ** -0.5, "mla_bwd", lse=lse_m, tq=ATTN_TQ_WIDE,
                                    hosted=exchange_in("mlp1"))
    received["mlp1"] = got
    dq_raw, dkv, dkpe, grads["mla_q_g"], grads["mla_k_g"] = _mla_prep_bwd(
        q_raw, kv, down, kpe_block, w["mla_q_g"], w["mla_k_g"], cos_t, sin_s, dqm, dkm, dvm)
    grads["w_uq"] = _matmul(cqn, dq_raw, "tn", "dw_uq", (BF16,))
    grads["w_ukv"] = _matmul(ckvn, dkv, "tn", "dw_ukv", (BF16,))
    dcqn = _matmul(dq_raw, w["w_uq"], "nt", "d_cq")
    dckvn = _matmul(dkv, w["w_ukv"], "nt", "d_ckv")
    ddown, grads["q_a_g"], grads["kv_a_g"] = _lora_norm_bwd(down, w["q_a_g"], w["kv_a_g"], dcqn, dckvn, dkpe, rank)
    grads["w_down"] = _matmul(h2, ddown, "tn", "dw_down", (BF16,))
    dh2 = _matmul(ddown, w["w_down"], "nt", "d_h2")
    dx2, dx2_16, grads["ln_mix1"] = _norm_bwd(x2, w["ln_mix1"], dh2, dx3, "mix1_norm_bwd")

    dx1, dx1_16, grads["ln_mlp0"], grads["w_up0"], grads["w_dn0"] = _mlp_bwd(
        x1, w["ln_mlp0"], w["w_up0"], w["w_dn0"], mlp0, dx2, dx2_16, "0")
    grads["w_o0"] = _matmul(o0, dx1_16, "tn", "dw_o0", (BF16,))
    do0 = _matmul(dx1_16, w["w_o0"], "nt", "do_mix0", (BF16,))
    dq_sb, dk_sb, dv_sb, *got = _attn_bwd("stick", pa, pa, pat, pat, o_sb_fine, do0, hs, HEAD_DIM, HEAD_DIM,
                                          HEAD_DIM ** -0.5, "stick_bwd", q_off=0, k_off=hs, kt_off=0, vt_off=hs, do_off=0,
                                          hosted=exchange_in("with_stick_bwd"))
    received["with_stick_bwd"] = got
    dqf, dkf, dv_fx, ds_rows, ds_cols, *got = _attn_bwd(
        "softmax", qf, kf, kft, pat, o_fx_fine, do0, hs, HEAD_DIM, HEAD_DIM, HEAD_DIM ** -0.5, "fox_bwd", vt_off=2 * hs,
        do_off=hs, lse=lse_fx, fcol=f_col, frow=f_row, hosted=exchange_in("with_fox_bwd"))
    received["with_fox_bwd"] = got
    dqk_fx, grads["fox_q_g"], grads["fox_k_g"] = _fox_norm_bwd(pb, w["fox_q_g"], w["fox_k_g"], dqf, dkf, hs)
    dlogit_t, db_f = _gate_bwd(ds_rows[:, :, 0] - ds_cols[:, 0, :], logit_t, bias_col)
    grads["b_f"] = db_f.reshape(1, hs)
    dpa = jnp.concatenate([dq_sb.astype(BF16), dk_sb.astype(BF16), dv_sb.astype(BF16), dv_fx.astype(BF16)], axis=1)
    dlogit_pad = jnp.pad(dlogit_t.T.astype(BF16), ((0, 0), (0, pb.shape[1] - 2 * sb_w - hs)))
    dpb = jnp.concatenate([dqk_fx, dlogit_pad], axis=1)
    grads["w_a"] = _matmul(h0, dpa, "tn", "dw_a", (BF16,))
    grads["w_b"] = _matmul(h0, dpb, "tn", "dw_b", (BF16,))
    dh0 = _matmul(dpb, w["w_b"], "nt", "d_h0_b")
    res = _matmul(dpa, w["w_a"], "nt", "d_h0_a", (F32,), _add_residual, (dh0,), hosted=exchange_in("mix0"))
    dh0, received["mix0"] = (res[0], list(res[1:])) if dist else (res, [])
    grad_x, _, grads["ln_mix0"] = _norm_bwd(x, w["ln_mix0"], dh0, dx1, "mix0_norm_bwd")
    return sq_err, grad_x, grads, received


PIECES = {
    "sf_w_in": ("sf_w_in", 0, 1), "sf_w_o": ("sf_w_o", 0, 0), "mla_w_down": ("mla_w_down", 0, 0),
    "mla_w_uq": ("mla_w_uq", 0, 1), "mla_w_ukv": ("mla_w_ukv", 0, 1), "mla_w_o": ("mla_w_o", 0, 0),
    "mlp_w_up0": ("mlp_w_up", 0, 1), "mlp_w_up1": ("mlp_w_up", 1, 1),
    "mlp_w_down0": ("mlp_w_down", 0, 0), "mlp_w_down1": ("mlp_w_down", 1, 0),
}
GROUPS = {
    "mix0": ["sf_w_in"], "mlp0": ["sf_w_o", "mlp_w_up0", "mlp_w_down0"],
    "mla": ["mla_w_down", "mla_w_uq", "mla_w_ukv", "mla_w_o"], "mlp1": ["mlp_w_up1", "mlp_w_down1"],
}
GROUPS["layer1"] = GROUPS["mla"] + GROUPS["mlp1"]
GROUPS["with_stick_bwd"] = GROUPS["mla"] + ["mlp_w_up0"]
GROUPS["with_fox_bwd"] = ["mlp_w_down0", "sf_w_o"]
SMALL = ["ln_mix_g", "ln_mlp_g", "sf_b_f", "fox_q_g", "fox_k_g", "mla_q_a_g", "mla_kv_a_g", "mla_q_g", "mla_k_g"]
ALL_W = ["ln_mix_g", "ln_mlp_g", "sf_w_in", "sf_b_f", "fox_q_g", "fox_k_g", "sf_w_o", "mla_w_down", "mla_q_a_g",
         "mla_kv_a_g", "mla_w_uq", "mla_w_ukv", "mla_q_g", "mla_k_g", "mla_w_o", "mlp_w_up", "mlp_w_down"]


def _weights_mix0(full, small):
    w_in = full["sf_w_in"]
    d_model = w_in.shape[0]
    n_fx = small["sf_b_f"].shape[1]
    sb_w = (w_in.shape[1] - n_fx) // 6
    cols = lambda i: w_in[:, i * sb_w:(i + 1) * sb_w]
    w_a = jnp.concatenate([cols(0), cols(1), cols(2), cols(5)], axis=1)
    w_b = jnp.concatenate([cols(3), cols(4), w_in[:, 6 * sb_w:], jnp.zeros((d_model, LANES - n_fx), w_in.dtype)], axis=1)
    half = ROPE_DIM // 2
    inv_freq = ROPE_THETA ** (-jnp.arange(half, dtype=F32) / half)
    zeros64 = jnp.zeros((ROPE_DIM,), F32)
    pad256 = lambda g: jnp.pad(g, ((0, 0), (0, MLA_PAD_DIM - MLA_QK_DIM)))
    pad_lanes = lambda g: jnp.pad(g, ((0, 0), (0, LANES - g.shape[1])))
    return dict(
        ln_mix0=small["ln_mix_g"][0:1], ln_mix1=small["ln_mix_g"][1:2],
        ln_mlp0=small["ln_mlp_g"][0:1], ln_mlp1=small["ln_mlp_g"][1:2],
        w_a=w_a, w_b=w_b, b_f=pad_lanes(small["sf_b_f"]), fox_q_g=small["fox_q_g"], fox_k_g=small["fox_k_g"],
        q_a_g=small["mla_q_a_g"], kv_a_g=small["mla_kv_a_g"],
        mla_q_g=pad256(small["mla_q_g"]), mla_k_g=pad256(small["mla_k_g"]),
        inv_freq=jnp.concatenate([inv_freq, inv_freq, zeros64]).reshape(1, LANES),
        sin_sign=jnp.concatenate([-jnp.ones((half,), F32), jnp.ones((half,), F32), zeros64]).reshape(1, LANES),
    )


def _weights_mlp0(full):
    return dict(w_o0=full["sf_w_o"], w_up0=full["mlp_w_up0"], w_dn0=full["mlp_w_down0"])


def _weights_layer1(full):
    rank = full["mla_w_uq"].shape[0]
    w_uq = full["mla_w_uq"].reshape(rank, MLA_HEADS, MLA_QK_DIM)
    w_uq = jnp.pad(w_uq, ((0, 0), (0, 0), (0, MLA_PAD_DIM - MLA_QK_DIM))).reshape(rank, MLA_HEADS * MLA_PAD_DIM)
    return dict(w_down=jnp.pad(full["mla_w_down"], ((0, 0), (0, LANES - ROPE_DIM))), w_uq=w_uq,
                w_ukv=full["mla_w_ukv"], w_o1=full["mla_w_o"], w_up1=full["mlp_w_up1"], w_dn1=full["mlp_w_down1"])


WEIGHTS_OF = {"mlp0": _weights_mlp0, "layer1": _weights_layer1}


def _piece_grad(g, piece):
    if piece == "sf_w_in":
        n_fx = g["b_f"].shape[1]
        ga, gb = g["w_a"], g["w_b"]
        sb_w = ga.shape[1] // 4
        ca = lambda i: ga[:, i * sb_w:(i + 1) * sb_w]
        return jnp.concatenate([ca(0), ca(1), ca(2), gb[:, :sb_w], gb[:, sb_w:2 * sb_w], ca(3),
                                gb[:, 2 * sb_w:2 * sb_w + n_fx]], axis=1)
    if piece == "mla_w_uq":
        rank = g["w_uq"].shape[0]
        return g["w_uq"].reshape(rank, MLA_HEADS, MLA_PAD_DIM)[:, :, :MLA_QK_DIM].reshape(rank, MLA_HEADS * MLA_QK_DIM)
    if piece == "mla_w_down":
        return g["w_down"][:, :g["w_down"].shape[1] - (LANES - ROPE_DIM)]
    return g[{"sf_w_o": "w_o0", "mla_w_ukv": "w_ukv", "mla_w_o": "w_o1", "mlp_w_up0": "w_up0", "mlp_w_up1": "w_up1",
              "mlp_w_down0": "w_dn0", "mlp_w_down1": "w_dn1"}[piece]]


def _small_grads(g):
    return {
        "ln_mix_g": jnp.concatenate([g["ln_mix0"], g["ln_mix1"]], axis=0),
        "ln_mlp_g": jnp.concatenate([g["ln_mlp0"], g["ln_mlp1"]], axis=0),
        "sf_b_f": g["b_f"], "fox_q_g": g["fox_q_g"], "fox_k_g": g["fox_k_g"],
        "mla_q_a_g": g["q_a_g"], "mla_kv_a_g": g["kv_a_g"],
        "mla_q_g": g["mla_q_g"][:, :MLA_QK_DIM], "mla_k_g": g["mla_k_g"][:, :MLA_QK_DIM],
    }


PACK_TILE = 1024


def _as_rows(a, row_multiple=16):
    flat = a.reshape(-1)
    rows = -(-flat.shape[0] // LANES)
    rows = -(-rows // row_multiple) * row_multiple
    return jnp.pad(flat, (0, rows * LANES - flat.shape[0])).reshape(rows, LANES)


def _pack_rows(parts, axis, dtype, row_multiple=PACK_TILE, spare_rows=0):
    used = sum(p.shape[axis] for p in parts)
    shape = list(parts[0].shape)
    shape[axis] = -(-used // row_multiple) * row_multiple + spare_rows - used
    return jnp.concatenate([p.astype(dtype) for p in parts] + [jnp.ones(shape, dtype)], axis=axis)


def _unshard(stack, axis):
    moved = jnp.moveaxis(stack, 0, axis)
    shape = list(stack.shape[1:])
    shape[axis] *= N_DEV
    return moved.reshape(shape)


def _shard_stack(full, axis):
    shape = list(full.shape)
    shape[axis:axis + 1] = [N_DEV, shape[axis] // N_DEV]
    return jnp.moveaxis(full.reshape(shape), axis, 0)


OPT_TILE_ELEMS = 128 * 1024


def _row_tile(rows, cols):
    best = 16
    for t in range(16, rows + 1, 16):
        if rows % t == 0 and t * cols <= OPT_TILE_ELEMS:
            best = t
    assert rows % best == 0
    return best


def _cast_bf16(a, name):
    return _rows_call(lambda v: v, name, [a], [], [(a.shape[1], BF16)], tile=_row_tile(*a.shape))


def _adam_math(w, g, m, v):
    m = ADAM_B1 * m + (1.0 - ADAM_B1) * g
    v = ADAM_B2 * v + (1.0 - ADAM_B2) * jnp.square(g)
    m_hat = m / (1.0 - ADAM_B1 ** ADAM_STEP)
    v_hat = v / (1.0 - ADAM_B2 ** ADAM_STEP)
    delta = -ADAM_LR * (m_hat / (jnp.sqrt(v_hat) + ADAM_EPS) + ADAM_WD * w)
    return delta, m, v


def _adam_big(recvs, w, m, v, name, hosted=()):
    layers, rows, cols = w.shape
    tile = _row_tile(rows, cols)
    n_tiles = rows // tile
    n_host = len(hosted)

    def body(*refs):
        recv_refs = refs[:layers]
        w_ref, m_ref, v_ref = refs[layers:layers + 3]
        base = layers + 3 + n_host
        g_ref, d_ref, nm_ref, nv_ref = refs[base:base + 4]
        layer = pl.program_id(0)
        host_args = (hosted, refs[layers + 3:base], refs[base + 4:base + 4 + n_host], refs[base + 4 + n_host:],
                     layer * n_tiles + pl.program_id(1), layers * n_tiles)
        _run_hosted(*host_args, "start")

        def total(r_ref):
            acc = r_ref[0].astype(F32)
            for s in range(1, N_DEV):
                acc = acc + r_ref[s].astype(F32)
            return acc

        g = total(recv_refs[0])
        for j in range(1, layers):
            g = jnp.where(layer == j, total(recv_refs[j]), g)
        delta, nm, nv = _adam_math(w_ref[...], g, m_ref[...], v_ref[...])
        g_ref[...] = g
        d_ref[...] = delta
        nm_ref[...] = nm
        nv_ref[...] = nv
        _run_hosted(*host_args, "finish")

    def recv_spec(j):
        return pl.BlockSpec((N_DEV, tile, cols),
                            lambda l, i: (0, jnp.where(l == j, i, jnp.where(l < j, 0, n_tiles - 1)), 0))

    spec = pl.BlockSpec((None, tile, cols), lambda l, i: (l, i, 0))
    out = jax.ShapeDtypeStruct(w.shape, F32)
    return pl.pallas_call(
        body, name=name, grid=(layers, n_tiles),
        in_specs=[recv_spec(j) for j in range(layers)] + [spec] * 3 + [ANY_SPEC] * n_host,
        out_specs=[spec] * 4 + [ANY_SPEC] * n_host,
        out_shape=[out] * 4 + [_exchange_out_shape(kd, arr) for kd, arr in hosted],
        scratch_shapes=EXCHANGE_SCRATCH * n_host, compiler_params=_params("arbitrary", "arbitrary"),
    )(*recvs, w, m, v, *[arr for _, arr in hosted])


def _sum_slots(gathered):
    rows = gathered.shape[1]

    def body(r_ref, o_ref):
        acc = r_ref[0]
        for s in range(1, N_DEV):
            acc = acc + r_ref[s]
        o_ref[...] = acc

    return pl.pallas_call(body, name="sum_small", out_shape=jax.ShapeDtypeStruct((rows, LANES), F32))(gathered)


def _adam_small(w, g, m, v):
    def fn(wv, gv, mv, vv):
        return _adam_math(wv, gv, mv, vv)
    return _rows_call(fn, "adam_small", [w, g, m, v], [], [(LANES, F32)] * 3, tile=w.shape[0])


def kernel(x, positions, ln_mix_g, ln_mlp_g, sf_w_in, sf_b_f, fox_q_g, fox_k_g, sf_w_o, mla_w_down, mla_q_a_g, mla_kv_a_g, mla_w_uq, mla_w_ukv, mla_q_g, mla_k_g, mla_w_o, mlp_w_up, mlp_w_down, loss_target, m_ln_mix_g, m_ln_mlp_g, m_sf_w_in, m_sf_b_f, m_fox_q_g, m_fox_k_g, m_sf_w_o, m_mla_w_down, m_mla_q_a_g, m_mla_kv_a_g, m_mla_w_uq, m_mla_w_ukv, m_mla_q_g, m_mla_k_g, m_mla_w_o, m_mlp_w_up, m_mlp_w_down, v_ln_mix_g, v_ln_mlp_g, v_sf_w_in, v_sf_b_f, v_fox_q_g, v_fox_k_g, v_sf_w_o, v_mla_w_down, v_mla_q_a_g, v_mla_kv_a_g, v_mla_w_uq, v_mla_w_ukv, v_mla_q_g, v_mla_k_g, v_mla_w_o, v_mlp_w_up, v_mlp_w_down):
    given = dict(locals())
    wts = {n: given[n] for n in ALL_W}
    mom = {n: given["m_" + n] for n in ALL_W}
    var = {n: given["v_" + n] for n in ALL_W}
    me = 4 * lax.axis_index("x") + 2 * lax.axis_index("y") + lax.axis_index("c")
    t_rows, d_model = x.shape[1], x.shape[2]
    big = sorted({name for name, _, _ in PIECES.values()})

    def whole_pieces(gathered, group):
        return {p: _unshard(s, PIECES[p][2]) for p, s in zip(GROUPS[group], gathered)}

    def w_in_slot_parts(grads):
        slots = _shard_stack(_piece_grad(grads, "sf_w_in"), PIECES["sf_w_in"][2])
        cuts = [0] + [(slots.shape[1] * f // 16) // 16 * 16 for f in (6, 11)] + [slots.shape[1]]
        return [slots[:, a:b] for a, b in zip(cuts, cuts[1:])]

    def slots_of(group, grads):
        if group == "mix0":
            return w_in_slot_parts(grads)[:1]
        return [_shard_stack(_piece_grad(grads, p), PIECES[p][2]) for p in GROUPS[group]]

    cast = {n: _cast_bf16(wts[n].reshape(-1, wts[n].shape[2]), f"cast_{n}").reshape(wts[n].shape) for n in big}
    blocks = {grp: [cast[PIECES[p][0]][PIECES[p][1]] for p in GROUPS[grp]] for grp in ("mix0", "mlp0", "layer1")}
    mix0 = whole_pieces(_exchange("gather", blocks["mix0"], "gather_mix0"), "mix0")
    gains, = _exchange("gather", [_as_rows(jnp.concatenate([mla_q_a_g, mla_kv_a_g], axis=1))], "gather_gains")
    lora_n = mla_q_a_g.shape[1]
    gains_flat = gains.reshape(N_DEV, -1)[:, :2 * lora_n]
    small = dict(ln_mix_g=ln_mix_g, ln_mlp_g=ln_mlp_g, sf_b_f=sf_b_f, fox_q_g=fox_q_g, fox_k_g=fox_k_g,
                 mla_q_a_g=gains_flat[:, :lora_n].reshape(1, -1), mla_kv_a_g=gains_flat[:, lora_n:].reshape(1, -1),
                 mla_q_g=mla_q_g, mla_k_g=mla_k_g)
    dist = dict(blocks=blocks, slots_of=slots_of,
                weights_of=lambda grp, gathered: WEIGHTS_OF[grp](whole_pieces(gathered, grp)))
    sq_err, grad_x, g, received = _local_step(x[0], positions.reshape(t_rows, 1), loss_target[0],
                                              _weights_mix0(mix0, small), dist)

    recv_of = {p: r for grp in ("mlp1", "with_stick_bwd", "with_fox_bwd") for p, r in zip(GROUPS[grp], received[grp])}
    late_parts = dict(zip(("mlp_w_down", "mlp_w_up"), w_in_slot_parts(g)[1:]))
    w_in_recv = list(received["mix0"])
    results = {kind: {} for kind in ("grad", "delta", "new_m", "new_v")}
    for n in sorted(big, key=lambda name: (name == "sf_w_in", name not in late_parts)):
        if n == "sf_w_in":
            recv_of["sf_w_in"] = jnp.concatenate(w_in_recv, axis=1)
        layers = [p for _, p in sorted((layer, p) for p, (name, layer, _) in PIECES.items() if name == n)]
        hosted = [("all_to_all", late_parts[n])] if n in late_parts else []
        outs = _adam_big([recv_of[p] for p in layers], wts[n], mom[n], var[n], f"adam_{n}", hosted)
        w_in_recv += outs[4:]
        for kind, out in zip(("grad", "delta", "new_m", "new_v"), outs[:4]):
            results[kind][n] = out

    small_g = _small_grads(g)
    small_parts = [_as_rows(small_g[n], 8) for n in SMALL] + [_as_rows(sq_err, 8)]
    small_sum = _sum_slots(_exchange("gather", [_pack_rows(small_parts, 0, F32, 8, 8)], "gather_small_grads")[0])
    red, off = {}, 0
    for n, p in zip(SMALL + ["loss"], small_parts):
        red[n] = small_sum[off:off + p.shape[0]].reshape(-1)
        off += p.shape[0]
    loss = 0.5 * red["loss"][0] / d_model
    for n in SMALL:
        if n in ("mla_q_a_g", "mla_kv_a_g"):
            results["grad"][n] = lax.dynamic_slice(red[n], (me * lora_n,), (lora_n,)).reshape(wts[n].shape)
        else:
            results["grad"][n] = red[n][:wts[n].size].reshape(wts[n].shape)
    pack_small = lambda d: jnp.concatenate([_as_rows(d[n], 8) for n in SMALL], axis=0)
    small_out = _adam_small(pack_small(wts), pack_small(results["grad"]), pack_small(mom), pack_small(var))
    off = 0
    for n in SMALL:
        r = _as_rows(wts[n], 8).shape[0]
        for kind, packed in zip(["delta", "new_m", "new_v"], small_out):
            results[kind][n] = packed[off:off + r].reshape(-1)[:wts[n].size].reshape(wts[n].shape)
        off += r

    outs = [loss, grad_x[None]]
    for kind in ["grad", "delta", "new_m", "new_v"]:
        outs += [results[kind][n] for n in ALL_W]
    return tuple(outs)
```

```python
import functools
import math

import jax
import jax.numpy as jnp
import numpy as np
from jax import lax
from jax.experimental import pallas as pl
from jax.experimental.pallas import tpu as pltpu

F32 = jnp.float32
BF16 = jnp.bfloat16

NORM_EPS = 1e-6
ROPE_THETA = 10000.0
HEAD_DIM = 128
ROPE_DIM = 64
MLA_HEADS = 16
MLA_QK_DIM = 192
MLA_PAD_DIM = 256
ADAM_LR, ADAM_B1, ADAM_B2, ADAM_EPS, ADAM_WD, ADAM_STEP = 0.001, 0.9, 0.999, 1e-08, 0.01, 10

N_DEV = 8
LANES = 128
VMEM_LIMIT = 56 * 1024 * 1024
MATMUL_VMEM_BUDGET = 40 * 1024 * 1024
MASKED = -1e30
ATTN_TQ, ATTN_TK = 256, 256
ATTN_TQ_WIDE = 512
ATTN_TK_SOFTMAX = 512
MESH = pl.DeviceIdType.MESH

NT_DIMS = (((1,), (1,)), ((), ()))
TN_DIMS = (((0,), (0,)), ((), ()))
NN_DIMS = (((1,), (0,)), ((), ()))


def _params(*sem):
    return pltpu.CompilerParams(dimension_semantics=sem, vmem_limit_bytes=VMEM_LIMIT)


def _pick(n, pref):
    best = None
    for t in range(LANES, min(n, pref) + 1, LANES):
        if n % t == 0:
            best = t
    return n if best is None or 2 * best < min(n, pref) else best


def _rows_call(fn, name, row_ins, full_ins, row_outs, acc_outs=(), tile=256):
    row_ins = [r if isinstance(r, tuple) else (r, r.shape[1], 0) for r in row_ins]
    t_rows = row_ins[0][0].shape[0]
    assert t_rows % tile == 0
    n_in = len(row_ins) + len(full_ins)
    n_row_out = len(row_outs)

    def body(*refs):
        res = fn(*[r[...] for r in refs[:n_in]])
        res = res if isinstance(res, tuple) else (res,)
        for ref, val in zip(refs[n_in:n_in + n_row_out], res[:n_row_out]):
            ref[...] = val.astype(ref.dtype)
        acc_refs = refs[n_in + n_row_out:]
        if acc_refs:
            @pl.when(pl.program_id(0) == 0)
            def _():
                for ref in acc_refs:
                    ref[...] = jnp.zeros_like(ref)
            for ref, val in zip(acc_refs, res[n_row_out:]):
                ref[...] += val.astype(ref.dtype)

    in_specs = [pl.BlockSpec((tile, w), functools.partial(lambda i, cb: (i, cb), cb=cb)) for _, w, cb in row_ins]
    in_specs += [pl.BlockSpec(a.shape, lambda i: (0, 0)) for a in full_ins]
    out_specs = [pl.BlockSpec((tile, c), lambda i: (i, 0)) for c, _ in row_outs]
    out_specs += [pl.BlockSpec(s, lambda i: (0, 0)) for s, _ in acc_outs]
    out_shape = [jax.ShapeDtypeStruct((t_rows, c), d) for c, d in row_outs]
    out_shape += [jax.ShapeDtypeStruct(s, d) for s, d in acc_outs]
    outs = pl.pallas_call(
        body, name=name, grid=(t_rows // tile,), in_specs=in_specs, out_specs=out_specs, out_shape=out_shape,
        compiler_params=_params("arbitrary"),
    )(*[r[0] for r in row_ins], *full_ins)
    return outs[0] if len(outs) == 1 else tuple(outs)


def _matmul_tiles(m, n, k, in_bytes, out_bytes):
    tn = n if n <= 1280 else _pick(n, 1024)
    tks = [k] + [k // d for d in (2, 4, 8, 16) if k % (d * LANES) == 0]
    for tk in [t for t in tks if t <= 4096] or [tks[-1]]:
        for tm in (1024, 512, 256):
            if m % tm:
                continue
            acc = 2 * tm * tn * 4 if tk < k else tm * tn * 4
            if 2 * (tm * tk + tk * tn) * in_bytes + 2 * tm * tn * out_bytes + acc <= MATMUL_VMEM_BUDGET:
                return tm, tn, tk
    raise ValueError(f"no matmul tiling for {m}x{n}x{k}")


def _matmul(a, b, form, name, out_dtypes=(F32,), epilogue=None, extras=(), hosted=()):
    if form == "nn":
        (m, k), n = a.shape, b.shape[1]
    elif form == "nt":
        (m, k), n = a.shape, b.shape[0]
    else:
        (k, m), n = a.shape, b.shape[1]
    in_bytes = max(a.dtype.itemsize, b.dtype.itemsize)
    out_bytes = sum(jnp.dtype(d).itemsize for d in out_dtypes) + sum(e.dtype.itemsize for e in extras)
    tm, tn, tk = _matmul_tiles(m, n, k, in_bytes, out_bytes)
    nk = k // tk
    dims = {"nn": NN_DIMS, "nt": NT_DIMS, "tn": TN_DIMS}[form]
    n_extra, n_out, n_host = len(extras), len(out_dtypes), len(hosted)
    grid = (m // tm, n // tn, nk)

    def body(*refs):
        a_ref, b_ref = refs[0], refs[1]
        extra_refs = refs[2:2 + n_extra]
        base = 2 + n_extra + n_host
        out_refs = refs[base:base + n_out]
        sems_at = base + n_out + n_host
        step = (pl.program_id(0) * grid[1] + pl.program_id(1)) * nk + pl.program_id(2)
        host_args = (hosted, refs[2 + n_extra:base], refs[base + n_out:sems_at], refs[sems_at:sems_at + 3 * n_host],
                     step, grid[0] * grid[1] * nk)
        _run_hosted(*host_args, "start")

        def finish(acc):
            vals = (acc,) if epilogue is None else epilogue(acc, *[r[...] for r in extra_refs])
            for ref, val in zip(out_refs, vals):
                ref[...] = val.astype(ref.dtype)

        part = lax.dot_general(a_ref[...].astype(BF16), b_ref[...].astype(BF16), dims, preferred_element_type=F32)
        if nk == 1:
            finish(part)
        else:
            acc_ref = refs[-1]
            kk = pl.program_id(2)

            @pl.when(kk == 0)
            def _():
                acc_ref[...] = part

            @pl.when(kk > 0)
            def _():
                acc_ref[...] += part

            @pl.when(kk == nk - 1)
            def _():
                finish(acc_ref[...])
        _run_hosted(*host_args, "finish")

    a_spec = pl.BlockSpec((tk, tm), lambda i, j, kk: (kk, i)) if form == "tn" else pl.BlockSpec((tm, tk), lambda i, j, kk: (i, kk))
    b_spec = pl.BlockSpec((tn, tk), lambda i, j, kk: (j, kk)) if form == "nt" else pl.BlockSpec((tk, tn), lambda i, j, kk: (kk, j))
    o_spec = pl.BlockSpec((tm, tn), lambda i, j, kk: (i, j))
    outs = pl.pallas_call(
        body, name=name, grid=grid, in_specs=[a_spec, b_spec] + [o_spec] * n_extra + [ANY_SPEC] * n_host,
        out_specs=[o_spec] * n_out + [ANY_SPEC] * n_host,
        out_shape=[jax.ShapeDtypeStruct((m, n), d) for d in out_dtypes]
        + [_exchange_out_shape(kd, arr) for kd, arr in hosted],
        scratch_shapes=EXCHANGE_SCRATCH * n_host + ([pltpu.VMEM((tm, tn), F32)] if nk > 1 else []),
        compiler_params=_params(*(("arbitrary",) * 3 if n_host else ("parallel", "parallel", "arbitrary"))),
    )(a, b, *extras, *[arr for _, arr in hosted])
    return outs[0] if n_out + n_host == 1 else tuple(outs)


def _add_residual(acc, res):
    return (acc + res,)


def _log_sigmoid_parts(z):
    return jnp.log1p(jnp.exp(-jnp.abs(z)))


def _rms_fwd(x, g, n=None):
    n = x.shape[-1] if n is None else n
    r = lax.rsqrt(jnp.sum(x * x, axis=-1, keepdims=True) / n + NORM_EPS)
    return x * r * g


def _rms_bwd(x, g, dout, n=None):
    n = x.shape[-1] if n is None else n
    r = lax.rsqrt(jnp.sum(x * x, axis=-1, keepdims=True) / n + NORM_EPS)
    y = x * r
    dg = jnp.sum(dout * y, axis=0, keepdims=True)
    dy = dout * g
    dx = r * (dy - y * (jnp.sum(dy * y, axis=-1, keepdims=True) / n))
    return dx, dg


def _swap_halves(r):
    lane = lax.broadcasted_iota(jnp.int32, r.shape, 1)
    return jnp.where(lane < ROPE_DIM // 2, pltpu.roll(r, LANES - ROPE_DIM // 2, 1), pltpu.roll(r, ROPE_DIM // 2, 1))


def _rope_fwd(r, cos_t, sin_s):
    return r * cos_t + _swap_halves(r) * sin_s


def _rope_bwd(dr, cos_t, sin_s):
    return dr * cos_t + _swap_halves(dr * sin_s)


def _split3(x):
    hi = x.astype(BF16)
    r1 = x - hi.astype(F32)
    mid = r1.astype(BF16)
    lo = (r1 - mid.astype(F32)).astype(BF16)
    return hi, mid, lo


def _mesh_position():
    x, y, c = lax.axis_index("x"), lax.axis_index("y"), lax.axis_index("c")
    return x, y, c, 4 * x + 2 * y + c


def _peer(x, y, c, k):
    bx, by, bc = (k >> 2) & 1, (k >> 1) & 1, k & 1
    px, py, pc = x ^ bx, y ^ by, c ^ bc
    return (px, py, pc), 4 * px + 2 * py + pc


def _gather_steps(x_ref, out_ref, send_sems, recv_sems, local_sem):
    x, y, c, me = _mesh_position()
    sibling = (x, y, 1 - c)
    chips = [(1 - x, y), (x, 1 - y), (1 - x, 1 - y)]

    def slot(px, py, pc):
        return out_ref.at[4 * px + 2 * py + pc]

    def copy(k, blk, to, src=None):
        return pltpu.make_async_remote_copy(
            src_ref=slot(*blk) if src is None else src, dst_ref=slot(*blk), send_sem=send_sems.at[k],
            recv_sem=recv_sems.at[k], device_id=to, device_id_type=MESH)

    mine = pltpu.make_async_copy(x_ref, out_ref.at[me], local_sem)
    first = [copy(0, (x, y, c), sibling, src=x_ref)]
    first += [copy(1 + j, (x, y, c), (*chip, c), src=x_ref) for j, chip in enumerate(chips)]
    passed = [copy(4 + j, (*chip, c), sibling) for j, chip in enumerate(chips)]

    def start():
        mine.start()
        for cp in first:
            cp.start()

    def forward():
        for j, chip in enumerate(chips):
            copy(1 + j, (*chip, c), (x, y, c)).wait_recv()
            passed[j].start()

    def finish():
        copy(0, (x, y, 1 - c), (x, y, c)).wait_recv()
        for j, chip in enumerate(chips):
            copy(4 + j, (*chip, 1 - c), (x, y, c)).wait_recv()
        for cp in first + passed:
            cp.wait_send()
        mine.wait()

    return start, forward, finish


def _all_to_all_steps(g_ref, out_ref, send_sems, recv_sems, local_sem):
    x, y, c, me = _mesh_position()
    mine = pltpu.make_async_copy(g_ref.at[me], out_ref.at[me], local_sem)
    copies = []
    for k in range(1, N_DEV):
        peer, peer_idx = _peer(x, y, c, k)
        copies.append(pltpu.make_async_remote_copy(
            src_ref=g_ref.at[peer_idx], dst_ref=out_ref.at[me], send_sem=send_sems.at[k - 1],
            recv_sem=recv_sems.at[k - 1], device_id=peer, device_id_type=MESH))

    def start():
        mine.start()
        for cp in copies:
            cp.start()

    def finish():
        for k in range(1, N_DEV):
            peer, peer_idx = _peer(x, y, c, k)
            pltpu.make_async_remote_copy(
                src_ref=g_ref.at[me], dst_ref=out_ref.at[peer_idx], send_sem=send_sems.at[k - 1],
                recv_sem=recv_sems.at[k - 1], device_id=peer, device_id_type=MESH).wait_recv()
        for cp in copies:
            cp.wait_send()
        mine.wait()

    return start, None, finish


EXCHANGE_STEPS = {"gather": _gather_steps, "all_to_all": _all_to_all_steps}
EXCHANGE_SCRATCH = [pltpu.SemaphoreType.DMA((7,)), pltpu.SemaphoreType.DMA((7,)), pltpu.SemaphoreType.DMA]
ANY_SPEC = pl.BlockSpec(memory_space=pl.ANY)


def _exchange_out_shape(kind, arr):
    return jax.ShapeDtypeStruct(((N_DEV,) + arr.shape) if kind == "gather" else arr.shape, arr.dtype)


def _exchange(kind, arrs, name):
    n = len(arrs)

    def body(*refs):
        steps = [EXCHANGE_STEPS[kind](refs[i], refs[n + i], *refs[2 * n + 3 * i:2 * n + 3 * i + 3]) for i in range(n)]
        for start, _, _ in steps:
            start()
        for _, forward, _ in steps:
            if forward is not None:
                forward()
        for _, _, finish in steps:
            finish()

    return pl.pallas_call(body, name=name, out_shape=[_exchange_out_shape(kind, a) for a in arrs],
                          in_specs=[ANY_SPEC] * n, out_specs=[ANY_SPEC] * n, scratch_shapes=EXCHANGE_SCRATCH * n)(*arrs)


def _run_hosted(hosted, src_refs, dst_refs, sem_refs, step, n_steps, when):
    for idx, (kind, _) in enumerate(hosted):
        start, forward, finish = EXCHANGE_STEPS[kind](src_refs[idx], dst_refs[idx], *sem_refs[3 * idx:3 * idx + 3])
        if when == "start":
            pl.when(step == 0)(start)
            if forward is not None:
                pl.when(step == (3 * n_steps) // 4)(forward)
        else:
            pl.when(step == n_steps - 1)(finish)


def _causal_iotas(qi, tq, tk):
    row = qi * tq + lax.broadcasted_iota(jnp.int32, (tq, tk), 0)
    col = lax.broadcasted_iota(jnp.int32, (tq, tk), 1)
    return row, col


def _suffix_matrix(tk, inclusive):
    j = lax.broadcasted_iota(jnp.int32, (2 * tk, tk), 0) % tk
    s = lax.broadcasted_iota(jnp.int32, (2 * tk, tk), 1)
    return jnp.where((j >= s) if inclusive else (j > s), 1.0, 0.0).astype(BF16)


def _suffix_sum(x, mat):
    hi = x.astype(BF16)
    lo = (x - hi.astype(F32)).astype(BF16)
    return lax.dot_general(jnp.concatenate([hi, lo], axis=1), mat, NN_DIMS, preferred_element_type=F32)


def _attn_specs(t_rows, tq, heads, dk, dv, q_off, k_off, v_off):
    q_spec = pl.BlockSpec((tq, dk), lambda h, i: (i, q_off + h))
    kt_spec = pl.BlockSpec((dk, t_rows), lambda h, i: (k_off + h, 0))
    v_spec = pl.BlockSpec((t_rows, dv), lambda h, i: (0, v_off + h))
    return q_spec, kt_spec, v_spec


def _split_weights(weights):
    hi = weights.astype(BF16)
    return hi, (weights - hi.astype(F32)).astype(BF16)


def _weighted_values(split, v):
    return (lax.dot_general(split[0], v, NN_DIMS, preferred_element_type=F32)
            + lax.dot_general(split[1], v, NN_DIMS, preferred_element_type=F32))


def _attn_fwd(kind, q_arr, kt_arr, v_arr, heads, dk, dv, scale, name, q_off=0, k_off=0, v_off=0, fcol=None, frow=None,
              tq=ATTN_TQ, tk=ATTN_TK, hosted=()):
    t_rows = q_arr.shape[0]
    tq, tk = min(tq, t_rows), min(tk, t_rows)
    nq = t_rows // tq
    stick = kind == "stick"
    decay = fcol is not None
    n_in = 5 if decay else 3
    n_out = 2 if stick else 3
    n_host = len(hosted)

    def body(*refs):
        q_ref, kt_ref, v_ref = refs[:3]
        fcol_ref, frow_ref = (refs[3], refs[4]) if decay else (None, None)
        base = n_in + n_host
        o_ref, fine_ref = refs[base], refs[base + 1]
        lse_ref = None if stick else refs[base + 2]
        host_args = (hosted, refs[n_in:base], refs[base + n_out:base + n_out + n_host], refs[base + n_out + n_host:],
                     pl.program_id(0) * nq + pl.program_id(1), heads * nq)
        _run_hosted(*host_args, "start")
        qi = pl.program_id(1)
        q = q_ref[...]
        row, col = _causal_iotas(qi, tq, tk)
        n_kb = ((qi + 1) * tq + tk - 1) // tk
        zeros_o = jnp.zeros((tq, dv), F32)

        no_weights = (jnp.zeros((tq, tk), BF16), jnp.zeros((tq, tk), BF16))

        def raw_logits(kb):
            return lax.dot_general(q, kt_ref[:, pl.ds(pl.multiple_of(kb * tk, tk), tk)], NN_DIMS,
                                   preferred_element_type=F32)

        def values(kb):
            return v_ref[pl.ds(pl.multiple_of(kb * tk, tk), tk), :]

        if stick:
            mat = _suffix_matrix(tk, inclusive=False)

            def step(i, carry):
                c, acc, raw, prev = carry
                raw_next = raw_logits(jnp.maximum(n_kb - 2 - i, 0))
                d_acc = _weighted_values(prev, values(jnp.minimum(n_kb - i, n_kb - 1)))
                ks = pl.multiple_of((n_kb - 1 - i) * tk, tk)
                z = raw * scale
                strict = (col + ks) < row
                lg = _log_sigmoid_parts(z)
                lom = jnp.where(strict, jnp.minimum(-z, 0.0) - lg, 0.0)
                log_w = (jnp.minimum(z, 0.0) - lg) + (_suffix_sum(lom, mat) + c)
                w = jnp.where(strict, jnp.exp(log_w), 0.0)
                return c + jnp.sum(lom, axis=1, keepdims=True), acc + d_acc, raw_next, _split_weights(w)

            _, acc, _, last = lax.fori_loop(0, n_kb, step, (jnp.zeros((tq, 1), F32), zeros_o, raw_logits(n_kb - 1),
                                                            no_weights))
            acc = acc + _weighted_values(last, values(0))
            o_ref[...] = acc.astype(o_ref.dtype)
            fine_ref[...] = acc
        else:
            fc = fcol_ref[...] if decay else None

            def step(kb, carry):
                m, l, acc, raw, prev = carry
                raw_next = raw_logits(jnp.minimum(kb + 1, n_kb - 1))
                d_acc = _weighted_values(prev, values(jnp.maximum(kb - 1, 0)))
                ks = pl.multiple_of(kb * tk, tk)
                s = raw * scale
                if decay:
                    s = (s + fc) - frow_ref[:, pl.ds(ks, tk)]
                s = jnp.where((col + ks) <= row, s, MASKED)
                m_new = jnp.maximum(m, jnp.max(s, axis=1, keepdims=True))
                alpha = jnp.exp(m - m_new)
                p = jnp.exp(s - m_new)
                l = alpha * l + jnp.sum(p, axis=1, keepdims=True)
                return m_new, l, alpha * (acc + d_acc), raw_next, _split_weights(p)

            m, l, acc, _, last = lax.fori_loop(
                0, n_kb, step, (jnp.full((tq, 1), MASKED, F32), jnp.zeros((tq, 1), F32), zeros_o, raw_logits(0),
                                no_weights))
            out = (acc + _weighted_values(last, values(n_kb - 1))) * (1.0 / l)
            o_ref[...] = out.astype(o_ref.dtype)
            fine_ref[...] = out
            lse_ref[...] = m + jnp.log(l)
        _run_hosted(*host_args, "finish")

    q_spec, k_spec, v_spec = _attn_specs(t_rows, tq, heads, dk, dv, q_off, k_off, v_off)
    stat_spec = pl.BlockSpec((None, tq, 1), lambda h, i: (h, i, 0))
    ins, in_specs = [q_arr, kt_arr, v_arr], [q_spec, k_spec, v_spec]
    if decay:
        ins += [fcol, frow]
        in_specs += [stat_spec, pl.BlockSpec((None, 1, t_rows), lambda h, i: (h, 0, 0))]
    o_spec = pl.BlockSpec((tq, dv), lambda h, i: (i, h))
    out_specs = [o_spec, o_spec]
    out_shape = [jax.ShapeDtypeStruct((t_rows, heads * dv), BF16), jax.ShapeDtypeStruct((t_rows, heads * dv), F32)]
    if not stick:
        out_specs.append(stat_spec)
        out_shape.append(jax.ShapeDtypeStruct((heads, t_rows, 1), F32))
    return tuple(pl.pallas_call(
        body, name=name, grid=(heads, nq), in_specs=in_specs + [ANY_SPEC] * n_host,
        out_specs=out_specs + [ANY_SPEC] * n_host,
        out_shape=out_shape + [_exchange_out_shape(kd, arr) for kd, arr in hosted],
        scratch_shapes=EXCHANGE_SCRATCH * n_host,
        compiler_params=_params("arbitrary" if n_host else "parallel", "arbitrary"),
    )(*ins, *[arr for _, arr in hosted]))


def _attn_bwd(kind, q_arr, k_arr, kt_arr, vt_arr, o_arr, do_arr, heads, dk, dv, scale, name, q_off=0, k_off=0, kt_off=0,
              vt_off=0, do_off=0, lse=None, fcol=None, frow=None, tq=ATTN_TQ, tk=ATTN_TK, hosted=()):
    t_rows = q_arr.shape[0]
    tq, tk = min(tq, t_rows), min(tk, t_rows)
    nq = t_rows // tq
    stick = kind == "stick"
    decay = fcol is not None
    n_in = 6 + (0 if stick else 1) + (2 if decay else 0)
    n_out = 5 if decay else 3
    n_host = len(hosted)

    def body(*refs):
        q_ref, k_ref, kt_ref, vt_ref, o_ref, do_ref = refs[:6]
        lse_ref = None if stick else refs[6]
        fcol_ref, frow_ref = (refs[7], refs[8]) if decay else (None, None)
        base = n_in + n_host
        dq_ref, dk_ref, dv_ref = refs[base:base + 3]
        dfcol_ref, dfrow_ref = (refs[base + 3], refs[base + 4]) if decay else (None, None)
        host_args = (hosted, refs[n_in:base], refs[base + n_out:base + n_out + n_host], refs[base + n_out + n_host:],
                     pl.program_id(0) * nq + pl.program_id(1), heads * nq)
        _run_hosted(*host_args, "start")
        qi = pl.program_id(1)

        @pl.when(qi == 0)
        def _():
            dk_ref[...] = jnp.zeros_like(dk_ref)
            dv_ref[...] = jnp.zeros_like(dv_ref)
            if decay:
                dfrow_ref[...] = jnp.zeros_like(dfrow_ref)

        q = q_ref[...]
        do = do_ref[...]
        delta = jnp.sum(do.astype(F32) * o_ref[...], axis=1, keepdims=True)
        row, col = _causal_iotas(qi, tq, tk)
        n_kb = ((qi + 1) * tq + tk - 1) // tk

        no_pair = (jnp.zeros((tq, tk), BF16), jnp.zeros((tq, tk), BF16))

        def accumulate(kb, pair):
            at = pl.ds(pl.multiple_of(kb * tk, tk), tk)
            dk_ref[at, :] += lax.dot_general(pair[0], q, TN_DIMS, preferred_element_type=F32)
            dv_ref[at, :] += lax.dot_general(pair[1], do, TN_DIMS, preferred_element_type=F32)
            return lax.dot_general(pair[0], k_ref[at, :], NN_DIMS, preferred_element_type=F32)

        def raw_products(kb):
            at = pl.ds(pl.multiple_of(kb * tk, tk), tk)
            return (lax.dot_general(q, kt_ref[:, at], NN_DIMS, preferred_element_type=F32),
                    lax.dot_general(do, vt_ref[:, at], NN_DIMS, preferred_element_type=F32))

        if stick:
            mat_ex = _suffix_matrix(tk, inclusive=False)
            mat_in = _suffix_matrix(tk, inclusive=True)

            def step(i, carry):
                c, gs, dq, (raw, dw), prev = carry
                raw_next = raw_products(jnp.maximum(n_kb - 2 - i, 0))
                dq = dq + accumulate(jnp.minimum(n_kb - i, n_kb - 1), prev)
                ks = pl.multiple_of((n_kb - 1 - i) * tk, tk)
                z = raw * scale
                strict = (col + ks) < row
                lg = _log_sigmoid_parts(z)
                log_beta = jnp.minimum(z, 0.0) - lg
                log_omb = jnp.minimum(-z, 0.0) - lg
                lom = jnp.where(strict, log_omb, 0.0)
                w = jnp.where(strict, jnp.exp(log_beta + (_suffix_sum(lom, mat_ex) + c)), 0.0)
                g = w * dw
                g_before = delta - (gs + _suffix_sum(g, mat_in))
                dz = jnp.where(strict, g * jnp.exp(log_omb) - g_before * jnp.exp(log_beta), 0.0)
                return (c + jnp.sum(lom, axis=1, keepdims=True), gs + jnp.sum(g, axis=1, keepdims=True), dq,
                        raw_next, ((dz * scale).astype(BF16), w.astype(BF16)))

            zero = jnp.zeros((tq, 1), F32)
            _, _, dq, _, last = lax.fori_loop(0, n_kb, step, (zero, zero, jnp.zeros((tq, dk), F32),
                                                              raw_products(n_kb - 1), no_pair))
            dq = dq + accumulate(0, last)
        else:
            lse_v = lse_ref[...]
            fc = fcol_ref[...] if decay else None

            def step(kb, carry):
                dq, row_sum, (raw, dp), prev = carry
                raw_next = raw_products(jnp.minimum(kb + 1, n_kb - 1))
                dq = dq + accumulate(jnp.maximum(kb - 1, 0), prev)
                ks = pl.multiple_of(kb * tk, tk)
                s = raw * scale
                if decay:
                    s = (s + fc) - frow_ref[:, pl.ds(ks, tk)]
                p = jnp.where((col + ks) <= row, jnp.exp(s - lse_v), 0.0)
                ds = p * (dp - delta)
                if decay:
                    dfrow_ref[:, pl.ds(ks, tk)] += jnp.sum(ds, axis=0, keepdims=True)
                    row_sum = row_sum + jnp.sum(ds, axis=1, keepdims=True)
                return dq, row_sum, raw_next, ((ds * scale).astype(BF16), p.astype(BF16))

            dq, row_sum, _, last = lax.fori_loop(0, n_kb, step, (jnp.zeros((tq, dk), F32), jnp.zeros((tq, 1), F32),
                                                                raw_products(0), no_pair))
            dq = dq + accumulate(n_kb - 1, last)
            if decay:
                dfcol_ref[...] = row_sum
        dq_ref[...] = dq
        _run_hosted(*host_args, "finish")

    q_spec, kt_spec, _ = _attn_specs(t_rows, tq, heads, dk, dv, q_off, kt_off, 0)
    stat_spec = pl.BlockSpec((None, tq, 1), lambda h, i: (h, i, 0))
    frow_spec = pl.BlockSpec((None, 1, t_rows), lambda h, i: (h, 0, 0))
    ins = [q_arr, k_arr, kt_arr, vt_arr, o_arr, do_arr]
    in_specs = [q_spec, pl.BlockSpec((t_rows, dk), lambda h, i: (0, k_off + h)), kt_spec,
                pl.BlockSpec((dv, t_rows), lambda h, i: (vt_off + h, 0)), pl.BlockSpec((tq, dv), lambda h, i: (i, h)),
                pl.BlockSpec((tq, dv), lambda h, i: (i, do_off + h))]
    if not stick:
        ins.append(lse)
        in_specs.append(stat_spec)
    if decay:
        ins += [fcol, frow]
        in_specs += [stat_spec, frow_spec]
    out_specs = [pl.BlockSpec((tq, dk), lambda h, i: (i, h)), pl.BlockSpec((t_rows, dk), lambda h, i: (0, h)),
                 pl.BlockSpec((t_rows, dv), lambda h, i: (0, h))]
    out_shape = [jax.ShapeDtypeStruct((t_rows, heads * dk), F32), jax.ShapeDtypeStruct((t_rows, heads * dk), F32),
                 jax.ShapeDtypeStruct((t_rows, heads * dv), F32)]
    if decay:
        out_specs += [stat_spec, frow_spec]
        out_shape += [jax.ShapeDtypeStruct((heads, t_rows, 1), F32), jax.ShapeDtypeStruct((heads, 1, t_rows), F32)]
    return pl.pallas_call(
        body, name=name, grid=(heads, nq), in_specs=in_specs + [ANY_SPEC] * n_host,
        out_specs=out_specs + [ANY_SPEC] * n_host,
        out_shape=out_shape + [_exchange_out_shape(kd, arr) for kd, arr in hosted],
        scratch_shapes=EXCHANGE_SCRATCH * n_host,
        compiler_params=_params("arbitrary" if n_host else "parallel", "arbitrary"),
    )(*ins, *[arr for _, arr in hosted])


def _prefix_matrix(reverse):
    j = lax.broadcasted_iota(jnp.int32, (LANES, LANES), 0)
    s = lax.broadcasted_iota(jnp.int32, (LANES, LANES), 1)
    return jnp.where((j >= s) if reverse else (j <= s), 1.0, 0.0).astype(BF16)


def _chunk_cumsum(x, mat):
    return sum(lax.dot_general(part, mat, NN_DIMS, preferred_element_type=F32) for part in _split3(x))


def _gate_fwd(logit_t, bias_col):
    heads, t_rows = logit_t.shape

    def body(x_ref, b_ref, out_ref):
        mat = _prefix_matrix(reverse=False)

        def step(ci, carry):
            cs = pl.multiple_of(ci * LANES, LANES)
            pre = x_ref[:, pl.ds(cs, LANES)] + b_ref[...]
            log_f = jnp.minimum(pre, 0.0) - _log_sigmoid_parts(pre)
            out_ref[:, pl.ds(cs, LANES)] = _chunk_cumsum(log_f, mat) + carry
            return carry + jnp.sum(log_f, axis=1, keepdims=True)

        lax.fori_loop(0, t_rows // LANES, step, jnp.zeros((heads, 1), F32))

    return pl.pallas_call(body, name="gate_fwd", out_shape=jax.ShapeDtypeStruct((heads, t_rows), F32),
                          compiler_params=pltpu.CompilerParams(vmem_limit_bytes=VMEM_LIMIT))(logit_t, bias_col)


def _gate_bwd(dcum_t, logit_t, bias_col):
    heads, t_rows = logit_t.shape
    n_chunks = t_rows // LANES

    def body(d_ref, x_ref, b_ref, dx_ref, db_ref):
        mat = _prefix_matrix(reverse=True)

        def step(i, carry):
            tail, db = carry
            cs = pl.multiple_of((n_chunks - 1 - i) * LANES, LANES)
            d = d_ref[:, pl.ds(cs, LANES)]
            d_log_f = _chunk_cumsum(d, mat) + tail
            pre = x_ref[:, pl.ds(cs, LANES)] + b_ref[...]
            e = jnp.exp(-jnp.abs(pre))
            d_pre = d_log_f * (jnp.where(pre >= 0.0, e, 1.0) / (1.0 + e))
            dx_ref[:, pl.ds(cs, LANES)] = d_pre
            return tail + jnp.sum(d, axis=1, keepdims=True), db + jnp.sum(d_pre, axis=1, keepdims=True)

        zero = jnp.zeros((heads, 1), F32)
        _, db = lax.fori_loop(0, n_chunks, step, (zero, zero))
        db_ref[...] = db

    return pl.pallas_call(body, name="gate_bwd",
                          out_shape=(jax.ShapeDtypeStruct((heads, t_rows), F32), jax.ShapeDtypeStruct((heads, 1), F32)),
                          compiler_params=pltpu.CompilerParams(vmem_limit_bytes=VMEM_LIMIT))(dcum_t, logit_t, bias_col)


def _norm_fwd(x, g, name):
    return _rows_call(lambda xv, gv: _rms_fwd(xv, gv), name, [x], [g], [(x.shape[1], BF16)])


def _norm_bwd(x, g, dh, dres, name):
    def fn(xv, dhv, dresv, gv):
        dx, dg = _rms_bwd(xv, gv, dhv)
        dx = dresv + dx
        return dx, dx, dg
    return _rows_call(fn, name, [x, dh, dres], [g], [(x.shape[1], F32), (x.shape[1], BF16)], [((1, x.shape[1]), F32)])


def _loss_fwd_bwd(y, target):
    d_model = y.shape[1]

    def fn(yv, tv):
        err = yv - tv
        dy = err * (1.0 / d_model)
        return dy, dy, jnp.sum(jnp.sum(err * err, axis=1, keepdims=True), axis=0, keepdims=True)
    return _rows_call(fn, "loss", [y, target], [], [(d_model, F32), (d_model, BF16)], [((1, 1), F32)])


def _heads_apply(fn, n_heads, width, *tiles):
    return [fn(*[t[:, h * width:(h + 1) * width] for t in tiles]) for h in range(n_heads)]


def _fox_norm_fwd(pb, gq, gk, heads):
    width = heads * HEAD_DIM

    def fn(qk, gqv, gkv):
        q = jnp.concatenate(_heads_apply(lambda t: _rms_fwd(t, gqv), heads, HEAD_DIM, qk[:, :width]), axis=1)
        k = jnp.concatenate(_heads_apply(lambda t: _rms_fwd(t, gkv), heads, HEAD_DIM, qk[:, width:]), axis=1)
        return q, k
    return _rows_call(fn, "fox_norm_fwd", [(pb, 2 * width, 0)], [gq, gk], [(width, BF16), (width, BF16)])


def _fox_norm_bwd(pb, gq, gk, dq, dk, heads):
    width = heads * HEAD_DIM

    def fn(qk, dqv, dkv, gqv, gkv):
        res_q = _heads_apply(lambda t, d: _rms_bwd(t, gqv, d), heads, HEAD_DIM, qk[:, :width], dqv)
        res_k = _heads_apply(lambda t, d: _rms_bwd(t, gkv, d), heads, HEAD_DIM, qk[:, width:], dkv)
        dqk = jnp.concatenate([r[0] for r in res_q] + [r[0] for r in res_k], axis=1)
        return dqk, sum(r[1] for r in res_q), sum(r[1] for r in res_k)
    return _rows_call(fn, "fox_norm_bwd", [(pb, 2 * width, 0), dq, dk], [gq, gk], [(2 * width, BF16)],
                      [((1, HEAD_DIM), F32), ((1, HEAD_DIM), F32)])


def _lora_norm_fwd(down, gq, gkv, rank):
    def fn(dv, gqv, gkvv):
        return _rms_fwd(dv[:, :rank], gqv), _rms_fwd(dv[:, rank:], gkvv)
    return _rows_call(fn, "lora_norm_fwd", [(down, 2 * rank, 0)], [gq, gkv], [(rank, BF16), (rank, BF16)])


def _lora_norm_bwd(down, gq, gkv, dcq, dckv, dkpe, rank):
    def fn(dv, dcqv, dckvv, dkpev, gqv, gkvv):
        dxq, dgq = _rms_bwd(dv[:, :rank], gqv, dcqv)
        dxkv, dgkv = _rms_bwd(dv[:, rank:], gkvv, dckvv)
        return jnp.concatenate([dxq, dxkv, dkpev], axis=1), dgq, dgkv
    return _rows_call(fn, "lora_norm_bwd", [(down, 2 * rank, 0), dcq, dckv, dkpe], [gq, gkv],
                      [(2 * rank + LANES, BF16)], [((1, rank), F32), ((1, rank), F32)])


def _rope_tables(pos_col, inv_freq, sin_sign):
    def fn(pos, invf, sign):
        ang = pos.astype(F32) * invf
        return jnp.cos(ang) * jnp.abs(sign), jnp.sin(ang) * sign
    return _rows_call(fn, "rope_tables", [pos_col], [inv_freq, sin_sign], [(LANES, F32), (LANES, F32)])


def _mla_prep_fwd(q_raw, kv, down, kpe_block, qg, kg, cos_t, sin_s):
    def fn(qv, kvv, kpe, cosv, sinv, qgv, kgv):
        qs, ks, vs = [], [], []
        for h in range(MLA_HEADS):
            qn = _rms_fwd(qv[:, h * MLA_PAD_DIM:(h + 1) * MLA_PAD_DIM], qgv, MLA_QK_DIM)
            qs += [qn[:, :HEAD_DIM], _rope_fwd(qn[:, HEAD_DIM:], cosv, sinv)]
            k_full = jnp.concatenate([kvv[:, h * MLA_PAD_DIM:h * MLA_PAD_DIM + HEAD_DIM], kpe], axis=1)
            kn = _rms_fwd(k_full, kgv, MLA_QK_DIM)
            ks += [kn[:, :HEAD_DIM], _rope_fwd(kn[:, HEAD_DIM:], cosv, sinv)]
            vs.append(kvv[:, h * MLA_PAD_DIM + HEAD_DIM:(h + 1) * MLA_PAD_DIM])
        return jnp.concatenate(qs, axis=1), jnp.concatenate(ks, axis=1), jnp.concatenate(vs, axis=1)
    wide = MLA_HEADS * MLA_PAD_DIM
    return _rows_call(fn, "mla_prep_fwd", [q_raw, kv, (down, LANES, kpe_block), cos_t, sin_s], [qg, kg],
                      [(wide, BF16), (wide, BF16), (MLA_HEADS * HEAD_DIM, BF16)], tile=128)


def _mla_prep_bwd(q_raw, kv, down, kpe_block, qg, kg, cos_t, sin_s, dq, dk, dv):
    def fn(qv, kvv, kpe, cosv, sinv, dqv, dkv, dvv, qgv, kgv):
        dqs, dkvs = [], []
        dkpe = jnp.zeros_like(kpe)
        dqg = jnp.zeros_like(qgv)
        dkg = jnp.zeros_like(kgv)
        for h in range(MLA_HEADS):
            lo, hi = h * MLA_PAD_DIM, (h + 1) * MLA_PAD_DIM
            dqn = jnp.concatenate([dqv[:, lo:lo + HEAD_DIM], _rope_bwd(dqv[:, lo + HEAD_DIM:hi], cosv, sinv)], axis=1)
            dqh, dg = _rms_bwd(qv[:, lo:hi], qgv, dqn, MLA_QK_DIM)
            dqs.append(dqh)
            dqg = dqg + dg
            k_full = jnp.concatenate([kvv[:, lo:lo + HEAD_DIM], kpe], axis=1)
            dkn = jnp.concatenate([dkv[:, lo:lo + HEAD_DIM], _rope_bwd(dkv[:, lo + HEAD_DIM:hi], cosv, sinv)], axis=1)
            dkh, dg = _rms_bwd(k_full, kgv, dkn, MLA_QK_DIM)
            dkg = dkg + dg
            dkpe = dkpe + dkh[:, HEAD_DIM:]
            dkvs += [dkh[:, :HEAD_DIM], dvv[:, h * HEAD_DIM:(h + 1) * HEAD_DIM]]
        return jnp.concatenate(dqs, axis=1), jnp.concatenate(dkvs, axis=1), dkpe, dqg, dkg
    wide = MLA_HEADS * MLA_PAD_DIM
    return _rows_call(fn, "mla_prep_bwd", [q_raw, kv, (down, LANES, kpe_block), cos_t, sin_s, dq, dk, dv], [qg, kg],
                      [(wide, BF16), (wide, BF16), (LANES, F32)], [((1, MLA_PAD_DIM), F32), ((1, MLA_PAD_DIM), F32)],
                      tile=128)


def _sqrelu_up(acc):
    return acc, jnp.square(jnp.maximum(acc, 0.0))


def _sqrelu_grad(acc, u):
    return (acc * (2.0 * jnp.maximum(u, 0.0)),)


def _mlp_fwd(x, g, w_up, w_down, tag):
    h = _norm_fwd(x, g, f"mlp_norm_fwd{tag}")
    u, a = _matmul(h, w_up, "nn", f"mlp_up{tag}", (F32, BF16), _sqrelu_up)
    return _matmul(a, w_down, "nn", f"mlp_down{tag}", (F32,), _add_residual, (x,)), (h, u, a)


def _mlp_bwd(x, g, w_up, w_down, saved, dy, dy16, tag):
    h, u, a = saved
    dw_down = _matmul(a, dy16, "tn", f"mlp_dwdown{tag}", (BF16,))
    du = _matmul(dy16, w_down, "nt", f"mlp_du{tag}", (BF16,), _sqrelu_grad, (u,))
    dw_up = _matmul(h, du, "tn", f"mlp_dwup{tag}", (BF16,))
    dh = _matmul(du, w_up, "nt", f"mlp_dh{tag}")
    dx, dx16, dg = _norm_bwd(x, g, dh, dy, f"mlp_norm_bwd{tag}")
    return dx, dx16, dg, dw_up, dw_down


def _local_step(x, pos_col, target, w, dist=None):
    w = dict(w)
    hs = w["w_a"].shape[1] // (4 * HEAD_DIM)
    sb_w = hs * HEAD_DIM
    grads, received = {}, {}

    def gather_in(group):
        return [("gather", blk) for blk in dist["blocks"][group]] if dist else []

    def exchange_in(group):
        return [("all_to_all", slots) for slots in dist["slots_of"](group, grads)] if dist else []

    h0 = _norm_fwd(x, w["ln_mix0"], "mix0_norm_fwd")
    pa = _matmul(h0, w["w_a"], "nn", "in_proj_a", (BF16,))
    pb = _matmul(h0, w["w_b"], "nn", "in_proj_b")
    pat = pa[:, sb_w:].T
    o_sb, o_sb_fine, *got = _attn_fwd("stick", pa, pat, pa, hs, HEAD_DIM, HEAD_DIM, HEAD_DIM ** -0.5, "stick_fwd",
                                      q_off=0, k_off=0, v_off=2 * hs, tq=ATTN_TQ_WIDE, hosted=gather_in("mlp0"))
    if dist:
        w.update(dist["weights_of"]("mlp0", got))
    logit_t = pb[:, 2 * sb_w:2 * sb_w + hs].T
    bias_col = w["b_f"][0, :hs].reshape(hs, 1)
    f_cum = _gate_fwd(logit_t, bias_col)
    f_col, f_row = f_cum[:, :, None], f_cum[:, None, :]
    qf, kf = _fox_norm_fwd(pb, w["fox_q_g"], w["fox_k_g"], hs)
    kft = kf.T
    o_fx, o_fx_fine, lse_fx, *got = _attn_fwd("softmax", qf, kft, pa, hs, HEAD_DIM, HEAD_DIM, HEAD_DIM ** -0.5,
                                              "fox_fwd", v_off=3 * hs, fcol=f_col, frow=f_row, tq=ATTN_TQ_WIDE, tk=ATTN_TK_SOFTMAX,
                                              hosted=gather_in("layer1"))
    if dist:
        w.update(dist["weights_of"]("layer1", got))
    o0 = jnp.concatenate([o_sb, o_fx], axis=1)
    x1 = _matmul(o0, w["w_o0"], "nn", "out_proj0", (F32,), _add_residual, (x,))
    x2, mlp0 = _mlp_fwd(x1, w["ln_mlp0"], w["w_up0"], w["w_dn0"], "0")

    rank = w["w_uq"].shape[0]
    h2 = _norm_fwd(x2, w["ln_mix1"], "mix1_norm_fwd")
    down = _matmul(h2, w["w_down"], "nn", "mla_down")
    cqn, ckvn = _lora_norm_fwd(down, w["q_a_g"], w["kv_a_g"], rank)
    q_raw = _matmul(cqn, w["w_uq"], "nn", "mla_uq")
    kv = _matmul(ckvn, w["w_ukv"], "nn", "mla_ukv")
    cos_t, sin_s = _rope_tables(pos_col, w["inv_freq"], w["sin_sign"])
    kpe_block = 2 * rank // LANES
    qm, km, vm = _mla_prep_fwd(q_raw, kv, down, kpe_block, w["mla_q_g"], w["mla_k_g"], cos_t, sin_s)
    kmt, vmt = km.T, vm.T
    o_m, o_m_fine, lse_m = _attn_fwd("softmax", qm, kmt, vm, MLA_HEADS, MLA_PAD_DIM, HEAD_DIM, MLA_QK_DIM ** -0.5,
                                     "mla_fwd", tq=ATTN_TQ_WIDE, tk=ATTN_TK_SOFTMAX)
    x3 = _matmul(o_m, w["w_o1"], "nn", "out_proj1", (F32,), _add_residual, (x2,))
    x4, mlp1 = _mlp_fwd(x3, w["ln_mlp1"], w["w_up1"], w["w_dn1"], "1")

    dy, dy16, sq_err = _loss_fwd_bwd(x4, target)

    dx3, dx3_16, grads["ln_mlp1"], grads["w_up1"], grads["w_dn1"] = _mlp_bwd(
        x3, w["ln_mlp1"], w["w_up1"], w["w_dn1"], mlp1, dy, dy16, "1")
    grads["w_o1"] = _matmul(o_m, dx3_16, "tn", "dw_o1", (BF16,))
    do_m = _matmul(dx3_16, w["w_o1"], "nt", "do_mla", (BF16,))
    dqm, dkm, dvm, *got = _attn_bwd("softmax", qm, km, kmt, vmt, o_m_fine, do_m, MLA_HEADS, MLA_PAD_DIM, HEAD_DIM,
                                    MLA_QK_DIM ** -0.5, "mla_bwd", lse=lse_m, tq=ATTN_TQ_WIDE, tk=ATTN_TK_SOFTMAX,
                                    hosted=exchange_in("mlp1"))
    received["mlp1"] = got
    dq_raw, dkv, dkpe, grads["mla_q_g"], grads["mla_k_g"] = _mla_prep_bwd(
        q_raw, kv, down, kpe_block, w["mla_q_g"], w["mla_k_g"], cos_t, sin_s, dqm, dkm, dvm)
    grads["w_uq"] = _matmul(cqn, dq_raw, "tn", "dw_uq", (BF16,))
    grads["w_ukv"] = _matmul(ckvn, dkv, "tn", "dw_ukv", (BF16,))
    dcqn = _matmul(dq_raw, w["w_uq"], "nt", "d_cq")
    dckvn = _matmul(dkv, w["w_ukv"], "nt", "d_ckv")
    ddown, grads["q_a_g"], grads["kv_a_g"] = _lora_norm_bwd(down, w["q_a_g"], w["kv_a_g"], dcqn, dckvn, dkpe, rank)
    grads["w_down"] = _matmul(h2, ddown, "tn", "dw_down", (BF16,))
    dh2 = _matmul(ddown, w["w_down"], "nt", "d_h2")
    dx2, dx2_16, grads["ln_mix1"] = _norm_bwd(x2, w["ln_mix1"], dh2, dx3, "mix1_norm_bwd")

    dx1, dx1_16, grads["ln_mlp0"], grads["w_up0"], grads["w_dn0"] = _mlp_bwd(
        x1, w["ln_mlp0"], w["w_up0"], w["w_dn0"], mlp0, dx2, dx2_16, "0")
    grads["w_o0"] = _matmul(o0, dx1_16, "tn", "dw_o0", (BF16,))
    do0 = _matmul(dx1_16, w["w_o0"], "nt", "do_mix0", (BF16,))
    dq_sb, dk_sb, dv_sb, *got = _attn_bwd("stick", pa, pa, pat, pat, o_sb_fine, do0, hs, HEAD_DIM, HEAD_DIM,
                                          HEAD_DIM ** -0.5, "stick_bwd", q_off=0, k_off=hs, kt_off=0, vt_off=hs, do_off=0,
                                          tq=ATTN_TQ_WIDE,
                                          hosted=exchange_in("with_stick_bwd"))
    received["with_stick_bwd"] = got
    dqf, dkf, dv_fx, ds_rows, ds_cols, *got = _attn_bwd(
        "softmax", qf, kf, kft, pat, o_fx_fine, do0, hs, HEAD_DIM, HEAD_DIM, HEAD_DIM ** -0.5, "fox_bwd", vt_off=2 * hs,
        tq=ATTN_TQ_WIDE, tk=ATTN_TK_SOFTMAX,
        do_off=hs, lse=lse_fx, fcol=f_col, frow=f_row, hosted=exchange_in("with_fox_bwd"))
    received["with_fox_bwd"] = got
    dqk_fx, grads["fox_q_g"], grads["fox_k_g"] = _fox_norm_bwd(pb, w["fox_q_g"], w["fox_k_g"], dqf, dkf, hs)
    dlogit_t, db_f = _gate_bwd(ds_rows[:, :, 0] - ds_cols[:, 0, :], logit_t, bias_col)
    grads["b_f"] = db_f.reshape(1, hs)
    dpa = jnp.concatenate([dq_sb.astype(BF16), dk_sb.astype(BF16), dv_sb.astype(BF16), dv_fx.astype(BF16)], axis=1)
    dlogit_pad = jnp.pad(dlogit_t.T.astype(BF16), ((0, 0), (0, pb.shape[1] - 2 * sb_w - hs)))
    dpb = jnp.concatenate([dqk_fx, dlogit_pad], axis=1)
    grads["w_a"] = _matmul(h0, dpa, "tn", "dw_a", (BF16,))
    grads["w_b"] = _matmul(h0, dpb, "tn", "dw_b", (BF16,))
    dh0 = _matmul(dpb, w["w_b"], "nt", "d_h0_b")
    res = _matmul(dpa, w["w_a"], "nt", "d_h0_a", (F32,), _add_residual, (dh0,), hosted=exchange_in("mix0"))
    dh0, received["mix0"] = (res[0], list(res[1:])) if dist else (res, [])
    grad_x, _, grads["ln_mix0"] = _norm_bwd(x, w["ln_mix0"], dh0, dx1, "mix0_norm_bwd")
    return sq_err, grad_x, grads, received


PIECES = {
    "sf_w_in": ("sf_w_in", 0, 1), "sf_w_o": ("sf_w_o", 0, 0), "mla_w_down": ("mla_w_down", 0, 0),
    "mla_w_uq": ("mla_w_uq", 0, 1), "mla_w_ukv": ("mla_w_ukv", 0, 1), "mla_w_o": ("mla_w_o", 0, 0),
    "mlp_w_up0": ("mlp_w_up", 0, 1), "mlp_w_up1": ("mlp_w_up", 1, 1),
    "mlp_w_down0": ("mlp_w_down", 0, 0), "mlp_w_down1": ("mlp_w_down", 1, 0),
}
GROUPS = {
    "mix0": ["sf_w_in"], "mlp0": ["sf_w_o", "mlp_w_up0", "mlp_w_down0"],
    "mla": ["mla_w_down", "mla_w_uq", "mla_w_ukv", "mla_w_o"], "mlp1": ["mlp_w_up1", "mlp_w_down1"],
}
GROUPS["layer1"] = GROUPS["mla"] + GROUPS["mlp1"]
GROUPS["with_stick_bwd"] = GROUPS["mla"] + ["mlp_w_up0"]
GROUPS["with_fox_bwd"] = ["mlp_w_down0", "sf_w_o"]
SMALL = ["ln_mix_g", "ln_mlp_g", "sf_b_f", "fox_q_g", "fox_k_g", "mla_q_a_g", "mla_kv_a_g", "mla_q_g", "mla_k_g"]
ALL_W = ["ln_mix_g", "ln_mlp_g", "sf_w_in", "sf_b_f", "fox_q_g", "fox_k_g", "sf_w_o", "mla_w_down", "mla_q_a_g",
         "mla_kv_a_g", "mla_w_uq", "mla_w_ukv", "mla_q_g", "mla_k_g", "mla_w_o", "mlp_w_up", "mlp_w_down"]


def _weights_mix0(full, small):
    w_in = full["sf_w_in"]
    d_model = w_in.shape[0]
    n_fx = small["sf_b_f"].shape[1]
    sb_w = (w_in.shape[1] - n_fx) // 6
    cols = lambda i: w_in[:, i * sb_w:(i + 1) * sb_w]
    w_a = jnp.concatenate([cols(0), cols(1), cols(2), cols(5)], axis=1)
    w_b = jnp.concatenate([cols(3), cols(4), w_in[:, 6 * sb_w:], jnp.zeros((d_model, LANES - n_fx), w_in.dtype)], axis=1)
    half = ROPE_DIM // 2
    inv_freq = ROPE_THETA ** (-jnp.arange(half, dtype=F32) / half)
    zeros64 = jnp.zeros((ROPE_DIM,), F32)
    pad256 = lambda g: jnp.pad(g, ((0, 0), (0, MLA_PAD_DIM - MLA_QK_DIM)))
    pad_lanes = lambda g: jnp.pad(g, ((0, 0), (0, LANES - g.shape[1])))
    return dict(
        ln_mix0=small["ln_mix_g"][0:1], ln_mix1=small["ln_mix_g"][1:2],
        ln_mlp0=small["ln_mlp_g"][0:1], ln_mlp1=small["ln_mlp_g"][1:2],
        w_a=w_a, w_b=w_b, b_f=pad_lanes(small["sf_b_f"]), fox_q_g=small["fox_q_g"], fox_k_g=small["fox_k_g"],
        q_a_g=small["mla_q_a_g"], kv_a_g=small["mla_kv_a_g"],
        mla_q_g=pad256(small["mla_q_g"]), mla_k_g=pad256(small["mla_k_g"]),
        inv_freq=jnp.concatenate([inv_freq, inv_freq, zeros64]).reshape(1, LANES),
        sin_sign=jnp.concatenate([-jnp.ones((half,), F32), jnp.ones((half,), F32), zeros64]).reshape(1, LANES),
    )


def _weights_mlp0(full):
    return dict(w_o0=full["sf_w_o"], w_up0=full["mlp_w_up0"], w_dn0=full["mlp_w_down0"])


def _weights_layer1(full):
    rank = full["mla_w_uq"].shape[0]
    w_uq = full["mla_w_uq"].reshape(rank, MLA_HEADS, MLA_QK_DIM)
    w_uq = jnp.pad(w_uq, ((0, 0), (0, 0), (0, MLA_PAD_DIM - MLA_QK_DIM))).reshape(rank, MLA_HEADS * MLA_PAD_DIM)
    return dict(w_down=jnp.pad(full["mla_w_down"], ((0, 0), (0, LANES - ROPE_DIM))), w_uq=w_uq,
                w_ukv=full["mla_w_ukv"], w_o1=full["mla_w_o"], w_up1=full["mlp_w_up1"], w_dn1=full["mlp_w_down1"])


WEIGHTS_OF = {"mlp0": _weights_mlp0, "layer1": _weights_layer1}


def _piece_grad(g, piece):
    if piece == "sf_w_in":
        n_fx = g["b_f"].shape[1]
        ga, gb = g["w_a"], g["w_b"]
        sb_w = ga.shape[1] // 4
        ca = lambda i: ga[:, i * sb_w:(i + 1) * sb_w]
        return jnp.concatenate([ca(0), ca(1), ca(2), gb[:, :sb_w], gb[:, sb_w:2 * sb_w], ca(3),
                                gb[:, 2 * sb_w:2 * sb_w + n_fx]], axis=1)
    if piece == "mla_w_uq":
        rank = g["w_uq"].shape[0]
        return g["w_uq"].reshape(rank, MLA_HEADS, MLA_PAD_DIM)[:, :, :MLA_QK_DIM].reshape(rank, MLA_HEADS * MLA_QK_DIM)
    if piece == "mla_w_down":
        return g["w_down"][:, :g["w_down"].shape[1] - (LANES - ROPE_DIM)]
    return g[{"sf_w_o": "w_o0", "mla_w_ukv": "w_ukv", "mla_w_o": "w_o1", "mlp_w_up0": "w_up0", "mlp_w_up1": "w_up1",
              "mlp_w_down0": "w_dn0", "mlp_w_down1": "w_dn1"}[piece]]


def _small_grads(g):
    return {
        "ln_mix_g": jnp.concatenate([g["ln_mix0"], g["ln_mix1"]], axis=0),
        "ln_mlp_g": jnp.concatenate([g["ln_mlp0"], g["ln_mlp1"]], axis=0),
        "sf_b_f": g["b_f"], "fox_q_g": g["fox_q_g"], "fox_k_g": g["fox_k_g"],
        "mla_q_a_g": g["q_a_g"], "mla_kv_a_g": g["kv_a_g"],
        "mla_q_g": g["mla_q_g"][:, :MLA_QK_DIM], "mla_k_g": g["mla_k_g"][:, :MLA_QK_DIM],
    }


PACK_TILE = 1024


def _as_rows(a, row_multiple=16):
    flat = a.reshape(-1)
    rows = -(-flat.shape[0] // LANES)
    rows = -(-rows // row_multiple) * row_multiple
    return jnp.pad(flat, (0, rows * LANES - flat.shape[0])).reshape(rows, LANES)


def _pack_rows(parts, axis, dtype, row_multiple=PACK_TILE, spare_rows=0):
    used = sum(p.shape[axis] for p in parts)
    shape = list(parts[0].shape)
    shape[axis] = -(-used // row_multiple) * row_multiple + spare_rows - used
    return jnp.concatenate([p.astype(dtype) for p in parts] + [jnp.ones(shape, dtype)], axis=axis)


def _unshard(stack, axis):
    moved = jnp.moveaxis(stack, 0, axis)
    shape = list(stack.shape[1:])
    shape[axis] *= N_DEV
    return moved.reshape(shape)


def _shard_stack(full, axis):
    shape = list(full.shape)
    shape[axis:axis + 1] = [N_DEV, shape[axis] // N_DEV]
    return jnp.moveaxis(full.reshape(shape), axis, 0)


OPT_TILE_ELEMS = 128 * 1024


def _row_tile(rows, cols):
    best = 16
    for t in range(16, rows + 1, 16):
        if rows % t == 0 and t * cols <= OPT_TILE_ELEMS:
            best = t
    assert rows % best == 0
    return best


def _cast_bf16(a, name):
    return _rows_call(lambda v: v, name, [a], [], [(a.shape[1], BF16)], tile=_row_tile(*a.shape))


def _adam_math(w, g, m, v):
    m = ADAM_B1 * m + (1.0 - ADAM_B1) * g
    v = ADAM_B2 * v + (1.0 - ADAM_B2) * jnp.square(g)
    m_hat = m / (1.0 - ADAM_B1 ** ADAM_STEP)
    v_hat = v / (1.0 - ADAM_B2 ** ADAM_STEP)
    delta = -ADAM_LR * (m_hat / (jnp.sqrt(v_hat) + ADAM_EPS) + ADAM_WD * w)
    return delta, m, v


def _adam_big(recvs, w, m, v, name, hosted=()):
    layers, rows, cols = w.shape
    tile = _row_tile(rows, cols)
    n_tiles = rows // tile
    n_host = len(hosted)

    def body(*refs):
        recv_refs = refs[:layers]
        w_ref, m_ref, v_ref = refs[layers:layers + 3]
        base = layers + 3 + n_host
        g_ref, d_ref, nm_ref, nv_ref = refs[base:base + 4]
        layer = pl.program_id(0)
        host_args = (hosted, refs[layers + 3:base], refs[base + 4:base + 4 + n_host], refs[base + 4 + n_host:],
                     layer * n_tiles + pl.program_id(1), layers * n_tiles)
        _run_hosted(*host_args, "start")

        def total(r_ref):
            acc = r_ref[0].astype(F32)
            for s in range(1, N_DEV):
                acc = acc + r_ref[s].astype(F32)
            return acc

        g = total(recv_refs[0])
        for j in range(1, layers):
            g = jnp.where(layer == j, total(recv_refs[j]), g)
        delta, nm, nv = _adam_math(w_ref[...], g, m_ref[...], v_ref[...])
        g_ref[...] = g
        d_ref[...] = delta
        nm_ref[...] = nm
        nv_ref[...] = nv
        _run_hosted(*host_args, "finish")

    def recv_spec(j):
        return pl.BlockSpec((N_DEV, tile, cols),
                            lambda l, i: (0, jnp.where(l == j, i, jnp.where(l < j, 0, n_tiles - 1)), 0))

    spec = pl.BlockSpec((None, tile, cols), lambda l, i: (l, i, 0))
    out = jax.ShapeDtypeStruct(w.shape, F32)
    return pl.pallas_call(
        body, name=name, grid=(layers, n_tiles),
        in_specs=[recv_spec(j) for j in range(layers)] + [spec] * 3 + [ANY_SPEC] * n_host,
        out_specs=[spec] * 4 + [ANY_SPEC] * n_host,
        out_shape=[out] * 4 + [_exchange_out_shape(kd, arr) for kd, arr in hosted],
        scratch_shapes=EXCHANGE_SCRATCH * n_host, compiler_params=_params("arbitrary", "arbitrary"),
    )(*recvs, w, m, v, *[arr for _, arr in hosted])


def _sum_slots(gathered):
    rows = gathered.shape[1]

    def body(r_ref, o_ref):
        acc = r_ref[0]
        for s in range(1, N_DEV):
            acc = acc + r_ref[s]
        o_ref[...] = acc

    return pl.pallas_call(body, name="sum_small", out_shape=jax.ShapeDtypeStruct((rows, LANES), F32))(gathered)


def _adam_small(w, g, m, v):
    def fn(wv, gv, mv, vv):
        return _adam_math(wv, gv, mv, vv)
    return _rows_call(fn, "adam_small", [w, g, m, v], [], [(LANES, F32)] * 3, tile=w.shape[0])


def kernel(x, positions, ln_mix_g, ln_mlp_g, sf_w_in, sf_b_f, fox_q_g, fox_k_g, sf_w_o, mla_w_down, mla_q_a_g, mla_kv_a_g, mla_w_uq, mla_w_ukv, mla_q_g, mla_k_g, mla_w_o, mlp_w_up, mlp_w_down, loss_target, m_ln_mix_g, m_ln_mlp_g, m_sf_w_in, m_sf_b_f, m_fox_q_g, m_fox_k_g, m_sf_w_o, m_mla_w_down, m_mla_q_a_g, m_mla_kv_a_g, m_mla_w_uq, m_mla_w_ukv, m_mla_q_g, m_mla_k_g, m_mla_w_o, m_mlp_w_up, m_mlp_w_down, v_ln_mix_g, v_ln_mlp_g, v_sf_w_in, v_sf_b_f, v_fox_q_g, v_fox_k_g, v_sf_w_o, v_mla_w_down, v_mla_q_a_g, v_mla_kv_a_g, v_mla_w_uq, v_mla_w_ukv, v_mla_q_g, v_mla_k_g, v_mla_w_o, v_mlp_w_up, v_mlp_w_down):
    given = dict(locals())
    wts = {n: given[n] for n in ALL_W}
    mom = {n: given["m_" + n] for n in ALL_W}
    var = {n: given["v_" + n] for n in ALL_W}
    me = 4 * lax.axis_index("x") + 2 * lax.axis_index("y") + lax.axis_index("c")
    t_rows, d_model = x.shape[1], x.shape[2]
    big = sorted({name for name, _, _ in PIECES.values()})

    def whole_pieces(gathered, group):
        return {p: _unshard(s, PIECES[p][2]) for p, s in zip(GROUPS[group], gathered)}

    def w_in_slot_parts(grads):
        slots = _shard_stack(_piece_grad(grads, "sf_w_in"), PIECES["sf_w_in"][2])
        cuts = [0] + [(slots.shape[1] * f // 16) // 16 * 16 for f in (6, 11)] + [slots.shape[1]]
        return [slots[:, a:b] for a, b in zip(cuts, cuts[1:])]

    def slots_of(group, grads):
        if group == "mix0":
            return w_in_slot_parts(grads)[:1]
        return [_shard_stack(_piece_grad(grads, p), PIECES[p][2]) for p in GROUPS[group]]

    cast = {n: _cast_bf16(wts[n].reshape(-1, wts[n].shape[2]), f"cast_{n}").reshape(wts[n].shape) for n in big}
    blocks = {grp: [cast[PIECES[p][0]][PIECES[p][1]] for p in GROUPS[grp]] for grp in ("mix0", "mlp0", "layer1")}
    mix0 = whole_pieces(_exchange("gather", blocks["mix0"], "gather_mix0"), "mix0")
    gains, = _exchange("gather", [_as_rows(jnp.concatenate([mla_q_a_g, mla_kv_a_g], axis=1))], "gather_gains")
    lora_n = mla_q_a_g.shape[1]
    gains_flat = gains.reshape(N_DEV, -1)[:, :2 * lora_n]
    small = dict(ln_mix_g=ln_mix_g, ln_mlp_g=ln_mlp_g, sf_b_f=sf_b_f, fox_q_g=fox_q_g, fox_k_g=fox_k_g,
                 mla_q_a_g=gains_flat[:, :lora_n].reshape(1, -1), mla_kv_a_g=gains_flat[:, lora_n:].reshape(1, -1),
                 mla_q_g=mla_q_g, mla_k_g=mla_k_g)
    dist = dict(blocks=blocks, slots_of=slots_of,
                weights_of=lambda grp, gathered: WEIGHTS_OF[grp](whole_pieces(gathered, grp)))
    sq_err, grad_x, g, received = _local_step(x[0], positions.reshape(t_rows, 1), loss_target[0],
                                              _weights_mix0(mix0, small), dist)

    recv_of = {p: r for grp in ("mlp1", "with_stick_bwd", "with_fox_bwd") for p, r in zip(GROUPS[grp], received[grp])}
    late_parts = dict(zip(("mlp_w_down", "mlp_w_up"), w_in_slot_parts(g)[1:]))
    w_in_recv = list(received["mix0"])
    results = {kind: {} for kind in ("grad", "delta", "new_m", "new_v")}
    for n in sorted(big, key=lambda name: (name == "sf_w_in", name not in late_parts)):
        if n == "sf_w_in":
            recv_of["sf_w_in"] = jnp.concatenate(w_in_recv, axis=1)
        layers = [p for _, p in sorted((layer, p) for p, (name, layer, _) in PIECES.items() if name == n)]
        hosted = [("all_to_all", late_parts[n])] if n in late_parts else []
        outs = _adam_big([recv_of[p] for p in layers], wts[n], mom[n], var[n], f"adam_{n}", hosted)
        w_in_recv += outs[4:]
        for kind, out in zip(("grad", "delta", "new_m", "new_v"), outs[:4]):
            results[kind][n] = out

    small_g = _small_grads(g)
    small_parts = [_as_rows(small_g[n], 8) for n in SMALL] + [_as_rows(sq_err, 8)]
    small_sum = _sum_slots(_exchange("gather", [_pack_rows(small_parts, 0, F32, 8, 8)], "gather_small_grads")[0])
    red, off = {}, 0
    for n, p in zip(SMALL + ["loss"], small_parts):
        red[n] = small_sum[off:off + p.shape[0]].reshape(-1)
        off += p.shape[0]
    loss = 0.5 * red["loss"][0] / d_model
    for n in SMALL:
        if n in ("mla_q_a_g", "mla_kv_a_g"):
            results["grad"][n] = lax.dynamic_slice(red[n], (me * lora_n,), (lora_n,)).reshape(wts[n].shape)
        else:
            results["grad"][n] = red[n][:wts[n].size].reshape(wts[n].shape)
    pack_small = lambda d: jnp.concatenate([_as_rows(d[n], 8) for n in SMALL], axis=0)
    small_out = _adam_small(pack_small(wts), pack_small(results["grad"]), pack_small(mom), pack_small(var))
    off = 0
    for n in SMALL:
        r = _as_rows(wts[n], 8).shape[0]
        for kind, packed in zip(["delta", "new_m", "new_v"], small_out):
            results[kind][n] = packed[off:off + r].reshape(-1)[:wts[n].size].reshape(wts[n].shape)
        off += r

    outs = [loss, grad_x[None]]
    for kind in ["grad", "delta", "new_m", "new_v"]:
        outs += [results[kind][n] for n in ALL_W]
    return tuple(outs)
```

```python
import functools
import math

import jax
import jax.numpy as jnp
import numpy as np
from jax import lax
from jax.experimental import pallas as pl
from jax.experimental.pallas import tpu as pltpu

F32 = jnp.float32
BF16 = jnp.bfloat16

NORM_EPS = 1e-6
ROPE_THETA = 10000.0
HEAD_DIM = 128
ROPE_DIM = 64
MLA_HEADS = 16
MLA_QK_DIM = 192
MLA_PAD_DIM = 256
ADAM_LR, ADAM_B1, ADAM_B2, ADAM_EPS, ADAM_WD, ADAM_STEP = 0.001, 0.9, 0.999, 1e-08, 0.01, 10

N_DEV = 8
LANES = 128
VMEM_LIMIT = 56 * 1024 * 1024
MATMUL_VMEM_BUDGET = 40 * 1024 * 1024
MASKED = -1e30
LOG2E = 1.4426950408889634
ATTN_TQ, ATTN_TK = 256, 256
ATTN_TQ_WIDE = 512
ATTN_TK_SOFTMAX = 512
MESH = pl.DeviceIdType.MESH

NT_DIMS = (((1,), (1,)), ((), ()))
TN_DIMS = (((0,), (0,)), ((), ()))
NN_DIMS = (((1,), (0,)), ((), ()))


def _params(*sem):
    return pltpu.CompilerParams(dimension_semantics=sem, vmem_limit_bytes=VMEM_LIMIT)


def _pick(n, pref):
    best = None
    for t in range(LANES, min(n, pref) + 1, LANES):
        if n % t == 0:
            best = t
    return n if best is None or 2 * best < min(n, pref) else best


def _rows_call(fn, name, row_ins, full_ins, row_outs, acc_outs=(), tile=256):
    row_ins = [r if isinstance(r, tuple) else (r, r.shape[1], 0) for r in row_ins]
    t_rows = row_ins[0][0].shape[0]
    assert t_rows % tile == 0
    n_in = len(row_ins) + len(full_ins)
    n_row_out = len(row_outs)

    def body(*refs):
        res = fn(*[r[...] for r in refs[:n_in]])
        res = res if isinstance(res, tuple) else (res,)
        for ref, val in zip(refs[n_in:n_in + n_row_out], res[:n_row_out]):
            ref[...] = val.astype(ref.dtype)
        acc_refs = refs[n_in + n_row_out:]
        if acc_refs:
            @pl.when(pl.program_id(0) == 0)
            def _():
                for ref in acc_refs:
                    ref[...] = jnp.zeros_like(ref)
            for ref, val in zip(acc_refs, res[n_row_out:]):
                ref[...] += val.astype(ref.dtype)

    in_specs = [pl.BlockSpec((tile, w), functools.partial(lambda i, cb: (i, cb), cb=cb)) for _, w, cb in row_ins]
    in_specs += [pl.BlockSpec(a.shape, lambda i: (0, 0)) for a in full_ins]
    out_specs = [pl.BlockSpec((tile, c), lambda i: (i, 0)) for c, _ in row_outs]
    out_specs += [pl.BlockSpec(s, lambda i: (0, 0)) for s, _ in acc_outs]
    out_shape = [jax.ShapeDtypeStruct((t_rows, c), d) for c, d in row_outs]
    out_shape += [jax.ShapeDtypeStruct(s, d) for s, d in acc_outs]
    outs = pl.pallas_call(
        body, name=name, grid=(t_rows // tile,), in_specs=in_specs, out_specs=out_specs, out_shape=out_shape,
        compiler_params=_params("arbitrary"),
    )(*[r[0] for r in row_ins], *full_ins)
    return outs[0] if len(outs) == 1 else tuple(outs)


def _matmul_tiles(m, n, k, in_bytes, out_bytes):
    tn = n if n <= 1280 else _pick(n, 1024)
    tks = [k] + [k // d for d in (2, 4, 8, 16) if k % (d * LANES) == 0]
    for tk in [t for t in tks if t <= 4096] or [tks[-1]]:
        for tm in (1024, 512, 256):
            if m % tm:
                continue
            acc = 2 * tm * tn * 4 if tk < k else tm * tn * 4
            if 2 * (tm * tk + tk * tn) * in_bytes + 2 * tm * tn * out_bytes + acc <= MATMUL_VMEM_BUDGET:
                return tm, tn, tk
    raise ValueError(f"no matmul tiling for {m}x{n}x{k}")


def _matmul(a, b, form, name, out_dtypes=(F32,), epilogue=None, extras=(), hosted=()):
    if form == "nn":
        (m, k), n = a.shape, b.shape[1]
    elif form == "nt":
        (m, k), n = a.shape, b.shape[0]
    else:
        (k, m), n = a.shape, b.shape[1]
    in_bytes = max(a.dtype.itemsize, b.dtype.itemsize)
    out_bytes = sum(jnp.dtype(d).itemsize for d in out_dtypes) + sum(e.dtype.itemsize for e in extras)
    tm, tn, tk = _matmul_tiles(m, n, k, in_bytes, out_bytes)
    nk = k // tk
    dims = {"nn": NN_DIMS, "nt": NT_DIMS, "tn": TN_DIMS}[form]
    n_extra, n_out, n_host = len(extras), len(out_dtypes), len(hosted)
    grid = (m // tm, n // tn, nk)

    def body(*refs):
        a_ref, b_ref = refs[0], refs[1]
        extra_refs = refs[2:2 + n_extra]
        base = 2 + n_extra + n_host
        out_refs = refs[base:base + n_out]
        sems_at = base + n_out + n_host
        step = (pl.program_id(0) * grid[1] + pl.program_id(1)) * nk + pl.program_id(2)
        host_args = (hosted, refs[2 + n_extra:base], refs[base + n_out:sems_at], refs[sems_at:sems_at + 3 * n_host],
                     step, grid[0] * grid[1] * nk)
        _run_hosted(*host_args, "start")

        def finish(acc):
            vals = (acc,) if epilogue is None else epilogue(acc, *[r[...] for r in extra_refs])
            for ref, val in zip(out_refs, vals):
                ref[...] = val.astype(ref.dtype)

        part = lax.dot_general(a_ref[...].astype(BF16), b_ref[...].astype(BF16), dims, preferred_element_type=F32)
        if nk == 1:
            finish(part)
        else:
            acc_ref = refs[-1]
            kk = pl.program_id(2)

            @pl.when(kk == 0)
            def _():
                acc_ref[...] = part

            @pl.when(kk > 0)
            def _():
                acc_ref[...] += part

            @pl.when(kk == nk - 1)
            def _():
                finish(acc_ref[...])
        _run_hosted(*host_args, "finish")

    a_spec = pl.BlockSpec((tk, tm), lambda i, j, kk: (kk, i)) if form == "tn" else pl.BlockSpec((tm, tk), lambda i, j, kk: (i, kk))
    b_spec = pl.BlockSpec((tn, tk), lambda i, j, kk: (j, kk)) if form == "nt" else pl.BlockSpec((tk, tn), lambda i, j, kk: (kk, j))
    o_spec = pl.BlockSpec((tm, tn), lambda i, j, kk: (i, j))
    outs = pl.pallas_call(
        body, name=name, grid=grid, in_specs=[a_spec, b_spec] + [o_spec] * n_extra + [ANY_SPEC] * n_host,
        out_specs=[o_spec] * n_out + [ANY_SPEC] * n_host,
        out_shape=[jax.ShapeDtypeStruct((m, n), d) for d in out_dtypes]
        + [_exchange_out_shape(kd, arr) for kd, arr in hosted],
        scratch_shapes=EXCHANGE_SCRATCH * n_host + ([pltpu.VMEM((tm, tn), F32)] if nk > 1 else []),
        compiler_params=_params(*(("arbitrary",) * 3 if n_host else ("parallel", "parallel", "arbitrary"))),
    )(a, b, *extras, *[arr for _, arr in hosted])
    return outs[0] if n_out + n_host == 1 else tuple(outs)


def _add_residual(acc, res):
    return (acc + res,)


def _log_sigmoid_parts(z):
    return jnp.log1p(jnp.exp(-jnp.abs(z)))


def _rms_fwd(x, g, n=None):
    n = x.shape[-1] if n is None else n
    r = lax.rsqrt(jnp.sum(x * x, axis=-1, keepdims=True) / n + NORM_EPS)
    return x * r * g


def _rms_bwd(x, g, dout, n=None):
    n = x.shape[-1] if n is None else n
    r = lax.rsqrt(jnp.sum(x * x, axis=-1, keepdims=True) / n + NORM_EPS)
    y = x * r
    dg = jnp.sum(dout * y, axis=0, keepdims=True)
    dy = dout * g
    dx = r * (dy - y * (jnp.sum(dy * y, axis=-1, keepdims=True) / n))
    return dx, dg


def _swap_halves(r):
    lane = lax.broadcasted_iota(jnp.int32, r.shape, 1)
    return jnp.where(lane < ROPE_DIM // 2, pltpu.roll(r, LANES - ROPE_DIM // 2, 1), pltpu.roll(r, ROPE_DIM // 2, 1))


def _rope_fwd(r, cos_t, sin_s):
    return r * cos_t + _swap_halves(r) * sin_s


def _rope_bwd(dr, cos_t, sin_s):
    return dr * cos_t + _swap_halves(dr * sin_s)


def _split3(x):
    hi = x.astype(BF16)
    r1 = x - hi.astype(F32)
    mid = r1.astype(BF16)
    lo = (r1 - mid.astype(F32)).astype(BF16)
    return hi, mid, lo


def _mesh_position():
    x, y, c = lax.axis_index("x"), lax.axis_index("y"), lax.axis_index("c")
    return x, y, c, 4 * x + 2 * y + c


def _peer(x, y, c, k):
    bx, by, bc = (k >> 2) & 1, (k >> 1) & 1, k & 1
    px, py, pc = x ^ bx, y ^ by, c ^ bc
    return (px, py, pc), 4 * px + 2 * py + pc


def _gather_steps(x_ref, out_ref, send_sems, recv_sems, local_sem):
    x, y, c, me = _mesh_position()
    sibling = (x, y, 1 - c)
    chips = [(1 - x, y), (x, 1 - y), (1 - x, 1 - y)]

    def slot(px, py, pc):
        return out_ref.at[4 * px + 2 * py + pc]

    def copy(k, blk, to, src=None):
        return pltpu.make_async_remote_copy(
            src_ref=slot(*blk) if src is None else src, dst_ref=slot(*blk), send_sem=send_sems.at[k],
            recv_sem=recv_sems.at[k], device_id=to, device_id_type=MESH)

    mine = pltpu.make_async_copy(x_ref, out_ref.at[me], local_sem)
    first = [copy(0, (x, y, c), sibling, src=x_ref)]
    first += [copy(1 + j, (x, y, c), (*chip, c), src=x_ref) for j, chip in enumerate(chips)]
    passed = [copy(4 + j, (*chip, c), sibling) for j, chip in enumerate(chips)]

    def start():
        mine.start()
        for cp in first:
            cp.start()

    def forward():
        for j, chip in enumerate(chips):
            copy(1 + j, (*chip, c), (x, y, c)).wait_recv()
            passed[j].start()

    def finish():
        copy(0, (x, y, 1 - c), (x, y, c)).wait_recv()
        for j, chip in enumerate(chips):
            copy(4 + j, (*chip, 1 - c), (x, y, c)).wait_recv()
        for cp in first + passed:
            cp.wait_send()
        mine.wait()

    return start, forward, finish


def _all_to_all_steps(g_ref, out_ref, send_sems, recv_sems, local_sem):
    x, y, c, me = _mesh_position()
    mine = pltpu.make_async_copy(g_ref.at[me], out_ref.at[me], local_sem)
    copies = []
    for k in range(1, N_DEV):
        peer, peer_idx = _peer(x, y, c, k)
        copies.append(pltpu.make_async_remote_copy(
            src_ref=g_ref.at[peer_idx], dst_ref=out_ref.at[me], send_sem=send_sems.at[k - 1],
            recv_sem=recv_sems.at[k - 1], device_id=peer, device_id_type=MESH))

    def start():
        mine.start()
        for cp in copies:
            cp.start()

    def finish():
        for k in range(1, N_DEV):
            peer, peer_idx = _peer(x, y, c, k)
            pltpu.make_async_remote_copy(
                src_ref=g_ref.at[me], dst_ref=out_ref.at[peer_idx], send_sem=send_sems.at[k - 1],
                recv_sem=recv_sems.at[k - 1], device_id=peer, device_id_type=MESH).wait_recv()
        for cp in copies:
            cp.wait_send()
        mine.wait()

    return start, None, finish


EXCHANGE_STEPS = {"gather": _gather_steps, "all_to_all": _all_to_all_steps}
EXCHANGE_SCRATCH = [pltpu.SemaphoreType.DMA((7,)), pltpu.SemaphoreType.DMA((7,)), pltpu.SemaphoreType.DMA]
ANY_SPEC = pl.BlockSpec(memory_space=pl.ANY)


def _exchange_out_shape(kind, arr):
    return jax.ShapeDtypeStruct(((N_DEV,) + arr.shape) if kind == "gather" else arr.shape, arr.dtype)


def _exchange(kind, arrs, name):
    n = len(arrs)

    def body(*refs):
        steps = [EXCHANGE_STEPS[kind](refs[i], refs[n + i], *refs[2 * n + 3 * i:2 * n + 3 * i + 3]) for i in range(n)]
        for start, _, _ in steps:
            start()
        for _, forward, _ in steps:
            if forward is not None:
                forward()
        for _, _, finish in steps:
            finish()

    return pl.pallas_call(body, name=name, out_shape=[_exchange_out_shape(kind, a) for a in arrs],
                          in_specs=[ANY_SPEC] * n, out_specs=[ANY_SPEC] * n, scratch_shapes=EXCHANGE_SCRATCH * n)(*arrs)


def _run_hosted(hosted, src_refs, dst_refs, sem_refs, step, n_steps, when):
    for idx, (kind, _) in enumerate(hosted):
        start, forward, finish = EXCHANGE_STEPS[kind](src_refs[idx], dst_refs[idx], *sem_refs[3 * idx:3 * idx + 3])
        if when == "start":
            pl.when(step == 0)(start)
            if forward is not None:
                pl.when(step == (3 * n_steps) // 4)(forward)
        else:
            pl.when(step == n_steps - 1)(finish)


def _causal_iotas(qi, tq, tk):
    row = qi * tq + lax.broadcasted_iota(jnp.int32, (tq, tk), 0)
    col = lax.broadcasted_iota(jnp.int32, (tq, tk), 1)
    return row, col


def _suffix_matrix(tk, inclusive):
    j = lax.broadcasted_iota(jnp.int32, (2 * tk, tk), 0) % tk
    s = lax.broadcasted_iota(jnp.int32, (2 * tk, tk), 1)
    return jnp.where((j >= s) if inclusive else (j > s), 1.0, 0.0).astype(BF16)


def _suffix_sum(x, mat):
    hi = x.astype(BF16)
    lo = (x - hi.astype(F32)).astype(BF16)
    return lax.dot_general(jnp.concatenate([hi, lo], axis=1), mat, NN_DIMS, preferred_element_type=F32)


def _attn_specs(t_rows, tq, heads, dk, dv, q_off, k_off, v_off):
    q_spec = pl.BlockSpec((tq, dk), lambda h, i: (i, q_off + h))
    kt_spec = pl.BlockSpec((dk, t_rows), lambda h, i: (k_off + h, 0))
    v_spec = pl.BlockSpec((t_rows, dv), lambda h, i: (0, v_off + h))
    return q_spec, kt_spec, v_spec


def _split_weights(weights):
    hi = weights.astype(BF16)
    return hi, (weights - hi.astype(F32)).astype(BF16)


def _weighted_values(split, v):
    return (lax.dot_general(split[0], v, NN_DIMS, preferred_element_type=F32)
            + lax.dot_general(split[1], v, NN_DIMS, preferred_element_type=F32))


def _attn_fwd(kind, q_arr, kt_arr, v_arr, heads, dk, dv, scale, name, q_off=0, k_off=0, v_off=0, fcol=None, frow=None,
              tq=ATTN_TQ, tk=ATTN_TK, hosted=()):
    t_rows = q_arr.shape[0]
    tq, tk = min(tq, t_rows), min(tk, t_rows)
    nq = t_rows // tq
    stick = kind == "stick"
    decay = fcol is not None
    n_in = 5 if decay else 3
    n_out = 2 if stick else 3
    n_host = len(hosted)

    def body(*refs):
        q_ref, kt_ref, v_ref = refs[:3]
        fcol_ref, frow_ref = (refs[3], refs[4]) if decay else (None, None)
        base = n_in + n_host
        o_ref, fine_ref = refs[base], refs[base + 1]
        lse_ref = None if stick else refs[base + 2]
        host_args = (hosted, refs[n_in:base], refs[base + n_out:base + n_out + n_host], refs[base + n_out + n_host:],
                     pl.program_id(0) * nq + pl.program_id(1), heads * nq)
        _run_hosted(*host_args, "start")
        qi = pl.program_id(1)
        q = q_ref[...]
        row, col = _causal_iotas(qi, tq, tk)
        n_kb = ((qi + 1) * tq + tk - 1) // tk
        n_diag = max(1, tq // tk)
        zeros_o = jnp.zeros((tq, dv), F32)

        no_weights = (jnp.zeros((tq, tk), BF16), jnp.zeros((tq, tk), BF16))

        def raw_logits(kb):
            return lax.dot_general(q, kt_ref[:, pl.ds(pl.multiple_of(kb * tk, tk), tk)], NN_DIMS,
                                   preferred_element_type=F32)

        def values(kb):
            return v_ref[pl.ds(pl.multiple_of(kb * tk, tk), tk), :]

        if stick:
            mat = _suffix_matrix(tk, inclusive=False)

            def make_step(masked):
                def step(i, carry):
                    c, acc, raw, prev = carry
                    raw_next = raw_logits(jnp.maximum(n_kb - 2 - i, 0))
                    d_acc = _weighted_values(prev, values(jnp.minimum(n_kb - i, n_kb - 1)))
                    z = raw * scale
                    lg = _log_sigmoid_parts(z)
                    lom = jnp.minimum(-z, 0.0) - lg
                    if masked:
                        strict = (col + (n_kb - 1 - i) * tk) < row
                        lom = jnp.where(strict, lom, 0.0)
                    w = jnp.exp((jnp.minimum(z, 0.0) - lg) + (_suffix_sum(lom, mat) + c))
                    if masked:
                        w = jnp.where(strict, w, 0.0)
                    return c + jnp.sum(lom, axis=1, keepdims=True), acc + d_acc, raw_next, _split_weights(w)
                return step

            carry = (jnp.zeros((tq, 1), F32), zeros_o, raw_logits(n_kb - 1), no_weights)
            for i in range(n_diag):
                carry = make_step(True)(i, carry)
            _, acc, _, last = lax.fori_loop(n_diag, n_kb, make_step(False), carry)
            acc = acc + _weighted_values(last, values(0))
            o_ref[...] = acc.astype(o_ref.dtype)
            fine_ref[...] = acc
        else:
            fc = fcol_ref[...] * LOG2E if decay else None

            def make_step(masked):
                def step(kb, carry):
                    m, l, acc, raw, prev = carry
                    raw_next = raw_logits(jnp.minimum(kb + 1, n_kb - 1))
                    d_acc = _weighted_values(prev, values(jnp.maximum(kb - 1, 0)))
                    ks = pl.multiple_of(kb * tk, tk)
                    s = raw * (scale * LOG2E)
                    if decay:
                        s = (s + fc) - frow_ref[:, pl.ds(ks, tk)] * LOG2E
                    if masked:
                        s = jnp.where((col + ks) <= row, s, MASKED)
                    m_new = jnp.maximum(m, jnp.max(s, axis=1, keepdims=True))
                    alpha = jnp.exp2(m - m_new)
                    p = jnp.exp2(s - m_new)
                    l = alpha * l + jnp.sum(p, axis=1, keepdims=True)
                    return m_new, l, alpha * (acc + d_acc), raw_next, _split_weights(p)
                return step

            carry = lax.fori_loop(0, n_kb - n_diag, make_step(False),
                                  (jnp.full((tq, 1), MASKED, F32), jnp.zeros((tq, 1), F32), zeros_o, raw_logits(0),
                                   no_weights))
            for d in range(n_diag):
                carry = make_step(True)(n_kb - n_diag + d, carry)
            m, l, acc, _, last = carry
            out = (acc + _weighted_values(last, values(n_kb - 1))) * (1.0 / l)
            o_ref[...] = out.astype(o_ref.dtype)
            fine_ref[...] = out
            lse_ref[...] = (m + jnp.log2(l)) * (1.0 / LOG2E)
        _run_hosted(*host_args, "finish")

    q_spec, k_spec, v_spec = _attn_specs(t_rows, tq, heads, dk, dv, q_off, k_off, v_off)
    stat_spec = pl.BlockSpec((None, tq, 1), lambda h, i: (h, i, 0))
    ins, in_specs = [q_arr, kt_arr, v_arr], [q_spec, k_spec, v_spec]
    if decay:
        ins += [fcol, frow]
        in_specs += [stat_spec, pl.BlockSpec((None, 1, t_rows), lambda h, i: (h, 0, 0))]
    o_spec = pl.BlockSpec((tq, dv), lambda h, i: (i, h))
    out_specs = [o_spec, o_spec]
    out_shape = [jax.ShapeDtypeStruct((t_rows, heads * dv), BF16), jax.ShapeDtypeStruct((t_rows, heads * dv), F32)]
    if not stick:
        out_specs.append(stat_spec)
        out_shape.append(jax.ShapeDtypeStruct((heads, t_rows, 1), F32))
    return tuple(pl.pallas_call(
        body, name=name, grid=(heads, nq), in_specs=in_specs + [ANY_SPEC] * n_host,
        out_specs=out_specs + [ANY_SPEC] * n_host,
        out_shape=out_shape + [_exchange_out_shape(kd, arr) for kd, arr in hosted],
        scratch_shapes=EXCHANGE_SCRATCH * n_host,
        compiler_params=_params("arbitrary" if n_host else "parallel", "arbitrary"),
    )(*ins, *[arr for _, arr in hosted]))


def _attn_bwd(kind, q_arr, k_arr, kt_arr, vt_arr, o_arr, do_arr, heads, dk, dv, scale, name, q_off=0, k_off=0, kt_off=0,
              vt_off=0, do_off=0, lse=None, fcol=None, frow=None, tq=ATTN_TQ, tk=ATTN_TK, hosted=()):
    t_rows = q_arr.shape[0]
    tq, tk = min(tq, t_rows), min(tk, t_rows)
    nq = t_rows // tq
    stick = kind == "stick"
    decay = fcol is not None
    n_in = 6 + (0 if stick else 1) + (2 if decay else 0)
    n_out = 5 if decay else 3
    n_host = len(hosted)

    def body(*refs):
        q_ref, k_ref, kt_ref, vt_ref, o_ref, do_ref = refs[:6]
        lse_ref = None if stick else refs[6]
        fcol_ref, frow_ref = (refs[7], refs[8]) if decay else (None, None)
        base = n_in + n_host
        dq_ref, dk_ref, dv_ref = refs[base:base + 3]
        dfcol_ref, dfrow_ref = (refs[base + 3], refs[base + 4]) if decay else (None, None)
        host_args = (hosted, refs[n_in:base], refs[base + n_out:base + n_out + n_host], refs[base + n_out + n_host:],
                     pl.program_id(0) * nq + pl.program_id(1), heads * nq)
        _run_hosted(*host_args, "start")
        qi = pl.program_id(1)

        @pl.when(qi == 0)
        def _():
            dk_ref[...] = jnp.zeros_like(dk_ref)
            dv_ref[...] = jnp.zeros_like(dv_ref)
            if decay:
                dfrow_ref[...] = jnp.zeros_like(dfrow_ref)

        q = q_ref[...]
        do = do_ref[...]
        delta = jnp.sum(do.astype(F32) * o_ref[...], axis=1, keepdims=True)
        row, col = _causal_iotas(qi, tq, tk)
        n_kb = ((qi + 1) * tq + tk - 1) // tk
        n_diag = max(1, tq // tk)

        no_pair = (jnp.zeros((tq, tk), BF16), jnp.zeros((tq, tk), BF16))

        def accumulate(kb, pair):
            at = pl.ds(pl.multiple_of(kb * tk, tk), tk)
            dk_ref[at, :] += lax.dot_general(pair[0], q, TN_DIMS, preferred_element_type=F32)
            dv_ref[at, :] += lax.dot_general(pair[1], do, TN_DIMS, preferred_element_type=F32)
            return lax.dot_general(pair[0], k_ref[at, :], NN_DIMS, preferred_element_type=F32)

        def raw_logits(kb):
            return lax.dot_general(q, kt_ref[:, pl.ds(pl.multiple_of(kb * tk, tk), tk)], NN_DIMS,
                                   preferred_element_type=F32)

        def d_weights(kb):
            return lax.dot_general(do, vt_ref[:, pl.ds(pl.multiple_of(kb * tk, tk), tk)], NN_DIMS,
                                   preferred_element_type=F32)

        if stick:
            mat_ex = _suffix_matrix(tk, inclusive=False)
            mat_in = _suffix_matrix(tk, inclusive=True)

            def make_step(masked):
                def step(i, carry):
                    c, gs, dq, raw, prev = carry
                    raw_next = raw_logits(jnp.maximum(n_kb - 2 - i, 0))
                    dw = d_weights(n_kb - 1 - i)
                    dq = dq + accumulate(jnp.minimum(n_kb - i, n_kb - 1), prev)
                    z = raw * scale
                    lg = _log_sigmoid_parts(z)
                    log_beta = jnp.minimum(z, 0.0) - lg
                    log_omb = jnp.minimum(-z, 0.0) - lg
                    lom = log_omb
                    if masked:
                        strict = (col + (n_kb - 1 - i) * tk) < row
                        lom = jnp.where(strict, log_omb, 0.0)
                    w = jnp.exp(log_beta + (_suffix_sum(lom, mat_ex) + c))
                    if masked:
                        w = jnp.where(strict, w, 0.0)
                    g = w * dw
                    g_before = delta - (gs + _suffix_sum(g, mat_in))
                    dz = g * jnp.exp(log_omb) - g_before * jnp.exp(log_beta)
                    if masked:
                        dz = jnp.where(strict, dz, 0.0)
                    return (c + jnp.sum(lom, axis=1, keepdims=True), gs + jnp.sum(g, axis=1, keepdims=True), dq,
                            raw_next, ((dz * scale).astype(BF16), w.astype(BF16)))
                return step

            zero = jnp.zeros((tq, 1), F32)
            carry = (zero, zero, jnp.zeros((tq, dk), F32), raw_logits(n_kb - 1), no_pair)
            for i in range(n_diag):
                carry = make_step(True)(i, carry)
            _, _, dq, _, last = lax.fori_loop(n_diag, n_kb, make_step(False), carry)
            dq = dq + accumulate(0, last)
        else:
            lse_v = lse_ref[...] * LOG2E
            fc = fcol_ref[...] * LOG2E if decay else None

            def make_step(masked):
                def step(kb, carry):
                    dq, row_sum, raw, prev = carry
                    raw_next = raw_logits(jnp.minimum(kb + 1, n_kb - 1))
                    dp = d_weights(kb)
                    dq = dq + accumulate(jnp.maximum(kb - 1, 0), prev)
                    ks = pl.multiple_of(kb * tk, tk)
                    s = raw * (scale * LOG2E)
                    if decay:
                        s = (s + fc) - frow_ref[:, pl.ds(ks, tk)] * LOG2E
                    p = jnp.exp2(s - lse_v)
                    if masked:
                        p = jnp.where((col + ks) <= row, p, 0.0)
                    ds = p * (dp - delta)
                    if decay:
                        dfrow_ref[:, pl.ds(ks, tk)] += jnp.sum(ds, axis=0, keepdims=True)
                        row_sum = row_sum + jnp.sum(ds, axis=1, keepdims=True)
                    return dq, row_sum, raw_next, ((ds * scale).astype(BF16), p.astype(BF16))
                return step

            carry = lax.fori_loop(0, n_kb - n_diag, make_step(False),
                                  (jnp.zeros((tq, dk), F32), jnp.zeros((tq, 1), F32), raw_logits(0), no_pair))
            for d in range(n_diag):
                carry = make_step(True)(n_kb - n_diag + d, carry)
            dq, row_sum, _, last = carry
            dq = dq + accumulate(n_kb - 1, last)
            if decay:
                dfcol_ref[...] = row_sum
        dq_ref[...] = dq
        _run_hosted(*host_args, "finish")

    q_spec, kt_spec, _ = _attn_specs(t_rows, tq, heads, dk, dv, q_off, kt_off, 0)
    stat_spec = pl.BlockSpec((None, tq, 1), lambda h, i: (h, i, 0))
    frow_spec = pl.BlockSpec((None, 1, t_rows), lambda h, i: (h, 0, 0))
    ins = [q_arr, k_arr, kt_arr, vt_arr, o_arr, do_arr]
    in_specs = [q_spec, pl.BlockSpec((t_rows, dk), lambda h, i: (0, k_off + h)), kt_spec,
                pl.BlockSpec((dv, t_rows), lambda h, i: (vt_off + h, 0)), pl.BlockSpec((tq, dv), lambda h, i: (i, h)),
                pl.BlockSpec((tq, dv), lambda h, i: (i, do_off + h))]
    if not stick:
        ins.append(lse)
        in_specs.append(stat_spec)
    if decay:
        ins += [fcol, frow]
        in_specs += [stat_spec, frow_spec]
    out_specs = [pl.BlockSpec((tq, dk), lambda h, i: (i, h)), pl.BlockSpec((t_rows, dk), lambda h, i: (0, h)),
                 pl.BlockSpec((t_rows, dv), lambda h, i: (0, h))]
    out_shape = [jax.ShapeDtypeStruct((t_rows, heads * dk), F32), jax.ShapeDtypeStruct((t_rows, heads * dk), F32),
                 jax.ShapeDtypeStruct((t_rows, heads * dv), F32)]
    if decay:
        out_specs += [stat_spec, frow_spec]
        out_shape += [jax.ShapeDtypeStruct((heads, t_rows, 1), F32), jax.ShapeDtypeStruct((heads, 1, t_rows), F32)]
    return pl.pallas_call(
        body, name=name, grid=(heads, nq), in_specs=in_specs + [ANY_SPEC] * n_host,
        out_specs=out_specs + [ANY_SPEC] * n_host,
        out_shape=out_shape + [_exchange_out_shape(kd, arr) for kd, arr in hosted],
        scratch_shapes=EXCHANGE_SCRATCH * n_host,
        compiler_params=_params("arbitrary" if n_host else "parallel", "arbitrary"),
    )(*ins, *[arr for _, arr in hosted])


def _prefix_matrix(reverse):
    j = lax.broadcasted_iota(jnp.int32, (LANES, LANES), 0)
    s = lax.broadcasted_iota(jnp.int32, (LANES, LANES), 1)
    return jnp.where((j >= s) if reverse else (j <= s), 1.0, 0.0).astype(BF16)


def _chunk_cumsum(x, mat):
    return sum(lax.dot_general(part, mat, NN_DIMS, preferred_element_type=F32) for part in _split3(x))


def _gate_fwd(logit_t, bias_col):
    heads, t_rows = logit_t.shape

    def body(x_ref, b_ref, out_ref):
        mat = _prefix_matrix(reverse=False)

        def step(ci, carry):
            cs = pl.multiple_of(ci * LANES, LANES)
            pre = x_ref[:, pl.ds(cs, LANES)] + b_ref[...]
            log_f = jnp.minimum(pre, 0.0) - _log_sigmoid_parts(pre)
            out_ref[:, pl.ds(cs, LANES)] = _chunk_cumsum(log_f, mat) + carry
            return carry + jnp.sum(log_f, axis=1, keepdims=True)

        lax.fori_loop(0, t_rows // LANES, step, jnp.zeros((heads, 1), F32))

    return pl.pallas_call(body, name="gate_fwd", out_shape=jax.ShapeDtypeStruct((heads, t_rows), F32),
                          compiler_params=pltpu.CompilerParams(vmem_limit_bytes=VMEM_LIMIT))(logit_t, bias_col)


def _gate_bwd(dcum_t, logit_t, bias_col):
    heads, t_rows = logit_t.shape
    n_chunks = t_rows // LANES

    def body(d_ref, x_ref, b_ref, dx_ref, db_ref):
        mat = _prefix_matrix(reverse=True)

        def step(i, carry):
            tail, db = carry
            cs = pl.multiple_of((n_chunks - 1 - i) * LANES, LANES)
            d = d_ref[:, pl.ds(cs, LANES)]
            d_log_f = _chunk_cumsum(d, mat) + tail
            pre = x_ref[:, pl.ds(cs, LANES)] + b_ref[...]
            e = jnp.exp(-jnp.abs(pre))
            d_pre = d_log_f * (jnp.where(pre >= 0.0, e, 1.0) / (1.0 + e))
            dx_ref[:, pl.ds(cs, LANES)] = d_pre
            return tail + jnp.sum(d, axis=1, keepdims=True), db + jnp.sum(d_pre, axis=1, keepdims=True)

        zero = jnp.zeros((heads, 1), F32)
        _, db = lax.fori_loop(0, n_chunks, step, (zero, zero))
        db_ref[...] = db

    return pl.pallas_call(body, name="gate_bwd",
                          out_shape=(jax.ShapeDtypeStruct((heads, t_rows), F32), jax.ShapeDtypeStruct((heads, 1), F32)),
                          compiler_params=pltpu.CompilerParams(vmem_limit_bytes=VMEM_LIMIT))(dcum_t, logit_t, bias_col)


def _norm_fwd(x, g, name):
    return _rows_call(lambda xv, gv: _rms_fwd(xv, gv), name, [x], [g], [(x.shape[1], BF16)])


def _norm_bwd(x, g, dh, dres, name):
    def fn(xv, dhv, dresv, gv):
        dx, dg = _rms_bwd(xv, gv, dhv)
        dx = dresv + dx
        return dx, dx, dg
    return _rows_call(fn, name, [x, dh, dres], [g], [(x.shape[1], F32), (x.shape[1], BF16)], [((1, x.shape[1]), F32)])


def _loss_fwd_bwd(y, target):
    d_model = y.shape[1]

    def fn(yv, tv):
        err = yv - tv
        dy = err * (1.0 / d_model)
        return dy, dy, jnp.sum(jnp.sum(err * err, axis=1, keepdims=True), axis=0, keepdims=True)
    return _rows_call(fn, "loss", [y, target], [], [(d_model, F32), (d_model, BF16)], [((1, 1), F32)])


def _heads_apply(fn, n_heads, width, *tiles):
    return [fn(*[t[:, h * width:(h + 1) * width] for t in tiles]) for h in range(n_heads)]


def _fox_norm_fwd(pb, gq, gk, heads):
    width = heads * HEAD_DIM

    def fn(qk, gqv, gkv):
        q = jnp.concatenate(_heads_apply(lambda t: _rms_fwd(t, gqv), heads, HEAD_DIM, qk[:, :width]), axis=1)
        k = jnp.concatenate(_heads_apply(lambda t: _rms_fwd(t, gkv), heads, HEAD_DIM, qk[:, width:]), axis=1)
        return q, k
    return _rows_call(fn, "fox_norm_fwd", [(pb, 2 * width, 0)], [gq, gk], [(width, BF16), (width, BF16)])


def _fox_norm_bwd(pb, gq, gk, dq, dk, heads):
    width = heads * HEAD_DIM

    def fn(qk, dqv, dkv, gqv, gkv):
        res_q = _heads_apply(lambda t, d: _rms_bwd(t, gqv, d), heads, HEAD_DIM, qk[:, :width], dqv)
        res_k = _heads_apply(lambda t, d: _rms_bwd(t, gkv, d), heads, HEAD_DIM, qk[:, width:], dkv)
        dqk = jnp.concatenate([r[0] for r in res_q] + [r[0] for r in res_k], axis=1)
        return dqk, sum(r[1] for r in res_q), sum(r[1] for r in res_k)
    return _rows_call(fn, "fox_norm_bwd", [(pb, 2 * width, 0), dq, dk], [gq, gk], [(2 * width, BF16)],
                      [((1, HEAD_DIM), F32), ((1, HEAD_DIM), F32)])


def _lora_norm_fwd(down, gq, gkv, rank):
    def fn(dv, gqv, gkvv):
        return _rms_fwd(dv[:, :rank], gqv), _rms_fwd(dv[:, rank:], gkvv)
    return _rows_call(fn, "lora_norm_fwd", [(down, 2 * rank, 0)], [gq, gkv], [(rank, BF16), (rank, BF16)])


def _lora_norm_bwd(down, gq, gkv, dcq, dckv, dkpe, rank):
    def fn(dv, dcqv, dckvv, dkpev, gqv, gkvv):
        dxq, dgq = _rms_bwd(dv[:, :rank], gqv, dcqv)
        dxkv, dgkv = _rms_bwd(dv[:, rank:], gkvv, dckvv)
        return jnp.concatenate([dxq, dxkv, dkpev], axis=1), dgq, dgkv
    return _rows_call(fn, "lora_norm_bwd", [(down, 2 * rank, 0), dcq, dckv, dkpe], [gq, gkv],
                      [(2 * rank + LANES, BF16)], [((1, rank), F32), ((1, rank), F32)])


def _rope_tables(pos_col, inv_freq, sin_sign):
    def fn(pos, invf, sign):
        ang = pos.astype(F32) * invf
        return jnp.cos(ang) * jnp.abs(sign), jnp.sin(ang) * sign
    return _rows_call(fn, "rope_tables", [pos_col], [inv_freq, sin_sign], [(LANES, F32), (LANES, F32)])


def _mla_prep_fwd(q_raw, kv, down, kpe_block, qg, kg, cos_t, sin_s):
    def fn(qv, kvv, kpe, cosv, sinv, qgv, kgv):
        qs, ks, vs = [], [], []
        for h in range(MLA_HEADS):
            qn = _rms_fwd(qv[:, h * MLA_PAD_DIM:(h + 1) * MLA_PAD_DIM], qgv, MLA_QK_DIM)
            qs += [qn[:, :HEAD_DIM], _rope_fwd(qn[:, HEAD_DIM:], cosv, sinv)]
            k_full = jnp.concatenate([kvv[:, h * MLA_PAD_DIM:h * MLA_PAD_DIM + HEAD_DIM], kpe], axis=1)
            kn = _rms_fwd(k_full, kgv, MLA_QK_DIM)
            ks += [kn[:, :HEAD_DIM], _rope_fwd(kn[:, HEAD_DIM:], cosv, sinv)]
            vs.append(kvv[:, h * MLA_PAD_DIM + HEAD_DIM:(h + 1) * MLA_PAD_DIM])
        return jnp.concatenate(qs, axis=1), jnp.concatenate(ks, axis=1), jnp.concatenate(vs, axis=1)
    wide = MLA_HEADS * MLA_PAD_DIM
    return _rows_call(fn, "mla_prep_fwd", [q_raw, kv, (down, LANES, kpe_block), cos_t, sin_s], [qg, kg],
                      [(wide, BF16), (wide, BF16), (MLA_HEADS * HEAD_DIM, BF16)], tile=128)


def _mla_prep_bwd(q_raw, kv, down, kpe_block, qg, kg, cos_t, sin_s, dq, dk, dv):
    def fn(qv, kvv, kpe, cosv, sinv, dqv, dkv, dvv, qgv, kgv):
        dqs, dkvs = [], []
        dkpe = jnp.zeros_like(kpe)
        dqg = jnp.zeros_like(qgv)
        dkg = jnp.zeros_like(kgv)
        for h in range(MLA_HEADS):
            lo, hi = h * MLA_PAD_DIM, (h + 1) * MLA_PAD_DIM
            dqn = jnp.concatenate([dqv[:, lo:lo + HEAD_DIM], _rope_bwd(dqv[:, lo + HEAD_DIM:hi], cosv, sinv)], axis=1)
            dqh, dg = _rms_bwd(qv[:, lo:hi], qgv, dqn, MLA_QK_DIM)
            dqs.append(dqh)
            dqg = dqg + dg
            k_full = jnp.concatenate([kvv[:, lo:lo + HEAD_DIM], kpe], axis=1)
            dkn = jnp.concatenate([dkv[:, lo:lo + HEAD_DIM], _rope_bwd(dkv[:, lo + HEAD_DIM:hi], cosv, sinv)], axis=1)
            dkh, dg = _rms_bwd(k_full, kgv, dkn, MLA_QK_DIM)
            dkg = dkg + dg
            dkpe = dkpe + dkh[:, HEAD_DIM:]
            dkvs += [dkh[:, :HEAD_DIM], dvv[:, h * HEAD_DIM:(h + 1) * HEAD_DIM]]
        return jnp.concatenate(dqs, axis=1), jnp.concatenate(dkvs, axis=1), dkpe, dqg, dkg
    wide = MLA_HEADS * MLA_PAD_DIM
    return _rows_call(fn, "mla_prep_bwd", [q_raw, kv, (down, LANES, kpe_block), cos_t, sin_s, dq, dk, dv], [qg, kg],
                      [(wide, BF16), (wide, BF16), (LANES, F32)], [((1, MLA_PAD_DIM), F32), ((1, MLA_PAD_DIM), F32)],
                      tile=128)


def _sqrelu_up(acc):
    return acc, jnp.square(jnp.maximum(acc, 0.0))


def _sqrelu_grad(acc, u):
    return (acc * (2.0 * jnp.maximum(u, 0.0)),)


def _mlp_fwd(x, g, w_up, w_down, tag):
    h = _norm_fwd(x, g, f"mlp_norm_fwd{tag}")
    u, a = _matmul(h, w_up, "nn", f"mlp_up{tag}", (F32, BF16), _sqrelu_up)
    return _matmul(a, w_down, "nn", f"mlp_down{tag}", (F32,), _add_residual, (x,)), (h, u, a)


def _mlp_bwd(x, g, w_up, w_down, saved, dy, dy16, tag):
    h, u, a = saved
    dw_down = _matmul(a, dy16, "tn", f"mlp_dwdown{tag}", (BF16,))
    du = _matmul(dy16, w_down, "nt", f"mlp_du{tag}", (BF16,), _sqrelu_grad, (u,))
    dw_up = _matmul(h, du, "tn", f"mlp_dwup{tag}", (BF16,))
    dh = _matmul(du, w_up, "nt", f"mlp_dh{tag}")
    dx, dx16, dg = _norm_bwd(x, g, dh, dy, f"mlp_norm_bwd{tag}")
    return dx, dx16, dg, dw_up, dw_down


def _local_step(x, pos_col, target, w, dist=None):
    w = dict(w)
    hs = w["w_a"].shape[1] // (4 * HEAD_DIM)
    sb_w = hs * HEAD_DIM
    grads, received = {}, {}

    def gather_in(group):
        return [("gather", blk) for blk in dist["blocks"][group]] if dist else []

    def exchange_in(group):
        return [("all_to_all", slots) for slots in dist["slots_of"](group, grads)] if dist else []

    h0 = _norm_fwd(x, w["ln_mix0"], "mix0_norm_fwd")
    pa = _matmul(h0, w["w_a"], "nn", "in_proj_a", (BF16,))
    pb = _matmul(h0, w["w_b"], "nn", "in_proj_b")
    pat = pa[:, sb_w:].T
    o_sb, o_sb_fine, *got = _attn_fwd("stick", pa, pat, pa, hs, HEAD_DIM, HEAD_DIM, HEAD_DIM ** -0.5, "stick_fwd",
                                      q_off=0, k_off=0, v_off=2 * hs, tq=ATTN_TQ_WIDE, hosted=gather_in("mlp0"))
    if dist:
        w.update(dist["weights_of"]("mlp0", got))
    logit_t = pb[:, 2 * sb_w:2 * sb_w + hs].T
    bias_col = w["b_f"][0, :hs].reshape(hs, 1)
    f_cum = _gate_fwd(logit_t, bias_col)
    f_col, f_row = f_cum[:, :, None], f_cum[:, None, :]
    qf, kf = _fox_norm_fwd(pb, w["fox_q_g"], w["fox_k_g"], hs)
    kft = kf.T
    o_fx, o_fx_fine, lse_fx, *got = _attn_fwd("softmax", qf, kft, pa, hs, HEAD_DIM, HEAD_DIM, HEAD_DIM ** -0.5,
                                              "fox_fwd", v_off=3 * hs, fcol=f_col, frow=f_row, tq=ATTN_TQ_WIDE, tk=ATTN_TK_SOFTMAX,
                                              hosted=gather_in("layer1"))
    if dist:
        w.update(dist["weights_of"]("layer1", got))
    o0 = jnp.concatenate([o_sb, o_fx], axis=1)
    x1 = _matmul(o0, w["w_o0"], "nn", "out_proj0", (F32,), _add_residual, (x,))
    x2, mlp0 = _mlp_fwd(x1, w["ln_mlp0"], w["w_up0"], w["w_dn0"], "0")

    rank = w["w_uq"].shape[0]
    h2 = _norm_fwd(x2, w["ln_mix1"], "mix1_norm_fwd")
    down = _matmul(h2, w["w_down"], "nn", "mla_down")
    cqn, ckvn = _lora_norm_fwd(down, w["q_a_g"], w["kv_a_g"], rank)
    q_raw = _matmul(cqn, w["w_uq"], "nn", "mla_uq")
    kv = _matmul(ckvn, w["w_ukv"], "nn", "mla_ukv")
    cos_t, sin_s = _rope_tables(pos_col, w["inv_freq"], w["sin_sign"])
    kpe_block = 2 * rank // LANES
    qm, km, vm = _mla_prep_fwd(q_raw, kv, down, kpe_block, w["mla_q_g"], w["mla_k_g"], cos_t, sin_s)
    kmt, vmt = km.T, vm.T
    o_m, o_m_fine, lse_m = _attn_fwd("softmax", qm, kmt, vm, MLA_HEADS, MLA_PAD_DIM, HEAD_DIM, MLA_QK_DIM ** -0.5,
                                     "mla_fwd", tq=ATTN_TQ_WIDE, tk=ATTN_TK_SOFTMAX)
    x3 = _matmul(o_m, w["w_o1"], "nn", "out_proj1", (F32,), _add_residual, (x2,))
    x4, mlp1 = _mlp_fwd(x3, w["ln_mlp1"], w["w_up1"], w["w_dn1"], "1")

    dy, dy16, sq_err = _loss_fwd_bwd(x4, target)

    dx3, dx3_16, grads["ln_mlp1"], grads["w_up1"], grads["w_dn1"] = _mlp_bwd(
        x3, w["ln_mlp1"], w["w_up1"], w["w_dn1"], mlp1, dy, dy16, "1")
    grads["w_o1"] = _matmul(o_m, dx3_16, "tn", "dw_o1", (BF16,))
    do_m = _matmul(dx3_16, w["w_o1"], "nt", "do_mla", (BF16,))
    dqm, dkm, dvm, *got = _attn_bwd("softmax", qm, km, kmt, vmt, o_m_fine, do_m, MLA_HEADS, MLA_PAD_DIM, HEAD_DIM,
                                    MLA_QK_DIM ** -0.5, "mla_bwd", lse=lse_m, tq=ATTN_TQ_WIDE, tk=ATTN_TK_SOFTMAX,
                                    hosted=exchange_in("mlp1"))
    received["mlp1"] = got
    dq_raw, dkv, dkpe, grads["mla_q_g"], grads["mla_k_g"] = _mla_prep_bwd(
        q_raw, kv, down, kpe_block, w["mla_q_g"], w["mla_k_g"], cos_t, sin_s, dqm, dkm, dvm)
    grads["w_uq"] = _matmul(cqn, dq_raw, "tn", "dw_uq", (BF16,))
    grads["w_ukv"] = _matmul(ckvn, dkv, "tn", "dw_ukv", (BF16,))
    dcqn = _matmul(dq_raw, w["w_uq"], "nt", "d_cq")
    dckvn = _matmul(dkv, w["w_ukv"], "nt", "d_ckv")
    ddown, grads["q_a_g"], grads["kv_a_g"] = _lora_norm_bwd(down, w["q_a_g"], w["kv_a_g"], dcqn, dckvn, dkpe, rank)
    grads["w_down"] = _matmul(h2, ddown, "tn", "dw_down", (BF16,))
    dh2 = _matmul(ddown, w["w_down"], "nt", "d_h2")
    dx2, dx2_16, grads["ln_mix1"] = _norm_bwd(x2, w["ln_mix1"], dh2, dx3, "mix1_norm_bwd")

    dx1, dx1_16, grads["ln_mlp0"], grads["w_up0"], grads["w_dn0"] = _mlp_bwd(
        x1, w["ln_mlp0"], w["w_up0"], w["w_dn0"], mlp0, dx2, dx2_16, "0")
    grads["w_o0"] = _matmul(o0, dx1_16, "tn", "dw_o0", (BF16,))
    do0 = _matmul(dx1_16, w["w_o0"], "nt", "do_mix0", (BF16,))
    dq_sb, dk_sb, dv_sb, *got = _attn_bwd("stick", pa, pa, pat, pat, o_sb_fine, do0, hs, HEAD_DIM, HEAD_DIM,
                                          HEAD_DIM ** -0.5, "stick_bwd", q_off=0, k_off=hs, kt_off=0, vt_off=hs, do_off=0,
                                          tq=ATTN_TQ_WIDE,
                                          hosted=exchange_in("with_stick_bwd"))
    received["with_stick_bwd"] = got
    dqf, dkf, dv_fx, ds_rows, ds_cols, *got = _attn_bwd(
        "softmax", qf, kf, kft, pat, o_fx_fine, do0, hs, HEAD_DIM, HEAD_DIM, HEAD_DIM ** -0.5, "fox_bwd", vt_off=2 * hs,
        tq=ATTN_TQ_WIDE, tk=ATTN_TK_SOFTMAX,
        do_off=hs, lse=lse_fx, fcol=f_col, frow=f_row, hosted=exchange_in("with_fox_bwd"))
    received["with_fox_bwd"] = got
    dqk_fx, grads["fox_q_g"], grads["fox_k_g"] = _fox_norm_bwd(pb, w["fox_q_g"], w["fox_k_g"], dqf, dkf, hs)
    dlogit_t, db_f = _gate_bwd(ds_rows[:, :, 0] - ds_cols[:, 0, :], logit_t, bias_col)
    grads["b_f"] = db_f.reshape(1, hs)
    dpa = jnp.concatenate([dq_sb.astype(BF16), dk_sb.astype(BF16), dv_sb.astype(BF16), dv_fx.astype(BF16)], axis=1)
    dlogit_pad = jnp.pad(dlogit_t.T.astype(BF16), ((0, 0), (0, pb.shape[1] - 2 * sb_w - hs)))
    dpb = jnp.concatenate([dqk_fx, dlogit_pad], axis=1)
    grads["w_a"] = _matmul(h0, dpa, "tn", "dw_a", (BF16,))
    grads["w_b"] = _matmul(h0, dpb, "tn", "dw_b", (BF16,))
    dh0 = _matmul(dpb, w["w_b"], "nt", "d_h0_b")
    res = _matmul(dpa, w["w_a"], "nt", "d_h0_a", (F32,), _add_residual, (dh0,), hosted=exchange_in("mix0"))
    dh0, received["mix0"] = (res[0], list(res[1:])) if dist else (res, [])
    grad_x, _, grads["ln_mix0"] = _norm_bwd(x, w["ln_mix0"], dh0, dx1, "mix0_norm_bwd")
    return sq_err, grad_x, grads, received


PIECES = {
    "sf_w_in": ("sf_w_in", 0, 1), "sf_w_o": ("sf_w_o", 0, 0), "mla_w_down": ("mla_w_down", 0, 0),
    "mla_w_uq": ("mla_w_uq", 0, 1), "mla_w_ukv": ("mla_w_ukv", 0, 1), "mla_w_o": ("mla_w_o", 0, 0),
    "mlp_w_up0": ("mlp_w_up", 0, 1), "mlp_w_up1": ("mlp_w_up", 1, 1),
    "mlp_w_down0": ("mlp_w_down", 0, 0), "mlp_w_down1": ("mlp_w_down", 1, 0),
}
GROUPS = {
    "mix0": ["sf_w_in"], "mlp0": ["sf_w_o", "mlp_w_up0", "mlp_w_down0"],
    "mla": ["mla_w_down", "mla_w_uq", "mla_w_ukv", "mla_w_o"], "mlp1": ["mlp_w_up1", "mlp_w_down1"],
}
GROUPS["layer1"] = GROUPS["mla"] + GROUPS["mlp1"]
GROUPS["with_stick_bwd"] = GROUPS["mla"] + ["mlp_w_up0"]
GROUPS["with_fox_bwd"] = ["mlp_w_down0", "sf_w_o"]
SMALL = ["ln_mix_g", "ln_mlp_g", "sf_b_f", "fox_q_g", "fox_k_g", "mla_q_a_g", "mla_kv_a_g", "mla_q_g", "mla_k_g"]
ALL_W = ["ln_mix_g", "ln_mlp_g", "sf_w_in", "sf_b_f", "fox_q_g", "fox_k_g", "sf_w_o", "mla_w_down", "mla_q_a_g",
         "mla_kv_a_g", "mla_w_uq", "mla_w_ukv", "mla_q_g", "mla_k_g", "mla_w_o", "mlp_w_up", "mlp_w_down"]


def _weights_mix0(full, small):
    w_in = full["sf_w_in"]
    d_model = w_in.shape[0]
    n_fx = small["sf_b_f"].shape[1]
    sb_w = (w_in.shape[1] - n_fx) // 6
    cols = lambda i: w_in[:, i * sb_w:(i + 1) * sb_w]
    w_a = jnp.concatenate([cols(0), cols(1), cols(2), cols(5)], axis=1)
    w_b = jnp.concatenate([cols(3), cols(4), w_in[:, 6 * sb_w:], jnp.zeros((d_model, LANES - n_fx), w_in.dtype)], axis=1)
    half = ROPE_DIM // 2
    inv_freq = ROPE_THETA ** (-jnp.arange(half, dtype=F32) / half)
    zeros64 = jnp.zeros((ROPE_DIM,), F32)
    pad256 = lambda g: jnp.pad(g, ((0, 0), (0, MLA_PAD_DIM - MLA_QK_DIM)))
    pad_lanes = lambda g: jnp.pad(g, ((0, 0), (0, LANES - g.shape[1])))
    return dict(
        ln_mix0=small["ln_mix_g"][0:1], ln_mix1=small["ln_mix_g"][1:2],
        ln_mlp0=small["ln_mlp_g"][0:1], ln_mlp1=small["ln_mlp_g"][1:2],
        w_a=w_a, w_b=w_b, b_f=pad_lanes(small["sf_b_f"]), fox_q_g=small["fox_q_g"], fox_k_g=small["fox_k_g"],
        q_a_g=small["mla_q_a_g"], kv_a_g=small["mla_kv_a_g"],
        mla_q_g=pad256(small["mla_q_g"]), mla_k_g=pad256(small["mla_k_g"]),
        inv_freq=jnp.concatenate([inv_freq, inv_freq, zeros64]).reshape(1, LANES),
        sin_sign=jnp.concatenate([-jnp.ones((half,), F32), jnp.ones((half,), F32), zeros64]).reshape(1, LANES),
    )


def _weights_mlp0(full):
    return dict(w_o0=full["sf_w_o"], w_up0=full["mlp_w_up0"], w_dn0=full["mlp_w_down0"])


def _weights_layer1(full):
    rank = full["mla_w_uq"].shape[0]
    w_uq = full["mla_w_uq"].reshape(rank, MLA_HEADS, MLA_QK_DIM)
    w_uq = jnp.pad(w_uq, ((0, 0), (0, 0), (0, MLA_PAD_DIM - MLA_QK_DIM))).reshape(rank, MLA_HEADS * MLA_PAD_DIM)
    return dict(w_down=jnp.pad(full["mla_w_down"], ((0, 0), (0, LANES - ROPE_DIM))), w_uq=w_uq,
                w_ukv=full["mla_w_ukv"], w_o1=full["mla_w_o"], w_up1=full["mlp_w_up1"], w_dn1=full["mlp_w_down1"])


WEIGHTS_OF = {"mlp0": _weights_mlp0, "layer1": _weights_layer1}


def _piece_grad(g, piece):
    if piece == "sf_w_in":
        n_fx = g["b_f"].shape[1]
        ga, gb = g["w_a"], g["w_b"]
        sb_w = ga.shape[1] // 4
        ca = lambda i: ga[:, i * sb_w:(i + 1) * sb_w]
        return jnp.concatenate([ca(0), ca(1), ca(2), gb[:, :sb_w], gb[:, sb_w:2 * sb_w], ca(3),
                                gb[:, 2 * sb_w:2 * sb_w + n_fx]], axis=1)
    if piece == "mla_w_uq":
        rank = g["w_uq"].shape[0]
        return g["w_uq"].reshape(rank, MLA_HEADS, MLA_PAD_DIM)[:, :, :MLA_QK_DIM].reshape(rank, MLA_HEADS * MLA_QK_DIM)
    if piece == "mla_w_down":
        return g["w_down"][:, :g["w_down"].shape[1] - (LANES - ROPE_DIM)]
    return g[{"sf_w_o": "w_o0", "mla_w_ukv": "w_ukv", "mla_w_o": "w_o1", "mlp_w_up0": "w_up0", "mlp_w_up1": "w_up1",
              "mlp_w_down0": "w_dn0", "mlp_w_down1": "w_dn1"}[piece]]


def _small_grads(g):
    return {
        "ln_mix_g": jnp.concatenate([g["ln_mix0"], g["ln_mix1"]], axis=0),
        "ln_mlp_g": jnp.concatenate([g["ln_mlp0"], g["ln_mlp1"]], axis=0),
        "sf_b_f": g["b_f"], "fox_q_g": g["fox_q_g"], "fox_k_g": g["fox_k_g"],
        "mla_q_a_g": g["q_a_g"], "mla_kv_a_g": g["kv_a_g"],
        "mla_q_g": g["mla_q_g"][:, :MLA_QK_DIM], "mla_k_g": g["mla_k_g"][:, :MLA_QK_DIM],
    }


PACK_TILE = 1024


def _as_rows(a, row_multiple=16):
    flat = a.reshape(-1)
    rows = -(-flat.shape[0] // LANES)
    rows = -(-rows // row_multiple) * row_multiple
    return jnp.pad(flat, (0, rows * LANES - flat.shape[0])).reshape(rows, LANES)


def _pack_rows(parts, axis, dtype, row_multiple=PACK_TILE, spare_rows=0):
    used = sum(p.shape[axis] for p in parts)
    shape = list(parts[0].shape)
    shape[axis] = -(-used // row_multiple) * row_multiple + spare_rows - used
    return jnp.concatenate([p.astype(dtype) for p in parts] + [jnp.ones(shape, dtype)], axis=axis)


def _unshard(stack, axis):
    moved = jnp.moveaxis(stack, 0, axis)
    shape = list(stack.shape[1:])
    shape[axis] *= N_DEV
    return moved.reshape(shape)


def _shard_stack(full, axis):
    shape = list(full.shape)
    shape[axis:axis + 1] = [N_DEV, shape[axis] // N_DEV]
    return jnp.moveaxis(full.reshape(shape), axis, 0)


OPT_TILE_ELEMS = 128 * 1024


def _row_tile(rows, cols):
    best = 16
    for t in range(16, rows + 1, 16):
        if rows % t == 0 and t * cols <= OPT_TILE_ELEMS:
            best = t
    assert rows % best == 0
    return best


def _cast_bf16(a, name):
    return _rows_call(lambda v: v, name, [a], [], [(a.shape[1], BF16)], tile=_row_tile(*a.shape))


def _adam_math(w, g, m, v):
    m = ADAM_B1 * m + (1.0 - ADAM_B1) * g
    v = ADAM_B2 * v + (1.0 - ADAM_B2) * jnp.square(g)
    m_hat = m / (1.0 - ADAM_B1 ** ADAM_STEP)
    v_hat = v / (1.0 - ADAM_B2 ** ADAM_STEP)
    delta = -ADAM_LR * (m_hat / (jnp.sqrt(v_hat) + ADAM_EPS) + ADAM_WD * w)
    return delta, m, v


def _adam_big(recvs, w, m, v, name, hosted=()):
    layers, rows, cols = w.shape
    tile = _row_tile(rows, cols)
    n_tiles = rows // tile
    n_host = len(hosted)

    def body(*refs):
        recv_refs = refs[:layers]
        w_ref, m_ref, v_ref = refs[layers:layers + 3]
        base = layers + 3 + n_host
        g_ref, d_ref, nm_ref, nv_ref = refs[base:base + 4]
        layer = pl.program_id(0)
        host_args = (hosted, refs[layers + 3:base], refs[base + 4:base + 4 + n_host], refs[base + 4 + n_host:],
                     layer * n_tiles + pl.program_id(1), layers * n_tiles)
        _run_hosted(*host_args, "start")

        def total(r_ref):
            acc = r_ref[0].astype(F32)
            for s in range(1, N_DEV):
                acc = acc + r_ref[s].astype(F32)
            return acc

        g = total(recv_refs[0])
        for j in range(1, layers):
            g = jnp.where(layer == j, total(recv_refs[j]), g)
        delta, nm, nv = _adam_math(w_ref[...], g, m_ref[...], v_ref[...])
        g_ref[...] = g
        d_ref[...] = delta
        nm_ref[...] = nm
        nv_ref[...] = nv
        _run_hosted(*host_args, "finish")

    def recv_spec(j):
        return pl.BlockSpec((N_DEV, tile, cols),
                            lambda l, i: (0, jnp.where(l == j, i, jnp.where(l < j, 0, n_tiles - 1)), 0))

    spec = pl.BlockSpec((None, tile, cols), lambda l, i: (l, i, 0))
    out = jax.ShapeDtypeStruct(w.shape, F32)
    return pl.pallas_call(
        body, name=name, grid=(layers, n_tiles),
        in_specs=[recv_spec(j) for j in range(layers)] + [spec] * 3 + [ANY_SPEC] * n_host,
        out_specs=[spec] * 4 + [ANY_SPEC] * n_host,
        out_shape=[out] * 4 + [_exchange_out_shape(kd, arr) for kd, arr in hosted],
        scratch_shapes=EXCHANGE_SCRATCH * n_host, compiler_params=_params("arbitrary", "arbitrary"),
    )(*recvs, w, m, v, *[arr for _, arr in hosted])


def _sum_slots(gathered):
    rows = gathered.shape[1]

    def body(r_ref, o_ref):
        acc = r_ref[0]
        for s in range(1, N_DEV):
            acc = acc + r_ref[s]
        o_ref[...] = acc

    return pl.pallas_call(body, name="sum_small", out_shape=jax.ShapeDtypeStruct((rows, LANES), F32))(gathered)


def _adam_small(w, g, m, v):
    def fn(wv, gv, mv, vv):
        return _adam_math(wv, gv, mv, vv)
    return _rows_call(fn, "adam_small", [w, g, m, v], [], [(LANES, F32)] * 3, tile=w.shape[0])


def kernel(x, positions, ln_mix_g, ln_mlp_g, sf_w_in, sf_b_f, fox_q_g, fox_k_g, sf_w_o, mla_w_down, mla_q_a_g, mla_kv_a_g, mla_w_uq, mla_w_ukv, mla_q_g, mla_k_g, mla_w_o, mlp_w_up, mlp_w_down, loss_target, m_ln_mix_g, m_ln_mlp_g, m_sf_w_in, m_sf_b_f, m_fox_q_g, m_fox_k_g, m_sf_w_o, m_mla_w_down, m_mla_q_a_g, m_mla_kv_a_g, m_mla_w_uq, m_mla_w_ukv, m_mla_q_g, m_mla_k_g, m_mla_w_o, m_mlp_w_up, m_mlp_w_down, v_ln_mix_g, v_ln_mlp_g, v_sf_w_in, v_sf_b_f, v_fox_q_g, v_fox_k_g, v_sf_w_o, v_mla_w_down, v_mla_q_a_g, v_mla_kv_a_g, v_mla_w_uq, v_mla_w_ukv, v_mla_q_g, v_mla_k_g, v_mla_w_o, v_mlp_w_up, v_mlp_w_down):
    given = dict(locals())
    wts = {n: given[n] for n in ALL_W}
    mom = {n: given["m_" + n] for n in ALL_W}
    var = {n: given["v_" + n] for n in ALL_W}
    me = 4 * lax.axis_index("x") + 2 * lax.axis_index("y") + lax.axis_index("c")
    t_rows, d_model = x.shape[1], x.shape[2]
    big = sorted({name for name, _, _ in PIECES.values()})

    def whole_pieces(gathered, group):
        return {p: _unshard(s, PIECES[p][2]) for p, s in zip(GROUPS[group], gathered)}

    def w_in_slot_parts(grads):
        slots = _shard_stack(_piece_grad(grads, "sf_w_in"), PIECES["sf_w_in"][2])
        cuts = [0] + [(slots.shape[1] * f // 16) // 16 * 16 for f in (6, 11)] + [slots.shape[1]]
        return [slots[:, a:b] for a, b in zip(cuts, cuts[1:])]

    def slots_of(group, grads):
        if group == "mix0":
            return w_in_slot_parts(grads)[:1]
        return [_shard_stack(_piece_grad(grads, p), PIECES[p][2]) for p in GROUPS[group]]

    cast = {n: _cast_bf16(wts[n].reshape(-1, wts[n].shape[2]), f"cast_{n}").reshape(wts[n].shape) for n in big}
    blocks = {grp: [cast[PIECES[p][0]][PIECES[p][1]] for p in GROUPS[grp]] for grp in ("mix0", "mlp0", "layer1")}
    mix0 = whole_pieces(_exchange("gather", blocks["mix0"], "gather_mix0"), "mix0")
    gains, = _exchange("gather", [_as_rows(jnp.concatenate([mla_q_a_g, mla_kv_a_g], axis=1))], "gather_gains")
    lora_n = mla_q_a_g.shape[1]
    gains_flat = gains.reshape(N_DEV, -1)[:, :2 * lora_n]
    small = dict(ln_mix_g=ln_mix_g, ln_mlp_g=ln_mlp_g, sf_b_f=sf_b_f, fox_q_g=fox_q_g, fox_k_g=fox_k_g,
                 mla_q_a_g=gains_flat[:, :lora_n].reshape(1, -1), mla_kv_a_g=gains_flat[:, lora_n:].reshape(1, -1),
                 mla_q_g=mla_q_g, mla_k_g=mla_k_g)
    dist = dict(blocks=blocks, slots_of=slots_of,
                weights_of=lambda grp, gathered: WEIGHTS_OF[grp](whole_pieces(gathered, grp)))
    sq_err, grad_x, g, received = _local_step(x[0], positions.reshape(t_rows, 1), loss_target[0],
                                              _weights_mix0(mix0, small), dist)

    recv_of = {p: r for grp in ("mlp1", "with_stick_bwd", "with_fox_bwd") for p, r in zip(GROUPS[grp], received[grp])}
    late_parts = dict(zip(("mlp_w_down", "mlp_w_up"), w_in_slot_parts(g)[1:]))
    w_in_recv = list(received["mix0"])
    results = {kind: {} for kind in ("grad", "delta", "new_m", "new_v")}
    for n in sorted(big, key=lambda name: (name == "sf_w_in", name not in late_parts)):
        if n == "sf_w_in":
            recv_of["sf_w_in"] = jnp.concatenate(w_in_recv, axis=1)
        layers = [p for _, p in sorted((layer, p) for p, (name, layer, _) in PIECES.items() if name == n)]
        hosted = [("all_to_all", late_parts[n])] if n in late_parts else []
        outs = _adam_big([recv_of[p] for p in layers], wts[n], mom[n], var[n], f"adam_{n}", hosted)
        w_in_recv += outs[4:]
        for kind, out in zip(("grad", "delta", "new_m", "new_v"), outs[:4]):
            results[kind][n] = out

    small_g = _small_grads(g)
    small_parts = [_as_rows(small_g[n], 8) for n in SMALL] + [_as_rows(sq_err, 8)]
    small_sum = _sum_slots(_exchange("gather", [_pack_rows(small_parts, 0, F32, 8, 8)], "gather_small_grads")[0])
    red, off = {}, 0
    for n, p in zip(SMALL + ["loss"], small_parts):
        red[n] = small_sum[off:off + p.shape[0]].reshape(-1)
        off += p.shape[0]
    loss = 0.5 * red["loss"][0] / d_model
    for n in SMALL:
        if n in ("mla_q_a_g", "mla_kv_a_g"):
            results["grad"][n] = lax.dynamic_slice(red[n], (me * lora_n,), (lora_n,)).reshape(wts[n].shape)
        else:
            results["grad"][n] = red[n][:wts[n].size].reshape(wts[n].shape)
    pack_small = lambda d: jnp.concatenate([_as_rows(d[n], 8) for n in SMALL], axis=0)
    small_out = _adam_small(pack_small(wts), pack_small(results["grad"]), pack_small(mom), pack_small(var))
    off = 0
    for n in SMALL:
        r = _as_rows(wts[n], 8).shape[0]
        for kind, packed in zip(["delta", "new_m", "new_v"], small_out):
            results[kind][n] = packed[off:off + r].reshape(-1)[:wts[n].size].reshape(wts[n].shape)
        off += r

    outs = [loss, grad_x[None]]
    for kind in ["grad", "delta", "new_m", "new_v"]:
        outs += [results[kind][n] for n in ALL_W]
    return tuple(outs)
```

```python
import functools
import math

import jax
import jax.numpy as jnp
import numpy as np
from jax import lax
from jax.experimental import pallas as pl
from jax.experimental.pallas import tpu as pltpu

F32 = jnp.float32
BF16 = jnp.bfloat16

NORM_EPS = 1e-6
ROPE_THETA = 10000.0
HEAD_DIM = 128
ROPE_DIM = 64
MLA_HEADS = 16
MLA_QK_DIM = 192
MLA_PAD_DIM = 256
ADAM_LR, ADAM_B1, ADAM_B2, ADAM_EPS, ADAM_WD, ADAM_STEP = 0.001, 0.9, 0.999, 1e-08, 0.01, 10

N_DEV = 8
LANES = 128
VMEM_LIMIT = 56 * 1024 * 1024
MATMUL_VMEM_BUDGET = 40 * 1024 * 1024
MASKED = -1e30
LOG2E = 1.4426950408889634
ATTN_TQ, ATTN_TK = 256, 256
ATTN_TQ_WIDE = 512
ATTN_TK_SOFTMAX = 512
MESH = pl.DeviceIdType.MESH

NT_DIMS = (((1,), (1,)), ((), ()))
TN_DIMS = (((0,), (0,)), ((), ()))
NN_DIMS = (((1,), (0,)), ((), ()))


def _params(*sem):
    return pltpu.CompilerParams(dimension_semantics=sem, vmem_limit_bytes=VMEM_LIMIT)


def _pick(n, pref):
    best = None
    for t in range(LANES, min(n, pref) + 1, LANES):
        if n % t == 0:
            best = t
    return n if best is None or 2 * best < min(n, pref) else best


def _rows_call(fn, name, row_ins, full_ins, row_outs, acc_outs=(), tile=256):
    row_ins = [r if isinstance(r, tuple) else (r, r.shape[1], 0) for r in row_ins]
    t_rows = row_ins[0][0].shape[0]
    assert t_rows % tile == 0
    n_in = len(row_ins) + len(full_ins)
    n_row_out = len(row_outs)

    def body(*refs):
        res = fn(*[r[...] for r in refs[:n_in]])
        res = res if isinstance(res, tuple) else (res,)
        for ref, val in zip(refs[n_in:n_in + n_row_out], res[:n_row_out]):
            ref[...] = val.astype(ref.dtype)
        acc_refs = refs[n_in + n_row_out:]
        if acc_refs:
            @pl.when(pl.program_id(0) == 0)
            def _():
                for ref in acc_refs:
                    ref[...] = jnp.zeros_like(ref)
            for ref, val in zip(acc_refs, res[n_row_out:]):
                ref[...] += val.astype(ref.dtype)

    in_specs = [pl.BlockSpec((tile, w), functools.partial(lambda i, cb: (i, cb), cb=cb)) for _, w, cb in row_ins]
    in_specs += [pl.BlockSpec(a.shape, lambda i: (0, 0)) for a in full_ins]
    out_specs = [pl.BlockSpec((tile, c), lambda i: (i, 0)) for c, _ in row_outs]
    out_specs += [pl.BlockSpec(s, lambda i: (0, 0)) for s, _ in acc_outs]
    out_shape = [jax.ShapeDtypeStruct((t_rows, c), d) for c, d in row_outs]
    out_shape += [jax.ShapeDtypeStruct(s, d) for s, d in acc_outs]
    outs = pl.pallas_call(
        body, name=name, grid=(t_rows // tile,), in_specs=in_specs, out_specs=out_specs, out_shape=out_shape,
        compiler_params=_params("arbitrary"),
    )(*[r[0] for r in row_ins], *full_ins)
    return outs[0] if len(outs) == 1 else tuple(outs)


def _matmul_tiles(m, n, k, in_bytes, out_bytes):
    tn = n if n <= 1280 else _pick(n, 1024)
    tks = [k] + [k // d for d in (2, 4, 8, 16) if k % (d * LANES) == 0]
    for tk in [t for t in tks if t <= 4096] or [tks[-1]]:
        for tm in (1024, 512, 256):
            if m % tm:
                continue
            acc = 2 * tm * tn * 4 if tk < k else tm * tn * 4
            if 2 * (tm * tk + tk * tn) * in_bytes + 2 * tm * tn * out_bytes + acc <= MATMUL_VMEM_BUDGET:
                return tm, tn, tk
    raise ValueError(f"no matmul tiling for {m}x{n}x{k}")


def _matmul(a, b, form, name, out_dtypes=(F32,), epilogue=None, extras=(), hosted=()):
    if form == "nn":
        (m, k), n = a.shape, b.shape[1]
    elif form == "nt":
        (m, k), n = a.shape, b.shape[0]
    else:
        (k, m), n = a.shape, b.shape[1]
    in_bytes = max(a.dtype.itemsize, b.dtype.itemsize)
    out_bytes = sum(jnp.dtype(d).itemsize for d in out_dtypes) + sum(e.dtype.itemsize for e in extras)
    tm, tn, tk = _matmul_tiles(m, n, k, in_bytes, out_bytes)
    nk = k // tk
    dims = {"nn": NN_DIMS, "nt": NT_DIMS, "tn": TN_DIMS}[form]
    n_extra, n_out, n_host = len(extras), len(out_dtypes), len(hosted)
    grid = (m // tm, n // tn, nk)

    def body(*refs):
        a_ref, b_ref = refs[0], refs[1]
        extra_refs = refs[2:2 + n_extra]
        base = 2 + n_extra + n_host
        out_refs = refs[base:base + n_out]
        sems_at = base + n_out + n_host
        step = (pl.program_id(0) * grid[1] + pl.program_id(1)) * nk + pl.program_id(2)
        host_args = (hosted, refs[2 + n_extra:base], refs[base + n_out:sems_at], refs[sems_at:sems_at + 3 * n_host],
                     step, grid[0] * grid[1] * nk)
        _run_hosted(*host_args, "start")

        def finish(acc):
            vals = (acc,) if epilogue is None else epilogue(acc, *[r[...] for r in extra_refs])
            for ref, val in zip(out_refs, vals):
                ref[...] = val.astype(ref.dtype)

        part = lax.dot_general(a_ref[...].astype(BF16), b_ref[...].astype(BF16), dims, preferred_element_type=F32)
        if nk == 1:
            finish(part)
        else:
            acc_ref = refs[-1]
            kk = pl.program_id(2)

            @pl.when(kk == 0)
            def _():
                acc_ref[...] = part

            @pl.when(kk > 0)
            def _():
                acc_ref[...] += part

            @pl.when(kk == nk - 1)
            def _():
                finish(acc_ref[...])
        _run_hosted(*host_args, "finish")

    a_spec = pl.BlockSpec((tk, tm), lambda i, j, kk: (kk, i)) if form == "tn" else pl.BlockSpec((tm, tk), lambda i, j, kk: (i, kk))
    b_spec = pl.BlockSpec((tn, tk), lambda i, j, kk: (j, kk)) if form == "nt" else pl.BlockSpec((tk, tn), lambda i, j, kk: (kk, j))
    o_spec = pl.BlockSpec((tm, tn), lambda i, j, kk: (i, j))
    outs = pl.pallas_call(
        body, name=name, grid=grid, in_specs=[a_spec, b_spec] + [o_spec] * n_extra + [ANY_SPEC] * n_host,
        out_specs=[o_spec] * n_out + [ANY_SPEC] * n_host,
        out_shape=[jax.ShapeDtypeStruct((m, n), d) for d in out_dtypes]
        + [_exchange_out_shape(kd, arr) for kd, arr in hosted],
        scratch_shapes=EXCHANGE_SCRATCH * n_host + ([pltpu.VMEM((tm, tn), F32)] if nk > 1 else []),
        compiler_params=_params(*(("arbitrary",) * 3 if n_host else ("parallel", "parallel", "arbitrary"))),
    )(a, b, *extras, *[arr for _, arr in hosted])
    return outs[0] if n_out + n_host == 1 else tuple(outs)


def _add_residual(acc, res):
    return (acc + res,)


def _log_sigmoid_parts(z):
    return jnp.log1p(jnp.exp(-jnp.abs(z)))


def _log2_gates(z2):
    lg = jnp.log2(1.0 + jnp.exp2(-jnp.abs(z2)))
    log_beta = jnp.minimum(z2, 0.0) - lg
    return log_beta, log_beta - z2


def _rms_fwd(x, g, n=None):
    n = x.shape[-1] if n is None else n
    r = lax.rsqrt(jnp.sum(x * x, axis=-1, keepdims=True) / n + NORM_EPS)
    return x * r * g


def _rms_bwd(x, g, dout, n=None):
    n = x.shape[-1] if n is None else n
    r = lax.rsqrt(jnp.sum(x * x, axis=-1, keepdims=True) / n + NORM_EPS)
    y = x * r
    dg = jnp.sum(dout * y, axis=0, keepdims=True)
    dy = dout * g
    dx = r * (dy - y * (jnp.sum(dy * y, axis=-1, keepdims=True) / n))
    return dx, dg


def _swap_halves(r):
    lane = lax.broadcasted_iota(jnp.int32, r.shape, 1)
    return jnp.where(lane < ROPE_DIM // 2, pltpu.roll(r, LANES - ROPE_DIM // 2, 1), pltpu.roll(r, ROPE_DIM // 2, 1))


def _rope_fwd(r, cos_t, sin_s):
    return r * cos_t + _swap_halves(r) * sin_s


def _rope_bwd(dr, cos_t, sin_s):
    return dr * cos_t + _swap_halves(dr * sin_s)


def _split3(x):
    hi = x.astype(BF16)
    r1 = x - hi.astype(F32)
    mid = r1.astype(BF16)
    lo = (r1 - mid.astype(F32)).astype(BF16)
    return hi, mid, lo


def _mesh_position():
    x, y, c = lax.axis_index("x"), lax.axis_index("y"), lax.axis_index("c")
    return x, y, c, 4 * x + 2 * y + c


def _peer(x, y, c, k):
    bx, by, bc = (k >> 2) & 1, (k >> 1) & 1, k & 1
    px, py, pc = x ^ bx, y ^ by, c ^ bc
    return (px, py, pc), 4 * px + 2 * py + pc


def _gather_steps(x_ref, out_ref, send_sems, recv_sems, local_sem):
    x, y, c, me = _mesh_position()
    sibling = (x, y, 1 - c)
    chips = [(1 - x, y), (x, 1 - y), (1 - x, 1 - y)]

    def slot(px, py, pc):
        return out_ref.at[4 * px + 2 * py + pc]

    def copy(k, blk, to, src=None):
        return pltpu.make_async_remote_copy(
            src_ref=slot(*blk) if src is None else src, dst_ref=slot(*blk), send_sem=send_sems.at[k],
            recv_sem=recv_sems.at[k], device_id=to, device_id_type=MESH)

    mine = pltpu.make_async_copy(x_ref, out_ref.at[me], local_sem)
    first = [copy(0, (x, y, c), sibling, src=x_ref)]
    first += [copy(1 + j, (x, y, c), (*chip, c), src=x_ref) for j, chip in enumerate(chips)]
    passed = [copy(4 + j, (*chip, c), sibling) for j, chip in enumerate(chips)]

    def start():
        mine.start()
        for cp in first:
            cp.start()

    def forward():
        for j, chip in enumerate(chips):
            copy(1 + j, (*chip, c), (x, y, c)).wait_recv()
            passed[j].start()

    def finish():
        copy(0, (x, y, 1 - c), (x, y, c)).wait_recv()
        for j, chip in enumerate(chips):
            copy(4 + j, (*chip, 1 - c), (x, y, c)).wait_recv()
        for cp in first + passed:
            cp.wait_send()
        mine.wait()

    return start, forward, finish


def _all_to_all_steps(g_ref, out_ref, send_sems, recv_sems, local_sem):
    x, y, c, me = _mesh_position()
    mine = pltpu.make_async_copy(g_ref.at[me], out_ref.at[me], local_sem)
    copies = []
    for k in range(1, N_DEV):
        peer, peer_idx = _peer(x, y, c, k)
        copies.append(pltpu.make_async_remote_copy(
            src_ref=g_ref.at[peer_idx], dst_ref=out_ref.at[me], send_sem=send_sems.at[k - 1],
            recv_sem=recv_sems.at[k - 1], device_id=peer, device_id_type=MESH))

    def start():
        mine.start()
        for cp in copies:
            cp.start()

    def finish():
        for k in range(1, N_DEV):
            peer, peer_idx = _peer(x, y, c, k)
            pltpu.make_async_remote_copy(
                src_ref=g_ref.at[me], dst_ref=out_ref.at[peer_idx], send_sem=send_sems.at[k - 1],
                recv_sem=recv_sems.at[k - 1], device_id=peer, device_id_type=MESH).wait_recv()
        for cp in copies:
            cp.wait_send()
        mine.wait()

    return start, None, finish


EXCHANGE_STEPS = {"gather": _gather_steps, "all_to_all": _all_to_all_steps}
EXCHANGE_SCRATCH = [pltpu.SemaphoreType.DMA((7,)), pltpu.SemaphoreType.DMA((7,)), pltpu.SemaphoreType.DMA]
ANY_SPEC = pl.BlockSpec(memory_space=pl.ANY)


def _exchange_out_shape(kind, arr):
    return jax.ShapeDtypeStruct(((N_DEV,) + arr.shape) if kind == "gather" else arr.shape, arr.dtype)


def _exchange(kind, arrs, name):
    n = len(arrs)

    def body(*refs):
        steps = [EXCHANGE_STEPS[kind](refs[i], refs[n + i], *refs[2 * n + 3 * i:2 * n + 3 * i + 3]) for i in range(n)]
        for start, _, _ in steps:
            start()
        for _, forward, _ in steps:
            if forward is not None:
                forward()
        for _, _, finish in steps:
            finish()

    return pl.pallas_call(body, name=name, out_shape=[_exchange_out_shape(kind, a) for a in arrs],
                          in_specs=[ANY_SPEC] * n, out_specs=[ANY_SPEC] * n, scratch_shapes=EXCHANGE_SCRATCH * n)(*arrs)


def _run_hosted(hosted, src_refs, dst_refs, sem_refs, step, n_steps, when):
    for idx, (kind, _) in enumerate(hosted):
        start, forward, finish = EXCHANGE_STEPS[kind](src_refs[idx], dst_refs[idx], *sem_refs[3 * idx:3 * idx + 3])
        if when == "start":
            pl.when(step == 0)(start)
            if forward is not None:
                pl.when(step == (3 * n_steps) // 4)(forward)
        else:
            pl.when(step == n_steps - 1)(finish)


def _causal_iotas(qi, tq, tk):
    row = qi * tq + lax.broadcasted_iota(jnp.int32, (tq, tk), 0)
    col = lax.broadcasted_iota(jnp.int32, (tq, tk), 1)
    return row, col


def _suffix_matrix(tk, inclusive):
    j = lax.broadcasted_iota(jnp.int32, (2 * tk, tk), 0) % tk
    s = lax.broadcasted_iota(jnp.int32, (2 * tk, tk), 1)
    return jnp.where((j >= s) if inclusive else (j > s), 1.0, 0.0).astype(BF16)


def _suffix_sum(x, mat):
    hi = x.astype(BF16)
    lo = (x - hi.astype(F32)).astype(BF16)
    return lax.dot_general(jnp.concatenate([hi, lo], axis=1), mat, NN_DIMS, preferred_element_type=F32)


def _attn_specs(t_rows, tq, heads, dk, dv, q_off, k_off, v_off):
    q_spec = pl.BlockSpec((tq, dk), lambda h, i: (i, q_off + h))
    kt_spec = pl.BlockSpec((dk, t_rows), lambda h, i: (k_off + h, 0))
    v_spec = pl.BlockSpec((t_rows, dv), lambda h, i: (0, v_off + h))
    return q_spec, kt_spec, v_spec


def _split_weights(weights, fine):
    hi = weights.astype(BF16)
    return (hi, (weights - hi.astype(F32)).astype(BF16)) if fine else (hi,)


def _weighted_values(split, v):
    return sum(lax.dot_general(part, v, NN_DIMS, preferred_element_type=F32) for part in split)


def _attn_fwd(kind, q_arr, kt_arr, v_arr, heads, dk, dv, scale, name, q_off=0, k_off=0, v_off=0, fcol=None, frow=None,
              tq=ATTN_TQ, tk=ATTN_TK, fine=True, hosted=()):
    t_rows = q_arr.shape[0]
    tq, tk = min(tq, t_rows), min(tk, t_rows)
    nq = t_rows // tq
    stick = kind == "stick"
    decay = fcol is not None
    n_in = 5 if decay else 3
    n_out = 2 if stick else 3
    n_host = len(hosted)

    def body(*refs):
        q_ref, kt_ref, v_ref = refs[:3]
        fcol_ref, frow_ref = (refs[3], refs[4]) if decay else (None, None)
        base = n_in + n_host
        o_ref, fine_ref = refs[base], refs[base + 1]
        lse_ref = None if stick else refs[base + 2]
        host_args = (hosted, refs[n_in:base], refs[base + n_out:base + n_out + n_host], refs[base + n_out + n_host:],
                     pl.program_id(0) * nq + pl.program_id(1), heads * nq)
        _run_hosted(*host_args, "start")
        qi = pl.program_id(1)
        q = q_ref[...]
        row, col = _causal_iotas(qi, tq, tk)
        n_kb = ((qi + 1) * tq + tk - 1) // tk
        n_diag = max(1, tq // tk)
        zeros_o = jnp.zeros((tq, dv), F32)

        no_weights = (jnp.zeros((tq, tk), BF16),) * (2 if fine else 1)

        def raw_logits(kb):
            return lax.dot_general(q, kt_ref[:, pl.ds(pl.multiple_of(kb * tk, tk), tk)], NN_DIMS,
                                   preferred_element_type=F32)

        def values(kb):
            return v_ref[pl.ds(pl.multiple_of(kb * tk, tk), tk), :]

        if stick:
            mat = _suffix_matrix(tk, inclusive=False)

            def make_step(masked):
                def step(i, carry):
                    c, acc, raw, prev = carry
                    raw_next = raw_logits(jnp.maximum(n_kb - 2 - i, 0))
                    d_acc = _weighted_values(prev, values(jnp.minimum(n_kb - i, n_kb - 1)))
                    log_beta, lom = _log2_gates(raw * (scale * LOG2E))
                    if masked:
                        strict = (col + (n_kb - 1 - i) * tk) < row
                        lom = jnp.where(strict, lom, 0.0)
                    w = jnp.exp2(log_beta + (_suffix_sum(lom, mat) + c))
                    if masked:
                        w = jnp.where(strict, w, 0.0)
                    return c + jnp.sum(lom, axis=1, keepdims=True), acc + d_acc, raw_next, _split_weights(w, fine)
                return step

            carry = (jnp.zeros((tq, 1), F32), zeros_o, raw_logits(n_kb - 1), no_weights)
            for i in range(n_diag):
                carry = make_step(True)(i, carry)
            _, acc, _, last = lax.fori_loop(n_diag, n_kb, make_step(False), carry)
            acc = acc + _weighted_values(last, values(0))
            o_ref[...] = acc.astype(o_ref.dtype)
            fine_ref[...] = acc
        else:
            fc = fcol_ref[...] * LOG2E if decay else None

            def make_step(masked):
                def step(kb, carry):
                    m, l, acc, raw, prev = carry
                    raw_next = raw_logits(jnp.minimum(kb + 1, n_kb - 1))
                    d_acc = _weighted_values(prev, values(jnp.maximum(kb - 1, 0)))
                    ks = pl.multiple_of(kb * tk, tk)
                    s = raw * (scale * LOG2E)
                    if decay:
                        s = (s + fc) - frow_ref[:, pl.ds(ks, tk)] * LOG2E
                    if masked:
                        s = jnp.where((col + ks) <= row, s, MASKED)
                    m_new = jnp.maximum(m, jnp.max(s, axis=1, keepdims=True))
                    alpha = jnp.exp2(m - m_new)
                    p = jnp.exp2(s - m_new)
                    l = alpha * l + jnp.sum(p, axis=1, keepdims=True)
                    return m_new, l, alpha * (acc + d_acc), raw_next, _split_weights(p, fine)
                return step

            carry = lax.fori_loop(0, n_kb - n_diag, make_step(False),
                                  (jnp.full((tq, 1), MASKED, F32), jnp.zeros((tq, 1), F32), zeros_o, raw_logits(0),
                                   no_weights))
            for d in range(n_diag):
                carry = make_step(True)(n_kb - n_diag + d, carry)
            m, l, acc, _, last = carry
            out = (acc + _weighted_values(last, values(n_kb - 1))) * (1.0 / l)
            o_ref[...] = out.astype(o_ref.dtype)
            fine_ref[...] = out
            lse_ref[...] = (m + jnp.log2(l)) * (1.0 / LOG2E)
        _run_hosted(*host_args, "finish")

    q_spec, k_spec, v_spec = _attn_specs(t_rows, tq, heads, dk, dv, q_off, k_off, v_off)
    stat_spec = pl.BlockSpec((None, tq, 1), lambda h, i: (h, i, 0))
    ins, in_specs = [q_arr, kt_arr, v_arr], [q_spec, k_spec, v_spec]
    if decay:
        ins += [fcol, frow]
        in_specs += [stat_spec, pl.BlockSpec((None, 1, t_rows), lambda h, i: (h, 0, 0))]
    o_spec = pl.BlockSpec((tq, dv), lambda h, i: (i, h))
    out_specs = [o_spec, o_spec]
    out_shape = [jax.ShapeDtypeStruct((t_rows, heads * dv), BF16), jax.ShapeDtypeStruct((t_rows, heads * dv), F32)]
    if not stick:
        out_specs.append(stat_spec)
        out_shape.append(jax.ShapeDtypeStruct((heads, t_rows, 1), F32))
    return tuple(pl.pallas_call(
        body, name=name, grid=(heads, nq), in_specs=in_specs + [ANY_SPEC] * n_host,
        out_specs=out_specs + [ANY_SPEC] * n_host,
        out_shape=out_shape + [_exchange_out_shape(kd, arr) for kd, arr in hosted],
        scratch_shapes=EXCHANGE_SCRATCH * n_host,
        compiler_params=_params("arbitrary" if n_host else "parallel", "arbitrary"),
    )(*ins, *[arr for _, arr in hosted]))


def _attn_bwd(kind, q_arr, k_arr, kt_arr, vt_arr, o_arr, do_arr, heads, dk, dv, scale, name, q_off=0, k_off=0, kt_off=0,
              vt_off=0, do_off=0, lse=None, fcol=None, frow=None, tq=ATTN_TQ, tk=ATTN_TK, hosted=()):
    t_rows = q_arr.shape[0]
    tq, tk = min(tq, t_rows), min(tk, t_rows)
    nq = t_rows // tq
    stick = kind == "stick"
    decay = fcol is not None
    n_in = 6 + (0 if stick else 1) + (2 if decay else 0)
    n_out = 5 if decay else 3
    n_host = len(hosted)

    def body(*refs):
        q_ref, k_ref, kt_ref, vt_ref, o_ref, do_ref = refs[:6]
        lse_ref = None if stick else refs[6]
        fcol_ref, frow_ref = (refs[7], refs[8]) if decay else (None, None)
        base = n_in + n_host
        dq_ref, dk_ref, dv_ref = refs[base:base + 3]
        dfcol_ref, dfrow_ref = (refs[base + 3], refs[base + 4]) if decay else (None, None)
        host_args = (hosted, refs[n_in:base], refs[base + n_out:base + n_out + n_host], refs[base + n_out + n_host:],
                     pl.program_id(0) * nq + pl.program_id(1), heads * nq)
        _run_hosted(*host_args, "start")
        qi = pl.program_id(1)

        @pl.when(qi == 0)
        def _():
            dk_ref[...] = jnp.zeros_like(dk_ref)
            dv_ref[...] = jnp.zeros_like(dv_ref)
            if decay:
                dfrow_ref[...] = jnp.zeros_like(dfrow_ref)

        q = q_ref[...]
        do = do_ref[...]
        delta = jnp.sum(do.astype(F32) * o_ref[...], axis=1, keepdims=True)
        row, col = _causal_iotas(qi, tq, tk)
        n_kb = ((qi + 1) * tq + tk - 1) // tk
        n_diag = max(1, tq // tk)

        no_pair = (jnp.zeros((tq, tk), BF16), jnp.zeros((tq, tk), BF16))

        def accumulate(kb, pair):
            at = pl.ds(pl.multiple_of(kb * tk, tk), tk)
            dk_ref[at, :] += lax.dot_general(pair[0], q, TN_DIMS, preferred_element_type=F32)
            dv_ref[at, :] += lax.dot_general(pair[1], do, TN_DIMS, preferred_element_type=F32)
            return lax.dot_general(pair[0], k_ref[at, :], NN_DIMS, preferred_element_type=F32)

        def raw_logits(kb):
            return lax.dot_general(q, kt_ref[:, pl.ds(pl.multiple_of(kb * tk, tk), tk)], NN_DIMS,
                                   preferred_element_type=F32)

        def d_weights(kb):
            return lax.dot_general(do, vt_ref[:, pl.ds(pl.multiple_of(kb * tk, tk), tk)], NN_DIMS,
                                   preferred_element_type=F32)

        if stick:
            mat_ex = _suffix_matrix(tk, inclusive=False)
            mat_in = _suffix_matrix(tk, inclusive=True)

            def make_step(masked):
                def step(i, carry):
                    c, gs, dq, raw, prev = carry
                    raw_next = raw_logits(jnp.maximum(n_kb - 2 - i, 0))
                    dw = d_weights(n_kb - 1 - i)
                    dq = dq + accumulate(jnp.minimum(n_kb - i, n_kb - 1), prev)
                    log_beta, log_omb = _log2_gates(raw * (scale * LOG2E))
                    lom = log_omb
                    if masked:
                        strict = (col + (n_kb - 1 - i) * tk) < row
                        lom = jnp.where(strict, log_omb, 0.0)
                    w = jnp.exp2(log_beta + (_suffix_sum(lom, mat_ex) + c))
                    if masked:
                        w = jnp.where(strict, w, 0.0)
                    g = w * dw
                    g_before = delta - (gs + _suffix_sum(g, mat_in))
                    dz = g * jnp.exp2(log_omb) - g_before * jnp.exp2(log_beta)
                    if masked:
                        dz = jnp.where(strict, dz, 0.0)
                    return (c + jnp.sum(lom, axis=1, keepdims=True), gs + jnp.sum(g, axis=1, keepdims=True), dq,
                            raw_next, ((dz * scale).astype(BF16), w.astype(BF16)))
                return step

            zero = jnp.zeros((tq, 1), F32)
            carry = (zero, zero, jnp.zeros((tq, dk), F32), raw_logits(n_kb - 1), no_pair)
            for i in range(n_diag):
                carry = make_step(True)(i, carry)
            _, _, dq, _, last = lax.fori_loop(n_diag, n_kb, make_step(False), carry)
            dq = dq + accumulate(0, last)
        else:
            lse_v = lse_ref[...] * LOG2E
            fc = fcol_ref[...] * LOG2E if decay else None

            def make_step(masked):
                def step(kb, carry):
                    dq, row_sum, raw, prev = carry
                    raw_next = raw_logits(jnp.minimum(kb + 1, n_kb - 1))
                    dp = d_weights(kb)
                    dq = dq + accumulate(jnp.maximum(kb - 1, 0), prev)
                    ks = pl.multiple_of(kb * tk, tk)
                    s = raw * (scale * LOG2E)
                    if decay:
                        s = (s + fc) - frow_ref[:, pl.ds(ks, tk)] * LOG2E
                    p = jnp.exp2(s - lse_v)
                    if masked:
                        p = jnp.where((col + ks) <= row, p, 0.0)
                    ds = p * (dp - delta)
                    if decay:
                        dfrow_ref[:, pl.ds(ks, tk)] += jnp.sum(ds, axis=0, keepdims=True)
                        row_sum = row_sum + jnp.sum(ds, axis=1, keepdims=True)
                    return dq, row_sum, raw_next, ((ds * scale).astype(BF16), p.astype(BF16))
                return step

            carry = lax.fori_loop(0, n_kb - n_diag, make_step(False),
                                  (jnp.zeros((tq, dk), F32), jnp.zeros((tq, 1), F32), raw_logits(0), no_pair))
            for d in range(n_diag):
                carry = make_step(True)(n_kb - n_diag + d, carry)
            dq, row_sum, _, last = carry
            dq = dq + accumulate(n_kb - 1, last)
            if decay:
                dfcol_ref[...] = row_sum
        dq_ref[...] = dq
        _run_hosted(*host_args, "finish")

    q_spec, kt_spec, _ = _attn_specs(t_rows, tq, heads, dk, dv, q_off, kt_off, 0)
    stat_spec = pl.BlockSpec((None, tq, 1), lambda h, i: (h, i, 0))
    frow_spec = pl.BlockSpec((None, 1, t_rows), lambda h, i: (h, 0, 0))
    ins = [q_arr, k_arr, kt_arr, vt_arr, o_arr, do_arr]
    in_specs = [q_spec, pl.BlockSpec((t_rows, dk), lambda h, i: (0, k_off + h)), kt_spec,
                pl.BlockSpec((dv, t_rows), lambda h, i: (vt_off + h, 0)), pl.BlockSpec((tq, dv), lambda h, i: (i, h)),
                pl.BlockSpec((tq, dv), lambda h, i: (i, do_off + h))]
    if not stick:
        ins.append(lse)
        in_specs.append(stat_spec)
    if decay:
        ins += [fcol, frow]
        in_specs += [stat_spec, frow_spec]
    out_specs = [pl.BlockSpec((tq, dk), lambda h, i: (i, h)), pl.BlockSpec((t_rows, dk), lambda h, i: (0, h)),
                 pl.BlockSpec((t_rows, dv), lambda h, i: (0, h))]
    out_shape = [jax.ShapeDtypeStruct((t_rows, heads * dk), F32), jax.ShapeDtypeStruct((t_rows, heads * dk), F32),
                 jax.ShapeDtypeStruct((t_rows, heads * dv), F32)]
    if decay:
        out_specs += [stat_spec, frow_spec]
        out_shape += [jax.ShapeDtypeStruct((heads, t_rows, 1), F32), jax.ShapeDtypeStruct((heads, 1, t_rows), F32)]
    return pl.pallas_call(
        body, name=name, grid=(heads, nq), in_specs=in_specs + [ANY_SPEC] * n_host,
        out_specs=out_specs + [ANY_SPEC] * n_host,
        out_shape=out_shape + [_exchange_out_shape(kd, arr) for kd, arr in hosted],
        scratch_shapes=EXCHANGE_SCRATCH * n_host,
        compiler_params=_params("arbitrary" if n_host else "parallel", "arbitrary"),
    )(*ins, *[arr for _, arr in hosted])


def _prefix_matrix(reverse):
    j = lax.broadcasted_iota(jnp.int32, (LANES, LANES), 0)
    s = lax.broadcasted_iota(jnp.int32, (LANES, LANES), 1)
    return jnp.where((j >= s) if reverse else (j <= s), 1.0, 0.0).astype(BF16)


def _chunk_cumsum(x, mat):
    return sum(lax.dot_general(part, mat, NN_DIMS, preferred_element_type=F32) for part in _split3(x))


def _gate_fwd(logit_t, bias_col):
    heads, t_rows = logit_t.shape

    def body(x_ref, b_ref, out_ref):
        mat = _prefix_matrix(reverse=False)

        def step(ci, carry):
            cs = pl.multiple_of(ci * LANES, LANES)
            pre = x_ref[:, pl.ds(cs, LANES)] + b_ref[...]
            log_f = jnp.minimum(pre, 0.0) - _log_sigmoid_parts(pre)
            out_ref[:, pl.ds(cs, LANES)] = _chunk_cumsum(log_f, mat) + carry
            return carry + jnp.sum(log_f, axis=1, keepdims=True)

        lax.fori_loop(0, t_rows // LANES, step, jnp.zeros((heads, 1), F32))

    return pl.pallas_call(body, name="gate_fwd", out_shape=jax.ShapeDtypeStruct((heads, t_rows), F32),
                          compiler_params=pltpu.CompilerParams(vmem_limit_bytes=VMEM_LIMIT))(logit_t, bias_col)


def _gate_bwd(dcum_t, logit_t, bias_col):
    heads, t_rows = logit_t.shape
    n_chunks = t_rows // LANES

    def body(d_ref, x_ref, b_ref, dx_ref, db_ref):
        mat = _prefix_matrix(reverse=True)

        def step(i, carry):
            tail, db = carry
            cs = pl.multiple_of((n_chunks - 1 - i) * LANES, LANES)
            d = d_ref[:, pl.ds(cs, LANES)]
            d_log_f = _chunk_cumsum(d, mat) + tail
            pre = x_ref[:, pl.ds(cs, LANES)] + b_ref[...]
            e = jnp.exp(-jnp.abs(pre))
            d_pre = d_log_f * (jnp.where(pre >= 0.0, e, 1.0) / (1.0 + e))
            dx_ref[:, pl.ds(cs, LANES)] = d_pre
            return tail + jnp.sum(d, axis=1, keepdims=True), db + jnp.sum(d_pre, axis=1, keepdims=True)

        zero = jnp.zeros((heads, 1), F32)
        _, db = lax.fori_loop(0, n_chunks, step, (zero, zero))
        db_ref[...] = db

    return pl.pallas_call(body, name="gate_bwd",
                          out_shape=(jax.ShapeDtypeStruct((heads, t_rows), F32), jax.ShapeDtypeStruct((heads, 1), F32)),
                          compiler_params=pltpu.CompilerParams(vmem_limit_bytes=VMEM_LIMIT))(dcum_t, logit_t, bias_col)


def _norm_fwd(x, g, name):
    return _rows_call(lambda xv, gv: _rms_fwd(xv, gv), name, [x], [g], [(x.shape[1], BF16)])


def _norm_bwd(x, g, dh, dres, name):
    def fn(xv, dhv, dresv, gv):
        dx, dg = _rms_bwd(xv, gv, dhv)
        dx = dresv + dx
        return dx, dx, dg
    return _rows_call(fn, name, [x, dh, dres], [g], [(x.shape[1], F32), (x.shape[1], BF16)], [((1, x.shape[1]), F32)])


def _loss_fwd_bwd(y, target):
    d_model = y.shape[1]

    def fn(yv, tv):
        err = yv - tv
        dy = err * (1.0 / d_model)
        return dy, dy, jnp.sum(jnp.sum(err * err, axis=1, keepdims=True), axis=0, keepdims=True)
    return _rows_call(fn, "loss", [y, target], [], [(d_model, F32), (d_model, BF16)], [((1, 1), F32)])


def _heads_apply(fn, n_heads, width, *tiles):
    return [fn(*[t[:, h * width:(h + 1) * width] for t in tiles]) for h in range(n_heads)]


def _fox_norm_fwd(pb, gq, gk, heads):
    width = heads * HEAD_DIM

    def fn(qk, gqv, gkv):
        q = jnp.concatenate(_heads_apply(lambda t: _rms_fwd(t, gqv), heads, HEAD_DIM, qk[:, :width]), axis=1)
        k = jnp.concatenate(_heads_apply(lambda t: _rms_fwd(t, gkv), heads, HEAD_DIM, qk[:, width:]), axis=1)
        return q, k
    return _rows_call(fn, "fox_norm_fwd", [(pb, 2 * width, 0)], [gq, gk], [(width, BF16), (width, BF16)])


def _fox_norm_bwd(pb, gq, gk, dq, dk, heads):
    width = heads * HEAD_DIM

    def fn(qk, dqv, dkv, gqv, gkv):
        res_q = _heads_apply(lambda t, d: _rms_bwd(t, gqv, d), heads, HEAD_DIM, qk[:, :width], dqv)
        res_k = _heads_apply(lambda t, d: _rms_bwd(t, gkv, d), heads, HEAD_DIM, qk[:, width:], dkv)
        dqk = jnp.concatenate([r[0] for r in res_q] + [r[0] for r in res_k], axis=1)
        return dqk, sum(r[1] for r in res_q), sum(r[1] for r in res_k)
    return _rows_call(fn, "fox_norm_bwd", [(pb, 2 * width, 0), dq, dk], [gq, gk], [(2 * width, BF16)],
                      [((1, HEAD_DIM), F32), ((1, HEAD_DIM), F32)])


def _lora_norm_fwd(down, gq, gkv, rank):
    def fn(dv, gqv, gkvv):
        return _rms_fwd(dv[:, :rank], gqv), _rms_fwd(dv[:, rank:], gkvv)
    return _rows_call(fn, "lora_norm_fwd", [(down, 2 * rank, 0)], [gq, gkv], [(rank, BF16), (rank, BF16)])


def _lora_norm_bwd(down, gq, gkv, dcq, dckv, dkpe, rank):
    def fn(dv, dcqv, dckvv, dkpev, gqv, gkvv):
        dxq, dgq = _rms_bwd(dv[:, :rank], gqv, dcqv)
        dxkv, dgkv = _rms_bwd(dv[:, rank:], gkvv, dckvv)
        return jnp.concatenate([dxq, dxkv, dkpev], axis=1), dgq, dgkv
    return _rows_call(fn, "lora_norm_bwd", [(down, 2 * rank, 0), dcq, dckv, dkpe], [gq, gkv],
                      [(2 * rank + LANES, BF16)], [((1, rank), F32), ((1, rank), F32)])


def _rope_tables(pos_col, inv_freq, sin_sign):
    def fn(pos, invf, sign):
        ang = pos.astype(F32) * invf
        return jnp.cos(ang) * jnp.abs(sign), jnp.sin(ang) * sign
    return _rows_call(fn, "rope_tables", [pos_col], [inv_freq, sin_sign], [(LANES, F32), (LANES, F32)])


def _mla_prep_fwd(q_raw, kv, down, kpe_block, qg, kg, cos_t, sin_s):
    def fn(qv, kvv, kpe, cosv, sinv, qgv, kgv):
        qs, ks, vs = [], [], []
        for h in range(MLA_HEADS):
            qn = _rms_fwd(qv[:, h * MLA_PAD_DIM:(h + 1) * MLA_PAD_DIM], qgv, MLA_QK_DIM)
            qs += [qn[:, :HEAD_DIM], _rope_fwd(qn[:, HEAD_DIM:], cosv, sinv)]
            k_full = jnp.concatenate([kvv[:, h * MLA_PAD_DIM:h * MLA_PAD_DIM + HEAD_DIM], kpe], axis=1)
            kn = _rms_fwd(k_full, kgv, MLA_QK_DIM)
            ks += [kn[:, :HEAD_DIM], _rope_fwd(kn[:, HEAD_DIM:], cosv, sinv)]
            vs.append(kvv[:, h * MLA_PAD_DIM + HEAD_DIM:(h + 1) * MLA_PAD_DIM])
        return jnp.concatenate(qs, axis=1), jnp.concatenate(ks, axis=1), jnp.concatenate(vs, axis=1)
    wide = MLA_HEADS * MLA_PAD_DIM
    return _rows_call(fn, "mla_prep_fwd", [q_raw, kv, (down, LANES, kpe_block), cos_t, sin_s], [qg, kg],
                      [(wide, BF16), (wide, BF16), (MLA_HEADS * HEAD_DIM, BF16)], tile=128)


def _mla_prep_bwd(q_raw, kv, down, kpe_block, qg, kg, cos_t, sin_s, dq, dk, dv):
    def fn(qv, kvv, kpe, cosv, sinv, dqv, dkv, dvv, qgv, kgv):
        dqs, dkvs = [], []
        dkpe = jnp.zeros_like(kpe)
        dqg = jnp.zeros_like(qgv)
        dkg = jnp.zeros_like(kgv)
        for h in range(MLA_HEADS):
            lo, hi = h * MLA_PAD_DIM, (h + 1) * MLA_PAD_DIM
            dqn = jnp.concatenate([dqv[:, lo:lo + HEAD_DIM], _rope_bwd(dqv[:, lo + HEAD_DIM:hi], cosv, sinv)], axis=1)
            dqh, dg = _rms_bwd(qv[:, lo:hi], qgv, dqn, MLA_QK_DIM)
            dqs.append(dqh)
            dqg = dqg + dg
            k_full = jnp.concatenate([kvv[:, lo:lo + HEAD_DIM], kpe], axis=1)
            dkn = jnp.concatenate([dkv[:, lo:lo + HEAD_DIM], _rope_bwd(dkv[:, lo + HEAD_DIM:hi], cosv, sinv)], axis=1)
            dkh, dg = _rms_bwd(k_full, kgv, dkn, MLA_QK_DIM)
            dkg = dkg + dg
            dkpe = dkpe + dkh[:, HEAD_DIM:]
            dkvs += [dkh[:, :HEAD_DIM], dvv[:, h * HEAD_DIM:(h + 1) * HEAD_DIM]]
        return jnp.concatenate(dqs, axis=1), jnp.concatenate(dkvs, axis=1), dkpe, dqg, dkg
    wide = MLA_HEADS * MLA_PAD_DIM
    return _rows_call(fn, "mla_prep_bwd", [q_raw, kv, (down, LANES, kpe_block), cos_t, sin_s, dq, dk, dv], [qg, kg],
                      [(wide, BF16), (wide, BF16), (LANES, F32)], [((1, MLA_PAD_DIM), F32), ((1, MLA_PAD_DIM), F32)],
                      tile=128)


def _sqrelu_up(acc):
    return acc, jnp.square(jnp.maximum(acc, 0.0))


def _sqrelu_grad(acc, u):
    return (acc * (2.0 * jnp.maximum(u, 0.0)),)


def _mlp_fwd(x, g, w_up, w_down, tag):
    h = _norm_fwd(x, g, f"mlp_norm_fwd{tag}")
    u, a = _matmul(h, w_up, "nn", f"mlp_up{tag}", (F32, BF16), _sqrelu_up)
    return _matmul(a, w_down, "nn", f"mlp_down{tag}", (F32,), _add_residual, (x,)), (h, u, a)


def _mlp_bwd(x, g, w_up, w_down, saved, dy, dy16, tag):
    h, u, a = saved
    dw_down = _matmul(a, dy16, "tn", f"mlp_dwdown{tag}", (BF16,))
    du = _matmul(dy16, w_down, "nt", f"mlp_du{tag}", (BF16,), _sqrelu_grad, (u,))
    dw_up = _matmul(h, du, "tn", f"mlp_dwup{tag}", (BF16,))
    dh = _matmul(du, w_up, "nt", f"mlp_dh{tag}")
    dx, dx16, dg = _norm_bwd(x, g, dh, dy, f"mlp_norm_bwd{tag}")
    return dx, dx16, dg, dw_up, dw_down


def _local_step(x, pos_col, target, w, dist=None):
    w = dict(w)
    hs = w["w_a"].shape[1] // (4 * HEAD_DIM)
    sb_w = hs * HEAD_DIM
    grads, received = {}, {}

    def gather_in(group):
        return [("gather", blk) for blk in dist["blocks"][group]] if dist else []

    def exchange_in(group):
        return [("all_to_all", slots) for slots in dist["slots_of"](group, grads)] if dist else []

    h0 = _norm_fwd(x, w["ln_mix0"], "mix0_norm_fwd")
    pa = _matmul(h0, w["w_a"], "nn", "in_proj_a", (BF16,))
    pb = _matmul(h0, w["w_b"], "nn", "in_proj_b")
    pat = pa[:, sb_w:].T
    o_sb, o_sb_fine, *got = _attn_fwd("stick", pa, pat, pa, hs, HEAD_DIM, HEAD_DIM, HEAD_DIM ** -0.5, "stick_fwd",
                                      q_off=0, k_off=0, v_off=2 * hs, tq=ATTN_TQ_WIDE, hosted=gather_in("mlp0"))
    if dist:
        w.update(dist["weights_of"]("mlp0", got))
    logit_t = pb[:, 2 * sb_w:2 * sb_w + hs].T
    bias_col = w["b_f"][0, :hs].reshape(hs, 1)
    f_cum = _gate_fwd(logit_t, bias_col)
    f_col, f_row = f_cum[:, :, None], f_cum[:, None, :]
    qf, kf = _fox_norm_fwd(pb, w["fox_q_g"], w["fox_k_g"], hs)
    kft = kf.T
    o_fx, o_fx_fine, lse_fx, *got = _attn_fwd("softmax", qf, kft, pa, hs, HEAD_DIM, HEAD_DIM, HEAD_DIM ** -0.5,
                                              "fox_fwd", v_off=3 * hs, fcol=f_col, frow=f_row, tq=ATTN_TQ_WIDE, tk=ATTN_TK_SOFTMAX,
                                              fine=False,
                                              hosted=gather_in("mla"))
    if dist:
        w.update(dist["weights_of"]("mla", got))
    o0 = jnp.concatenate([o_sb, o_fx], axis=1)
    x1 = _matmul(o0, w["w_o0"], "nn", "out_proj0", (F32,), _add_residual, (x,))
    x2, mlp0 = _mlp_fwd(x1, w["ln_mlp0"], w["w_up0"], w["w_dn0"], "0")

    rank = w["w_uq"].shape[0]
    h2 = _norm_fwd(x2, w["ln_mix1"], "mix1_norm_fwd")
    down = _matmul(h2, w["w_down"], "nn", "mla_down")
    cqn, ckvn = _lora_norm_fwd(down, w["q_a_g"], w["kv_a_g"], rank)
    q_raw = _matmul(cqn, w["w_uq"], "nn", "mla_uq")
    kv = _matmul(ckvn, w["w_ukv"], "nn", "mla_ukv")
    cos_t, sin_s = _rope_tables(pos_col, w["inv_freq"], w["sin_sign"])
    kpe_block = 2 * rank // LANES
    qm, km, vm = _mla_prep_fwd(q_raw, kv, down, kpe_block, w["mla_q_g"], w["mla_k_g"], cos_t, sin_s)
    kmt, vmt = km.T, vm.T
    o_m, o_m_fine, lse_m, *got = _attn_fwd("softmax", qm, kmt, vm, MLA_HEADS, MLA_PAD_DIM, HEAD_DIM,
                                           MLA_QK_DIM ** -0.5, "mla_fwd", tq=ATTN_TQ_WIDE, tk=ATTN_TK_SOFTMAX,
                                           fine=False, hosted=gather_in("mlp1"))
    if dist:
        w.update(dist["weights_of"]("mlp1", got))
    x3 = _matmul(o_m, w["w_o1"], "nn", "out_proj1", (F32,), _add_residual, (x2,))
    x4, mlp1 = _mlp_fwd(x3, w["ln_mlp1"], w["w_up1"], w["w_dn1"], "1")

    dy, dy16, sq_err = _loss_fwd_bwd(x4, target)

    dx3, dx3_16, grads["ln_mlp1"], grads["w_up1"], grads["w_dn1"] = _mlp_bwd(
        x3, w["ln_mlp1"], w["w_up1"], w["w_dn1"], mlp1, dy, dy16, "1")
    grads["w_o1"] = _matmul(o_m, dx3_16, "tn", "dw_o1", (BF16,))
    do_m = _matmul(dx3_16, w["w_o1"], "nt", "do_mla", (BF16,))
    dqm, dkm, dvm, *got = _attn_bwd("softmax", qm, km, kmt, vmt, o_m_fine, do_m, MLA_HEADS, MLA_PAD_DIM, HEAD_DIM,
                                    MLA_QK_DIM ** -0.5, "mla_bwd", lse=lse_m, tq=ATTN_TQ_WIDE, tk=ATTN_TK_SOFTMAX,
                                    hosted=exchange_in("mlp1"))
    received["mlp1"] = got
    dq_raw, dkv, dkpe, grads["mla_q_g"], grads["mla_k_g"] = _mla_prep_bwd(
        q_raw, kv, down, kpe_block, w["mla_q_g"], w["mla_k_g"], cos_t, sin_s, dqm, dkm, dvm)
    grads["w_uq"] = _matmul(cqn, dq_raw, "tn", "dw_uq", (BF16,))
    grads["w_ukv"] = _matmul(ckvn, dkv, "tn", "dw_ukv", (BF16,))
    dcqn = _matmul(dq_raw, w["w_uq"], "nt", "d_cq")
    dckvn = _matmul(dkv, w["w_ukv"], "nt", "d_ckv")
    ddown, grads["q_a_g"], grads["kv_a_g"] = _lora_norm_bwd(down, w["q_a_g"], w["kv_a_g"], dcqn, dckvn, dkpe, rank)
    grads["w_down"] = _matmul(h2, ddown, "tn", "dw_down", (BF16,))
    dh2 = _matmul(ddown, w["w_down"], "nt", "d_h2")
    dx2, dx2_16, grads["ln_mix1"] = _norm_bwd(x2, w["ln_mix1"], dh2, dx3, "mix1_norm_bwd")

    dx1, dx1_16, grads["ln_mlp0"], grads["w_up0"], grads["w_dn0"] = _mlp_bwd(
        x1, w["ln_mlp0"], w["w_up0"], w["w_dn0"], mlp0, dx2, dx2_16, "0")
    grads["w_o0"] = _matmul(o0, dx1_16, "tn", "dw_o0", (BF16,))
    do0 = _matmul(dx1_16, w["w_o0"], "nt", "do_mix0", (BF16,))
    dq_sb, dk_sb, dv_sb, *got = _attn_bwd("stick", pa, pa, pat, pat, o_sb_fine, do0, hs, HEAD_DIM, HEAD_DIM,
                                          HEAD_DIM ** -0.5, "stick_bwd", q_off=0, k_off=hs, kt_off=0, vt_off=hs, do_off=0,
                                          tq=ATTN_TQ_WIDE,
                                          hosted=exchange_in("with_stick_bwd"))
    received["with_stick_bwd"] = got
    dqf, dkf, dv_fx, ds_rows, ds_cols, *got = _attn_bwd(
        "softmax", qf, kf, kft, pat, o_fx_fine, do0, hs, HEAD_DIM, HEAD_DIM, HEAD_DIM ** -0.5, "fox_bwd", vt_off=2 * hs,
        tq=ATTN_TQ_WIDE, tk=ATTN_TK_SOFTMAX,
        do_off=hs, lse=lse_fx, fcol=f_col, frow=f_row, hosted=exchange_in("with_fox_bwd"))
    received["with_fox_bwd"] = got
    dqk_fx, grads["fox_q_g"], grads["fox_k_g"] = _fox_norm_bwd(pb, w["fox_q_g"], w["fox_k_g"], dqf, dkf, hs)
    dlogit_t, db_f = _gate_bwd(ds_rows[:, :, 0] - ds_cols[:, 0, :], logit_t, bias_col)
    grads["b_f"] = db_f.reshape(1, hs)
    dpa = jnp.concatenate([dq_sb.astype(BF16), dk_sb.astype(BF16), dv_sb.astype(BF16), dv_fx.astype(BF16)], axis=1)
    dlogit_pad = jnp.pad(dlogit_t.T.astype(BF16), ((0, 0), (0, pb.shape[1] - 2 * sb_w - hs)))
    dpb = jnp.concatenate([dqk_fx, dlogit_pad], axis=1)
    grads["w_a"] = _matmul(h0, dpa, "tn", "dw_a", (BF16,))
    grads["w_b"] = _matmul(h0, dpb, "tn", "dw_b", (BF16,))
    dh0 = _matmul(dpb, w["w_b"], "nt", "d_h0_b")
    res = _matmul(dpa, w["w_a"], "nt", "d_h0_a", (F32,), _add_residual, (dh0,), hosted=exchange_in("mix0"))
    dh0, received["mix0"] = (res[0], list(res[1:])) if dist else (res, [])
    grad_x, _, grads["ln_mix0"] = _norm_bwd(x, w["ln_mix0"], dh0, dx1, "mix0_norm_bwd")
    return sq_err, grad_x, grads, received


PIECES = {
    "sf_w_in": ("sf_w_in", 0, 1), "sf_w_o": ("sf_w_o", 0, 0), "mla_w_down": ("mla_w_down", 0, 0),
    "mla_w_uq": ("mla_w_uq", 0, 1), "mla_w_ukv": ("mla_w_ukv", 0, 1), "mla_w_o": ("mla_w_o", 0, 0),
    "mlp_w_up0": ("mlp_w_up", 0, 1), "mlp_w_up1": ("mlp_w_up", 1, 1),
    "mlp_w_down0": ("mlp_w_down", 0, 0), "mlp_w_down1": ("mlp_w_down", 1, 0),
}
GROUPS = {
    "mix0": ["sf_w_in"], "mlp0": ["sf_w_o", "mlp_w_up0", "mlp_w_down0"],
    "mla": ["mla_w_down", "mla_w_uq", "mla_w_ukv", "mla_w_o"], "mlp1": ["mlp_w_up1", "mlp_w_down1"],
}
GROUPS["with_stick_bwd"] = GROUPS["mla"] + ["mlp_w_up0"]
GROUPS["with_fox_bwd"] = ["mlp_w_down0", "sf_w_o"]
SMALL = ["ln_mix_g", "ln_mlp_g", "sf_b_f", "fox_q_g", "fox_k_g", "mla_q_a_g", "mla_kv_a_g", "mla_q_g", "mla_k_g"]
ALL_W = ["ln_mix_g", "ln_mlp_g", "sf_w_in", "sf_b_f", "fox_q_g", "fox_k_g", "sf_w_o", "mla_w_down", "mla_q_a_g",
         "mla_kv_a_g", "mla_w_uq", "mla_w_ukv", "mla_q_g", "mla_k_g", "mla_w_o", "mlp_w_up", "mlp_w_down"]


def _weights_mix0(full, small):
    w_in = full["sf_w_in"]
    d_model = w_in.shape[0]
    n_fx = small["sf_b_f"].shape[1]
    sb_w = (w_in.shape[1] - n_fx) // 6
    cols = lambda i: w_in[:, i * sb_w:(i + 1) * sb_w]
    w_a = jnp.concatenate([cols(0), cols(1), cols(2), cols(5)], axis=1)
    w_b = jnp.concatenate([cols(3), cols(4), w_in[:, 6 * sb_w:], jnp.zeros((d_model, LANES - n_fx), w_in.dtype)], axis=1)
    half = ROPE_DIM // 2
    inv_freq = ROPE_THETA ** (-jnp.arange(half, dtype=F32) / half)
    zeros64 = jnp.zeros((ROPE_DIM,), F32)
    pad256 = lambda g: jnp.pad(g, ((0, 0), (0, MLA_PAD_DIM - MLA_QK_DIM)))
    pad_lanes = lambda g: jnp.pad(g, ((0, 0), (0, LANES - g.shape[1])))
    return dict(
        ln_mix0=small["ln_mix_g"][0:1], ln_mix1=small["ln_mix_g"][1:2],
        ln_mlp0=small["ln_mlp_g"][0:1], ln_mlp1=small["ln_mlp_g"][1:2],
        w_a=w_a, w_b=w_b, b_f=pad_lanes(small["sf_b_f"]), fox_q_g=small["fox_q_g"], fox_k_g=small["fox_k_g"],
        q_a_g=small["mla_q_a_g"], kv_a_g=small["mla_kv_a_g"],
        mla_q_g=pad256(small["mla_q_g"]), mla_k_g=pad256(small["mla_k_g"]),
        inv_freq=jnp.concatenate([inv_freq, inv_freq, zeros64]).reshape(1, LANES),
        sin_sign=jnp.concatenate([-jnp.ones((half,), F32), jnp.ones((half,), F32), zeros64]).reshape(1, LANES),
    )


def _weights_mlp0(full):
    return dict(w_o0=full["sf_w_o"], w_up0=full["mlp_w_up0"], w_dn0=full["mlp_w_down0"])


def _weights_mla(full):
    rank = full["mla_w_uq"].shape[0]
    w_uq = full["mla_w_uq"].reshape(rank, MLA_HEADS, MLA_QK_DIM)
    w_uq = jnp.pad(w_uq, ((0, 0), (0, 0), (0, MLA_PAD_DIM - MLA_QK_DIM))).reshape(rank, MLA_HEADS * MLA_PAD_DIM)
    return dict(w_down=jnp.pad(full["mla_w_down"], ((0, 0), (0, LANES - ROPE_DIM))), w_uq=w_uq,
                w_ukv=full["mla_w_ukv"], w_o1=full["mla_w_o"])


def _weights_mlp1(full):
    return dict(w_up1=full["mlp_w_up1"], w_dn1=full["mlp_w_down1"])


WEIGHTS_OF = {"mlp0": _weights_mlp0, "mla": _weights_mla, "mlp1": _weights_mlp1}


def _piece_grad(g, piece):
    if piece == "sf_w_in":
        n_fx = g["b_f"].shape[1]
        ga, gb = g["w_a"], g["w_b"]
        sb_w = ga.shape[1] // 4
        ca = lambda i: ga[:, i * sb_w:(i + 1) * sb_w]
        return jnp.concatenate([ca(0), ca(1), ca(2), gb[:, :sb_w], gb[:, sb_w:2 * sb_w], ca(3),
                                gb[:, 2 * sb_w:2 * sb_w + n_fx]], axis=1)
    if piece == "mla_w_uq":
        rank = g["w_uq"].shape[0]
        return g["w_uq"].reshape(rank, MLA_HEADS, MLA_PAD_DIM)[:, :, :MLA_QK_DIM].reshape(rank, MLA_HEADS * MLA_QK_DIM)
    if piece == "mla_w_down":
        return g["w_down"][:, :g["w_down"].shape[1] - (LANES - ROPE_DIM)]
    return g[{"sf_w_o": "w_o0", "mla_w_ukv": "w_ukv", "mla_w_o": "w_o1", "mlp_w_up0": "w_up0", "mlp_w_up1": "w_up1",
              "mlp_w_down0": "w_dn0", "mlp_w_down1": "w_dn1"}[piece]]


def _small_grads(g):
    return {
        "ln_mix_g": jnp.concatenate([g["ln_mix0"], g["ln_mix1"]], axis=0),
        "ln_mlp_g": jnp.concatenate([g["ln_mlp0"], g["ln_mlp1"]], axis=0),
        "sf_b_f": g["b_f"], "fox_q_g": g["fox_q_g"], "fox_k_g": g["fox_k_g"],
        "mla_q_a_g": g["q_a_g"], "mla_kv_a_g": g["kv_a_g"],
        "mla_q_g": g["mla_q_g"][:, :MLA_QK_DIM], "mla_k_g": g["mla_k_g"][:, :MLA_QK_DIM],
    }


PACK_TILE = 1024


def _as_rows(a, row_multiple=16):
    flat = a.reshape(-1)
    rows = -(-flat.shape[0] // LANES)
    rows = -(-rows // row_multiple) * row_multiple
    return jnp.pad(flat, (0, rows * LANES - flat.shape[0])).reshape(rows, LANES)


def _pack_rows(parts, axis, dtype, row_multiple=PACK_TILE, spare_rows=0):
    used = sum(p.shape[axis] for p in parts)
    shape = list(parts[0].shape)
    shape[axis] = -(-used // row_multiple) * row_multiple + spare_rows - used
    return jnp.concatenate([p.astype(dtype) for p in parts] + [jnp.ones(shape, dtype)], axis=axis)


def _unshard(stack, axis):
    moved = jnp.moveaxis(stack, 0, axis)
    shape = list(stack.shape[1:])
    shape[axis] *= N_DEV
    return moved.reshape(shape)


def _shard_stack(full, axis):
    shape = list(full.shape)
    shape[axis:axis + 1] = [N_DEV, shape[axis] // N_DEV]
    return jnp.moveaxis(full.reshape(shape), axis, 0)


OPT_TILE_ELEMS = 128 * 1024


def _row_tile(rows, cols):
    best = 16
    for t in range(16, rows + 1, 16):
        if rows % t == 0 and t * cols <= OPT_TILE_ELEMS:
            best = t
    assert rows % best == 0
    return best


def _cast_bf16(a, name):
    return _rows_call(lambda v: v, name, [a], [], [(a.shape[1], BF16)], tile=_row_tile(*a.shape))


def _adam_math(w, g, m, v):
    m = ADAM_B1 * m + (1.0 - ADAM_B1) * g
    v = ADAM_B2 * v + (1.0 - ADAM_B2) * jnp.square(g)
    m_hat = m / (1.0 - ADAM_B1 ** ADAM_STEP)
    v_hat = v / (1.0 - ADAM_B2 ** ADAM_STEP)
    delta = -ADAM_LR * (m_hat / (jnp.sqrt(v_hat) + ADAM_EPS) + ADAM_WD * w)
    return delta, m, v


def _adam_big(recvs, w, m, v, name, hosted=()):
    layers, rows, cols = w.shape
    tile = _row_tile(rows, cols)
    n_tiles = rows // tile
    n_host = len(hosted)

    def body(*refs):
        recv_refs = refs[:layers]
        w_ref, m_ref, v_ref = refs[layers:layers + 3]
        base = layers + 3 + n_host
        g_ref, d_ref, nm_ref, nv_ref = refs[base:base + 4]
        layer = pl.program_id(0)
        host_args = (hosted, refs[layers + 3:base], refs[base + 4:base + 4 + n_host], refs[base + 4 + n_host:],
                     layer * n_tiles + pl.program_id(1), layers * n_tiles)
        _run_hosted(*host_args, "start")

        def total(r_ref):
            acc = r_ref[0].astype(F32)
            for s in range(1, N_DEV):
                acc = acc + r_ref[s].astype(F32)
            return acc

        g = total(recv_refs[0])
        for j in range(1, layers):
            g = jnp.where(layer == j, total(recv_refs[j]), g)
        delta, nm, nv = _adam_math(w_ref[...], g, m_ref[...], v_ref[...])
        g_ref[...] = g
        d_ref[...] = delta
        nm_ref[...] = nm
        nv_ref[...] = nv
        _run_hosted(*host_args, "finish")

    def recv_spec(j):
        return pl.BlockSpec((N_DEV, tile, cols),
                            lambda l, i: (0, jnp.where(l == j, i, jnp.where(l < j, 0, n_tiles - 1)), 0))

    spec = pl.BlockSpec((None, tile, cols), lambda l, i: (l, i, 0))
    out = jax.ShapeDtypeStruct(w.shape, F32)
    return pl.pallas_call(
        body, name=name, grid=(layers, n_tiles),
        in_specs=[recv_spec(j) for j in range(layers)] + [spec] * 3 + [ANY_SPEC] * n_host,
        out_specs=[spec] * 4 + [ANY_SPEC] * n_host,
        out_shape=[out] * 4 + [_exchange_out_shape(kd, arr) for kd, arr in hosted],
        scratch_shapes=EXCHANGE_SCRATCH * n_host, compiler_params=_params("arbitrary", "arbitrary"),
    )(*recvs, w, m, v, *[arr for _, arr in hosted])


def _sum_slots(gathered):
    rows = gathered.shape[1]

    def body(r_ref, o_ref):
        acc = r_ref[0]
        for s in range(1, N_DEV):
            acc = acc + r_ref[s]
        o_ref[...] = acc

    return pl.pallas_call(body, name="sum_small", out_shape=jax.ShapeDtypeStruct((rows, LANES), F32))(gathered)


def _adam_small(w, g, m, v):
    def fn(wv, gv, mv, vv):
        return _adam_math(wv, gv, mv, vv)
    return _rows_call(fn, "adam_small", [w, g, m, v], [], [(LANES, F32)] * 3, tile=w.shape[0])


def kernel(x, positions, ln_mix_g, ln_mlp_g, sf_w_in, sf_b_f, fox_q_g, fox_k_g, sf_w_o, mla_w_down, mla_q_a_g, mla_kv_a_g, mla_w_uq, mla_w_ukv, mla_q_g, mla_k_g, mla_w_o, mlp_w_up, mlp_w_down, loss_target, m_ln_mix_g, m_ln_mlp_g, m_sf_w_in, m_sf_b_f, m_fox_q_g, m_fox_k_g, m_sf_w_o, m_mla_w_down, m_mla_q_a_g, m_mla_kv_a_g, m_mla_w_uq, m_mla_w_ukv, m_mla_q_g, m_mla_k_g, m_mla_w_o, m_mlp_w_up, m_mlp_w_down, v_ln_mix_g, v_ln_mlp_g, v_sf_w_in, v_sf_b_f, v_fox_q_g, v_fox_k_g, v_sf_w_o, v_mla_w_down, v_mla_q_a_g, v_mla_kv_a_g, v_mla_w_uq, v_mla_w_ukv, v_mla_q_g, v_mla_k_g, v_mla_w_o, v_mlp_w_up, v_mlp_w_down):
    given = dict(locals())
    wts = {n: given[n] for n in ALL_W}
    mom = {n: given["m_" + n] for n in ALL_W}
    var = {n: given["v_" + n] for n in ALL_W}
    me = 4 * lax.axis_index("x") + 2 * lax.axis_index("y") + lax.axis_index("c")
    t_rows, d_model = x.shape[1], x.shape[2]
    big = sorted({name for name, _, _ in PIECES.values()})

    def whole_pieces(gathered, group):
        return {p: _unshard(s, PIECES[p][2]) for p, s in zip(GROUPS[group], gathered)}

    def w_in_slot_parts(grads):
        slots = _shard_stack(_piece_grad(grads, "sf_w_in"), PIECES["sf_w_in"][2])
        cuts = [0] + [(slots.shape[1] * f // 16) // 16 * 16 for f in (6, 11)] + [slots.shape[1]]
        return [slots[:, a:b] for a, b in zip(cuts, cuts[1:])]

    def slots_of(group, grads):
        if group == "mix0":
            return w_in_slot_parts(grads)[:1]
        return [_shard_stack(_piece_grad(grads, p), PIECES[p][2]) for p in GROUPS[group]]

    cast = {n: _cast_bf16(wts[n].reshape(-1, wts[n].shape[2]), f"cast_{n}").reshape(wts[n].shape) for n in big}
    blocks = {grp: [cast[PIECES[p][0]][PIECES[p][1]] for p in GROUPS[grp]] for grp in ("mix0", "mlp0", "mla", "mlp1")}
    mix0 = whole_pieces(_exchange("gather", blocks["mix0"], "gather_mix0"), "mix0")
    gains, = _exchange("gather", [_as_rows(jnp.concatenate([mla_q_a_g, mla_kv_a_g], axis=1))], "gather_gains")
    lora_n = mla_q_a_g.shape[1]
    gains_flat = gains.reshape(N_DEV, -1)[:, :2 * lora_n]
    small = dict(ln_mix_g=ln_mix_g, ln_mlp_g=ln_mlp_g, sf_b_f=sf_b_f, fox_q_g=fox_q_g, fox_k_g=fox_k_g,
                 mla_q_a_g=gains_flat[:, :lora_n].reshape(1, -1), mla_kv_a_g=gains_flat[:, lora_n:].reshape(1, -1),
                 mla_q_g=mla_q_g, mla_k_g=mla_k_g)
    dist = dict(blocks=blocks, slots_of=slots_of,
                weights_of=lambda grp, gathered: WEIGHTS_OF[grp](whole_pieces(gathered, grp)))
    sq_err, grad_x, g, received = _local_step(x[0], positions.reshape(t_rows, 1), loss_target[0],
                                              _weights_mix0(mix0, small), dist)

    recv_of = {p: r for grp in ("mlp1", "with_stick_bwd", "with_fox_bwd") for p, r in zip(GROUPS[grp], received[grp])}
    late_parts = dict(zip(("mlp_w_down", "mlp_w_up"), w_in_slot_parts(g)[1:]))
    w_in_recv = list(received["mix0"])
    results = {kind: {} for kind in ("grad", "delta", "new_m", "new_v")}
    for n in sorted(big, key=lambda name: (name == "sf_w_in", name not in late_parts)):
        if n == "sf_w_in":
            recv_of["sf_w_in"] = jnp.concatenate(w_in_recv, axis=1)
        layers = [p for _, p in sorted((layer, p) for p, (name, layer, _) in PIECES.items() if name == n)]
        hosted = [("all_to_all", late_parts[n])] if n in late_parts else []
        outs = _adam_big([recv_of[p] for p in layers], wts[n], mom[n], var[n], f"adam_{n}", hosted)
        w_in_recv += outs[4:]
        for kind, out in zip(("grad", "delta", "new_m", "new_v"), outs[:4]):
            results[kind][n] = out

    small_g = _small_grads(g)
    small_parts = [_as_rows(small_g[n], 8) for n in SMALL] + [_as_rows(sq_err, 8)]
    small_sum = _sum_slots(_exchange("gather", [_pack_rows(small_parts, 0, F32, 8, 8)], "gather_small_grads")[0])
    red, off = {}, 0
    for n, p in zip(SMALL + ["loss"], small_parts):
        red[n] = small_sum[off:off + p.shape[0]].reshape(-1)
        off += p.shape[0]
    loss = 0.5 * red["loss"][0] / d_model
    for n in SMALL:
        if n in ("mla_q_a_g", "mla_kv_a_g"):
            results["grad"][n] = lax.dynamic_slice(red[n], (me * lora_n,), (lora_n,)).reshape(wts[n].shape)
        else:
            results["grad"][n] = red[n][:wts[n].size].reshape(wts[n].shape)
    pack_small = lambda d: jnp.concatenate([_as_rows(d[n], 8) for n in SMALL], axis=0)
    small_out = _adam_small(pack_small(wts), pack_small(results["grad"]), pack_small(mom), pack_small(var))
    off = 0
    for n in SMALL:
        r = _as_rows(wts[n], 8).shape[0]
        for kind, packed in zip(["delta", "new_m", "new_v"], small_out):
            results[kind][n] = packed[off:off + r].reshape(-1)[:wts[n].size].reshape(wts[n].shape)
        off += r

    outs = [loss, grad_x[None]]
    for kind in ["grad", "delta", "new_m", "new_v"]:
        outs += [results[kind][n] for n in ALL_W]
    return tuple(outs)
```

```python
import functools
import math

import jax
import jax.numpy as jnp
import numpy as np
from jax import lax
from jax.experimental import pallas as pl
from jax.experimental.pallas import tpu as pltpu

F32 = jnp.float32
BF16 = jnp.bfloat16

NORM_EPS = 1e-6
ROPE_THETA = 10000.0
HEAD_DIM = 128
ROPE_DIM = 64
MLA_HEADS = 16
MLA_QK_DIM = 192
MLA_PAD_DIM = 256
ADAM_LR, ADAM_B1, ADAM_B2, ADAM_EPS, ADAM_WD, ADAM_STEP = 0.001, 0.9, 0.999, 1e-08, 0.01, 10

N_DEV = 8
LANES = 128
VMEM_LIMIT = 56 * 1024 * 1024
MATMUL_VMEM_BUDGET = 40 * 1024 * 1024
MASKED = -1e30
LOG2E = 1.4426950408889634
ATTN_TQ, ATTN_TK = 256, 256
ATTN_TQ_WIDE = 512
ATTN_TK_SOFTMAX = 512
MESH = pl.DeviceIdType.MESH

NT_DIMS = (((1,), (1,)), ((), ()))
TN_DIMS = (((0,), (0,)), ((), ()))
NN_DIMS = (((1,), (0,)), ((), ()))


def _params(*sem):
    return pltpu.CompilerParams(dimension_semantics=sem, vmem_limit_bytes=VMEM_LIMIT)


def _pick(n, pref):
    best = None
    for t in range(LANES, min(n, pref) + 1, LANES):
        if n % t == 0:
            best = t
    return n if best is None or 2 * best < min(n, pref) else best


def _rows_call(fn, name, row_ins, full_ins, row_outs, acc_outs=(), tile=256):
    row_ins = [r if isinstance(r, tuple) else (r, r.shape[1], 0) for r in row_ins]
    t_rows = row_ins[0][0].shape[0]
    assert t_rows % tile == 0
    n_in = len(row_ins) + len(full_ins)
    n_row_out = len(row_outs)

    def body(*refs):
        res = fn(*[r[...] for r in refs[:n_in]])
        res = res if isinstance(res, tuple) else (res,)
        for ref, val in zip(refs[n_in:n_in + n_row_out], res[:n_row_out]):
            ref[...] = val.astype(ref.dtype)
        acc_refs = refs[n_in + n_row_out:]
        if acc_refs:
            @pl.when(pl.program_id(0) == 0)
            def _():
                for ref in acc_refs:
                    ref[...] = jnp.zeros_like(ref)
            for ref, val in zip(acc_refs, res[n_row_out:]):
                ref[...] += val.astype(ref.dtype)

    in_specs = [pl.BlockSpec((tile, w), functools.partial(lambda i, cb: (i, cb), cb=cb)) for _, w, cb in row_ins]
    in_specs += [pl.BlockSpec(a.shape, lambda i: (0, 0)) for a in full_ins]
    out_specs = [pl.BlockSpec((tile, c), lambda i: (i, 0)) for c, _ in row_outs]
    out_specs += [pl.BlockSpec(s, lambda i: (0, 0)) for s, _ in acc_outs]
    out_shape = [jax.ShapeDtypeStruct((t_rows, c), d) for c, d in row_outs]
    out_shape += [jax.ShapeDtypeStruct(s, d) for s, d in acc_outs]
    outs = pl.pallas_call(
        body, name=name, grid=(t_rows // tile,), in_specs=in_specs, out_specs=out_specs, out_shape=out_shape,
        compiler_params=_params("arbitrary"),
    )(*[r[0] for r in row_ins], *full_ins)
    return outs[0] if len(outs) == 1 else tuple(outs)


def _matmul_tiles(m, n, k, in_bytes, out_bytes):
    tn = n if n <= 1280 else _pick(n, 1024)
    tks = [k] + [k // d for d in (2, 4, 8, 16) if k % (d * LANES) == 0]
    for tk in [t for t in tks if t <= 4096] or [tks[-1]]:
        for tm in (1024, 512, 256):
            if m % tm:
                continue
            acc = 2 * tm * tn * 4 if tk < k else tm * tn * 4
            if 2 * (tm * tk + tk * tn) * in_bytes + 2 * tm * tn * out_bytes + acc <= MATMUL_VMEM_BUDGET:
                return tm, tn, tk
    raise ValueError(f"no matmul tiling for {m}x{n}x{k}")


def _matmul(a, b, form, name, out_dtypes=(F32,), epilogue=None, extras=(), hosted=()):
    if form == "nn":
        (m, k), n = a.shape, b.shape[1]
    elif form == "nt":
        (m, k), n = a.shape, b.shape[0]
    else:
        (k, m), n = a.shape, b.shape[1]
    in_bytes = max(a.dtype.itemsize, b.dtype.itemsize)
    out_bytes = sum(jnp.dtype(d).itemsize for d in out_dtypes) + sum(e.dtype.itemsize for e in extras)
    tm, tn, tk = _matmul_tiles(m, n, k, in_bytes, out_bytes)
    nk = k // tk
    dims = {"nn": NN_DIMS, "nt": NT_DIMS, "tn": TN_DIMS}[form]
    n_extra, n_out, n_host = len(extras), len(out_dtypes), len(hosted)
    grid = (m // tm, n // tn, nk)

    def body(*refs):
        a_ref, b_ref = refs[0], refs[1]
        extra_refs = refs[2:2 + n_extra]
        base = 2 + n_extra + n_host
        out_refs = refs[base:base + n_out]
        sems_at = base + n_out + n_host
        step = (pl.program_id(0) * grid[1] + pl.program_id(1)) * nk + pl.program_id(2)
        host_args = (hosted, refs[2 + n_extra:base], refs[base + n_out:sems_at], refs[sems_at:sems_at + 3 * n_host],
                     step, grid[0] * grid[1] * nk)
        _run_hosted(*host_args, "start")

        def finish(acc):
            vals = (acc,) if epilogue is None else epilogue(acc, *[r[...] for r in extra_refs])
            for ref, val in zip(out_refs, vals):
                ref[...] = val.astype(ref.dtype)

        part = lax.dot_general(a_ref[...].astype(BF16), b_ref[...].astype(BF16), dims, preferred_element_type=F32)
        if nk == 1:
            finish(part)
        else:
            acc_ref = refs[-1]
            kk = pl.program_id(2)

            @pl.when(kk == 0)
            def _():
                acc_ref[...] = part

            @pl.when(kk > 0)
            def _():
                acc_ref[...] += part

            @pl.when(kk == nk - 1)
            def _():
                finish(acc_ref[...])
        _run_hosted(*host_args, "finish")

    a_spec = pl.BlockSpec((tk, tm), lambda i, j, kk: (kk, i)) if form == "tn" else pl.BlockSpec((tm, tk), lambda i, j, kk: (i, kk))
    b_spec = pl.BlockSpec((tn, tk), lambda i, j, kk: (j, kk)) if form == "nt" else pl.BlockSpec((tk, tn), lambda i, j, kk: (kk, j))
    o_spec = pl.BlockSpec((tm, tn), lambda i, j, kk: (i, j))
    outs = pl.pallas_call(
        body, name=name, grid=grid, in_specs=[a_spec, b_spec] + [o_spec] * n_extra + [ANY_SPEC] * n_host,
        out_specs=[o_spec] * n_out + [ANY_SPEC] * n_host,
        out_shape=[jax.ShapeDtypeStruct((m, n), d) for d in out_dtypes]
        + [_exchange_out_shape(kd, arr) for kd, arr in hosted],
        scratch_shapes=EXCHANGE_SCRATCH * n_host + ([pltpu.VMEM((tm, tn), F32)] if nk > 1 else []),
        compiler_params=_params(*(("arbitrary",) * 3 if n_host else ("parallel", "parallel", "arbitrary"))),
    )(a, b, *extras, *[arr for _, arr in hosted])
    return outs[0] if n_out + n_host == 1 else tuple(outs)


def _add_residual(acc, res):
    return (acc + res,)


def _log_sigmoid_parts(z):
    return jnp.log1p(jnp.exp(-jnp.abs(z)))


def _log2_gates(z2):
    lg = jnp.log2(1.0 + jnp.exp2(-jnp.abs(z2)))
    log_beta = jnp.minimum(z2, 0.0) - lg
    return log_beta, log_beta - z2


def _rms_fwd(x, g, n=None):
    n = x.shape[-1] if n is None else n
    r = lax.rsqrt(jnp.sum(x * x, axis=-1, keepdims=True) / n + NORM_EPS)
    return x * r * g


def _rms_bwd(x, g, dout, n=None):
    n = x.shape[-1] if n is None else n
    r = lax.rsqrt(jnp.sum(x * x, axis=-1, keepdims=True) / n + NORM_EPS)
    y = x * r
    dg = jnp.sum(dout * y, axis=0, keepdims=True)
    dy = dout * g
    dx = r * (dy - y * (jnp.sum(dy * y, axis=-1, keepdims=True) / n))
    return dx, dg


def _swap_halves(r):
    lane = lax.broadcasted_iota(jnp.int32, r.shape, 1)
    return jnp.where(lane < ROPE_DIM // 2, pltpu.roll(r, LANES - ROPE_DIM // 2, 1), pltpu.roll(r, ROPE_DIM // 2, 1))


def _rope_fwd(r, cos_t, sin_s):
    return r * cos_t + _swap_halves(r) * sin_s


def _rope_bwd(dr, cos_t, sin_s):
    return dr * cos_t + _swap_halves(dr * sin_s)


def _split3(x):
    hi = x.astype(BF16)
    r1 = x - hi.astype(F32)
    mid = r1.astype(BF16)
    lo = (r1 - mid.astype(F32)).astype(BF16)
    return hi, mid, lo


def _mesh_position():
    x, y, c = lax.axis_index("x"), lax.axis_index("y"), lax.axis_index("c")
    return x, y, c, 4 * x + 2 * y + c


def _peer(x, y, c, k):
    bx, by, bc = (k >> 2) & 1, (k >> 1) & 1, k & 1
    px, py, pc = x ^ bx, y ^ by, c ^ bc
    return (px, py, pc), 4 * px + 2 * py + pc


def _gather_steps(x_ref, out_ref, send_sems, recv_sems, local_sem):
    x, y, c, me = _mesh_position()
    sibling = (x, y, 1 - c)
    chips = [(1 - x, y), (x, 1 - y), (1 - x, 1 - y)]

    def slot(px, py, pc):
        return out_ref.at[4 * px + 2 * py + pc]

    def copy(k, blk, to, src=None):
        return pltpu.make_async_remote_copy(
            src_ref=slot(*blk) if src is None else src, dst_ref=slot(*blk), send_sem=send_sems.at[k],
            recv_sem=recv_sems.at[k], device_id=to, device_id_type=MESH)

    mine = pltpu.make_async_copy(x_ref, out_ref.at[me], local_sem)
    first = [copy(0, (x, y, c), sibling, src=x_ref)]
    first += [copy(1 + j, (x, y, c), (*chip, c), src=x_ref) for j, chip in enumerate(chips)]
    passed = [copy(4 + j, (*chip, c), sibling) for j, chip in enumerate(chips)]

    def start():
        mine.start()
        for cp in first:
            cp.start()

    def forward():
        for j, chip in enumerate(chips):
            copy(1 + j, (*chip, c), (x, y, c)).wait_recv()
            passed[j].start()

    def finish():
        copy(0, (x, y, 1 - c), (x, y, c)).wait_recv()
        for j, chip in enumerate(chips):
            copy(4 + j, (*chip, 1 - c), (x, y, c)).wait_recv()
        for cp in first + passed:
            cp.wait_send()
        mine.wait()

    return start, forward, finish


def _all_to_all_steps(g_ref, out_ref, send_sems, recv_sems, local_sem):
    x, y, c, me = _mesh_position()
    mine = pltpu.make_async_copy(g_ref.at[me], out_ref.at[me], local_sem)
    copies = []
    for k in range(1, N_DEV):
        peer, peer_idx = _peer(x, y, c, k)
        copies.append(pltpu.make_async_remote_copy(
            src_ref=g_ref.at[peer_idx], dst_ref=out_ref.at[me], send_sem=send_sems.at[k - 1],
            recv_sem=recv_sems.at[k - 1], device_id=peer, device_id_type=MESH))

    def start():
        mine.start()
        for cp in copies:
            cp.start()

    def finish():
        for k in range(1, N_DEV):
            peer, peer_idx = _peer(x, y, c, k)
            pltpu.make_async_remote_copy(
                src_ref=g_ref.at[me], dst_ref=out_ref.at[peer_idx], send_sem=send_sems.at[k - 1],
                recv_sem=recv_sems.at[k - 1], device_id=peer, device_id_type=MESH).wait_recv()
        for cp in copies:
            cp.wait_send()
        mine.wait()

    return start, None, finish


EXCHANGE_STEPS = {"gather": _gather_steps, "all_to_all": _all_to_all_steps}
EXCHANGE_SCRATCH = [pltpu.SemaphoreType.DMA((7,)), pltpu.SemaphoreType.DMA((7,)), pltpu.SemaphoreType.DMA]
ANY_SPEC = pl.BlockSpec(memory_space=pl.ANY)


def _exchange_out_shape(kind, arr):
    return jax.ShapeDtypeStruct(((N_DEV,) + arr.shape) if kind == "gather" else arr.shape, arr.dtype)


def _exchange(kind, arrs, name):
    n = len(arrs)

    def body(*refs):
        steps = [EXCHANGE_STEPS[kind](refs[i], refs[n + i], *refs[2 * n + 3 * i:2 * n + 3 * i + 3]) for i in range(n)]
        for start, _, _ in steps:
            start()
        for _, forward, _ in steps:
            if forward is not None:
                forward()
        for _, _, finish in steps:
            finish()

    return pl.pallas_call(body, name=name, out_shape=[_exchange_out_shape(kind, a) for a in arrs],
                          in_specs=[ANY_SPEC] * n, out_specs=[ANY_SPEC] * n, scratch_shapes=EXCHANGE_SCRATCH * n)(*arrs)


def _run_hosted(hosted, src_refs, dst_refs, sem_refs, step, n_steps, when):
    for idx, (kind, _) in enumerate(hosted):
        start, forward, finish = EXCHANGE_STEPS[kind](src_refs[idx], dst_refs[idx], *sem_refs[3 * idx:3 * idx + 3])
        if when == "start":
            pl.when(step == 0)(start)
            if forward is not None:
                pl.when(step == (3 * n_steps) // 4)(forward)
        else:
            pl.when(step == n_steps - 1)(finish)


def _causal_iotas(qi, tq, tk):
    row = qi * tq + lax.broadcasted_iota(jnp.int32, (tq, tk), 0)
    col = lax.broadcasted_iota(jnp.int32, (tq, tk), 1)
    return row, col


def _suffix_matrix(tk, inclusive):
    j = lax.broadcasted_iota(jnp.int32, (2 * tk, tk), 0) % tk
    s = lax.broadcasted_iota(jnp.int32, (2 * tk, tk), 1)
    return jnp.where((j >= s) if inclusive else (j > s), 1.0, 0.0).astype(BF16)


def _suffix_sum(x, mat):
    hi = x.astype(BF16)
    lo = (x - hi.astype(F32)).astype(BF16)
    return lax.dot_general(jnp.concatenate([hi, lo], axis=1), mat, NN_DIMS, preferred_element_type=F32)


def _attn_specs(t_rows, tq, heads, dk, dv, q_off, k_off, v_off):
    q_spec = pl.BlockSpec((tq, dk), lambda h, i: (i, q_off + h))
    kt_spec = pl.BlockSpec((dk, t_rows), lambda h, i: (k_off + h, 0))
    v_spec = pl.BlockSpec((t_rows, dv), lambda h, i: (0, v_off + h))
    return q_spec, kt_spec, v_spec


def _split_weights(weights, fine):
    hi = weights.astype(BF16)
    return (hi, (weights - hi.astype(F32)).astype(BF16)) if fine else (hi,)


def _weighted_values(split, v):
    return sum(lax.dot_general(part, v, NN_DIMS, preferred_element_type=F32) for part in split)


def _attn_fwd(kind, q_arr, kt_arr, v_arr, heads, dk, dv, scale, name, q_off=0, k_off=0, v_off=0, fcol=None, frow=None,
              tq=ATTN_TQ, tk=ATTN_TK, fine=True, hosted=()):
    t_rows = q_arr.shape[0]
    tq, tk = min(tq, t_rows), min(tk, t_rows)
    nq = t_rows // tq
    stick = kind == "stick"
    decay = fcol is not None
    n_in = 5 if decay else 3
    n_out = 2 if stick else 3
    n_host = len(hosted)

    def body(*refs):
        q_ref, kt_ref, v_ref = refs[:3]
        fcol_ref, frow_ref = (refs[3], refs[4]) if decay else (None, None)
        base = n_in + n_host
        o_ref, fine_ref = refs[base], refs[base + 1]
        lse_ref = None if stick else refs[base + 2]
        host_args = (hosted, refs[n_in:base], refs[base + n_out:base + n_out + n_host], refs[base + n_out + n_host:],
                     pl.program_id(0) * nq + pl.program_id(1), heads * nq)
        _run_hosted(*host_args, "start")
        qi = pl.program_id(1)
        q = q_ref[...]
        row, col = _causal_iotas(qi, tq, tk)
        n_kb = ((qi + 1) * tq + tk - 1) // tk
        n_diag = max(1, tq // tk)
        zeros_o = jnp.zeros((tq, dv), F32)

        no_weights = (jnp.zeros((tq, tk), BF16),) * (2 if fine else 1)

        def raw_logits(kb):
            return lax.dot_general(q, kt_ref[:, pl.ds(pl.multiple_of(kb * tk, tk), tk)], NN_DIMS,
                                   preferred_element_type=F32)

        def values(kb):
            return v_ref[pl.ds(pl.multiple_of(kb * tk, tk), tk), :]

        if stick:
            mat = _suffix_matrix(tk, inclusive=False)

            def make_step(masked):
                def step(i, carry):
                    c, acc, raw, prev = carry
                    raw_next = raw_logits(jnp.maximum(n_kb - 2 - i, 0))
                    d_acc = _weighted_values(prev, values(jnp.minimum(n_kb - i, n_kb - 1)))
                    log_beta, lom = _log2_gates(raw * (scale * LOG2E))
                    if masked:
                        strict = (col + (n_kb - 1 - i) * tk) < row
                        lom = jnp.where(strict, lom, 0.0)
                    w = jnp.exp2(log_beta + (_suffix_sum(lom, mat) + c))
                    if masked:
                        w = jnp.where(strict, w, 0.0)
                    return c + jnp.sum(lom, axis=1, keepdims=True), acc + d_acc, raw_next, _split_weights(w, fine)
                return step

            carry = (jnp.zeros((tq, 1), F32), zeros_o, raw_logits(n_kb - 1), no_weights)
            for i in range(n_diag):
                carry = make_step(True)(i, carry)
            _, acc, _, last = lax.fori_loop(n_diag, n_kb, make_step(False), carry)
            acc = acc + _weighted_values(last, values(0))
            o_ref[...] = acc.astype(o_ref.dtype)
            fine_ref[...] = acc
        else:
            fc = fcol_ref[...] * LOG2E if decay else None

            def make_step(masked):
                def step(kb, carry):
                    m, l, acc, raw, prev = carry
                    raw_next = raw_logits(jnp.minimum(kb + 1, n_kb - 1))
                    d_acc = _weighted_values(prev, values(jnp.maximum(kb - 1, 0)))
                    ks = pl.multiple_of(kb * tk, tk)
                    s = raw * (scale * LOG2E)
                    if decay:
                        s = (s + fc) - frow_ref[:, pl.ds(ks, tk)] * LOG2E
                    if masked:
                        s = jnp.where((col + ks) <= row, s, MASKED)
                    m_new = jnp.maximum(m, jnp.max(s, axis=1, keepdims=True))
                    alpha = jnp.exp2(m - m_new)
                    p = jnp.exp2(s - m_new)
                    l = alpha * l + jnp.sum(p, axis=1, keepdims=True)
                    return m_new, l, alpha * (acc + d_acc), raw_next, _split_weights(p, fine)
                return step

            carry = lax.fori_loop(0, n_kb - n_diag, make_step(False),
                                  (jnp.full((tq, 1), MASKED, F32), jnp.zeros((tq, 1), F32), zeros_o, raw_logits(0),
                                   no_weights))
            for d in range(n_diag):
                carry = make_step(True)(n_kb - n_diag + d, carry)
            m, l, acc, _, last = carry
            out = (acc + _weighted_values(last, values(n_kb - 1))) * (1.0 / l)
            o_ref[...] = out.astype(o_ref.dtype)
            fine_ref[...] = out
            lse_ref[...] = (m + jnp.log2(l)) * (1.0 / LOG2E)
        _run_hosted(*host_args, "finish")

    q_spec, k_spec, v_spec = _attn_specs(t_rows, tq, heads, dk, dv, q_off, k_off, v_off)
    stat_spec = pl.BlockSpec((None, tq, 1), lambda h, i: (h, i, 0))
    ins, in_specs = [q_arr, kt_arr, v_arr], [q_spec, k_spec, v_spec]
    if decay:
        ins += [fcol, frow]
        in_specs += [stat_spec, pl.BlockSpec((None, 1, t_rows), lambda h, i: (h, 0, 0))]
    o_spec = pl.BlockSpec((tq, dv), lambda h, i: (i, h))
    out_specs = [o_spec, o_spec]
    out_shape = [jax.ShapeDtypeStruct((t_rows, heads * dv), BF16), jax.ShapeDtypeStruct((t_rows, heads * dv), F32)]
    if not stick:
        out_specs.append(stat_spec)
        out_shape.append(jax.ShapeDtypeStruct((heads, t_rows, 1), F32))
    return tuple(pl.pallas_call(
        body, name=name, grid=(heads, nq), in_specs=in_specs + [ANY_SPEC] * n_host,
        out_specs=out_specs + [ANY_SPEC] * n_host,
        out_shape=out_shape + [_exchange_out_shape(kd, arr) for kd, arr in hosted],
        scratch_shapes=EXCHANGE_SCRATCH * n_host,
        compiler_params=_params("arbitrary" if n_host else "parallel", "arbitrary"),
    )(*ins, *[arr for _, arr in hosted]))


def _attn_bwd(kind, q_arr, k_arr, kt_arr, vt_arr, o_arr, do_arr, heads, dk, dv, scale, name, q_off=0, k_off=0, kt_off=0,
              vt_off=0, do_off=0, lse=None, fcol=None, frow=None, tq=ATTN_TQ, tk=ATTN_TK, hosted=()):
    t_rows = q_arr.shape[0]
    tq, tk = min(tq, t_rows), min(tk, t_rows)
    nq = t_rows // tq
    stick = kind == "stick"
    decay = fcol is not None
    n_in = 6 + (0 if stick else 1) + (2 if decay else 0)
    n_out = 5 if decay else 3
    n_host = len(hosted)

    def body(*refs):
        q_ref, k_ref, kt_ref, vt_ref, o_ref, do_ref = refs[:6]
        lse_ref = None if stick else refs[6]
        fcol_ref, frow_ref = (refs[7], refs[8]) if decay else (None, None)
        base = n_in + n_host
        dq_ref, dk_ref, dv_ref = refs[base:base + 3]
        dfcol_ref, dfrow_ref = (refs[base + 3], refs[base + 4]) if decay else (None, None)
        host_args = (hosted, refs[n_in:base], refs[base + n_out:base + n_out + n_host], refs[base + n_out + n_host:],
                     pl.program_id(0) * nq + pl.program_id(1), heads * nq)
        _run_hosted(*host_args, "start")
        qi = pl.program_id(1)

        @pl.when(qi == 0)
        def _():
            dk_ref[...] = jnp.zeros_like(dk_ref)
            dv_ref[...] = jnp.zeros_like(dv_ref)
            if decay:
                dfrow_ref[...] = jnp.zeros_like(dfrow_ref)

        q = q_ref[...]
        do = do_ref[...]
        delta = jnp.sum(do.astype(F32) * o_ref[...], axis=1, keepdims=True)
        row, col = _causal_iotas(qi, tq, tk)
        n_kb = ((qi + 1) * tq + tk - 1) // tk
        n_diag = max(1, tq // tk)

        no_pair = (jnp.zeros((tq, tk), BF16), jnp.zeros((tq, tk), BF16))

        def accumulate(kb, pair):
            at = pl.ds(pl.multiple_of(kb * tk, tk), tk)
            dk_ref[at, :] += lax.dot_general(pair[0], q, TN_DIMS, preferred_element_type=F32)
            dv_ref[at, :] += lax.dot_general(pair[1], do, TN_DIMS, preferred_element_type=F32)
            return lax.dot_general(pair[0], k_ref[at, :], NN_DIMS, preferred_element_type=F32)

        def raw_logits(kb):
            return lax.dot_general(q, kt_ref[:, pl.ds(pl.multiple_of(kb * tk, tk), tk)], NN_DIMS,
                                   preferred_element_type=F32)

        def d_weights(kb):
            return lax.dot_general(do, vt_ref[:, pl.ds(pl.multiple_of(kb * tk, tk), tk)], NN_DIMS,
                                   preferred_element_type=F32)

        if stick:
            mat_ex = _suffix_matrix(tk, inclusive=False)
            mat_in = _suffix_matrix(tk, inclusive=True)

            def make_step(masked):
                def step(i, carry):
                    c, gs, dq, raw, prev = carry
                    raw_next = raw_logits(jnp.maximum(n_kb - 2 - i, 0))
                    dw = d_weights(n_kb - 1 - i)
                    dq = dq + accumulate(jnp.minimum(n_kb - i, n_kb - 1), prev)
                    log_beta, log_omb = _log2_gates(raw * (scale * LOG2E))
                    lom = log_omb
                    if masked:
                        strict = (col + (n_kb - 1 - i) * tk) < row
                        lom = jnp.where(strict, log_omb, 0.0)
                    w = jnp.exp2(log_beta + (_suffix_sum(lom, mat_ex) + c))
                    if masked:
                        w = jnp.where(strict, w, 0.0)
                    g = w * dw
                    g_before = delta - (gs + _suffix_sum(g, mat_in))
                    dz = g * jnp.exp2(log_omb) - g_before * jnp.exp2(log_beta)
                    if masked:
                        dz = jnp.where(strict, dz, 0.0)
                    return (c + jnp.sum(lom, axis=1, keepdims=True), gs + jnp.sum(g, axis=1, keepdims=True), dq,
                            raw_next, ((dz * scale).astype(BF16), w.astype(BF16)))
                return step

            zero = jnp.zeros((tq, 1), F32)
            carry = (zero, zero, jnp.zeros((tq, dk), F32), raw_logits(n_kb - 1), no_pair)
            for i in range(n_diag):
                carry = make_step(True)(i, carry)
            _, _, dq, _, last = lax.fori_loop(n_diag, n_kb, make_step(False), carry)
            dq = dq + accumulate(0, last)
        else:
            lse_v = lse_ref[...] * LOG2E
            fc = fcol_ref[...] * LOG2E if decay else None

            def make_step(masked):
                def step(kb, carry):
                    dq, row_sum, raw, prev = carry
                    raw_next = raw_logits(jnp.minimum(kb + 1, n_kb - 1))
                    dp = d_weights(kb)
                    dq = dq + accumulate(jnp.maximum(kb - 1, 0), prev)
                    ks = pl.multiple_of(kb * tk, tk)
                    s = raw * (scale * LOG2E)
                    if decay:
                        s = (s + fc) - frow_ref[:, pl.ds(ks, tk)] * LOG2E
                    p = jnp.exp2(s - lse_v)
                    if masked:
                        p = jnp.where((col + ks) <= row, p, 0.0)
                    ds = p * (dp - delta)
                    if decay:
                        dfrow_ref[:, pl.ds(ks, tk)] += jnp.sum(ds, axis=0, keepdims=True)
                        row_sum = row_sum + jnp.sum(ds, axis=1, keepdims=True)
                    return dq, row_sum, raw_next, ((ds * scale).astype(BF16), p.astype(BF16))
                return step

            carry = lax.fori_loop(0, n_kb - n_diag, make_step(False),
                                  (jnp.zeros((tq, dk), F32), jnp.zeros((tq, 1), F32), raw_logits(0), no_pair))
            for d in range(n_diag):
                carry = make_step(True)(n_kb - n_diag + d, carry)
            dq, row_sum, _, last = carry
            dq = dq + accumulate(n_kb - 1, last)
            if decay:
                dfcol_ref[...] = row_sum
        dq_ref[...] = dq
        _run_hosted(*host_args, "finish")

    q_spec, kt_spec, _ = _attn_specs(t_rows, tq, heads, dk, dv, q_off, kt_off, 0)
    stat_spec = pl.BlockSpec((None, tq, 1), lambda h, i: (h, i, 0))
    frow_spec = pl.BlockSpec((None, 1, t_rows), lambda h, i: (h, 0, 0))
    ins = [q_arr, k_arr, kt_arr, vt_arr, o_arr, do_arr]
    in_specs = [q_spec, pl.BlockSpec((t_rows, dk), lambda h, i: (0, k_off + h)), kt_spec,
                pl.BlockSpec((dv, t_rows), lambda h, i: (vt_off + h, 0)), pl.BlockSpec((tq, dv), lambda h, i: (i, h)),
                pl.BlockSpec((tq, dv), lambda h, i: (i, do_off + h))]
    if not stick:
        ins.append(lse)
        in_specs.append(stat_spec)
    if decay:
        ins += [fcol, frow]
        in_specs += [stat_spec, frow_spec]
    out_specs = [pl.BlockSpec((tq, dk), lambda h, i: (i, h)), pl.BlockSpec((t_rows, dk), lambda h, i: (0, h)),
                 pl.BlockSpec((t_rows, dv), lambda h, i: (0, h))]
    out_shape = [jax.ShapeDtypeStruct((t_rows, heads * dk), F32), jax.ShapeDtypeStruct((t_rows, heads * dk), F32),
                 jax.ShapeDtypeStruct((t_rows, heads * dv), F32)]
    if decay:
        out_specs += [stat_spec, frow_spec]
        out_shape += [jax.ShapeDtypeStruct((heads, t_rows, 1), F32), jax.ShapeDtypeStruct((heads, 1, t_rows), F32)]
    return pl.pallas_call(
        body, name=name, grid=(heads, nq), in_specs=in_specs + [ANY_SPEC] * n_host,
        out_specs=out_specs + [ANY_SPEC] * n_host,
        out_shape=out_shape + [_exchange_out_shape(kd, arr) for kd, arr in hosted],
        scratch_shapes=EXCHANGE_SCRATCH * n_host,
        compiler_params=_params("arbitrary" if n_host else "parallel", "arbitrary"),
    )(*ins, *[arr for _, arr in hosted])


def _prefix_matrix(reverse):
    j = lax.broadcasted_iota(jnp.int32, (LANES, LANES), 0)
    s = lax.broadcasted_iota(jnp.int32, (LANES, LANES), 1)
    return jnp.where((j >= s) if reverse else (j <= s), 1.0, 0.0).astype(BF16)


def _chunk_cumsum(x, mat):
    return sum(lax.dot_general(part, mat, NN_DIMS, preferred_element_type=F32) for part in _split3(x))


def _gate_fwd(logit_t, bias_col):
    heads, t_rows = logit_t.shape

    def body(x_ref, b_ref, out_ref):
        mat = _prefix_matrix(reverse=False)

        def step(ci, carry):
            cs = pl.multiple_of(ci * LANES, LANES)
            pre = x_ref[:, pl.ds(cs, LANES)] + b_ref[...]
            log_f = jnp.minimum(pre, 0.0) - _log_sigmoid_parts(pre)
            out_ref[:, pl.ds(cs, LANES)] = _chunk_cumsum(log_f, mat) + carry
            return carry + jnp.sum(log_f, axis=1, keepdims=True)

        lax.fori_loop(0, t_rows // LANES, step, jnp.zeros((heads, 1), F32))

    return pl.pallas_call(body, name="gate_fwd", out_shape=jax.ShapeDtypeStruct((heads, t_rows), F32),
                          compiler_params=pltpu.CompilerParams(vmem_limit_bytes=VMEM_LIMIT))(logit_t, bias_col)


def _gate_bwd(dcum_t, logit_t, bias_col):
    heads, t_rows = logit_t.shape
    n_chunks = t_rows // LANES

    def body(d_ref, x_ref, b_ref, dx_ref, db_ref):
        mat = _prefix_matrix(reverse=True)

        def step(i, carry):
            tail, db = carry
            cs = pl.multiple_of((n_chunks - 1 - i) * LANES, LANES)
            d = d_ref[:, pl.ds(cs, LANES)]
            d_log_f = _chunk_cumsum(d, mat) + tail
            pre = x_ref[:, pl.ds(cs, LANES)] + b_ref[...]
            e = jnp.exp(-jnp.abs(pre))
            d_pre = d_log_f * (jnp.where(pre >= 0.0, e, 1.0) / (1.0 + e))
            dx_ref[:, pl.ds(cs, LANES)] = d_pre
            return tail + jnp.sum(d, axis=1, keepdims=True), db + jnp.sum(d_pre, axis=1, keepdims=True)

        zero = jnp.zeros((heads, 1), F32)
        _, db = lax.fori_loop(0, n_chunks, step, (zero, zero))
        db_ref[...] = db

    return pl.pallas_call(body, name="gate_bwd",
                          out_shape=(jax.ShapeDtypeStruct((heads, t_rows), F32), jax.ShapeDtypeStruct((heads, 1), F32)),
                          compiler_params=pltpu.CompilerParams(vmem_limit_bytes=VMEM_LIMIT))(dcum_t, logit_t, bias_col)


def _norm_fwd(x, g, name):
    return _rows_call(lambda xv, gv: _rms_fwd(xv, gv), name, [x], [g], [(x.shape[1], BF16)])


def _norm_bwd(x, g, dh, dres, name):
    def fn(xv, dhv, dresv, gv):
        dx, dg = _rms_bwd(xv, gv, dhv)
        dx = dresv + dx
        return dx, dx, dg
    return _rows_call(fn, name, [x, dh, dres], [g], [(x.shape[1], F32), (x.shape[1], BF16)], [((1, x.shape[1]), F32)])


def _loss_fwd_bwd(y, target):
    d_model = y.shape[1]

    def fn(yv, tv):
        err = yv - tv
        dy = err * (1.0 / d_model)
        return dy, dy, jnp.sum(jnp.sum(err * err, axis=1, keepdims=True), axis=0, keepdims=True)
    return _rows_call(fn, "loss", [y, target], [], [(d_model, F32), (d_model, BF16)], [((1, 1), F32)])


def _heads_apply(fn, n_heads, width, *tiles):
    return [fn(*[t[:, h * width:(h + 1) * width] for t in tiles]) for h in range(n_heads)]


def _fox_norm_fwd(pb, gq, gk, heads):
    width = heads * HEAD_DIM

    def fn(qk, gqv, gkv):
        q = jnp.concatenate(_heads_apply(lambda t: _rms_fwd(t, gqv), heads, HEAD_DIM, qk[:, :width]), axis=1)
        k = jnp.concatenate(_heads_apply(lambda t: _rms_fwd(t, gkv), heads, HEAD_DIM, qk[:, width:]), axis=1)
        return q, k
    return _rows_call(fn, "fox_norm_fwd", [(pb, 2 * width, 0)], [gq, gk], [(width, BF16), (width, BF16)])


def _fox_norm_bwd(pb, gq, gk, dq, dk, heads):
    width = heads * HEAD_DIM

    def fn(qk, dqv, dkv, gqv, gkv):
        res_q = _heads_apply(lambda t, d: _rms_bwd(t, gqv, d), heads, HEAD_DIM, qk[:, :width], dqv)
        res_k = _heads_apply(lambda t, d: _rms_bwd(t, gkv, d), heads, HEAD_DIM, qk[:, width:], dkv)
        dqk = jnp.concatenate([r[0] for r in res_q] + [r[0] for r in res_k], axis=1)
        return dqk, sum(r[1] for r in res_q), sum(r[1] for r in res_k)
    return _rows_call(fn, "fox_norm_bwd", [(pb, 2 * width, 0), dq, dk], [gq, gk], [(2 * width, BF16)],
                      [((1, HEAD_DIM), F32), ((1, HEAD_DIM), F32)])


def _lora_norm_fwd(down, gq, gkv, rank):
    def fn(dv, gqv, gkvv):
        return _rms_fwd(dv[:, :rank], gqv), _rms_fwd(dv[:, rank:], gkvv)
    return _rows_call(fn, "lora_norm_fwd", [(down, 2 * rank, 0)], [gq, gkv], [(rank, BF16), (rank, BF16)])


def _lora_norm_bwd(down, gq, gkv, dcq, dckv, dkpe, rank):
    def fn(dv, dcqv, dckvv, dkpev, gqv, gkvv):
        dxq, dgq = _rms_bwd(dv[:, :rank], gqv, dcqv)
        dxkv, dgkv = _rms_bwd(dv[:, rank:], gkvv, dckvv)
        return jnp.concatenate([dxq, dxkv, dkpev], axis=1), dgq, dgkv
    return _rows_call(fn, "lora_norm_bwd", [(down, 2 * rank, 0), dcq, dckv, dkpe], [gq, gkv],
                      [(2 * rank + LANES, BF16)], [((1, rank), F32), ((1, rank), F32)])


def _rope_tables(pos_col, inv_freq, sin_sign):
    def fn(pos, invf, sign):
        ang = pos.astype(F32) * invf
        return jnp.cos(ang) * jnp.abs(sign), jnp.sin(ang) * sign
    return _rows_call(fn, "rope_tables", [pos_col], [inv_freq, sin_sign], [(LANES, F32), (LANES, F32)])


def _mla_prep_fwd(q_raw, kv, down, kpe_block, qg, kg, cos_t, sin_s):
    def fn(qv, kvv, kpe, cosv, sinv, qgv, kgv):
        qs, ks, vs = [], [], []
        for h in range(MLA_HEADS):
            qn = _rms_fwd(qv[:, h * MLA_PAD_DIM:(h + 1) * MLA_PAD_DIM], qgv, MLA_QK_DIM)
            qs += [qn[:, :HEAD_DIM], _rope_fwd(qn[:, HEAD_DIM:], cosv, sinv)]
            k_full = jnp.concatenate([kvv[:, h * MLA_PAD_DIM:h * MLA_PAD_DIM + HEAD_DIM], kpe], axis=1)
            kn = _rms_fwd(k_full, kgv, MLA_QK_DIM)
            ks += [kn[:, :HEAD_DIM], _rope_fwd(kn[:, HEAD_DIM:], cosv, sinv)]
            vs.append(kvv[:, h * MLA_PAD_DIM + HEAD_DIM:(h + 1) * MLA_PAD_DIM])
        return jnp.concatenate(qs, axis=1), jnp.concatenate(ks, axis=1), jnp.concatenate(vs, axis=1)
    wide = MLA_HEADS * MLA_PAD_DIM
    return _rows_call(fn, "mla_prep_fwd", [q_raw, kv, (down, LANES, kpe_block), cos_t, sin_s], [qg, kg],
                      [(wide, BF16), (wide, BF16), (MLA_HEADS * HEAD_DIM, BF16)], tile=128)


def _mla_prep_bwd(q_raw, kv, down, kpe_block, qg, kg, cos_t, sin_s, dq, dk, dv):
    def fn(qv, kvv, kpe, cosv, sinv, dqv, dkv, dvv, qgv, kgv):
        dqs, dkvs = [], []
        dkpe = jnp.zeros_like(kpe)
        dqg = jnp.zeros_like(qgv)
        dkg = jnp.zeros_like(kgv)
        for h in range(MLA_HEADS):
            lo, hi = h * MLA_PAD_DIM, (h + 1) * MLA_PAD_DIM
            dqn = jnp.concatenate([dqv[:, lo:lo + HEAD_DIM], _rope_bwd(dqv[:, lo + HEAD_DIM:hi], cosv, sinv)], axis=1)
            dqh, dg = _rms_bwd(qv[:, lo:hi], qgv, dqn, MLA_QK_DIM)
            dqs.append(dqh)
            dqg = dqg + dg
            k_full = jnp.concatenate([kvv[:, lo:lo + HEAD_DIM], kpe], axis=1)
            dkn = jnp.concatenate([dkv[:, lo:lo + HEAD_DIM], _rope_bwd(dkv[:, lo + HEAD_DIM:hi], cosv, sinv)], axis=1)
            dkh, dg = _rms_bwd(k_full, kgv, dkn, MLA_QK_DIM)
            dkg = dkg + dg
            dkpe = dkpe + dkh[:, HEAD_DIM:]
            dkvs += [dkh[:, :HEAD_DIM], dvv[:, h * HEAD_DIM:(h + 1) * HEAD_DIM]]
        return jnp.concatenate(dqs, axis=1), jnp.concatenate(dkvs, axis=1), dkpe, dqg, dkg
    wide = MLA_HEADS * MLA_PAD_DIM
    return _rows_call(fn, "mla_prep_bwd", [q_raw, kv, (down, LANES, kpe_block), cos_t, sin_s, dq, dk, dv], [qg, kg],
                      [(wide, BF16), (wide, BF16), (LANES, F32)], [((1, MLA_PAD_DIM), F32), ((1, MLA_PAD_DIM), F32)],
                      tile=128)


def _sqrelu_up(acc):
    return acc, jnp.square(jnp.maximum(acc, 0.0))


def _sqrelu_grad(acc, u):
    return (acc * (2.0 * jnp.maximum(u, 0.0)),)


def _mlp_fwd(x, g, w_up, w_down, tag):
    h = _norm_fwd(x, g, f"mlp_norm_fwd{tag}")
    u, a = _matmul(h, w_up, "nn", f"mlp_up{tag}", (F32, BF16), _sqrelu_up)
    return _matmul(a, w_down, "nn", f"mlp_down{tag}", (F32,), _add_residual, (x,)), (h, u, a)


def _mlp_bwd(x, g, w_up, w_down, saved, dy, dy16, tag):
    h, u, a = saved
    dw_down = _matmul(a, dy16, "tn", f"mlp_dwdown{tag}", (BF16,))
    du = _matmul(dy16, w_down, "nt", f"mlp_du{tag}", (BF16,), _sqrelu_grad, (u,))
    dw_up = _matmul(h, du, "tn", f"mlp_dwup{tag}", (BF16,))
    dh = _matmul(du, w_up, "nt", f"mlp_dh{tag}")
    dx, dx16, dg = _norm_bwd(x, g, dh, dy, f"mlp_norm_bwd{tag}")
    return dx, dx16, dg, dw_up, dw_down


def _local_step(x, pos_col, target, w, dist=None):
    w = dict(w)
    hs = w["w_a"].shape[1] // (4 * HEAD_DIM)
    sb_w = hs * HEAD_DIM
    grads, received = {}, {}

    def gather_in(group):
        return [("gather", blk) for blk in dist["blocks"][group]] if dist else []

    def exchange_in(group):
        return [("all_to_all", slots) for slots in dist["slots_of"](group, grads)] if dist else []

    h0 = _norm_fwd(x, w["ln_mix0"], "mix0_norm_fwd")
    pa = _matmul(h0, w["w_a"], "nn", "in_proj_a", (BF16,))
    pb = _matmul(h0, w["w_b"], "nn", "in_proj_b")
    pat = pa[:, sb_w:].T
    o_sb, o_sb_fine, *got = _attn_fwd("stick", pa, pat, pa, hs, HEAD_DIM, HEAD_DIM, HEAD_DIM ** -0.5, "stick_fwd",
                                      q_off=0, k_off=0, v_off=2 * hs, tq=ATTN_TQ_WIDE,
                                      hosted=gather_in("with_stick_fwd"))
    if dist:
        w.update(dist["weights_of"]("with_stick_fwd", got))
    logit_t = pb[:, 2 * sb_w:2 * sb_w + hs].T
    bias_col = w["b_f"][0, :hs].reshape(hs, 1)
    f_cum = _gate_fwd(logit_t, bias_col)
    f_col, f_row = f_cum[:, :, None], f_cum[:, None, :]
    qf, kf = _fox_norm_fwd(pb, w["fox_q_g"], w["fox_k_g"], hs)
    kft = kf.T
    o_fx, o_fx_fine, lse_fx, *got = _attn_fwd("softmax", qf, kft, pa, hs, HEAD_DIM, HEAD_DIM, HEAD_DIM ** -0.5,
                                              "fox_fwd", v_off=3 * hs, fcol=f_col, frow=f_row, tq=ATTN_TQ_WIDE, tk=ATTN_TK_SOFTMAX,
                                              fine=False,
                                              hosted=gather_in("with_fox_fwd"))
    if dist:
        w.update(dist["weights_of"]("with_fox_fwd", got))
    o0 = jnp.concatenate([o_sb, o_fx], axis=1)
    x1 = _matmul(o0, w["w_o0"], "nn", "out_proj0", (F32,), _add_residual, (x,))
    x2, mlp0 = _mlp_fwd(x1, w["ln_mlp0"], w["w_up0"], w["w_dn0"], "0")

    rank = w["w_uq"].shape[0]
    h2 = _norm_fwd(x2, w["ln_mix1"], "mix1_norm_fwd")
    down = _matmul(h2, w["w_down"], "nn", "mla_down")
    cqn, ckvn = _lora_norm_fwd(down, w["q_a_g"], w["kv_a_g"], rank)
    q_raw = _matmul(cqn, w["w_uq"], "nn", "mla_uq")
    kv = _matmul(ckvn, w["w_ukv"], "nn", "mla_ukv")
    cos_t, sin_s = _rope_tables(pos_col, w["inv_freq"], w["sin_sign"])
    kpe_block = 2 * rank // LANES
    qm, km, vm = _mla_prep_fwd(q_raw, kv, down, kpe_block, w["mla_q_g"], w["mla_k_g"], cos_t, sin_s)
    kmt, vmt = km.T, vm.T
    o_m, o_m_fine, lse_m, *got = _attn_fwd("softmax", qm, kmt, vm, MLA_HEADS, MLA_PAD_DIM, HEAD_DIM,
                                           MLA_QK_DIM ** -0.5, "mla_fwd", tq=ATTN_TQ_WIDE, tk=ATTN_TK_SOFTMAX,
                                           fine=False, hosted=gather_in("mlp1"))
    if dist:
        w.update(dist["weights_of"]("mlp1", got))
    x3 = _matmul(o_m, w["w_o1"], "nn", "out_proj1", (F32,), _add_residual, (x2,))
    x4, mlp1 = _mlp_fwd(x3, w["ln_mlp1"], w["w_up1"], w["w_dn1"], "1")

    dy, dy16, sq_err = _loss_fwd_bwd(x4, target)

    dx3, dx3_16, grads["ln_mlp1"], grads["w_up1"], grads["w_dn1"] = _mlp_bwd(
        x3, w["ln_mlp1"], w["w_up1"], w["w_dn1"], mlp1, dy, dy16, "1")
    grads["w_o1"] = _matmul(o_m, dx3_16, "tn", "dw_o1", (BF16,))
    do_m = _matmul(dx3_16, w["w_o1"], "nt", "do_mla", (BF16,))
    dqm, dkm, dvm, *got = _attn_bwd("softmax", qm, km, kmt, vmt, o_m_fine, do_m, MLA_HEADS, MLA_PAD_DIM, HEAD_DIM,
                                    MLA_QK_DIM ** -0.5, "mla_bwd", lse=lse_m, tq=ATTN_TQ_WIDE, tk=ATTN_TK_SOFTMAX,
                                    hosted=exchange_in("mlp1"))
    received["mlp1"] = got
    dq_raw, dkv, dkpe, grads["mla_q_g"], grads["mla_k_g"] = _mla_prep_bwd(
        q_raw, kv, down, kpe_block, w["mla_q_g"], w["mla_k_g"], cos_t, sin_s, dqm, dkm, dvm)
    grads["w_uq"] = _matmul(cqn, dq_raw, "tn", "dw_uq", (BF16,))
    grads["w_ukv"] = _matmul(ckvn, dkv, "tn", "dw_ukv", (BF16,))
    dcqn = _matmul(dq_raw, w["w_uq"], "nt", "d_cq")
    dckvn = _matmul(dkv, w["w_ukv"], "nt", "d_ckv")
    ddown, grads["q_a_g"], grads["kv_a_g"] = _lora_norm_bwd(down, w["q_a_g"], w["kv_a_g"], dcqn, dckvn, dkpe, rank)
    grads["w_down"] = _matmul(h2, ddown, "tn", "dw_down", (BF16,))
    dh2 = _matmul(ddown, w["w_down"], "nt", "d_h2")
    dx2, dx2_16, grads["ln_mix1"] = _norm_bwd(x2, w["ln_mix1"], dh2, dx3, "mix1_norm_bwd")

    dx1, dx1_16, grads["ln_mlp0"], grads["w_up0"], grads["w_dn0"] = _mlp_bwd(
        x1, w["ln_mlp0"], w["w_up0"], w["w_dn0"], mlp0, dx2, dx2_16, "0")
    grads["w_o0"] = _matmul(o0, dx1_16, "tn", "dw_o0", (BF16,))
    do0 = _matmul(dx1_16, w["w_o0"], "nt", "do_mix0", (BF16,))
    dq_sb, dk_sb, dv_sb, *got = _attn_bwd("stick", pa, pa, pat, pat, o_sb_fine, do0, hs, HEAD_DIM, HEAD_DIM,
                                          HEAD_DIM ** -0.5, "stick_bwd", q_off=0, k_off=hs, kt_off=0, vt_off=hs, do_off=0,
                                          tq=ATTN_TQ_WIDE,
                                          hosted=exchange_in("with_stick_bwd"))
    received["with_stick_bwd"] = got
    dqf, dkf, dv_fx, ds_rows, ds_cols, *got = _attn_bwd(
        "softmax", qf, kf, kft, pat, o_fx_fine, do0, hs, HEAD_DIM, HEAD_DIM, HEAD_DIM ** -0.5, "fox_bwd", vt_off=2 * hs,
        tq=ATTN_TQ_WIDE, tk=ATTN_TK_SOFTMAX,
        do_off=hs, lse=lse_fx, fcol=f_col, frow=f_row, hosted=exchange_in("with_fox_bwd"))
    received["with_fox_bwd"] = got
    dqk_fx, grads["fox_q_g"], grads["fox_k_g"] = _fox_norm_bwd(pb, w["fox_q_g"], w["fox_k_g"], dqf, dkf, hs)
    dlogit_t, db_f = _gate_bwd(ds_rows[:, :, 0] - ds_cols[:, 0, :], logit_t, bias_col)
    grads["b_f"] = db_f.reshape(1, hs)
    dpa = jnp.concatenate([dq_sb.astype(BF16), dk_sb.astype(BF16), dv_sb.astype(BF16), dv_fx.astype(BF16)], axis=1)
    dlogit_pad = jnp.pad(dlogit_t.T.astype(BF16), ((0, 0), (0, pb.shape[1] - 2 * sb_w - hs)))
    dpb = jnp.concatenate([dqk_fx, dlogit_pad], axis=1)
    grads["w_a"] = _matmul(h0, dpa, "tn", "dw_a", (BF16,))
    grads["w_b"] = _matmul(h0, dpb, "tn", "dw_b", (BF16,))
    parts = exchange_in("mix0")
    res_b = _matmul(dpb, w["w_b"], "nt", "d_h0_b", hosted=parts[:1])
    dh0 = res_b[0] if dist else res_b
    res_a = _matmul(dpa, w["w_a"], "nt", "d_h0_a", (F32,), _add_residual, (dh0,), hosted=parts[1:])
    dh0, received["mix0"] = (res_a[0], [res_b[1], res_a[1]]) if dist else (res_a, [])
    grad_x, _, grads["ln_mix0"] = _norm_bwd(x, w["ln_mix0"], dh0, dx1, "mix0_norm_bwd")
    return sq_err, grad_x, grads, received


PIECES = {
    "sf_w_in": ("sf_w_in", 0, 1), "sf_w_o": ("sf_w_o", 0, 0), "mla_w_down": ("mla_w_down", 0, 0),
    "mla_w_uq": ("mla_w_uq", 0, 1), "mla_w_ukv": ("mla_w_ukv", 0, 1), "mla_w_o": ("mla_w_o", 0, 0),
    "mlp_w_up0": ("mlp_w_up", 0, 1), "mlp_w_up1": ("mlp_w_up", 1, 1),
    "mlp_w_down0": ("mlp_w_down", 0, 0), "mlp_w_down1": ("mlp_w_down", 1, 0),
}
GROUPS = {
    "mix0": ["sf_w_in"], "mla": ["mla_w_down", "mla_w_uq", "mla_w_ukv", "mla_w_o"], "mlp1": ["mlp_w_up1", "mlp_w_down1"],
}
GROUPS["with_stick_fwd"] = ["sf_w_o", "mlp_w_up0"]
GROUPS["with_fox_fwd"] = GROUPS["mla"] + ["mlp_w_down0"]
GROUPS["with_stick_bwd"] = GROUPS["mla"] + ["mlp_w_up0"]
GROUPS["with_fox_bwd"] = ["mlp_w_down0", "sf_w_o"]
SMALL = ["ln_mix_g", "ln_mlp_g", "sf_b_f", "fox_q_g", "fox_k_g", "mla_q_a_g", "mla_kv_a_g", "mla_q_g", "mla_k_g"]
ALL_W = ["ln_mix_g", "ln_mlp_g", "sf_w_in", "sf_b_f", "fox_q_g", "fox_k_g", "sf_w_o", "mla_w_down", "mla_q_a_g",
         "mla_kv_a_g", "mla_w_uq", "mla_w_ukv", "mla_q_g", "mla_k_g", "mla_w_o", "mlp_w_up", "mlp_w_down"]


def _weights_mix0(full, small):
    w_in = full["sf_w_in"]
    d_model = w_in.shape[0]
    n_fx = small["sf_b_f"].shape[1]
    sb_w = (w_in.shape[1] - n_fx) // 6
    cols = lambda i: w_in[:, i * sb_w:(i + 1) * sb_w]
    w_a = jnp.concatenate([cols(0), cols(1), cols(2), cols(5)], axis=1)
    w_b = jnp.concatenate([cols(3), cols(4), w_in[:, 6 * sb_w:], jnp.zeros((d_model, LANES - n_fx), w_in.dtype)], axis=1)
    half = ROPE_DIM // 2
    inv_freq = ROPE_THETA ** (-jnp.arange(half, dtype=F32) / half)
    zeros64 = jnp.zeros((ROPE_DIM,), F32)
    pad256 = lambda g: jnp.pad(g, ((0, 0), (0, MLA_PAD_DIM - MLA_QK_DIM)))
    pad_lanes = lambda g: jnp.pad(g, ((0, 0), (0, LANES - g.shape[1])))
    return dict(
        ln_mix0=small["ln_mix_g"][0:1], ln_mix1=small["ln_mix_g"][1:2],
        ln_mlp0=small["ln_mlp_g"][0:1], ln_mlp1=small["ln_mlp_g"][1:2],
        w_a=w_a, w_b=w_b, b_f=pad_lanes(small["sf_b_f"]), fox_q_g=small["fox_q_g"], fox_k_g=small["fox_k_g"],
        q_a_g=small["mla_q_a_g"], kv_a_g=small["mla_kv_a_g"],
        mla_q_g=pad256(small["mla_q_g"]), mla_k_g=pad256(small["mla_k_g"]),
        inv_freq=jnp.concatenate([inv_freq, inv_freq, zeros64]).reshape(1, LANES),
        sin_sign=jnp.concatenate([-jnp.ones((half,), F32), jnp.ones((half,), F32), zeros64]).reshape(1, LANES),
    )


def _late_weights(full):
    plain = {"sf_w_o": "w_o0", "mlp_w_up0": "w_up0", "mlp_w_down0": "w_dn0", "mla_w_ukv": "w_ukv", "mla_w_o": "w_o1",
             "mlp_w_up1": "w_up1", "mlp_w_down1": "w_dn1"}
    out = {key: full[p] for p, key in plain.items() if p in full}
    if "mla_w_down" in full:
        out["w_down"] = jnp.pad(full["mla_w_down"], ((0, 0), (0, LANES - ROPE_DIM)))
    if "mla_w_uq" in full:
        rank = full["mla_w_uq"].shape[0]
        w_uq = full["mla_w_uq"].reshape(rank, MLA_HEADS, MLA_QK_DIM)
        out["w_uq"] = jnp.pad(w_uq, ((0, 0), (0, 0), (0, MLA_PAD_DIM - MLA_QK_DIM))).reshape(rank, MLA_HEADS * MLA_PAD_DIM)
    return out


def _piece_grad(g, piece):
    if piece == "sf_w_in":
        n_fx = g["b_f"].shape[1]
        ga, gb = g["w_a"], g["w_b"]
        sb_w = ga.shape[1] // 4
        ca = lambda i: ga[:, i * sb_w:(i + 1) * sb_w]
        return jnp.concatenate([ca(0), ca(1), ca(2), gb[:, :sb_w], gb[:, sb_w:2 * sb_w], ca(3),
                                gb[:, 2 * sb_w:2 * sb_w + n_fx]], axis=1)
    if piece == "mla_w_uq":
        rank = g["w_uq"].shape[0]
        return g["w_uq"].reshape(rank, MLA_HEADS, MLA_PAD_DIM)[:, :, :MLA_QK_DIM].reshape(rank, MLA_HEADS * MLA_QK_DIM)
    if piece == "mla_w_down":
        return g["w_down"][:, :g["w_down"].shape[1] - (LANES - ROPE_DIM)]
    return g[{"sf_w_o": "w_o0", "mla_w_ukv": "w_ukv", "mla_w_o": "w_o1", "mlp_w_up0": "w_up0", "mlp_w_up1": "w_up1",
              "mlp_w_down0": "w_dn0", "mlp_w_down1": "w_dn1"}[piece]]


def _small_grads(g):
    return {
        "ln_mix_g": jnp.concatenate([g["ln_mix0"], g["ln_mix1"]], axis=0),
        "ln_mlp_g": jnp.concatenate([g["ln_mlp0"], g["ln_mlp1"]], axis=0),
        "sf_b_f": g["b_f"], "fox_q_g": g["fox_q_g"], "fox_k_g": g["fox_k_g"],
        "mla_q_a_g": g["q_a_g"], "mla_kv_a_g": g["kv_a_g"],
        "mla_q_g": g["mla_q_g"][:, :MLA_QK_DIM], "mla_k_g": g["mla_k_g"][:, :MLA_QK_DIM],
    }


PACK_TILE = 1024


def _as_rows(a, row_multiple=16):
    flat = a.reshape(-1)
    rows = -(-flat.shape[0] // LANES)
    rows = -(-rows // row_multiple) * row_multiple
    return jnp.pad(flat, (0, rows * LANES - flat.shape[0])).reshape(rows, LANES)


def _pack_rows(parts, axis, dtype, row_multiple=PACK_TILE, spare_rows=0):
    used = sum(p.shape[axis] for p in parts)
    shape = list(parts[0].shape)
    shape[axis] = -(-used // row_multiple) * row_multiple + spare_rows - used
    return jnp.concatenate([p.astype(dtype) for p in parts] + [jnp.ones(shape, dtype)], axis=axis)


def _unshard(stack, axis):
    moved = jnp.moveaxis(stack, 0, axis)
    shape = list(stack.shape[1:])
    shape[axis] *= N_DEV
    return moved.reshape(shape)


def _shard_stack(full, axis):
    shape = list(full.shape)
    shape[axis:axis + 1] = [N_DEV, shape[axis] // N_DEV]
    return jnp.moveaxis(full.reshape(shape), axis, 0)


OPT_TILE_ELEMS = 128 * 1024


def _row_tile(rows, cols):
    best = 16
    for t in range(16, rows + 1, 16):
        if rows % t == 0 and t * cols <= OPT_TILE_ELEMS:
            best = t
    assert rows % best == 0
    return best


def _cast_bf16(a, name):
    return _rows_call(lambda v: v, name, [a], [], [(a.shape[1], BF16)], tile=_row_tile(*a.shape))


def _adam_math(w, g, m, v):
    m = ADAM_B1 * m + (1.0 - ADAM_B1) * g
    v = ADAM_B2 * v + (1.0 - ADAM_B2) * jnp.square(g)
    m_hat = m / (1.0 - ADAM_B1 ** ADAM_STEP)
    v_hat = v / (1.0 - ADAM_B2 ** ADAM_STEP)
    delta = -ADAM_LR * (m_hat / (jnp.sqrt(v_hat) + ADAM_EPS) + ADAM_WD * w)
    return delta, m, v


def _adam_big(recvs, w, m, v, name, hosted=()):
    layers, rows, cols = w.shape
    tile = _row_tile(rows, cols)
    n_tiles = rows // tile
    n_host = len(hosted)

    def body(*refs):
        recv_refs = refs[:layers]
        w_ref, m_ref, v_ref = refs[layers:layers + 3]
        base = layers + 3 + n_host
        g_ref, d_ref, nm_ref, nv_ref = refs[base:base + 4]
        layer = pl.program_id(0)
        host_args = (hosted, refs[layers + 3:base], refs[base + 4:base + 4 + n_host], refs[base + 4 + n_host:],
                     layer * n_tiles + pl.program_id(1), layers * n_tiles)
        _run_hosted(*host_args, "start")

        def total(r_ref):
            acc = r_ref[0].astype(F32)
            for s in range(1, N_DEV):
                acc = acc + r_ref[s].astype(F32)
            return acc

        g = total(recv_refs[0])
        for j in range(1, layers):
            g = jnp.where(layer == j, total(recv_refs[j]), g)
        delta, nm, nv = _adam_math(w_ref[...], g, m_ref[...], v_ref[...])
        g_ref[...] = g
        d_ref[...] = delta
        nm_ref[...] = nm
        nv_ref[...] = nv
        _run_hosted(*host_args, "finish")

    def recv_spec(j):
        return pl.BlockSpec((N_DEV, tile, cols),
                            lambda l, i: (0, jnp.where(l == j, i, jnp.where(l < j, 0, n_tiles - 1)), 0))

    spec = pl.BlockSpec((None, tile, cols), lambda l, i: (l, i, 0))
    out = jax.ShapeDtypeStruct(w.shape, F32)
    return pl.pallas_call(
        body, name=name, grid=(layers, n_tiles),
        in_specs=[recv_spec(j) for j in range(layers)] + [spec] * 3 + [ANY_SPEC] * n_host,
        out_specs=[spec] * 4 + [ANY_SPEC] * n_host,
        out_shape=[out] * 4 + [_exchange_out_shape(kd, arr) for kd, arr in hosted],
        scratch_shapes=EXCHANGE_SCRATCH * n_host, compiler_params=_params("arbitrary", "arbitrary"),
    )(*recvs, w, m, v, *[arr for _, arr in hosted])


def _sum_slots(gathered):
    rows = gathered.shape[1]

    def body(r_ref, o_ref):
        acc = r_ref[0]
        for s in range(1, N_DEV):
            acc = acc + r_ref[s]
        o_ref[...] = acc

    return pl.pallas_call(body, name="sum_small", out_shape=jax.ShapeDtypeStruct((rows, LANES), F32))(gathered)


def _adam_small(w, g, m, v):
    def fn(wv, gv, mv, vv):
        return _adam_math(wv, gv, mv, vv)
    return _rows_call(fn, "adam_small", [w, g, m, v], [], [(LANES, F32)] * 3, tile=w.shape[0])


def kernel(x, positions, ln_mix_g, ln_mlp_g, sf_w_in, sf_b_f, fox_q_g, fox_k_g, sf_w_o, mla_w_down, mla_q_a_g, mla_kv_a_g, mla_w_uq, mla_w_ukv, mla_q_g, mla_k_g, mla_w_o, mlp_w_up, mlp_w_down, loss_target, m_ln_mix_g, m_ln_mlp_g, m_sf_w_in, m_sf_b_f, m_fox_q_g, m_fox_k_g, m_sf_w_o, m_mla_w_down, m_mla_q_a_g, m_mla_kv_a_g, m_mla_w_uq, m_mla_w_ukv, m_mla_q_g, m_mla_k_g, m_mla_w_o, m_mlp_w_up, m_mlp_w_down, v_ln_mix_g, v_ln_mlp_g, v_sf_w_in, v_sf_b_f, v_fox_q_g, v_fox_k_g, v_sf_w_o, v_mla_w_down, v_mla_q_a_g, v_mla_kv_a_g, v_mla_w_uq, v_mla_w_ukv, v_mla_q_g, v_mla_k_g, v_mla_w_o, v_mlp_w_up, v_mlp_w_down):
    given = dict(locals())
    wts = {n: given[n] for n in ALL_W}
    mom = {n: given["m_" + n] for n in ALL_W}
    var = {n: given["v_" + n] for n in ALL_W}
    me = 4 * lax.axis_index("x") + 2 * lax.axis_index("y") + lax.axis_index("c")
    t_rows, d_model = x.shape[1], x.shape[2]
    big = sorted({name for name, _, _ in PIECES.values()})

    def whole_pieces(gathered, group):
        return {p: _unshard(s, PIECES[p][2]) for p, s in zip(GROUPS[group], gathered)}

    def w_in_slot_parts(grads):
        slots = _shard_stack(_piece_grad(grads, "sf_w_in"), PIECES["sf_w_in"][2])
        cuts = [0] + [(slots.shape[1] * f // 64) // 16 * 16 for f in (11, 32, 49)] + [slots.shape[1]]
        return [slots[:, a:b] for a, b in zip(cuts, cuts[1:])]

    def slots_of(group, grads):
        if group == "mix0":
            return w_in_slot_parts(grads)[:2]
        return [_shard_stack(_piece_grad(grads, p), PIECES[p][2]) for p in GROUPS[group]]

    cast = {n: _cast_bf16(wts[n].reshape(-1, wts[n].shape[2]), f"cast_{n}").reshape(wts[n].shape) for n in big}
    blocks = {grp: [cast[PIECES[p][0]][PIECES[p][1]] for p in GROUPS[grp]] for grp in ("mix0", "with_stick_fwd", "with_fox_fwd", "mlp1")}
    mix0 = whole_pieces(_exchange("gather", blocks["mix0"], "gather_mix0"), "mix0")
    gains, = _exchange("gather", [_as_rows(jnp.concatenate([mla_q_a_g, mla_kv_a_g], axis=1))], "gather_gains")
    lora_n = mla_q_a_g.shape[1]
    gains_flat = gains.reshape(N_DEV, -1)[:, :2 * lora_n]
    small = dict(ln_mix_g=ln_mix_g, ln_mlp_g=ln_mlp_g, sf_b_f=sf_b_f, fox_q_g=fox_q_g, fox_k_g=fox_k_g,
                 mla_q_a_g=gains_flat[:, :lora_n].reshape(1, -1), mla_kv_a_g=gains_flat[:, lora_n:].reshape(1, -1),
                 mla_q_g=mla_q_g, mla_k_g=mla_k_g)
    dist = dict(blocks=blocks, slots_of=slots_of,
                weights_of=lambda grp, gathered: _late_weights(whole_pieces(gathered, grp)))
    sq_err, grad_x, g, received = _local_step(x[0], positions.reshape(t_rows, 1), loss_target[0],
                                              _weights_mix0(mix0, small), dist)

    recv_of = {p: r for grp in ("mlp1", "with_stick_bwd", "with_fox_bwd") for p, r in zip(GROUPS[grp], received[grp])}
    late_parts = dict(zip(("mlp_w_down", "mlp_w_up"), w_in_slot_parts(g)[2:]))
    w_in_recv = list(received["mix0"])
    results = {kind: {} for kind in ("grad", "delta", "new_m", "new_v")}
    for n in sorted(big, key=lambda name: (name == "sf_w_in", name not in late_parts)):
        if n == "sf_w_in":
            recv_of["sf_w_in"] = jnp.concatenate(w_in_recv, axis=1)
        layers = [p for _, p in sorted((layer, p) for p, (name, layer, _) in PIECES.items() if name == n)]
        hosted = [("all_to_all", late_parts[n])] if n in late_parts else []
        outs = _adam_big([recv_of[p] for p in layers], wts[n], mom[n], var[n], f"adam_{n}", hosted)
        w_in_recv += outs[4:]
        for kind, out in zip(("grad", "delta", "new_m", "new_v"), outs[:4]):
            results[kind][n] = out

    small_g = _small_grads(g)
    small_parts = [_as_rows(small_g[n], 8) for n in SMALL] + [_as_rows(sq_err, 8)]
    small_sum = _sum_slots(_exchange("gather", [_pack_rows(small_parts, 0, F32, 8, 8)], "gather_small_grads")[0])
    red, off = {}, 0
    for n, p in zip(SMALL + ["loss"], small_parts):
        red[n] = small_sum[off:off + p.shape[0]].reshape(-1)
        off += p.shape[0]
    loss = 0.5 * red["loss"][0] / d_model
    for n in SMALL:
        if n in ("mla_q_a_g", "mla_kv_a_g"):
            results["grad"][n] = lax.dynamic_slice(red[n], (me * lora_n,), (lora_n,)).reshape(wts[n].shape)
        else:
            results["grad"][n] = red[n][:wts[n].size].reshape(wts[n].shape)
    pack_small = lambda d: jnp.concatenate([_as_rows(d[n], 8) for n in SMALL], axis=0)
    small_out = _adam_small(pack_small(wts), pack_small(results["grad"]), pack_small(mom), pack_small(var))
    off = 0
    for n in SMALL:
        r = _as_rows(wts[n], 8).shape[0]
        for kind, packed in zip(["delta", "new_m", "new_v"], small_out):
            results[kind][n] = packed[off:off + r].reshape(-1)[:wts[n].size].reshape(wts[n].shape)
        off += r

    outs = [loss, grad_x[None]]
    for kind in ["grad", "delta", "new_m", "new_v"]:
        outs += [results[kind][n] for n in ALL_W]
    return tuple(outs)
```

```python
import functools
import math

import jax
import jax.numpy as jnp
import numpy as np
from jax import lax
from jax.experimental import pallas as pl
from jax.experimental.pallas import tpu as pltpu

F32 = jnp.float32
BF16 = jnp.bfloat16

NORM_EPS = 1e-6
ROPE_THETA = 10000.0
HEAD_DIM = 128
ROPE_DIM = 64
MLA_HEADS = 16
MLA_QK_DIM = 192
MLA_PAD_DIM = 256
ADAM_LR, ADAM_B1, ADAM_B2, ADAM_EPS, ADAM_WD, ADAM_STEP = 0.001, 0.9, 0.999, 1e-08, 0.01, 10

N_DEV = 8
LANES = 128
VMEM_LIMIT = 56 * 1024 * 1024
MATMUL_VMEM_BUDGET = 40 * 1024 * 1024
MASKED = -1e30
LOG2E = 1.4426950408889634
ATTN_TQ, ATTN_TK = 256, 256
ATTN_TQ_WIDE = 512
ATTN_TK_SOFTMAX = 512
MESH = pl.DeviceIdType.MESH

NT_DIMS = (((1,), (1,)), ((), ()))
TN_DIMS = (((0,), (0,)), ((), ()))
NN_DIMS = (((1,), (0,)), ((), ()))


def _params(*sem):
    return pltpu.CompilerParams(dimension_semantics=sem, vmem_limit_bytes=VMEM_LIMIT)


def _pick(n, pref):
    best = None
    for t in range(LANES, min(n, pref) + 1, LANES):
        if n % t == 0:
            best = t
    return n if best is None or 2 * best < min(n, pref) else best


def _rows_call(fn, name, row_ins, full_ins, row_outs, acc_outs=(), tile=256):
    row_ins = [r if isinstance(r, tuple) else (r, r.shape[1], 0) for r in row_ins]
    t_rows = row_ins[0][0].shape[0]
    assert t_rows % tile == 0
    n_in = len(row_ins) + len(full_ins)
    n_row_out = len(row_outs)

    def body(*refs):
        res = fn(*[r[...] for r in refs[:n_in]])
        res = res if isinstance(res, tuple) else (res,)
        for ref, val in zip(refs[n_in:n_in + n_row_out], res[:n_row_out]):
            ref[...] = val.astype(ref.dtype)
        acc_refs = refs[n_in + n_row_out:]
        if acc_refs:
            @pl.when(pl.program_id(0) == 0)
            def _():
                for ref in acc_refs:
                    ref[...] = jnp.zeros_like(ref)
            for ref, val in zip(acc_refs, res[n_row_out:]):
                ref[...] += val.astype(ref.dtype)

    in_specs = [pl.BlockSpec((tile, w), functools.partial(lambda i, cb: (i, cb), cb=cb)) for _, w, cb in row_ins]
    in_specs += [pl.BlockSpec(a.shape, lambda i: (0, 0)) for a in full_ins]
    out_specs = [pl.BlockSpec((tile, c), lambda i: (i, 0)) for c, _ in row_outs]
    out_specs += [pl.BlockSpec(s, lambda i: (0, 0)) for s, _ in acc_outs]
    out_shape = [jax.ShapeDtypeStruct((t_rows, c), d) for c, d in row_outs]
    out_shape += [jax.ShapeDtypeStruct(s, d) for s, d in acc_outs]
    outs = pl.pallas_call(
        body, name=name, grid=(t_rows // tile,), in_specs=in_specs, out_specs=out_specs, out_shape=out_shape,
        compiler_params=_params("arbitrary"),
    )(*[r[0] for r in row_ins], *full_ins)
    return outs[0] if len(outs) == 1 else tuple(outs)


def _matmul_tiles(m, n, k, in_bytes, out_bytes):
    tn = n if n <= 1280 else _pick(n, 1024)
    tks = [k] + [k // d for d in (2, 4, 8, 16) if k % (d * LANES) == 0]
    for tk in [t for t in tks if t <= 4096] or [tks[-1]]:
        for tm in (1024, 512, 256):
            if m % tm:
                continue
            acc = 2 * tm * tn * 4 if tk < k else tm * tn * 4
            if 2 * (tm * tk + tk * tn) * in_bytes + 2 * tm * tn * out_bytes + acc <= MATMUL_VMEM_BUDGET:
                return tm, tn, tk
    raise ValueError(f"no matmul tiling for {m}x{n}x{k}")


def _matmul(a, b, form, name, out_dtypes=(F32,), epilogue=None, extras=(), hosted=(), column_slots=False):
    if form == "nn":
        (m, k), n = a.shape, b.shape[1]
    elif form == "nt":
        (m, k), n = a.shape, b.shape[0]
    else:
        (k, m), n = a.shape, b.shape[1]
    in_bytes = max(a.dtype.itemsize, b.dtype.itemsize)
    out_bytes = sum(jnp.dtype(d).itemsize for d in out_dtypes) + sum(e.dtype.itemsize for e in extras)
    tm, tn, tk = _matmul_tiles(m, n, k, in_bytes, out_bytes)
    nk = k // tk
    dims = {"nn": NN_DIMS, "nt": NT_DIMS, "tn": TN_DIMS}[form]
    n_extra, n_out, n_host = len(extras), len(out_dtypes), len(hosted)
    grid = (m // tm, n // tn, nk)

    def body(*refs):
        a_ref, b_ref = refs[0], refs[1]
        extra_refs = refs[2:2 + n_extra]
        base = 2 + n_extra + n_host
        out_refs = refs[base:base + n_out]
        sems_at = base + n_out + n_host
        step = (pl.program_id(0) * grid[1] + pl.program_id(1)) * nk + pl.program_id(2)
        host_args = (hosted, refs[2 + n_extra:base], refs[base + n_out:sems_at], refs[sems_at:sems_at + 3 * n_host],
                     step, grid[0] * grid[1] * nk)
        _run_hosted(*host_args, "start")

        def finish(acc):
            vals = (acc,) if epilogue is None else epilogue(acc, *[r[...] for r in extra_refs])
            for ref, val in zip(out_refs, vals):
                ref[...] = val.astype(ref.dtype)

        part = lax.dot_general(a_ref[...].astype(BF16), b_ref[...].astype(BF16), dims, preferred_element_type=F32)
        if nk == 1:
            finish(part)
        else:
            acc_ref = refs[-1]
            kk = pl.program_id(2)

            @pl.when(kk == 0)
            def _():
                acc_ref[...] = part

            @pl.when(kk > 0)
            def _():
                acc_ref[...] += part

            @pl.when(kk == nk - 1)
            def _():
                finish(acc_ref[...])
        _run_hosted(*host_args, "finish")

    a_spec = pl.BlockSpec((tk, tm), lambda i, j, kk: (kk, i)) if form == "tn" else pl.BlockSpec((tm, tk), lambda i, j, kk: (i, kk))
    b_spec = pl.BlockSpec((tn, tk), lambda i, j, kk: (j, kk)) if form == "nt" else pl.BlockSpec((tk, tn), lambda i, j, kk: (kk, j))
    o_spec = pl.BlockSpec((tm, tn), lambda i, j, kk: (i, j))
    out_shapes = [jax.ShapeDtypeStruct((m, n), d) for d in out_dtypes]
    out_specs = [o_spec] * n_out
    if column_slots:
        assert not extras
        out_shapes = [jax.ShapeDtypeStruct((n // tn, m, tn), d) for d in out_dtypes]
        out_specs = [pl.BlockSpec((None, tm, tn), lambda i, j, kk: (j, i, 0))] * n_out
    outs = pl.pallas_call(
        body, name=name, grid=grid, in_specs=[a_spec, b_spec] + [o_spec] * n_extra + [ANY_SPEC] * n_host,
        out_specs=out_specs + [ANY_SPEC] * n_host,
        out_shape=out_shapes + [_exchange_out_shape(kd, arr) for kd, arr in hosted],
        scratch_shapes=EXCHANGE_SCRATCH * n_host + ([pltpu.VMEM((tm, tn), F32)] if nk > 1 else []),
        compiler_params=_params(*(("arbitrary",) * 3 if n_host else ("parallel", "parallel", "arbitrary"))),
    )(a, b, *extras, *[arr for _, arr in hosted])
    return outs[0] if n_out + n_host == 1 else tuple(outs)


def _add_residual(acc, res):
    return (acc + res,)


def _log_sigmoid_parts(z):
    return jnp.log1p(jnp.exp(-jnp.abs(z)))


def _log2_gates(z2):
    lg = jnp.log2(1.0 + jnp.exp2(-jnp.abs(z2)))
    log_beta = jnp.minimum(z2, 0.0) - lg
    return log_beta, log_beta - z2


def _rms_fwd(x, g, n=None):
    n = x.shape[-1] if n is None else n
    r = lax.rsqrt(jnp.sum(x * x, axis=-1, keepdims=True) / n + NORM_EPS)
    return x * r * g


def _rms_bwd(x, g, dout, n=None):
    n = x.shape[-1] if n is None else n
    r = lax.rsqrt(jnp.sum(x * x, axis=-1, keepdims=True) / n + NORM_EPS)
    y = x * r
    dg = jnp.sum(dout * y, axis=0, keepdims=True)
    dy = dout * g
    dx = r * (dy - y * (jnp.sum(dy * y, axis=-1, keepdims=True) / n))
    return dx, dg


def _swap_halves(r):
    lane = lax.broadcasted_iota(jnp.int32, r.shape, 1)
    return jnp.where(lane < ROPE_DIM // 2, pltpu.roll(r, LANES - ROPE_DIM // 2, 1), pltpu.roll(r, ROPE_DIM // 2, 1))


def _rope_fwd(r, cos_t, sin_s):
    return r * cos_t + _swap_halves(r) * sin_s


def _rope_bwd(dr, cos_t, sin_s):
    return dr * cos_t + _swap_halves(dr * sin_s)


def _split3(x):
    hi = x.astype(BF16)
    r1 = x - hi.astype(F32)
    mid = r1.astype(BF16)
    lo = (r1 - mid.astype(F32)).astype(BF16)
    return hi, mid, lo


def _mesh_position():
    x, y, c = lax.axis_index("x"), lax.axis_index("y"), lax.axis_index("c")
    return x, y, c, 4 * x + 2 * y + c


def _peer(x, y, c, k):
    bx, by, bc = (k >> 2) & 1, (k >> 1) & 1, k & 1
    px, py, pc = x ^ bx, y ^ by, c ^ bc
    return (px, py, pc), 4 * px + 2 * py + pc


def _gather_steps(x_ref, out_ref, send_sems, recv_sems, local_sem):
    x, y, c, me = _mesh_position()
    sibling = (x, y, 1 - c)
    chips = [(1 - x, y), (x, 1 - y), (1 - x, 1 - y)]

    def slot(px, py, pc):
        return out_ref.at[4 * px + 2 * py + pc]

    def copy(k, blk, to, src=None):
        return pltpu.make_async_remote_copy(
            src_ref=slot(*blk) if src is None else src, dst_ref=slot(*blk), send_sem=send_sems.at[k],
            recv_sem=recv_sems.at[k], device_id=to, device_id_type=MESH)

    mine = pltpu.make_async_copy(x_ref, out_ref.at[me], local_sem)
    first = [copy(0, (x, y, c), sibling, src=x_ref)]
    first += [copy(1 + j, (x, y, c), (*chip, c), src=x_ref) for j, chip in enumerate(chips)]
    passed = [copy(4 + j, (*chip, c), sibling) for j, chip in enumerate(chips)]

    def start():
        mine.start()
        for cp in first:
            cp.start()

    def forward():
        for j, chip in enumerate(chips):
            copy(1 + j, (*chip, c), (x, y, c)).wait_recv()
            passed[j].start()

    def finish():
        copy(0, (x, y, 1 - c), (x, y, c)).wait_recv()
        for j, chip in enumerate(chips):
            copy(4 + j, (*chip, 1 - c), (x, y, c)).wait_recv()
        for cp in first + passed:
            cp.wait_send()
        mine.wait()

    return start, forward, finish


def _all_to_all_steps(g_ref, out_ref, send_sems, recv_sems, local_sem):
    x, y, c, me = _mesh_position()
    mine = pltpu.make_async_copy(g_ref.at[me], out_ref.at[me], local_sem)
    copies = []
    for k in range(1, N_DEV):
        peer, peer_idx = _peer(x, y, c, k)
        copies.append(pltpu.make_async_remote_copy(
            src_ref=g_ref.at[peer_idx], dst_ref=out_ref.at[me], send_sem=send_sems.at[k - 1],
            recv_sem=recv_sems.at[k - 1], device_id=peer, device_id_type=MESH))

    def start():
        mine.start()
        for cp in copies:
            cp.start()

    def finish():
        for k in range(1, N_DEV):
            peer, peer_idx = _peer(x, y, c, k)
            pltpu.make_async_remote_copy(
                src_ref=g_ref.at[me], dst_ref=out_ref.at[peer_idx], send_sem=send_sems.at[k - 1],
                recv_sem=recv_sems.at[k - 1], device_id=peer, device_id_type=MESH).wait_recv()
        for cp in copies:
            cp.wait_send()
        mine.wait()

    return start, None, finish


EXCHANGE_STEPS = {"gather": _gather_steps, "all_to_all": _all_to_all_steps}
EXCHANGE_SCRATCH = [pltpu.SemaphoreType.DMA((7,)), pltpu.SemaphoreType.DMA((7,)), pltpu.SemaphoreType.DMA]
ANY_SPEC = pl.BlockSpec(memory_space=pl.ANY)


def _exchange_out_shape(kind, arr):
    return jax.ShapeDtypeStruct(((N_DEV,) + arr.shape) if kind == "gather" else arr.shape, arr.dtype)


def _exchange(kind, arrs, name):
    n = len(arrs)

    def body(*refs):
        steps = [EXCHANGE_STEPS[kind](refs[i], refs[n + i], *refs[2 * n + 3 * i:2 * n + 3 * i + 3]) for i in range(n)]
        for start, _, _ in steps:
            start()
        for _, forward, _ in steps:
            if forward is not None:
                forward()
        for _, _, finish in steps:
            finish()

    return pl.pallas_call(body, name=name, out_shape=[_exchange_out_shape(kind, a) for a in arrs],
                          in_specs=[ANY_SPEC] * n, out_specs=[ANY_SPEC] * n, scratch_shapes=EXCHANGE_SCRATCH * n)(*arrs)


def _run_hosted(hosted, src_refs, dst_refs, sem_refs, step, n_steps, when):
    for idx, (kind, _) in enumerate(hosted):
        start, forward, finish = EXCHANGE_STEPS[kind](src_refs[idx], dst_refs[idx], *sem_refs[3 * idx:3 * idx + 3])
        if when == "start":
            pl.when(step == 0)(start)
            if forward is not None:
                pl.when(step == (3 * n_steps) // 4)(forward)
        else:
            pl.when(step == n_steps - 1)(finish)


def _causal_iotas(qi, tq, tk):
    row = qi * tq + lax.broadcasted_iota(jnp.int32, (tq, tk), 0)
    col = lax.broadcasted_iota(jnp.int32, (tq, tk), 1)
    return row, col


def _suffix_matrix(tk, inclusive):
    j = lax.broadcasted_iota(jnp.int32, (2 * tk, tk), 0) % tk
    s = lax.broadcasted_iota(jnp.int32, (2 * tk, tk), 1)
    return jnp.where((j >= s) if inclusive else (j > s), 1.0, 0.0).astype(BF16)


def _suffix_sum(x, mat):
    hi = x.astype(BF16)
    lo = (x - hi.astype(F32)).astype(BF16)
    return lax.dot_general(jnp.concatenate([hi, lo], axis=1), mat, NN_DIMS, preferred_element_type=F32)


def _attn_specs(t_rows, tq, heads, dk, dv, q_off, k_off, v_off):
    q_spec = pl.BlockSpec((tq, dk), lambda h, i: (i, q_off + h))
    kt_spec = pl.BlockSpec((dk, t_rows), lambda h, i: (k_off + h, 0))
    v_spec = pl.BlockSpec((t_rows, dv), lambda h, i: (0, v_off + h))
    return q_spec, kt_spec, v_spec


def _split_weights(weights, fine):
    hi = weights.astype(BF16)
    return (hi, (weights - hi.astype(F32)).astype(BF16)) if fine else (hi,)


def _weighted_values(split, v):
    return sum(lax.dot_general(part, v, NN_DIMS, preferred_element_type=F32) for part in split)


def _attn_fwd(kind, q_arr, kt_arr, v_arr, heads, dk, dv, scale, name, q_off=0, k_off=0, v_off=0, fcol=None, frow=None,
              tq=ATTN_TQ, tk=ATTN_TK, fine=True, hosted=()):
    t_rows = q_arr.shape[0]
    tq, tk = min(tq, t_rows), min(tk, t_rows)
    nq = t_rows // tq
    stick = kind == "stick"
    decay = fcol is not None
    n_in = 5 if decay else 3
    n_out = 2 if stick else 3
    n_host = len(hosted)

    def body(*refs):
        q_ref, kt_ref, v_ref = refs[:3]
        fcol_ref, frow_ref = (refs[3], refs[4]) if decay else (None, None)
        base = n_in + n_host
        o_ref, fine_ref = refs[base], refs[base + 1]
        lse_ref = None if stick else refs[base + 2]
        host_args = (hosted, refs[n_in:base], refs[base + n_out:base + n_out + n_host], refs[base + n_out + n_host:],
                     pl.program_id(0) * nq + pl.program_id(1), heads * nq)
        _run_hosted(*host_args, "start")
        qi = pl.program_id(1)
        q = q_ref[...]
        row, col = _causal_iotas(qi, tq, tk)
        n_kb = ((qi + 1) * tq + tk - 1) // tk
        n_diag = max(1, tq // tk)
        zeros_o = jnp.zeros((tq, dv), F32)

        no_weights = (jnp.zeros((tq, tk), BF16),) * (2 if fine else 1)

        def raw_logits(kb):
            return lax.dot_general(q, kt_ref[:, pl.ds(pl.multiple_of(kb * tk, tk), tk)], NN_DIMS,
                                   preferred_element_type=F32)

        def values(kb):
            return v_ref[pl.ds(pl.multiple_of(kb * tk, tk), tk), :]

        if stick:
            mat = _suffix_matrix(tk, inclusive=False)

            def make_step(masked):
                def step(i, carry):
                    c, acc, raw, prev = carry
                    raw_next = raw_logits(jnp.maximum(n_kb - 2 - i, 0))
                    d_acc = _weighted_values(prev, values(jnp.minimum(n_kb - i, n_kb - 1)))
                    log_beta, lom = _log2_gates(raw * (scale * LOG2E))
                    if masked:
                        strict = (col + (n_kb - 1 - i) * tk) < row
                        lom = jnp.where(strict, lom, 0.0)
                    w = jnp.exp2(log_beta + (_suffix_sum(lom, mat) + c))
                    if masked:
                        w = jnp.where(strict, w, 0.0)
                    return c + jnp.sum(lom, axis=1, keepdims=True), acc + d_acc, raw_next, _split_weights(w, fine)
                return step

            carry = (jnp.zeros((tq, 1), F32), zeros_o, raw_logits(n_kb - 1), no_weights)
            for i in range(n_diag):
                carry = make_step(True)(i, carry)
            _, acc, _, last = lax.fori_loop(n_diag, n_kb, make_step(False), carry)
            acc = acc + _weighted_values(last, values(0))
            o_ref[...] = acc.astype(o_ref.dtype)
            fine_ref[...] = acc
        else:
            fc = fcol_ref[...] * LOG2E if decay else None

            def make_step(masked):
                def step(kb, carry):
                    m, l, acc, raw, prev = carry
                    raw_next = raw_logits(jnp.minimum(kb + 1, n_kb - 1))
                    d_acc = _weighted_values(prev, values(jnp.maximum(kb - 1, 0)))
                    ks = pl.multiple_of(kb * tk, tk)
                    s = raw * (scale * LOG2E)
                    if decay:
                        s = (s + fc) - frow_ref[:, pl.ds(ks, tk)] * LOG2E
                    if masked:
                        s = jnp.where((col + ks) <= row, s, MASKED)
                    m_new = jnp.maximum(m, jnp.max(s, axis=1, keepdims=True))
                    alpha = jnp.exp2(m - m_new)
                    p = jnp.exp2(s - m_new)
                    l = alpha * l + jnp.sum(p, axis=1, keepdims=True)
                    return m_new, l, alpha * (acc + d_acc), raw_next, _split_weights(p, fine)
                return step

            carry = lax.fori_loop(0, n_kb - n_diag, make_step(False),
                                  (jnp.full((tq, 1), MASKED, F32), jnp.zeros((tq, 1), F32), zeros_o, raw_logits(0),
                                   no_weights))
            for d in range(n_diag):
                carry = make_step(True)(n_kb - n_diag + d, carry)
            m, l, acc, _, last = carry
            out = (acc + _weighted_values(last, values(n_kb - 1))) * (1.0 / l)
            o_ref[...] = out.astype(o_ref.dtype)
            fine_ref[...] = out
            lse_ref[...] = (m + jnp.log2(l)) * (1.0 / LOG2E)
        _run_hosted(*host_args, "finish")

    q_spec, k_spec, v_spec = _attn_specs(t_rows, tq, heads, dk, dv, q_off, k_off, v_off)
    stat_spec = pl.BlockSpec((None, tq, 1), lambda h, i: (h, i, 0))
    ins, in_specs = [q_arr, kt_arr, v_arr], [q_spec, k_spec, v_spec]
    if decay:
        ins += [fcol, frow]
        in_specs += [stat_spec, pl.BlockSpec((None, 1, t_rows), lambda h, i: (h, 0, 0))]
    o_spec = pl.BlockSpec((tq, dv), lambda h, i: (i, h))
    out_specs = [o_spec, o_spec]
    out_shape = [jax.ShapeDtypeStruct((t_rows, heads * dv), BF16), jax.ShapeDtypeStruct((t_rows, heads * dv), F32)]
    if not stick:
        out_specs.append(stat_spec)
        out_shape.append(jax.ShapeDtypeStruct((heads, t_rows, 1), F32))
    return tuple(pl.pallas_call(
        body, name=name, grid=(heads, nq), in_specs=in_specs + [ANY_SPEC] * n_host,
        out_specs=out_specs + [ANY_SPEC] * n_host,
        out_shape=out_shape + [_exchange_out_shape(kd, arr) for kd, arr in hosted],
        scratch_shapes=EXCHANGE_SCRATCH * n_host,
        compiler_params=_params("arbitrary" if n_host else "parallel", "arbitrary"),
    )(*ins, *[arr for _, arr in hosted]))


def _attn_bwd(kind, q_arr, k_arr, kt_arr, vt_arr, o_arr, do_arr, heads, dk, dv, scale, name, q_off=0, k_off=0, kt_off=0,
              vt_off=0, do_off=0, lse=None, fcol=None, frow=None, tq=ATTN_TQ, tk=ATTN_TK, hosted=()):
    t_rows = q_arr.shape[0]
    tq, tk = min(tq, t_rows), min(tk, t_rows)
    nq = t_rows // tq
    stick = kind == "stick"
    decay = fcol is not None
    n_in = 6 + (0 if stick else 1) + (2 if decay else 0)
    n_out = 5 if decay else 3
    n_host = len(hosted)

    def body(*refs):
        q_ref, k_ref, kt_ref, vt_ref, o_ref, do_ref = refs[:6]
        lse_ref = None if stick else refs[6]
        fcol_ref, frow_ref = (refs[7], refs[8]) if decay else (None, None)
        base = n_in + n_host
        dq_ref, dk_ref, dv_ref = refs[base:base + 3]
        dfcol_ref, dfrow_ref = (refs[base + 3], refs[base + 4]) if decay else (None, None)
        host_args = (hosted, refs[n_in:base], refs[base + n_out:base + n_out + n_host], refs[base + n_out + n_host:],
                     pl.program_id(0) * nq + pl.program_id(1), heads * nq)
        _run_hosted(*host_args, "start")
        qi = pl.program_id(1)

        @pl.when(qi == 0)
        def _():
            dk_ref[...] = jnp.zeros_like(dk_ref)
            dv_ref[...] = jnp.zeros_like(dv_ref)
            if decay:
                dfrow_ref[...] = jnp.zeros_like(dfrow_ref)

        q = q_ref[...]
        do = do_ref[...]
        delta = jnp.sum(do.astype(F32) * o_ref[...], axis=1, keepdims=True)
        row, col = _causal_iotas(qi, tq, tk)
        n_kb = ((qi + 1) * tq + tk - 1) // tk
        n_diag = max(1, tq // tk)

        no_pair = (jnp.zeros((tq, tk), BF16), jnp.zeros((tq, tk), BF16))

        def accumulate(kb, pair):
            at = pl.ds(pl.multiple_of(kb * tk, tk), tk)
            dk_ref[at, :] += lax.dot_general(pair[0], q, TN_DIMS, preferred_element_type=F32)
            dv_ref[at, :] += lax.dot_general(pair[1], do, TN_DIMS, preferred_element_type=F32)
            return lax.dot_general(pair[0], k_ref[at, :], NN_DIMS, preferred_element_type=F32)

        def raw_logits(kb):
            return lax.dot_general(q, kt_ref[:, pl.ds(pl.multiple_of(kb * tk, tk), tk)], NN_DIMS,
                                   preferred_element_type=F32)

        def d_weights(kb):
            return lax.dot_general(do, vt_ref[:, pl.ds(pl.multiple_of(kb * tk, tk), tk)], NN_DIMS,
                                   preferred_element_type=F32)

        if stick:
            mat_ex = _suffix_matrix(tk, inclusive=False)
            mat_in = _suffix_matrix(tk, inclusive=True)

            def make_step(masked):
                def step(i, carry):
                    c, gs, dq, raw, prev = carry
                    raw_next = raw_logits(jnp.maximum(n_kb - 2 - i, 0))
                    dw = d_weights(n_kb - 1 - i)
                    dq = dq + accumulate(jnp.minimum(n_kb - i, n_kb - 1), prev)
                    log_beta, log_omb = _log2_gates(raw * (scale * LOG2E))
                    lom = log_omb
                    if masked:
                        strict = (col + (n_kb - 1 - i) * tk) < row
                        lom = jnp.where(strict, log_omb, 0.0)
                    w = jnp.exp2(log_beta + (_suffix_sum(lom, mat_ex) + c))
                    if masked:
                        w = jnp.where(strict, w, 0.0)
                    g = w * dw
                    g_before = delta - (gs + _suffix_sum(g, mat_in))
                    dz = g * jnp.exp2(log_omb) - g_before * jnp.exp2(log_beta)
                    if masked:
                        dz = jnp.where(strict, dz, 0.0)
                    return (c + jnp.sum(lom, axis=1, keepdims=True), gs + jnp.sum(g, axis=1, keepdims=True), dq,
                            raw_next, ((dz * scale).astype(BF16), w.astype(BF16)))
                return step

            zero = jnp.zeros((tq, 1), F32)
            carry = (zero, zero, jnp.zeros((tq, dk), F32), raw_logits(n_kb - 1), no_pair)
            for i in range(n_diag):
                carry = make_step(True)(i, carry)
            _, _, dq, _, last = lax.fori_loop(n_diag, n_kb, make_step(False), carry)
            dq = dq + accumulate(0, last)
        else:
            lse_v = lse_ref[...] * LOG2E
            fc = fcol_ref[...] * LOG2E if decay else None

            def make_step(masked):
                def step(kb, carry):
                    dq, row_sum, raw, prev = carry
                    raw_next = raw_logits(jnp.minimum(kb + 1, n_kb - 1))
                    dp = d_weights(kb)
                    dq = dq + accumulate(jnp.maximum(kb - 1, 0), prev)
                    ks = pl.multiple_of(kb * tk, tk)
                    s = raw * (scale * LOG2E)
                    if decay:
                        s = (s + fc) - frow_ref[:, pl.ds(ks, tk)] * LOG2E
                    p = jnp.exp2(s - lse_v)
                    if masked:
                        p = jnp.where((col + ks) <= row, p, 0.0)
                    ds = p * (dp - delta)
                    if decay:
                        dfrow_ref[:, pl.ds(ks, tk)] += jnp.sum(ds, axis=0, keepdims=True)
                        row_sum = row_sum + jnp.sum(ds, axis=1, keepdims=True)
                    return dq, row_sum, raw_next, ((ds * scale).astype(BF16), p.astype(BF16))
                return step

            carry = lax.fori_loop(0, n_kb - n_diag, make_step(False),
                                  (jnp.zeros((tq, dk), F32), jnp.zeros((tq, 1), F32), raw_logits(0), no_pair))
            for d in range(n_diag):
                carry = make_step(True)(n_kb - n_diag + d, carry)
            dq, row_sum, _, last = carry
            dq = dq + accumulate(n_kb - 1, last)
            if decay:
                dfcol_ref[...] = row_sum
        dq_ref[...] = dq
        _run_hosted(*host_args, "finish")

    q_spec, kt_spec, _ = _attn_specs(t_rows, tq, heads, dk, dv, q_off, kt_off, 0)
    stat_spec = pl.BlockSpec((None, tq, 1), lambda h, i: (h, i, 0))
    frow_spec = pl.BlockSpec((None, 1, t_rows), lambda h, i: (h, 0, 0))
    ins = [q_arr, k_arr, kt_arr, vt_arr, o_arr, do_arr]
    in_specs = [q_spec, pl.BlockSpec((t_rows, dk), lambda h, i: (0, k_off + h)), kt_spec,
                pl.BlockSpec((dv, t_rows), lambda h, i: (vt_off + h, 0)), pl.BlockSpec((tq, dv), lambda h, i: (i, h)),
                pl.BlockSpec((tq, dv), lambda h, i: (i, do_off + h))]
    if not stick:
        ins.append(lse)
        in_specs.append(stat_spec)
    if decay:
        ins += [fcol, frow]
        in_specs += [stat_spec, frow_spec]
    out_specs = [pl.BlockSpec((tq, dk), lambda h, i: (i, h)), pl.BlockSpec((t_rows, dk), lambda h, i: (0, h)),
                 pl.BlockSpec((t_rows, dv), lambda h, i: (0, h))]
    out_shape = [jax.ShapeDtypeStruct((t_rows, heads * dk), F32), jax.ShapeDtypeStruct((t_rows, heads * dk), F32),
                 jax.ShapeDtypeStruct((t_rows, heads * dv), F32)]
    if decay:
        out_specs += [stat_spec, frow_spec]
        out_shape += [jax.ShapeDtypeStruct((heads, t_rows, 1), F32), jax.ShapeDtypeStruct((heads, 1, t_rows), F32)]
    return pl.pallas_call(
        body, name=name, grid=(heads, nq), in_specs=in_specs + [ANY_SPEC] * n_host,
        out_specs=out_specs + [ANY_SPEC] * n_host,
        out_shape=out_shape + [_exchange_out_shape(kd, arr) for kd, arr in hosted],
        scratch_shapes=EXCHANGE_SCRATCH * n_host,
        compiler_params=_params("arbitrary" if n_host else "parallel", "arbitrary"),
    )(*ins, *[arr for _, arr in hosted])


def _prefix_matrix(reverse):
    j = lax.broadcasted_iota(jnp.int32, (LANES, LANES), 0)
    s = lax.broadcasted_iota(jnp.int32, (LANES, LANES), 1)
    return jnp.where((j >= s) if reverse else (j <= s), 1.0, 0.0).astype(BF16)


def _chunk_cumsum(x, mat):
    return sum(lax.dot_general(part, mat, NN_DIMS, preferred_element_type=F32) for part in _split3(x))


def _gate_fwd(logit_t, bias_col):
    heads, t_rows = logit_t.shape

    def body(x_ref, b_ref, out_ref):
        mat = _prefix_matrix(reverse=False)

        def step(ci, carry):
            cs = pl.multiple_of(ci * LANES, LANES)
            pre = x_ref[:, pl.ds(cs, LANES)] + b_ref[...]
            log_f = jnp.minimum(pre, 0.0) - _log_sigmoid_parts(pre)
            out_ref[:, pl.ds(cs, LANES)] = _chunk_cumsum(log_f, mat) + carry
            return carry + jnp.sum(log_f, axis=1, keepdims=True)

        lax.fori_loop(0, t_rows // LANES, step, jnp.zeros((heads, 1), F32))

    return pl.pallas_call(body, name="gate_fwd", out_shape=jax.ShapeDtypeStruct((heads, t_rows), F32),
                          compiler_params=pltpu.CompilerParams(vmem_limit_bytes=VMEM_LIMIT))(logit_t, bias_col)


def _gate_bwd(dcum_t, logit_t, bias_col):
    heads, t_rows = logit_t.shape
    n_chunks = t_rows // LANES

    def body(d_ref, x_ref, b_ref, dx_ref, db_ref):
        mat = _prefix_matrix(reverse=True)

        def step(i, carry):
            tail, db = carry
            cs = pl.multiple_of((n_chunks - 1 - i) * LANES, LANES)
            d = d_ref[:, pl.ds(cs, LANES)]
            d_log_f = _chunk_cumsum(d, mat) + tail
            pre = x_ref[:, pl.ds(cs, LANES)] + b_ref[...]
            e = jnp.exp(-jnp.abs(pre))
            d_pre = d_log_f * (jnp.where(pre >= 0.0, e, 1.0) / (1.0 + e))
            dx_ref[:, pl.ds(cs, LANES)] = d_pre
            return tail + jnp.sum(d, axis=1, keepdims=True), db + jnp.sum(d_pre, axis=1, keepdims=True)

        zero = jnp.zeros((heads, 1), F32)
        _, db = lax.fori_loop(0, n_chunks, step, (zero, zero))
        db_ref[...] = db

    return pl.pallas_call(body, name="gate_bwd",
                          out_shape=(jax.ShapeDtypeStruct((heads, t_rows), F32), jax.ShapeDtypeStruct((heads, 1), F32)),
                          compiler_params=pltpu.CompilerParams(vmem_limit_bytes=VMEM_LIMIT))(dcum_t, logit_t, bias_col)


def _norm_fwd(x, g, name):
    return _rows_call(lambda xv, gv: _rms_fwd(xv, gv), name, [x], [g], [(x.shape[1], BF16)])


def _norm_bwd(x, g, dh, dres, name):
    def fn(xv, dhv, dresv, gv):
        dx, dg = _rms_bwd(xv, gv, dhv)
        dx = dresv + dx
        return dx, dx, dg
    return _rows_call(fn, name, [x, dh, dres], [g], [(x.shape[1], F32), (x.shape[1], BF16)], [((1, x.shape[1]), F32)])


def _loss_fwd_bwd(y, target):
    d_model = y.shape[1]

    def fn(yv, tv):
        err = yv - tv
        dy = err * (1.0 / d_model)
        return dy, dy, jnp.sum(jnp.sum(err * err, axis=1, keepdims=True), axis=0, keepdims=True)
    return _rows_call(fn, "loss", [y, target], [], [(d_model, F32), (d_model, BF16)], [((1, 1), F32)])


def _heads_apply(fn, n_heads, width, *tiles):
    return [fn(*[t[:, h * width:(h + 1) * width] for t in tiles]) for h in range(n_heads)]


def _fox_norm_fwd(pb, gq, gk, heads):
    width = heads * HEAD_DIM

    def fn(qk, gqv, gkv):
        q = jnp.concatenate(_heads_apply(lambda t: _rms_fwd(t, gqv), heads, HEAD_DIM, qk[:, :width]), axis=1)
        k = jnp.concatenate(_heads_apply(lambda t: _rms_fwd(t, gkv), heads, HEAD_DIM, qk[:, width:]), axis=1)
        return q, k
    return _rows_call(fn, "fox_norm_fwd", [(pb, 2 * width, 0)], [gq, gk], [(width, BF16), (width, BF16)])


def _fox_norm_bwd(pb, gq, gk, dq, dk, heads):
    width = heads * HEAD_DIM

    def fn(qk, dqv, dkv, gqv, gkv):
        res_q = _heads_apply(lambda t, d: _rms_bwd(t, gqv, d), heads, HEAD_DIM, qk[:, :width], dqv)
        res_k = _heads_apply(lambda t, d: _rms_bwd(t, gkv, d), heads, HEAD_DIM, qk[:, width:], dkv)
        dqk = jnp.concatenate([r[0] for r in res_q] + [r[0] for r in res_k], axis=1)
        return dqk, sum(r[1] for r in res_q), sum(r[1] for r in res_k)
    return _rows_call(fn, "fox_norm_bwd", [(pb, 2 * width, 0), dq, dk], [gq, gk], [(2 * width, BF16)],
                      [((1, HEAD_DIM), F32), ((1, HEAD_DIM), F32)])


def _lora_norm_fwd(down, gq, gkv, rank):
    def fn(dv, gqv, gkvv):
        return _rms_fwd(dv[:, :rank], gqv), _rms_fwd(dv[:, rank:], gkvv)
    return _rows_call(fn, "lora_norm_fwd", [(down, 2 * rank, 0)], [gq, gkv], [(rank, BF16), (rank, BF16)])


def _lora_norm_bwd(down, gq, gkv, dcq, dckv, dkpe, rank):
    def fn(dv, dcqv, dckvv, dkpev, gqv, gkvv):
        dxq, dgq = _rms_bwd(dv[:, :rank], gqv, dcqv)
        dxkv, dgkv = _rms_bwd(dv[:, rank:], gkvv, dckvv)
        return jnp.concatenate([dxq, dxkv, dkpev], axis=1), dgq, dgkv
    return _rows_call(fn, "lora_norm_bwd", [(down, 2 * rank, 0), dcq, dckv, dkpe], [gq, gkv],
                      [(2 * rank + LANES, BF16)], [((1, rank), F32), ((1, rank), F32)])


def _rope_tables(pos_col, inv_freq, sin_sign):
    def fn(pos, invf, sign):
        ang = pos.astype(F32) * invf
        return jnp.cos(ang) * jnp.abs(sign), jnp.sin(ang) * sign
    return _rows_call(fn, "rope_tables", [pos_col], [inv_freq, sin_sign], [(LANES, F32), (LANES, F32)])


def _mla_prep_fwd(q_raw, kv, down, kpe_block, qg, kg, cos_t, sin_s):
    def fn(qv, kvv, kpe, cosv, sinv, qgv, kgv):
        qs, ks, vs = [], [], []
        for h in range(MLA_HEADS):
            qn = _rms_fwd(qv[:, h * MLA_PAD_DIM:(h + 1) * MLA_PAD_DIM], qgv, MLA_QK_DIM)
            qs += [qn[:, :HEAD_DIM], _rope_fwd(qn[:, HEAD_DIM:], cosv, sinv)]
            k_full = jnp.concatenate([kvv[:, h * MLA_PAD_DIM:h * MLA_PAD_DIM + HEAD_DIM], kpe], axis=1)
            kn = _rms_fwd(k_full, kgv, MLA_QK_DIM)
            ks += [kn[:, :HEAD_DIM], _rope_fwd(kn[:, HEAD_DIM:], cosv, sinv)]
            vs.append(kvv[:, h * MLA_PAD_DIM + HEAD_DIM:(h + 1) * MLA_PAD_DIM])
        return jnp.concatenate(qs, axis=1), jnp.concatenate(ks, axis=1), jnp.concatenate(vs, axis=1)
    wide = MLA_HEADS * MLA_PAD_DIM
    return _rows_call(fn, "mla_prep_fwd", [q_raw, kv, (down, LANES, kpe_block), cos_t, sin_s], [qg, kg],
                      [(wide, BF16), (wide, BF16), (MLA_HEADS * HEAD_DIM, BF16)], tile=128)


def _mla_prep_bwd(q_raw, kv, down, kpe_block, qg, kg, cos_t, sin_s, dq, dk, dv):
    def fn(qv, kvv, kpe, cosv, sinv, dqv, dkv, dvv, qgv, kgv):
        dqs, dkvs = [], []
        dkpe = jnp.zeros_like(kpe)
        dqg = jnp.zeros_like(qgv)
        dkg = jnp.zeros_like(kgv)
        for h in range(MLA_HEADS):
            lo, hi = h * MLA_PAD_DIM, (h + 1) * MLA_PAD_DIM
            dqn = jnp.concatenate([dqv[:, lo:lo + HEAD_DIM], _rope_bwd(dqv[:, lo + HEAD_DIM:hi], cosv, sinv)], axis=1)
            dqh, dg = _rms_bwd(qv[:, lo:hi], qgv, dqn, MLA_QK_DIM)
            dqs.append(dqh)
            dqg = dqg + dg
            k_full = jnp.concatenate([kvv[:, lo:lo + HEAD_DIM], kpe], axis=1)
            dkn = jnp.concatenate([dkv[:, lo:lo + HEAD_DIM], _rope_bwd(dkv[:, lo + HEAD_DIM:hi], cosv, sinv)], axis=1)
            dkh, dg = _rms_bwd(k_full, kgv, dkn, MLA_QK_DIM)
            dkg = dkg + dg
            dkpe = dkpe + dkh[:, HEAD_DIM:]
            dkvs += [dkh[:, :HEAD_DIM], dvv[:, h * HEAD_DIM:(h + 1) * HEAD_DIM]]
        return jnp.concatenate(dqs, axis=1), jnp.concatenate(dkvs, axis=1), dkpe, dqg, dkg
    wide = MLA_HEADS * MLA_PAD_DIM
    return _rows_call(fn, "mla_prep_bwd", [q_raw, kv, (down, LANES, kpe_block), cos_t, sin_s, dq, dk, dv], [qg, kg],
                      [(wide, BF16), (wide, BF16), (LANES, F32)], [((1, MLA_PAD_DIM), F32), ((1, MLA_PAD_DIM), F32)],
                      tile=128)


def _sqrelu_up(acc):
    return acc, jnp.square(jnp.maximum(acc, 0.0))


def _sqrelu_grad(acc, u):
    return (acc * (2.0 * jnp.maximum(u, 0.0)),)


def _mlp_fwd(x, g, w_up, w_down, tag):
    h = _norm_fwd(x, g, f"mlp_norm_fwd{tag}")
    u, a = _matmul(h, w_up, "nn", f"mlp_up{tag}", (F32, BF16), _sqrelu_up)
    return _matmul(a, w_down, "nn", f"mlp_down{tag}", (F32,), _add_residual, (x,)), (h, u, a)


def _mlp_bwd(x, g, w_up, w_down, saved, dy, dy16, tag):
    h, u, a = saved
    dw_down = _matmul(a, dy16, "tn", f"mlp_dwdown{tag}", (BF16,))
    du = _matmul(dy16, w_down, "nt", f"mlp_du{tag}", (BF16,), _sqrelu_grad, (u,))
    dw_up = _matmul(h, du, "tn", f"mlp_dwup{tag}", (BF16,), column_slots=True)
    dh = _matmul(du, w_up, "nt", f"mlp_dh{tag}")
    dx, dx16, dg = _norm_bwd(x, g, dh, dy, f"mlp_norm_bwd{tag}")
    return dx, dx16, dg, dw_up, dw_down


def _local_step(x, pos_col, target, w, dist=None):
    w = dict(w)
    hs = w["w_a"].shape[1] // (4 * HEAD_DIM)
    sb_w = hs * HEAD_DIM
    grads, received = {}, {}

    def gather_in(group):
        return [("gather", blk) for blk in dist["blocks"][group]] if dist else []

    def exchange_in(group):
        return [("all_to_all", slots) for slots in dist["slots_of"](group, grads)] if dist else []

    h0 = _norm_fwd(x, w["ln_mix0"], "mix0_norm_fwd")
    pa = _matmul(h0, w["w_a"], "nn", "in_proj_a", (BF16,))
    pb = _matmul(h0, w["w_b"], "nn", "in_proj_b")
    pat = pa[:, sb_w:].T
    o_sb, o_sb_fine, *got = _attn_fwd("stick", pa, pat, pa, hs, HEAD_DIM, HEAD_DIM, HEAD_DIM ** -0.5, "stick_fwd",
                                      q_off=0, k_off=0, v_off=2 * hs, tq=ATTN_TQ_WIDE,
                                      hosted=gather_in("with_stick_fwd"))
    if dist:
        w.update(dist["weights_of"]("with_stick_fwd", got))
    logit_t = pb[:, 2 * sb_w:2 * sb_w + hs].T
    bias_col = w["b_f"][0, :hs].reshape(hs, 1)
    f_cum = _gate_fwd(logit_t, bias_col)
    f_col, f_row = f_cum[:, :, None], f_cum[:, None, :]
    qf, kf = _fox_norm_fwd(pb, w["fox_q_g"], w["fox_k_g"], hs)
    kft = kf.T
    o_fx, o_fx_fine, lse_fx, *got = _attn_fwd("softmax", qf, kft, pa, hs, HEAD_DIM, HEAD_DIM, HEAD_DIM ** -0.5,
                                              "fox_fwd", v_off=3 * hs, fcol=f_col, frow=f_row, tq=ATTN_TQ_WIDE, tk=ATTN_TK_SOFTMAX,
                                              fine=False,
                                              hosted=gather_in("with_fox_fwd"))
    if dist:
        w.update(dist["weights_of"]("with_fox_fwd", got))
    o0 = jnp.concatenate([o_sb, o_fx], axis=1)
    x1 = _matmul(o0, w["w_o0"], "nn", "out_proj0", (F32,), _add_residual, (x,))
    x2, mlp0 = _mlp_fwd(x1, w["ln_mlp0"], w["w_up0"], w["w_dn0"], "0")

    rank = w["w_uq"].shape[0]
    h2 = _norm_fwd(x2, w["ln_mix1"], "mix1_norm_fwd")
    down = _matmul(h2, w["w_down"], "nn", "mla_down")
    cqn, ckvn = _lora_norm_fwd(down, w["q_a_g"], w["kv_a_g"], rank)
    q_raw = _matmul(cqn, w["w_uq"], "nn", "mla_uq")
    kv = _matmul(ckvn, w["w_ukv"], "nn", "mla_ukv")
    cos_t, sin_s = _rope_tables(pos_col, w["inv_freq"], w["sin_sign"])
    kpe_block = 2 * rank // LANES
    qm, km, vm = _mla_prep_fwd(q_raw, kv, down, kpe_block, w["mla_q_g"], w["mla_k_g"], cos_t, sin_s)
    kmt, vmt = km.T, vm.T
    o_m, o_m_fine, lse_m, *got = _attn_fwd("softmax", qm, kmt, vm, MLA_HEADS, MLA_PAD_DIM, HEAD_DIM,
                                           MLA_QK_DIM ** -0.5, "mla_fwd", tq=ATTN_TQ_WIDE, tk=ATTN_TK_SOFTMAX,
                                           fine=False, hosted=gather_in("mlp1"))
    if dist:
        w.update(dist["weights_of"]("mlp1", got))
    x3 = _matmul(o_m, w["w_o1"], "nn", "out_proj1", (F32,), _add_residual, (x2,))
    x4, mlp1 = _mlp_fwd(x3, w["ln_mlp1"], w["w_up1"], w["w_dn1"], "1")

    dy, dy16, sq_err = _loss_fwd_bwd(x4, target)

    dx3, dx3_16, grads["ln_mlp1"], grads["w_up1"], grads["w_dn1"] = _mlp_bwd(
        x3, w["ln_mlp1"], w["w_up1"], w["w_dn1"], mlp1, dy, dy16, "1")
    grads["w_o1"] = _matmul(o_m, dx3_16, "tn", "dw_o1", (BF16,))
    do_m = _matmul(dx3_16, w["w_o1"], "nt", "do_mla", (BF16,))
    dqm, dkm, dvm, *got = _attn_bwd("softmax", qm, km, kmt, vmt, o_m_fine, do_m, MLA_HEADS, MLA_PAD_DIM, HEAD_DIM,
                                    MLA_QK_DIM ** -0.5, "mla_bwd", lse=lse_m, tq=ATTN_TQ_WIDE, tk=ATTN_TK_SOFTMAX,
                                    hosted=exchange_in("mlp1"))
    received["mlp1"] = got
    dq_raw, dkv, dkpe, grads["mla_q_g"], grads["mla_k_g"] = _mla_prep_bwd(
        q_raw, kv, down, kpe_block, w["mla_q_g"], w["mla_k_g"], cos_t, sin_s, dqm, dkm, dvm)
    grads["w_uq"] = _matmul(cqn, dq_raw, "tn", "dw_uq", (BF16,))
    grads["w_ukv"] = _matmul(ckvn, dkv, "tn", "dw_ukv", (BF16,))
    dcqn = _matmul(dq_raw, w["w_uq"], "nt", "d_cq")
    dckvn = _matmul(dkv, w["w_ukv"], "nt", "d_ckv")
    ddown, grads["q_a_g"], grads["kv_a_g"] = _lora_norm_bwd(down, w["q_a_g"], w["kv_a_g"], dcqn, dckvn, dkpe, rank)
    grads["w_down"] = _matmul(h2, ddown, "tn", "dw_down", (BF16,))
    dh2 = _matmul(ddown, w["w_down"], "nt", "d_h2")
    dx2, dx2_16, grads["ln_mix1"] = _norm_bwd(x2, w["ln_mix1"], dh2, dx3, "mix1_norm_bwd")

    dx1, dx1_16, grads["ln_mlp0"], grads["w_up0"], grads["w_dn0"] = _mlp_bwd(
        x1, w["ln_mlp0"], w["w_up0"], w["w_dn0"], mlp0, dx2, dx2_16, "0")
    grads["w_o0"] = _matmul(o0, dx1_16, "tn", "dw_o0", (BF16,))
    do0 = _matmul(dx1_16, w["w_o0"], "nt", "do_mix0", (BF16,))
    dq_sb, dk_sb, dv_sb, *got = _attn_bwd("stick", pa, pa, pat, pat, o_sb_fine, do0, hs, HEAD_DIM, HEAD_DIM,
                                          HEAD_DIM ** -0.5, "stick_bwd", q_off=0, k_off=hs, kt_off=0, vt_off=hs, do_off=0,
                                          tq=ATTN_TQ_WIDE,
                                          hosted=exchange_in("with_stick_bwd"))
    received["with_stick_bwd"] = got
    dqf, dkf, dv_fx, ds_rows, ds_cols, *got = _attn_bwd(
        "softmax", qf, kf, kft, pat, o_fx_fine, do0, hs, HEAD_DIM, HEAD_DIM, HEAD_DIM ** -0.5, "fox_bwd", vt_off=2 * hs,
        tq=ATTN_TQ_WIDE, tk=ATTN_TK_SOFTMAX,
        do_off=hs, lse=lse_fx, fcol=f_col, frow=f_row, hosted=exchange_in("with_fox_bwd"))
    received["with_fox_bwd"] = got
    dqk_fx, grads["fox_q_g"], grads["fox_k_g"] = _fox_norm_bwd(pb, w["fox_q_g"], w["fox_k_g"], dqf, dkf, hs)
    dlogit_t, db_f = _gate_bwd(ds_rows[:, :, 0] - ds_cols[:, 0, :], logit_t, bias_col)
    grads["b_f"] = db_f.reshape(1, hs)
    dpa = jnp.concatenate([dq_sb.astype(BF16), dk_sb.astype(BF16), dv_sb.astype(BF16), dv_fx.astype(BF16)], axis=1)
    dlogit_pad = jnp.pad(dlogit_t.T.astype(BF16), ((0, 0), (0, pb.shape[1] - 2 * sb_w - hs)))
    dpb = jnp.concatenate([dqk_fx, dlogit_pad], axis=1)
    grads["w_a"] = _matmul(h0, dpa, "tn", "dw_a", (BF16,))
    grads["w_b"] = _matmul(h0, dpb, "tn", "dw_b", (BF16,))
    parts = exchange_in("mix0")
    res_b = _matmul(dpb, w["w_b"], "nt", "d_h0_b", hosted=parts[:1])
    dh0 = res_b[0] if dist else res_b
    res_a = _matmul(dpa, w["w_a"], "nt", "d_h0_a", (F32,), _add_residual, (dh0,), hosted=parts[1:])
    dh0, received["mix0"] = (res_a[0], [res_b[1], res_a[1]]) if dist else (res_a, [])
    grad_x, _, grads["ln_mix0"] = _norm_bwd(x, w["ln_mix0"], dh0, dx1, "mix0_norm_bwd")
    return sq_err, grad_x, grads, received


PIECES = {
    "sf_w_in": ("sf_w_in", 0, 1), "sf_w_o": ("sf_w_o", 0, 0), "mla_w_down": ("mla_w_down", 0, 0),
    "mla_w_uq": ("mla_w_uq", 0, 1), "mla_w_ukv": ("mla_w_ukv", 0, 1), "mla_w_o": ("mla_w_o", 0, 0),
    "mlp_w_up0": ("mlp_w_up", 0, 1), "mlp_w_up1": ("mlp_w_up", 1, 1),
    "mlp_w_down0": ("mlp_w_down", 0, 0), "mlp_w_down1": ("mlp_w_down", 1, 0),
}
GROUPS = {
    "mix0": ["sf_w_in"], "mla": ["mla_w_down", "mla_w_uq", "mla_w_ukv", "mla_w_o"], "mlp1": ["mlp_w_up1", "mlp_w_down1"],
}
GROUPS["with_stick_fwd"] = ["sf_w_o", "mlp_w_up0"]
GROUPS["with_fox_fwd"] = GROUPS["mla"] + ["mlp_w_down0"]
GROUPS["with_stick_bwd"] = GROUPS["mla"] + ["mlp_w_up0"]
GROUPS["with_fox_bwd"] = ["mlp_w_down0", "sf_w_o"]
SMALL = ["ln_mix_g", "ln_mlp_g", "sf_b_f", "fox_q_g", "fox_k_g", "mla_q_a_g", "mla_kv_a_g", "mla_q_g", "mla_k_g"]
ALL_W = ["ln_mix_g", "ln_mlp_g", "sf_w_in", "sf_b_f", "fox_q_g", "fox_k_g", "sf_w_o", "mla_w_down", "mla_q_a_g",
         "mla_kv_a_g", "mla_w_uq", "mla_w_ukv", "mla_q_g", "mla_k_g", "mla_w_o", "mlp_w_up", "mlp_w_down"]


def _weights_mix0(full, small):
    w_in = full["sf_w_in"]
    d_model = w_in.shape[0]
    n_fx = small["sf_b_f"].shape[1]
    sb_w = (w_in.shape[1] - n_fx) // 6
    cols = lambda i: w_in[:, i * sb_w:(i + 1) * sb_w]
    w_a = jnp.concatenate([cols(0), cols(1), cols(2), cols(5)], axis=1)
    w_b = jnp.concatenate([cols(3), cols(4), w_in[:, 6 * sb_w:], jnp.zeros((d_model, LANES - n_fx), w_in.dtype)], axis=1)
    half = ROPE_DIM // 2
    inv_freq = ROPE_THETA ** (-jnp.arange(half, dtype=F32) / half)
    zeros64 = jnp.zeros((ROPE_DIM,), F32)
    pad256 = lambda g: jnp.pad(g, ((0, 0), (0, MLA_PAD_DIM - MLA_QK_DIM)))
    pad_lanes = lambda g: jnp.pad(g, ((0, 0), (0, LANES - g.shape[1])))
    return dict(
        ln_mix0=small["ln_mix_g"][0:1], ln_mix1=small["ln_mix_g"][1:2],
        ln_mlp0=small["ln_mlp_g"][0:1], ln_mlp1=small["ln_mlp_g"][1:2],
        w_a=w_a, w_b=w_b, b_f=pad_lanes(small["sf_b_f"]), fox_q_g=small["fox_q_g"], fox_k_g=small["fox_k_g"],
        q_a_g=small["mla_q_a_g"], kv_a_g=small["mla_kv_a_g"],
        mla_q_g=pad256(small["mla_q_g"]), mla_k_g=pad256(small["mla_k_g"]),
        inv_freq=jnp.concatenate([inv_freq, inv_freq, zeros64]).reshape(1, LANES),
        sin_sign=jnp.concatenate([-jnp.ones((half,), F32), jnp.ones((half,), F32), zeros64]).reshape(1, LANES),
    )


def _late_weights(full):
    plain = {"sf_w_o": "w_o0", "mlp_w_up0": "w_up0", "mlp_w_down0": "w_dn0", "mla_w_ukv": "w_ukv", "mla_w_o": "w_o1",
             "mlp_w_up1": "w_up1", "mlp_w_down1": "w_dn1"}
    out = {key: full[p] for p, key in plain.items() if p in full}
    if "mla_w_down" in full:
        out["w_down"] = jnp.pad(full["mla_w_down"], ((0, 0), (0, LANES - ROPE_DIM)))
    if "mla_w_uq" in full:
        rank = full["mla_w_uq"].shape[0]
        w_uq = full["mla_w_uq"].reshape(rank, MLA_HEADS, MLA_QK_DIM)
        out["w_uq"] = jnp.pad(w_uq, ((0, 0), (0, 0), (0, MLA_PAD_DIM - MLA_QK_DIM))).reshape(rank, MLA_HEADS * MLA_PAD_DIM)
    return out


def _piece_slots(g, piece):
    if piece in ("mlp_w_up0", "mlp_w_up1"):
        return g[{"mlp_w_up0": "w_up0", "mlp_w_up1": "w_up1"}[piece]]
    return _shard_stack(_piece_grad(g, piece), PIECES[piece][2])


def _piece_grad(g, piece):
    if piece in ("mlp_w_up0", "mlp_w_up1"):
        return _unshard(_piece_slots(g, piece), PIECES[piece][2])
    if piece == "sf_w_in":
        n_fx = g["b_f"].shape[1]
        ga, gb = g["w_a"], g["w_b"]
        sb_w = ga.shape[1] // 4
        ca = lambda i: ga[:, i * sb_w:(i + 1) * sb_w]
        return jnp.concatenate([ca(0), ca(1), ca(2), gb[:, :sb_w], gb[:, sb_w:2 * sb_w], ca(3),
                                gb[:, 2 * sb_w:2 * sb_w + n_fx]], axis=1)
    if piece == "mla_w_uq":
        rank = g["w_uq"].shape[0]
        return g["w_uq"].reshape(rank, MLA_HEADS, MLA_PAD_DIM)[:, :, :MLA_QK_DIM].reshape(rank, MLA_HEADS * MLA_QK_DIM)
    if piece == "mla_w_down":
        return g["w_down"][:, :g["w_down"].shape[1] - (LANES - ROPE_DIM)]
    return g[{"sf_w_o": "w_o0", "mla_w_ukv": "w_ukv", "mla_w_o": "w_o1", "mlp_w_up0": "w_up0", "mlp_w_up1": "w_up1",
              "mlp_w_down0": "w_dn0", "mlp_w_down1": "w_dn1"}[piece]]


def _small_grads(g):
    return {
        "ln_mix_g": jnp.concatenate([g["ln_mix0"], g["ln_mix1"]], axis=0),
        "ln_mlp_g": jnp.concatenate([g["ln_mlp0"], g["ln_mlp1"]], axis=0),
        "sf_b_f": g["b_f"], "fox_q_g": g["fox_q_g"], "fox_k_g": g["fox_k_g"],
        "mla_q_a_g": g["q_a_g"], "mla_kv_a_g": g["kv_a_g"],
        "mla_q_g": g["mla_q_g"][:, :MLA_QK_DIM], "mla_k_g": g["mla_k_g"][:, :MLA_QK_DIM],
    }


PACK_TILE = 1024


def _as_rows(a, row_multiple=16):
    flat = a.reshape(-1)
    rows = -(-flat.shape[0] // LANES)
    rows = -(-rows // row_multiple) * row_multiple
    return jnp.pad(flat, (0, rows * LANES - flat.shape[0])).reshape(rows, LANES)


def _pack_rows(parts, axis, dtype, row_multiple=PACK_TILE, spare_rows=0):
    used = sum(p.shape[axis] for p in parts)
    shape = list(parts[0].shape)
    shape[axis] = -(-used // row_multiple) * row_multiple + spare_rows - used
    return jnp.concatenate([p.astype(dtype) for p in parts] + [jnp.ones(shape, dtype)], axis=axis)


def _unshard(stack, axis):
    moved = jnp.moveaxis(stack, 0, axis)
    shape = list(stack.shape[1:])
    shape[axis] *= stack.shape[0]
    return moved.reshape(shape)


def _shard_stack(full, axis):
    shape = list(full.shape)
    shape[axis:axis + 1] = [N_DEV, shape[axis] // N_DEV]
    return jnp.moveaxis(full.reshape(shape), axis, 0)


OPT_TILE_ELEMS = 128 * 1024


def _row_tile(rows, cols):
    best = 16
    for t in range(16, rows + 1, 16):
        if rows % t == 0 and t * cols <= OPT_TILE_ELEMS:
            best = t
    assert rows % best == 0
    return best


def _cast_bf16(a, name):
    return _rows_call(lambda v: v, name, [a], [], [(a.shape[1], BF16)], tile=_row_tile(*a.shape))


def _adam_math(w, g, m, v):
    m = ADAM_B1 * m + (1.0 - ADAM_B1) * g
    v = ADAM_B2 * v + (1.0 - ADAM_B2) * jnp.square(g)
    m_hat = m / (1.0 - ADAM_B1 ** ADAM_STEP)
    v_hat = v / (1.0 - ADAM_B2 ** ADAM_STEP)
    delta = -ADAM_LR * (m_hat / (jnp.sqrt(v_hat) + ADAM_EPS) + ADAM_WD * w)
    return delta, m, v


def _adam_big(recvs, w, m, v, name, hosted=()):
    layers, rows, cols = w.shape
    tile = _row_tile(rows, cols)
    n_tiles = rows // tile
    n_host = len(hosted)

    def body(*refs):
        recv_refs = refs[:layers]
        w_ref, m_ref, v_ref = refs[layers:layers + 3]
        base = layers + 3 + n_host
        g_ref, d_ref, nm_ref, nv_ref = refs[base:base + 4]
        layer = pl.program_id(0)
        host_args = (hosted, refs[layers + 3:base], refs[base + 4:base + 4 + n_host], refs[base + 4 + n_host:],
                     layer * n_tiles + pl.program_id(1), layers * n_tiles)
        _run_hosted(*host_args, "start")

        def total(r_ref):
            acc = r_ref[0].astype(F32)
            for s in range(1, N_DEV):
                acc = acc + r_ref[s].astype(F32)
            return acc

        g = total(recv_refs[0])
        for j in range(1, layers):
            g = jnp.where(layer == j, total(recv_refs[j]), g)
        delta, nm, nv = _adam_math(w_ref[...], g, m_ref[...], v_ref[...])
        g_ref[...] = g
        d_ref[...] = delta
        nm_ref[...] = nm
        nv_ref[...] = nv
        _run_hosted(*host_args, "finish")

    def recv_spec(j):
        return pl.BlockSpec((N_DEV, tile, cols),
                            lambda l, i: (0, jnp.where(l == j, i, jnp.where(l < j, 0, n_tiles - 1)), 0))

    spec = pl.BlockSpec((None, tile, cols), lambda l, i: (l, i, 0))
    out = jax.ShapeDtypeStruct(w.shape, F32)
    return pl.pallas_call(
        body, name=name, grid=(layers, n_tiles),
        in_specs=[recv_spec(j) for j in range(layers)] + [spec] * 3 + [ANY_SPEC] * n_host,
        out_specs=[spec] * 4 + [ANY_SPEC] * n_host,
        out_shape=[out] * 4 + [_exchange_out_shape(kd, arr) for kd, arr in hosted],
        scratch_shapes=EXCHANGE_SCRATCH * n_host, compiler_params=_params("arbitrary", "arbitrary"),
    )(*recvs, w, m, v, *[arr for _, arr in hosted])


def _sum_slots(gathered):
    rows = gathered.shape[1]

    def body(r_ref, o_ref):
        acc = r_ref[0]
        for s in range(1, N_DEV):
            acc = acc + r_ref[s]
        o_ref[...] = acc

    return pl.pallas_call(body, name="sum_small", out_shape=jax.ShapeDtypeStruct((rows, LANES), F32))(gathered)


def _adam_small(w, g, m, v):
    def fn(wv, gv, mv, vv):
        return _adam_math(wv, gv, mv, vv)
    return _rows_call(fn, "adam_small", [w, g, m, v], [], [(LANES, F32)] * 3, tile=w.shape[0])


def kernel(x, positions, ln_mix_g, ln_mlp_g, sf_w_in, sf_b_f, fox_q_g, fox_k_g, sf_w_o, mla_w_down, mla_q_a_g, mla_kv_a_g, mla_w_uq, mla_w_ukv, mla_q_g, mla_k_g, mla_w_o, mlp_w_up, mlp_w_down, loss_target, m_ln_mix_g, m_ln_mlp_g, m_sf_w_in, m_sf_b_f, m_fox_q_g, m_fox_k_g, m_sf_w_o, m_mla_w_down, m_mla_q_a_g, m_mla_kv_a_g, m_mla_w_uq, m_mla_w_ukv, m_mla_q_g, m_mla_k_g, m_mla_w_o, m_mlp_w_up, m_mlp_w_down, v_ln_mix_g, v_ln_mlp_g, v_sf_w_in, v_sf_b_f, v_fox_q_g, v_fox_k_g, v_sf_w_o, v_mla_w_down, v_mla_q_a_g, v_mla_kv_a_g, v_mla_w_uq, v_mla_w_ukv, v_mla_q_g, v_mla_k_g, v_mla_w_o, v_mlp_w_up, v_mlp_w_down):
    given = dict(locals())
    wts = {n: given[n] for n in ALL_W}
    mom = {n: given["m_" + n] for n in ALL_W}
    var = {n: given["v_" + n] for n in ALL_W}
    me = 4 * lax.axis_index("x") + 2 * lax.axis_index("y") + lax.axis_index("c")
    t_rows, d_model = x.shape[1], x.shape[2]
    big = sorted({name for name, _, _ in PIECES.values()})

    def whole_pieces(gathered, group):
        return {p: _unshard(s, PIECES[p][2]) for p, s in zip(GROUPS[group], gathered)}

    def w_in_slot_parts(grads):
        slots = _shard_stack(_piece_grad(grads, "sf_w_in"), PIECES["sf_w_in"][2])
        cuts = [0] + [(slots.shape[1] * f // 64) // 16 * 16 for f in (11, 32, 49)] + [slots.shape[1]]
        return [slots[:, a:b] for a, b in zip(cuts, cuts[1:])]

    def slots_of(group, grads):
        if group == "mix0":
            return w_in_slot_parts(grads)[:2]
        slots = [_piece_slots(grads, p) for p in GROUPS[group]]
        assert all(s.shape[0] == N_DEV for s in slots)
        return slots

    cast = {n: _cast_bf16(wts[n].reshape(-1, wts[n].shape[2]), f"cast_{n}").reshape(wts[n].shape) for n in big}
    blocks = {grp: [cast[PIECES[p][0]][PIECES[p][1]] for p in GROUPS[grp]] for grp in ("mix0", "with_stick_fwd", "with_fox_fwd", "mlp1")}
    mix0 = whole_pieces(_exchange("gather", blocks["mix0"], "gather_mix0"), "mix0")
    gains, = _exchange("gather", [_as_rows(jnp.concatenate([mla_q_a_g, mla_kv_a_g], axis=1))], "gather_gains")
    lora_n = mla_q_a_g.shape[1]
    gains_flat = gains.reshape(N_DEV, -1)[:, :2 * lora_n]
    small = dict(ln_mix_g=ln_mix_g, ln_mlp_g=ln_mlp_g, sf_b_f=sf_b_f, fox_q_g=fox_q_g, fox_k_g=fox_k_g,
                 mla_q_a_g=gains_flat[:, :lora_n].reshape(1, -1), mla_kv_a_g=gains_flat[:, lora_n:].reshape(1, -1),
                 mla_q_g=mla_q_g, mla_k_g=mla_k_g)
    dist = dict(blocks=blocks, slots_of=slots_of,
                weights_of=lambda grp, gathered: _late_weights(whole_pieces(gathered, grp)))
    sq_err, grad_x, g, received = _local_step(x[0], positions.reshape(t_rows, 1), loss_target[0],
                                              _weights_mix0(mix0, small), dist)

    recv_of = {p: r for grp in ("mlp1", "with_stick_bwd", "with_fox_bwd") for p, r in zip(GROUPS[grp], received[grp])}
    late_parts = dict(zip(("mlp_w_down", "mlp_w_up"), w_in_slot_parts(g)[2:]))
    w_in_recv = list(received["mix0"])
    results = {kind: {} for kind in ("grad", "delta", "new_m", "new_v")}
    for n in sorted(big, key=lambda name: (name == "sf_w_in", name not in late_parts)):
        if n == "sf_w_in":
            recv_of["sf_w_in"] = jnp.concatenate(w_in_recv, axis=1)
        layers = [p for _, p in sorted((layer, p) for p, (name, layer, _) in PIECES.items() if name == n)]
        hosted = [("all_to_all", late_parts[n])] if n in late_parts else []
        outs = _adam_big([recv_of[p] for p in layers], wts[n], mom[n], var[n], f"adam_{n}", hosted)
        w_in_recv += outs[4:]
        for kind, out in zip(("grad", "delta", "new_m", "new_v"), outs[:4]):
            results[kind][n] = out

    small_g = _small_grads(g)
    small_parts = [_as_rows(small_g[n], 8) for n in SMALL] + [_as_rows(sq_err, 8)]
    small_sum = _sum_slots(_exchange("gather", [_pack_rows(small_parts, 0, F32, 8, 8)], "gather_small_grads")[0])
    red, off = {}, 0
    for n, p in zip(SMALL + ["loss"], small_parts):
        red[n] = small_sum[off:off + p.shape[0]].reshape(-1)
        off += p.shape[0]
    loss = 0.5 * red["loss"][0] / d_model
    for n in SMALL:
        if n in ("mla_q_a_g", "mla_kv_a_g"):
            results["grad"][n] = lax.dynamic_slice(red[n], (me * lora_n,), (lora_n,)).reshape(wts[n].shape)
        else:
            results["grad"][n] = red[n][:wts[n].size].reshape(wts[n].shape)
    pack_small = lambda d: jnp.concatenate([_as_rows(d[n], 8) for n in SMALL], axis=0)
    small_out = _adam_small(pack_small(wts), pack_small(results["grad"]), pack_small(mom), pack_small(var))
    off = 0
    for n in SMALL:
        r = _as_rows(wts[n], 8).shape[0]
        for kind, packed in zip(["delta", "new_m", "new_v"], small_out):
            results[kind][n] = packed[off:off + r].reshape(-1)[:wts[n].size].reshape(wts[n].shape)
        off += r

    outs = [loss, grad_x[None]]
    for kind in ["grad", "delta", "new_m", "new_v"]:
        outs += [results[kind][n] for n in ALL_W]
    return tuple(outs)
```

```python
import functools

import jax
import jax.numpy as jnp
from jax import lax
from jax.experimental import pallas as pl
from jax.experimental.pallas import tpu as pltpu

F32 = jnp.float32
BF16 = jnp.bfloat16

NORM_EPS = 1e-6
ROPE_THETA = 10000.0
HEAD_DIM = 128
ROPE_DIM = 64
MLA_HEADS = 16
MLA_QK_DIM = 192
MLA_PAD_DIM = 256
ADAM_LR, ADAM_B1, ADAM_B2, ADAM_EPS, ADAM_WD, ADAM_STEP = 0.001, 0.9, 0.999, 1e-08, 0.01, 10

N_DEV = 8
LANES = 128
VMEM_LIMIT = 56 * 1024 * 1024
MATMUL_VMEM_BUDGET = 40 * 1024 * 1024
MASKED = -1e30
ROW_INPUT_BUFFERS = 3
LOG2E = 1.4426950408889634
ATTN_TQ, ATTN_TK = 256, 256
ATTN_TQ_WIDE = 512
ATTN_TK_SOFTMAX = 512
MESH = pl.DeviceIdType.MESH

NT_DIMS = (((1,), (1,)), ((), ()))
TN_DIMS = (((0,), (0,)), ((), ()))
NN_DIMS = (((1,), (0,)), ((), ()))


def _params(*sem):
    return pltpu.CompilerParams(dimension_semantics=sem, vmem_limit_bytes=VMEM_LIMIT)


def _pick(n, pref):
    best = None
    for t in range(LANES, min(n, pref) + 1, LANES):
        if n % t == 0:
            best = t
    return n if best is None or 2 * best < min(n, pref) else best


def _rows_call(fn, name, row_ins, full_ins, row_outs, acc_outs=(), tile=256):
    row_ins = [r if isinstance(r, tuple) else (r, r.shape[1], 0) for r in row_ins]
    t_rows = row_ins[0][0].shape[0]
    assert t_rows % tile == 0
    n_row, n_in = len(row_ins), len(row_ins) + len(full_ins)
    n_row_out = len(row_outs)
    n_steps = t_rows // tile
    slots = ROW_INPUT_BUFFERS

    def body(*refs):
        bufs, sems = refs[len(refs) - 2 * n_row:len(refs) - n_row], refs[len(refs) - n_row:]
        refs = refs[:len(refs) - 2 * n_row]
        step = pl.program_id(0)

        def fetch(k, at, slot):
            _, w, cb = row_ins[k]
            return pltpu.make_async_copy(refs[k].at[pl.ds(at * tile, tile), pl.ds(cb * w, w)], bufs[k].at[slot],
                                         sems[k].at[slot])

        @pl.when(step == 0)
        def _():
            for ahead in range(min(slots - 1, n_steps)):
                for k in range(n_row):
                    fetch(k, ahead, ahead).start()

        @pl.when(step + (slots - 1) < n_steps)
        def _():
            for k in range(n_row):
                fetch(k, step + (slots - 1), (step + (slots - 1)) % slots).start()

        for k in range(n_row):
            fetch(k, step, step % slots).wait()
        res = fn(*[bufs[k][step % slots] for k in range(n_row)], *[r[...] for r in refs[n_row:n_in]])
        res = res if isinstance(res, tuple) else (res,)
        for ref, val in zip(refs[n_in:n_in + n_row_out], res[:n_row_out]):
            ref[...] = val.astype(ref.dtype)
        acc_refs = refs[n_in + n_row_out:]
        if acc_refs:
            @pl.when(pl.program_id(0) == 0)
            def _():
                for ref in acc_refs:
                    ref[...] = jnp.zeros_like(ref)
            for ref, val in zip(acc_refs, res[n_row_out:]):
                ref[...] += val.astype(ref.dtype)

    in_specs = [pl.BlockSpec(memory_space=pl.ANY)] * n_row
    in_specs += [pl.BlockSpec(a.shape, lambda i: (0, 0)) for a in full_ins]
    out_specs = [pl.BlockSpec((tile, c), lambda i: (i, 0)) for c, _ in row_outs]
    out_specs += [pl.BlockSpec(s, lambda i: (0, 0)) for s, _ in acc_outs]
    out_shape = [jax.ShapeDtypeStruct((t_rows, c), d) for c, d in row_outs]
    out_shape += [jax.ShapeDtypeStruct(s, d) for s, d in acc_outs]
    outs = pl.pallas_call(
        body, name=name, grid=(n_steps,), in_specs=in_specs, out_specs=out_specs, out_shape=out_shape,
        scratch_shapes=[pltpu.VMEM((slots, tile, w), a.dtype) for a, w, _ in row_ins]
        + [pltpu.SemaphoreType.DMA((slots,))] * n_row,
        compiler_params=_params("arbitrary"),
    )(*[r[0] for r in row_ins], *full_ins)
    return outs[0] if len(outs) == 1 else tuple(outs)


def _matmul_tiles(m, n, k, in_bytes, out_bytes):
    tn = n if n <= 1280 else _pick(n, 1024)
    tks = [k] + [k // d for d in (2, 4, 8, 16) if k % (d * LANES) == 0]
    for tk in [t for t in tks if t <= 4096] or [tks[-1]]:
        for tm in (1024, 512, 256):
            if m % tm:
                continue
            acc = 2 * tm * tn * 4 if tk < k else tm * tn * 4
            if 2 * (tm * tk + tk * tn) * in_bytes + 2 * tm * tn * out_bytes + acc <= MATMUL_VMEM_BUDGET:
                return tm, tn, tk
    raise ValueError(f"no matmul tiling for {m}x{n}x{k}")


def _matmul(a, b, form, name, out_dtypes=(F32,), epilogue=None, extras=(), hosted=(), column_slots=False):
    if form == "nn":
        (m, k), n = a.shape, b.shape[1]
    elif form == "nt":
        (m, k), n = a.shape, b.shape[0]
    else:
        (k, m), n = a.shape, b.shape[1]
    in_bytes = max(a.dtype.itemsize, b.dtype.itemsize)
    out_bytes = sum(jnp.dtype(d).itemsize for d in out_dtypes) + sum(e.dtype.itemsize for e in extras)
    tm, tn, tk = _matmul_tiles(m, n, k, in_bytes, out_bytes)
    nk = k // tk
    dims = {"nn": NN_DIMS, "nt": NT_DIMS, "tn": TN_DIMS}[form]
    n_extra, n_out, n_host = len(extras), len(out_dtypes), len(hosted)
    grid = (m // tm, n // tn, nk)

    def body(*refs):
        a_ref, b_ref = refs[0], refs[1]
        extra_refs = refs[2:2 + n_extra]
        base = 2 + n_extra + n_host
        out_refs = refs[base:base + n_out]
        sems_at = base + n_out + n_host
        step = (pl.program_id(0) * grid[1] + pl.program_id(1)) * nk + pl.program_id(2)
        host_args = (hosted, refs[2 + n_extra:base], refs[base + n_out:sems_at], refs[sems_at:sems_at + 3 * n_host],
                     step, grid[0] * grid[1] * nk)
        _run_hosted(*host_args, "start")

        def finish(acc):
            vals = (acc,) if epilogue is None else epilogue(acc, *[r[...] for r in extra_refs])
            for ref, val in zip(out_refs, vals):
                ref[...] = val.astype(ref.dtype)

        part = lax.dot_general(a_ref[...].astype(BF16), b_ref[...].astype(BF16), dims, preferred_element_type=F32)
        if nk == 1:
            finish(part)
        else:
            acc_ref = refs[-1]
            kk = pl.program_id(2)

            @pl.when(kk == 0)
            def _():
                acc_ref[...] = part

            @pl.when(kk > 0)
            def _():
                acc_ref[...] += part

            @pl.when(kk == nk - 1)
            def _():
                finish(acc_ref[...])
        _run_hosted(*host_args, "finish")

    a_spec = pl.BlockSpec((tk, tm), lambda i, j, kk: (kk, i)) if form == "tn" else pl.BlockSpec((tm, tk), lambda i, j, kk: (i, kk))
    b_spec = pl.BlockSpec((tn, tk), lambda i, j, kk: (j, kk)) if form == "nt" else pl.BlockSpec((tk, tn), lambda i, j, kk: (kk, j))
    o_spec = pl.BlockSpec((tm, tn), lambda i, j, kk: (i, j))
    out_shapes = [jax.ShapeDtypeStruct((m, n), d) for d in out_dtypes]
    out_specs = [o_spec] * n_out
    if column_slots:
        assert not extras
        out_shapes = [jax.ShapeDtypeStruct((n // tn, m, tn), d) for d in out_dtypes]
        out_specs = [pl.BlockSpec((None, tm, tn), lambda i, j, kk: (j, i, 0))] * n_out
    outs = pl.pallas_call(
        body, name=name, grid=grid, in_specs=[a_spec, b_spec] + [o_spec] * n_extra + [ANY_SPEC] * n_host,
        out_specs=out_specs + [ANY_SPEC] * n_host,
        out_shape=out_shapes + [_exchange_out_shape(kd, arr) for kd, arr in hosted],
        scratch_shapes=EXCHANGE_SCRATCH * n_host + ([pltpu.VMEM((tm, tn), F32)] if nk > 1 else []),
        compiler_params=_params(*(("arbitrary",) * 3 if n_host else ("parallel", "parallel", "arbitrary"))),
    )(a, b, *extras, *[arr for _, arr in hosted])
    return outs[0] if n_out + n_host == 1 else tuple(outs)


def _add_residual(acc, res):
    return (acc + res,)


def _log_sigmoid_parts(z):
    return jnp.log1p(jnp.exp(-jnp.abs(z)))


def _log2_gates(z2):
    lg = jnp.log2(1.0 + jnp.exp2(-jnp.abs(z2)))
    log_beta = jnp.minimum(z2, 0.0) - lg
    return log_beta, log_beta - z2


def _rms_fwd(x, g, n=None):
    n = x.shape[-1] if n is None else n
    r = lax.rsqrt(jnp.sum(x * x, axis=-1, keepdims=True) / n + NORM_EPS)
    return x * r * g


def _rms_bwd(x, g, dout, n=None):
    n = x.shape[-1] if n is None else n
    r = lax.rsqrt(jnp.sum(x * x, axis=-1, keepdims=True) / n + NORM_EPS)
    y = x * r
    dg = jnp.sum(dout * y, axis=0, keepdims=True)
    dy = dout * g
    dx = r * (dy - y * (jnp.sum(dy * y, axis=-1, keepdims=True) / n))
    return dx, dg


def _swap_halves(r):
    lane = lax.broadcasted_iota(jnp.int32, r.shape, 1)
    return jnp.where(lane < ROPE_DIM // 2, pltpu.roll(r, LANES - ROPE_DIM // 2, 1), pltpu.roll(r, ROPE_DIM // 2, 1))


def _rope_fwd(r, cos_t, sin_s):
    return r * cos_t + _swap_halves(r) * sin_s


def _rope_bwd(dr, cos_t, sin_s):
    return dr * cos_t + _swap_halves(dr * sin_s)


def _split3(x):
    hi = x.astype(BF16)
    r1 = x - hi.astype(F32)
    mid = r1.astype(BF16)
    lo = (r1 - mid.astype(F32)).astype(BF16)
    return hi, mid, lo


def _mesh_position():
    x, y, c = lax.axis_index("x"), lax.axis_index("y"), lax.axis_index("c")
    return x, y, c, 4 * x + 2 * y + c


def _peer(x, y, c, k):
    bx, by, bc = (k >> 2) & 1, (k >> 1) & 1, k & 1
    px, py, pc = x ^ bx, y ^ by, c ^ bc
    return (px, py, pc), 4 * px + 2 * py + pc


def _gather_steps(x_ref, out_ref, send_sems, recv_sems, local_sem):
    x, y, c, me = _mesh_position()
    sibling = (x, y, 1 - c)
    chips = [(1 - x, y), (x, 1 - y), (1 - x, 1 - y)]

    def slot(px, py, pc):
        return out_ref.at[4 * px + 2 * py + pc]

    def copy(k, blk, to, src=None):
        return pltpu.make_async_remote_copy(
            src_ref=slot(*blk) if src is None else src, dst_ref=slot(*blk), send_sem=send_sems.at[k],
            recv_sem=recv_sems.at[k], device_id=to, device_id_type=MESH)

    mine = pltpu.make_async_copy(x_ref, out_ref.at[me], local_sem)
    first = [copy(0, (x, y, c), sibling, src=x_ref)]
    first += [copy(1 + j, (x, y, c), (*chip, c), src=x_ref) for j, chip in enumerate(chips)]
    passed = [copy(4 + j, (*chip, c), sibling) for j, chip in enumerate(chips)]

    def start():
        mine.start()
        for cp in first:
            cp.start()

    def forward():
        for j, chip in enumerate(chips):
            copy(1 + j, (*chip, c), (x, y, c)).wait_recv()
            passed[j].start()

    def finish():
        copy(0, (x, y, 1 - c), (x, y, c)).wait_recv()
        for j, chip in enumerate(chips):
            copy(4 + j, (*chip, 1 - c), (x, y, c)).wait_recv()
        for cp in first + passed:
            cp.wait_send()
        mine.wait()

    return start, forward, finish


def _all_to_all_steps(g_ref, out_ref, send_sems, recv_sems, local_sem):
    x, y, c, me = _mesh_position()
    mine = pltpu.make_async_copy(g_ref.at[me], out_ref.at[me], local_sem)
    copies = []
    for k in range(1, N_DEV):
        peer, peer_idx = _peer(x, y, c, k)
        copies.append(pltpu.make_async_remote_copy(
            src_ref=g_ref.at[peer_idx], dst_ref=out_ref.at[me], send_sem=send_sems.at[k - 1],
            recv_sem=recv_sems.at[k - 1], device_id=peer, device_id_type=MESH))

    def start():
        mine.start()
        for cp in copies:
            cp.start()

    def finish():
        for k in range(1, N_DEV):
            peer, peer_idx = _peer(x, y, c, k)
            pltpu.make_async_remote_copy(
                src_ref=g_ref.at[me], dst_ref=out_ref.at[peer_idx], send_sem=send_sems.at[k - 1],
                recv_sem=recv_sems.at[k - 1], device_id=peer, device_id_type=MESH).wait_recv()
        for cp in copies:
            cp.wait_send()
        mine.wait()

    return start, None, finish


EXCHANGE_STEPS = {"gather": _gather_steps, "all_to_all": _all_to_all_steps}
EXCHANGE_SCRATCH = [pltpu.SemaphoreType.DMA((7,)), pltpu.SemaphoreType.DMA((7,)), pltpu.SemaphoreType.DMA]
ANY_SPEC = pl.BlockSpec(memory_space=pl.ANY)


def _exchange_out_shape(kind, arr):
    return jax.ShapeDtypeStruct(((N_DEV,) + arr.shape) if kind == "gather" else arr.shape, arr.dtype)


def _exchange(kind, arrs, name):
    n = len(arrs)

    def body(*refs):
        steps = [EXCHANGE_STEPS[kind](refs[i], refs[n + i], *refs[2 * n + 3 * i:2 * n + 3 * i + 3]) for i in range(n)]
        for start, _, _ in steps:
            start()
        for _, forward, _ in steps:
            if forward is not None:
                forward()
        for _, _, finish in steps:
            finish()

    return pl.pallas_call(body, name=name, out_shape=[_exchange_out_shape(kind, a) for a in arrs],
                          in_specs=[ANY_SPEC] * n, out_specs=[ANY_SPEC] * n, scratch_shapes=EXCHANGE_SCRATCH * n)(*arrs)


def _run_hosted(hosted, src_refs, dst_refs, sem_refs, step, n_steps, when):
    for idx, (kind, _) in enumerate(hosted):
        start, forward, finish = EXCHANGE_STEPS[kind](src_refs[idx], dst_refs[idx], *sem_refs[3 * idx:3 * idx + 3])
        if when == "start":
            pl.when(step == 0)(start)
            if forward is not None:
                pl.when(step == (3 * n_steps) // 4)(forward)
        else:
            pl.when(step == n_steps - 1)(finish)


def _causal_iotas(qi, tq, tk):
    row = qi * tq + lax.broadcasted_iota(jnp.int32, (tq, tk), 0)
    col = lax.broadcasted_iota(jnp.int32, (tq, tk), 1)
    return row, col


def _suffix_matrix(tk, inclusive):
    j = lax.broadcasted_iota(jnp.int32, (2 * tk, tk), 0) % tk
    s = lax.broadcasted_iota(jnp.int32, (2 * tk, tk), 1)
    return jnp.where((j >= s) if inclusive else (j > s), 1.0, 0.0).astype(BF16)


def _suffix_sum(x, mat):
    hi = x.astype(BF16)
    lo = (x - hi.astype(F32)).astype(BF16)
    return lax.dot_general(jnp.concatenate([hi, lo], axis=1), mat, NN_DIMS, preferred_element_type=F32)


def _attn_specs(t_rows, tq, heads, dk, dv, q_off, k_off, v_off):
    q_spec = pl.BlockSpec((tq, dk), lambda h, i: (i, q_off + h))
    kt_spec = pl.BlockSpec((dk, t_rows), lambda h, i: (k_off + h, 0))
    v_spec = pl.BlockSpec((t_rows, dv), lambda h, i: (0, v_off + h))
    return q_spec, kt_spec, v_spec


def _split_weights(weights, fine):
    hi = weights.astype(BF16)
    return (hi, (weights - hi.astype(F32)).astype(BF16)) if fine else (hi,)


def _weighted_values(split, v):
    return sum(lax.dot_general(part, v, NN_DIMS, preferred_element_type=F32) for part in split)


def _attn_fwd(kind, q_arr, kt_arr, v_arr, heads, dk, dv, scale, name, q_off=0, k_off=0, v_off=0, fcol=None, frow=None,
              tq=ATTN_TQ, tk=ATTN_TK, fine=True, hosted=()):
    t_rows = q_arr.shape[0]
    tq, tk = min(tq, t_rows), min(tk, t_rows)
    nq = t_rows // tq
    stick = kind == "stick"
    decay = fcol is not None
    n_in = 5 if decay else 3
    n_out = 2 if stick else 3
    n_host = len(hosted)

    def body(*refs):
        q_ref, kt_ref, v_ref = refs[:3]
        fcol_ref, frow_ref = (refs[3], refs[4]) if decay else (None, None)
        base = n_in + n_host
        o_ref, fine_ref = refs[base], refs[base + 1]
        lse_ref = None if stick else refs[base + 2]
        host_args = (hosted, refs[n_in:base], refs[base + n_out:base + n_out + n_host], refs[base + n_out + n_host:],
                     pl.program_id(0) * nq + pl.program_id(1), heads * nq)
        _run_hosted(*host_args, "start")
        qi = pl.program_id(1)
        q = q_ref[...]
        row, col = _causal_iotas(qi, tq, tk)
        n_kb = ((qi + 1) * tq + tk - 1) // tk
        n_diag = max(1, tq // tk)
        zeros_o = jnp.zeros((tq, dv), F32)

        no_weights = (jnp.zeros((tq, tk), BF16),) * (2 if fine else 1)

        def raw_logits(kb):
            return lax.dot_general(q, kt_ref[:, pl.ds(pl.multiple_of(kb * tk, tk), tk)], NN_DIMS,
                                   preferred_element_type=F32)

        def values(kb):
            return v_ref[pl.ds(pl.multiple_of(kb * tk, tk), tk), :]

        if stick:
            mat = _suffix_matrix(tk, inclusive=False)

            def make_step(masked):
                def step(i, carry):
                    c, acc, raw, prev = carry
                    raw_next = raw_logits(jnp.maximum(n_kb - 2 - i, 0))
                    d_acc = _weighted_values(prev, values(jnp.minimum(n_kb - i, n_kb - 1)))
                    log_beta, lom = _log2_gates(raw * (scale * LOG2E))
                    if masked:
                        strict = (col + (n_kb - 1 - i) * tk) < row
                        lom = jnp.where(strict, lom, 0.0)
                    w = jnp.exp2(log_beta + (_suffix_sum(lom, mat) + c))
                    if masked:
                        w = jnp.where(strict, w, 0.0)
                    return c + jnp.sum(lom, axis=1, keepdims=True), acc + d_acc, raw_next, _split_weights(w, fine)
                return step

            carry = (jnp.zeros((tq, 1), F32), zeros_o, raw_logits(n_kb - 1), no_weights)
            for i in range(n_diag):
                carry = make_step(True)(i, carry)
            _, acc, _, last = lax.fori_loop(n_diag, n_kb, make_step(False), carry)
            acc = acc + _weighted_values(last, values(0))
            o_ref[...] = acc.astype(o_ref.dtype)
            fine_ref[...] = acc
        else:
            fc = fcol_ref[...] * LOG2E if decay else None

            def make_step(masked):
                def step(kb, carry):
                    m, l, acc, raw, prev = carry
                    raw_next = raw_logits(jnp.minimum(kb + 1, n_kb - 1))
                    d_acc = _weighted_values(prev, values(jnp.maximum(kb - 1, 0)))
                    ks = pl.multiple_of(kb * tk, tk)
                    s = raw * (scale * LOG2E)
                    if decay:
                        s = (s + fc) - frow_ref[:, pl.ds(ks, tk)] * LOG2E
                    if masked:
                        s = jnp.where((col + ks) <= row, s, MASKED)
                    m_new = jnp.maximum(m, jnp.max(s, axis=1, keepdims=True))
                    alpha = jnp.exp2(m - m_new)
                    p = jnp.exp2(s - m_new)
                    l = alpha * l + jnp.sum(p, axis=1, keepdims=True)
                    return m_new, l, alpha * (acc + d_acc), raw_next, _split_weights(p, fine)
                return step

            carry = lax.fori_loop(0, n_kb - n_diag, make_step(False),
                                  (jnp.full((tq, 1), MASKED, F32), jnp.zeros((tq, 1), F32), zeros_o, raw_logits(0),
                                   no_weights))
            for d in range(n_diag):
                carry = make_step(True)(n_kb - n_diag + d, carry)
            m, l, acc, _, last = carry
            out = (acc + _weighted_values(last, values(n_kb - 1))) * (1.0 / l)
            o_ref[...] = out.astype(o_ref.dtype)
            fine_ref[...] = out
            lse_ref[...] = (m + jnp.log2(l)) * (1.0 / LOG2E)
        _run_hosted(*host_args, "finish")

    q_spec, k_spec, v_spec = _attn_specs(t_rows, tq, heads, dk, dv, q_off, k_off, v_off)
    stat_spec = pl.BlockSpec((None, tq, 1), lambda h, i: (h, i, 0))
    ins, in_specs = [q_arr, kt_arr, v_arr], [q_spec, k_spec, v_spec]
    if decay:
        ins += [fcol, frow]
        in_specs += [stat_spec, pl.BlockSpec((None, 1, t_rows), lambda h, i: (h, 0, 0))]
    o_spec = pl.BlockSpec((tq, dv), lambda h, i: (i, h))
    out_specs = [o_spec, o_spec]
    out_shape = [jax.ShapeDtypeStruct((t_rows, heads * dv), BF16), jax.ShapeDtypeStruct((t_rows, heads * dv), F32)]
    if not stick:
        out_specs.append(stat_spec)
        out_shape.append(jax.ShapeDtypeStruct((heads, t_rows, 1), F32))
    return tuple(pl.pallas_call(
        body, name=name, grid=(heads, nq), in_specs=in_specs + [ANY_SPEC] * n_host,
        out_specs=out_specs + [ANY_SPEC] * n_host,
        out_shape=out_shape + [_exchange_out_shape(kd, arr) for kd, arr in hosted],
        scratch_shapes=EXCHANGE_SCRATCH * n_host,
        compiler_params=_params("arbitrary" if n_host else "parallel", "arbitrary"),
    )(*ins, *[arr for _, arr in hosted]))


def _attn_bwd(kind, q_arr, k_arr, kt_arr, vt_arr, o_arr, do_arr, heads, dk, dv, scale, name, q_off=0, k_off=0, kt_off=0,
              vt_off=0, do_off=0, lse=None, fcol=None, frow=None, tq=ATTN_TQ, tk=ATTN_TK, hosted=()):
    t_rows = q_arr.shape[0]
    tq, tk = min(tq, t_rows), min(tk, t_rows)
    nq = t_rows // tq
    stick = kind == "stick"
    decay = fcol is not None
    n_in = 6 + (0 if stick else 1) + (2 if decay else 0)
    n_out = 5 if decay else 3
    n_host = len(hosted)

    def body(*refs):
        q_ref, k_ref, kt_ref, vt_ref, o_ref, do_ref = refs[:6]
        lse_ref = None if stick else refs[6]
        fcol_ref, frow_ref = (refs[7], refs[8]) if decay else (None, None)
        base = n_in + n_host
        dq_ref, dk_ref, dv_ref = refs[base:base + 3]
        dfcol_ref, dfrow_ref = (refs[base + 3], refs[base + 4]) if decay else (None, None)
        host_args = (hosted, refs[n_in:base], refs[base + n_out:base + n_out + n_host], refs[base + n_out + n_host:],
                     pl.program_id(0) * nq + pl.program_id(1), heads * nq)
        _run_hosted(*host_args, "start")
        qi = pl.program_id(1)

        @pl.when(qi == 0)
        def _():
            dk_ref[...] = jnp.zeros_like(dk_ref)
            dv_ref[...] = jnp.zeros_like(dv_ref)
            if decay:
                dfrow_ref[...] = jnp.zeros_like(dfrow_ref)

        q = q_ref[...]
        do = do_ref[...]
        delta = jnp.sum(do.astype(F32) * o_ref[...], axis=1, keepdims=True)
        row, col = _causal_iotas(qi, tq, tk)
        n_kb = ((qi + 1) * tq + tk - 1) // tk
        n_diag = max(1, tq // tk)

        no_pair = (jnp.zeros((tq, tk), BF16), jnp.zeros((tq, tk), BF16))

        def accumulate(kb, pair):
            at = pl.ds(pl.multiple_of(kb * tk, tk), tk)
            dk_ref[at, :] += lax.dot_general(pair[0], q, TN_DIMS, preferred_element_type=F32)
            dv_ref[at, :] += lax.dot_general(pair[1], do, TN_DIMS, preferred_element_type=F32)
            return lax.dot_general(pair[0], k_ref[at, :], NN_DIMS, preferred_element_type=F32)

        def raw_logits(kb):
            return lax.dot_general(q, kt_ref[:, pl.ds(pl.multiple_of(kb * tk, tk), tk)], NN_DIMS,
                                   preferred_element_type=F32)

        def d_weights(kb):
            return lax.dot_general(do, vt_ref[:, pl.ds(pl.multiple_of(kb * tk, tk), tk)], NN_DIMS,
                                   preferred_element_type=F32)

        if stick:
            mat_ex = _suffix_matrix(tk, inclusive=False)
            mat_in = _suffix_matrix(tk, inclusive=True)

            def make_step(masked):
                def step(i, carry):
                    c, gs, dq, raw, prev = carry
                    raw_next = raw_logits(jnp.maximum(n_kb - 2 - i, 0))
                    dw = d_weights(n_kb - 1 - i)
                    dq = dq + accumulate(jnp.minimum(n_kb - i, n_kb - 1), prev)
                    log_beta, log_omb = _log2_gates(raw * (scale * LOG2E))
                    lom = log_omb
                    if masked:
                        strict = (col + (n_kb - 1 - i) * tk) < row
                        lom = jnp.where(strict, log_omb, 0.0)
                    w = jnp.exp2(log_beta + (_suffix_sum(lom, mat_ex) + c))
                    if masked:
                        w = jnp.where(strict, w, 0.0)
                    g = w * dw
                    g_before = delta - (gs + _suffix_sum(g, mat_in))
                    dz = g * jnp.exp2(log_omb) - g_before * jnp.exp2(log_beta)
                    if masked:
                        dz = jnp.where(strict, dz, 0.0)
                    return (c + jnp.sum(lom, axis=1, keepdims=True), gs + jnp.sum(g, axis=1, keepdims=True), dq,
                            raw_next, ((dz * scale).astype(BF16), w.astype(BF16)))
                return step

            zero = jnp.zeros((tq, 1), F32)
            carry = (zero, zero, jnp.zeros((tq, dk), F32), raw_logits(n_kb - 1), no_pair)
            for i in range(n_diag):
                carry = make_step(True)(i, carry)
            _, _, dq, _, last = lax.fori_loop(n_diag, n_kb, make_step(False), carry)
            dq = dq + accumulate(0, last)
        else:
            lse_v = lse_ref[...] * LOG2E
            fc = fcol_ref[...] * LOG2E if decay else None

            def make_step(masked):
                def step(kb, carry):
                    dq, row_sum, raw, prev = carry
                    raw_next = raw_logits(jnp.minimum(kb + 1, n_kb - 1))
                    dp = d_weights(kb)
                    dq = dq + accumulate(jnp.maximum(kb - 1, 0), prev)
                    ks = pl.multiple_of(kb * tk, tk)
                    s = raw * (scale * LOG2E)
                    if decay:
                        s = (s + fc) - frow_ref[:, pl.ds(ks, tk)] * LOG2E
                    p = jnp.exp2(s - lse_v)
                    if masked:
                        p = jnp.where((col + ks) <= row, p, 0.0)
                    ds = p * (dp - delta)
                    if decay:
                        dfrow_ref[:, pl.ds(ks, tk)] += jnp.sum(ds, axis=0, keepdims=True)
                        row_sum = row_sum + jnp.sum(ds, axis=1, keepdims=True)
                    return dq, row_sum, raw_next, ((ds * scale).astype(BF16), p.astype(BF16))
                return step

            carry = lax.fori_loop(0, n_kb - n_diag, make_step(False),
                                  (jnp.zeros((tq, dk), F32), jnp.zeros((tq, 1), F32), raw_logits(0), no_pair))
            for d in range(n_diag):
                carry = make_step(True)(n_kb - n_diag + d, carry)
            dq, row_sum, _, last = carry
            dq = dq + accumulate(n_kb - 1, last)
            if decay:
                dfcol_ref[...] = row_sum
        dq_ref[...] = dq
        _run_hosted(*host_args, "finish")

    q_spec, kt_spec, _ = _attn_specs(t_rows, tq, heads, dk, dv, q_off, kt_off, 0)
    stat_spec = pl.BlockSpec((None, tq, 1), lambda h, i: (h, i, 0))
    frow_spec = pl.BlockSpec((None, 1, t_rows), lambda h, i: (h, 0, 0))
    ins = [q_arr, k_arr, kt_arr, vt_arr, o_arr, do_arr]
    in_specs = [q_spec, pl.BlockSpec((t_rows, dk), lambda h, i: (0, k_off + h)), kt_spec,
                pl.BlockSpec((dv, t_rows), lambda h, i: (vt_off + h, 0)), pl.BlockSpec((tq, dv), lambda h, i: (i, h)),
                pl.BlockSpec((tq, dv), lambda h, i: (i, do_off + h))]
    if not stick:
        ins.append(lse)
        in_specs.append(stat_spec)
    if decay:
        ins += [fcol, frow]
        in_specs += [stat_spec, frow_spec]
    out_specs = [pl.BlockSpec((tq, dk), lambda h, i: (i, h)), pl.BlockSpec((t_rows, dk), lambda h, i: (0, h)),
                 pl.BlockSpec((t_rows, dv), lambda h, i: (0, h))]
    out_shape = [jax.ShapeDtypeStruct((t_rows, heads * dk), F32), jax.ShapeDtypeStruct((t_rows, heads * dk), F32),
                 jax.ShapeDtypeStruct((t_rows, heads * dv), F32)]
    if decay:
        out_specs += [stat_spec, frow_spec]
        out_shape += [jax.ShapeDtypeStruct((heads, t_rows, 1), F32), jax.ShapeDtypeStruct((heads, 1, t_rows), F32)]
    return pl.pallas_call(
        body, name=name, grid=(heads, nq), in_specs=in_specs + [ANY_SPEC] * n_host,
        out_specs=out_specs + [ANY_SPEC] * n_host,
        out_shape=out_shape + [_exchange_out_shape(kd, arr) for kd, arr in hosted],
        scratch_shapes=EXCHANGE_SCRATCH * n_host,
        compiler_params=_params("arbitrary" if n_host else "parallel", "arbitrary"),
    )(*ins, *[arr for _, arr in hosted])


def _prefix_matrix(reverse):
    j = lax.broadcasted_iota(jnp.int32, (LANES, LANES), 0)
    s = lax.broadcasted_iota(jnp.int32, (LANES, LANES), 1)
    return jnp.where((j >= s) if reverse else (j <= s), 1.0, 0.0).astype(BF16)


def _chunk_cumsum(x, mat):
    return sum(lax.dot_general(part, mat, NN_DIMS, preferred_element_type=F32) for part in _split3(x))


def _gate_fwd(logit_t, bias_col):
    heads, t_rows = logit_t.shape

    def body(x_ref, b_ref, out_ref):
        mat = _prefix_matrix(reverse=False)

        def step(ci, carry):
            cs = pl.multiple_of(ci * LANES, LANES)
            pre = x_ref[:, pl.ds(cs, LANES)] + b_ref[...]
            log_f = jnp.minimum(pre, 0.0) - _log_sigmoid_parts(pre)
            out_ref[:, pl.ds(cs, LANES)] = _chunk_cumsum(log_f, mat) + carry
            return carry + jnp.sum(log_f, axis=1, keepdims=True)

        lax.fori_loop(0, t_rows // LANES, step, jnp.zeros((heads, 1), F32))

    return pl.pallas_call(body, name="gate_fwd", out_shape=jax.ShapeDtypeStruct((heads, t_rows), F32),
                          compiler_params=pltpu.CompilerParams(vmem_limit_bytes=VMEM_LIMIT))(logit_t, bias_col)


def _gate_bwd(dcum_t, logit_t, bias_col):
    heads, t_rows = logit_t.shape
    n_chunks = t_rows // LANES

    def body(d_ref, x_ref, b_ref, dx_ref, db_ref):
        mat = _prefix_matrix(reverse=True)

        def step(i, carry):
            tail, db = carry
            cs = pl.multiple_of((n_chunks - 1 - i) * LANES, LANES)
            d = d_ref[:, pl.ds(cs, LANES)]
            d_log_f = _chunk_cumsum(d, mat) + tail
            pre = x_ref[:, pl.ds(cs, LANES)] + b_ref[...]
            e = jnp.exp(-jnp.abs(pre))
            d_pre = d_log_f * (jnp.where(pre >= 0.0, e, 1.0) / (1.0 + e))
            dx_ref[:, pl.ds(cs, LANES)] = d_pre
            return tail + jnp.sum(d, axis=1, keepdims=True), db + jnp.sum(d_pre, axis=1, keepdims=True)

        zero = jnp.zeros((heads, 1), F32)
        _, db = lax.fori_loop(0, n_chunks, step, (zero, zero))
        db_ref[...] = db

    return pl.pallas_call(body, name="gate_bwd",
                          out_shape=(jax.ShapeDtypeStruct((heads, t_rows), F32), jax.ShapeDtypeStruct((heads, 1), F32)),
                          compiler_params=pltpu.CompilerParams(vmem_limit_bytes=VMEM_LIMIT))(dcum_t, logit_t, bias_col)


def _norm_fwd(x, g, name):
    return _rows_call(lambda xv, gv: _rms_fwd(xv, gv), name, [x], [g], [(x.shape[1], BF16)])


def _norm_bwd(x, g, dh, dres, name):
    def fn(xv, dhv, dresv, gv):
        dx, dg = _rms_bwd(xv, gv, dhv)
        dx = dresv + dx
        return dx, dx, dg
    return _rows_call(fn, name, [x, dh, dres], [g], [(x.shape[1], F32), (x.shape[1], BF16)], [((1, x.shape[1]), F32)])


def _loss_fwd_bwd(y, target):
    d_model = y.shape[1]

    def fn(yv, tv):
        err = yv - tv
        dy = err * (1.0 / d_model)
        return dy, dy, jnp.sum(jnp.sum(err * err, axis=1, keepdims=True), axis=0, keepdims=True)
    return _rows_call(fn, "loss", [y, target], [], [(d_model, F32), (d_model, BF16)], [((1, 1), F32)])


def _heads_apply(fn, n_heads, width, *tiles):
    return [fn(*[t[:, h * width:(h + 1) * width] for t in tiles]) for h in range(n_heads)]


def _fox_norm_fwd(pb, gq, gk, heads):
    width = heads * HEAD_DIM

    def fn(qk, gqv, gkv):
        q = jnp.concatenate(_heads_apply(lambda t: _rms_fwd(t, gqv), heads, HEAD_DIM, qk[:, :width]), axis=1)
        k = jnp.concatenate(_heads_apply(lambda t: _rms_fwd(t, gkv), heads, HEAD_DIM, qk[:, width:]), axis=1)
        return q, k
    return _rows_call(fn, "fox_norm_fwd", [(pb, 2 * width, 0)], [gq, gk], [(width, BF16), (width, BF16)])


def _fox_norm_bwd(pb, gq, gk, dq, dk, heads):
    width = heads * HEAD_DIM

    def fn(qk, dqv, dkv, gqv, gkv):
        res_q = _heads_apply(lambda t, d: _rms_bwd(t, gqv, d), heads, HEAD_DIM, qk[:, :width], dqv)
        res_k = _heads_apply(lambda t, d: _rms_bwd(t, gkv, d), heads, HEAD_DIM, qk[:, width:], dkv)
        dqk = jnp.concatenate([r[0] for r in res_q] + [r[0] for r in res_k], axis=1)
        return dqk, sum(r[1] for r in res_q), sum(r[1] for r in res_k)
    return _rows_call(fn, "fox_norm_bwd", [(pb, 2 * width, 0), dq, dk], [gq, gk], [(2 * width, BF16)],
                      [((1, HEAD_DIM), F32), ((1, HEAD_DIM), F32)])


def _lora_norm_fwd(down, gq, gkv, rank):
    def fn(dv, gqv, gkvv):
        return _rms_fwd(dv[:, :rank], gqv), _rms_fwd(dv[:, rank:], gkvv)
    return _rows_call(fn, "lora_norm_fwd", [(down, 2 * rank, 0)], [gq, gkv], [(rank, BF16), (rank, BF16)])


def _lora_norm_bwd(down, gq, gkv, dcq, dckv, dkpe, rank):
    def fn(dv, dcqv, dckvv, dkpev, gqv, gkvv):
        dxq, dgq = _rms_bwd(dv[:, :rank], gqv, dcqv)
        dxkv, dgkv = _rms_bwd(dv[:, rank:], gkvv, dckvv)
        return jnp.concatenate([dxq, dxkv, dkpev], axis=1), dgq, dgkv
    return _rows_call(fn, "lora_norm_bwd", [(down, 2 * rank, 0), dcq, dckv, dkpe], [gq, gkv],
                      [(2 * rank + LANES, BF16)], [((1, rank), F32), ((1, rank), F32)])


def _rope_tables(pos_col, inv_freq, sin_sign):
    def fn(pos, invf, sign):
        ang = pos.astype(F32) * invf
        return jnp.cos(ang) * jnp.abs(sign), jnp.sin(ang) * sign
    return _rows_call(fn, "rope_tables", [pos_col], [inv_freq, sin_sign], [(LANES, F32), (LANES, F32)])


def _mla_prep_fwd(q_raw, kv, down, kpe_block, qg, kg, cos_t, sin_s):
    def fn(qv, kvv, kpe, cosv, sinv, qgv, kgv):
        qs, ks, vs = [], [], []
        for h in range(MLA_HEADS):
            qn = _rms_fwd(qv[:, h * MLA_PAD_DIM:(h + 1) * MLA_PAD_DIM], qgv, MLA_QK_DIM)
            qs += [qn[:, :HEAD_DIM], _rope_fwd(qn[:, HEAD_DIM:], cosv, sinv)]
            k_full = jnp.concatenate([kvv[:, h * MLA_PAD_DIM:h * MLA_PAD_DIM + HEAD_DIM], kpe], axis=1)
            kn = _rms_fwd(k_full, kgv, MLA_QK_DIM)
            ks += [kn[:, :HEAD_DIM], _rope_fwd(kn[:, HEAD_DIM:], cosv, sinv)]
            vs.append(kvv[:, h * MLA_PAD_DIM + HEAD_DIM:(h + 1) * MLA_PAD_DIM])
        return jnp.concatenate(qs, axis=1), jnp.concatenate(ks, axis=1), jnp.concatenate(vs, axis=1)
    wide = MLA_HEADS * MLA_PAD_DIM
    return _rows_call(fn, "mla_prep_fwd", [q_raw, kv, (down, LANES, kpe_block), cos_t, sin_s], [qg, kg],
                      [(wide, BF16), (wide, BF16), (MLA_HEADS * HEAD_DIM, BF16)], tile=128)


def _mla_prep_bwd(q_raw, kv, down, kpe_block, qg, kg, cos_t, sin_s, dq, dk, dv):
    def fn(qv, kvv, kpe, cosv, sinv, dqv, dkv, dvv, qgv, kgv):
        dqs, dkvs = [], []
        dkpe = jnp.zeros_like(kpe)
        dqg = jnp.zeros_like(qgv)
        dkg = jnp.zeros_like(kgv)
        for h in range(MLA_HEADS):
            lo, hi = h * MLA_PAD_DIM, (h + 1) * MLA_PAD_DIM
            dqn = jnp.concatenate([dqv[:, lo:lo + HEAD_DIM], _rope_bwd(dqv[:, lo + HEAD_DIM:hi], cosv, sinv)], axis=1)
            dqh, dg = _rms_bwd(qv[:, lo:hi], qgv, dqn, MLA_QK_DIM)
            dqs.append(dqh)
            dqg = dqg + dg
            k_full = jnp.concatenate([kvv[:, lo:lo + HEAD_DIM], kpe], axis=1)
            dkn = jnp.concatenate([dkv[:, lo:lo + HEAD_DIM], _rope_bwd(dkv[:, lo + HEAD_DIM:hi], cosv, sinv)], axis=1)
            dkh, dg = _rms_bwd(k_full, kgv, dkn, MLA_QK_DIM)
            dkg = dkg + dg
            dkpe = dkpe + dkh[:, HEAD_DIM:]
            dkvs += [dkh[:, :HEAD_DIM], dvv[:, h * HEAD_DIM:(h + 1) * HEAD_DIM]]
        return jnp.concatenate(dqs, axis=1), jnp.concatenate(dkvs, axis=1), dkpe, dqg, dkg
    wide = MLA_HEADS * MLA_PAD_DIM
    return _rows_call(fn, "mla_prep_bwd", [q_raw, kv, (down, LANES, kpe_block), cos_t, sin_s, dq, dk, dv], [qg, kg],
                      [(wide, BF16), (wide, BF16), (LANES, F32)], [((1, MLA_PAD_DIM), F32), ((1, MLA_PAD_DIM), F32)],
                      tile=128)


def _sqrelu_up(acc):
    return acc, jnp.square(jnp.maximum(acc, 0.0))


def _sqrelu_grad(acc, u):
    return (acc * (2.0 * jnp.maximum(u, 0.0)),)


def _mlp_fwd(x, g, w_up, w_down, tag):
    h = _norm_fwd(x, g, f"mlp_norm_fwd{tag}")
    u, a = _matmul(h, w_up, "nn", f"mlp_up{tag}", (F32, BF16), _sqrelu_up)
    return _matmul(a, w_down, "nn", f"mlp_down{tag}", (F32,), _add_residual, (x,)), (h, u, a)


def _mlp_bwd(x, g, w_up, w_down, saved, dy, dy16, tag):
    h, u, a = saved
    dw_down = _matmul(a, dy16, "tn", f"mlp_dwdown{tag}", (BF16,))
    du = _matmul(dy16, w_down, "nt", f"mlp_du{tag}", (BF16,), _sqrelu_grad, (u,))
    dw_up = _matmul(h, du, "tn", f"mlp_dwup{tag}", (BF16,), column_slots=True)
    dh = _matmul(du, w_up, "nt", f"mlp_dh{tag}")
    dx, dx16, dg = _norm_bwd(x, g, dh, dy, f"mlp_norm_bwd{tag}")
    return dx, dx16, dg, dw_up, dw_down


def _local_step(x, pos_col, target, w, dist=None):
    w = dict(w)
    hs = w["w_a"].shape[1] // (4 * HEAD_DIM)
    sb_w = hs * HEAD_DIM
    grads, received = {}, {}

    def gather_in(group):
        return [("gather", blk) for blk in dist["blocks"][group]] if dist else []

    def exchange_in(group):
        return [("all_to_all", slots) for slots in dist["slots_of"](group, grads)] if dist else []

    h0 = _norm_fwd(x, w["ln_mix0"], "mix0_norm_fwd")
    pa = _matmul(h0, w["w_a"], "nn", "in_proj_a", (BF16,))
    pb = _matmul(h0, w["w_b"], "nn", "in_proj_b")
    pat = pa[:, sb_w:].T
    o_sb, o_sb_fine, *got = _attn_fwd("stick", pa, pat, pa, hs, HEAD_DIM, HEAD_DIM, HEAD_DIM ** -0.5, "stick_fwd",
                                      q_off=0, k_off=0, v_off=2 * hs, tq=ATTN_TQ_WIDE,
                                      hosted=gather_in("with_stick_fwd"))
    if dist:
        w.update(dist["weights_of"]("with_stick_fwd", got))
    logit_t = pb[:, 2 * sb_w:2 * sb_w + hs].T
    bias_col = w["b_f"][0, :hs].reshape(hs, 1)
    f_cum = _gate_fwd(logit_t, bias_col)
    f_col, f_row = f_cum[:, :, None], f_cum[:, None, :]
    qf, kf = _fox_norm_fwd(pb, w["fox_q_g"], w["fox_k_g"], hs)
    kft = kf.T
    o_fx, o_fx_fine, lse_fx, *got = _attn_fwd("softmax", qf, kft, pa, hs, HEAD_DIM, HEAD_DIM, HEAD_DIM ** -0.5,
                                              "fox_fwd", v_off=3 * hs, fcol=f_col, frow=f_row, tq=ATTN_TQ_WIDE, tk=ATTN_TK_SOFTMAX,
                                              fine=False,
                                              hosted=gather_in("with_fox_fwd"))
    if dist:
        w.update(dist["weights_of"]("with_fox_fwd", got))
    o0 = jnp.concatenate([o_sb, o_fx], axis=1)
    x1 = _matmul(o0, w["w_o0"], "nn", "out_proj0", (F32,), _add_residual, (x,))
    x2, mlp0 = _mlp_fwd(x1, w["ln_mlp0"], w["w_up0"], w["w_dn0"], "0")

    rank = w["w_uq"].shape[0]
    h2 = _norm_fwd(x2, w["ln_mix1"], "mix1_norm_fwd")
    down = _matmul(h2, w["w_down"], "nn", "mla_down")
    cqn, ckvn = _lora_norm_fwd(down, w["q_a_g"], w["kv_a_g"], rank)
    q_raw = _matmul(cqn, w["w_uq"], "nn", "mla_uq")
    kv = _matmul(ckvn, w["w_ukv"], "nn", "mla_ukv")
    cos_t, sin_s = _rope_tables(pos_col, w["inv_freq"], w["sin_sign"])
    kpe_block = 2 * rank // LANES
    qm, km, vm = _mla_prep_fwd(q_raw, kv, down, kpe_block, w["mla_q_g"], w["mla_k_g"], cos_t, sin_s)
    kmt, vmt = km.T, vm.T
    o_m, o_m_fine, lse_m, *got = _attn_fwd("softmax", qm, kmt, vm, MLA_HEADS, MLA_PAD_DIM, HEAD_DIM,
                                           MLA_QK_DIM ** -0.5, "mla_fwd", tq=ATTN_TQ_WIDE, tk=ATTN_TK_SOFTMAX,
                                           fine=False, hosted=gather_in("mlp1"))
    if dist:
        w.update(dist["weights_of"]("mlp1", got))
    x3 = _matmul(o_m, w["w_o1"], "nn", "out_proj1", (F32,), _add_residual, (x2,))
    x4, mlp1 = _mlp_fwd(x3, w["ln_mlp1"], w["w_up1"], w["w_dn1"], "1")

    dy, dy16, sq_err = _loss_fwd_bwd(x4, target)

    dx3, dx3_16, grads["ln_mlp1"], grads["w_up1"], grads["w_dn1"] = _mlp_bwd(
        x3, w["ln_mlp1"], w["w_up1"], w["w_dn1"], mlp1, dy, dy16, "1")
    grads["w_o1"] = _matmul(o_m, dx3_16, "tn", "dw_o1", (BF16,))
    do_m = _matmul(dx3_16, w["w_o1"], "nt", "do_mla", (BF16,))
    dqm, dkm, dvm, *got = _attn_bwd("softmax", qm, km, kmt, vmt, o_m_fine, do_m, MLA_HEADS, MLA_PAD_DIM, HEAD_DIM,
                                    MLA_QK_DIM ** -0.5, "mla_bwd", lse=lse_m, tq=ATTN_TQ_WIDE, tk=ATTN_TK_SOFTMAX,
                                    hosted=exchange_in("mlp1"))
    received["mlp1"] = got
    dq_raw, dkv, dkpe, grads["mla_q_g"], grads["mla_k_g"] = _mla_prep_bwd(
        q_raw, kv, down, kpe_block, w["mla_q_g"], w["mla_k_g"], cos_t, sin_s, dqm, dkm, dvm)
    grads["w_uq"] = _matmul(cqn, dq_raw, "tn", "dw_uq", (BF16,))
    grads["w_ukv"] = _matmul(ckvn, dkv, "tn", "dw_ukv", (BF16,))
    dcqn = _matmul(dq_raw, w["w_uq"], "nt", "d_cq")
    dckvn = _matmul(dkv, w["w_ukv"], "nt", "d_ckv")
    ddown, grads["q_a_g"], grads["kv_a_g"] = _lora_norm_bwd(down, w["q_a_g"], w["kv_a_g"], dcqn, dckvn, dkpe, rank)
    grads["w_down"] = _matmul(h2, ddown, "tn", "dw_down", (BF16,))
    dh2 = _matmul(ddown, w["w_down"], "nt", "d_h2")
    dx2, dx2_16, grads["ln_mix1"] = _norm_bwd(x2, w["ln_mix1"], dh2, dx3, "mix1_norm_bwd")

    dx1, dx1_16, grads["ln_mlp0"], grads["w_up0"], grads["w_dn0"] = _mlp_bwd(
        x1, w["ln_mlp0"], w["w_up0"], w["w_dn0"], mlp0, dx2, dx2_16, "0")
    grads["w_o0"] = _matmul(o0, dx1_16, "tn", "dw_o0", (BF16,))
    do0 = _matmul(dx1_16, w["w_o0"], "nt", "do_mix0", (BF16,))
    dq_sb, dk_sb, dv_sb, *got = _attn_bwd("stick", pa, pa, pat, pat, o_sb_fine, do0, hs, HEAD_DIM, HEAD_DIM,
                                          HEAD_DIM ** -0.5, "stick_bwd", q_off=0, k_off=hs, kt_off=0, vt_off=hs, do_off=0,
                                          tq=ATTN_TQ_WIDE,
                                          hosted=exchange_in("with_stick_bwd"))
    received["with_stick_bwd"] = got
    dqf, dkf, dv_fx, ds_rows, ds_cols, *got = _attn_bwd(
        "softmax", qf, kf, kft, pat, o_fx_fine, do0, hs, HEAD_DIM, HEAD_DIM, HEAD_DIM ** -0.5, "fox_bwd", vt_off=2 * hs,
        tq=ATTN_TQ_WIDE, tk=ATTN_TK_SOFTMAX,
        do_off=hs, lse=lse_fx, fcol=f_col, frow=f_row, hosted=exchange_in("with_fox_bwd"))
    received["with_fox_bwd"] = got
    dqk_fx, grads["fox_q_g"], grads["fox_k_g"] = _fox_norm_bwd(pb, w["fox_q_g"], w["fox_k_g"], dqf, dkf, hs)
    dlogit_t, db_f = _gate_bwd(ds_rows[:, :, 0] - ds_cols[:, 0, :], logit_t, bias_col)
    grads["b_f"] = db_f.reshape(1, hs)
    dpa = jnp.concatenate([dq_sb.astype(BF16), dk_sb.astype(BF16), dv_sb.astype(BF16), dv_fx.astype(BF16)], axis=1)
    dlogit_pad = jnp.pad(dlogit_t.T.astype(BF16), ((0, 0), (0, pb.shape[1] - 2 * sb_w - hs)))
    dpb = jnp.concatenate([dqk_fx, dlogit_pad], axis=1)
    grads["w_a"] = _matmul(h0, dpa, "tn", "dw_a", (BF16,))
    grads["w_b"] = _matmul(h0, dpb, "tn", "dw_b", (BF16,))
    parts = exchange_in("mix0")
    res_b = _matmul(dpb, w["w_b"], "nt", "d_h0_b", hosted=parts[:1])
    dh0 = res_b[0] if dist else res_b
    res_a = _matmul(dpa, w["w_a"], "nt", "d_h0_a", (F32,), _add_residual, (dh0,), hosted=parts[1:])
    dh0, received["mix0"] = (res_a[0], [res_b[1], res_a[1]]) if dist else (res_a, [])
    grad_x, _, grads["ln_mix0"] = _norm_bwd(x, w["ln_mix0"], dh0, dx1, "mix0_norm_bwd")
    return sq_err, grad_x, grads, received


PIECES = {
    "sf_w_in": ("sf_w_in", 0, 1), "sf_w_o": ("sf_w_o", 0, 0), "mla_w_down": ("mla_w_down", 0, 0),
    "mla_w_uq": ("mla_w_uq", 0, 1), "mla_w_ukv": ("mla_w_ukv", 0, 1), "mla_w_o": ("mla_w_o", 0, 0),
    "mlp_w_up0": ("mlp_w_up", 0, 1), "mlp_w_up1": ("mlp_w_up", 1, 1),
    "mlp_w_down0": ("mlp_w_down", 0, 0), "mlp_w_down1": ("mlp_w_down", 1, 0),
}
GROUPS = {
    "mix0": ["sf_w_in"], "mla": ["mla_w_down", "mla_w_uq", "mla_w_ukv", "mla_w_o"], "mlp1": ["mlp_w_up1", "mlp_w_down1"],
}
GROUPS["with_stick_fwd"] = ["sf_w_o", "mlp_w_up0"]
GROUPS["with_fox_fwd"] = GROUPS["mla"] + ["mlp_w_down0"]
GROUPS["with_stick_bwd"] = GROUPS["mla"] + ["mlp_w_up0"]
GROUPS["with_fox_bwd"] = ["mlp_w_down0", "sf_w_o"]
SMALL = ["ln_mix_g", "ln_mlp_g", "sf_b_f", "fox_q_g", "fox_k_g", "mla_q_a_g", "mla_kv_a_g", "mla_q_g", "mla_k_g"]
ALL_W = ["ln_mix_g", "ln_mlp_g", "sf_w_in", "sf_b_f", "fox_q_g", "fox_k_g", "sf_w_o", "mla_w_down", "mla_q_a_g",
         "mla_kv_a_g", "mla_w_uq", "mla_w_ukv", "mla_q_g", "mla_k_g", "mla_w_o", "mlp_w_up", "mlp_w_down"]


def _weights_mix0(full, small):
    w_in = full["sf_w_in"]
    d_model = w_in.shape[0]
    n_fx = small["sf_b_f"].shape[1]
    sb_w = (w_in.shape[1] - n_fx) // 6
    cols = lambda i: w_in[:, i * sb_w:(i + 1) * sb_w]
    w_a = jnp.concatenate([cols(0), cols(1), cols(2), cols(5)], axis=1)
    w_b = jnp.concatenate([cols(3), cols(4), w_in[:, 6 * sb_w:], jnp.zeros((d_model, LANES - n_fx), w_in.dtype)], axis=1)
    half = ROPE_DIM // 2
    inv_freq = ROPE_THETA ** (-jnp.arange(half, dtype=F32) / half)
    zeros64 = jnp.zeros((ROPE_DIM,), F32)
    pad256 = lambda g: jnp.pad(g, ((0, 0), (0, MLA_PAD_DIM - MLA_QK_DIM)))
    pad_lanes = lambda g: jnp.pad(g, ((0, 0), (0, LANES - g.shape[1])))
    return dict(
        ln_mix0=small["ln_mix_g"][0:1], ln_mix1=small["ln_mix_g"][1:2],
        ln_mlp0=small["ln_mlp_g"][0:1], ln_mlp1=small["ln_mlp_g"][1:2],
        w_a=w_a, w_b=w_b, b_f=pad_lanes(small["sf_b_f"]), fox_q_g=small["fox_q_g"], fox_k_g=small["fox_k_g"],
        q_a_g=small["mla_q_a_g"], kv_a_g=small["mla_kv_a_g"],
        mla_q_g=pad256(small["mla_q_g"]), mla_k_g=pad256(small["mla_k_g"]),
        inv_freq=jnp.concatenate([inv_freq, inv_freq, zeros64]).reshape(1, LANES),
        sin_sign=jnp.concatenate([-jnp.ones((half,), F32), jnp.ones((half,), F32), zeros64]).reshape(1, LANES),
    )


def _late_weights(full):
    plain = {"sf_w_o": "w_o0", "mlp_w_up0": "w_up0", "mlp_w_down0": "w_dn0", "mla_w_ukv": "w_ukv", "mla_w_o": "w_o1",
             "mlp_w_up1": "w_up1", "mlp_w_down1": "w_dn1"}
    out = {key: full[p] for p, key in plain.items() if p in full}
    if "mla_w_down" in full:
        out["w_down"] = jnp.pad(full["mla_w_down"], ((0, 0), (0, LANES - ROPE_DIM)))
    if "mla_w_uq" in full:
        rank = full["mla_w_uq"].shape[0]
        w_uq = full["mla_w_uq"].reshape(rank, MLA_HEADS, MLA_QK_DIM)
        out["w_uq"] = jnp.pad(w_uq, ((0, 0), (0, 0), (0, MLA_PAD_DIM - MLA_QK_DIM))).reshape(rank, MLA_HEADS * MLA_PAD_DIM)
    return out


def _piece_slots(g, piece):
    if piece in ("mlp_w_up0", "mlp_w_up1"):
        return g[{"mlp_w_up0": "w_up0", "mlp_w_up1": "w_up1"}[piece]]
    return _shard_stack(_piece_grad(g, piece), PIECES[piece][2])


def _piece_grad(g, piece):
    if piece in ("mlp_w_up0", "mlp_w_up1"):
        return _unshard(_piece_slots(g, piece), PIECES[piece][2])
    if piece == "sf_w_in":
        n_fx = g["b_f"].shape[1]
        ga, gb = g["w_a"], g["w_b"]
        sb_w = ga.shape[1] // 4
        ca = lambda i: ga[:, i * sb_w:(i + 1) * sb_w]
        return jnp.concatenate([ca(0), ca(1), ca(2), gb[:, :sb_w], gb[:, sb_w:2 * sb_w], ca(3),
                                gb[:, 2 * sb_w:2 * sb_w + n_fx]], axis=1)
    if piece == "mla_w_uq":
        rank = g["w_uq"].shape[0]
        return g["w_uq"].reshape(rank, MLA_HEADS, MLA_PAD_DIM)[:, :, :MLA_QK_DIM].reshape(rank, MLA_HEADS * MLA_QK_DIM)
    if piece == "mla_w_down":
        return g["w_down"][:, :g["w_down"].shape[1] - (LANES - ROPE_DIM)]
    return g[{"sf_w_o": "w_o0", "mla_w_ukv": "w_ukv", "mla_w_o": "w_o1", "mlp_w_up0": "w_up0", "mlp_w_up1": "w_up1",
              "mlp_w_down0": "w_dn0", "mlp_w_down1": "w_dn1"}[piece]]


def _small_grads(g):
    return {
        "ln_mix_g": jnp.concatenate([g["ln_mix0"], g["ln_mix1"]], axis=0),
        "ln_mlp_g": jnp.concatenate([g["ln_mlp0"], g["ln_mlp1"]], axis=0),
        "sf_b_f": g["b_f"], "fox_q_g": g["fox_q_g"], "fox_k_g": g["fox_k_g"],
        "mla_q_a_g": g["q_a_g"], "mla_kv_a_g": g["kv_a_g"],
        "mla_q_g": g["mla_q_g"][:, :MLA_QK_DIM], "mla_k_g": g["mla_k_g"][:, :MLA_QK_DIM],
    }


PACK_TILE = 1024


def _as_rows(a, row_multiple=16):
    flat = a.reshape(-1)
    rows = -(-flat.shape[0] // LANES)
    rows = -(-rows // row_multiple) * row_multiple
    return jnp.pad(flat, (0, rows * LANES - flat.shape[0])).reshape(rows, LANES)


def _pack_rows(parts, axis, dtype, row_multiple=PACK_TILE, spare_rows=0):
    used = sum(p.shape[axis] for p in parts)
    shape = list(parts[0].shape)
    shape[axis] = -(-used // row_multiple) * row_multiple + spare_rows - used
    return jnp.concatenate([p.astype(dtype) for p in parts] + [jnp.ones(shape, dtype)], axis=axis)


def _unshard(stack, axis):
    moved = jnp.moveaxis(stack, 0, axis)
    shape = list(stack.shape[1:])
    shape[axis] *= stack.shape[0]
    return moved.reshape(shape)


def _shard_stack(full, axis):
    shape = list(full.shape)
    shape[axis:axis + 1] = [N_DEV, shape[axis] // N_DEV]
    return jnp.moveaxis(full.reshape(shape), axis, 0)


OPT_TILE_ELEMS = 128 * 1024


def _row_tile(rows, cols):
    best = 16
    for t in range(16, rows + 1, 16):
        if rows % t == 0 and t * cols <= OPT_TILE_ELEMS:
            best = t
    assert rows % best == 0
    return best


def _cast_bf16(a, name):
    return _rows_call(lambda v: v, name, [a], [], [(a.shape[1], BF16)], tile=_row_tile(*a.shape))


def _adam_math(w, g, m, v):
    m = ADAM_B1 * m + (1.0 - ADAM_B1) * g
    v = ADAM_B2 * v + (1.0 - ADAM_B2) * jnp.square(g)
    m_hat = m / (1.0 - ADAM_B1 ** ADAM_STEP)
    v_hat = v / (1.0 - ADAM_B2 ** ADAM_STEP)
    delta = -ADAM_LR * (m_hat / (jnp.sqrt(v_hat) + ADAM_EPS) + ADAM_WD * w)
    return delta, m, v


def _adam_big(recvs, w, m, v, name, hosted=()):
    layers, rows, cols = w.shape
    tile = _row_tile(rows, cols)
    n_tiles = rows // tile
    n_host = len(hosted)

    def body(*refs):
        recv_refs = refs[:layers]
        w_ref, m_ref, v_ref = refs[layers:layers + 3]
        base = layers + 3 + n_host
        g_ref, d_ref, nm_ref, nv_ref = refs[base:base + 4]
        layer = pl.program_id(0)
        host_args = (hosted, refs[layers + 3:base], refs[base + 4:base + 4 + n_host], refs[base + 4 + n_host:],
                     layer * n_tiles + pl.program_id(1), layers * n_tiles)
        _run_hosted(*host_args, "start")

        def total(r_ref):
            acc = r_ref[0].astype(F32)
            for s in range(1, N_DEV):
                acc = acc + r_ref[s].astype(F32)
            return acc

        g = total(recv_refs[0])
        for j in range(1, layers):
            g = jnp.where(layer == j, total(recv_refs[j]), g)
        delta, nm, nv = _adam_math(w_ref[...], g, m_ref[...], v_ref[...])
        g_ref[...] = g
        d_ref[...] = delta
        nm_ref[...] = nm
        nv_ref[...] = nv
        _run_hosted(*host_args, "finish")

    def recv_spec(j):
        return pl.BlockSpec((N_DEV, tile, cols),
                            lambda l, i: (0, jnp.where(l == j, i, jnp.where(l < j, 0, n_tiles - 1)), 0))

    spec = pl.BlockSpec((None, tile, cols), lambda l, i: (l, i, 0))
    out = jax.ShapeDtypeStruct(w.shape, F32)
    return pl.pallas_call(
        body, name=name, grid=(layers, n_tiles),
        in_specs=[recv_spec(j) for j in range(layers)] + [spec] * 3 + [ANY_SPEC] * n_host,
        out_specs=[spec] * 4 + [ANY_SPEC] * n_host,
        out_shape=[out] * 4 + [_exchange_out_shape(kd, arr) for kd, arr in hosted],
        scratch_shapes=EXCHANGE_SCRATCH * n_host, compiler_params=_params("arbitrary", "arbitrary"),
    )(*recvs, w, m, v, *[arr for _, arr in hosted])


def _sum_slots(gathered):
    rows = gathered.shape[1]

    def body(r_ref, o_ref):
        acc = r_ref[0]
        for s in range(1, N_DEV):
            acc = acc + r_ref[s]
        o_ref[...] = acc

    return pl.pallas_call(body, name="sum_small", out_shape=jax.ShapeDtypeStruct((rows, LANES), F32))(gathered)


def _adam_small(w, g, m, v):
    def fn(wv, gv, mv, vv):
        return _adam_math(wv, gv, mv, vv)
    return _rows_call(fn, "adam_small", [w, g, m, v], [], [(LANES, F32)] * 3, tile=w.shape[0])


def kernel(x, positions, ln_mix_g, ln_mlp_g, sf_w_in, sf_b_f, fox_q_g, fox_k_g, sf_w_o, mla_w_down, mla_q_a_g, mla_kv_a_g, mla_w_uq, mla_w_ukv, mla_q_g, mla_k_g, mla_w_o, mlp_w_up, mlp_w_down, loss_target, m_ln_mix_g, m_ln_mlp_g, m_sf_w_in, m_sf_b_f, m_fox_q_g, m_fox_k_g, m_sf_w_o, m_mla_w_down, m_mla_q_a_g, m_mla_kv_a_g, m_mla_w_uq, m_mla_w_ukv, m_mla_q_g, m_mla_k_g, m_mla_w_o, m_mlp_w_up, m_mlp_w_down, v_ln_mix_g, v_ln_mlp_g, v_sf_w_in, v_sf_b_f, v_fox_q_g, v_fox_k_g, v_sf_w_o, v_mla_w_down, v_mla_q_a_g, v_mla_kv_a_g, v_mla_w_uq, v_mla_w_ukv, v_mla_q_g, v_mla_k_g, v_mla_w_o, v_mlp_w_up, v_mlp_w_down):
    given = dict(locals())
    wts = {n: given[n] for n in ALL_W}
    mom = {n: given["m_" + n] for n in ALL_W}
    var = {n: given["v_" + n] for n in ALL_W}
    me = 4 * lax.axis_index("x") + 2 * lax.axis_index("y") + lax.axis_index("c")
    t_rows, d_model = x.shape[1], x.shape[2]
    big = sorted({name for name, _, _ in PIECES.values()})

    def whole_pieces(gathered, group):
        return {p: _unshard(s, PIECES[p][2]) for p, s in zip(GROUPS[group], gathered)}

    def w_in_slot_parts(grads):
        slots = _shard_stack(_piece_grad(grads, "sf_w_in"), PIECES["sf_w_in"][2])
        cuts = [0] + [(slots.shape[1] * f // 64) // 16 * 16 for f in (11, 32, 49)] + [slots.shape[1]]
        return [slots[:, a:b] for a, b in zip(cuts, cuts[1:])]

    def slots_of(group, grads):
        if group == "mix0":
            return w_in_slot_parts(grads)[:2]
        slots = [_piece_slots(grads, p) for p in GROUPS[group]]
        assert all(s.shape[0] == N_DEV for s in slots)
        return slots

    cast = {n: _cast_bf16(wts[n].reshape(-1, wts[n].shape[2]), f"cast_{n}").reshape(wts[n].shape) for n in big}
    blocks = {grp: [cast[PIECES[p][0]][PIECES[p][1]] for p in GROUPS[grp]] for grp in ("mix0", "with_stick_fwd", "with_fox_fwd", "mlp1")}
    mix0 = whole_pieces(_exchange("gather", blocks["mix0"], "gather_mix0"), "mix0")
    gains, = _exchange("gather", [_as_rows(jnp.concatenate([mla_q_a_g, mla_kv_a_g], axis=1))], "gather_gains")
    lora_n = mla_q_a_g.shape[1]
    gains_flat = gains.reshape(N_DEV, -1)[:, :2 * lora_n]
    small = dict(ln_mix_g=ln_mix_g, ln_mlp_g=ln_mlp_g, sf_b_f=sf_b_f, fox_q_g=fox_q_g, fox_k_g=fox_k_g,
                 mla_q_a_g=gains_flat[:, :lora_n].reshape(1, -1), mla_kv_a_g=gains_flat[:, lora_n:].reshape(1, -1),
                 mla_q_g=mla_q_g, mla_k_g=mla_k_g)
    dist = dict(blocks=blocks, slots_of=slots_of,
                weights_of=lambda grp, gathered: _late_weights(whole_pieces(gathered, grp)))
    sq_err, grad_x, g, received = _local_step(x[0], positions.reshape(t_rows, 1), loss_target[0],
                                              _weights_mix0(mix0, small), dist)

    recv_of = {p: r for grp in ("mlp1", "with_stick_bwd", "with_fox_bwd") for p, r in zip(GROUPS[grp], received[grp])}
    late_parts = dict(zip(("mlp_w_down", "mlp_w_up"), w_in_slot_parts(g)[2:]))
    w_in_recv = list(received["mix0"])
    results = {kind: {} for kind in ("grad", "delta", "new_m", "new_v")}
    for n in sorted(big, key=lambda name: (name == "sf_w_in", name not in late_parts)):
        if n == "sf_w_in":
            recv_of["sf_w_in"] = jnp.concatenate(w_in_recv, axis=1)
        layers = [p for _, p in sorted((layer, p) for p, (name, layer, _) in PIECES.items() if name == n)]
        hosted = [("all_to_all", late_parts[n])] if n in late_parts else []
        outs = _adam_big([recv_of[p] for p in layers], wts[n], mom[n], var[n], f"adam_{n}", hosted)
        w_in_recv += outs[4:]
        for kind, out in zip(("grad", "delta", "new_m", "new_v"), outs[:4]):
            results[kind][n] = out

    small_g = _small_grads(g)
    small_parts = [_as_rows(small_g[n], 8) for n in SMALL] + [_as_rows(sq_err, 8)]
    small_sum = _sum_slots(_exchange("gather", [_pack_rows(small_parts, 0, F32, 8, 8)], "gather_small_grads")[0])
    red, off = {}, 0
    for n, p in zip(SMALL + ["loss"], small_parts):
        red[n] = small_sum[off:off + p.shape[0]].reshape(-1)
        off += p.shape[0]
    loss = 0.5 * red["loss"][0] / d_model
    for n in SMALL:
        if n in ("mla_q_a_g", "mla_kv_a_g"):
            results["grad"][n] = lax.dynamic_slice(red[n], (me * lora_n,), (lora_n,)).reshape(wts[n].shape)
        else:
            results["grad"][n] = red[n][:wts[n].size].reshape(wts[n].shape)
    pack_small = lambda d: jnp.concatenate([_as_rows(d[n], 8) for n in SMALL], axis=0)
    small_out = _adam_small(pack_small(wts), pack_small(results["grad"]), pack_small(mom), pack_small(var))
    off = 0
    for n in SMALL:
        r = _as_rows(wts[n], 8).shape[0]
        for kind, packed in zip(["delta", "new_m", "new_v"], small_out):
            results[kind][n] = packed[off:off + r].reshape(-1)[:wts[n].size].reshape(wts[n].shape)
        off += r

    outs = [loss, grad_x[None]]
    for kind in ["grad", "delta", "new_m", "new_v"]:
        outs += [results[kind][n] for n in ALL_W]
    return tuple(outs)
```

```python
import functools

import jax
import jax.numpy as jnp
from jax import lax
from jax.experimental import pallas as pl
from jax.experimental.pallas import tpu as pltpu

F32 = jnp.float32
BF16 = jnp.bfloat16

NORM_EPS = 1e-6
ROPE_THETA = 10000.0
HEAD_DIM = 128
ROPE_DIM = 64
MLA_HEADS = 16
MLA_QK_DIM = 192
MLA_PAD_DIM = 256
ADAM_LR, ADAM_B1, ADAM_B2, ADAM_EPS, ADAM_WD, ADAM_STEP = 0.001, 0.9, 0.999, 1e-08, 0.01, 10

N_DEV = 8
LANES = 128
VMEM_LIMIT = 56 * 1024 * 1024
MATMUL_VMEM_BUDGET = 40 * 1024 * 1024
MASKED = -1e30
ROW_INPUT_BUFFERS = 3
LOG2E = 1.4426950408889634
ATTN_TQ, ATTN_TK = 256, 256
ATTN_TQ_WIDE = 512
ATTN_TK_SOFTMAX = 512
MESH = pl.DeviceIdType.MESH

NT_DIMS = (((1,), (1,)), ((), ()))
TN_DIMS = (((0,), (0,)), ((), ()))
NN_DIMS = (((1,), (0,)), ((), ()))


def _params(*sem):
    return pltpu.CompilerParams(dimension_semantics=sem, vmem_limit_bytes=VMEM_LIMIT)


def _pick(n, pref):
    best = None
    for t in range(LANES, min(n, pref) + 1, LANES):
        if n % t == 0:
            best = t
    return n if best is None or 2 * best < min(n, pref) else best


def _rows_call(fn, name, row_ins, full_ins, row_outs, acc_outs=(), tile=256, hosted=()):
    row_ins = [r if isinstance(r, tuple) else (r, r.shape[1], 0) for r in row_ins]
    t_rows = row_ins[0][0].shape[0]
    assert t_rows % tile == 0
    n_row, n_in = len(row_ins), len(row_ins) + len(full_ins)
    n_row_out, n_acc, n_host = len(row_outs), len(acc_outs), len(hosted)
    n_steps = t_rows // tile
    slots = ROW_INPUT_BUFFERS
    out0 = n_in + n_host

    def body(*refs):
        bufs, sems = refs[len(refs) - 2 * n_row:len(refs) - n_row], refs[len(refs) - n_row:]
        refs = refs[:len(refs) - 2 * n_row]
        step = pl.program_id(0)
        outs_end = out0 + n_row_out + n_acc
        host_args = (hosted, refs[n_in:out0], refs[outs_end:outs_end + n_host], refs[outs_end + n_host:], step, n_steps)
        _run_hosted(*host_args, "start")

        def fetch(k, at, slot):
            _, w, cb = row_ins[k]
            return pltpu.make_async_copy(refs[k].at[pl.ds(at * tile, tile), pl.ds(cb * w, w)], bufs[k].at[slot],
                                         sems[k].at[slot])

        @pl.when(step == 0)
        def _():
            for ahead in range(min(slots - 1, n_steps)):
                for k in range(n_row):
                    fetch(k, ahead, ahead).start()

        @pl.when(step + (slots - 1) < n_steps)
        def _():
            for k in range(n_row):
                fetch(k, step + (slots - 1), (step + (slots - 1)) % slots).start()

        for k in range(n_row):
            fetch(k, step, step % slots).wait()
        res = fn(*[bufs[k][step % slots] for k in range(n_row)], *[r[...] for r in refs[n_row:n_in]])
        res = res if isinstance(res, tuple) else (res,)
        for ref, val in zip(refs[out0:out0 + n_row_out], res[:n_row_out]):
            ref[...] = val.astype(ref.dtype)
        acc_refs = refs[out0 + n_row_out:outs_end]
        if acc_refs:
            @pl.when(pl.program_id(0) == 0)
            def _():
                for ref in acc_refs:
                    ref[...] = jnp.zeros_like(ref)
            for ref, val in zip(acc_refs, res[n_row_out:]):
                ref[...] += val.astype(ref.dtype)
        _run_hosted(*host_args, "finish")

    in_specs = [pl.BlockSpec(memory_space=pl.ANY)] * n_row
    in_specs += [pl.BlockSpec(a.shape, lambda i: (0, 0)) for a in full_ins] + [ANY_SPEC] * n_host
    out_specs = [pl.BlockSpec((tile, c), lambda i: (i, 0)) for c, _ in row_outs]
    out_specs += [pl.BlockSpec(s, lambda i: (0, 0)) for s, _ in acc_outs] + [ANY_SPEC] * n_host
    out_shape = [jax.ShapeDtypeStruct((t_rows, c), d) for c, d in row_outs]
    out_shape += [jax.ShapeDtypeStruct(s, d) for s, d in acc_outs]
    out_shape += [_exchange_out_shape(kd, arr) for kd, arr in hosted]
    outs = pl.pallas_call(
        body, name=name, grid=(n_steps,), in_specs=in_specs, out_specs=out_specs, out_shape=out_shape,
        scratch_shapes=EXCHANGE_SCRATCH * n_host + [pltpu.VMEM((slots, tile, w), a.dtype) for a, w, _ in row_ins]
        + [pltpu.SemaphoreType.DMA((slots,))] * n_row,
        compiler_params=_params("arbitrary"),
    )(*[r[0] for r in row_ins], *full_ins, *[arr for _, arr in hosted])
    return outs[0] if len(outs) == 1 else tuple(outs)


def _matmul_tiles(m, n, k, in_bytes, out_bytes):
    tn = n if n <= 1280 else _pick(n, 1024)
    tks = [k] + [k // d for d in (2, 4, 8, 16) if k % (d * LANES) == 0]
    for tk in [t for t in tks if t <= 4096] or [tks[-1]]:
        for tm in (1024, 512, 256):
            if m % tm:
                continue
            acc = 2 * tm * tn * 4 if tk < k else tm * tn * 4
            if 2 * (tm * tk + tk * tn) * in_bytes + 2 * tm * tn * out_bytes + acc <= MATMUL_VMEM_BUDGET:
                return tm, tn, tk
    raise ValueError(f"no matmul tiling for {m}x{n}x{k}")


def _matmul(a, b, form, name, out_dtypes=(F32,), epilogue=None, extras=(), hosted=(), column_slots=False):
    if form == "nn":
        (m, k), n = a.shape, b.shape[1]
    elif form == "nt":
        (m, k), n = a.shape, b.shape[0]
    else:
        (k, m), n = a.shape, b.shape[1]
    in_bytes = max(a.dtype.itemsize, b.dtype.itemsize)
    out_bytes = sum(jnp.dtype(d).itemsize for d in out_dtypes) + sum(e.dtype.itemsize for e in extras)
    tm, tn, tk = _matmul_tiles(m, n, k, in_bytes, out_bytes)
    nk = k // tk
    dims = {"nn": NN_DIMS, "nt": NT_DIMS, "tn": TN_DIMS}[form]
    n_extra, n_out, n_host = len(extras), len(out_dtypes), len(hosted)
    grid = (m // tm, n // tn, nk)

    def body(*refs):
        a_ref, b_ref = refs[0], refs[1]
        extra_refs = refs[2:2 + n_extra]
        base = 2 + n_extra + n_host
        out_refs = refs[base:base + n_out]
        sems_at = base + n_out + n_host
        step = (pl.program_id(0) * grid[1] + pl.program_id(1)) * nk + pl.program_id(2)
        host_args = (hosted, refs[2 + n_extra:base], refs[base + n_out:sems_at], refs[sems_at:sems_at + 3 * n_host],
                     step, grid[0] * grid[1] * nk)
        _run_hosted(*host_args, "start")

        def finish(acc):
            vals = (acc,) if epilogue is None else epilogue(acc, *[r[...] for r in extra_refs])
            for ref, val in zip(out_refs, vals):
                ref[...] = val.astype(ref.dtype)

        part = lax.dot_general(a_ref[...].astype(BF16), b_ref[...].astype(BF16), dims, preferred_element_type=F32)
        if nk == 1:
            finish(part)
        else:
            acc_ref = refs[-1]
            kk = pl.program_id(2)

            @pl.when(kk == 0)
            def _():
                acc_ref[...] = part

            @pl.when(kk > 0)
            def _():
                acc_ref[...] += part

            @pl.when(kk == nk - 1)
            def _():
                finish(acc_ref[...])
        _run_hosted(*host_args, "finish")

    a_spec = pl.BlockSpec((tk, tm), lambda i, j, kk: (kk, i)) if form == "tn" else pl.BlockSpec((tm, tk), lambda i, j, kk: (i, kk))
    b_spec = pl.BlockSpec((tn, tk), lambda i, j, kk: (j, kk)) if form == "nt" else pl.BlockSpec((tk, tn), lambda i, j, kk: (kk, j))
    o_spec = pl.BlockSpec((tm, tn), lambda i, j, kk: (i, j))
    out_shapes = [jax.ShapeDtypeStruct((m, n), d) for d in out_dtypes]
    out_specs = [o_spec] * n_out
    if column_slots:
        assert not extras
        out_shapes = [jax.ShapeDtypeStruct((n // tn, m, tn), d) for d in out_dtypes]
        out_specs = [pl.BlockSpec((None, tm, tn), lambda i, j, kk: (j, i, 0))] * n_out
    outs = pl.pallas_call(
        body, name=name, grid=grid, in_specs=[a_spec, b_spec] + [o_spec] * n_extra + [ANY_SPEC] * n_host,
        out_specs=out_specs + [ANY_SPEC] * n_host,
        out_shape=out_shapes + [_exchange_out_shape(kd, arr) for kd, arr in hosted],
        scratch_shapes=EXCHANGE_SCRATCH * n_host + ([pltpu.VMEM((tm, tn), F32)] if nk > 1 else []),
        compiler_params=_params(*(("arbitrary",) * 3 if n_host else ("parallel", "parallel", "arbitrary"))),
    )(a, b, *extras, *[arr for _, arr in hosted])
    return outs[0] if n_out + n_host == 1 else tuple(outs)


def _add_residual(acc, res):
    return (acc + res,)


def _log_sigmoid_parts(z):
    return jnp.log1p(jnp.exp(-jnp.abs(z)))


def _log2_gates(z2):
    lg = jnp.log2(1.0 + jnp.exp2(-jnp.abs(z2)))
    log_beta = jnp.minimum(z2, 0.0) - lg
    return log_beta, log_beta - z2


def _rms_fwd(x, g, n=None):
    n = x.shape[-1] if n is None else n
    r = lax.rsqrt(jnp.sum(x * x, axis=-1, keepdims=True) / n + NORM_EPS)
    return x * r * g


def _rms_bwd(x, g, dout, n=None):
    n = x.shape[-1] if n is None else n
    r = lax.rsqrt(jnp.sum(x * x, axis=-1, keepdims=True) / n + NORM_EPS)
    y = x * r
    dg = jnp.sum(dout * y, axis=0, keepdims=True)
    dy = dout * g
    dx = r * (dy - y * (jnp.sum(dy * y, axis=-1, keepdims=True) / n))
    return dx, dg


def _swap_halves(r):
    lane = lax.broadcasted_iota(jnp.int32, r.shape, 1)
    return jnp.where(lane < ROPE_DIM // 2, pltpu.roll(r, LANES - ROPE_DIM // 2, 1), pltpu.roll(r, ROPE_DIM // 2, 1))


def _rope_fwd(r, cos_t, sin_s):
    return r * cos_t + _swap_halves(r) * sin_s


def _rope_bwd(dr, cos_t, sin_s):
    return dr * cos_t + _swap_halves(dr * sin_s)


def _split3(x):
    hi = x.astype(BF16)
    r1 = x - hi.astype(F32)
    mid = r1.astype(BF16)
    lo = (r1 - mid.astype(F32)).astype(BF16)
    return hi, mid, lo


def _mesh_position():
    x, y, c = lax.axis_index("x"), lax.axis_index("y"), lax.axis_index("c")
    return x, y, c, 4 * x + 2 * y + c


def _peer(x, y, c, k):
    bx, by, bc = (k >> 2) & 1, (k >> 1) & 1, k & 1
    px, py, pc = x ^ bx, y ^ by, c ^ bc
    return (px, py, pc), 4 * px + 2 * py + pc


def _gather_steps(x_ref, out_ref, send_sems, recv_sems, local_sem):
    x, y, c, me = _mesh_position()
    sibling = (x, y, 1 - c)
    chips = [(1 - x, y), (x, 1 - y), (1 - x, 1 - y)]

    def slot(px, py, pc):
        return out_ref.at[4 * px + 2 * py + pc]

    def copy(k, blk, to, src=None):
        return pltpu.make_async_remote_copy(
            src_ref=slot(*blk) if src is None else src, dst_ref=slot(*blk), send_sem=send_sems.at[k],
            recv_sem=recv_sems.at[k], device_id=to, device_id_type=MESH)

    mine = pltpu.make_async_copy(x_ref, out_ref.at[me], local_sem)
    first = [copy(0, (x, y, c), sibling, src=x_ref)]
    first += [copy(1 + j, (x, y, c), (*chip, c), src=x_ref) for j, chip in enumerate(chips)]
    passed = [copy(4 + j, (*chip, c), sibling) for j, chip in enumerate(chips)]

    def start():
        mine.start()
        for cp in first:
            cp.start()

    def forward():
        for j, chip in enumerate(chips):
            copy(1 + j, (*chip, c), (x, y, c)).wait_recv()
            passed[j].start()

    def finish():
        copy(0, (x, y, 1 - c), (x, y, c)).wait_recv()
        for j, chip in enumerate(chips):
            copy(4 + j, (*chip, 1 - c), (x, y, c)).wait_recv()
        for cp in first + passed:
            cp.wait_send()
        mine.wait()

    return start, forward, finish


def _all_to_all_steps(g_ref, out_ref, send_sems, recv_sems, local_sem):
    x, y, c, me = _mesh_position()
    mine = pltpu.make_async_copy(g_ref.at[me], out_ref.at[me], local_sem)
    copies = []
    for k in range(1, N_DEV):
        peer, peer_idx = _peer(x, y, c, k)
        copies.append(pltpu.make_async_remote_copy(
            src_ref=g_ref.at[peer_idx], dst_ref=out_ref.at[me], send_sem=send_sems.at[k - 1],
            recv_sem=recv_sems.at[k - 1], device_id=peer, device_id_type=MESH))

    def start():
        mine.start()
        for cp in copies:
            cp.start()

    def finish():
        for k in range(1, N_DEV):
            peer, peer_idx = _peer(x, y, c, k)
            pltpu.make_async_remote_copy(
                src_ref=g_ref.at[me], dst_ref=out_ref.at[peer_idx], send_sem=send_sems.at[k - 1],
                recv_sem=recv_sems.at[k - 1], device_id=peer, device_id_type=MESH).wait_recv()
        for cp in copies:
            cp.wait_send()
        mine.wait()

    return start, None, finish


EXCHANGE_STEPS = {"gather": _gather_steps, "all_to_all": _all_to_all_steps}
EXCHANGE_SCRATCH = [pltpu.SemaphoreType.DMA((7,)), pltpu.SemaphoreType.DMA((7,)), pltpu.SemaphoreType.DMA]
ANY_SPEC = pl.BlockSpec(memory_space=pl.ANY)


def _exchange_out_shape(kind, arr):
    return jax.ShapeDtypeStruct(((N_DEV,) + arr.shape) if kind == "gather" else arr.shape, arr.dtype)


def _exchange(kind, arrs, name):
    n = len(arrs)

    def body(*refs):
        steps = [EXCHANGE_STEPS[kind](refs[i], refs[n + i], *refs[2 * n + 3 * i:2 * n + 3 * i + 3]) for i in range(n)]
        for start, _, _ in steps:
            start()
        for _, forward, _ in steps:
            if forward is not None:
                forward()
        for _, _, finish in steps:
            finish()

    return pl.pallas_call(body, name=name, out_shape=[_exchange_out_shape(kind, a) for a in arrs],
                          in_specs=[ANY_SPEC] * n, out_specs=[ANY_SPEC] * n, scratch_shapes=EXCHANGE_SCRATCH * n)(*arrs)


def _run_hosted(hosted, src_refs, dst_refs, sem_refs, step, n_steps, when):
    for idx, (kind, _) in enumerate(hosted):
        start, forward, finish = EXCHANGE_STEPS[kind](src_refs[idx], dst_refs[idx], *sem_refs[3 * idx:3 * idx + 3])
        if when == "start":
            pl.when(step == 0)(start)
            if forward is not None:
                pl.when(step == (3 * n_steps) // 4)(forward)
        else:
            pl.when(step == n_steps - 1)(finish)


def _causal_iotas(qi, tq, tk):
    row = qi * tq + lax.broadcasted_iota(jnp.int32, (tq, tk), 0)
    col = lax.broadcasted_iota(jnp.int32, (tq, tk), 1)
    return row, col


def _suffix_matrix(tk, inclusive):
    j = lax.broadcasted_iota(jnp.int32, (2 * tk, tk), 0) % tk
    s = lax.broadcasted_iota(jnp.int32, (2 * tk, tk), 1)
    return jnp.where((j >= s) if inclusive else (j > s), 1.0, 0.0).astype(BF16)


def _suffix_sum(x, mat):
    hi = x.astype(BF16)
    lo = (x - hi.astype(F32)).astype(BF16)
    return lax.dot_general(jnp.concatenate([hi, lo], axis=1), mat, NN_DIMS, preferred_element_type=F32)


def _attn_specs(t_rows, tq, heads, dk, dv, q_off, k_off, v_off):
    q_spec = pl.BlockSpec((tq, dk), lambda h, i: (i, q_off + h))
    kt_spec = pl.BlockSpec((dk, t_rows), lambda h, i: (k_off + h, 0))
    v_spec = pl.BlockSpec((t_rows, dv), lambda h, i: (0, v_off + h))
    return q_spec, kt_spec, v_spec


def _split_weights(weights, fine):
    hi = weights.astype(BF16)
    return (hi, (weights - hi.astype(F32)).astype(BF16)) if fine else (hi,)


def _weighted_values(split, v):
    return sum(lax.dot_general(part, v, NN_DIMS, preferred_element_type=F32) for part in split)


def _attn_fwd(kind, q_arr, kt_arr, v_arr, heads, dk, dv, scale, name, q_off=0, k_off=0, v_off=0, fcol=None, frow=None,
              tq=ATTN_TQ, tk=ATTN_TK, fine=True, hosted=()):
    t_rows = q_arr.shape[0]
    tq, tk = min(tq, t_rows), min(tk, t_rows)
    nq = t_rows // tq
    stick = kind == "stick"
    decay = fcol is not None
    n_in = 5 if decay else 3
    n_out = 2 if stick else 3
    n_host = len(hosted)

    def body(*refs):
        q_ref, kt_ref, v_ref = refs[:3]
        fcol_ref, frow_ref = (refs[3], refs[4]) if decay else (None, None)
        base = n_in + n_host
        o_ref, fine_ref = refs[base], refs[base + 1]
        lse_ref = None if stick else refs[base + 2]
        host_args = (hosted, refs[n_in:base], refs[base + n_out:base + n_out + n_host], refs[base + n_out + n_host:],
                     pl.program_id(0) * nq + pl.program_id(1), heads * nq)
        _run_hosted(*host_args, "start")
        qi = pl.program_id(1)
        q = q_ref[...]
        row, col = _causal_iotas(qi, tq, tk)
        n_kb = ((qi + 1) * tq + tk - 1) // tk
        n_diag = max(1, tq // tk)
        zeros_o = jnp.zeros((tq, dv), F32)

        no_weights = (jnp.zeros((tq, tk), BF16),) * (2 if fine else 1)

        def raw_logits(kb):
            return lax.dot_general(q, kt_ref[:, pl.ds(pl.multiple_of(kb * tk, tk), tk)], NN_DIMS,
                                   preferred_element_type=F32)

        def values(kb):
            return v_ref[pl.ds(pl.multiple_of(kb * tk, tk), tk), :]

        if stick:
            mat = _suffix_matrix(tk, inclusive=False)

            def make_step(masked):
                def step(i, carry):
                    c, acc, raw, prev = carry
                    raw_next = raw_logits(jnp.maximum(n_kb - 2 - i, 0))
                    d_acc = _weighted_values(prev, values(jnp.minimum(n_kb - i, n_kb - 1)))
                    log_beta, lom = _log2_gates(raw * (scale * LOG2E))
                    if masked:
                        strict = (col + (n_kb - 1 - i) * tk) < row
                        lom = jnp.where(strict, lom, 0.0)
                    w = jnp.exp2(log_beta + (_suffix_sum(lom, mat) + c))
                    if masked:
                        w = jnp.where(strict, w, 0.0)
                    return c + jnp.sum(lom, axis=1, keepdims=True), acc + d_acc, raw_next, _split_weights(w, fine)
                return step

            carry = (jnp.zeros((tq, 1), F32), zeros_o, raw_logits(n_kb - 1), no_weights)
            for i in range(n_diag):
                carry = make_step(True)(i, carry)
            _, acc, _, last = lax.fori_loop(n_diag, n_kb, make_step(False), carry)
            acc = acc + _weighted_values(last, values(0))
            o_ref[...] = acc.astype(o_ref.dtype)
            fine_ref[...] = acc
        else:
            fc = fcol_ref[...] * LOG2E if decay else None

            def make_step(masked):
                def step(kb, carry):
                    m, l, acc, raw, prev = carry
                    raw_next = raw_logits(jnp.minimum(kb + 1, n_kb - 1))
                    d_acc = _weighted_values(prev, values(jnp.maximum(kb - 1, 0)))
                    ks = pl.multiple_of(kb * tk, tk)
                    s = raw * (scale * LOG2E)
                    if decay:
                        s = (s + fc) - frow_ref[:, pl.ds(ks, tk)] * LOG2E
                    if masked:
                        s = jnp.where((col + ks) <= row, s, MASKED)
                    m_new = jnp.maximum(m, jnp.max(s, axis=1, keepdims=True))
                    alpha = jnp.exp2(m - m_new)
                    p = jnp.exp2(s - m_new)
                    l = alpha * l + jnp.sum(p, axis=1, keepdims=True)
                    return m_new, l, alpha * (acc + d_acc), raw_next, _split_weights(p, fine)
                return step

            carry = lax.fori_loop(0, n_kb - n_diag, make_step(False),
                                  (jnp.full((tq, 1), MASKED, F32), jnp.zeros((tq, 1), F32), zeros_o, raw_logits(0),
                                   no_weights))
            for d in range(n_diag):
                carry = make_step(True)(n_kb - n_diag + d, carry)
            m, l, acc, _, last = carry
            out = (acc + _weighted_values(last, values(n_kb - 1))) * (1.0 / l)
            o_ref[...] = out.astype(o_ref.dtype)
            fine_ref[...] = out
            lse_ref[...] = (m + jnp.log2(l)) * (1.0 / LOG2E)
        _run_hosted(*host_args, "finish")

    q_spec, k_spec, v_spec = _attn_specs(t_rows, tq, heads, dk, dv, q_off, k_off, v_off)
    stat_spec = pl.BlockSpec((None, tq, 1), lambda h, i: (h, i, 0))
    ins, in_specs = [q_arr, kt_arr, v_arr], [q_spec, k_spec, v_spec]
    if decay:
        ins += [fcol, frow]
        in_specs += [stat_spec, pl.BlockSpec((None, 1, t_rows), lambda h, i: (h, 0, 0))]
    o_spec = pl.BlockSpec((tq, dv), lambda h, i: (i, h))
    out_specs = [o_spec, o_spec]
    out_shape = [jax.ShapeDtypeStruct((t_rows, heads * dv), BF16), jax.ShapeDtypeStruct((t_rows, heads * dv), F32)]
    if not stick:
        out_specs.append(stat_spec)
        out_shape.append(jax.ShapeDtypeStruct((heads, t_rows, 1), F32))
    return tuple(pl.pallas_call(
        body, name=name, grid=(heads, nq), in_specs=in_specs + [ANY_SPEC] * n_host,
        out_specs=out_specs + [ANY_SPEC] * n_host,
        out_shape=out_shape + [_exchange_out_shape(kd, arr) for kd, arr in hosted],
        scratch_shapes=EXCHANGE_SCRATCH * n_host,
        compiler_params=_params("arbitrary" if n_host else "parallel", "arbitrary"),
    )(*ins, *[arr for _, arr in hosted]))


def _attn_bwd(kind, q_arr, k_arr, kt_arr, vt_arr, o_arr, do_arr, heads, dk, dv, scale, name, q_off=0, k_off=0, kt_off=0,
              vt_off=0, do_off=0, lse=None, fcol=None, frow=None, tq=ATTN_TQ, tk=ATTN_TK, hosted=()):
    t_rows = q_arr.shape[0]
    tq, tk = min(tq, t_rows), min(tk, t_rows)
    nq = t_rows // tq
    stick = kind == "stick"
    decay = fcol is not None
    n_in = 6 + (0 if stick else 1) + (2 if decay else 0)
    n_out = 5 if decay else 3
    n_host = len(hosted)

    def body(*refs):
        q_ref, k_ref, kt_ref, vt_ref, o_ref, do_ref = refs[:6]
        lse_ref = None if stick else refs[6]
        fcol_ref, frow_ref = (refs[7], refs[8]) if decay else (None, None)
        base = n_in + n_host
        dq_ref, dk_ref, dv_ref = refs[base:base + 3]
        dfcol_ref, dfrow_ref = (refs[base + 3], refs[base + 4]) if decay else (None, None)
        host_args = (hosted, refs[n_in:base], refs[base + n_out:base + n_out + n_host], refs[base + n_out + n_host:],
                     pl.program_id(0) * nq + pl.program_id(1), heads * nq)
        _run_hosted(*host_args, "start")
        qi = pl.program_id(1)

        @pl.when(qi == 0)
        def _():
            dk_ref[...] = jnp.zeros_like(dk_ref)
            dv_ref[...] = jnp.zeros_like(dv_ref)
            if decay:
                dfrow_ref[...] = jnp.zeros_like(dfrow_ref)

        q = q_ref[...]
        do = do_ref[...]
        delta = jnp.sum(do.astype(F32) * o_ref[...], axis=1, keepdims=True)
        row, col = _causal_iotas(qi, tq, tk)
        n_kb = ((qi + 1) * tq + tk - 1) // tk
        n_diag = max(1, tq // tk)

        no_pair = (jnp.zeros((tq, tk), BF16), jnp.zeros((tq, tk), BF16))

        def accumulate(kb, pair):
            at = pl.ds(pl.multiple_of(kb * tk, tk), tk)
            dk_ref[at, :] += lax.dot_general(pair[0], q, TN_DIMS, preferred_element_type=F32)
            dv_ref[at, :] += lax.dot_general(pair[1], do, TN_DIMS, preferred_element_type=F32)
            return lax.dot_general(pair[0], k_ref[at, :], NN_DIMS, preferred_element_type=F32)

        def raw_logits(kb):
            return lax.dot_general(q, kt_ref[:, pl.ds(pl.multiple_of(kb * tk, tk), tk)], NN_DIMS,
                                   preferred_element_type=F32)

        def d_weights(kb):
            return lax.dot_general(do, vt_ref[:, pl.ds(pl.multiple_of(kb * tk, tk), tk)], NN_DIMS,
                                   preferred_element_type=F32)

        if stick:
            mat_ex = _suffix_matrix(tk, inclusive=False)
            mat_in = _suffix_matrix(tk, inclusive=True)

            def make_step(masked):
                def step(i, carry):
                    c, gs, dq, raw, prev = carry
                    raw_next = raw_logits(jnp.maximum(n_kb - 2 - i, 0))
                    dw = d_weights(n_kb - 1 - i)
                    dq = dq + accumulate(jnp.minimum(n_kb - i, n_kb - 1), prev)
                    log_beta, log_omb = _log2_gates(raw * (scale * LOG2E))
                    lom = log_omb
                    if masked:
                        strict = (col + (n_kb - 1 - i) * tk) < row
                        lom = jnp.where(strict, log_omb, 0.0)
                    w = jnp.exp2(log_beta + (_suffix_sum(lom, mat_ex) + c))
                    if masked:
                        w = jnp.where(strict, w, 0.0)
                    g = w * dw
                    g_before = delta - (gs + _suffix_sum(g, mat_in))
                    dz = g * jnp.exp2(log_omb) - g_before * jnp.exp2(log_beta)
                    if masked:
                        dz = jnp.where(strict, dz, 0.0)
                    return (c + jnp.sum(lom, axis=1, keepdims=True), gs + jnp.sum(g, axis=1, keepdims=True), dq,
                            raw_next, ((dz * scale).astype(BF16), w.astype(BF16)))
                return step

            zero = jnp.zeros((tq, 1), F32)
            carry = (zero, zero, jnp.zeros((tq, dk), F32), raw_logits(n_kb - 1), no_pair)
            for i in range(n_diag):
                carry = make_step(True)(i, carry)
            _, _, dq, _, last = lax.fori_loop(n_diag, n_kb, make_step(False), carry)
            dq = dq + accumulate(0, last)
        else:
            lse_v = lse_ref[...] * LOG2E
            fc = fcol_ref[...] * LOG2E if decay else None

            def make_step(masked):
                def step(kb, carry):
                    dq, row_sum, raw, prev = carry
                    raw_next = raw_logits(jnp.minimum(kb + 1, n_kb - 1))
                    dp = d_weights(kb)
                    dq = dq + accumulate(jnp.maximum(kb - 1, 0), prev)
                    ks = pl.multiple_of(kb * tk, tk)
                    s = raw * (scale * LOG2E)
                    if decay:
                        s = (s + fc) - frow_ref[:, pl.ds(ks, tk)] * LOG2E
                    p = jnp.exp2(s - lse_v)
                    if masked:
                        p = jnp.where((col + ks) <= row, p, 0.0)
                    ds = p * (dp - delta)
                    if decay:
                        dfrow_ref[:, pl.ds(ks, tk)] += jnp.sum(ds, axis=0, keepdims=True)
                        row_sum = row_sum + jnp.sum(ds, axis=1, keepdims=True)
                    return dq, row_sum, raw_next, ((ds * scale).astype(BF16), p.astype(BF16))
                return step

            carry = lax.fori_loop(0, n_kb - n_diag, make_step(False),
                                  (jnp.zeros((tq, dk), F32), jnp.zeros((tq, 1), F32), raw_logits(0), no_pair))
            for d in range(n_diag):
                carry = make_step(True)(n_kb - n_diag + d, carry)
            dq, row_sum, _, last = carry
            dq = dq + accumulate(n_kb - 1, last)
            if decay:
                dfcol_ref[...] = row_sum
        dq_ref[...] = dq
        _run_hosted(*host_args, "finish")

    q_spec, kt_spec, _ = _attn_specs(t_rows, tq, heads, dk, dv, q_off, kt_off, 0)
    stat_spec = pl.BlockSpec((None, tq, 1), lambda h, i: (h, i, 0))
    frow_spec = pl.BlockSpec((None, 1, t_rows), lambda h, i: (h, 0, 0))
    ins = [q_arr, k_arr, kt_arr, vt_arr, o_arr, do_arr]
    in_specs = [q_spec, pl.BlockSpec((t_rows, dk), lambda h, i: (0, k_off + h)), kt_spec,
                pl.BlockSpec((dv, t_rows), lambda h, i: (vt_off + h, 0)), pl.BlockSpec((tq, dv), lambda h, i: (i, h)),
                pl.BlockSpec((tq, dv), lambda h, i: (i, do_off + h))]
    if not stick:
        ins.append(lse)
        in_specs.append(stat_spec)
    if decay:
        ins += [fcol, frow]
        in_specs += [stat_spec, frow_spec]
    out_specs = [pl.BlockSpec((tq, dk), lambda h, i: (i, h)), pl.BlockSpec((t_rows, dk), lambda h, i: (0, h)),
                 pl.BlockSpec((t_rows, dv), lambda h, i: (0, h))]
    out_shape = [jax.ShapeDtypeStruct((t_rows, heads * dk), F32), jax.ShapeDtypeStruct((t_rows, heads * dk), F32),
                 jax.ShapeDtypeStruct((t_rows, heads * dv), F32)]
    if decay:
        out_specs += [stat_spec, frow_spec]
        out_shape += [jax.ShapeDtypeStruct((heads, t_rows, 1), F32), jax.ShapeDtypeStruct((heads, 1, t_rows), F32)]
    return pl.pallas_call(
        body, name=name, grid=(heads, nq), in_specs=in_specs + [ANY_SPEC] * n_host,
        out_specs=out_specs + [ANY_SPEC] * n_host,
        out_shape=out_shape + [_exchange_out_shape(kd, arr) for kd, arr in hosted],
        scratch_shapes=EXCHANGE_SCRATCH * n_host,
        compiler_params=_params("arbitrary" if n_host else "parallel", "arbitrary"),
    )(*ins, *[arr for _, arr in hosted])


def _prefix_matrix(reverse):
    j = lax.broadcasted_iota(jnp.int32, (LANES, LANES), 0)
    s = lax.broadcasted_iota(jnp.int32, (LANES, LANES), 1)
    return jnp.where((j >= s) if reverse else (j <= s), 1.0, 0.0).astype(BF16)


def _chunk_cumsum(x, mat):
    return sum(lax.dot_general(part, mat, NN_DIMS, preferred_element_type=F32) for part in _split3(x))


def _gate_fwd(logit_t, bias_col):
    heads, t_rows = logit_t.shape

    def body(x_ref, b_ref, out_ref):
        mat = _prefix_matrix(reverse=False)

        def step(ci, carry):
            cs = pl.multiple_of(ci * LANES, LANES)
            pre = x_ref[:, pl.ds(cs, LANES)] + b_ref[...]
            log_f = jnp.minimum(pre, 0.0) - _log_sigmoid_parts(pre)
            out_ref[:, pl.ds(cs, LANES)] = _chunk_cumsum(log_f, mat) + carry
            return carry + jnp.sum(log_f, axis=1, keepdims=True)

        lax.fori_loop(0, t_rows // LANES, step, jnp.zeros((heads, 1), F32))

    return pl.pallas_call(body, name="gate_fwd", out_shape=jax.ShapeDtypeStruct((heads, t_rows), F32),
                          compiler_params=pltpu.CompilerParams(vmem_limit_bytes=VMEM_LIMIT))(logit_t, bias_col)


def _gate_bwd(dcum_t, logit_t, bias_col):
    heads, t_rows = logit_t.shape
    n_chunks = t_rows // LANES

    def body(d_ref, x_ref, b_ref, dx_ref, db_ref):
        mat = _prefix_matrix(reverse=True)

        def step(i, carry):
            tail, db = carry
            cs = pl.multiple_of((n_chunks - 1 - i) * LANES, LANES)
            d = d_ref[:, pl.ds(cs, LANES)]
            d_log_f = _chunk_cumsum(d, mat) + tail
            pre = x_ref[:, pl.ds(cs, LANES)] + b_ref[...]
            e = jnp.exp(-jnp.abs(pre))
            d_pre = d_log_f * (jnp.where(pre >= 0.0, e, 1.0) / (1.0 + e))
            dx_ref[:, pl.ds(cs, LANES)] = d_pre
            return tail + jnp.sum(d, axis=1, keepdims=True), db + jnp.sum(d_pre, axis=1, keepdims=True)

        zero = jnp.zeros((heads, 1), F32)
        _, db = lax.fori_loop(0, n_chunks, step, (zero, zero))
        db_ref[...] = db

    return pl.pallas_call(body, name="gate_bwd",
                          out_shape=(jax.ShapeDtypeStruct((heads, t_rows), F32), jax.ShapeDtypeStruct((heads, 1), F32)),
                          compiler_params=pltpu.CompilerParams(vmem_limit_bytes=VMEM_LIMIT))(dcum_t, logit_t, bias_col)


def _norm_fwd(x, g, name):
    return _rows_call(lambda xv, gv: _rms_fwd(xv, gv), name, [x], [g], [(x.shape[1], BF16)])


def _norm_bwd(x, g, dh, dres, name):
    def fn(xv, dhv, dresv, gv):
        dx, dg = _rms_bwd(xv, gv, dhv)
        dx = dresv + dx
        return dx, dx, dg
    return _rows_call(fn, name, [x, dh, dres], [g], [(x.shape[1], F32), (x.shape[1], BF16)], [((1, x.shape[1]), F32)])


def _loss_fwd_bwd(y, target):
    d_model = y.shape[1]

    def fn(yv, tv):
        err = yv - tv
        dy = err * (1.0 / d_model)
        return dy, dy, jnp.sum(jnp.sum(err * err, axis=1, keepdims=True), axis=0, keepdims=True)
    return _rows_call(fn, "loss", [y, target], [], [(d_model, F32), (d_model, BF16)], [((1, 1), F32)])


def _heads_apply(fn, n_heads, width, *tiles):
    return [fn(*[t[:, h * width:(h + 1) * width] for t in tiles]) for h in range(n_heads)]


def _fox_norm_fwd(pb, gq, gk, heads):
    width = heads * HEAD_DIM

    def fn(qk, gqv, gkv):
        q = jnp.concatenate(_heads_apply(lambda t: _rms_fwd(t, gqv), heads, HEAD_DIM, qk[:, :width]), axis=1)
        k = jnp.concatenate(_heads_apply(lambda t: _rms_fwd(t, gkv), heads, HEAD_DIM, qk[:, width:]), axis=1)
        return q, k
    return _rows_call(fn, "fox_norm_fwd", [(pb, 2 * width, 0)], [gq, gk], [(width, BF16), (width, BF16)])


def _fox_norm_bwd(pb, gq, gk, dq, dk, heads):
    width = heads * HEAD_DIM

    def fn(qk, dqv, dkv, gqv, gkv):
        res_q = _heads_apply(lambda t, d: _rms_bwd(t, gqv, d), heads, HEAD_DIM, qk[:, :width], dqv)
        res_k = _heads_apply(lambda t, d: _rms_bwd(t, gkv, d), heads, HEAD_DIM, qk[:, width:], dkv)
        dqk = jnp.concatenate([r[0] for r in res_q] + [r[0] for r in res_k], axis=1)
        return dqk, sum(r[1] for r in res_q), sum(r[1] for r in res_k)
    return _rows_call(fn, "fox_norm_bwd", [(pb, 2 * width, 0), dq, dk], [gq, gk], [(2 * width, BF16)],
                      [((1, HEAD_DIM), F32), ((1, HEAD_DIM), F32)])


def _lora_norm_fwd(down, gq, gkv, rank):
    def fn(dv, gqv, gkvv):
        return _rms_fwd(dv[:, :rank], gqv), _rms_fwd(dv[:, rank:], gkvv)
    return _rows_call(fn, "lora_norm_fwd", [(down, 2 * rank, 0)], [gq, gkv], [(rank, BF16), (rank, BF16)])


def _lora_norm_bwd(down, gq, gkv, dcq, dckv, dkpe, rank):
    def fn(dv, dcqv, dckvv, dkpev, gqv, gkvv):
        dxq, dgq = _rms_bwd(dv[:, :rank], gqv, dcqv)
        dxkv, dgkv = _rms_bwd(dv[:, rank:], gkvv, dckvv)
        return jnp.concatenate([dxq, dxkv, dkpev], axis=1), dgq, dgkv
    return _rows_call(fn, "lora_norm_bwd", [(down, 2 * rank, 0), dcq, dckv, dkpe], [gq, gkv],
                      [(2 * rank + LANES, BF16)], [((1, rank), F32), ((1, rank), F32)])


def _rope_tables(pos_col, inv_freq, sin_sign):
    def fn(pos, invf, sign):
        ang = pos.astype(F32) * invf
        return jnp.cos(ang) * jnp.abs(sign), jnp.sin(ang) * sign
    return _rows_call(fn, "rope_tables", [pos_col], [inv_freq, sin_sign], [(LANES, F32), (LANES, F32)])


def _mla_prep_fwd(q_raw, kv, down, kpe_block, qg, kg, cos_t, sin_s):
    def fn(qv, kvv, kpe, cosv, sinv, qgv, kgv):
        qs, ks, vs = [], [], []
        for h in range(MLA_HEADS):
            qn = _rms_fwd(qv[:, h * MLA_PAD_DIM:(h + 1) * MLA_PAD_DIM], qgv, MLA_QK_DIM)
            qs += [qn[:, :HEAD_DIM], _rope_fwd(qn[:, HEAD_DIM:], cosv, sinv)]
            k_full = jnp.concatenate([kvv[:, h * MLA_PAD_DIM:h * MLA_PAD_DIM + HEAD_DIM], kpe], axis=1)
            kn = _rms_fwd(k_full, kgv, MLA_QK_DIM)
            ks += [kn[:, :HEAD_DIM], _rope_fwd(kn[:, HEAD_DIM:], cosv, sinv)]
            vs.append(kvv[:, h * MLA_PAD_DIM + HEAD_DIM:(h + 1) * MLA_PAD_DIM])
        return jnp.concatenate(qs, axis=1), jnp.concatenate(ks, axis=1), jnp.concatenate(vs, axis=1)
    wide = MLA_HEADS * MLA_PAD_DIM
    return _rows_call(fn, "mla_prep_fwd", [q_raw, kv, (down, LANES, kpe_block), cos_t, sin_s], [qg, kg],
                      [(wide, BF16), (wide, BF16), (MLA_HEADS * HEAD_DIM, BF16)], tile=128)


def _mla_prep_bwd(q_raw, kv, down, kpe_block, qg, kg, cos_t, sin_s, dq, dk, dv):
    def fn(qv, kvv, kpe, cosv, sinv, dqv, dkv, dvv, qgv, kgv):
        dqs, dkvs = [], []
        dkpe = jnp.zeros_like(kpe)
        dqg = jnp.zeros_like(qgv)
        dkg = jnp.zeros_like(kgv)
        for h in range(MLA_HEADS):
            lo, hi = h * MLA_PAD_DIM, (h + 1) * MLA_PAD_DIM
            dqn = jnp.concatenate([dqv[:, lo:lo + HEAD_DIM], _rope_bwd(dqv[:, lo + HEAD_DIM:hi], cosv, sinv)], axis=1)
            dqh, dg = _rms_bwd(qv[:, lo:hi], qgv, dqn, MLA_QK_DIM)
            dqs.append(dqh)
            dqg = dqg + dg
            k_full = jnp.concatenate([kvv[:, lo:lo + HEAD_DIM], kpe], axis=1)
            dkn = jnp.concatenate([dkv[:, lo:lo + HEAD_DIM], _rope_bwd(dkv[:, lo + HEAD_DIM:hi], cosv, sinv)], axis=1)
            dkh, dg = _rms_bwd(k_full, kgv, dkn, MLA_QK_DIM)
            dkg = dkg + dg
            dkpe = dkpe + dkh[:, HEAD_DIM:]
            dkvs += [dkh[:, :HEAD_DIM], dvv[:, h * HEAD_DIM:(h + 1) * HEAD_DIM]]
        return jnp.concatenate(dqs, axis=1), jnp.concatenate(dkvs, axis=1), dkpe, dqg, dkg
    wide = MLA_HEADS * MLA_PAD_DIM
    return _rows_call(fn, "mla_prep_bwd", [q_raw, kv, (down, LANES, kpe_block), cos_t, sin_s, dq, dk, dv], [qg, kg],
                      [(wide, BF16), (wide, BF16), (LANES, F32)], [((1, MLA_PAD_DIM), F32), ((1, MLA_PAD_DIM), F32)],
                      tile=128)


def _sqrelu_up(acc):
    return acc, jnp.square(jnp.maximum(acc, 0.0))


def _sqrelu_grad(acc, u):
    return (acc * (2.0 * jnp.maximum(u, 0.0)),)


def _mlp_fwd(x, g, w_up, w_down, tag):
    h = _norm_fwd(x, g, f"mlp_norm_fwd{tag}")
    u, a = _matmul(h, w_up, "nn", f"mlp_up{tag}", (F32, BF16), _sqrelu_up)
    return _matmul(a, w_down, "nn", f"mlp_down{tag}", (F32,), _add_residual, (x,)), (h, u, a)


def _mlp_bwd(x, g, w_up, w_down, saved, dy, dy16, tag):
    h, u, a = saved
    dw_down = _matmul(a, dy16, "tn", f"mlp_dwdown{tag}", (BF16,))
    du = _matmul(dy16, w_down, "nt", f"mlp_du{tag}", (BF16,), _sqrelu_grad, (u,))
    dw_up = _matmul(h, du, "tn", f"mlp_dwup{tag}", (BF16,), column_slots=True)
    dh = _matmul(du, w_up, "nt", f"mlp_dh{tag}")
    dx, dx16, dg = _norm_bwd(x, g, dh, dy, f"mlp_norm_bwd{tag}")
    return dx, dx16, dg, dw_up, dw_down


def _local_step(x, pos_col, target, w, dist=None):
    w = dict(w)
    hs = w["w_a"].shape[1] // (4 * HEAD_DIM)
    sb_w = hs * HEAD_DIM
    grads, received = {}, {}

    def gather_in(group):
        return [("gather", blk) for blk in dist["blocks"][group]] if dist else []

    def exchange_in(group):
        return [("all_to_all", slots) for slots in dist["slots_of"](group, grads)] if dist else []

    h0 = _norm_fwd(x, w["ln_mix0"], "mix0_norm_fwd")
    pa = _matmul(h0, w["w_a"], "nn", "in_proj_a", (BF16,))
    pb = _matmul(h0, w["w_b"], "nn", "in_proj_b")
    pat = pa[:, sb_w:].T
    o_sb, o_sb_fine, *got = _attn_fwd("stick", pa, pat, pa, hs, HEAD_DIM, HEAD_DIM, HEAD_DIM ** -0.5, "stick_fwd",
                                      q_off=0, k_off=0, v_off=2 * hs, tq=ATTN_TQ_WIDE,
                                      hosted=gather_in("with_stick_fwd"))
    if dist:
        w.update(dist["weights_of"]("with_stick_fwd", got))
    logit_t = pb[:, 2 * sb_w:2 * sb_w + hs].T
    bias_col = w["b_f"][0, :hs].reshape(hs, 1)
    f_cum = _gate_fwd(logit_t, bias_col)
    f_col, f_row = f_cum[:, :, None], f_cum[:, None, :]
    qf, kf = _fox_norm_fwd(pb, w["fox_q_g"], w["fox_k_g"], hs)
    kft = kf.T
    o_fx, o_fx_fine, lse_fx, *got = _attn_fwd("softmax", qf, kft, pa, hs, HEAD_DIM, HEAD_DIM, HEAD_DIM ** -0.5,
                                              "fox_fwd", v_off=3 * hs, fcol=f_col, frow=f_row, tq=ATTN_TQ_WIDE, tk=ATTN_TK_SOFTMAX,
                                              fine=False,
                                              hosted=gather_in("with_fox_fwd"))
    if dist:
        w.update(dist["weights_of"]("with_fox_fwd", got))
    o0 = jnp.concatenate([o_sb, o_fx], axis=1)
    x1 = _matmul(o0, w["w_o0"], "nn", "out_proj0", (F32,), _add_residual, (x,))
    x2, mlp0 = _mlp_fwd(x1, w["ln_mlp0"], w["w_up0"], w["w_dn0"], "0")

    rank = w["w_uq"].shape[0]
    h2 = _norm_fwd(x2, w["ln_mix1"], "mix1_norm_fwd")
    down = _matmul(h2, w["w_down"], "nn", "mla_down")
    cqn, ckvn = _lora_norm_fwd(down, w["q_a_g"], w["kv_a_g"], rank)
    q_raw = _matmul(cqn, w["w_uq"], "nn", "mla_uq")
    kv = _matmul(ckvn, w["w_ukv"], "nn", "mla_ukv")
    cos_t, sin_s = _rope_tables(pos_col, w["inv_freq"], w["sin_sign"])
    kpe_block = 2 * rank // LANES
    qm, km, vm = _mla_prep_fwd(q_raw, kv, down, kpe_block, w["mla_q_g"], w["mla_k_g"], cos_t, sin_s)
    kmt, vmt = km.T, vm.T
    o_m, o_m_fine, lse_m, *got = _attn_fwd("softmax", qm, kmt, vm, MLA_HEADS, MLA_PAD_DIM, HEAD_DIM,
                                           MLA_QK_DIM ** -0.5, "mla_fwd", tq=ATTN_TQ_WIDE, tk=ATTN_TK_SOFTMAX,
                                           fine=False, hosted=gather_in("mlp1"))
    if dist:
        w.update(dist["weights_of"]("mlp1", got))
    x3 = _matmul(o_m, w["w_o1"], "nn", "out_proj1", (F32,), _add_residual, (x2,))
    x4, mlp1 = _mlp_fwd(x3, w["ln_mlp1"], w["w_up1"], w["w_dn1"], "1")

    dy, dy16, sq_err = _loss_fwd_bwd(x4, target)

    dx3, dx3_16, grads["ln_mlp1"], grads["w_up1"], grads["w_dn1"] = _mlp_bwd(
        x3, w["ln_mlp1"], w["w_up1"], w["w_dn1"], mlp1, dy, dy16, "1")
    grads["w_o1"] = _matmul(o_m, dx3_16, "tn", "dw_o1", (BF16,))
    do_m = _matmul(dx3_16, w["w_o1"], "nt", "do_mla", (BF16,))
    dqm, dkm, dvm, *got = _attn_bwd("softmax", qm, km, kmt, vmt, o_m_fine, do_m, MLA_HEADS, MLA_PAD_DIM, HEAD_DIM,
                                    MLA_QK_DIM ** -0.5, "mla_bwd", lse=lse_m, tq=ATTN_TQ_WIDE, tk=ATTN_TK_SOFTMAX,
                                    hosted=exchange_in("mlp1"))
    received["mlp1"] = got
    dq_raw, dkv, dkpe, grads["mla_q_g"], grads["mla_k_g"] = _mla_prep_bwd(
        q_raw, kv, down, kpe_block, w["mla_q_g"], w["mla_k_g"], cos_t, sin_s, dqm, dkm, dvm)
    grads["w_uq"] = _matmul(cqn, dq_raw, "tn", "dw_uq", (BF16,))
    grads["w_ukv"] = _matmul(ckvn, dkv, "tn", "dw_ukv", (BF16,))
    dcqn = _matmul(dq_raw, w["w_uq"], "nt", "d_cq")
    dckvn = _matmul(dkv, w["w_ukv"], "nt", "d_ckv")
    ddown, grads["q_a_g"], grads["kv_a_g"] = _lora_norm_bwd(down, w["q_a_g"], w["kv_a_g"], dcqn, dckvn, dkpe, rank)
    grads["w_down"] = _matmul(h2, ddown, "tn", "dw_down", (BF16,))
    dh2 = _matmul(ddown, w["w_down"], "nt", "d_h2")
    dx2, dx2_16, grads["ln_mix1"] = _norm_bwd(x2, w["ln_mix1"], dh2, dx3, "mix1_norm_bwd")

    dx1, dx1_16, grads["ln_mlp0"], grads["w_up0"], grads["w_dn0"] = _mlp_bwd(
        x1, w["ln_mlp0"], w["w_up0"], w["w_dn0"], mlp0, dx2, dx2_16, "0")
    grads["w_o0"] = _matmul(o0, dx1_16, "tn", "dw_o0", (BF16,))
    do0 = _matmul(dx1_16, w["w_o0"], "nt", "do_mix0", (BF16,))
    dq_sb, dk_sb, dv_sb, *got = _attn_bwd("stick", pa, pa, pat, pat, o_sb_fine, do0, hs, HEAD_DIM, HEAD_DIM,
                                          HEAD_DIM ** -0.5, "stick_bwd", q_off=0, k_off=hs, kt_off=0, vt_off=hs, do_off=0,
                                          tq=ATTN_TQ_WIDE,
                                          hosted=exchange_in("with_stick_bwd"))
    received["with_stick_bwd"] = got
    dqf, dkf, dv_fx, ds_rows, ds_cols, *got = _attn_bwd(
        "softmax", qf, kf, kft, pat, o_fx_fine, do0, hs, HEAD_DIM, HEAD_DIM, HEAD_DIM ** -0.5, "fox_bwd", vt_off=2 * hs,
        tq=ATTN_TQ_WIDE, tk=ATTN_TK_SOFTMAX,
        do_off=hs, lse=lse_fx, fcol=f_col, frow=f_row, hosted=exchange_in("with_fox_bwd"))
    received["with_fox_bwd"] = got
    dqk_fx, grads["fox_q_g"], grads["fox_k_g"] = _fox_norm_bwd(pb, w["fox_q_g"], w["fox_k_g"], dqf, dkf, hs)
    dlogit_t, db_f = _gate_bwd(ds_rows[:, :, 0] - ds_cols[:, 0, :], logit_t, bias_col)
    grads["b_f"] = db_f.reshape(1, hs)
    dpa = jnp.concatenate([dq_sb.astype(BF16), dk_sb.astype(BF16), dv_sb.astype(BF16), dv_fx.astype(BF16)], axis=1)
    dlogit_pad = jnp.pad(dlogit_t.T.astype(BF16), ((0, 0), (0, pb.shape[1] - 2 * sb_w - hs)))
    dpb = jnp.concatenate([dqk_fx, dlogit_pad], axis=1)
    grads["w_a"] = _matmul(h0, dpa, "tn", "dw_a", (BF16,))
    grads["w_b"] = _matmul(h0, dpb, "tn", "dw_b", (BF16,))
    parts = exchange_in("mix0")
    res_b = _matmul(dpb, w["w_b"], "nt", "d_h0_b", hosted=parts[:1])
    dh0 = res_b[0] if dist else res_b
    res_a = _matmul(dpa, w["w_a"], "nt", "d_h0_a", (F32,), _add_residual, (dh0,), hosted=parts[1:])
    dh0, received["mix0"] = (res_a[0], [res_b[1], res_a[1]]) if dist else (res_a, [])
    grad_x, _, grads["ln_mix0"] = _norm_bwd(x, w["ln_mix0"], dh0, dx1, "mix0_norm_bwd")
    return sq_err, grad_x, grads, received


PIECES = {
    "sf_w_in": ("sf_w_in", 0, 1), "sf_w_o": ("sf_w_o", 0, 0), "mla_w_down": ("mla_w_down", 0, 0),
    "mla_w_uq": ("mla_w_uq", 0, 1), "mla_w_ukv": ("mla_w_ukv", 0, 1), "mla_w_o": ("mla_w_o", 0, 0),
    "mlp_w_up0": ("mlp_w_up", 0, 1), "mlp_w_up1": ("mlp_w_up", 1, 1),
    "mlp_w_down0": ("mlp_w_down", 0, 0), "mlp_w_down1": ("mlp_w_down", 1, 0),
}
GROUPS = {
    "mix0": ["sf_w_in"], "mla": ["mla_w_down", "mla_w_uq", "mla_w_ukv", "mla_w_o"], "mlp1": ["mlp_w_up1", "mlp_w_down1"],
}
GROUPS["with_stick_fwd"] = ["sf_w_o", "mlp_w_up0"]
GROUPS["with_fox_fwd"] = GROUPS["mla"] + ["mlp_w_down0"]
GROUPS["with_stick_bwd"] = GROUPS["mla"] + ["mlp_w_up0"]
GROUPS["with_fox_bwd"] = ["mlp_w_down0", "sf_w_o"]
SMALL = ["ln_mix_g", "ln_mlp_g", "sf_b_f", "fox_q_g", "fox_k_g", "mla_q_a_g", "mla_kv_a_g", "mla_q_g", "mla_k_g"]
ALL_W = ["ln_mix_g", "ln_mlp_g", "sf_w_in", "sf_b_f", "fox_q_g", "fox_k_g", "sf_w_o", "mla_w_down", "mla_q_a_g",
         "mla_kv_a_g", "mla_w_uq", "mla_w_ukv", "mla_q_g", "mla_k_g", "mla_w_o", "mlp_w_up", "mlp_w_down"]


def _weights_mix0(full, small):
    w_in = full["sf_w_in"]
    d_model = w_in.shape[0]
    n_fx = small["sf_b_f"].shape[1]
    sb_w = (w_in.shape[1] - n_fx) // 6
    cols = lambda i: w_in[:, i * sb_w:(i + 1) * sb_w]
    w_a = jnp.concatenate([cols(0), cols(1), cols(2), cols(5)], axis=1)
    w_b = jnp.concatenate([cols(3), cols(4), w_in[:, 6 * sb_w:], jnp.zeros((d_model, LANES - n_fx), w_in.dtype)], axis=1)
    half = ROPE_DIM // 2
    inv_freq = ROPE_THETA ** (-jnp.arange(half, dtype=F32) / half)
    zeros64 = jnp.zeros((ROPE_DIM,), F32)
    pad256 = lambda g: jnp.pad(g, ((0, 0), (0, MLA_PAD_DIM - MLA_QK_DIM)))
    pad_lanes = lambda g: jnp.pad(g, ((0, 0), (0, LANES - g.shape[1])))
    return dict(
        ln_mix0=small["ln_mix_g"][0:1], ln_mix1=small["ln_mix_g"][1:2],
        ln_mlp0=small["ln_mlp_g"][0:1], ln_mlp1=small["ln_mlp_g"][1:2],
        w_a=w_a, w_b=w_b, b_f=pad_lanes(small["sf_b_f"]), fox_q_g=small["fox_q_g"], fox_k_g=small["fox_k_g"],
        q_a_g=small["mla_q_a_g"], kv_a_g=small["mla_kv_a_g"],
        mla_q_g=pad256(small["mla_q_g"]), mla_k_g=pad256(small["mla_k_g"]),
        inv_freq=jnp.concatenate([inv_freq, inv_freq, zeros64]).reshape(1, LANES),
        sin_sign=jnp.concatenate([-jnp.ones((half,), F32), jnp.ones((half,), F32), zeros64]).reshape(1, LANES),
    )


def _late_weights(full):
    plain = {"sf_w_o": "w_o0", "mlp_w_up0": "w_up0", "mlp_w_down0": "w_dn0", "mla_w_ukv": "w_ukv", "mla_w_o": "w_o1",
             "mlp_w_up1": "w_up1", "mlp_w_down1": "w_dn1"}
    out = {key: full[p] for p, key in plain.items() if p in full}
    if "mla_w_down" in full:
        out["w_down"] = jnp.pad(full["mla_w_down"], ((0, 0), (0, LANES - ROPE_DIM)))
    if "mla_w_uq" in full:
        rank = full["mla_w_uq"].shape[0]
        w_uq = full["mla_w_uq"].reshape(rank, MLA_HEADS, MLA_QK_DIM)
        out["w_uq"] = jnp.pad(w_uq, ((0, 0), (0, 0), (0, MLA_PAD_DIM - MLA_QK_DIM))).reshape(rank, MLA_HEADS * MLA_PAD_DIM)
    return out


def _piece_slots(g, piece):
    if piece in ("mlp_w_up0", "mlp_w_up1"):
        return g[{"mlp_w_up0": "w_up0", "mlp_w_up1": "w_up1"}[piece]]
    return _shard_stack(_piece_grad(g, piece), PIECES[piece][2])


def _piece_grad(g, piece):
    if piece in ("mlp_w_up0", "mlp_w_up1"):
        return _unshard(_piece_slots(g, piece), PIECES[piece][2])
    if piece == "sf_w_in":
        n_fx = g["b_f"].shape[1]
        ga, gb = g["w_a"], g["w_b"]
        sb_w = ga.shape[1] // 4
        ca = lambda i: ga[:, i * sb_w:(i + 1) * sb_w]
        return jnp.concatenate([ca(0), ca(1), ca(2), gb[:, :sb_w], gb[:, sb_w:2 * sb_w], ca(3),
                                gb[:, 2 * sb_w:2 * sb_w + n_fx]], axis=1)
    if piece == "mla_w_uq":
        rank = g["w_uq"].shape[0]
        return g["w_uq"].reshape(rank, MLA_HEADS, MLA_PAD_DIM)[:, :, :MLA_QK_DIM].reshape(rank, MLA_HEADS * MLA_QK_DIM)
    if piece == "mla_w_down":
        return g["w_down"][:, :g["w_down"].shape[1] - (LANES - ROPE_DIM)]
    return g[{"sf_w_o": "w_o0", "mla_w_ukv": "w_ukv", "mla_w_o": "w_o1", "mlp_w_up0": "w_up0", "mlp_w_up1": "w_up1",
              "mlp_w_down0": "w_dn0", "mlp_w_down1": "w_dn1"}[piece]]


def _small_grads(g):
    return {
        "ln_mix_g": jnp.concatenate([g["ln_mix0"], g["ln_mix1"]], axis=0),
        "ln_mlp_g": jnp.concatenate([g["ln_mlp0"], g["ln_mlp1"]], axis=0),
        "sf_b_f": g["b_f"], "fox_q_g": g["fox_q_g"], "fox_k_g": g["fox_k_g"],
        "mla_q_a_g": g["q_a_g"], "mla_kv_a_g": g["kv_a_g"],
        "mla_q_g": g["mla_q_g"][:, :MLA_QK_DIM], "mla_k_g": g["mla_k_g"][:, :MLA_QK_DIM],
    }


PACK_TILE = 1024


def _as_rows(a, row_multiple=16):
    flat = a.reshape(-1)
    rows = -(-flat.shape[0] // LANES)
    rows = -(-rows // row_multiple) * row_multiple
    return jnp.pad(flat, (0, rows * LANES - flat.shape[0])).reshape(rows, LANES)


def _pack_rows(parts, axis, dtype, row_multiple=PACK_TILE, spare_rows=0):
    used = sum(p.shape[axis] for p in parts)
    shape = list(parts[0].shape)
    shape[axis] = -(-used // row_multiple) * row_multiple + spare_rows - used
    return jnp.concatenate([p.astype(dtype) for p in parts] + [jnp.ones(shape, dtype)], axis=axis)


def _unshard(stack, axis):
    moved = jnp.moveaxis(stack, 0, axis)
    shape = list(stack.shape[1:])
    shape[axis] *= stack.shape[0]
    return moved.reshape(shape)


def _shard_stack(full, axis):
    shape = list(full.shape)
    shape[axis:axis + 1] = [N_DEV, shape[axis] // N_DEV]
    return jnp.moveaxis(full.reshape(shape), axis, 0)


OPT_TILE_ELEMS = 128 * 1024


def _row_tile(rows, cols):
    best = 16
    for t in range(16, rows + 1, 16):
        if rows % t == 0 and t * cols <= OPT_TILE_ELEMS:
            best = t
    assert rows % best == 0
    return best


def _cast_bf16(a, name, hosted=()):
    return _rows_call(lambda v: v, name, [a], [], [(a.shape[1], BF16)], tile=_row_tile(*a.shape), hosted=hosted)


def _adam_math(w, g, m, v):
    m = ADAM_B1 * m + (1.0 - ADAM_B1) * g
    v = ADAM_B2 * v + (1.0 - ADAM_B2) * jnp.square(g)
    m_hat = m / (1.0 - ADAM_B1 ** ADAM_STEP)
    v_hat = v / (1.0 - ADAM_B2 ** ADAM_STEP)
    delta = -ADAM_LR * (m_hat / (jnp.sqrt(v_hat) + ADAM_EPS) + ADAM_WD * w)
    return delta, m, v


def _adam_big(recvs, w, m, v, name, hosted=()):
    layers, rows, cols = w.shape
    tile = _row_tile(rows, cols)
    n_tiles = rows // tile
    n_host = len(hosted)

    def body(*refs):
        recv_refs = refs[:layers]
        w_ref, m_ref, v_ref = refs[layers:layers + 3]
        base = layers + 3 + n_host
        g_ref, d_ref, nm_ref, nv_ref = refs[base:base + 4]
        layer = pl.program_id(0)
        host_args = (hosted, refs[layers + 3:base], refs[base + 4:base + 4 + n_host], refs[base + 4 + n_host:],
                     layer * n_tiles + pl.program_id(1), layers * n_tiles)
        _run_hosted(*host_args, "start")

        def total(r_ref):
            acc = r_ref[0].astype(F32)
            for s in range(1, N_DEV):
                acc = acc + r_ref[s].astype(F32)
            return acc

        g = total(recv_refs[0])
        for j in range(1, layers):
            g = jnp.where(layer == j, total(recv_refs[j]), g)
        delta, nm, nv = _adam_math(w_ref[...], g, m_ref[...], v_ref[...])
        g_ref[...] = g
        d_ref[...] = delta
        nm_ref[...] = nm
        nv_ref[...] = nv
        _run_hosted(*host_args, "finish")

    def recv_spec(j):
        return pl.BlockSpec((N_DEV, tile, cols),
                            lambda l, i: (0, jnp.where(l == j, i, jnp.where(l < j, 0, n_tiles - 1)), 0))

    spec = pl.BlockSpec((None, tile, cols), lambda l, i: (l, i, 0))
    out = jax.ShapeDtypeStruct(w.shape, F32)
    return pl.pallas_call(
        body, name=name, grid=(layers, n_tiles),
        in_specs=[recv_spec(j) for j in range(layers)] + [spec] * 3 + [ANY_SPEC] * n_host,
        out_specs=[spec] * 4 + [ANY_SPEC] * n_host,
        out_shape=[out] * 4 + [_exchange_out_shape(kd, arr) for kd, arr in hosted],
        scratch_shapes=EXCHANGE_SCRATCH * n_host, compiler_params=_params("arbitrary", "arbitrary"),
    )(*recvs, w, m, v, *[arr for _, arr in hosted])


def _sum_slots(gathered):
    rows = gathered.shape[1]

    def body(r_ref, o_ref):
        acc = r_ref[0]
        for s in range(1, N_DEV):
            acc = acc + r_ref[s]
        o_ref[...] = acc

    return pl.pallas_call(body, name="sum_small", out_shape=jax.ShapeDtypeStruct((rows, LANES), F32))(gathered)


def _adam_small(w, g, m, v):
    def fn(wv, gv, mv, vv):
        return _adam_math(wv, gv, mv, vv)
    return _rows_call(fn, "adam_small", [w, g, m, v], [], [(LANES, F32)] * 3, tile=w.shape[0])


def kernel(x, positions, ln_mix_g, ln_mlp_g, sf_w_in, sf_b_f, fox_q_g, fox_k_g, sf_w_o, mla_w_down, mla_q_a_g, mla_kv_a_g, mla_w_uq, mla_w_ukv, mla_q_g, mla_k_g, mla_w_o, mlp_w_up, mlp_w_down, loss_target, m_ln_mix_g, m_ln_mlp_g, m_sf_w_in, m_sf_b_f, m_fox_q_g, m_fox_k_g, m_sf_w_o, m_mla_w_down, m_mla_q_a_g, m_mla_kv_a_g, m_mla_w_uq, m_mla_w_ukv, m_mla_q_g, m_mla_k_g, m_mla_w_o, m_mlp_w_up, m_mlp_w_down, v_ln_mix_g, v_ln_mlp_g, v_sf_w_in, v_sf_b_f, v_fox_q_g, v_fox_k_g, v_sf_w_o, v_mla_w_down, v_mla_q_a_g, v_mla_kv_a_g, v_mla_w_uq, v_mla_w_ukv, v_mla_q_g, v_mla_k_g, v_mla_w_o, v_mlp_w_up, v_mlp_w_down):
    given = dict(locals())
    wts = {n: given[n] for n in ALL_W}
    mom = {n: given["m_" + n] for n in ALL_W}
    var = {n: given["v_" + n] for n in ALL_W}
    me = 4 * lax.axis_index("x") + 2 * lax.axis_index("y") + lax.axis_index("c")
    t_rows, d_model = x.shape[1], x.shape[2]
    big = sorted({name for name, _, _ in PIECES.values()})

    def whole_pieces(gathered, group):
        return {p: _unshard(s, PIECES[p][2]) for p, s in zip(GROUPS[group], gathered)}

    def w_in_slot_parts(grads):
        slots = _shard_stack(_piece_grad(grads, "sf_w_in"), PIECES["sf_w_in"][2])
        cuts = [0] + [(slots.shape[1] * f // 64) // 16 * 16 for f in (11, 32, 49)] + [slots.shape[1]]
        return [slots[:, a:b] for a, b in zip(cuts, cuts[1:])]

    def slots_of(group, grads):
        if group == "mix0":
            return w_in_slot_parts(grads)[:2]
        slots = [_piece_slots(grads, p) for p in GROUPS[group]]
        assert all(s.shape[0] == N_DEV for s in slots)
        return slots

    flat = lambda n: wts[n].reshape(-1, wts[n].shape[2])
    cast = {"sf_w_in": _cast_bf16(flat("sf_w_in"), "cast_sf_w_in").reshape(sf_w_in.shape)}
    cast_down, mix0_gathered = _cast_bf16(flat("mlp_w_down"), "cast_mlp_w_down", [("gather", cast["sf_w_in"][0])])
    cast["mlp_w_down"] = cast_down.reshape(mlp_w_down.shape)
    for n in big:
        if n not in cast:
            cast[n] = _cast_bf16(flat(n), f"cast_{n}").reshape(wts[n].shape)
    blocks = {grp: [cast[PIECES[p][0]][PIECES[p][1]] for p in GROUPS[grp]] for grp in ("with_stick_fwd", "with_fox_fwd", "mlp1")}
    mix0 = whole_pieces([mix0_gathered], "mix0")
    gains, = _exchange("gather", [_as_rows(jnp.concatenate([mla_q_a_g, mla_kv_a_g], axis=1))], "gather_gains")
    lora_n = mla_q_a_g.shape[1]
    gains_flat = gains.reshape(N_DEV, -1)[:, :2 * lora_n]
    small = dict(ln_mix_g=ln_mix_g, ln_mlp_g=ln_mlp_g, sf_b_f=sf_b_f, fox_q_g=fox_q_g, fox_k_g=fox_k_g,
                 mla_q_a_g=gains_flat[:, :lora_n].reshape(1, -1), mla_kv_a_g=gains_flat[:, lora_n:].reshape(1, -1),
                 mla_q_g=mla_q_g, mla_k_g=mla_k_g)
    dist = dict(blocks=blocks, slots_of=slots_of,
                weights_of=lambda grp, gathered: _late_weights(whole_pieces(gathered, grp)))
    sq_err, grad_x, g, received = _local_step(x[0], positions.reshape(t_rows, 1), loss_target[0],
                                              _weights_mix0(mix0, small), dist)

    recv_of = {p: r for grp in ("mlp1", "with_stick_bwd", "with_fox_bwd") for p, r in zip(GROUPS[grp], received[grp])}
    late_parts = dict(zip(("mlp_w_down", "mlp_w_up"), w_in_slot_parts(g)[2:]))
    w_in_recv = list(received["mix0"])
    results = {kind: {} for kind in ("grad", "delta", "new_m", "new_v")}
    for n in sorted(big, key=lambda name: (name == "sf_w_in", name not in late_parts)):
        if n == "sf_w_in":
            recv_of["sf_w_in"] = jnp.concatenate(w_in_recv, axis=1)
        layers = [p for _, p in sorted((layer, p) for p, (name, layer, _) in PIECES.items() if name == n)]
        hosted = [("all_to_all", late_parts[n])] if n in late_parts else []
        outs = _adam_big([recv_of[p] for p in layers], wts[n], mom[n], var[n], f"adam_{n}", hosted)
        w_in_recv += outs[4:]
        for kind, out in zip(("grad", "delta", "new_m", "new_v"), outs[:4]):
            results[kind][n] = out

    small_g = _small_grads(g)
    small_parts = [_as_rows(small_g[n], 8) for n in SMALL] + [_as_rows(sq_err, 8)]
    small_sum = _sum_slots(_exchange("gather", [_pack_rows(small_parts, 0, F32, 8, 8)], "gather_small_grads")[0])
    red, off = {}, 0
    for n, p in zip(SMALL + ["loss"], small_parts):
        red[n] = small_sum[off:off + p.shape[0]].reshape(-1)
        off += p.shape[0]
    loss = 0.5 * red["loss"][0] / d_model
    for n in SMALL:
        if n in ("mla_q_a_g", "mla_kv_a_g"):
            results["grad"][n] = lax.dynamic_slice(red[n], (me * lora_n,), (lora_n,)).reshape(wts[n].shape)
        else:
            results["grad"][n] = red[n][:wts[n].size].reshape(wts[n].shape)
    pack_small = lambda d: jnp.concatenate([_as_rows(d[n], 8) for n in SMALL], axis=0)
    small_out = _adam_small(pack_small(wts), pack_small(results["grad"]), pack_small(mom), pack_small(var))
    off = 0
    for n in SMALL:
        r = _as_rows(wts[n], 8).shape[0]
        for kind, packed in zip(["delta", "new_m", "new_v"], small_out):
            results[kind][n] = packed[off:off + r].reshape(-1)[:wts[n].size].reshape(wts[n].shape)
        off += r

    outs = [loss, grad_x[None]]
    for kind in ["grad", "delta", "new_m", "new_v"]:
        outs += [results[kind][n] for n in ALL_W]
    return tuple(outs)
```
